```python
import math
import jax, jax.numpy as jnp
from jax import lax
import numpy as np

D_MODEL = 1024
BATCH = 4
SEQ = 4096
DEPTH = 1

MEM_LEN = 256

MLA_HEADS = 8
QK_NOPE = 64
QK_ROPE = 32
V_HEAD = 64
Q_LORA = 256
KV_LORA = 256
ROPE_THETA = 10000.0
Q_BLOCK = 128

S5_GROUP_CH = 16
S5_WIDTH = 512
S5_GROUPS = S5_WIDTH // S5_GROUP_CH
S5_STATE = 64
DT_MIN = 1e-3
DT_MAX = 1e-1

XATTN_HEADS = 4
XATTN_HEAD_DIM = 128

N_EXPERTS = 64
TOP_K = 8
N_EXPERT_GROUPS = 8
TOPK_GROUPS = 4
EXPERT_FF = 256
SHARED_FF = 256
ROUTED_SCALE = 2.5
MOE_BLOCK = 128

LN_EPS = 1e-5
RMS_EPS = 1e-6

DEEPNORM_ALPHA = (2.0 * DEPTH) ** 0.25
DEEPNORM_BETA = (8.0 * DEPTH) ** -0.25

IN_SIZES = (Q_LORA, KV_LORA, QK_ROPE, S5_WIDTH, D_MODEL, D_MODEL)
IN_WIDTH = sum(IN_SIZES)
IN_OFFSETS = tuple(int(v) for v in np.cumsum(IN_SIZES)[:-1])

kernel_name = "hybrid_mla_s5_gated_deepnorm_moe"


def layer_norm(x, g, b):
    xf = x.astype(jnp.float32)
    mu = jnp.mean(xf, axis=-1, keepdims=True)
    var = jnp.mean(jnp.square(xf - mu), axis=-1, keepdims=True)
    y = (xf - mu) * lax.rsqrt(var + LN_EPS) * g.astype(jnp.float32) + b.astype(jnp.float32)
    return y.astype(x.dtype)


def rms_norm(x, g):
    xf = x.astype(jnp.float32)
    y = xf * lax.rsqrt(jnp.mean(jnp.square(xf), axis=-1, keepdims=True) + RMS_EPS)
    return (y * g.astype(jnp.float32)).astype(x.dtype)


def rope_cos_sin(positions):
    inv_freq = ROPE_THETA ** (-jnp.arange(0, QK_ROPE, 2, dtype=jnp.float32) / QK_ROPE)
    ang = positions.astype(jnp.float32)[..., None] * inv_freq
    return jnp.cos(ang), jnp.sin(ang)


def apply_rope(x, cos, sin):
    cos = cos.astype(x.dtype)
    sin = sin.astype(x.dtype)
    x1, x2 = jnp.split(x, 2, axis=-1)
    return jnp.concatenate([x1 * cos - x2 * sin, x2 * cos + x1 * sin], axis=-1)


def causal_block_attention(q_nope, q_rope, k_nope, k_rope, v):
    B, S, H, _ = q_nope.shape
    nb = S // Q_BLOCK
    scale = (QK_NOPE + QK_ROPE) ** -0.5
    qn_b = q_nope.reshape(B, nb, Q_BLOCK, H, QK_NOPE).transpose(1, 0, 2, 3, 4)
    qr_b = q_rope.reshape(B, nb, Q_BLOCK, H, QK_ROPE).transpose(1, 0, 2, 3, 4)
    key_pos = jnp.arange(S)

    def one_block(args):
        i, qn, qr = args
        s = (jnp.einsum('bqhd,bkhd->bhqk', qn, k_nope)
             + jnp.einsum('bqhr,bkr->bhqk', qr, k_rope)).astype(jnp.float32) * scale
        q_pos = i * Q_BLOCK + jnp.arange(Q_BLOCK)
        mask = key_pos[None, :] <= q_pos[:, None]
        s = jnp.where(mask[None, None], s, -jnp.inf)
        p = jax.nn.softmax(s, axis=-1).astype(v.dtype)
        return jnp.einsum('bhqk,bkhd->bqhd', p, v)

    out = lax.map(one_block, (jnp.arange(nb), qn_b, qr_b))
    return out.transpose(1, 0, 2, 3, 4).reshape(B, S, H, V_HEAD)


def mla_branch(h_q, h_kv, k_rope_raw, positions, q_norm_g, kv_norm_g, w_uq, w_ukv, w_mla_o):
    B, S, _ = h_q.shape
    q = (rms_norm(h_q, q_norm_g) @ w_uq).reshape(B, S, MLA_HEADS, QK_NOPE + QK_ROPE)
    q_nope, q_rope = q[..., :QK_NOPE], q[..., QK_NOPE:]
    kv = (rms_norm(h_kv, kv_norm_g) @ w_ukv).reshape(B, S, MLA_HEADS, QK_NOPE + V_HEAD)
    k_nope, v = kv[..., :QK_NOPE], kv[..., QK_NOPE:]
    cos, sin = rope_cos_sin(positions)
    q_rope = apply_rope(q_rope, cos[:, :, None, :], sin[:, :, None, :])
    k_rope = apply_rope(k_rope_raw, cos, sin)
    attn = causal_block_attention(q_nope, q_rope, k_nope, k_rope, v)
    return attn.reshape(B, S, MLA_HEADS * V_HEAD) @ w_mla_o


def _complex_affine_combine(c1, c2):
    a1r, a1i, b1r, b1i = c1
    a2r, a2i, b2r, b2i = c2
    ar = a2r * a1r - a2i * a1i
    ai = a2r * a1i + a2i * a1r
    br = a2r * b1r - a2i * b1i + b2r
    bi = a2r * b1i + a2i * b1r + b2i
    return (ar, ai, br, bi)


def s5_branch(u, a_re, a_im, log_dt, b_re, b_im, c_re, c_im, d, w_glu):
    B, S, _ = u.shape
    f32 = jnp.float32
    uf = u.astype(f32).reshape(B, S, S5_GROUPS, S5_GROUP_CH)
    dt = jnp.exp(log_dt.astype(f32))[:, None]
    ar, ai = a_re.astype(f32), a_im.astype(f32)
    mag = jnp.exp(ar * dt)
    abar_r, abar_i = mag * jnp.cos(ai * dt), mag * jnp.sin(ai * dt)
    den = ar * ar + ai * ai
    nr, ni = abar_r - 1.0, abar_i
    coef_r = ((nr * ar + ni * ai) / den)[..., None]
    coef_i = ((ni * ar - nr * ai) / den)[..., None]
    br, bi = b_re.astype(f32), b_im.astype(f32)
    bbar_r = coef_r * br - coef_i * bi
    bbar_i = coef_r * bi + coef_i * br
    bu_r = jnp.einsum('bsgh,gph->bsgp', uf, bbar_r)
    bu_i = jnp.einsum('bsgh,gph->bsgp', uf, bbar_i)
    a_r = jnp.broadcast_to(abar_r, bu_r.shape)
    a_i = jnp.broadcast_to(abar_i, bu_i.shape)
    _, _, xr, xi = lax.associative_scan(_complex_affine_combine, (a_r, a_i, bu_r, bu_i), axis=1)
    y = (jnp.einsum('bsgp,ghp->bsgh', xr, c_re.astype(f32))
         - jnp.einsum('bsgp,ghp->bsgh', xi, c_im.astype(f32)))
    y = y.reshape(B, S, S5_WIDTH) + d.astype(f32) * uf.reshape(B, S, S5_WIDTH)
    y = jax.nn.gelu(y).astype(u.dtype)
    val, gate = jnp.split(y @ w_glu, 2, axis=-1)
    return val * jax.nn.sigmoid(gate)


def hybrid_mixer(x, positions, w_in, q_norm_g, kv_norm_g, w_uq, w_ukv, w_mla_o,
                 s5_a_re, s5_a_im, s5_log_dt, s5_b_re, s5_b_im, s5_c_re, s5_c_im,
                 s5_d, w_s5_glu, w_out):
    proj = x @ w_in
    h_q, h_kv, k_rope_raw, u, g_mla, g_s5 = jnp.split(proj, IN_OFFSETS, axis=-1)
    y_mla = mla_branch(h_q, h_kv, k_rope_raw, positions, q_norm_g, kv_norm_g, w_uq, w_ukv, w_mla_o)
    y_s5 = s5_branch(u, s5_a_re, s5_a_im, s5_log_dt, s5_b_re, s5_b_im, s5_c_re, s5_c_im, s5_d, w_s5_glu)
    merged = jax.nn.sigmoid(g_mla) * y_mla + jax.nn.sigmoid(g_s5) * y_s5
    return merged @ w_out


def memory_cross_attention(x, mem, mem_ln_g, mem_ln_b, w_xq, w_xkv, w_xo):
    B, S, _ = x.shape
    M = mem.shape[1]
    m = layer_norm(mem, mem_ln_g, mem_ln_b)
    q = (x @ w_xq).reshape(B, S, XATTN_HEADS, XATTN_HEAD_DIM)
    kv = (m @ w_xkv).reshape(B, M, 2, XATTN_HEADS, XATTN_HEAD_DIM)
    k, v = kv[:, :, 0], kv[:, :, 1]
    s = jnp.einsum('bshd,bmhd->bhsm', q, k).astype(jnp.float32) * (XATTN_HEAD_DIM ** -0.5)
    p = jax.nn.softmax(s, axis=-1).astype(v.dtype)
    o = jnp.einsum('bhsm,bmhd->bshd', p, v).reshape(B, S, XATTN_HEADS * XATTN_HEAD_DIM)
    return o @ w_xo


def swiglu(x, w_gu, w_down):
    g, u = jnp.split(x @ w_gu, 2, axis=-1)
    return (jax.nn.silu(g) * u) @ w_down


def moe_ffn(x2d, w_router, router_bias, w_exp_gu, w_exp_down, w_sh_gu, w_sh_down):
    T, D = x2d.shape
    scores = jax.nn.sigmoid((x2d @ w_router).astype(jnp.float32))
    sel = scores + router_bias.astype(jnp.float32)
    per_group = N_EXPERTS // N_EXPERT_GROUPS
    grp_score = lax.top_k(sel.reshape(T, N_EXPERT_GROUPS, per_group), 2)[0].sum(-1)
    _, grp_idx = lax.top_k(grp_score, TOPK_GROUPS)
    grp_mask = jax.nn.one_hot(grp_idx, N_EXPERT_GROUPS, dtype=jnp.float32).sum(1) > 0
    sel = jnp.where(jnp.repeat(grp_mask, per_group, axis=1), sel, -jnp.inf)
    _, top_idx = lax.top_k(sel, TOP_K)
    w = jnp.take_along_axis(scores, top_idx, axis=1)
    w = w / jnp.sum(w, axis=-1, keepdims=True) * ROUTED_SCALE

    A = T * TOP_K
    flat_e = top_idx.reshape(A).astype(jnp.int32)
    flat_tok = jnp.repeat(jnp.arange(T, dtype=jnp.int32), TOP_K)
    flat_w = w.reshape(A).astype(x2d.dtype)
    order = jnp.argsort(flat_e)
    se = flat_e[order]
    counts = jnp.zeros((N_EXPERTS,), jnp.int32).at[flat_e].add(1)
    starts = jnp.cumsum(counts) - counts
    padded = (counts + MOE_BLOCK - 1) // MOE_BLOCK * MOE_BLOCK
    pends = jnp.cumsum(padded)
    pstarts = pends - padded
    dest = pstarts[se] + (jnp.arange(A, dtype=jnp.int32) - starts[se])
    n_blocks = (A + N_EXPERTS * (MOE_BLOCK - 1) + MOE_BLOCK - 1) // MOE_BLOCK
    P = n_blocks * MOE_BLOCK
    buf_tok = jnp.full((P,), T, jnp.int32).at[dest].set(flat_tok[order])
    buf_w = jnp.zeros((P,), x2d.dtype).at[dest].set(flat_w[order])
    block_start = jnp.arange(n_blocks, dtype=jnp.int32) * MOE_BLOCK
    block_e = jnp.minimum(jnp.searchsorted(pends, block_start, side='right'), N_EXPERTS - 1)
    x_pad = jnp.concatenate([x2d, jnp.zeros((1, D), x2d.dtype)], axis=0)

    def expert_block(args):
        tok, e = args
        return swiglu(x_pad[tok], w_exp_gu[e], w_exp_down[e])

    out = lax.map(expert_block, (buf_tok.reshape(n_blocks, MOE_BLOCK), block_e))
    out = out.reshape(P, D) * buf_w[:, None]
    routed = jnp.zeros((T + 1, D), x2d.dtype).at[buf_tok].add(out)[:T]
    return routed + swiglu(x2d, w_sh_gu, w_sh_down)


def setup_inputs(seed: int = 0) -> dict:
    key = jax.random.key(seed)
    ks = iter(jax.random.split(key, 48))
    L, D = DEPTH, D_MODEL
    f32 = jnp.float32

    def nrm(shape, scale):
        return jax.random.normal(next(ks), shape, f32) * scale

    def gain(shape):
        return 1.0 + 0.02 * jax.random.normal(next(ks), shape, f32)

    x = jax.random.normal(next(ks), (BATCH, SEQ, D), f32)
    mem = jax.random.normal(next(ks), (BATCH, MEM_LEN, D), f32)
    positions = (jnp.arange(SEQ, dtype=jnp.int32)[None, :]
                 + jax.random.randint(next(ks), (BATCH, 1), 0, 1024, jnp.int32))
    a_re = -0.5 + 0.01 * jax.random.normal(next(ks), (L, S5_GROUPS, S5_STATE), f32)
    a_im = (jnp.pi * jnp.arange(S5_STATE, dtype=f32))[None, None, :] \
        + 0.01 * jax.random.normal(next(ks), (L, S5_GROUPS, S5_STATE), f32)
    log_dt = jax.random.uniform(next(ks), (L, S5_GROUPS), f32, math.log(DT_MIN), math.log(DT_MAX))
    beta = DEEPNORM_BETA
    return {
        "x": x,
        "mem": mem,
        "positions": positions,
        "w_in": nrm((L, D, IN_WIDTH), D ** -0.5),
        "q_norm_g": gain((L, Q_LORA)),
        "kv_norm_g": gain((L, KV_LORA)),
        "w_uq": nrm((L, Q_LORA, MLA_HEADS * (QK_NOPE + QK_ROPE)), Q_LORA ** -0.5),
        "w_ukv": nrm((L, KV_LORA, MLA_HEADS * (QK_NOPE + V_HEAD)), KV_LORA ** -0.5),
        "w_mla_o": nrm((L, MLA_HEADS * V_HEAD, D), (MLA_HEADS * V_HEAD) ** -0.5),
        "s5_a_re": a_re,
        "s5_a_im": a_im,
        "s5_log_dt": log_dt,
        "s5_b_re": nrm((L, S5_GROUPS, S5_STATE, S5_GROUP_CH), (2 * S5_GROUP_CH) ** -0.5),
        "s5_b_im": nrm((L, S5_GROUPS, S5_STATE, S5_GROUP_CH), (2 * S5_GROUP_CH) ** -0.5),
        "s5_c_re": nrm((L, S5_GROUPS, S5_GROUP_CH, S5_STATE), (2 * S5_STATE) ** -0.5 * 4.0),
        "s5_c_im": nrm((L, S5_GROUPS, S5_GROUP_CH, S5_STATE), (2 * S5_STATE) ** -0.5 * 4.0),
        "s5_d": nrm((L, S5_WIDTH), 1.0),
        "w_s5_glu": nrm((L, S5_WIDTH, 2 * D), S5_WIDTH ** -0.5),
        "w_out": nrm((L, D, D), D ** -0.5 * beta),
        "ln1_g": gain((L, D)),
        "ln1_b": nrm((L, D), 0.02),
        "mem_ln_g": gain((L, D)),
        "mem_ln_b": nrm((L, D), 0.02),
        "w_xq": nrm((L, D, XATTN_HEADS * XATTN_HEAD_DIM), D ** -0.5),
        "w_xkv": nrm((L, D, 2 * XATTN_HEADS * XATTN_HEAD_DIM), D ** -0.5),
        "w_xo": nrm((L, XATTN_HEADS * XATTN_HEAD_DIM, D), (XATTN_HEADS * XATTN_HEAD_DIM) ** -0.5 * beta),
        "ln2_g": gain((L, D)),
        "ln2_b": nrm((L, D), 0.02),
        "w_router": nrm((L, D, N_EXPERTS), D ** -0.5),
        "router_bias": nrm((L, N_EXPERTS), 0.01),
        "w_exp_gu": nrm((L, N_EXPERTS, D, 2 * EXPERT_FF), D ** -0.5),
        "w_exp_down": nrm((L, N_EXPERTS, EXPERT_FF, D), EXPERT_FF ** -0.5 * beta),
        "w_sh_gu": nrm((L, D, 2 * SHARED_FF), D ** -0.5),
        "w_sh_down": nrm((L, SHARED_FF, D), SHARED_FF ** -0.5 * beta),
        "ln3_g": gain((L, D)),
        "ln3_b": nrm((L, D), 0.02),
    }


def reference(x, mem, positions, w_in, q_norm_g, kv_norm_g, w_uq, w_ukv, w_mla_o,
              s5_a_re, s5_a_im, s5_log_dt, s5_b_re, s5_b_im, s5_c_re, s5_c_im, s5_d,
              w_s5_glu, w_out, ln1_g, ln1_b, mem_ln_g, mem_ln_b, w_xq, w_xkv, w_xo,
              ln2_g, ln2_b, w_router, router_bias, w_exp_gu, w_exp_down, w_sh_gu,
              w_sh_down, ln3_g, ln3_b):
    B, S, D = x.shape
    for l in range(DEPTH):
        mix = hybrid_mixer(x, positions, w_in[l], q_norm_g[l], kv_norm_g[l], w_uq[l], w_ukv[l],
                           w_mla_o[l], s5_a_re[l], s5_a_im[l], s5_log_dt[l], s5_b_re[l],
                           s5_b_im[l], s5_c_re[l], s5_c_im[l], s5_d[l], w_s5_glu[l], w_out[l])
        x = layer_norm(DEEPNORM_ALPHA * x + mix, ln1_g[l], ln1_b[l])
        xa = memory_cross_attention(x, mem, mem_ln_g[l], mem_ln_b[l], w_xq[l], w_xkv[l], w_xo[l])
        x = layer_norm(DEEPNORM_ALPHA * x + xa, ln2_g[l], ln2_b[l])
        ff = moe_ffn(x.reshape(B * S, D), w_router[l], router_bias[l], w_exp_gu[l],
                     w_exp_down[l], w_sh_gu[l], w_sh_down[l]).reshape(B, S, D)
        x = layer_norm(DEEPNORM_ALPHA * x + ff, ln3_g[l], ln3_b[l])
    return x
```

```python
import functools
import math

import jax
import jax.numpy as jnp
from jax import lax
from jax.experimental import pallas as pl
from jax.experimental.pallas import tpu as pltpu

F32 = jnp.float32
BF16 = jnp.bfloat16
I32 = jnp.int32
U32 = jnp.uint32

D_MODEL = 1024
MLA_HEADS = 8
QK_NOPE = 64
QK_ROPE = 32
V_HEAD = 64
Q_LORA = 256
KV_LORA = 256
ROPE_THETA = 10000.0
S5_GROUP_CH = 16
S5_WIDTH = 512
S5_GROUPS = 32
S5_STATE = 64
XATTN_HEADS = 4
XATTN_HEAD_DIM = 128
N_EXPERTS = 64
TOP_K = 8
N_EXPERT_GROUPS = 8
TOPK_GROUPS = 4
EXPERT_FF = 256
SHARED_FF = 256
ROUTED_SCALE = 2.5
LN_EPS = 1e-5
RMS_EPS = 1e-6
DEPTH = 1
DEEPNORM_ALPHA = (2.0 * DEPTH) ** 0.25

LANES = 128
HEAD_PAD = 128
ROPE_LO = QK_NOPE
ROPE_HALF = QK_ROPE // 2

TM = 512
TQ = 512
S5_CHUNK = 16
TM_DISPATCH = 256
TM_EXPERT = 256
VMEM_LIMIT = 48 * 1024 * 1024


def _cparams(*sem):
    return pltpu.CompilerParams(dimension_semantics=sem, vmem_limit_bytes=VMEM_LIMIT)


def _dot(a, b):
    return jnp.dot(a, b, preferred_element_type=F32)


def _dot_nt(a, b, precision=None):
    return lax.dot_general(a, b, (((1,), (1,)), ((), ())), preferred_element_type=F32,
                           precision=precision)


def _dot_exact(a, b):
    return jnp.dot(a, b, preferred_element_type=F32, precision=lax.Precision.HIGHEST)


def _layer_norm(h, g, b):
    mu = jnp.mean(h, axis=-1, keepdims=True)
    c = h - mu
    var = jnp.mean(c * c, axis=-1, keepdims=True)
    return c * lax.rsqrt(var + LN_EPS) * g + b


def _rms_norm(h, g):
    return h * lax.rsqrt(jnp.mean(h * h, axis=-1, keepdims=True) + RMS_EPS) * g


def _full(shape):
    n = len(shape)
    return pl.BlockSpec(shape, lambda *_: (0,) * n)


def _rope_angle_kernel(pos_ref, invf_ref, cos_ref, sin_ref):
    ang = pos_ref[...].astype(F32) * invf_ref[...]
    cos_ref[...] = jnp.cos(ang)
    sin_ref[...] = jnp.sin(ang)


def _rope_tables(positions):
    T = positions.size
    rows = T * ROPE_HALF // LANES
    pos_rep = jnp.repeat(positions.reshape(T), ROPE_HALF).reshape(rows, LANES)
    inv_freq = ROPE_THETA ** (-jnp.arange(0, QK_ROPE, 2, dtype=F32) / QK_ROPE)
    invf = jnp.tile(inv_freq, LANES // ROPE_HALF).reshape(1, LANES)
    cos, sin = pl.pallas_call(
        _rope_angle_kernel,
        out_shape=(jax.ShapeDtypeStruct((rows, LANES), F32),) * 2,
        name="rope_angles",
    )(pos_rep, invf)
    cos = cos.reshape(T, ROPE_HALF)
    sin = sin.reshape(T, ROPE_HALF)
    one = jnp.ones((T, ROPE_LO), F32)
    zero_lo = jnp.zeros((T, ROPE_LO), F32)
    zero_h = jnp.zeros((T, ROPE_HALF), F32)
    tail1 = jnp.ones((T, HEAD_PAD - ROPE_LO - QK_ROPE), F32)
    tail0 = jnp.zeros((T, HEAD_PAD - ROPE_LO - QK_ROPE), F32)
    c_tab = jnp.concatenate([one, cos, cos, tail1], axis=1)
    s_up = jnp.concatenate([zero_lo, -sin, zero_h, tail0], axis=1)
    s_dn = jnp.concatenate([zero_lo, zero_h, sin, tail0], axis=1)
    return c_tab, s_up, s_dn


def _rope(x, c_tab, s_up, s_dn):
    return (x * c_tab + pltpu.roll(x, HEAD_PAD - ROPE_HALF, axis=1) * s_up
            + pltpu.roll(x, ROPE_HALF, axis=1) * s_dn)


def _inproj_kernel(x_ref, c_ref, su_ref, sd_ref, wlat_ref, wu_ref, wgm_ref, wgs_ref, qg_ref, kvg_ref,
                   wuq_ref, wuk_ref, wuv_ref, q_ref, k_ref, v_ref, u_ref, gm_ref, gs_ref):
    xb = x_ref[...].astype(BF16)
    lat = _dot(xb, wlat_ref[...])
    qn = _rms_norm(lat[:, :Q_LORA], qg_ref[...]).astype(BF16)
    kvn = _rms_norm(lat[:, Q_LORA:Q_LORA + KV_LORA], kvg_ref[...]).astype(BF16)
    c_tab, s_up, s_dn = c_ref[...], su_ref[...], sd_ref[...]
    k_rope = _rope(lat[:, Q_LORA + KV_LORA:], c_tab, s_up, s_dn)
    q = _dot(qn, wuq_ref[...])
    k = _dot(kvn, wuk_ref[...])
    scale = (QK_NOPE + QK_ROPE) ** -0.5
    for h in range(MLA_HEADS):
        sl = slice(h * HEAD_PAD, (h + 1) * HEAD_PAD)
        q_ref[:, sl] = (_rope(q[:, sl], c_tab, s_up, s_dn) * scale).astype(BF16)
        k_ref[:, sl] = (k[:, sl] + k_rope).astype(BF16)
    v_ref[...] = _dot(kvn, wuv_ref[...]).astype(BF16)
    u_ref[...] = _dot(xb, wu_ref[...]).astype(BF16)
    gm_ref[...] = _dot(xb, wgm_ref[...]).astype(BF16)
    gs_ref[...] = _dot(xb, wgs_ref[...]).astype(BF16)


def _pad_heads(w, head_w, lo_w):
    K = w.shape[0]
    w = w.reshape(K, MLA_HEADS, head_w)[:, :, :lo_w]
    w = jnp.pad(w, ((0, 0), (0, 0), (0, HEAD_PAD - lo_w)))
    return w.reshape(K, MLA_HEADS * HEAD_PAD)


def _inproj(x2d, tabs, w_in, q_norm_g, kv_norm_g, w_uq, w_ukv):
    T = x2d.shape[0]
    o_rope = Q_LORA + KV_LORA
    o_u = o_rope + QK_ROPE
    o_gm = o_u + S5_WIDTH
    o_gs = o_gm + D_MODEL
    w_rope = jnp.pad(w_in[:, o_rope:o_u], ((0, 0), (ROPE_LO, HEAD_PAD - ROPE_LO - QK_ROPE)))
    w_lat = jnp.concatenate([w_in[:, :o_rope], w_rope], axis=1).astype(BF16)
    w_u = w_in[:, o_u:o_gm].astype(BF16)
    w_gm = w_in[:, o_gm:o_gs].astype(BF16)
    w_gs = w_in[:, o_gs:].astype(BF16)
    wuq = _pad_heads(w_uq, QK_NOPE + QK_ROPE, QK_NOPE + QK_ROPE).astype(BF16)
    kv3 = w_ukv.reshape(KV_LORA, MLA_HEADS, QK_NOPE + V_HEAD)
    wuk = _pad_heads(kv3[:, :, :QK_NOPE].reshape(KV_LORA, -1), QK_NOPE, QK_NOPE).astype(BF16)
    wuv = _pad_heads(kv3[:, :, QK_NOPE:].reshape(KV_LORA, -1), V_HEAD, V_HEAD).astype(BF16)
    HP = MLA_HEADS * HEAD_PAD
    tile = lambda w: pl.BlockSpec((TM, w), lambda i: (i, 0))
    return pl.pallas_call(
        _inproj_kernel,
        grid=(T // TM,),
        in_specs=[tile(D_MODEL), tile(HEAD_PAD), tile(HEAD_PAD), tile(HEAD_PAD),
                  _full(w_lat.shape), _full(w_u.shape), _full(w_gm.shape), _full(w_gs.shape),
                  _full((1, Q_LORA)), _full((1, KV_LORA)),
                  _full(wuq.shape), _full(wuk.shape), _full(wuv.shape)],
        out_specs=[tile(HP), tile(HP), tile(HP), tile(S5_WIDTH), tile(D_MODEL), tile(D_MODEL)],
        out_shape=[jax.ShapeDtypeStruct((T, HP), BF16)] * 3
        + [jax.ShapeDtypeStruct((T, S5_WIDTH), BF16)]
        + [jax.ShapeDtypeStruct((T, D_MODEL), BF16)] * 2,
        compiler_params=_cparams("parallel"),
        name="inproj_mla_prep",
    )(x2d, *tabs, w_lat, w_u, w_gm, w_gs, q_norm_g.reshape(1, -1), kv_norm_g.reshape(1, -1),
      wuq, wuk, wuv)


def _attn_kernel(q_ref, k_ref, v_ref, o_ref):
    qi = pl.program_id(2)
    q = q_ref[0]

    def step(j, carry, masked):
        m, l, acc = carry
        start = pl.multiple_of(j * TQ, TQ)
        k = k_ref[0, pl.ds(start, TQ), :]
        v = v_ref[0, pl.ds(start, TQ), :]
        s = _dot_nt(q, k)
        if masked:
            row = lax.broadcasted_iota(I32, (TQ, TQ), 0)
            col = lax.broadcasted_iota(I32, (TQ, TQ), 1)
            s = jnp.where(col <= row, s, -jnp.inf)
        m_new = jnp.maximum(m, jnp.max(s, axis=-1, keepdims=True))
        p = jnp.exp(s - m_new)
        a = jnp.exp(m - m_new)
        l = a * l + jnp.sum(p, axis=-1, keepdims=True)
        acc = a * acc + _dot(p.astype(BF16), v)
        return m_new, l, acc

    init = (jnp.full((TQ, 1), -jnp.inf, F32), jnp.zeros((TQ, 1), F32), jnp.zeros((TQ, HEAD_PAD), F32))
    carry = lax.fori_loop(0, qi, lambda j, c: step(j, c, False), init)
    m, l, acc = step(qi, carry, True)
    o_ref[0] = (acc / l).astype(BF16)


def _attention(q, k, v, B, S):
    HP = MLA_HEADS * HEAD_PAD
    q, k, v = (a.reshape(B, S, HP) for a in (q, k, v))
    o = pl.pallas_call(
        _attn_kernel,
        grid=(B, MLA_HEADS, S // TQ),
        in_specs=[pl.BlockSpec((1, TQ, HEAD_PAD), lambda b, h, i: (b, i, h)),
                  pl.BlockSpec((1, S, HEAD_PAD), lambda b, h, i: (b, 0, h)),
                  pl.BlockSpec((1, S, HEAD_PAD), lambda b, h, i: (b, 0, h))],
        out_specs=pl.BlockSpec((1, TQ, HEAD_PAD), lambda b, h, i: (b, i, h)),
        out_shape=jax.ShapeDtypeStruct((B, S, HP), BF16),
        compiler_params=_cparams("parallel", "parallel", "arbitrary"),
        name="mla_flash_attention",
    )(q, k, v)
    return o.reshape(B * S, HP)


CH2 = S5_CHUNK * S5_GROUP_CH
P2 = 2 * S5_STATE
SCAN_LEVELS_MAX = 16


def _s5_prep_kernel(n_levels, arow_ref, acol_ref, bt_ref, ct_ref, d_ref,
                    mt_ref, wt_ref, vt_ref, apow_ref, dvec_ref, mt_acc):
    P, H, L = S5_STATE, S5_GROUP_CH, S5_CHUNK

    def powers(ar, ai, dt, n):
        e = jnp.exp(n * (ar * dt))
        ang = n * (ai * dt)
        return e * jnp.cos(ang), e * jnp.sin(ang)

    def zoh_coef(ar, ai, dt):
        pr, pi = powers(ar, ai, dt, 1.0)
        nr, ni = pr - 1.0, pi
        den = ar * ar + ai * ai
        return (nr * ar + ni * ai) / den, (ni * ar - nr * ai) / den

    row = arow_ref[0]
    ar, ai, dt = row[0:1], row[1:2], jnp.exp(row[2:3])
    cr, ci = zoh_coef(ar, ai, dt)
    bt = bt_ref[0]
    btr = jnp.concatenate([bt[:H], bt[:H]], axis=1)
    bti = jnp.concatenate([bt[H:], bt[H:]], axis=1)
    bbr = cr * btr - ci * bti
    bbi = cr * bti + ci * btr
    lane = lax.broadcasted_iota(I32, (1, P2), 1)
    is_re = lane < P
    ridx = lax.broadcasted_iota(I32, (CH2, P2), 0) // H
    pwr, pwi = powers(ar, ai, dt, (L - 1 - ridx).astype(F32))
    rsel = (lax.broadcasted_iota(I32, (CH2, H), 0) % H == lax.broadcasted_iota(I32, (CH2, H), 1)).astype(F32)
    bbr_t = _dot_exact(rsel, bbr)
    bbi_t = _dot_exact(rsel, bbi)
    w_re = pwr * bbr_t - pwi * bbi_t
    w_im = pwr * bbi_t + pwi * bbr_t
    wt_ref[0] = jnp.where(is_re, w_re, w_im).astype(BF16)
    a_r, a_i = powers(ar, ai, dt, float(L))
    sign = jnp.where(is_re, -1.0, 1.0)
    for lvl in range(SCAN_LEVELS_MAX):
        if lvl < n_levels:
            apow_ref[0, lvl:lvl + 1, :] = a_r
            apow_ref[0, SCAN_LEVELS_MAX + lvl:SCAN_LEVELS_MAX + lvl + 1, :] = sign * a_i
            a_r, a_i = a_r * a_r - a_i * a_i, 2.0 * a_r * a_i
        else:
            apow_ref[0, lvl:lvl + 1, :] = jnp.zeros_like(a_r)
            apow_ref[0, SCAN_LEVELS_MAX + lvl:SCAN_LEVELS_MAX + lvl + 1, :] = jnp.zeros_like(a_r)

    col = acol_ref[0]
    arc, aic, dtc = col[:, 0:1], col[:, 1:2], jnp.exp(col[:, 2:3])
    ct = ct_ref[0]
    csel = (lax.broadcasted_iota(I32, (H, CH2), 1) % H == lax.broadcasted_iota(I32, (H, CH2), 0)).astype(F32)
    ctr = _dot_exact(ct[:, :H], csel)
    cti = _dot_exact(ct[:, H:], csel)
    sidx = (lax.broadcasted_iota(I32, (P, CH2), 1) // H).astype(F32)
    pr, pi = powers(arc, aic, dtc, sidx)
    g_r = ctr * pr - cti * pi
    g_i = ctr * pi + cti * pr
    a1r, a1i = powers(arc, aic, dtc, 1.0)
    v_r = g_r * a1r - g_i * a1i
    v_i = g_r * a1i + g_i * a1r
    vt_ref[0, :P, :] = v_r.astype(BF16)
    vt_ref[0, P:, :] = (-v_i).astype(BF16)
    bbr64, bbi64 = bbr[:, :P], bbi[:, :P]
    kt = _dot_exact(bbr64, g_r) - _dot_exact(bbi64, g_i)
    lane2 = lax.broadcasted_iota(I32, (H, CH2), 1)
    for r in range(L):
        blk = kt if r == 0 else jnp.where(lane2 >= r * H, pltpu.roll(kt, r * H, axis=1), 0.0)
        mt_acc[r * H:(r + 1) * H, :] = blk
    mt_ref[0] = mt_acc[...].astype(BF16)
    dvec_ref[0] = _dot_exact(d_ref[0], csel)


def _s5_prep(n_levels, s5_a_re, s5_a_im, s5_log_dt, s5_b_re, s5_b_im, s5_c_re, s5_c_im, s5_d):
    G, P, H = S5_GROUPS, S5_STATE, S5_GROUP_CH
    ldt = jnp.broadcast_to(s5_log_dt[:, None], (G, P))
    arow = jnp.stack([s5_a_re, s5_a_im, ldt], axis=1)
    arow = jnp.concatenate([arow, arow], axis=2)
    acol = jnp.stack([s5_a_re, s5_a_im, ldt], axis=2)
    bt = jnp.concatenate([s5_b_re.transpose(0, 2, 1), s5_b_im.transpose(0, 2, 1)], axis=1)
    ct = jnp.concatenate([s5_c_re.transpose(0, 2, 1), s5_c_im.transpose(0, 2, 1)], axis=2)
    d = s5_d.reshape(G, 1, H)
    blk = lambda *s: pl.BlockSpec((1,) + s, lambda g: (g, 0, 0))
    return pl.pallas_call(
        functools.partial(_s5_prep_kernel, n_levels),
        grid=(G,),
        in_specs=[blk(3, P2), blk(P, 3), blk(2 * H, P), blk(P, 2 * H), blk(1, H)],
        out_specs=[blk(CH2, CH2), blk(CH2, P2), blk(P2, CH2), blk(2 * SCAN_LEVELS_MAX, P2), blk(1, CH2)],
        out_shape=[jax.ShapeDtypeStruct((G, CH2, CH2), BF16),
                   jax.ShapeDtypeStruct((G, CH2, P2), BF16),
                   jax.ShapeDtypeStruct((G, P2, CH2), BF16),
                   jax.ShapeDtypeStruct((G, 2 * SCAN_LEVELS_MAX, P2), F32),
                   jax.ShapeDtypeStruct((G, 1, CH2), F32)],
        scratch_shapes=[pltpu.VMEM((CH2, CH2), F32)],
        compiler_params=_cparams("parallel"),
        name="s5_discretise",
    )(arow, acol, bt, ct, d)


def _gelu_tanh(y):
    return 0.5 * y * (1.0 + jnp.tanh(math.sqrt(2.0 / math.pi) * (y + 0.044715 * (y * y * y))))


def _s5_scan_kernel(n_levels, batch, u_ref, mt_ref, wt_ref, vt_ref, apow_ref, dvec_ref, y_ref):
    u = u_ref[0]
    rows = u.shape[0]
    x = _dot(u, wt_ref[0])
    ridx = lax.broadcasted_iota(I32, (rows, P2), 0)
    apow = apow_ref[0]
    for lvl in range(n_levels):
        sh = batch * (1 << lvl)
        prev = jnp.where(ridx >= sh, pltpu.roll(x, sh, axis=0), 0.0)
        a_r = apow[lvl:lvl + 1]
        a_i = apow[SCAN_LEVELS_MAX + lvl:SCAN_LEVELS_MAX + lvl + 1]
        x = x + a_r * prev + a_i * pltpu.roll(prev, S5_STATE, axis=1)
    x_in = jnp.where(ridx >= batch, pltpu.roll(x, batch, axis=0), 0.0)
    y = _dot(u, mt_ref[0]) + _dot(x_in.astype(BF16), vt_ref[0]) + dvec_ref[0] * u.astype(F32)
    y_ref[0] = _gelu_tanh(y).astype(BF16)


def _s5_branch(u, B, S, s5_params):
    G, H, L = S5_GROUPS, S5_GROUP_CH, S5_CHUNK
    nc = S // L
    n_levels = max(1, (nc - 1).bit_length())
    assert n_levels <= SCAN_LEVELS_MAX
    mt, wt, vt, apow, dvec = _s5_prep(n_levels, *s5_params)
    rows = nc * B
    ug = u.reshape(B, nc, L, G, H).transpose(3, 1, 0, 2, 4).reshape(G, rows, CH2)
    blk = lambda *s: pl.BlockSpec((1,) + s, lambda g: (g, 0, 0))
    yg = pl.pallas_call(
        functools.partial(_s5_scan_kernel, n_levels, B),
        grid=(G,),
        in_specs=[blk(rows, CH2), blk(CH2, CH2), blk(CH2, P2), blk(P2, CH2),
                  blk(2 * SCAN_LEVELS_MAX, P2), blk(1, CH2)],
        out_specs=blk(rows, CH2),
        out_shape=jax.ShapeDtypeStruct((G, rows, CH2), BF16),
        compiler_params=_cparams("parallel"),
        name="s5_chunk_scan",
    )(ug, mt, wt, vt, apow, dvec)
    return yg.reshape(G, nc, B, L, H).transpose(2, 1, 3, 0, 4).reshape(B * S, S5_WIDTH)


def _merge_kernel(x_ref, o_ref, ys_ref, gm_ref, gs_ref, wmo_ref, wglu_ref, wout_ref, g_ref, b_ref, x1_ref):
    y_mla = _dot(o_ref[...], wmo_ref[...])
    z = _dot(ys_ref[...], wglu_ref[...])
    y_s5 = z[:, :D_MODEL] * jax.nn.sigmoid(z[:, D_MODEL:])
    merged = (jax.nn.sigmoid(gm_ref[...].astype(F32)) * y_mla
              + jax.nn.sigmoid(gs_ref[...].astype(F32)) * y_s5)
    mix = _dot(merged.astype(BF16), wout_ref[...])
    x1_ref[...] = _layer_norm(DEEPNORM_ALPHA * x_ref[...] + mix, g_ref[...], b_ref[...])


def _merge(x2d, o, ys, gm, gs, w_mla_o, w_s5_glu, w_out, ln_g, ln_b):
    T = x2d.shape[0]
    HP = MLA_HEADS * HEAD_PAD
    wmo = jnp.pad(w_mla_o.reshape(MLA_HEADS, V_HEAD, D_MODEL), ((0, 0), (0, HEAD_PAD - V_HEAD), (0, 0)))
    wmo = wmo.reshape(HP, D_MODEL).astype(BF16)
    tile = lambda w: pl.BlockSpec((TM, w), lambda i: (i, 0))
    return pl.pallas_call(
        _merge_kernel,
        grid=(T // TM,),
        in_specs=[tile(D_MODEL), tile(HP), tile(S5_WIDTH), tile(D_MODEL), tile(D_MODEL),
                  _full((HP, D_MODEL)), _full((S5_WIDTH, 2 * D_MODEL)), _full((D_MODEL, D_MODEL)),
                  _full((1, D_MODEL)), _full((1, D_MODEL))],
        out_specs=tile(D_MODEL),
        out_shape=jax.ShapeDtypeStruct((T, D_MODEL), F32),
        compiler_params=_cparams("parallel"),
        name="merge_outproj_ln1",
    )(x2d, o, ys, gm, gs, wmo, w_s5_glu.astype(BF16), w_out.astype(BF16),
      ln_g.reshape(1, -1), ln_b.reshape(1, -1))


def _memkv_kernel(mem_ref, g_ref, b_ref, w_ref, k_ref, v_ref):
    m = _layer_norm(mem_ref[0], g_ref[...], b_ref[...]).astype(BF16)
    kv = _dot(m, w_ref[...])
    hd = XATTN_HEADS * XATTN_HEAD_DIM
    k_ref[0] = kv[:, :hd].astype(BF16)
    v_ref[0] = kv[:, hd:].astype(BF16)


def _xattn_kernel(x1_ref, k_ref, v_ref, wq_ref, wo_ref, g_ref, b_ref, x2_ref, x2p_ref):
    x1 = x1_ref[0]
    q = (_dot(x1.astype(BF16), wq_ref[...]) * (XATTN_HEAD_DIM ** -0.5)).astype(BF16)
    k = k_ref[0]
    v = v_ref[0]
    outs = []
    for h in range(XATTN_HEADS):
        sl = slice(h * XATTN_HEAD_DIM, (h + 1) * XATTN_HEAD_DIM)
        s = _dot_nt(q[:, sl], k[:, sl])
        p = jnp.exp(s - jnp.max(s, axis=-1, keepdims=True))
        o = _dot(p.astype(BF16), v[:, sl]) / jnp.sum(p, axis=-1, keepdims=True)
        outs.append(o.astype(BF16))
    xa = _dot(jnp.concatenate(outs, axis=1), wo_ref[...])
    x2 = _layer_norm(DEEPNORM_ALPHA * x1 + xa, g_ref[...], b_ref[...])
    x2_ref[0] = x2
    half = D_MODEL // 2
    lo = pltpu.bitcast(x2[:, :half].astype(BF16).astype(F32), U32) >> 16
    hi = pltpu.bitcast(x2[:, half:].astype(BF16).astype(F32), U32)
    x2p_ref[0] = hi | lo


def _cross_attention(x1, mem, B, S, mem_ln_g, mem_ln_b, w_xq, w_xkv, w_xo, ln_g, ln_b):
    M = mem.shape[1]
    hd = XATTN_HEADS * XATTN_HEAD_DIM
    row = lambda a: a.reshape(1, -1)
    k, v = pl.pallas_call(
        _memkv_kernel,
        grid=(B,),
        in_specs=[pl.BlockSpec((1, M, D_MODEL), lambda b: (b, 0, 0)), _full((1, D_MODEL)), _full((1, D_MODEL)),
                  _full((D_MODEL, 2 * hd))],
        out_specs=[pl.BlockSpec((1, M, hd), lambda b: (b, 0, 0))] * 2,
        out_shape=[jax.ShapeDtypeStruct((B, M, hd), BF16)] * 2,
        compiler_params=_cparams("parallel"),
        name="memory_kv",
    )(mem, row(mem_ln_g), row(mem_ln_b), w_xkv.astype(BF16))
    tile = lambda w: pl.BlockSpec((1, TM, w), lambda b, i: (b, i, 0))
    x2, x2p = pl.pallas_call(
        _xattn_kernel,
        grid=(B, S // TM),
        in_specs=[tile(D_MODEL),
                  pl.BlockSpec((1, M, hd), lambda b, i: (b, 0, 0)), pl.BlockSpec((1, M, hd), lambda b, i: (b, 0, 0)),
                  _full((D_MODEL, hd)), _full((hd, D_MODEL)), _full((1, D_MODEL)), _full((1, D_MODEL))],
        out_specs=[tile(D_MODEL), tile(D_MODEL // 2)],
        out_shape=[jax.ShapeDtypeStruct((B, S, D_MODEL), F32), jax.ShapeDtypeStruct((B, S, D_MODEL // 2), U32)],
        compiler_params=_cparams("parallel", "parallel"),
        name="cross_attention_ln2",
    )(x1.reshape(B, S, D_MODEL), k, v, w_xq.astype(BF16), w_xo.astype(BF16), row(ln_g), row(ln_b))
    return x2.reshape(B * S, D_MODEL), x2p.reshape(B * S, D_MODEL // 2)


def _router_kernel(x_ref, w_ref, bias_ref, eidx_ref, rank_ref, wk_ref, cnt_ref, carry):
    E, per = N_EXPERTS, N_EXPERTS // N_EXPERT_GROUPS
    tm = x_ref.shape[0]

    @pl.when(pl.program_id(0) == 0)
    def _():
        carry[...] = jnp.zeros_like(carry)

    logits = _dot_nt(w_ref[...], x_ref[...], precision=lax.Precision.HIGHEST)
    scores = jax.nn.sigmoid(logits)
    sel = scores + bias_ref[...]
    neg = -jnp.inf
    i8 = lax.broadcasted_iota(I32, (per, tm), 0)
    gscore = []
    for g in range(N_EXPERT_GROUPS):
        blk = sel[g * per:(g + 1) * per]
        m1 = jnp.max(blk, axis=0, keepdims=True)
        i1 = jnp.min(jnp.where(blk == m1, i8, per), axis=0, keepdims=True)
        m2 = jnp.max(jnp.where(i8 == i1, neg, blk), axis=0, keepdims=True)
        gscore.append(m1 + m2)
    blocks = []
    for g in range(N_EXPERT_GROUPS):
        ahead = jnp.zeros((1, tm), I32)
        for o in range(N_EXPERT_GROUPS):
            if o == g:
                continue
            before = (gscore[o] >= gscore[g]) if o < g else (gscore[o] > gscore[g])
            ahead = ahead + before.astype(I32)
        blocks.append(jnp.where(ahead < TOPK_GROUPS, sel[g * per:(g + 1) * per], neg))
    cur = jnp.concatenate(blocks, axis=0)
    ie = lax.broadcasted_iota(I32, (E, tm), 0)
    picked = jnp.zeros((E, tm), F32)
    idxs = []
    for _ in range(TOP_K):
        m = jnp.max(cur, axis=0, keepdims=True)
        idx = jnp.min(jnp.where(cur == m, ie, E), axis=0, keepdims=True)
        hit = ie == idx
        picked = jnp.where(hit, 1.0, picked)
        cur = jnp.where(hit, neg, cur)
        idxs.append(idx)
    wsel = scores * picked
    wnorm = wsel / jnp.sum(wsel, axis=0, keepdims=True) * ROUTED_SCALE
    tri = (lax.broadcasted_iota(I32, (tm, tm), 0) <= lax.broadcasted_iota(I32, (tm, tm), 1)).astype(BF16)
    incl = _dot(picked.astype(BF16), tri)
    rank = carry[...] + incl - 1.0
    for kk in range(TOP_K):
        hit = ie == idxs[kk]
        eidx_ref[kk:kk + 1, :] = idxs[kk]
        rank_ref[kk:kk + 1, :] = jnp.sum(jnp.where(hit, rank, 0.0), axis=0, keepdims=True).astype(I32)
        wk_ref[kk:kk + 1, :] = jnp.sum(jnp.where(hit, wnorm, 0.0), axis=0, keepdims=True)
    total = carry[...] + jnp.sum(picked, axis=1, keepdims=True)
    carry[...] = total
    cnt_ref[...] = jnp.broadcast_to(total, cnt_ref.shape)


def _router(x2, w_router, router_bias):
    T = x2.shape[0]
    E = N_EXPERTS
    kt = pl.BlockSpec((TOP_K, TM), lambda i: (0, i))
    return pl.pallas_call(
        _router_kernel,
        grid=(T // TM,),
        in_specs=[pl.BlockSpec((TM, D_MODEL), lambda i: (i, 0)), _full((E, D_MODEL)), _full((E, 1))],
        out_specs=[kt, kt, kt, _full((E, LANES))],
        out_shape=[jax.ShapeDtypeStruct((TOP_K, T), I32), jax.ShapeDtypeStruct((TOP_K, T), I32),
                   jax.ShapeDtypeStruct((TOP_K, T), F32), jax.ShapeDtypeStruct((E, LANES), F32)],
        scratch_shapes=[pltpu.VMEM((E, 1), F32)],
        compiler_params=_cparams("arbitrary"),
        name="router_topk",
    )(x2, w_router.T, router_bias.reshape(E, 1))


def _dispatch_kernel(slots_hbm, x_ref, xs_in, xs_hbm, slots_smem, ssem, dsem):
    del xs_in
    i = pl.program_id(0)
    tm = x_ref.shape[0]
    cp = pltpu.make_async_copy(slots_hbm.at[i], slots_smem, ssem)
    cp.start()
    cp.wait()

    def row_copy(t, slot):
        return pltpu.make_async_copy(x_ref.at[pl.ds(t, 1), :], xs_hbm.at[pl.ds(slot, 1), :], dsem)

    def issue(t, c):
        for kk in range(TOP_K):
            row_copy(t, slots_smem[t * TOP_K + kk]).start()
        return c

    lax.fori_loop(0, tm, issue, 0)

    def drain(t, c):
        for kk in range(TOP_K):
            row_copy(0, 0).wait()
        return c

    lax.fori_loop(0, tm, drain, 0)


def _dispatch(slots_tiled, x2p, n_rows):
    T, W = x2p.shape
    nt = T // TM_DISPATCH
    xs0 = jnp.zeros((n_rows, W), U32)
    return pl.pallas_call(
        _dispatch_kernel,
        grid=(nt,),
        in_specs=[pl.BlockSpec(memory_space=pl.ANY),
                  pl.BlockSpec((TM_DISPATCH, W), lambda i: (i, 0)),
                  pl.BlockSpec(memory_space=pl.ANY)],
        out_specs=pl.BlockSpec(memory_space=pl.ANY),
        out_shape=jax.ShapeDtypeStruct((n_rows, W), U32),
        scratch_shapes=[pltpu.SMEM((TM_DISPATCH * TOP_K,), I32), pltpu.SemaphoreType.DMA(()),
                        pltpu.SemaphoreType.DMA(())],
        input_output_aliases={2: 0},
        compiler_params=_cparams("arbitrary"),
        name="moe_dispatch",
    )(slots_tiled, x2p, xs0)


def _expert_kernel(be_ref, bv_ref, xs_ref, wgu_ref, wdn_ref, y_ref):
    i = pl.program_id(0)
    half = D_MODEL // 2

    @pl.when(bv_ref[i] > 0)
    def _():
        w = xs_ref[...]
        lo = pltpu.bitcast(w << 16, F32).astype(BF16)
        hi = pltpu.bitcast(w & jnp.uint32(0xFFFF0000), F32).astype(BF16)
        h = _dot(lo, wgu_ref[0, :half, :]) + _dot(hi, wgu_ref[0, half:, :])
        g, u = h[:, :EXPERT_FF], h[:, EXPERT_FF:]
        act = (g * jax.nn.sigmoid(g) * u).astype(BF16)
        y_ref[...] = _dot(act, wdn_ref[0])

    @pl.when(bv_ref[i] == 0)
    def _():
        y_ref[...] = jnp.zeros_like(y_ref)


def _experts(block_e, block_valid, xs, w_exp_gu, w_exp_down):
    n_rows, W = xs.shape
    nb = n_rows // TM_EXPERT
    grid_spec = pltpu.PrefetchScalarGridSpec(
        num_scalar_prefetch=2,
        grid=(nb,),
        in_specs=[pl.BlockSpec((TM_EXPERT, W), lambda i, be, bv: (i, 0)),
                  pl.BlockSpec((1, D_MODEL, 2 * EXPERT_FF), lambda i, be, bv: (be[i], 0, 0)),
                  pl.BlockSpec((1, EXPERT_FF, D_MODEL), lambda i, be, bv: (be[i], 0, 0))],
        out_specs=pl.BlockSpec((TM_EXPERT, D_MODEL), lambda i, be, bv: (i, 0)),
    )
    return pl.pallas_call(
        _expert_kernel,
        grid_spec=grid_spec,
        out_shape=jax.ShapeDtypeStruct((n_rows, D_MODEL), F32),
        compiler_params=_cparams("arbitrary"),
        name="moe_grouped_swiglu",
    )(block_e, block_valid, xs, w_exp_gu.astype(BF16), w_exp_down.astype(BF16))


def _combine_kernel(slots_hbm, y_hbm, x2_ref, wk_ref, wsg_ref, wsd_ref, g_ref, b_ref, out_ref,
                    slots_smem, buf, ssem, dsem):
    i = pl.program_id(0)
    tm = x2_ref.shape[0]
    cp = pltpu.make_async_copy(slots_hbm.at[i], slots_smem, ssem)
    cp.start()
    cp.wait()

    def row_copy(t, kk, slot):
        return pltpu.make_async_copy(y_hbm.at[pl.ds(slot, 1), :], buf.at[kk, pl.ds(t, 1), :], dsem)

    def issue(t, c):
        for kk in range(TOP_K):
            row_copy(t, kk, slots_smem[t * TOP_K + kk]).start()
        return c

    lax.fori_loop(0, tm, issue, 0)

    x2 = x2_ref[...]
    h = _dot(x2.astype(BF16), wsg_ref[...])
    g, u = h[:, :SHARED_FF], h[:, SHARED_FF:]
    ff = _dot((g * jax.nn.sigmoid(g) * u).astype(BF16), wsd_ref[...])

    def drain(t, c):
        for kk in range(TOP_K):
            row_copy(0, kk, 0).wait()
        return c

    lax.fori_loop(0, tm, drain, 0)
    wk = wk_ref[...]
    for kk in range(TOP_K):
        ff = ff + wk[:, kk:kk + 1] * buf[kk]
    out_ref[...] = _layer_norm(DEEPNORM_ALPHA * x2 + ff, g_ref[...], b_ref[...])


def _combine(slots_tiled, y, x2, wk_t, w_sh_gu, w_sh_down, ln_g, ln_b):
    T = x2.shape[0]
    nt = T // TM_DISPATCH
    tile = lambda w: pl.BlockSpec((TM_DISPATCH, w), lambda i: (i, 0))
    return pl.pallas_call(
        _combine_kernel,
        grid=(nt,),
        in_specs=[pl.BlockSpec(memory_space=pl.ANY), pl.BlockSpec(memory_space=pl.ANY),
                  tile(D_MODEL), tile(TOP_K),
                  _full((D_MODEL, 2 * SHARED_FF)), _full((SHARED_FF, D_MODEL)),
                  _full((1, D_MODEL)), _full((1, D_MODEL))],
        out_specs=tile(D_MODEL),
        out_shape=jax.ShapeDtypeStruct((T, D_MODEL), F32),
        scratch_shapes=[pltpu.SMEM((TM_DISPATCH * TOP_K,), I32),
                        pltpu.VMEM((TOP_K, TM_DISPATCH, D_MODEL), F32),
                        pltpu.SemaphoreType.DMA(()), pltpu.SemaphoreType.DMA(())],
        compiler_params=_cparams("arbitrary"),
        name="moe_combine_shared_ln3",
    )(slots_tiled, y, x2, wk_t, w_sh_gu.astype(BF16), w_sh_down.astype(BF16),
      ln_g.reshape(1, -1), ln_b.reshape(1, -1))


def _moe(x2, x2p, w_router, router_bias, w_exp_gu, w_exp_down, w_sh_gu, w_sh_down, ln_g, ln_b):
    T = x2.shape[0]
    E = N_EXPERTS
    eidx, rank, wk, cnt = _router(x2, w_router, router_bias)
    counts = cnt[:, 0].astype(I32)
    padded = (counts + TM_EXPERT - 1) // TM_EXPERT * TM_EXPERT
    pends = jnp.cumsum(padded)
    pstarts = pends - padded
    nb = (T * TOP_K + E * (TM_EXPERT - 1)) // TM_EXPERT
    block_start = jnp.arange(nb, dtype=I32) * TM_EXPERT
    block_e = jnp.minimum(jnp.searchsorted(pends, block_start, side="right"), E - 1).astype(I32)
    block_valid = (block_start < pends[-1]).astype(I32)
    slots = pstarts[eidx] + rank
    nt = T // TM_DISPATCH
    slots_tiled = slots.reshape(TOP_K, nt, TM_DISPATCH).transpose(1, 2, 0).reshape(nt, TM_DISPATCH * TOP_K)
    xs = _dispatch(slots_tiled, x2p, nb * TM_EXPERT)
    y = _experts(block_e, block_valid, xs, w_exp_gu, w_exp_down)
    return _combine(slots_tiled, y, x2, wk.T, w_sh_gu, w_sh_down, ln_g, ln_b)


def kernel(x, mem, positions, w_in, q_norm_g, kv_norm_g, w_uq, w_ukv, w_mla_o, s5_a_re, s5_a_im, s5_log_dt, s5_b_re, s5_b_im, s5_c_re, s5_c_im, s5_d, w_s5_glu, w_out, ln1_g, ln1_b, mem_ln_g, mem_ln_b, w_xq, w_xkv, w_xo, ln2_g, ln2_b, w_router, router_bias, w_exp_gu, w_exp_down, w_sh_gu, w_sh_down, ln3_g, ln3_b):
    B, S, D = x.shape
    assert D == D_MODEL and S % TQ == 0 and (B * S) % TM == 0 and S % S5_CHUNK == 0
    xc = x.reshape(B * S, D)
    for l in range(w_in.shape[0]):
        tabs = _rope_tables(positions)
        q, k, v, u, gm, gs = _inproj(xc, tabs, w_in[l], q_norm_g[l], kv_norm_g[l], w_uq[l], w_ukv[l])
        o = _attention(q, k, v, B, S)
        ys = _s5_branch(u, B, S, (s5_a_re[l], s5_a_im[l], s5_log_dt[l], s5_b_re[l], s5_b_im[l],
                                  s5_c_re[l], s5_c_im[l], s5_d[l]))
        x1 = _merge(xc, o, ys, gm, gs, w_mla_o[l], w_s5_glu[l], w_out[l], ln1_g[l], ln1_b[l])
        x2, x2p = _cross_attention(x1, mem, B, S, mem_ln_g[l], mem_ln_b[l], w_xq[l], w_xkv[l], w_xo[l],
                                   ln2_g[l], ln2_b[l])
        xc = _moe(x2, x2p, w_router[l], router_bias[l], w_exp_gu[l], w_exp_down[l], w_sh_gu[l],
                  w_sh_down[l], ln3_g[l], ln3_b[l])
    return xc.reshape(B, S, D)
```

```python
import functools
import math

import jax
import jax.numpy as jnp
from jax import lax
from jax.experimental import pallas as pl
from jax.experimental.pallas import tpu as pltpu

F32 = jnp.float32
BF16 = jnp.bfloat16
I32 = jnp.int32
U32 = jnp.uint32

D_MODEL = 1024
MLA_HEADS = 8
QK_NOPE = 64
QK_ROPE = 32
V_HEAD = 64
Q_LORA = 256
KV_LORA = 256
ROPE_THETA = 10000.0
S5_GROUP_CH = 16
S5_WIDTH = 512
S5_GROUPS = 32
S5_STATE = 64
XATTN_HEADS = 4
XATTN_HEAD_DIM = 128
N_EXPERTS = 64
TOP_K = 8
N_EXPERT_GROUPS = 8
TOPK_GROUPS = 4
EXPERT_FF = 256
SHARED_FF = 256
ROUTED_SCALE = 2.5
LN_EPS = 1e-5
RMS_EPS = 1e-6
DEPTH = 1
DEEPNORM_ALPHA = (2.0 * DEPTH) ** 0.25

LANES = 128
HEAD_PAD = 128
ROPE_LO = QK_NOPE
ROPE_HALF = QK_ROPE // 2

TM = 512
TQ = 512
S5_CHUNK = 16
TM_DISPATCH = 256
TM_EXPERT = 256
VMEM_LIMIT = 48 * 1024 * 1024


def _cparams(*sem):
    return pltpu.CompilerParams(dimension_semantics=sem, vmem_limit_bytes=VMEM_LIMIT)


def _dot(a, b):
    return jnp.dot(a, b, preferred_element_type=F32)


def _dot_nt(a, b, precision=None):
    return lax.dot_general(a, b, (((1,), (1,)), ((), ())), preferred_element_type=F32,
                           precision=precision)


def _dot_exact(a, b):
    return jnp.dot(a, b, preferred_element_type=F32, precision=lax.Precision.HIGHEST)


def _layer_norm(h, g, b):
    mu = jnp.mean(h, axis=-1, keepdims=True)
    c = h - mu
    var = jnp.mean(c * c, axis=-1, keepdims=True)
    return c * lax.rsqrt(var + LN_EPS) * g + b


def _rms_norm(h, g):
    return h * lax.rsqrt(jnp.mean(h * h, axis=-1, keepdims=True) + RMS_EPS) * g


def _full(shape):
    n = len(shape)
    return pl.BlockSpec(shape, lambda *_: (0,) * n)


def _rope_angle_kernel(pos_ref, invf_ref, cos_ref, sin_ref):
    ang = pos_ref[...].astype(F32) * invf_ref[...]
    cos_ref[...] = jnp.cos(ang)
    sin_ref[...] = jnp.sin(ang)


def _rope_tables(positions):
    T = positions.size
    rows = T * ROPE_HALF // LANES
    pos_rep = jnp.repeat(positions.reshape(T), ROPE_HALF).reshape(rows, LANES)
    inv_freq = ROPE_THETA ** (-jnp.arange(0, QK_ROPE, 2, dtype=F32) / QK_ROPE)
    invf = jnp.tile(inv_freq, LANES // ROPE_HALF).reshape(1, LANES)
    cos, sin = pl.pallas_call(
        _rope_angle_kernel,
        out_shape=(jax.ShapeDtypeStruct((rows, LANES), F32),) * 2,
        name="rope_angles",
    )(pos_rep, invf)
    cos = cos.reshape(T, ROPE_HALF)
    sin = sin.reshape(T, ROPE_HALF)
    one = jnp.ones((T, ROPE_LO), F32)
    zero_lo = jnp.zeros((T, ROPE_LO), F32)
    zero_h = jnp.zeros((T, ROPE_HALF), F32)
    tail1 = jnp.ones((T, HEAD_PAD - ROPE_LO - QK_ROPE), F32)
    tail0 = jnp.zeros((T, HEAD_PAD - ROPE_LO - QK_ROPE), F32)
    c_tab = jnp.concatenate([one, cos, cos, tail1], axis=1)
    s_up = jnp.concatenate([zero_lo, -sin, zero_h, tail0], axis=1)
    s_dn = jnp.concatenate([zero_lo, zero_h, sin, tail0], axis=1)
    return c_tab, s_up, s_dn


def _rope(x, c_tab, s_up, s_dn):
    return (x * c_tab + pltpu.roll(x, HEAD_PAD - ROPE_HALF, axis=1) * s_up
            + pltpu.roll(x, ROPE_HALF, axis=1) * s_dn)


def _inproj_kernel(x_ref, c_ref, su_ref, sd_ref, wlat_ref, wu_ref, wgm_ref, wgs_ref, qg_ref, kvg_ref,
                   wuq_ref, wuk_ref, wuv_ref, q_ref, k_ref, v_ref, u_ref, gm_ref, gs_ref):
    xb = x_ref[...].astype(BF16)
    lat = _dot(xb, wlat_ref[...])
    qn = _rms_norm(lat[:, :Q_LORA], qg_ref[...]).astype(BF16)
    kvn = _rms_norm(lat[:, Q_LORA:Q_LORA + KV_LORA], kvg_ref[...]).astype(BF16)
    c_tab, s_up, s_dn = c_ref[...], su_ref[...], sd_ref[...]
    k_rope = _rope(lat[:, Q_LORA + KV_LORA:], c_tab, s_up, s_dn)
    q = _dot(qn, wuq_ref[...])
    k = _dot(kvn, wuk_ref[...])
    scale = (QK_NOPE + QK_ROPE) ** -0.5
    for h in range(MLA_HEADS):
        sl = slice(h * HEAD_PAD, (h + 1) * HEAD_PAD)
        q_ref[:, sl] = (_rope(q[:, sl], c_tab, s_up, s_dn) * scale).astype(BF16)
        k_ref[:, sl] = (k[:, sl] + k_rope).astype(BF16)
    v_ref[...] = _dot(kvn, wuv_ref[...]).astype(BF16)
    u_ref[...] = _dot(xb, wu_ref[...]).astype(BF16)
    gm_ref[...] = _dot(xb, wgm_ref[...]).astype(BF16)
    gs_ref[...] = _dot(xb, wgs_ref[...]).astype(BF16)


def _pad_heads(w, head_w, lo_w):
    K = w.shape[0]
    w = w.reshape(K, MLA_HEADS, head_w)[:, :, :lo_w]
    w = jnp.pad(w, ((0, 0), (0, 0), (0, HEAD_PAD - lo_w)))
    return w.reshape(K, MLA_HEADS * HEAD_PAD)


def _inproj(x2d, tabs, w_in, q_norm_g, kv_norm_g, w_uq, w_ukv):
    T = x2d.shape[0]
    o_rope = Q_LORA + KV_LORA
    o_u = o_rope + QK_ROPE
    o_gm = o_u + S5_WIDTH
    o_gs = o_gm + D_MODEL
    w_rope = jnp.pad(w_in[:, o_rope:o_u], ((0, 0), (ROPE_LO, HEAD_PAD - ROPE_LO - QK_ROPE)))
    w_lat = jnp.concatenate([w_in[:, :o_rope], w_rope], axis=1).astype(BF16)
    w_u = w_in[:, o_u:o_gm].astype(BF16)
    w_gm = w_in[:, o_gm:o_gs].astype(BF16)
    w_gs = w_in[:, o_gs:].astype(BF16)
    wuq = _pad_heads(w_uq, QK_NOPE + QK_ROPE, QK_NOPE + QK_ROPE).astype(BF16)
    kv3 = w_ukv.reshape(KV_LORA, MLA_HEADS, QK_NOPE + V_HEAD)
    wuk = _pad_heads(kv3[:, :, :QK_NOPE].reshape(KV_LORA, -1), QK_NOPE, QK_NOPE).astype(BF16)
    wuv = _pad_heads(kv3[:, :, QK_NOPE:].reshape(KV_LORA, -1), V_HEAD, V_HEAD).astype(BF16)
    HP = MLA_HEADS * HEAD_PAD
    tile = lambda w: pl.BlockSpec((TM, w), lambda i: (i, 0))
    return pl.pallas_call(
        _inproj_kernel,
        grid=(T // TM,),
        in_specs=[tile(D_MODEL), tile(HEAD_PAD), tile(HEAD_PAD), tile(HEAD_PAD),
                  _full(w_lat.shape), _full(w_u.shape), _full(w_gm.shape), _full(w_gs.shape),
                  _full((1, Q_LORA)), _full((1, KV_LORA)),
                  _full(wuq.shape), _full(wuk.shape), _full(wuv.shape)],
        out_specs=[tile(HP), tile(HP), tile(HP), tile(S5_WIDTH), tile(D_MODEL), tile(D_MODEL)],
        out_shape=[jax.ShapeDtypeStruct((T, HP), BF16)] * 3
        + [jax.ShapeDtypeStruct((T, S5_WIDTH), BF16)]
        + [jax.ShapeDtypeStruct((T, D_MODEL), BF16)] * 2,
        compiler_params=_cparams("parallel"),
        name="inproj_mla_prep",
    )(x2d, *tabs, w_lat, w_u, w_gm, w_gs, q_norm_g.reshape(1, -1), kv_norm_g.reshape(1, -1),
      wuq, wuk, wuv)


def _attn_kernel(q_ref, k_ref, v_ref, o_ref):
    qi = pl.program_id(2)
    q = q_ref[0]

    def step(j, carry, masked):
        m, l, acc = carry
        start = pl.multiple_of(j * TQ, TQ)
        k = k_ref[0, pl.ds(start, TQ), :]
        v = v_ref[0, pl.ds(start, TQ), :]
        s = _dot_nt(q, k)
        if masked:
            row = lax.broadcasted_iota(I32, (TQ, TQ), 0)
            col = lax.broadcasted_iota(I32, (TQ, TQ), 1)
            s = jnp.where(col <= row, s, -jnp.inf)
        m_new = jnp.maximum(m, jnp.max(s, axis=-1, keepdims=True))
        p = jnp.exp(s - m_new)
        a = jnp.exp(m - m_new)
        l = a * l + jnp.sum(p, axis=-1, keepdims=True)
        acc = a * acc + _dot(p.astype(BF16), v)
        return m_new, l, acc

    init = (jnp.full((TQ, 1), -jnp.inf, F32), jnp.zeros((TQ, 1), F32), jnp.zeros((TQ, HEAD_PAD), F32))
    carry = lax.fori_loop(0, qi, lambda j, c: step(j, c, False), init)
    m, l, acc = step(qi, carry, True)
    o_ref[0] = (acc / l).astype(BF16)


def _attention(q, k, v, B, S):
    HP = MLA_HEADS * HEAD_PAD
    q, k, v = (a.reshape(B, S, HP) for a in (q, k, v))
    o = pl.pallas_call(
        _attn_kernel,
        grid=(B, MLA_HEADS, S // TQ),
        in_specs=[pl.BlockSpec((1, TQ, HEAD_PAD), lambda b, h, i: (b, i, h)),
                  pl.BlockSpec((1, S, HEAD_PAD), lambda b, h, i: (b, 0, h)),
                  pl.BlockSpec((1, S, HEAD_PAD), lambda b, h, i: (b, 0, h))],
        out_specs=pl.BlockSpec((1, TQ, HEAD_PAD), lambda b, h, i: (b, i, h)),
        out_shape=jax.ShapeDtypeStruct((B, S, HP), BF16),
        compiler_params=_cparams("parallel", "parallel", "arbitrary"),
        name="mla_flash_attention",
    )(q, k, v)
    return o.reshape(B * S, HP)


CH2 = S5_CHUNK * S5_GROUP_CH
P2 = 2 * S5_STATE
SCAN_LEVELS_MAX = 16


def _s5_prep_kernel(n_levels, arow_ref, acol_ref, bt_ref, ct_ref, d_ref,
                    mt_ref, wt_ref, vt_ref, apow_ref, dvec_ref, mt_acc):
    P, H, L = S5_STATE, S5_GROUP_CH, S5_CHUNK

    def powers(ar, ai, dt, n):
        e = jnp.exp(n * (ar * dt))
        ang = n * (ai * dt)
        return e * jnp.cos(ang), e * jnp.sin(ang)

    def zoh_coef(ar, ai, dt):
        pr, pi = powers(ar, ai, dt, 1.0)
        nr, ni = pr - 1.0, pi
        den = ar * ar + ai * ai
        return (nr * ar + ni * ai) / den, (ni * ar - nr * ai) / den

    row = arow_ref[0]
    ar, ai, dt = row[0:1], row[1:2], jnp.exp(row[2:3])
    cr, ci = zoh_coef(ar, ai, dt)
    bt = bt_ref[0]
    btr = jnp.concatenate([bt[:H], bt[:H]], axis=1)
    bti = jnp.concatenate([bt[H:], bt[H:]], axis=1)
    bbr = cr * btr - ci * bti
    bbi = cr * bti + ci * btr
    lane = lax.broadcasted_iota(I32, (1, P2), 1)
    is_re = lane < P
    ridx = lax.broadcasted_iota(I32, (CH2, P2), 0) // H
    pwr, pwi = powers(ar, ai, dt, (L - 1 - ridx).astype(F32))
    rsel = (lax.broadcasted_iota(I32, (CH2, H), 0) % H == lax.broadcasted_iota(I32, (CH2, H), 1)).astype(F32)
    bbr_t = _dot_exact(rsel, bbr)
    bbi_t = _dot_exact(rsel, bbi)
    w_re = pwr * bbr_t - pwi * bbi_t
    w_im = pwr * bbi_t + pwi * bbr_t
    wt_ref[0] = jnp.where(is_re, w_re, w_im).astype(BF16)
    a_r, a_i = powers(ar, ai, dt, float(L))
    sign = jnp.where(is_re, -1.0, 1.0)
    for lvl in range(SCAN_LEVELS_MAX):
        if lvl < n_levels:
            apow_ref[0, lvl:lvl + 1, :] = a_r
            apow_ref[0, SCAN_LEVELS_MAX + lvl:SCAN_LEVELS_MAX + lvl + 1, :] = sign * a_i
            a_r, a_i = a_r * a_r - a_i * a_i, 2.0 * a_r * a_i
        else:
            apow_ref[0, lvl:lvl + 1, :] = jnp.zeros_like(a_r)
            apow_ref[0, SCAN_LEVELS_MAX + lvl:SCAN_LEVELS_MAX + lvl + 1, :] = jnp.zeros_like(a_r)

    col = acol_ref[0]
    arc, aic, dtc = col[:, 0:1], col[:, 1:2], jnp.exp(col[:, 2:3])
    ct = ct_ref[0]
    csel = (lax.broadcasted_iota(I32, (H, CH2), 1) % H == lax.broadcasted_iota(I32, (H, CH2), 0)).astype(F32)
    ctr = _dot_exact(ct[:, :H], csel)
    cti = _dot_exact(ct[:, H:], csel)
    sidx = (lax.broadcasted_iota(I32, (P, CH2), 1) // H).astype(F32)
    pr, pi = powers(arc, aic, dtc, sidx)
    g_r = ctr * pr - cti * pi
    g_i = ctr * pi + cti * pr
    a1r, a1i = powers(arc, aic, dtc, 1.0)
    v_r = g_r * a1r - g_i * a1i
    v_i = g_r * a1i + g_i * a1r
    vt_ref[0, :P, :] = v_r.astype(BF16)
    vt_ref[0, P:, :] = (-v_i).astype(BF16)
    bbr64, bbi64 = bbr[:, :P], bbi[:, :P]
    kt = _dot_exact(bbr64, g_r) - _dot_exact(bbi64, g_i)
    lane2 = lax.broadcasted_iota(I32, (H, CH2), 1)
    for r in range(L):
        blk = kt if r == 0 else jnp.where(lane2 >= r * H, pltpu.roll(kt, r * H, axis=1), 0.0)
        mt_acc[r * H:(r + 1) * H, :] = blk
    mt_ref[0] = mt_acc[...].astype(BF16)
    dvec_ref[0] = _dot_exact(d_ref[0], csel)


def _s5_prep(n_levels, s5_a_re, s5_a_im, s5_log_dt, s5_b_re, s5_b_im, s5_c_re, s5_c_im, s5_d):
    G, P, H = S5_GROUPS, S5_STATE, S5_GROUP_CH
    ldt = jnp.broadcast_to(s5_log_dt[:, None], (G, P))
    arow = jnp.stack([s5_a_re, s5_a_im, ldt], axis=1)
    arow = jnp.concatenate([arow, arow], axis=2)
    acol = jnp.stack([s5_a_re, s5_a_im, ldt], axis=2)
    bt = jnp.concatenate([s5_b_re.transpose(0, 2, 1), s5_b_im.transpose(0, 2, 1)], axis=1)
    ct = jnp.concatenate([s5_c_re.transpose(0, 2, 1), s5_c_im.transpose(0, 2, 1)], axis=2)
    d = s5_d.reshape(G, 1, H)
    blk = lambda *s: pl.BlockSpec((1,) + s, lambda g: (g, 0, 0))
    return pl.pallas_call(
        functools.partial(_s5_prep_kernel, n_levels),
        grid=(G,),
        in_specs=[blk(3, P2), blk(P, 3), blk(2 * H, P), blk(P, 2 * H), blk(1, H)],
        out_specs=[blk(CH2, CH2), blk(CH2, P2), blk(P2, CH2), blk(2 * SCAN_LEVELS_MAX, P2), blk(1, CH2)],
        out_shape=[jax.ShapeDtypeStruct((G, CH2, CH2), BF16),
                   jax.ShapeDtypeStruct((G, CH2, P2), BF16),
                   jax.ShapeDtypeStruct((G, P2, CH2), BF16),
                   jax.ShapeDtypeStruct((G, 2 * SCAN_LEVELS_MAX, P2), F32),
                   jax.ShapeDtypeStruct((G, 1, CH2), F32)],
        scratch_shapes=[pltpu.VMEM((CH2, CH2), F32)],
        compiler_params=_cparams("parallel"),
        name="s5_discretise",
    )(arow, acol, bt, ct, d)


def _gelu_tanh(y):
    return 0.5 * y * (1.0 + jnp.tanh(math.sqrt(2.0 / math.pi) * (y + 0.044715 * (y * y * y))))


def _s5_scan_kernel(n_levels, batch, u_ref, mt_ref, wt_ref, vt_ref, apow_ref, dvec_ref, y_ref):
    u = u_ref[0]
    rows = u.shape[0]
    x = _dot(u, wt_ref[0])
    ridx = lax.broadcasted_iota(I32, (rows, P2), 0)
    apow = apow_ref[0]
    for lvl in range(n_levels):
        sh = batch * (1 << lvl)
        prev = jnp.where(ridx >= sh, pltpu.roll(x, sh, axis=0), 0.0)
        a_r = apow[lvl:lvl + 1]
        a_i = apow[SCAN_LEVELS_MAX + lvl:SCAN_LEVELS_MAX + lvl + 1]
        x = x + a_r * prev + a_i * pltpu.roll(prev, S5_STATE, axis=1)
    x_in = jnp.where(ridx >= batch, pltpu.roll(x, batch, axis=0), 0.0)
    y = _dot(u, mt_ref[0]) + _dot(x_in.astype(BF16), vt_ref[0]) + dvec_ref[0] * u.astype(F32)
    y_ref[0] = _gelu_tanh(y).astype(BF16)


def _s5_branch(u, B, S, s5_params):
    G, H, L = S5_GROUPS, S5_GROUP_CH, S5_CHUNK
    nc = S // L
    n_levels = max(1, (nc - 1).bit_length())
    assert n_levels <= SCAN_LEVELS_MAX
    mt, wt, vt, apow, dvec = _s5_prep(n_levels, *s5_params)
    rows = nc * B
    ug = u.reshape(B, nc, L, G, H).transpose(3, 1, 0, 2, 4).reshape(G, rows, CH2)
    blk = lambda *s: pl.BlockSpec((1,) + s, lambda g: (g, 0, 0))
    yg = pl.pallas_call(
        functools.partial(_s5_scan_kernel, n_levels, B),
        grid=(G,),
        in_specs=[blk(rows, CH2), blk(CH2, CH2), blk(CH2, P2), blk(P2, CH2),
                  blk(2 * SCAN_LEVELS_MAX, P2), blk(1, CH2)],
        out_specs=blk(rows, CH2),
        out_shape=jax.ShapeDtypeStruct((G, rows, CH2), BF16),
        compiler_params=_cparams("parallel"),
        name="s5_chunk_scan",
    )(ug, mt, wt, vt, apow, dvec)
    return yg.reshape(G, nc, B, L, H).transpose(2, 1, 3, 0, 4).reshape(B * S, S5_WIDTH)


def _merge_kernel(x_ref, o_ref, ys_ref, gm_ref, gs_ref, wmo_ref, wglu_ref, wout_ref, g_ref, b_ref, x1_ref):
    y_mla = _dot(o_ref[...], wmo_ref[...])
    z = _dot(ys_ref[...], wglu_ref[...])
    y_s5 = z[:, :D_MODEL] * jax.nn.sigmoid(z[:, D_MODEL:])
    merged = (jax.nn.sigmoid(gm_ref[...].astype(F32)) * y_mla
              + jax.nn.sigmoid(gs_ref[...].astype(F32)) * y_s5)
    mix = _dot(merged.astype(BF16), wout_ref[...])
    x1_ref[...] = _layer_norm(DEEPNORM_ALPHA * x_ref[...] + mix, g_ref[...], b_ref[...])


def _merge(x2d, o, ys, gm, gs, w_mla_o, w_s5_glu, w_out, ln_g, ln_b):
    T = x2d.shape[0]
    HP = MLA_HEADS * HEAD_PAD
    wmo = jnp.pad(w_mla_o.reshape(MLA_HEADS, V_HEAD, D_MODEL), ((0, 0), (0, HEAD_PAD - V_HEAD), (0, 0)))
    wmo = wmo.reshape(HP, D_MODEL).astype(BF16)
    tile = lambda w: pl.BlockSpec((TM, w), lambda i: (i, 0))
    return pl.pallas_call(
        _merge_kernel,
        grid=(T // TM,),
        in_specs=[tile(D_MODEL), tile(HP), tile(S5_WIDTH), tile(D_MODEL), tile(D_MODEL),
                  _full((HP, D_MODEL)), _full((S5_WIDTH, 2 * D_MODEL)), _full((D_MODEL, D_MODEL)),
                  _full((1, D_MODEL)), _full((1, D_MODEL))],
        out_specs=tile(D_MODEL),
        out_shape=jax.ShapeDtypeStruct((T, D_MODEL), F32),
        compiler_params=_cparams("parallel"),
        name="merge_outproj_ln1",
    )(x2d, o, ys, gm, gs, wmo, w_s5_glu.astype(BF16), w_out.astype(BF16),
      ln_g.reshape(1, -1), ln_b.reshape(1, -1))


def _memkv_kernel(mem_ref, g_ref, b_ref, w_ref, k_ref, v_ref):
    m = _layer_norm(mem_ref[0], g_ref[...], b_ref[...]).astype(BF16)
    kv = _dot(m, w_ref[...])
    hd = XATTN_HEADS * XATTN_HEAD_DIM
    k_ref[0] = kv[:, :hd].astype(BF16)
    v_ref[0] = kv[:, hd:].astype(BF16)


def _xattn_kernel(x1_ref, k_ref, v_ref, wq_ref, wo_ref, g_ref, b_ref, x2_ref, x2p_ref):
    x1 = x1_ref[0]
    q = (_dot(x1.astype(BF16), wq_ref[...]) * (XATTN_HEAD_DIM ** -0.5)).astype(BF16)
    k = k_ref[0]
    v = v_ref[0]
    outs = []
    for h in range(XATTN_HEADS):
        sl = slice(h * XATTN_HEAD_DIM, (h + 1) * XATTN_HEAD_DIM)
        s = _dot_nt(q[:, sl], k[:, sl])
        p = jnp.exp(s - jnp.max(s, axis=-1, keepdims=True))
        o = _dot(p.astype(BF16), v[:, sl]) / jnp.sum(p, axis=-1, keepdims=True)
        outs.append(o.astype(BF16))
    xa = _dot(jnp.concatenate(outs, axis=1), wo_ref[...])
    x2 = _layer_norm(DEEPNORM_ALPHA * x1 + xa, g_ref[...], b_ref[...])
    x2_ref[0] = x2
    half = D_MODEL // 2
    lo = pltpu.bitcast(x2[:, :half].astype(BF16).astype(F32), U32) >> 16
    hi = pltpu.bitcast(x2[:, half:].astype(BF16).astype(F32), U32)
    x2p_ref[0] = hi | lo


def _cross_attention(x1, mem, B, S, mem_ln_g, mem_ln_b, w_xq, w_xkv, w_xo, ln_g, ln_b):
    M = mem.shape[1]
    hd = XATTN_HEADS * XATTN_HEAD_DIM
    row = lambda a: a.reshape(1, -1)
    k, v = pl.pallas_call(
        _memkv_kernel,
        grid=(B,),
        in_specs=[pl.BlockSpec((1, M, D_MODEL), lambda b: (b, 0, 0)), _full((1, D_MODEL)), _full((1, D_MODEL)),
                  _full((D_MODEL, 2 * hd))],
        out_specs=[pl.BlockSpec((1, M, hd), lambda b: (b, 0, 0))] * 2,
        out_shape=[jax.ShapeDtypeStruct((B, M, hd), BF16)] * 2,
        compiler_params=_cparams("parallel"),
        name="memory_kv",
    )(mem, row(mem_ln_g), row(mem_ln_b), w_xkv.astype(BF16))
    tile = lambda w: pl.BlockSpec((1, TM, w), lambda b, i: (b, i, 0))
    x2, x2p = pl.pallas_call(
        _xattn_kernel,
        grid=(B, S // TM),
        in_specs=[tile(D_MODEL),
                  pl.BlockSpec((1, M, hd), lambda b, i: (b, 0, 0)), pl.BlockSpec((1, M, hd), lambda b, i: (b, 0, 0)),
                  _full((D_MODEL, hd)), _full((hd, D_MODEL)), _full((1, D_MODEL)), _full((1, D_MODEL))],
        out_specs=[tile(D_MODEL), tile(D_MODEL // 2)],
        out_shape=[jax.ShapeDtypeStruct((B, S, D_MODEL), F32), jax.ShapeDtypeStruct((B, S, D_MODEL // 2), U32)],
        compiler_params=_cparams("parallel", "parallel"),
        name="cross_attention_ln2",
    )(x1.reshape(B, S, D_MODEL), k, v, w_xq.astype(BF16), w_xo.astype(BF16), row(ln_g), row(ln_b))
    return x2.reshape(B * S, D_MODEL), x2p.reshape(B * S, D_MODEL // 2)


def _router_kernel(x_ref, w_ref, bias_ref, eidx_ref, rank_ref, wk_ref, cnt_ref, carry):
    E, per = N_EXPERTS, N_EXPERTS // N_EXPERT_GROUPS
    tm = x_ref.shape[0]

    @pl.when(pl.program_id(0) == 0)
    def _():
        carry[...] = jnp.zeros_like(carry)

    logits = _dot_nt(w_ref[...], x_ref[...], precision=lax.Precision.HIGHEST)
    scores = jax.nn.sigmoid(logits)
    sel = scores + bias_ref[...]
    neg = -jnp.inf
    i8 = lax.broadcasted_iota(I32, (per, tm), 0)
    gscore = []
    for g in range(N_EXPERT_GROUPS):
        blk = sel[g * per:(g + 1) * per]
        m1 = jnp.max(blk, axis=0, keepdims=True)
        i1 = jnp.min(jnp.where(blk == m1, i8, per), axis=0, keepdims=True)
        m2 = jnp.max(jnp.where(i8 == i1, neg, blk), axis=0, keepdims=True)
        gscore.append(m1 + m2)
    blocks = []
    for g in range(N_EXPERT_GROUPS):
        ahead = jnp.zeros((1, tm), I32)
        for o in range(N_EXPERT_GROUPS):
            if o == g:
                continue
            before = (gscore[o] >= gscore[g]) if o < g else (gscore[o] > gscore[g])
            ahead = ahead + before.astype(I32)
        blocks.append(jnp.where(ahead < TOPK_GROUPS, sel[g * per:(g + 1) * per], neg))
    cur = jnp.concatenate(blocks, axis=0)
    ie = lax.broadcasted_iota(I32, (E, tm), 0)
    picked = jnp.zeros((E, tm), F32)
    idxs = []
    for _ in range(TOP_K):
        m = jnp.max(cur, axis=0, keepdims=True)
        idx = jnp.min(jnp.where(cur == m, ie, E), axis=0, keepdims=True)
        hit = ie == idx
        picked = jnp.where(hit, 1.0, picked)
        cur = jnp.where(hit, neg, cur)
        idxs.append(idx)
    wsel = scores * picked
    wnorm = wsel / jnp.sum(wsel, axis=0, keepdims=True) * ROUTED_SCALE
    tri = (lax.broadcasted_iota(I32, (tm, tm), 0) <= lax.broadcasted_iota(I32, (tm, tm), 1)).astype(BF16)
    incl = _dot(picked.astype(BF16), tri)
    rank = carry[...] + incl - 1.0
    for kk in range(TOP_K):
        hit = ie == idxs[kk]
        eidx_ref[kk:kk + 1, :] = idxs[kk]
        rank_ref[kk:kk + 1, :] = jnp.sum(jnp.where(hit, rank, 0.0), axis=0, keepdims=True).astype(I32)
        wk_ref[kk:kk + 1, :] = jnp.sum(jnp.where(hit, wnorm, 0.0), axis=0, keepdims=True)
    total = carry[...] + jnp.sum(picked, axis=1, keepdims=True)
    carry[...] = total
    cnt_ref[...] = jnp.broadcast_to(total, cnt_ref.shape)


def _router(x2, w_router, router_bias):
    T = x2.shape[0]
    E = N_EXPERTS
    kt = pl.BlockSpec((TOP_K, TM), lambda i: (0, i))
    return pl.pallas_call(
        _router_kernel,
        grid=(T // TM,),
        in_specs=[pl.BlockSpec((TM, D_MODEL), lambda i: (i, 0)), _full((E, D_MODEL)), _full((E, 1))],
        out_specs=[kt, kt, kt, _full((E, LANES))],
        out_shape=[jax.ShapeDtypeStruct((TOP_K, T), I32), jax.ShapeDtypeStruct((TOP_K, T), I32),
                   jax.ShapeDtypeStruct((TOP_K, T), F32), jax.ShapeDtypeStruct((E, LANES), F32)],
        scratch_shapes=[pltpu.VMEM((E, 1), F32)],
        compiler_params=_cparams("arbitrary"),
        name="router_topk",
    )(x2, w_router.T, router_bias.reshape(E, 1))


def _dispatch_kernel(slots_hbm, x_ref, xs_in, xs_hbm, slots_smem, ssem, dsem):
    del xs_in
    i = pl.program_id(0)
    tm = x_ref.shape[0]
    cp = pltpu.make_async_copy(slots_hbm.at[i], slots_smem, ssem)
    cp.start()
    cp.wait()

    def row_copy(t, slot):
        return pltpu.make_async_copy(x_ref.at[pl.ds(t, 1), :], xs_hbm.at[pl.ds(slot, 1), :], dsem)

    def issue(t, c):
        for kk in range(TOP_K):
            row_copy(t, slots_smem[t * TOP_K + kk]).start()
        return c

    lax.fori_loop(0, tm, issue, 0)

    def drain(t, c):
        for kk in range(TOP_K):
            row_copy(0, 0).wait()
        return c

    lax.fori_loop(0, tm, drain, 0)


def _dispatch(slots_tiled, x2p, n_rows):
    T, W = x2p.shape
    nt = T // TM_DISPATCH
    xs0 = jnp.zeros((n_rows, W), U32)
    return pl.pallas_call(
        _dispatch_kernel,
        grid=(nt,),
        in_specs=[pl.BlockSpec(memory_space=pl.ANY),
                  pl.BlockSpec((TM_DISPATCH, W), lambda i: (i, 0)),
                  pl.BlockSpec(memory_space=pl.ANY)],
        out_specs=pl.BlockSpec(memory_space=pl.ANY),
        out_shape=jax.ShapeDtypeStruct((n_rows, W), U32),
        scratch_shapes=[pltpu.SMEM((TM_DISPATCH * TOP_K,), I32), pltpu.SemaphoreType.DMA(()),
                        pltpu.SemaphoreType.DMA(())],
        input_output_aliases={2: 0},
        compiler_params=_cparams("arbitrary"),
        name="moe_dispatch",
    )(slots_tiled, x2p, xs0)


def _expert_kernel(be_ref, bv_ref, xs_ref, wgu_ref, wdn_ref, y_ref):
    i = pl.program_id(0)
    half = D_MODEL // 2

    @pl.when(bv_ref[i] > 0)
    def _():
        w = xs_ref[...]
        lo = pltpu.bitcast(w << 16, F32).astype(BF16)
        hi = pltpu.bitcast(w & jnp.uint32(0xFFFF0000), F32).astype(BF16)
        h = _dot(lo, wgu_ref[0, :half, :]) + _dot(hi, wgu_ref[0, half:, :])
        g, u = h[:, :EXPERT_FF], h[:, EXPERT_FF:]
        act = (g * jax.nn.sigmoid(g) * u).astype(BF16)
        y_ref[...] = _dot(act, wdn_ref[0])

    @pl.when(bv_ref[i] == 0)
    def _():
        y_ref[...] = jnp.zeros_like(y_ref)


def _experts(block_e, block_valid, xs, w_exp_gu, w_exp_down):
    n_rows, W = xs.shape
    nb = n_rows // TM_EXPERT
    grid_spec = pltpu.PrefetchScalarGridSpec(
        num_scalar_prefetch=2,
        grid=(nb,),
        in_specs=[pl.BlockSpec((TM_EXPERT, W), lambda i, be, bv: (i, 0)),
                  pl.BlockSpec((1, D_MODEL, 2 * EXPERT_FF), lambda i, be, bv: (be[i], 0, 0)),
                  pl.BlockSpec((1, EXPERT_FF, D_MODEL), lambda i, be, bv: (be[i], 0, 0))],
        out_specs=pl.BlockSpec((TM_EXPERT, D_MODEL), lambda i, be, bv: (i, 0)),
    )
    return pl.pallas_call(
        _expert_kernel,
        grid_spec=grid_spec,
        out_shape=jax.ShapeDtypeStruct((n_rows, D_MODEL), F32),
        compiler_params=_cparams("arbitrary"),
        name="moe_grouped_swiglu",
    )(block_e, block_valid, xs, w_exp_gu.astype(BF16), w_exp_down.astype(BF16))


def _combine_kernel(slots_hbm, y_hbm, x2_ref, wk_ref, wsg_ref, wsd_ref, g_ref, b_ref, out_ref,
                    slots_smem, buf, ssem, dsem):
    i = pl.program_id(0)
    tm = x2_ref.shape[0]
    cp = pltpu.make_async_copy(slots_hbm.at[i], slots_smem, ssem)
    cp.start()
    cp.wait()

    def row_copy(t, kk, slot):
        return pltpu.make_async_copy(y_hbm.at[pl.ds(slot, 1), :], buf.at[kk, pl.ds(t, 1), :], dsem)

    def issue(t, c):
        for kk in range(TOP_K):
            row_copy(t, kk, slots_smem[t * TOP_K + kk]).start()
        return c

    lax.fori_loop(0, tm, issue, 0)

    x2 = x2_ref[...]
    h = _dot(x2.astype(BF16), wsg_ref[...])
    g, u = h[:, :SHARED_FF], h[:, SHARED_FF:]
    ff = _dot((g * jax.nn.sigmoid(g) * u).astype(BF16), wsd_ref[...])

    def drain(t, c):
        for kk in range(TOP_K):
            row_copy(0, kk, 0).wait()
        return c

    lax.fori_loop(0, tm, drain, 0)
    wk = wk_ref[...]
    for kk in range(TOP_K):
        ff = ff + wk[:, kk:kk + 1] * buf[kk]
    out_ref[...] = _layer_norm(DEEPNORM_ALPHA * x2 + ff, g_ref[...], b_ref[...])


def _combine(slots_tiled, y, x2, wk_t, w_sh_gu, w_sh_down, ln_g, ln_b):
    T = x2.shape[0]
    nt = T // TM_DISPATCH
    tile = lambda w: pl.BlockSpec((TM_DISPATCH, w), lambda i: (i, 0))
    return pl.pallas_call(
        _combine_kernel,
        grid=(nt,),
        in_specs=[pl.BlockSpec(memory_space=pl.ANY), pl.BlockSpec(memory_space=pl.ANY),
                  tile(D_MODEL), tile(TOP_K),
                  _full((D_MODEL, 2 * SHARED_FF)), _full((SHARED_FF, D_MODEL)),
                  _full((1, D_MODEL)), _full((1, D_MODEL))],
        out_specs=tile(D_MODEL),
        out_shape=jax.ShapeDtypeStruct((T, D_MODEL), F32),
        scratch_shapes=[pltpu.SMEM((TM_DISPATCH * TOP_K,), I32),
                        pltpu.VMEM((TOP_K, TM_DISPATCH, D_MODEL), F32),
                        pltpu.SemaphoreType.DMA(()), pltpu.SemaphoreType.DMA(())],
        compiler_params=_cparams("arbitrary"),
        name="moe_combine_shared_ln3",
    )(slots_tiled, y, x2, wk_t, w_sh_gu.astype(BF16), w_sh_down.astype(BF16),
      ln_g.reshape(1, -1), ln_b.reshape(1, -1))


def _moe(x2, x2p, w_router, router_bias, w_exp_gu, w_exp_down, w_sh_gu, w_sh_down, ln_g, ln_b):
    T = x2.shape[0]
    E = N_EXPERTS
    eidx, rank, wk, cnt = _router(x2, w_router, router_bias)
    counts = cnt[:, 0].astype(I32)
    padded = (counts + TM_EXPERT - 1) // TM_EXPERT * TM_EXPERT
    pends = jnp.cumsum(padded)
    pstarts = pends - padded
    nb = (T * TOP_K + E * (TM_EXPERT - 1)) // TM_EXPERT
    block_start = jnp.arange(nb, dtype=I32) * TM_EXPERT
    block_e = jnp.minimum(jnp.sum(block_start[:, None] >= pends[None, :], axis=1), E - 1).astype(I32)
    block_valid = (block_start < pends[-1]).astype(I32)
    onehot = eidx[:, :, None] == jnp.arange(E, dtype=I32)[None, None, :]
    slots = jnp.sum(jnp.where(onehot, pstarts[None, None, :], 0), axis=2) + rank
    nt = T // TM_DISPATCH
    slots_tiled = slots.reshape(TOP_K, nt, TM_DISPATCH).transpose(1, 2, 0).reshape(nt, TM_DISPATCH * TOP_K)
    xs = _dispatch(slots_tiled, x2p, nb * TM_EXPERT)
    y = _experts(block_e, block_valid, xs, w_exp_gu, w_exp_down)
    return _combine(slots_tiled, y, x2, wk.T, w_sh_gu, w_sh_down, ln_g, ln_b)


def kernel(x, mem, positions, w_in, q_norm_g, kv_norm_g, w_uq, w_ukv, w_mla_o, s5_a_re, s5_a_im, s5_log_dt, s5_b_re, s5_b_im, s5_c_re, s5_c_im, s5_d, w_s5_glu, w_out, ln1_g, ln1_b, mem_ln_g, mem_ln_b, w_xq, w_xkv, w_xo, ln2_g, ln2_b, w_router, router_bias, w_exp_gu, w_exp_down, w_sh_gu, w_sh_down, ln3_g, ln3_b):
    B, S, D = x.shape
    assert D == D_MODEL and S % TQ == 0 and (B * S) % TM == 0 and S % S5_CHUNK == 0
    xc = x.reshape(B * S, D)
    for l in range(w_in.shape[0]):
        tabs = _rope_tables(positions)
        q, k, v, u, gm, gs = _inproj(xc, tabs, w_in[l], q_norm_g[l], kv_norm_g[l], w_uq[l], w_ukv[l])
        o = _attention(q, k, v, B, S)
        ys = _s5_branch(u, B, S, (s5_a_re[l], s5_a_im[l], s5_log_dt[l], s5_b_re[l], s5_b_im[l],
                                  s5_c_re[l], s5_c_im[l], s5_d[l]))
        x1 = _merge(xc, o, ys, gm, gs, w_mla_o[l], w_s5_glu[l], w_out[l], ln1_g[l], ln1_b[l])
        x2, x2p = _cross_attention(x1, mem, B, S, mem_ln_g[l], mem_ln_b[l], w_xq[l], w_xkv[l], w_xo[l],
                                   ln2_g[l], ln2_b[l])
        xc = _moe(x2, x2p, w_router[l], router_bias[l], w_exp_gu[l], w_exp_down[l], w_sh_gu[l],
                  w_sh_down[l], ln3_g[l], ln3_b[l])
    return xc.reshape(B, S, D)
```

```python
import functools
import math

import jax
import jax.numpy as jnp
from jax import lax
from jax.experimental import pallas as pl
from jax.experimental.pallas import tpu as pltpu

F32 = jnp.float32
BF16 = jnp.bfloat16
I32 = jnp.int32
U32 = jnp.uint32

D_MODEL = 1024
MLA_HEADS = 8
QK_NOPE = 64
QK_ROPE = 32
V_HEAD = 64
Q_LORA = 256
KV_LORA = 256
ROPE_THETA = 10000.0
S5_GROUP_CH = 16
S5_WIDTH = 512
S5_GROUPS = 32
S5_STATE = 64
XATTN_HEADS = 4
XATTN_HEAD_DIM = 128
N_EXPERTS = 64
TOP_K = 8
N_EXPERT_GROUPS = 8
TOPK_GROUPS = 4
EXPERT_FF = 256
SHARED_FF = 256
ROUTED_SCALE = 2.5
LN_EPS = 1e-5
RMS_EPS = 1e-6
DEPTH = 1
DEEPNORM_ALPHA = (2.0 * DEPTH) ** 0.25

LANES = 128
HEAD_PAD = 128
ROPE_LO = QK_NOPE
ROPE_HALF = QK_ROPE // 2

TM = 512
TQ = 512
S5_CHUNK = 16
TW = 512
OH_ROWS = 512
PR = 16
PAD_ROWS = PR
SEG_ALIGN = 8
SORT_ROWS = TOP_K * TW + N_EXPERTS * SEG_ALIGN
TS = SORT_ROWS + PAD_ROWS
XW = D_MODEL // 2 + LANES
CR = 256
PPC = CR // PR
VMEM_LIMIT = 48 * 1024 * 1024


def _cparams(*sem):
    return pltpu.CompilerParams(dimension_semantics=sem, vmem_limit_bytes=VMEM_LIMIT)


def _dot(a, b):
    return jnp.dot(a, b, preferred_element_type=F32)


def _dot_nt(a, b, precision=None):
    return lax.dot_general(a, b, (((1,), (1,)), ((), ())), preferred_element_type=F32,
                           precision=precision)


def _dot_exact(a, b):
    return jnp.dot(a, b, preferred_element_type=F32, precision=lax.Precision.HIGHEST)


def _layer_norm(h, g, b):
    mu = jnp.mean(h, axis=-1, keepdims=True)
    c = h - mu
    var = jnp.mean(c * c, axis=-1, keepdims=True)
    return c * lax.rsqrt(var + LN_EPS) * g + b


def _rms_norm(h, g):
    return h * lax.rsqrt(jnp.mean(h * h, axis=-1, keepdims=True) + RMS_EPS) * g


def _full(shape):
    n = len(shape)
    return pl.BlockSpec(shape, lambda *_: (0,) * n)


def _rope_angle_kernel(pos_ref, invf_ref, cos_ref, sin_ref):
    ang = pos_ref[...].astype(F32) * invf_ref[...]
    cos_ref[...] = jnp.cos(ang)
    sin_ref[...] = jnp.sin(ang)


def _rope_tables(positions):
    T = positions.size
    rows = T * ROPE_HALF // LANES
    pos_rep = jnp.repeat(positions.reshape(T), ROPE_HALF).reshape(rows, LANES)
    inv_freq = ROPE_THETA ** (-jnp.arange(0, QK_ROPE, 2, dtype=F32) / QK_ROPE)
    invf = jnp.tile(inv_freq, LANES // ROPE_HALF).reshape(1, LANES)
    cos, sin = pl.pallas_call(
        _rope_angle_kernel,
        out_shape=(jax.ShapeDtypeStruct((rows, LANES), F32),) * 2,
        name="rope_angles",
    )(pos_rep, invf)
    cos = cos.reshape(T, ROPE_HALF)
    sin = sin.reshape(T, ROPE_HALF)
    one = jnp.ones((T, ROPE_LO), F32)
    zero_lo = jnp.zeros((T, ROPE_LO), F32)
    zero_h = jnp.zeros((T, ROPE_HALF), F32)
    tail1 = jnp.ones((T, HEAD_PAD - ROPE_LO - QK_ROPE), F32)
    tail0 = jnp.zeros((T, HEAD_PAD - ROPE_LO - QK_ROPE), F32)
    c_tab = jnp.concatenate([one, cos, cos, tail1], axis=1)
    s_up = jnp.concatenate([zero_lo, -sin, zero_h, tail0], axis=1)
    s_dn = jnp.concatenate([zero_lo, zero_h, sin, tail0], axis=1)
    return c_tab, s_up, s_dn


def _rope(x, c_tab, s_up, s_dn):
    return (x * c_tab + pltpu.roll(x, HEAD_PAD - ROPE_HALF, axis=1) * s_up
            + pltpu.roll(x, ROPE_HALF, axis=1) * s_dn)


def _inproj_kernel(x_ref, c_ref, su_ref, sd_ref, wlat_ref, wu_ref, wgm_ref, wgs_ref, qg_ref, kvg_ref,
                   wuq_ref, wuk_ref, wuv_ref, q_ref, k_ref, v_ref, u_ref, gm_ref, gs_ref):
    xb = x_ref[...].astype(BF16)
    lat = _dot(xb, wlat_ref[...])
    qn = _rms_norm(lat[:, :Q_LORA], qg_ref[...]).astype(BF16)
    kvn = _rms_norm(lat[:, Q_LORA:Q_LORA + KV_LORA], kvg_ref[...]).astype(BF16)
    c_tab, s_up, s_dn = c_ref[...], su_ref[...], sd_ref[...]
    k_rope = _rope(lat[:, Q_LORA + KV_LORA:], c_tab, s_up, s_dn)
    q = _dot(qn, wuq_ref[...])
    k = _dot(kvn, wuk_ref[...])
    scale = (QK_NOPE + QK_ROPE) ** -0.5
    for h in range(MLA_HEADS):
        sl = slice(h * HEAD_PAD, (h + 1) * HEAD_PAD)
        q_ref[:, sl] = (_rope(q[:, sl], c_tab, s_up, s_dn) * scale).astype(BF16)
        k_ref[:, sl] = (k[:, sl] + k_rope).astype(BF16)
    v_ref[...] = _dot(kvn, wuv_ref[...]).astype(BF16)
    u_ref[...] = _dot(xb, wu_ref[...]).astype(BF16)
    gm_ref[...] = _dot(xb, wgm_ref[...]).astype(BF16)
    gs_ref[...] = _dot(xb, wgs_ref[...]).astype(BF16)


def _pad_heads(w, head_w, lo_w):
    K = w.shape[0]
    w = w.reshape(K, MLA_HEADS, head_w)[:, :, :lo_w]
    w = jnp.pad(w, ((0, 0), (0, 0), (0, HEAD_PAD - lo_w)))
    return w.reshape(K, MLA_HEADS * HEAD_PAD)


def _inproj(x2d, tabs, w_in, q_norm_g, kv_norm_g, w_uq, w_ukv):
    T = x2d.shape[0]
    o_rope = Q_LORA + KV_LORA
    o_u = o_rope + QK_ROPE
    o_gm = o_u + S5_WIDTH
    o_gs = o_gm + D_MODEL
    w_rope = jnp.pad(w_in[:, o_rope:o_u], ((0, 0), (ROPE_LO, HEAD_PAD - ROPE_LO - QK_ROPE)))
    w_lat = jnp.concatenate([w_in[:, :o_rope], w_rope], axis=1).astype(BF16)
    w_u = w_in[:, o_u:o_gm].astype(BF16)
    w_gm = w_in[:, o_gm:o_gs].astype(BF16)
    w_gs = w_in[:, o_gs:].astype(BF16)
    wuq = _pad_heads(w_uq, QK_NOPE + QK_ROPE, QK_NOPE + QK_ROPE).astype(BF16)
    kv3 = w_ukv.reshape(KV_LORA, MLA_HEADS, QK_NOPE + V_HEAD)
    wuk = _pad_heads(kv3[:, :, :QK_NOPE].reshape(KV_LORA, -1), QK_NOPE, QK_NOPE).astype(BF16)
    wuv = _pad_heads(kv3[:, :, QK_NOPE:].reshape(KV_LORA, -1), V_HEAD, V_HEAD).astype(BF16)
    HP = MLA_HEADS * HEAD_PAD
    tile = lambda w: pl.BlockSpec((TM, w), lambda i: (i, 0))
    return pl.pallas_call(
        _inproj_kernel,
        grid=(T // TM,),
        in_specs=[tile(D_MODEL), tile(HEAD_PAD), tile(HEAD_PAD), tile(HEAD_PAD),
                  _full(w_lat.shape), _full(w_u.shape), _full(w_gm.shape), _full(w_gs.shape),
                  _full((1, Q_LORA)), _full((1, KV_LORA)),
                  _full(wuq.shape), _full(wuk.shape), _full(wuv.shape)],
        out_specs=[tile(HP), tile(HP), tile(HP), tile(S5_WIDTH), tile(D_MODEL), tile(D_MODEL)],
        out_shape=[jax.ShapeDtypeStruct((T, HP), BF16)] * 3
        + [jax.ShapeDtypeStruct((T, S5_WIDTH), BF16)]
        + [jax.ShapeDtypeStruct((T, D_MODEL), BF16)] * 2,
        compiler_params=_cparams("parallel"),
        name="inproj_mla_prep",
    )(x2d, *tabs, w_lat, w_u, w_gm, w_gs, q_norm_g.reshape(1, -1), kv_norm_g.reshape(1, -1),
      wuq, wuk, wuv)


def _attn_kernel(q_ref, k_ref, v_ref, o_ref):
    qi = pl.program_id(2)
    q = q_ref[0]

    def step(j, carry, masked):
        m, l, acc = carry
        start = pl.multiple_of(j * TQ, TQ)
        k = k_ref[0, pl.ds(start, TQ), :]
        v = v_ref[0, pl.ds(start, TQ), :]
        s = _dot_nt(q, k)
        if masked:
            row = lax.broadcasted_iota(I32, (TQ, TQ), 0)
            col = lax.broadcasted_iota(I32, (TQ, TQ), 1)
            s = jnp.where(col <= row, s, -jnp.inf)
        m_new = jnp.maximum(m, jnp.max(s, axis=-1, keepdims=True))
        p = jnp.exp(s - m_new)
        a = jnp.exp(m - m_new)
        l = a * l + jnp.sum(p, axis=-1, keepdims=True)
        acc = a * acc + _dot(p.astype(BF16), v)
        return m_new, l, acc

    init = (jnp.full((TQ, 1), -jnp.inf, F32), jnp.zeros((TQ, 1), F32), jnp.zeros((TQ, HEAD_PAD), F32))
    carry = lax.fori_loop(0, qi, lambda j, c: step(j, c, False), init)
    m, l, acc = step(qi, carry, True)
    o_ref[0] = (acc / l).astype(BF16)


def _attention(q, k, v, B, S):
    HP = MLA_HEADS * HEAD_PAD
    q, k, v = (a.reshape(B, S, HP) for a in (q, k, v))
    o = pl.pallas_call(
        _attn_kernel,
        grid=(B, MLA_HEADS, S // TQ),
        in_specs=[pl.BlockSpec((1, TQ, HEAD_PAD), lambda b, h, i: (b, i, h)),
                  pl.BlockSpec((1, S, HEAD_PAD), lambda b, h, i: (b, 0, h)),
                  pl.BlockSpec((1, S, HEAD_PAD), lambda b, h, i: (b, 0, h))],
        out_specs=pl.BlockSpec((1, TQ, HEAD_PAD), lambda b, h, i: (b, i, h)),
        out_shape=jax.ShapeDtypeStruct((B, S, HP), BF16),
        compiler_params=_cparams("parallel", "parallel", "arbitrary"),
        name="mla_flash_attention",
    )(q, k, v)
    return o.reshape(B * S, HP)


CH2 = S5_CHUNK * S5_GROUP_CH
P2 = 2 * S5_STATE
SCAN_LEVELS_MAX = 16


def _s5_prep_kernel(n_levels, arow_ref, acol_ref, bt_ref, ct_ref, d_ref,
                    mt_ref, wt_ref, vt_ref, apow_ref, dvec_ref, mt_acc):
    P, H, L = S5_STATE, S5_GROUP_CH, S5_CHUNK

    def powers(ar, ai, dt, n):
        e = jnp.exp(n * (ar * dt))
        ang = n * (ai * dt)
        return e * jnp.cos(ang), e * jnp.sin(ang)

    def zoh_coef(ar, ai, dt):
        pr, pi = powers(ar, ai, dt, 1.0)
        nr, ni = pr - 1.0, pi
        den = ar * ar + ai * ai
        return (nr * ar + ni * ai) / den, (ni * ar - nr * ai) / den

    row = arow_ref[0]
    ar, ai, dt = row[0:1], row[1:2], jnp.exp(row[2:3])
    cr, ci = zoh_coef(ar, ai, dt)
    bt = bt_ref[0]
    btr = jnp.concatenate([bt[:H], bt[:H]], axis=1)
    bti = jnp.concatenate([bt[H:], bt[H:]], axis=1)
    bbr = cr * btr - ci * bti
    bbi = cr * bti + ci * btr
    lane = lax.broadcasted_iota(I32, (1, P2), 1)
    is_re = lane < P
    ridx = lax.broadcasted_iota(I32, (CH2, P2), 0) // H
    pwr, pwi = powers(ar, ai, dt, (L - 1 - ridx).astype(F32))
    rsel = (lax.broadcasted_iota(I32, (CH2, H), 0) % H == lax.broadcasted_iota(I32, (CH2, H), 1)).astype(F32)
    bbr_t = _dot_exact(rsel, bbr)
    bbi_t = _dot_exact(rsel, bbi)
    w_re = pwr * bbr_t - pwi * bbi_t
    w_im = pwr * bbi_t + pwi * bbr_t
    wt_ref[0] = jnp.where(is_re, w_re, w_im).astype(BF16)
    a_r, a_i = powers(ar, ai, dt, float(L))
    sign = jnp.where(is_re, -1.0, 1.0)
    for lvl in range(SCAN_LEVELS_MAX):
        if lvl < n_levels:
            apow_ref[0, lvl:lvl + 1, :] = a_r
            apow_ref[0, SCAN_LEVELS_MAX + lvl:SCAN_LEVELS_MAX + lvl + 1, :] = sign * a_i
            a_r, a_i = a_r * a_r - a_i * a_i, 2.0 * a_r * a_i
        else:
            apow_ref[0, lvl:lvl + 1, :] = jnp.zeros_like(a_r)
            apow_ref[0, SCAN_LEVELS_MAX + lvl:SCAN_LEVELS_MAX + lvl + 1, :] = jnp.zeros_like(a_r)

    col = acol_ref[0]
    arc, aic, dtc = col[:, 0:1], col[:, 1:2], jnp.exp(col[:, 2:3])
    ct = ct_ref[0]
    csel = (lax.broadcasted_iota(I32, (H, CH2), 1) % H == lax.broadcasted_iota(I32, (H, CH2), 0)).astype(F32)
    ctr = _dot_exact(ct[:, :H], csel)
    cti = _dot_exact(ct[:, H:], csel)
    sidx = (lax.broadcasted_iota(I32, (P, CH2), 1) // H).astype(F32)
    pr, pi = powers(arc, aic, dtc, sidx)
    g_r = ctr * pr - cti * pi
    g_i = ctr * pi + cti * pr
    a1r, a1i = powers(arc, aic, dtc, 1.0)
    v_r = g_r * a1r - g_i * a1i
    v_i = g_r * a1i + g_i * a1r
    vt_ref[0, :P, :] = v_r.astype(BF16)
    vt_ref[0, P:, :] = (-v_i).astype(BF16)
    bbr64, bbi64 = bbr[:, :P], bbi[:, :P]
    kt = _dot_exact(bbr64, g_r) - _dot_exact(bbi64, g_i)
    lane2 = lax.broadcasted_iota(I32, (H, CH2), 1)
    for r in range(L):
        blk = kt if r == 0 else jnp.where(lane2 >= r * H, pltpu.roll(kt, r * H, axis=1), 0.0)
        mt_acc[r * H:(r + 1) * H, :] = blk
    mt_ref[0] = mt_acc[...].astype(BF16)
    dvec_ref[0] = _dot_exact(d_ref[0], csel)


def _s5_prep(n_levels, s5_a_re, s5_a_im, s5_log_dt, s5_b_re, s5_b_im, s5_c_re, s5_c_im, s5_d):
    G, P, H = S5_GROUPS, S5_STATE, S5_GROUP_CH
    ldt = jnp.broadcast_to(s5_log_dt[:, None], (G, P))
    arow = jnp.stack([s5_a_re, s5_a_im, ldt], axis=1)
    arow = jnp.concatenate([arow, arow], axis=2)
    acol = jnp.stack([s5_a_re, s5_a_im, ldt], axis=2)
    bt = jnp.concatenate([s5_b_re.transpose(0, 2, 1), s5_b_im.transpose(0, 2, 1)], axis=1)
    ct = jnp.concatenate([s5_c_re.transpose(0, 2, 1), s5_c_im.transpose(0, 2, 1)], axis=2)
    d = s5_d.reshape(G, 1, H)
    blk = lambda *s: pl.BlockSpec((1,) + s, lambda g: (g, 0, 0))
    return pl.pallas_call(
        functools.partial(_s5_prep_kernel, n_levels),
        grid=(G,),
        in_specs=[blk(3, P2), blk(P, 3), blk(2 * H, P), blk(P, 2 * H), blk(1, H)],
        out_specs=[blk(CH2, CH2), blk(CH2, P2), blk(P2, CH2), blk(2 * SCAN_LEVELS_MAX, P2), blk(1, CH2)],
        out_shape=[jax.ShapeDtypeStruct((G, CH2, CH2), BF16),
                   jax.ShapeDtypeStruct((G, CH2, P2), BF16),
                   jax.ShapeDtypeStruct((G, P2, CH2), BF16),
                   jax.ShapeDtypeStruct((G, 2 * SCAN_LEVELS_MAX, P2), F32),
                   jax.ShapeDtypeStruct((G, 1, CH2), F32)],
        scratch_shapes=[pltpu.VMEM((CH2, CH2), F32)],
        compiler_params=_cparams("parallel"),
        name="s5_discretise",
    )(arow, acol, bt, ct, d)


def _gelu_tanh(y):
    return 0.5 * y * (1.0 + jnp.tanh(math.sqrt(2.0 / math.pi) * (y + 0.044715 * (y * y * y))))


def _s5_scan_kernel(n_levels, batch, u_ref, mt_ref, wt_ref, vt_ref, apow_ref, dvec_ref, y_ref):
    u = u_ref[0]
    rows = u.shape[0]
    x = _dot(u, wt_ref[0])
    ridx = lax.broadcasted_iota(I32, (rows, P2), 0)
    apow = apow_ref[0]
    for lvl in range(n_levels):
        sh = batch * (1 << lvl)
        prev = jnp.where(ridx >= sh, pltpu.roll(x, sh, axis=0), 0.0)
        a_r = apow[lvl:lvl + 1]
        a_i = apow[SCAN_LEVELS_MAX + lvl:SCAN_LEVELS_MAX + lvl + 1]
        x = x + a_r * prev + a_i * pltpu.roll(prev, S5_STATE, axis=1)
    x_in = jnp.where(ridx >= batch, pltpu.roll(x, batch, axis=0), 0.0)
    y = _dot(u, mt_ref[0]) + _dot(x_in.astype(BF16), vt_ref[0]) + dvec_ref[0] * u.astype(F32)
    y_ref[0] = _gelu_tanh(y).astype(BF16)


def _s5_branch(u, B, S, s5_params):
    G, H, L = S5_GROUPS, S5_GROUP_CH, S5_CHUNK
    nc = S // L
    n_levels = max(1, (nc - 1).bit_length())
    assert n_levels <= SCAN_LEVELS_MAX
    mt, wt, vt, apow, dvec = _s5_prep(n_levels, *s5_params)
    rows = nc * B
    ug = u.reshape(B, nc, L, G, H).transpose(3, 1, 0, 2, 4).reshape(G, rows, CH2)
    blk = lambda *s: pl.BlockSpec((1,) + s, lambda g: (g, 0, 0))
    yg = pl.pallas_call(
        functools.partial(_s5_scan_kernel, n_levels, B),
        grid=(G,),
        in_specs=[blk(rows, CH2), blk(CH2, CH2), blk(CH2, P2), blk(P2, CH2),
                  blk(2 * SCAN_LEVELS_MAX, P2), blk(1, CH2)],
        out_specs=blk(rows, CH2),
        out_shape=jax.ShapeDtypeStruct((G, rows, CH2), BF16),
        compiler_params=_cparams("parallel"),
        name="s5_chunk_scan",
    )(ug, mt, wt, vt, apow, dvec)
    return yg.reshape(G, nc, B, L, H).transpose(2, 1, 3, 0, 4).reshape(B * S, S5_WIDTH)


def _merge_kernel(x_ref, o_ref, ys_ref, gm_ref, gs_ref, wmo_ref, wglu_ref, wout_ref, g_ref, b_ref, x1_ref):
    y_mla = _dot(o_ref[...], wmo_ref[...])
    z = _dot(ys_ref[...], wglu_ref[...])
    y_s5 = z[:, :D_MODEL] * jax.nn.sigmoid(z[:, D_MODEL:])
    merged = (jax.nn.sigmoid(gm_ref[...].astype(F32)) * y_mla
              + jax.nn.sigmoid(gs_ref[...].astype(F32)) * y_s5)
    mix = _dot(merged.astype(BF16), wout_ref[...])
    x1_ref[...] = _layer_norm(DEEPNORM_ALPHA * x_ref[...] + mix, g_ref[...], b_ref[...])


def _merge(x2d, o, ys, gm, gs, w_mla_o, w_s5_glu, w_out, ln_g, ln_b):
    T = x2d.shape[0]
    HP = MLA_HEADS * HEAD_PAD
    wmo = jnp.pad(w_mla_o.reshape(MLA_HEADS, V_HEAD, D_MODEL), ((0, 0), (0, HEAD_PAD - V_HEAD), (0, 0)))
    wmo = wmo.reshape(HP, D_MODEL).astype(BF16)
    tile = lambda w: pl.BlockSpec((TM, w), lambda i: (i, 0))
    return pl.pallas_call(
        _merge_kernel,
        grid=(T // TM,),
        in_specs=[tile(D_MODEL), tile(HP), tile(S5_WIDTH), tile(D_MODEL), tile(D_MODEL),
                  _full((HP, D_MODEL)), _full((S5_WIDTH, 2 * D_MODEL)), _full((D_MODEL, D_MODEL)),
                  _full((1, D_MODEL)), _full((1, D_MODEL))],
        out_specs=tile(D_MODEL),
        out_shape=jax.ShapeDtypeStruct((T, D_MODEL), F32),
        compiler_params=_cparams("parallel"),
        name="merge_outproj_ln1",
    )(x2d, o, ys, gm, gs, wmo, w_s5_glu.astype(BF16), w_out.astype(BF16),
      ln_g.reshape(1, -1), ln_b.reshape(1, -1))


def _memkv_kernel(mem_ref, g_ref, b_ref, w_ref, k_ref, v_ref):
    m = _layer_norm(mem_ref[0], g_ref[...], b_ref[...]).astype(BF16)
    kv = _dot(m, w_ref[...])
    hd = XATTN_HEADS * XATTN_HEAD_DIM
    k_ref[0] = kv[:, :hd].astype(BF16)
    v_ref[0] = kv[:, hd:].astype(BF16)


def _xattn_kernel(x1_ref, k_ref, v_ref, wq_ref, wo_ref, g_ref, b_ref, x2_ref):
    x1 = x1_ref[0]
    q = (_dot(x1.astype(BF16), wq_ref[...]) * (XATTN_HEAD_DIM ** -0.5)).astype(BF16)
    k = k_ref[0]
    v = v_ref[0]
    outs = []
    for h in range(XATTN_HEADS):
        sl = slice(h * XATTN_HEAD_DIM, (h + 1) * XATTN_HEAD_DIM)
        s = _dot_nt(q[:, sl], k[:, sl])
        p = jnp.exp(s - jnp.max(s, axis=-1, keepdims=True))
        o = _dot(p.astype(BF16), v[:, sl]) / jnp.sum(p, axis=-1, keepdims=True)
        outs.append(o.astype(BF16))
    xa = _dot(jnp.concatenate(outs, axis=1), wo_ref[...])
    x2 = _layer_norm(DEEPNORM_ALPHA * x1 + xa, g_ref[...], b_ref[...])
    x2_ref[0] = x2


def _cross_attention(x1, mem, B, S, mem_ln_g, mem_ln_b, w_xq, w_xkv, w_xo, ln_g, ln_b):
    M = mem.shape[1]
    hd = XATTN_HEADS * XATTN_HEAD_DIM
    row = lambda a: a.reshape(1, -1)
    k, v = pl.pallas_call(
        _memkv_kernel,
        grid=(B,),
        in_specs=[pl.BlockSpec((1, M, D_MODEL), lambda b: (b, 0, 0)), _full((1, D_MODEL)), _full((1, D_MODEL)),
                  _full((D_MODEL, 2 * hd))],
        out_specs=[pl.BlockSpec((1, M, hd), lambda b: (b, 0, 0))] * 2,
        out_shape=[jax.ShapeDtypeStruct((B, M, hd), BF16)] * 2,
        compiler_params=_cparams("parallel"),
        name="memory_kv",
    )(mem, row(mem_ln_g), row(mem_ln_b), w_xkv.astype(BF16))
    tile = lambda w: pl.BlockSpec((1, TM, w), lambda b, i: (b, i, 0))
    x2 = pl.pallas_call(
        _xattn_kernel,
        grid=(B, S // TM),
        in_specs=[tile(D_MODEL),
                  pl.BlockSpec((1, M, hd), lambda b, i: (b, 0, 0)), pl.BlockSpec((1, M, hd), lambda b, i: (b, 0, 0)),
                  _full((D_MODEL, hd)), _full((hd, D_MODEL)), _full((1, D_MODEL)), _full((1, D_MODEL))],
        out_specs=tile(D_MODEL),
        out_shape=jax.ShapeDtypeStruct((B, S, D_MODEL), F32),
        compiler_params=_cparams("parallel", "parallel"),
        name="cross_attention_ln2",
    )(x1.reshape(B, S, D_MODEL), k, v, w_xq.astype(BF16), w_xo.astype(BF16), row(ln_g), row(ln_b))
    return x2.reshape(B * S, D_MODEL)


def _route_sort_kernel(x_ref, w_ref, bias_ref, xs_ref, pos_ref, cnt_ref, off_ref):
    E, per = N_EXPERTS, N_EXPERTS // N_EXPERT_GROUPS
    tm = x_ref.shape[0]
    x = x_ref[...]
    logits = _dot_nt(w_ref[...], x, precision=lax.Precision.HIGHEST)
    scores = jax.nn.sigmoid(logits)
    sel = scores + bias_ref[...]
    neg = -jnp.inf
    i8 = lax.broadcasted_iota(I32, (per, tm), 0)
    gscore = []
    for g in range(N_EXPERT_GROUPS):
        blk = sel[g * per:(g + 1) * per]
        m1 = jnp.max(blk, axis=0, keepdims=True)
        i1 = jnp.min(jnp.where(blk == m1, i8, per), axis=0, keepdims=True)
        m2 = jnp.max(jnp.where(i8 == i1, neg, blk), axis=0, keepdims=True)
        gscore.append(m1 + m2)
    blocks = []
    for g in range(N_EXPERT_GROUPS):
        ahead = jnp.zeros((1, tm), I32)
        for o in range(N_EXPERT_GROUPS):
            if o == g:
                continue
            before = (gscore[o] >= gscore[g]) if o < g else (gscore[o] > gscore[g])
            ahead = ahead + before.astype(I32)
        blocks.append(jnp.where(ahead < TOPK_GROUPS, sel[g * per:(g + 1) * per], neg))
    cur = jnp.concatenate(blocks, axis=0)
    ie = lax.broadcasted_iota(I32, (E, tm), 0)
    picked = jnp.zeros((E, tm), F32)
    idxs = []
    for _ in range(TOP_K):
        m = jnp.max(cur, axis=0, keepdims=True)
        idx = jnp.min(jnp.where(cur == m, ie, E), axis=0, keepdims=True)
        hit = ie == idx
        picked = jnp.where(hit, 1.0, picked)
        cur = jnp.where(hit, neg, cur)
        idxs.append(idx)
    wsel = scores * picked
    wnorm = wsel / jnp.sum(wsel, axis=0, keepdims=True) * ROUTED_SCALE
    pb = picked.astype(BF16)
    tri = (lax.broadcasted_iota(I32, (tm, tm), 0) <= lax.broadcasted_iota(I32, (tm, tm), 1)).astype(BF16)
    incl = _dot(pb, tri)
    cnt_col = jnp.sum(picked, axis=1, keepdims=True)
    lower = (lax.broadcasted_iota(I32, (E, E), 1) < lax.broadcasted_iota(I32, (E, E), 0)).astype(F32)
    al_col = jnp.floor((cnt_col + (SEG_ALIGN - 1.0)) * (1.0 / SEG_ALIGN)) * SEG_ALIGN
    off_col = _dot_exact(lower, jnp.broadcast_to(al_col, (E, LANES)))[:, 0:1]
    posmat = off_col + incl - 1.0
    poss = []
    for kk in range(TOP_K):
        p = jnp.sum(jnp.where(ie == idxs[kk], posmat, 0.0), axis=0, keepdims=True).astype(I32)
        pos_ref[kk:kk + 1, :] = p
        poss.append(p)
    cnt_ref[0] = jnp.broadcast_to(cnt_col, (E, LANES)).astype(I32)
    off_ref[0] = jnp.broadcast_to(off_col, (E, LANES)).astype(I32)
    cnt_row = _dot_nt(jnp.ones((8, tm), BF16), pb)
    upper = (lax.broadcasted_iota(I32, (E, E), 0) < lax.broadcasted_iota(I32, (E, E), 1)).astype(F32)
    al_row = jnp.floor((cnt_row + (SEG_ALIGN - 1.0)) * (1.0 / SEG_ALIGN)) * SEG_ALIGN
    off_row = _dot_exact(al_row, upper)[0:1]
    end_row = off_row + cnt_row[0:1]

    xb = x.astype(BF16)
    half = D_MODEL // 2

    def build(c, carry):
        r0 = pl.multiple_of(c * OH_ROWS, OH_ROWS)
        j = lax.broadcasted_iota(I32, (OH_ROWS, tm), 0) + r0
        oh = jnp.zeros((OH_ROWS, tm), F32)
        for p in poss:
            oh = jnp.where(j == p, 1.0, oh)
        rows = _dot(oh.astype(BF16), xb)
        bits = pltpu.bitcast(rows, U32)
        xs_ref[0, pl.ds(r0, OH_ROWS), 0:half] = bits[:, half:] | (bits[:, :half] >> 16)
        gw = _dot_nt(oh, wnorm, precision=lax.Precision.HIGHEST)
        jr = (lax.broadcasted_iota(I32, (OH_ROWS, E), 0) + r0).astype(F32)
        mine = jnp.where(jr >= off_row, jnp.where(jr < end_row, gw, 0.0), 0.0)
        wrow = jnp.sum(mine, axis=1, keepdims=True)
        xs_ref[0, pl.ds(r0, OH_ROWS), half:] = pltpu.bitcast(jnp.broadcast_to(wrow, (OH_ROWS, LANES)), U32)
        return carry

    lax.fori_loop(0, SORT_ROWS // OH_ROWS, build, 0)
    xs_ref[0, SORT_ROWS:, :] = jnp.zeros((PAD_ROWS, XW), U32)


def _route_sort(x2, w_router, router_bias):
    T = x2.shape[0]
    E = N_EXPERTS
    nt = T // TW
    seg = pl.BlockSpec((1, E, LANES), lambda i: (i, 0, 0))
    return pl.pallas_call(
        _route_sort_kernel,
        grid=(nt,),
        in_specs=[pl.BlockSpec((TW, D_MODEL), lambda i: (i, 0)), _full((E, D_MODEL)), _full((E, 1))],
        out_specs=[pl.BlockSpec((1, TS, XW), lambda i: (i, 0, 0)),
                   pl.BlockSpec((TOP_K, TW), lambda i: (0, i)), seg, seg],
        out_shape=[jax.ShapeDtypeStruct((nt, TS, XW), U32), jax.ShapeDtypeStruct((TOP_K, T), I32),
                   jax.ShapeDtypeStruct((nt, E, LANES), I32), jax.ShapeDtypeStruct((nt, E, LANES), I32)],
        compiler_params=_cparams("parallel"),
        name="route_local_sort",
    )(x2, w_router.T, router_bias.reshape(E, 1))


def _plan_kernel(nt, n_pieces_max, n_chunks_max, cnt_ref, off_ref, pa_ref, ce_ref, nch_ref):
    def clear(p, c):
        pa_ref[p] = -SEG_ALIGN
        return c

    lax.fori_loop(0, n_pieces_max, clear, 0)

    def per_expert(e, carry):
        p0, g = carry

        def per_tile(i, p):
            n = cnt_ref[i * N_EXPERTS + e]
            base = i * TS + off_ref[i * N_EXPERTS + e]
            n_pc = (n + PR - 1) // PR

            def put(q, c):
                pa_ref[p + q] = base + q * PR
                return c

            lax.fori_loop(0, n_pc, put, 0)
            return p + n_pc

        p1 = lax.fori_loop(0, nt, per_tile, p0)
        n_ch = (p1 - p0 + PPC - 1) // PPC

        def mark(c, z):
            ce_ref[g + c] = e
            return z

        lax.fori_loop(0, n_ch, mark, 0)
        return p0 + n_ch * PPC, g + n_ch

    _, g = lax.fori_loop(0, N_EXPERTS, per_expert, (jnp.int32(0), jnp.int32(0)))
    nch_ref[0] = g

    def tail(c, z):
        ce_ref[c] = N_EXPERTS - 1
        return z

    lax.fori_loop(g, n_chunks_max, tail, 0)


def _plan(cnt, off, nt, n_pieces_max, n_chunks_max):
    smem = pl.BlockSpec(memory_space=pltpu.SMEM)
    return pl.pallas_call(
        functools.partial(_plan_kernel, nt, n_pieces_max, n_chunks_max),
        in_specs=[smem, smem],
        out_specs=[smem, smem, smem],
        out_shape=[jax.ShapeDtypeStruct((n_pieces_max,), I32), jax.ShapeDtypeStruct((n_chunks_max,), I32),
                   jax.ShapeDtypeStruct((1,), I32)],
        name="moe_plan",
    )(cnt, off)


def _unpack_bf16_pair(w):
    lo = pltpu.bitcast(w << 16, F32).astype(BF16)
    hi = pltpu.bitcast(w & jnp.uint32(0xFFFF0000), F32).astype(BF16)
    return lo, hi


def _pack_bf16_pair(lo, hi):
    lo_bits = pltpu.bitcast(lo.astype(BF16).astype(F32), U32) >> 16
    hi_bits = pltpu.bitcast(hi.astype(BF16).astype(F32), U32)
    return hi_bits | lo_bits


def _expert_kernel(ce_ref, pa_ref, nch_ref, tot_ref, xs_hbm, wgu_ref, wdn_ref, y_hbm,
                   xbuf, ybuf, zbuf, wgu_b, wdn_b, gsem, wsem, zsem):
    g = pl.program_id(0)
    n = nch_ref[0]
    slot = g % 2
    half = D_MODEL // 2

    def pieces(c, fn):
        for jp in range(PPC):
            a = pl.multiple_of(pa_ref[c * PPC + jp], SEG_ALIGN)

            @pl.when(a >= 0)
            def _():
                fn(a, jp)

    def gather_copy(s, a, jp):
        return pltpu.make_async_copy(xs_hbm.at[pl.ds(a, PR), :], xbuf.at[s, pl.ds(jp * PR, PR), :], gsem.at[s])

    def write_copy(s, a, jp):
        return pltpu.make_async_copy(ybuf.at[s, pl.ds(jp * PR, PR), :], y_hbm.at[pl.ds(a, PR), :], wsem.at[s])

    @pl.when(g == 0)
    def _():
        xbuf[...] = jnp.zeros_like(xbuf)
        pieces(0, lambda a, jp: gather_copy(0, a, jp).start())
        zbuf[...] = jnp.zeros_like(zbuf)

        def zero_copy(row):
            return pltpu.make_async_copy(zbuf, y_hbm.at[pl.ds(pl.multiple_of(row, SEG_ALIGN), SEG_ALIGN), :], zsem)

        def tail_of(i):
            return i * TS + tot_ref[i], (TS - tot_ref[i]) // SEG_ALIGN

        def zero_start(i, c):
            base, cnt8 = tail_of(i)
            lax.fori_loop(0, cnt8, lambda q, z: (zero_copy(base + q * SEG_ALIGN).start(), z)[1], 0)
            return c

        def zero_wait(i, c):
            base, cnt8 = tail_of(i)
            lax.fori_loop(0, cnt8, lambda q, z: (zero_copy(base + q * SEG_ALIGN).wait(), z)[1], 0)
            return c

        lax.fori_loop(0, tot_ref.shape[0], zero_start, 0)
        lax.fori_loop(0, tot_ref.shape[0], zero_wait, 0)

    @pl.when(g + 1 < n)
    def _():
        pieces(g + 1, lambda a, jp: gather_copy(1 - slot, a, jp).start())

    @pl.when(g < n)
    def _():
        e_prev = ce_ref[jnp.maximum(g - 1, 0)]

        @pl.when((g == 0) | (ce_ref[g] != e_prev))
        def _():
            wgu_b[...] = wgu_ref[0].astype(BF16)
            wdn_b[...] = wdn_ref[0].astype(BF16)

        pieces(g, lambda a, jp: gather_copy(slot, a, jp).wait())
        xw = xbuf[slot]
        lo, hi = _unpack_bf16_pair(xw[:, :half])
        h = _dot(lo, wgu_b[:half, :]) + _dot(hi, wgu_b[half:, :])
        gate, up = h[:, :EXPERT_FF], h[:, EXPERT_FF:]
        act = (gate * jax.nn.sigmoid(gate) * up).astype(BF16)
        y = _dot(act, wdn_b[...])
        w_row = pltpu.bitcast(xw[:, half:], F32)
        y = y * jnp.concatenate([w_row] * (D_MODEL // LANES), axis=1)

        @pl.when(g >= 1)
        def _():
            pieces(g - 1, lambda a, jp: write_copy(1 - slot, a, jp).wait())

        ybuf[slot] = _pack_bf16_pair(y[:, :half], y[:, half:])
        pieces(g, lambda a, jp: write_copy(slot, a, jp).start())

    @pl.when(g == pl.num_programs(0) - 1)
    def _():
        last = n - 1
        pieces(last, lambda a, jp: write_copy(last % 2, a, jp).wait())


def _experts(ce, pa, nch, tot, xs, w_exp_gu, w_exp_down):
    n_rows = xs.shape[0]
    half = D_MODEL // 2
    grid_spec = pltpu.PrefetchScalarGridSpec(
        num_scalar_prefetch=4,
        grid=(ce.shape[0],),
        in_specs=[pl.BlockSpec(memory_space=pl.ANY),
                  pl.BlockSpec((1, D_MODEL, 2 * EXPERT_FF), lambda g, ce, pa, nch, tot: (ce[g], 0, 0)),
                  pl.BlockSpec((1, EXPERT_FF, D_MODEL), lambda g, ce, pa, nch, tot: (ce[g], 0, 0))],
        out_specs=pl.BlockSpec(memory_space=pl.ANY),
        scratch_shapes=[pltpu.VMEM((2, CR, XW), U32), pltpu.VMEM((2, CR, half), U32),
                        pltpu.VMEM((SEG_ALIGN, half), U32),
                        pltpu.VMEM((D_MODEL, 2 * EXPERT_FF), BF16), pltpu.VMEM((EXPERT_FF, D_MODEL), BF16),
                        pltpu.SemaphoreType.DMA((2,)), pltpu.SemaphoreType.DMA((2,)),
                        pltpu.SemaphoreType.DMA(())],
    )
    return pl.pallas_call(
        _expert_kernel,
        grid_spec=grid_spec,
        out_shape=jax.ShapeDtypeStruct((n_rows, half), U32),
        compiler_params=_cparams("arbitrary"),
        name="moe_grouped_swiglu",
    )(ce, pa, nch, tot, xs, w_exp_gu, w_exp_down)


def _combine_kernel(tot_ref, y_ref, pos_ref, x2_ref, wsg_ref, wsd_ref, g_ref, b_ref, out_ref, acc_lo, acc_hi):
    tm = x2_ref.shape[0]
    n_sorted = tot_ref[pl.program_id(0)]
    x2 = x2_ref[...]
    h = _dot(x2.astype(BF16), wsg_ref[...])
    gate, up = h[:, :SHARED_FF], h[:, SHARED_FF:]
    shared = _dot((gate * jax.nn.sigmoid(gate) * up).astype(BF16), wsd_ref[...])
    pos = pos_ref[...]
    acc_lo[...] = jnp.zeros_like(acc_lo)
    acc_hi[...] = jnp.zeros_like(acc_hi)

    def gather_back(c, carry):
        r0 = pl.multiple_of(c * OH_ROWS, OH_ROWS)
        j = lax.broadcasted_iota(I32, (tm, OH_ROWS), 1) + r0
        oh = jnp.zeros((tm, OH_ROWS), F32)
        for kk in range(TOP_K):
            oh = jnp.where(j == pos[:, kk:kk + 1], 1.0, oh)
        ohb = oh.astype(BF16)
        yw = y_ref[0, pl.ds(r0, OH_ROWS), :]
        row = lax.broadcasted_iota(I32, yw.shape, 0) + r0
        lo, hi = _unpack_bf16_pair(jnp.where(row < n_sorted, yw, jnp.uint32(0)))
        acc_lo[...] += _dot(ohb, lo)
        acc_hi[...] += _dot(ohb, hi)
        return carry

    lax.fori_loop(0, SORT_ROWS // OH_ROWS, gather_back, 0)
    ff = shared + jnp.concatenate([acc_lo[...], acc_hi[...]], axis=1)
    out_ref[...] = _layer_norm(DEEPNORM_ALPHA * x2 + ff, g_ref[...], b_ref[...])


def _combine(tot, y, pos_t, x2, w_sh_gu, w_sh_down, ln_g, ln_b):
    T = x2.shape[0]
    nt = T // TW
    half = D_MODEL // 2
    tile = lambda w: pl.BlockSpec((TW, w), lambda i: (i, 0))
    return pl.pallas_call(
        _combine_kernel,
        grid=(nt,),
        in_specs=[pl.BlockSpec(memory_space=pltpu.SMEM),
                  pl.BlockSpec((1, TS, half), lambda i: (i, 0, 0)), tile(TOP_K), tile(D_MODEL),
                  _full((D_MODEL, 2 * SHARED_FF)), _full((SHARED_FF, D_MODEL)),
                  _full((1, D_MODEL)), _full((1, D_MODEL))],
        out_specs=tile(D_MODEL),
        out_shape=jax.ShapeDtypeStruct((T, D_MODEL), F32),
        scratch_shapes=[pltpu.VMEM((TW, half), F32), pltpu.VMEM((TW, half), F32)],
        compiler_params=_cparams("parallel"),
        name="moe_combine_shared_ln3",
    )(tot, y.reshape(nt, TS, half), pos_t, x2, w_sh_gu.astype(BF16), w_sh_down.astype(BF16),
      ln_g.reshape(1, -1), ln_b.reshape(1, -1))


def _moe(x2, w_router, router_bias, w_exp_gu, w_exp_down, w_sh_gu, w_sh_down, ln_g, ln_b):
    T = x2.shape[0]
    E = N_EXPERTS
    nt = T // TW
    xs, pos, cnt, off = _route_sort(x2, w_router, router_bias)
    n_pieces_max = (T * TOP_K + nt * E * (PR - 1)) // PR
    n_chunks_max = n_pieces_max // PPC + E
    pa, ce, nch = _plan(cnt[:, :, 0].reshape(nt * E), off[:, :, 0].reshape(nt * E), nt,
                        n_chunks_max * PPC, n_chunks_max)
    last_al = (cnt[:, E - 1, 0] + SEG_ALIGN - 1) // SEG_ALIGN * SEG_ALIGN
    tot = off[:, E - 1, 0] + last_al
    y = _experts(ce, pa, nch, tot, xs.reshape(nt * TS, XW), w_exp_gu, w_exp_down)
    return _combine(tot, y, pos.T, x2, w_sh_gu, w_sh_down, ln_g, ln_b)


def kernel(x, mem, positions, w_in, q_norm_g, kv_norm_g, w_uq, w_ukv, w_mla_o, s5_a_re, s5_a_im, s5_log_dt, s5_b_re, s5_b_im, s5_c_re, s5_c_im, s5_d, w_s5_glu, w_out, ln1_g, ln1_b, mem_ln_g, mem_ln_b, w_xq, w_xkv, w_xo, ln2_g, ln2_b, w_router, router_bias, w_exp_gu, w_exp_down, w_sh_gu, w_sh_down, ln3_g, ln3_b):
    B, S, D = x.shape
    assert D == D_MODEL and S % TQ == 0 and (B * S) % TM == 0 and S % S5_CHUNK == 0
    xc = x.reshape(B * S, D)
    for l in range(w_in.shape[0]):
        tabs = _rope_tables(positions)
        q, k, v, u, gm, gs = _inproj(xc, tabs, w_in[l], q_norm_g[l], kv_norm_g[l], w_uq[l], w_ukv[l])
        o = _attention(q, k, v, B, S)
        ys = _s5_branch(u, B, S, (s5_a_re[l], s5_a_im[l], s5_log_dt[l], s5_b_re[l], s5_b_im[l],
                                  s5_c_re[l], s5_c_im[l], s5_d[l]))
        x1 = _merge(xc, o, ys, gm, gs, w_mla_o[l], w_s5_glu[l], w_out[l], ln1_g[l], ln1_b[l])
        x2 = _cross_attention(x1, mem, B, S, mem_ln_g[l], mem_ln_b[l], w_xq[l], w_xkv[l], w_xo[l],
                                   ln2_g[l], ln2_b[l])
        xc = _moe(x2, w_router[l], router_bias[l], w_exp_gu[l], w_exp_down[l], w_sh_gu[l],
                  w_sh_down[l], ln3_g[l], ln3_b[l])
    return xc.reshape(B, S, D)
```

```python
import functools
import math

import jax
import jax.numpy as jnp
from jax import lax
from jax.experimental import pallas as pl
from jax.experimental.pallas import tpu as pltpu

F32 = jnp.float32
BF16 = jnp.bfloat16
I32 = jnp.int32
U32 = jnp.uint32

D_MODEL = 1024
MLA_HEADS = 8
QK_NOPE = 64
QK_ROPE = 32
V_HEAD = 64
Q_LORA = 256
KV_LORA = 256
ROPE_THETA = 10000.0
S5_GROUP_CH = 16
S5_WIDTH = 512
S5_GROUPS = 32
S5_STATE = 64
XATTN_HEADS = 4
XATTN_HEAD_DIM = 128
N_EXPERTS = 64
TOP_K = 8
N_EXPERT_GROUPS = 8
TOPK_GROUPS = 4
EXPERT_FF = 256
SHARED_FF = 256
ROUTED_SCALE = 2.5
LN_EPS = 1e-5
RMS_EPS = 1e-6
DEPTH = 1
DEEPNORM_ALPHA = (2.0 * DEPTH) ** 0.25

LANES = 128
HEAD_PAD = 128
ROPE_LO = QK_NOPE
ROPE_HALF = QK_ROPE // 2

TM = 512
TQ = 512
S5_CHUNK = 16
TW = 512
OH_ROWS = 512
PR = 16
PAD_ROWS = PR
SEG_ALIGN = 8
SORT_ROWS = TOP_K * TW + N_EXPERTS * SEG_ALIGN
TS = SORT_ROWS + PAD_ROWS
XW = D_MODEL // 2 + LANES
CR = 256
PPC = CR // PR
VMEM_LIMIT = 48 * 1024 * 1024


def _cparams(*sem):
    return pltpu.CompilerParams(dimension_semantics=sem, vmem_limit_bytes=VMEM_LIMIT)


def _dot(a, b):
    return jnp.dot(a, b, preferred_element_type=F32)


def _dot_nt(a, b, precision=None):
    return lax.dot_general(a, b, (((1,), (1,)), ((), ())), preferred_element_type=F32,
                           precision=precision)


def _dot_exact(a, b):
    return jnp.dot(a, b, preferred_element_type=F32, precision=lax.Precision.HIGHEST)


def _layer_norm(h, g, b):
    mu = jnp.mean(h, axis=-1, keepdims=True)
    c = h - mu
    var = jnp.mean(c * c, axis=-1, keepdims=True)
    return c * lax.rsqrt(var + LN_EPS) * g + b


def _rms_norm(h, g):
    return h * lax.rsqrt(jnp.mean(h * h, axis=-1, keepdims=True) + RMS_EPS) * g


def _full(shape):
    n = len(shape)
    return pl.BlockSpec(shape, lambda *_: (0,) * n)


def _rope_angle_kernel(pos_ref, invf_ref, cos_ref, sin_ref):
    ang = pos_ref[...].astype(F32) * invf_ref[...]
    cos_ref[...] = jnp.cos(ang)
    sin_ref[...] = jnp.sin(ang)


def _rope_tables(positions):
    T = positions.size
    rows = T * ROPE_HALF // LANES
    pos_rep = jnp.repeat(positions.reshape(T), ROPE_HALF).reshape(rows, LANES)
    inv_freq = ROPE_THETA ** (-jnp.arange(0, QK_ROPE, 2, dtype=F32) / QK_ROPE)
    invf = jnp.tile(inv_freq, LANES // ROPE_HALF).reshape(1, LANES)
    cos, sin = pl.pallas_call(
        _rope_angle_kernel,
        out_shape=(jax.ShapeDtypeStruct((rows, LANES), F32),) * 2,
        name="rope_angles",
    )(pos_rep, invf)
    cos = cos.reshape(T, ROPE_HALF)
    sin = sin.reshape(T, ROPE_HALF)
    one = jnp.ones((T, ROPE_LO), F32)
    zero_lo = jnp.zeros((T, ROPE_LO), F32)
    zero_h = jnp.zeros((T, ROPE_HALF), F32)
    tail1 = jnp.ones((T, HEAD_PAD - ROPE_LO - QK_ROPE), F32)
    tail0 = jnp.zeros((T, HEAD_PAD - ROPE_LO - QK_ROPE), F32)
    c_tab = jnp.concatenate([one, cos, cos, tail1], axis=1)
    s_up = jnp.concatenate([zero_lo, -sin, zero_h, tail0], axis=1)
    s_dn = jnp.concatenate([zero_lo, zero_h, sin, tail0], axis=1)
    return c_tab, s_up, s_dn


def _rope(x, c_tab, s_up, s_dn):
    return (x * c_tab + pltpu.roll(x, HEAD_PAD - ROPE_HALF, axis=1) * s_up
            + pltpu.roll(x, ROPE_HALF, axis=1) * s_dn)


def _inproj_kernel(x_ref, c_ref, su_ref, sd_ref, wlat_ref, wu_ref, wgm_ref, wgs_ref, qg_ref, kvg_ref,
                   wuq_ref, wuk_ref, wuv_ref, q_ref, k_ref, v_ref, u_ref, gm_ref, gs_ref):
    xb = x_ref[...].astype(BF16)
    lat = _dot(xb, wlat_ref[...])
    qn = _rms_norm(lat[:, :Q_LORA], qg_ref[...]).astype(BF16)
    kvn = _rms_norm(lat[:, Q_LORA:Q_LORA + KV_LORA], kvg_ref[...]).astype(BF16)
    c_tab, s_up, s_dn = c_ref[...], su_ref[...], sd_ref[...]
    k_rope = _rope(lat[:, Q_LORA + KV_LORA:], c_tab, s_up, s_dn)
    q = _dot(qn, wuq_ref[...])
    k = _dot(kvn, wuk_ref[...])
    scale = (QK_NOPE + QK_ROPE) ** -0.5
    for h in range(MLA_HEADS):
        sl = slice(h * HEAD_PAD, (h + 1) * HEAD_PAD)
        q_ref[:, sl] = (_rope(q[:, sl], c_tab, s_up, s_dn) * scale).astype(BF16)
        k_ref[:, sl] = (k[:, sl] + k_rope).astype(BF16)
    v_ref[...] = _dot(kvn, wuv_ref[...]).astype(BF16)
    u_ref[...] = _dot(xb, wu_ref[...]).astype(BF16)
    gm_ref[...] = _dot(xb, wgm_ref[...]).astype(BF16)
    gs_ref[...] = _dot(xb, wgs_ref[...]).astype(BF16)


def _pad_heads(w, head_w, lo_w):
    K = w.shape[0]
    w = w.reshape(K, MLA_HEADS, head_w)[:, :, :lo_w]
    w = jnp.pad(w, ((0, 0), (0, 0), (0, HEAD_PAD - lo_w)))
    return w.reshape(K, MLA_HEADS * HEAD_PAD)


def _inproj(x2d, tabs, w_in, q_norm_g, kv_norm_g, w_uq, w_ukv):
    T = x2d.shape[0]
    o_rope = Q_LORA + KV_LORA
    o_u = o_rope + QK_ROPE
    o_gm = o_u + S5_WIDTH
    o_gs = o_gm + D_MODEL
    w_rope = jnp.pad(w_in[:, o_rope:o_u], ((0, 0), (ROPE_LO, HEAD_PAD - ROPE_LO - QK_ROPE)))
    w_lat = jnp.concatenate([w_in[:, :o_rope], w_rope], axis=1).astype(BF16)
    w_u = w_in[:, o_u:o_gm].astype(BF16)
    w_gm = w_in[:, o_gm:o_gs].astype(BF16)
    w_gs = w_in[:, o_gs:].astype(BF16)
    wuq = _pad_heads(w_uq, QK_NOPE + QK_ROPE, QK_NOPE + QK_ROPE).astype(BF16)
    kv3 = w_ukv.reshape(KV_LORA, MLA_HEADS, QK_NOPE + V_HEAD)
    wuk = _pad_heads(kv3[:, :, :QK_NOPE].reshape(KV_LORA, -1), QK_NOPE, QK_NOPE).astype(BF16)
    wuv = _pad_heads(kv3[:, :, QK_NOPE:].reshape(KV_LORA, -1), V_HEAD, V_HEAD).astype(BF16)
    HP = MLA_HEADS * HEAD_PAD
    tile = lambda w: pl.BlockSpec((TM, w), lambda i: (i, 0))
    return pl.pallas_call(
        _inproj_kernel,
        grid=(T // TM,),
        in_specs=[tile(D_MODEL), tile(HEAD_PAD), tile(HEAD_PAD), tile(HEAD_PAD),
                  _full(w_lat.shape), _full(w_u.shape), _full(w_gm.shape), _full(w_gs.shape),
                  _full((1, Q_LORA)), _full((1, KV_LORA)),
                  _full(wuq.shape), _full(wuk.shape), _full(wuv.shape)],
        out_specs=[tile(HP), tile(HP), tile(HP), tile(S5_WIDTH), tile(D_MODEL), tile(D_MODEL)],
        out_shape=[jax.ShapeDtypeStruct((T, HP), BF16)] * 3
        + [jax.ShapeDtypeStruct((T, S5_WIDTH), BF16)]
        + [jax.ShapeDtypeStruct((T, D_MODEL), BF16)] * 2,
        compiler_params=_cparams("parallel"),
        name="inproj_mla_prep",
    )(x2d, *tabs, w_lat, w_u, w_gm, w_gs, q_norm_g.reshape(1, -1), kv_norm_g.reshape(1, -1),
      wuq, wuk, wuv)


def _attn_kernel(q_ref, k_ref, v_ref, o_ref):
    qi = pl.program_id(2)
    q = q_ref[0]

    def step(j, carry, masked):
        m, l, acc = carry
        start = pl.multiple_of(j * TQ, TQ)
        k = k_ref[0, pl.ds(start, TQ), :]
        v = v_ref[0, pl.ds(start, TQ), :]
        s = _dot_nt(q, k)
        if masked:
            row = lax.broadcasted_iota(I32, (TQ, TQ), 0)
            col = lax.broadcasted_iota(I32, (TQ, TQ), 1)
            s = jnp.where(col <= row, s, -jnp.inf)
        m_new = jnp.maximum(m, jnp.max(s, axis=-1, keepdims=True))
        p = jnp.exp(s - m_new)
        a = jnp.exp(m - m_new)
        l = a * l + jnp.sum(p, axis=-1, keepdims=True)
        acc = a * acc + _dot(p.astype(BF16), v)
        return m_new, l, acc

    init = (jnp.full((TQ, 1), -jnp.inf, F32), jnp.zeros((TQ, 1), F32), jnp.zeros((TQ, HEAD_PAD), F32))
    carry = lax.fori_loop(0, qi, lambda j, c: step(j, c, False), init)
    m, l, acc = step(qi, carry, True)
    o_ref[0] = (acc / l).astype(BF16)


def _attention(q, k, v, B, S):
    HP = MLA_HEADS * HEAD_PAD
    q, k, v = (a.reshape(B, S, HP) for a in (q, k, v))
    o = pl.pallas_call(
        _attn_kernel,
        grid=(B, MLA_HEADS, S // TQ),
        in_specs=[pl.BlockSpec((1, TQ, HEAD_PAD), lambda b, h, i: (b, i, h)),
                  pl.BlockSpec((1, S, HEAD_PAD), lambda b, h, i: (b, 0, h)),
                  pl.BlockSpec((1, S, HEAD_PAD), lambda b, h, i: (b, 0, h))],
        out_specs=pl.BlockSpec((1, TQ, HEAD_PAD), lambda b, h, i: (b, i, h)),
        out_shape=jax.ShapeDtypeStruct((B, S, HP), BF16),
        compiler_params=_cparams("parallel", "parallel", "arbitrary"),
        name="mla_flash_attention",
    )(q, k, v)
    return o.reshape(B * S, HP)


CH2 = S5_CHUNK * S5_GROUP_CH
P2 = 2 * S5_STATE
SCAN_LEVELS_MAX = 16


def _s5_prep_kernel(n_levels, arow_ref, acol_ref, bt_ref, ct_ref, d_ref,
                    mt_ref, wt_ref, vt_ref, apow_ref, dvec_ref, mt_acc):
    P, H, L = S5_STATE, S5_GROUP_CH, S5_CHUNK

    def powers(ar, ai, dt, n):
        e = jnp.exp(n * (ar * dt))
        ang = n * (ai * dt)
        return e * jnp.cos(ang), e * jnp.sin(ang)

    def zoh_coef(ar, ai, dt):
        pr, pi = powers(ar, ai, dt, 1.0)
        nr, ni = pr - 1.0, pi
        den = ar * ar + ai * ai
        return (nr * ar + ni * ai) / den, (ni * ar - nr * ai) / den

    row = arow_ref[0]
    ar, ai, dt = row[0:1], row[1:2], jnp.exp(row[2:3])
    cr, ci = zoh_coef(ar, ai, dt)
    bt = bt_ref[0]
    btr = jnp.concatenate([bt[:H], bt[:H]], axis=1)
    bti = jnp.concatenate([bt[H:], bt[H:]], axis=1)
    bbr = cr * btr - ci * bti
    bbi = cr * bti + ci * btr
    lane = lax.broadcasted_iota(I32, (1, P2), 1)
    is_re = lane < P
    ridx = lax.broadcasted_iota(I32, (CH2, P2), 0) // H
    pwr, pwi = powers(ar, ai, dt, (L - 1 - ridx).astype(F32))
    rsel = (lax.broadcasted_iota(I32, (CH2, H), 0) % H == lax.broadcasted_iota(I32, (CH2, H), 1)).astype(F32)
    bbr_t = _dot_exact(rsel, bbr)
    bbi_t = _dot_exact(rsel, bbi)
    w_re = pwr * bbr_t - pwi * bbi_t
    w_im = pwr * bbi_t + pwi * bbr_t
    wt_ref[0] = jnp.where(is_re, w_re, w_im).astype(BF16)
    a_r, a_i = powers(ar, ai, dt, float(L))
    sign = jnp.where(is_re, -1.0, 1.0)
    for lvl in range(SCAN_LEVELS_MAX):
        if lvl < n_levels:
            apow_ref[0, lvl:lvl + 1, :] = a_r
            apow_ref[0, SCAN_LEVELS_MAX + lvl:SCAN_LEVELS_MAX + lvl + 1, :] = sign * a_i
            a_r, a_i = a_r * a_r - a_i * a_i, 2.0 * a_r * a_i
        else:
            apow_ref[0, lvl:lvl + 1, :] = jnp.zeros_like(a_r)
            apow_ref[0, SCAN_LEVELS_MAX + lvl:SCAN_LEVELS_MAX + lvl + 1, :] = jnp.zeros_like(a_r)

    col = acol_ref[0]
    arc, aic, dtc = col[:, 0:1], col[:, 1:2], jnp.exp(col[:, 2:3])
    ct = ct_ref[0]
    csel = (lax.broadcasted_iota(I32, (H, CH2), 1) % H == lax.broadcasted_iota(I32, (H, CH2), 0)).astype(F32)
    ctr = _dot_exact(ct[:, :H], csel)
    cti = _dot_exact(ct[:, H:], csel)
    sidx = (lax.broadcasted_iota(I32, (P, CH2), 1) // H).astype(F32)
    pr, pi = powers(arc, aic, dtc, sidx)
    g_r = ctr * pr - cti * pi
    g_i = ctr * pi + cti * pr
    a1r, a1i = powers(arc, aic, dtc, 1.0)
    v_r = g_r * a1r - g_i * a1i
    v_i = g_r * a1i + g_i * a1r
    vt_ref[0, :P, :] = v_r.astype(BF16)
    vt_ref[0, P:, :] = (-v_i).astype(BF16)
    bbr64, bbi64 = bbr[:, :P], bbi[:, :P]
    kt = _dot_exact(bbr64, g_r) - _dot_exact(bbi64, g_i)
    lane2 = lax.broadcasted_iota(I32, (H, CH2), 1)
    for r in range(L):
        blk = kt if r == 0 else jnp.where(lane2 >= r * H, pltpu.roll(kt, r * H, axis=1), 0.0)
        mt_acc[r * H:(r + 1) * H, :] = blk
    mt_ref[0] = mt_acc[...].astype(BF16)
    dvec_ref[0] = _dot_exact(d_ref[0], csel)


def _s5_prep(n_levels, s5_a_re, s5_a_im, s5_log_dt, s5_b_re, s5_b_im, s5_c_re, s5_c_im, s5_d):
    G, P, H = S5_GROUPS, S5_STATE, S5_GROUP_CH
    ldt = jnp.broadcast_to(s5_log_dt[:, None], (G, P))
    arow = jnp.stack([s5_a_re, s5_a_im, ldt], axis=1)
    arow = jnp.concatenate([arow, arow], axis=2)
    acol = jnp.stack([s5_a_re, s5_a_im, ldt], axis=2)
    bt = jnp.concatenate([s5_b_re.transpose(0, 2, 1), s5_b_im.transpose(0, 2, 1)], axis=1)
    ct = jnp.concatenate([s5_c_re.transpose(0, 2, 1), s5_c_im.transpose(0, 2, 1)], axis=2)
    d = s5_d.reshape(G, 1, H)
    blk = lambda *s: pl.BlockSpec((1,) + s, lambda g: (g, 0, 0))
    return pl.pallas_call(
        functools.partial(_s5_prep_kernel, n_levels),
        grid=(G,),
        in_specs=[blk(3, P2), blk(P, 3), blk(2 * H, P), blk(P, 2 * H), blk(1, H)],
        out_specs=[blk(CH2, CH2), blk(CH2, P2), blk(P2, CH2), blk(2 * SCAN_LEVELS_MAX, P2), blk(1, CH2)],
        out_shape=[jax.ShapeDtypeStruct((G, CH2, CH2), BF16),
                   jax.ShapeDtypeStruct((G, CH2, P2), BF16),
                   jax.ShapeDtypeStruct((G, P2, CH2), BF16),
                   jax.ShapeDtypeStruct((G, 2 * SCAN_LEVELS_MAX, P2), F32),
                   jax.ShapeDtypeStruct((G, 1, CH2), F32)],
        scratch_shapes=[pltpu.VMEM((CH2, CH2), F32)],
        compiler_params=_cparams("parallel"),
        name="s5_discretise",
    )(arow, acol, bt, ct, d)


def _gelu_tanh(y):
    return 0.5 * y * (1.0 + jnp.tanh(math.sqrt(2.0 / math.pi) * (y + 0.044715 * (y * y * y))))


def _s5_scan_kernel(n_levels, batch, u_ref, mt_ref, wt_ref, vt_ref, apow_ref, dvec_ref, y_ref):
    u = u_ref[0]
    rows = u.shape[0]
    x = _dot(u, wt_ref[0])
    ridx = lax.broadcasted_iota(I32, (rows, P2), 0)
    apow = apow_ref[0]
    for lvl in range(n_levels):
        sh = batch * (1 << lvl)
        prev = jnp.where(ridx >= sh, pltpu.roll(x, sh, axis=0), 0.0)
        a_r = apow[lvl:lvl + 1]
        a_i = apow[SCAN_LEVELS_MAX + lvl:SCAN_LEVELS_MAX + lvl + 1]
        x = x + a_r * prev + a_i * pltpu.roll(prev, S5_STATE, axis=1)
    x_in = jnp.where(ridx >= batch, pltpu.roll(x, batch, axis=0), 0.0)
    y = _dot(u, mt_ref[0]) + _dot(x_in.astype(BF16), vt_ref[0]) + dvec_ref[0] * u.astype(F32)
    y_ref[0] = _gelu_tanh(y).astype(BF16)


def _s5_branch(u, B, S, s5_params):
    G, H, L = S5_GROUPS, S5_GROUP_CH, S5_CHUNK
    nc = S // L
    n_levels = max(1, (nc - 1).bit_length())
    assert n_levels <= SCAN_LEVELS_MAX
    mt, wt, vt, apow, dvec = _s5_prep(n_levels, *s5_params)
    rows = nc * B
    ug = u.reshape(B, nc, L, G, H).transpose(3, 1, 0, 2, 4).reshape(G, rows, CH2)
    blk = lambda *s: pl.BlockSpec((1,) + s, lambda g: (g, 0, 0))
    yg = pl.pallas_call(
        functools.partial(_s5_scan_kernel, n_levels, B),
        grid=(G,),
        in_specs=[blk(rows, CH2), blk(CH2, CH2), blk(CH2, P2), blk(P2, CH2),
                  blk(2 * SCAN_LEVELS_MAX, P2), blk(1, CH2)],
        out_specs=blk(rows, CH2),
        out_shape=jax.ShapeDtypeStruct((G, rows, CH2), BF16),
        compiler_params=_cparams("parallel"),
        name="s5_chunk_scan",
    )(ug, mt, wt, vt, apow, dvec)
    return yg.reshape(G, nc, B, L, H).transpose(2, 1, 3, 0, 4).reshape(B * S, S5_WIDTH)


def _merge_kernel(x_ref, o_ref, ys_ref, gm_ref, gs_ref, wmo_ref, wglu_ref, wout_ref, g_ref, b_ref, x1_ref):
    y_mla = _dot(o_ref[...], wmo_ref[...])
    z = _dot(ys_ref[...], wglu_ref[...])
    y_s5 = z[:, :D_MODEL] * jax.nn.sigmoid(z[:, D_MODEL:])
    merged = (jax.nn.sigmoid(gm_ref[...].astype(F32)) * y_mla
              + jax.nn.sigmoid(gs_ref[...].astype(F32)) * y_s5)
    mix = _dot(merged.astype(BF16), wout_ref[...])
    x1_ref[...] = _layer_norm(DEEPNORM_ALPHA * x_ref[...] + mix, g_ref[...], b_ref[...])


def _merge(x2d, o, ys, gm, gs, w_mla_o, w_s5_glu, w_out, ln_g, ln_b):
    T = x2d.shape[0]
    HP = MLA_HEADS * HEAD_PAD
    wmo = jnp.pad(w_mla_o.reshape(MLA_HEADS, V_HEAD, D_MODEL), ((0, 0), (0, HEAD_PAD - V_HEAD), (0, 0)))
    wmo = wmo.reshape(HP, D_MODEL).astype(BF16)
    tile = lambda w: pl.BlockSpec((TM, w), lambda i: (i, 0))
    return pl.pallas_call(
        _merge_kernel,
        grid=(T // TM,),
        in_specs=[tile(D_MODEL), tile(HP), tile(S5_WIDTH), tile(D_MODEL), tile(D_MODEL),
                  _full((HP, D_MODEL)), _full((S5_WIDTH, 2 * D_MODEL)), _full((D_MODEL, D_MODEL)),
                  _full((1, D_MODEL)), _full((1, D_MODEL))],
        out_specs=tile(D_MODEL),
        out_shape=jax.ShapeDtypeStruct((T, D_MODEL), F32),
        compiler_params=_cparams("parallel"),
        name="merge_outproj_ln1",
    )(x2d, o, ys, gm, gs, wmo, w_s5_glu.astype(BF16), w_out.astype(BF16),
      ln_g.reshape(1, -1), ln_b.reshape(1, -1))


def _memkv_kernel(mem_ref, g_ref, b_ref, w_ref, k_ref, v_ref):
    m = _layer_norm(mem_ref[0], g_ref[...], b_ref[...]).astype(BF16)
    kv = _dot(m, w_ref[...])
    hd = XATTN_HEADS * XATTN_HEAD_DIM
    k_ref[0] = kv[:, :hd].astype(BF16)
    v_ref[0] = kv[:, hd:].astype(BF16)


def _xattn_kernel(x1_ref, k_ref, v_ref, wq_ref, wo_ref, g_ref, b_ref, x2_ref):
    x1 = x1_ref[0]
    q = (_dot(x1.astype(BF16), wq_ref[...]) * (XATTN_HEAD_DIM ** -0.5)).astype(BF16)
    k = k_ref[0]
    v = v_ref[0]
    outs = []
    for h in range(XATTN_HEADS):
        sl = slice(h * XATTN_HEAD_DIM, (h + 1) * XATTN_HEAD_DIM)
        s = _dot_nt(q[:, sl], k[:, sl])
        p = jnp.exp(s - jnp.max(s, axis=-1, keepdims=True))
        o = _dot(p.astype(BF16), v[:, sl]) / jnp.sum(p, axis=-1, keepdims=True)
        outs.append(o.astype(BF16))
    xa = _dot(jnp.concatenate(outs, axis=1), wo_ref[...])
    x2 = _layer_norm(DEEPNORM_ALPHA * x1 + xa, g_ref[...], b_ref[...])
    x2_ref[0] = x2


def _cross_attention(x1, mem, B, S, mem_ln_g, mem_ln_b, w_xq, w_xkv, w_xo, ln_g, ln_b):
    M = mem.shape[1]
    hd = XATTN_HEADS * XATTN_HEAD_DIM
    row = lambda a: a.reshape(1, -1)
    k, v = pl.pallas_call(
        _memkv_kernel,
        grid=(B,),
        in_specs=[pl.BlockSpec((1, M, D_MODEL), lambda b: (b, 0, 0)), _full((1, D_MODEL)), _full((1, D_MODEL)),
                  _full((D_MODEL, 2 * hd))],
        out_specs=[pl.BlockSpec((1, M, hd), lambda b: (b, 0, 0))] * 2,
        out_shape=[jax.ShapeDtypeStruct((B, M, hd), BF16)] * 2,
        compiler_params=_cparams("parallel"),
        name="memory_kv",
    )(mem, row(mem_ln_g), row(mem_ln_b), w_xkv.astype(BF16))
    tile = lambda w: pl.BlockSpec((1, TM, w), lambda b, i: (b, i, 0))
    x2 = pl.pallas_call(
        _xattn_kernel,
        grid=(B, S // TM),
        in_specs=[tile(D_MODEL),
                  pl.BlockSpec((1, M, hd), lambda b, i: (b, 0, 0)), pl.BlockSpec((1, M, hd), lambda b, i: (b, 0, 0)),
                  _full((D_MODEL, hd)), _full((hd, D_MODEL)), _full((1, D_MODEL)), _full((1, D_MODEL))],
        out_specs=tile(D_MODEL),
        out_shape=jax.ShapeDtypeStruct((B, S, D_MODEL), F32),
        compiler_params=_cparams("parallel", "parallel"),
        name="cross_attention_ln2",
    )(x1.reshape(B, S, D_MODEL), k, v, w_xq.astype(BF16), w_xo.astype(BF16), row(ln_g), row(ln_b))
    return x2.reshape(B * S, D_MODEL)


def _route_sort_kernel(x_ref, w_ref, bias_ref, xs_ref, pos_ref, cnt_ref, off_ref):
    E, per = N_EXPERTS, N_EXPERTS // N_EXPERT_GROUPS
    tm = x_ref.shape[0]
    x = x_ref[...]
    logits = _dot_nt(w_ref[...], x, precision=lax.Precision.HIGHEST)
    scores = jax.nn.sigmoid(logits)
    sel = scores + bias_ref[...]
    neg = -jnp.inf
    i8 = lax.broadcasted_iota(I32, (per, tm), 0)
    gscore = []
    for g in range(N_EXPERT_GROUPS):
        blk = sel[g * per:(g + 1) * per]
        m1 = jnp.max(blk, axis=0, keepdims=True)
        i1 = jnp.min(jnp.where(blk == m1, i8, per), axis=0, keepdims=True)
        m2 = jnp.max(jnp.where(i8 == i1, neg, blk), axis=0, keepdims=True)
        gscore.append(m1 + m2)
    blocks = []
    for g in range(N_EXPERT_GROUPS):
        ahead = jnp.zeros((1, tm), I32)
        for o in range(N_EXPERT_GROUPS):
            if o == g:
                continue
            before = (gscore[o] >= gscore[g]) if o < g else (gscore[o] > gscore[g])
            ahead = ahead + before.astype(I32)
        blocks.append(jnp.where(ahead < TOPK_GROUPS, sel[g * per:(g + 1) * per], neg))
    cur = jnp.concatenate(blocks, axis=0)
    ie = lax.broadcasted_iota(I32, (E, tm), 0)
    picked = jnp.zeros((E, tm), F32)
    idxs = []
    for _ in range(TOP_K):
        m = jnp.max(cur, axis=0, keepdims=True)
        idx = jnp.min(jnp.where(cur == m, ie, E), axis=0, keepdims=True)
        hit = ie == idx
        picked = jnp.where(hit, 1.0, picked)
        cur = jnp.where(hit, neg, cur)
        idxs.append(idx)
    wsel = scores * picked
    wnorm = wsel / jnp.sum(wsel, axis=0, keepdims=True) * ROUTED_SCALE
    pb = picked.astype(BF16)
    tri = (lax.broadcasted_iota(I32, (tm, tm), 0) <= lax.broadcasted_iota(I32, (tm, tm), 1)).astype(BF16)
    incl = _dot(pb, tri)
    cnt_col = jnp.sum(picked, axis=1, keepdims=True)
    lower = (lax.broadcasted_iota(I32, (E, E), 1) < lax.broadcasted_iota(I32, (E, E), 0)).astype(F32)
    al_col = jnp.floor((cnt_col + (SEG_ALIGN - 1.0)) * (1.0 / SEG_ALIGN)) * SEG_ALIGN
    off_col = _dot_exact(lower, jnp.broadcast_to(al_col, (E, LANES)))[:, 0:1]
    posmat = jnp.where(picked > 0.0, off_col + incl - 1.0, -1.0)
    pos_ref[...] = posmat
    cnt_ref[0] = jnp.broadcast_to(cnt_col, (E, LANES)).astype(I32)
    off_ref[0] = jnp.broadcast_to(off_col, (E, LANES)).astype(I32)
    cnt_row = _dot_nt(jnp.ones((8, tm), BF16), pb)
    upper = (lax.broadcasted_iota(I32, (E, E), 0) < lax.broadcasted_iota(I32, (E, E), 1)).astype(F32)
    al_row = jnp.floor((cnt_row + (SEG_ALIGN - 1.0)) * (1.0 / SEG_ALIGN)) * SEG_ALIGN
    off_row = _dot_exact(al_row, upper)[0:1]
    end_row = off_row + al_row[0:1]

    xb = x.astype(BF16)
    half = D_MODEL // 2

    def build(c, carry):
        r0 = pl.multiple_of(c * OH_ROWS, OH_ROWS)
        jr = (lax.broadcasted_iota(I32, (OH_ROWS, E), 0) + r0).astype(F32)
        member = jnp.where(jr >= off_row, jnp.where(jr < end_row, 1.0, 0.0), 0.0)
        target = _dot_exact(member, posmat)
        j = (lax.broadcasted_iota(I32, (OH_ROWS, tm), 0) + r0).astype(F32)
        oh = jnp.where(target == j, 1.0, 0.0)
        rows = _dot(oh.astype(BF16), xb)
        bits = pltpu.bitcast(rows, U32)
        xs_ref[0, pl.ds(r0, OH_ROWS), 0:half] = bits[:, half:] | (bits[:, :half] >> 16)
        gw = _dot_nt(oh, wnorm, precision=lax.Precision.HIGHEST)
        wrow = jnp.sum(member * gw, axis=1, keepdims=True)
        xs_ref[0, pl.ds(r0, OH_ROWS), half:] = pltpu.bitcast(jnp.broadcast_to(wrow, (OH_ROWS, LANES)), U32)
        return carry

    lax.fori_loop(0, SORT_ROWS // OH_ROWS, build, 0)
    xs_ref[0, SORT_ROWS:, :] = jnp.zeros((PAD_ROWS, XW), U32)


def _route_sort(x2, w_router, router_bias):
    T = x2.shape[0]
    E = N_EXPERTS
    nt = T // TW
    seg = pl.BlockSpec((1, E, LANES), lambda i: (i, 0, 0))
    return pl.pallas_call(
        _route_sort_kernel,
        grid=(nt,),
        in_specs=[pl.BlockSpec((TW, D_MODEL), lambda i: (i, 0)), _full((E, D_MODEL)), _full((E, 1))],
        out_specs=[pl.BlockSpec((1, TS, XW), lambda i: (i, 0, 0)),
                   pl.BlockSpec((E, TW), lambda i: (0, i)), seg, seg],
        out_shape=[jax.ShapeDtypeStruct((nt, TS, XW), U32), jax.ShapeDtypeStruct((E, T), F32),
                   jax.ShapeDtypeStruct((nt, E, LANES), I32), jax.ShapeDtypeStruct((nt, E, LANES), I32)],
        compiler_params=_cparams("parallel"),
        name="route_local_sort",
    )(x2, w_router.T, router_bias.reshape(E, 1))


def _plan_kernel(nt, n_pieces_max, n_chunks_max, cnt_ref, off_ref, pa_ref, ce_ref, nch_ref):
    def unused(p, z):
        pa_ref[p] = -SEG_ALIGN
        return z

    def per_expert(e, carry):
        p0, g = carry

        def per_tile(i, p):
            n = cnt_ref[i * N_EXPERTS + e]
            base = i * TS + off_ref[i * N_EXPERTS + e]
            n_pc = (n + PR - 1) // PR

            def put(q, c):
                pa_ref[p + q] = base + q * PR
                return c

            lax.fori_loop(0, n_pc, put, 0)
            return p + n_pc

        p1 = lax.fori_loop(0, nt, per_tile, p0)
        n_ch = (p1 - p0 + PPC - 1) // PPC

        lax.fori_loop(p1, p0 + n_ch * PPC, unused, 0)

        def mark(c, z):
            ce_ref[g + c] = e
            return z

        lax.fori_loop(0, n_ch, mark, 0)
        return p0 + n_ch * PPC, g + n_ch

    p_end, g = lax.fori_loop(0, N_EXPERTS, per_expert, (jnp.int32(0), jnp.int32(0)))
    lax.fori_loop(p_end, n_pieces_max, unused, 0)
    nch_ref[0] = g

    def tail(c, z):
        ce_ref[c] = N_EXPERTS - 1
        return z

    lax.fori_loop(g, n_chunks_max, tail, 0)


def _plan(cnt, off, nt, n_pieces_max, n_chunks_max):
    smem = pl.BlockSpec(memory_space=pltpu.SMEM)
    return pl.pallas_call(
        functools.partial(_plan_kernel, nt, n_pieces_max, n_chunks_max),
        in_specs=[smem, smem],
        out_specs=[smem, smem, smem],
        out_shape=[jax.ShapeDtypeStruct((n_pieces_max,), I32), jax.ShapeDtypeStruct((n_chunks_max,), I32),
                   jax.ShapeDtypeStruct((1,), I32)],
        name="moe_plan",
    )(cnt, off)


def _unpack_bf16_pair(w):
    lo = pltpu.bitcast(w << 16, F32).astype(BF16)
    hi = pltpu.bitcast(w & jnp.uint32(0xFFFF0000), F32).astype(BF16)
    return lo, hi


def _pack_bf16_pair(lo, hi):
    lo_bits = pltpu.bitcast(lo.astype(BF16).astype(F32), U32) >> 16
    hi_bits = pltpu.bitcast(hi.astype(BF16).astype(F32), U32)
    return hi_bits | lo_bits


def _expert_kernel(ce_ref, pa_ref, nch_ref, tot_ref, xs_hbm, wgu_ref, wdn_ref, y_hbm,
                   xbuf, ybuf, zbuf, wgu_b, wdn_b, gsem, wsem, zsem):
    g = pl.program_id(0)
    n = nch_ref[0]
    slot = g % 2
    half = D_MODEL // 2

    dummy_base = (tot_ref.shape[0] - 1) * TS

    def gather_copy(s, a, jp):
        return pltpu.make_async_copy(xs_hbm.at[pl.ds(pl.multiple_of(a, SEG_ALIGN), PR), :],
                                     xbuf.at[s, pl.ds(jp * PR, PR), :], gsem.at[s])

    def write_copy(s, a, jp):
        return pltpu.make_async_copy(ybuf.at[s, pl.ds(jp * PR, PR), :],
                                     y_hbm.at[pl.ds(pl.multiple_of(a, SEG_ALIGN), PR), :], wsem.at[s])

    def gather_start(c, s):
        for jp in range(PPC):
            gather_copy(s, jnp.maximum(pa_ref[c * PPC + jp], 0), jp).start()

    def write_start(c, s):
        for jp in range(PPC):
            a = pa_ref[c * PPC + jp]
            write_copy(s, jnp.where(a >= 0, a, dummy_base + jp * PR), jp).start()

    def gather_wait(s):
        for jp in range(PPC):
            gather_copy(s, 0, jp).wait()

    def write_wait(s):
        for jp in range(PPC):
            write_copy(s, 0, jp).wait()

    @pl.when(g == 0)
    def _():
        gather_start(0, 0)
        zbuf[...] = jnp.zeros_like(zbuf)

        def zero_copy(row):
            return pltpu.make_async_copy(zbuf, y_hbm.at[pl.ds(pl.multiple_of(row, SEG_ALIGN), SEG_ALIGN), :], zsem)

        def tail_of(i):
            return i * TS + tot_ref[i], (TS - tot_ref[i]) // SEG_ALIGN

        def zero_start(i, c):
            base, cnt8 = tail_of(i)
            lax.fori_loop(0, cnt8, lambda q, z: (zero_copy(base + q * SEG_ALIGN).start(), z)[1], 0)
            return c

        def zero_wait(i, c):
            base, cnt8 = tail_of(i)
            lax.fori_loop(0, cnt8, lambda q, z: (zero_copy(base + q * SEG_ALIGN).wait(), z)[1], 0)
            return c

        lax.fori_loop(0, tot_ref.shape[0], zero_start, 0)
        lax.fori_loop(0, tot_ref.shape[0], zero_wait, 0)

    @pl.when(g + 1 < n)
    def _():
        gather_start(g + 1, 1 - slot)

    @pl.when(g < n)
    def _():
        e_prev = ce_ref[jnp.maximum(g - 1, 0)]

        @pl.when((g == 0) | (ce_ref[g] != e_prev))
        def _():
            wgu_b[...] = wgu_ref[0].astype(BF16)
            wdn_b[...] = wdn_ref[0].astype(BF16)

        gather_wait(slot)
        xw = xbuf[slot]
        lo, hi = _unpack_bf16_pair(xw[:, :half])
        h = _dot(lo, wgu_b[:half, :]) + _dot(hi, wgu_b[half:, :])
        gate, up = h[:, :EXPERT_FF], h[:, EXPERT_FF:]
        act = (gate * jax.nn.sigmoid(gate) * up).astype(BF16)
        y = _dot(act, wdn_b[...])
        w_row = pltpu.bitcast(xw[:, half:], F32)
        y = y * jnp.concatenate([w_row] * (D_MODEL // LANES), axis=1)

        @pl.when(g >= 1)
        def _():
            write_wait(1 - slot)

        ybuf[slot] = _pack_bf16_pair(y[:, :half], y[:, half:])
        write_start(g, slot)

    @pl.when(g == pl.num_programs(0) - 1)
    def _():
        write_wait((n - 1) % 2)


def _experts(ce, pa, nch, tot, xs, w_exp_gu, w_exp_down):
    n_rows = xs.shape[0]
    half = D_MODEL // 2
    grid_spec = pltpu.PrefetchScalarGridSpec(
        num_scalar_prefetch=4,
        grid=(ce.shape[0],),
        in_specs=[pl.BlockSpec(memory_space=pl.ANY),
                  pl.BlockSpec((1, D_MODEL, 2 * EXPERT_FF), lambda g, ce, pa, nch, tot: (ce[g], 0, 0)),
                  pl.BlockSpec((1, EXPERT_FF, D_MODEL), lambda g, ce, pa, nch, tot: (ce[g], 0, 0))],
        out_specs=pl.BlockSpec(memory_space=pl.ANY),
        scratch_shapes=[pltpu.VMEM((2, CR, XW), U32), pltpu.VMEM((2, CR, half), U32),
                        pltpu.VMEM((SEG_ALIGN, half), U32),
                        pltpu.VMEM((D_MODEL, 2 * EXPERT_FF), BF16), pltpu.VMEM((EXPERT_FF, D_MODEL), BF16),
                        pltpu.SemaphoreType.DMA((2,)), pltpu.SemaphoreType.DMA((2,)),
                        pltpu.SemaphoreType.DMA(())],
    )
    return pl.pallas_call(
        _expert_kernel,
        grid_spec=grid_spec,
        out_shape=jax.ShapeDtypeStruct((n_rows + TS, half), U32),
        compiler_params=_cparams("arbitrary"),
        name="moe_grouped_swiglu",
    )(ce, pa, nch, tot, xs, w_exp_gu, w_exp_down)


def _combine_kernel(tot_ref, y_ref, pos_ref, cnt_ref, off_ref, x2_ref, wsg_ref, wsd_ref, g_ref, b_ref, out_ref,
                    acc_lo, acc_hi):
    tm = x2_ref.shape[0]
    E = N_EXPERTS
    n_sorted = tot_ref[pl.program_id(0)]
    x2 = x2_ref[...]
    h = _dot(x2.astype(BF16), wsg_ref[...])
    gate, up = h[:, :SHARED_FF], h[:, SHARED_FF:]
    shared = _dot((gate * jax.nn.sigmoid(gate) * up).astype(BF16), wsd_ref[...])
    pos_t = pos_ref[...].T
    off_col = off_ref[0][:, 0:1].astype(F32)
    cnt_col = cnt_ref[0][:, 0:1].astype(F32)
    end_col = off_col + jnp.floor((cnt_col + (SEG_ALIGN - 1.0)) * (1.0 / SEG_ALIGN)) * SEG_ALIGN
    acc_lo[...] = jnp.zeros_like(acc_lo)
    acc_hi[...] = jnp.zeros_like(acc_hi)

    def gather_back(c, carry):
        r0 = pl.multiple_of(c * OH_ROWS, OH_ROWS)
        jc = (lax.broadcasted_iota(I32, (E, OH_ROWS), 1) + r0).astype(F32)
        member = jnp.where(jc >= off_col, jnp.where(jc < end_col, 1.0, 0.0), 0.0)
        target = _dot_exact(pos_t, member)
        j = (lax.broadcasted_iota(I32, (tm, OH_ROWS), 1) + r0).astype(F32)
        ohb = jnp.where(target == j, 1.0, 0.0).astype(BF16)
        yw = y_ref[0, pl.ds(r0, OH_ROWS), :]
        row = lax.broadcasted_iota(I32, yw.shape, 0) + r0
        lo, hi = _unpack_bf16_pair(jnp.where(row < n_sorted, yw, jnp.uint32(0)))
        acc_lo[...] += _dot(ohb, lo)
        acc_hi[...] += _dot(ohb, hi)
        return carry

    lax.fori_loop(0, SORT_ROWS // OH_ROWS, gather_back, 0)
    ff = shared + jnp.concatenate([acc_lo[...], acc_hi[...]], axis=1)
    out_ref[...] = _layer_norm(DEEPNORM_ALPHA * x2 + ff, g_ref[...], b_ref[...])


def _combine(tot, y, pos, cnt, off, x2, w_sh_gu, w_sh_down, ln_g, ln_b):
    T = x2.shape[0]
    nt = T // TW
    half = D_MODEL // 2
    tile = lambda w: pl.BlockSpec((TW, w), lambda i: (i, 0))
    return pl.pallas_call(
        _combine_kernel,
        grid=(nt,),
        in_specs=[pl.BlockSpec(memory_space=pltpu.SMEM),
                  pl.BlockSpec((1, TS, half), lambda i: (i, 0, 0)),
                  pl.BlockSpec((N_EXPERTS, TW), lambda i: (0, i)),
                  pl.BlockSpec((1, N_EXPERTS, LANES), lambda i: (i, 0, 0)),
                  pl.BlockSpec((1, N_EXPERTS, LANES), lambda i: (i, 0, 0)), tile(D_MODEL),
                  _full((D_MODEL, 2 * SHARED_FF)), _full((SHARED_FF, D_MODEL)),
                  _full((1, D_MODEL)), _full((1, D_MODEL))],
        out_specs=tile(D_MODEL),
        out_shape=jax.ShapeDtypeStruct((T, D_MODEL), F32),
        scratch_shapes=[pltpu.VMEM((TW, half), F32), pltpu.VMEM((TW, half), F32)],
        compiler_params=_cparams("parallel"),
        name="moe_combine_shared_ln3",
    )(tot, y.reshape(nt + 1, TS, half), pos, cnt, off, x2, w_sh_gu.astype(BF16), w_sh_down.astype(BF16),
      ln_g.reshape(1, -1), ln_b.reshape(1, -1))


def _moe(x2, w_router, router_bias, w_exp_gu, w_exp_down, w_sh_gu, w_sh_down, ln_g, ln_b):
    T = x2.shape[0]
    E = N_EXPERTS
    nt = T // TW
    xs, pos, cnt, off = _route_sort(x2, w_router, router_bias)
    n_pieces_max = (T * TOP_K + nt * E * (PR - 1)) // PR
    n_chunks_max = n_pieces_max // PPC + E
    pa, ce, nch = _plan(cnt[:, :, 0].reshape(nt * E), off[:, :, 0].reshape(nt * E), nt,
                        n_chunks_max * PPC, n_chunks_max)
    last_al = (cnt[:, E - 1, 0] + SEG_ALIGN - 1) // SEG_ALIGN * SEG_ALIGN
    tot = jnp.concatenate([off[:, E - 1, 0] + last_al, jnp.zeros((1,), I32)])
    y = _experts(ce, pa, nch, tot, xs.reshape(nt * TS, XW), w_exp_gu, w_exp_down)
    return _combine(tot, y, pos, cnt, off, x2, w_sh_gu, w_sh_down, ln_g, ln_b)


def kernel(x, mem, positions, w_in, q_norm_g, kv_norm_g, w_uq, w_ukv, w_mla_o, s5_a_re, s5_a_im, s5_log_dt, s5_b_re, s5_b_im, s5_c_re, s5_c_im, s5_d, w_s5_glu, w_out, ln1_g, ln1_b, mem_ln_g, mem_ln_b, w_xq, w_xkv, w_xo, ln2_g, ln2_b, w_router, router_bias, w_exp_gu, w_exp_down, w_sh_gu, w_sh_down, ln3_g, ln3_b):
    B, S, D = x.shape
    assert D == D_MODEL and S % TQ == 0 and (B * S) % TM == 0 and S % S5_CHUNK == 0
    xc = x.reshape(B * S, D)
    for l in range(w_in.shape[0]):
        tabs = _rope_tables(positions)
        q, k, v, u, gm, gs = _inproj(xc, tabs, w_in[l], q_norm_g[l], kv_norm_g[l], w_uq[l], w_ukv[l])
        o = _attention(q, k, v, B, S)
        ys = _s5_branch(u, B, S, (s5_a_re[l], s5_a_im[l], s5_log_dt[l], s5_b_re[l], s5_b_im[l],
                                  s5_c_re[l], s5_c_im[l], s5_d[l]))
        x1 = _merge(xc, o, ys, gm, gs, w_mla_o[l], w_s5_glu[l], w_out[l], ln1_g[l], ln1_b[l])
        x2 = _cross_attention(x1, mem, B, S, mem_ln_g[l], mem_ln_b[l], w_xq[l], w_xkv[l], w_xo[l],
                                   ln2_g[l], ln2_b[l])
        xc = _moe(x2, w_router[l], router_bias[l], w_exp_gu[l], w_exp_down[l], w_sh_gu[l],
                  w_sh_down[l], ln3_g[l], ln3_b[l])
    return xc.reshape(B, S, D)
```

```python
import functools
import math

import jax
import jax.numpy as jnp
from jax import lax
from jax.experimental import pallas as pl
from jax.experimental.pallas import tpu as pltpu

F32 = jnp.float32
BF16 = jnp.bfloat16
I32 = jnp.int32
U32 = jnp.uint32

D_MODEL = 1024
MLA_HEADS = 8
QK_NOPE = 64
QK_ROPE = 32
V_HEAD = 64
Q_LORA = 256
KV_LORA = 256
ROPE_THETA = 10000.0
S5_GROUP_CH = 16
S5_WIDTH = 512
S5_GROUPS = 32
S5_STATE = 64
XATTN_HEADS = 4
XATTN_HEAD_DIM = 128
N_EXPERTS = 64
TOP_K = 8
N_EXPERT_GROUPS = 8
TOPK_GROUPS = 4
EXPERT_FF = 256
SHARED_FF = 256
ROUTED_SCALE = 2.5
LN_EPS = 1e-5
RMS_EPS = 1e-6
DEPTH = 1
DEEPNORM_ALPHA = (2.0 * DEPTH) ** 0.25

LANES = 128
HEAD_PAD = 128
ROPE_LO = QK_NOPE
ROPE_HALF = QK_ROPE // 2

TM = 512
TQ = 512
S5_CHUNK = 16
TW = 512
OH_ROWS = 512
PR = 16
PAD_ROWS = PR
SEG_ALIGN = 8
SORT_ROWS = TOP_K * TW + N_EXPERTS * SEG_ALIGN
TS = SORT_ROWS + PAD_ROWS
XW = D_MODEL // 2 + LANES
CR = 256
PPC = CR // PR
GATHER_AHEAD = 2
N_SLOTS = GATHER_AHEAD + 1
POS_RADIX = 64
POS_NONE_Q = 127
assert SORT_ROWS <= POS_RADIX * POS_NONE_Q and SORT_ROWS % OH_ROWS == 0
VMEM_LIMIT = 48 * 1024 * 1024


def _cparams(*sem):
    return pltpu.CompilerParams(dimension_semantics=sem, vmem_limit_bytes=VMEM_LIMIT)


def _dot(a, b):
    return jnp.dot(a, b, preferred_element_type=F32)


def _dot_nt(a, b, precision=None):
    return lax.dot_general(a, b, (((1,), (1,)), ((), ())), preferred_element_type=F32,
                           precision=precision)


def _dot_exact(a, b):
    return jnp.dot(a, b, preferred_element_type=F32, precision=lax.Precision.HIGHEST)


def _layer_norm(h, g, b):
    mu = jnp.mean(h, axis=-1, keepdims=True)
    c = h - mu
    var = jnp.mean(c * c, axis=-1, keepdims=True)
    return c * lax.rsqrt(var + LN_EPS) * g + b


def _rms_norm(h, g):
    return h * lax.rsqrt(jnp.mean(h * h, axis=-1, keepdims=True) + RMS_EPS) * g


def _full(shape):
    n = len(shape)
    return pl.BlockSpec(shape, lambda *_: (0,) * n)


def _rope_angle_kernel(pos_ref, invf_ref, cos_ref, sin_ref):
    ang = pos_ref[...].astype(F32) * invf_ref[...]
    cos_ref[...] = jnp.cos(ang)
    sin_ref[...] = jnp.sin(ang)


def _rope_tables(positions):
    T = positions.size
    rows = T * ROPE_HALF // LANES
    pos_rep = jnp.repeat(positions.reshape(T), ROPE_HALF).reshape(rows, LANES)
    inv_freq = ROPE_THETA ** (-jnp.arange(0, QK_ROPE, 2, dtype=F32) / QK_ROPE)
    invf = jnp.tile(inv_freq, LANES // ROPE_HALF).reshape(1, LANES)
    cos, sin = pl.pallas_call(
        _rope_angle_kernel,
        out_shape=(jax.ShapeDtypeStruct((rows, LANES), F32),) * 2,
        name="rope_angles",
    )(pos_rep, invf)
    cos = cos.reshape(T, ROPE_HALF)
    sin = sin.reshape(T, ROPE_HALF)
    one = jnp.ones((T, ROPE_LO), F32)
    zero_lo = jnp.zeros((T, ROPE_LO), F32)
    zero_h = jnp.zeros((T, ROPE_HALF), F32)
    tail1 = jnp.ones((T, HEAD_PAD - ROPE_LO - QK_ROPE), F32)
    tail0 = jnp.zeros((T, HEAD_PAD - ROPE_LO - QK_ROPE), F32)
    c_tab = jnp.concatenate([one, cos, cos, tail1], axis=1)
    s_up = jnp.concatenate([zero_lo, -sin, zero_h, tail0], axis=1)
    s_dn = jnp.concatenate([zero_lo, zero_h, sin, tail0], axis=1)
    return c_tab, s_up, s_dn


def _rope(x, c_tab, s_up, s_dn):
    return (x * c_tab + pltpu.roll(x, HEAD_PAD - ROPE_HALF, axis=1) * s_up
            + pltpu.roll(x, ROPE_HALF, axis=1) * s_dn)


def _inproj_kernel(x_ref, c_ref, su_ref, sd_ref, wlat_ref, wu_ref, wgm_ref, wgs_ref, qg_ref, kvg_ref,
                   wuq_ref, wuk_ref, wuv_ref, q_ref, k_ref, v_ref, u_ref, gm_ref, gs_ref):
    xb = x_ref[...].astype(BF16)
    lat = _dot(xb, wlat_ref[...])
    qn = _rms_norm(lat[:, :Q_LORA], qg_ref[...]).astype(BF16)
    kvn = _rms_norm(lat[:, Q_LORA:Q_LORA + KV_LORA], kvg_ref[...]).astype(BF16)
    c_tab, s_up, s_dn = c_ref[...], su_ref[...], sd_ref[...]
    k_rope = _rope(lat[:, Q_LORA + KV_LORA:], c_tab, s_up, s_dn)
    q = _dot(qn, wuq_ref[...])
    k = _dot(kvn, wuk_ref[...])
    scale = (QK_NOPE + QK_ROPE) ** -0.5 * math.log2(math.e)
    for h in range(MLA_HEADS):
        sl = slice(h * HEAD_PAD, (h + 1) * HEAD_PAD)
        q_ref[:, sl] = (_rope(q[:, sl], c_tab, s_up, s_dn) * scale).astype(BF16)
        k_ref[:, sl] = (k[:, sl] + k_rope).astype(BF16)
    v = _dot(kvn, wuv_ref[...])
    ones_lane = lax.broadcasted_iota(I32, (1, v.shape[1]), 1) % HEAD_PAD == V_HEAD
    v_ref[...] = jnp.where(ones_lane, 1.0, v).astype(BF16)
    u_ref[...] = _dot(xb, wu_ref[...]).astype(BF16)
    gm_ref[...] = _dot(xb, wgm_ref[...]).astype(BF16)
    gs_ref[...] = _dot(xb, wgs_ref[...]).astype(BF16)


def _pad_heads(w, head_w, lo_w):
    K = w.shape[0]
    w = w.reshape(K, MLA_HEADS, head_w)[:, :, :lo_w]
    w = jnp.pad(w, ((0, 0), (0, 0), (0, HEAD_PAD - lo_w)))
    return w.reshape(K, MLA_HEADS * HEAD_PAD)


def _inproj(x2d, tabs, w_in, q_norm_g, kv_norm_g, w_uq, w_ukv):
    T = x2d.shape[0]
    o_rope = Q_LORA + KV_LORA
    o_u = o_rope + QK_ROPE
    o_gm = o_u + S5_WIDTH
    o_gs = o_gm + D_MODEL
    w_rope = jnp.pad(w_in[:, o_rope:o_u], ((0, 0), (ROPE_LO, HEAD_PAD - ROPE_LO - QK_ROPE)))
    w_lat = jnp.concatenate([w_in[:, :o_rope], w_rope], axis=1).astype(BF16)
    w_u = w_in[:, o_u:o_gm].astype(BF16)
    w_gm = w_in[:, o_gm:o_gs].astype(BF16)
    w_gs = w_in[:, o_gs:].astype(BF16)
    wuq = _pad_heads(w_uq, QK_NOPE + QK_ROPE, QK_NOPE + QK_ROPE).astype(BF16)
    kv3 = w_ukv.reshape(KV_LORA, MLA_HEADS, QK_NOPE + V_HEAD)
    wuk = _pad_heads(kv3[:, :, :QK_NOPE].reshape(KV_LORA, -1), QK_NOPE, QK_NOPE).astype(BF16)
    wuv = _pad_heads(kv3[:, :, QK_NOPE:].reshape(KV_LORA, -1), V_HEAD, V_HEAD).astype(BF16)
    HP = MLA_HEADS * HEAD_PAD
    tile = lambda w: pl.BlockSpec((TM, w), lambda i: (i, 0))
    return pl.pallas_call(
        _inproj_kernel,
        grid=(T // TM,),
        in_specs=[tile(D_MODEL), tile(HEAD_PAD), tile(HEAD_PAD), tile(HEAD_PAD),
                  _full(w_lat.shape), _full(w_u.shape), _full(w_gm.shape), _full(w_gs.shape),
                  _full((1, Q_LORA)), _full((1, KV_LORA)),
                  _full(wuq.shape), _full(wuk.shape), _full(wuv.shape)],
        out_specs=[tile(HP), tile(HP), tile(HP), tile(S5_WIDTH), tile(D_MODEL), tile(D_MODEL)],
        out_shape=[jax.ShapeDtypeStruct((T, HP), BF16)] * 3
        + [jax.ShapeDtypeStruct((T, S5_WIDTH), BF16)]
        + [jax.ShapeDtypeStruct((T, D_MODEL), BF16)] * 2,
        compiler_params=_cparams("parallel"),
        name="inproj_mla_prep",
    )(x2d, *tabs, w_lat, w_u, w_gm, w_gs, q_norm_g.reshape(1, -1), kv_norm_g.reshape(1, -1),
      wuq, wuk, wuv)


def _attn_kernel(q_ref, k_ref, v_ref, o_ref):
    qi = pl.program_id(2)
    q = q_ref[0]

    def step(j, carry, masked):
        m, acc = carry
        start = pl.multiple_of(j * TQ, TQ)
        k = k_ref[0, pl.ds(start, TQ), :]
        v = v_ref[0, pl.ds(start, TQ), :]
        s = _dot_nt(q, k)
        if masked:
            row = lax.broadcasted_iota(I32, (TQ, TQ), 0)
            col = lax.broadcasted_iota(I32, (TQ, TQ), 1)
            s = jnp.where(col <= row, s, -jnp.inf)
        m_new = jnp.maximum(m, jnp.max(s, axis=-1, keepdims=True))
        p = jnp.exp2(s - m_new)
        acc = jnp.exp2(m - m_new) * acc + _dot(p.astype(BF16), v)
        return m_new, acc

    init = (jnp.full((TQ, 1), -jnp.inf, F32), jnp.zeros((TQ, HEAD_PAD), F32))
    carry = lax.fori_loop(0, qi, lambda j, c: step(j, c, False), init)
    m, acc = step(qi, carry, True)
    o_ref[0] = (acc / acc[:, V_HEAD:V_HEAD + 1]).astype(BF16)


def _attention(q, k, v, B, S):
    HP = MLA_HEADS * HEAD_PAD
    q, k, v = (a.reshape(B, S, HP) for a in (q, k, v))
    o = pl.pallas_call(
        _attn_kernel,
        grid=(B, MLA_HEADS, S // TQ),
        in_specs=[pl.BlockSpec((1, TQ, HEAD_PAD), lambda b, h, i: (b, i, h)),
                  pl.BlockSpec((1, S, HEAD_PAD), lambda b, h, i: (b, 0, h)),
                  pl.BlockSpec((1, S, HEAD_PAD), lambda b, h, i: (b, 0, h))],
        out_specs=pl.BlockSpec((1, TQ, HEAD_PAD), lambda b, h, i: (b, i, h)),
        out_shape=jax.ShapeDtypeStruct((B, S, HP), BF16),
        compiler_params=_cparams("parallel", "parallel", "arbitrary"),
        name="mla_flash_attention",
    )(q, k, v)
    return o.reshape(B * S, HP)


CH2 = S5_CHUNK * S5_GROUP_CH
P2 = 2 * S5_STATE
SCAN_LEVELS_MAX = 16


def _s5_prep_kernel(n_levels, arow_ref, acol_ref, bt_ref, ct_ref, d_ref,
                    mt_ref, wt_ref, vt_ref, apow_ref, dvec_ref, mt_acc):
    P, H, L = S5_STATE, S5_GROUP_CH, S5_CHUNK

    def powers(ar, ai, dt, n):
        e = jnp.exp(n * (ar * dt))
        ang = n * (ai * dt)
        return e * jnp.cos(ang), e * jnp.sin(ang)

    def zoh_coef(ar, ai, dt):
        pr, pi = powers(ar, ai, dt, 1.0)
        nr, ni = pr - 1.0, pi
        den = ar * ar + ai * ai
        return (nr * ar + ni * ai) / den, (ni * ar - nr * ai) / den

    row = arow_ref[0]
    ar, ai, dt = row[0:1], row[1:2], jnp.exp(row[2:3])
    cr, ci = zoh_coef(ar, ai, dt)
    bt = bt_ref[0]
    btr = jnp.concatenate([bt[:H], bt[:H]], axis=1)
    bti = jnp.concatenate([bt[H:], bt[H:]], axis=1)
    bbr = cr * btr - ci * bti
    bbi = cr * bti + ci * btr
    lane = lax.broadcasted_iota(I32, (1, P2), 1)
    is_re = lane < P
    ridx = lax.broadcasted_iota(I32, (CH2, P2), 0) // H
    pwr, pwi = powers(ar, ai, dt, (L - 1 - ridx).astype(F32))
    rsel = (lax.broadcasted_iota(I32, (CH2, H), 0) % H == lax.broadcasted_iota(I32, (CH2, H), 1)).astype(F32)
    bbr_t = _dot_exact(rsel, bbr)
    bbi_t = _dot_exact(rsel, bbi)
    w_re = pwr * bbr_t - pwi * bbi_t
    w_im = pwr * bbi_t + pwi * bbr_t
    wt_ref[0] = jnp.where(is_re, w_re, w_im).astype(BF16)
    a_r, a_i = powers(ar, ai, dt, float(L))
    sign = jnp.where(is_re, -1.0, 1.0)
    for lvl in range(SCAN_LEVELS_MAX):
        if lvl < n_levels:
            apow_ref[0, lvl:lvl + 1, :] = a_r
            apow_ref[0, SCAN_LEVELS_MAX + lvl:SCAN_LEVELS_MAX + lvl + 1, :] = sign * a_i
            a_r, a_i = a_r * a_r - a_i * a_i, 2.0 * a_r * a_i
        else:
            apow_ref[0, lvl:lvl + 1, :] = jnp.zeros_like(a_r)
            apow_ref[0, SCAN_LEVELS_MAX + lvl:SCAN_LEVELS_MAX + lvl + 1, :] = jnp.zeros_like(a_r)

    col = acol_ref[0]
    arc, aic, dtc = col[:, 0:1], col[:, 1:2], jnp.exp(col[:, 2:3])
    ct = ct_ref[0]
    csel = (lax.broadcasted_iota(I32, (H, CH2), 1) % H == lax.broadcasted_iota(I32, (H, CH2), 0)).astype(F32)
    ctr = _dot_exact(ct[:, :H], csel)
    cti = _dot_exact(ct[:, H:], csel)
    sidx = (lax.broadcasted_iota(I32, (P, CH2), 1) // H).astype(F32)
    pr, pi = powers(arc, aic, dtc, sidx)
    g_r = ctr * pr - cti * pi
    g_i = ctr * pi + cti * pr
    a1r, a1i = powers(arc, aic, dtc, 1.0)
    v_r = g_r * a1r - g_i * a1i
    v_i = g_r * a1i + g_i * a1r
    vt_ref[0, :P, :] = v_r.astype(BF16)
    vt_ref[0, P:, :] = (-v_i).astype(BF16)
    bbr64, bbi64 = bbr[:, :P], bbi[:, :P]
    kt = _dot_exact(bbr64, g_r) - _dot_exact(bbi64, g_i)
    lane2 = lax.broadcasted_iota(I32, (H, CH2), 1)
    for r in range(L):
        blk = kt if r == 0 else jnp.where(lane2 >= r * H, pltpu.roll(kt, r * H, axis=1), 0.0)
        mt_acc[r * H:(r + 1) * H, :] = blk
    mt_ref[0] = mt_acc[...].astype(BF16)
    dvec_ref[0] = _dot_exact(d_ref[0], csel)


def _s5_prep(n_levels, s5_a_re, s5_a_im, s5_log_dt, s5_b_re, s5_b_im, s5_c_re, s5_c_im, s5_d):
    G, P, H = S5_GROUPS, S5_STATE, S5_GROUP_CH
    ldt = jnp.broadcast_to(s5_log_dt[:, None], (G, P))
    arow = jnp.stack([s5_a_re, s5_a_im, ldt], axis=1)
    arow = jnp.concatenate([arow, arow], axis=2)
    acol = jnp.stack([s5_a_re, s5_a_im, ldt], axis=2)
    bt = jnp.concatenate([s5_b_re.transpose(0, 2, 1), s5_b_im.transpose(0, 2, 1)], axis=1)
    ct = jnp.concatenate([s5_c_re.transpose(0, 2, 1), s5_c_im.transpose(0, 2, 1)], axis=2)
    d = s5_d.reshape(G, 1, H)
    blk = lambda *s: pl.BlockSpec((1,) + s, lambda g: (g, 0, 0))
    return pl.pallas_call(
        functools.partial(_s5_prep_kernel, n_levels),
        grid=(G,),
        in_specs=[blk(3, P2), blk(P, 3), blk(2 * H, P), blk(P, 2 * H), blk(1, H)],
        out_specs=[blk(CH2, CH2), blk(CH2, P2), blk(P2, CH2), blk(2 * SCAN_LEVELS_MAX, P2), blk(1, CH2)],
        out_shape=[jax.ShapeDtypeStruct((G, CH2, CH2), BF16),
                   jax.ShapeDtypeStruct((G, CH2, P2), BF16),
                   jax.ShapeDtypeStruct((G, P2, CH2), BF16),
                   jax.ShapeDtypeStruct((G, 2 * SCAN_LEVELS_MAX, P2), F32),
                   jax.ShapeDtypeStruct((G, 1, CH2), F32)],
        scratch_shapes=[pltpu.VMEM((CH2, CH2), F32)],
        compiler_params=_cparams("parallel"),
        name="s5_discretise",
    )(arow, acol, bt, ct, d)


def _gelu_tanh(y):
    return 0.5 * y * (1.0 + jnp.tanh(math.sqrt(2.0 / math.pi) * (y + 0.044715 * (y * y * y))))


def _s5_scan_kernel(n_levels, batch, u_ref, mt_ref, wt_ref, vt_ref, apow_ref, dvec_ref, y_ref):
    u = u_ref[0]
    rows = u.shape[0]
    x = _dot(u, wt_ref[0])
    ridx = lax.broadcasted_iota(I32, (rows, P2), 0)
    apow = apow_ref[0]
    for lvl in range(n_levels):
        sh = batch * (1 << lvl)
        prev = jnp.where(ridx >= sh, pltpu.roll(x, sh, axis=0), 0.0)
        a_r = apow[lvl:lvl + 1]
        a_i = apow[SCAN_LEVELS_MAX + lvl:SCAN_LEVELS_MAX + lvl + 1]
        x = x + a_r * prev + a_i * pltpu.roll(prev, S5_STATE, axis=1)
    x_in = jnp.where(ridx >= batch, pltpu.roll(x, batch, axis=0), 0.0)
    y = _dot(u, mt_ref[0]) + _dot(x_in.astype(BF16), vt_ref[0]) + dvec_ref[0] * u.astype(F32)
    y_ref[0] = _gelu_tanh(y).astype(BF16)


def _s5_branch(u, B, S, s5_params):
    G, H, L = S5_GROUPS, S5_GROUP_CH, S5_CHUNK
    nc = S // L
    n_levels = max(1, (nc - 1).bit_length())
    assert n_levels <= SCAN_LEVELS_MAX
    mt, wt, vt, apow, dvec = _s5_prep(n_levels, *s5_params)
    rows = nc * B
    ug = u.reshape(B, nc, L, G, H).transpose(3, 1, 0, 2, 4).reshape(G, rows, CH2)
    blk = lambda *s: pl.BlockSpec((1,) + s, lambda g: (g, 0, 0))
    yg = pl.pallas_call(
        functools.partial(_s5_scan_kernel, n_levels, B),
        grid=(G,),
        in_specs=[blk(rows, CH2), blk(CH2, CH2), blk(CH2, P2), blk(P2, CH2),
                  blk(2 * SCAN_LEVELS_MAX, P2), blk(1, CH2)],
        out_specs=blk(rows, CH2),
        out_shape=jax.ShapeDtypeStruct((G, rows, CH2), BF16),
        compiler_params=_cparams("parallel"),
        name="s5_chunk_scan",
    )(ug, mt, wt, vt, apow, dvec)
    return yg.reshape(G, nc, B, L, H).transpose(2, 1, 3, 0, 4).reshape(B * S, S5_WIDTH)


def _merge_kernel(x_ref, o_ref, ys_ref, gm_ref, gs_ref, wmo_ref, wglu_ref, wout_ref, g_ref, b_ref, x1_ref):
    y_mla = _dot(o_ref[...], wmo_ref[...])
    z = _dot(ys_ref[...], wglu_ref[...])
    y_s5 = z[:, :D_MODEL] * jax.nn.sigmoid(z[:, D_MODEL:])
    merged = (jax.nn.sigmoid(gm_ref[...].astype(F32)) * y_mla
              + jax.nn.sigmoid(gs_ref[...].astype(F32)) * y_s5)
    mix = _dot(merged.astype(BF16), wout_ref[...])
    x1_ref[...] = _layer_norm(DEEPNORM_ALPHA * x_ref[...] + mix, g_ref[...], b_ref[...])


def _merge(x2d, o, ys, gm, gs, w_mla_o, w_s5_glu, w_out, ln_g, ln_b):
    T = x2d.shape[0]
    HP = MLA_HEADS * HEAD_PAD
    wmo = jnp.pad(w_mla_o.reshape(MLA_HEADS, V_HEAD, D_MODEL), ((0, 0), (0, HEAD_PAD - V_HEAD), (0, 0)))
    wmo = wmo.reshape(HP, D_MODEL).astype(BF16)
    tile = lambda w: pl.BlockSpec((TM, w), lambda i: (i, 0))
    return pl.pallas_call(
        _merge_kernel,
        grid=(T // TM,),
        in_specs=[tile(D_MODEL), tile(HP), tile(S5_WIDTH), tile(D_MODEL), tile(D_MODEL),
                  _full((HP, D_MODEL)), _full((S5_WIDTH, 2 * D_MODEL)), _full((D_MODEL, D_MODEL)),
                  _full((1, D_MODEL)), _full((1, D_MODEL))],
        out_specs=tile(D_MODEL),
        out_shape=jax.ShapeDtypeStruct((T, D_MODEL), F32),
        compiler_params=_cparams("parallel"),
        name="merge_outproj_ln1",
    )(x2d, o, ys, gm, gs, wmo, w_s5_glu.astype(BF16), w_out.astype(BF16),
      ln_g.reshape(1, -1), ln_b.reshape(1, -1))


def _memkv_kernel(mem_ref, g_ref, b_ref, w_ref, k_ref, v_ref):
    m = _layer_norm(mem_ref[0], g_ref[...], b_ref[...]).astype(BF16)
    kv = _dot(m, w_ref[...])
    hd = XATTN_HEADS * XATTN_HEAD_DIM
    k_ref[0] = kv[:, :hd].astype(BF16)
    v_ref[0] = kv[:, hd:].astype(BF16)


def _xattn_kernel(x1_ref, k_ref, v_ref, wq_ref, wo_ref, g_ref, b_ref, x2_ref):
    x1 = x1_ref[0]
    q = (_dot(x1.astype(BF16), wq_ref[...]) * (XATTN_HEAD_DIM ** -0.5)).astype(BF16)
    k = k_ref[0]
    v = v_ref[0]
    outs = []
    for h in range(XATTN_HEADS):
        sl = slice(h * XATTN_HEAD_DIM, (h + 1) * XATTN_HEAD_DIM)
        s = _dot_nt(q[:, sl], k[:, sl])
        p = jnp.exp(s - jnp.max(s, axis=-1, keepdims=True))
        o = _dot(p.astype(BF16), v[:, sl]) / jnp.sum(p, axis=-1, keepdims=True)
        outs.append(o.astype(BF16))
    xa = _dot(jnp.concatenate(outs, axis=1), wo_ref[...])
    x2 = _layer_norm(DEEPNORM_ALPHA * x1 + xa, g_ref[...], b_ref[...])
    x2_ref[0] = x2


def _cross_attention(x1, mem, B, S, mem_ln_g, mem_ln_b, w_xq, w_xkv, w_xo, ln_g, ln_b):
    M = mem.shape[1]
    hd = XATTN_HEADS * XATTN_HEAD_DIM
    row = lambda a: a.reshape(1, -1)
    k, v = pl.pallas_call(
        _memkv_kernel,
        grid=(B,),
        in_specs=[pl.BlockSpec((1, M, D_MODEL), lambda b: (b, 0, 0)), _full((1, D_MODEL)), _full((1, D_MODEL)),
                  _full((D_MODEL, 2 * hd))],
        out_specs=[pl.BlockSpec((1, M, hd), lambda b: (b, 0, 0))] * 2,
        out_shape=[jax.ShapeDtypeStruct((B, M, hd), BF16)] * 2,
        compiler_params=_cparams("parallel"),
        name="memory_kv",
    )(mem, row(mem_ln_g), row(mem_ln_b), w_xkv.astype(BF16))
    tile = lambda w: pl.BlockSpec((1, TM, w), lambda b, i: (b, i, 0))
    x2 = pl.pallas_call(
        _xattn_kernel,
        grid=(B, S // TM),
        in_specs=[tile(D_MODEL),
                  pl.BlockSpec((1, M, hd), lambda b, i: (b, 0, 0)), pl.BlockSpec((1, M, hd), lambda b, i: (b, 0, 0)),
                  _full((D_MODEL, hd)), _full((hd, D_MODEL)), _full((1, D_MODEL)), _full((1, D_MODEL))],
        out_specs=tile(D_MODEL),
        out_shape=jax.ShapeDtypeStruct((B, S, D_MODEL), F32),
        compiler_params=_cparams("parallel", "parallel"),
        name="cross_attention_ln2",
    )(x1.reshape(B, S, D_MODEL), k, v, w_xq.astype(BF16), w_xo.astype(BF16), row(ln_g), row(ln_b))
    return x2.reshape(B * S, D_MODEL)


def _route_sort_kernel(x_ref, w_ref, bias_ref, xs_ref, pos_ref, cnt_ref, off_ref):
    E, per = N_EXPERTS, N_EXPERTS // N_EXPERT_GROUPS
    tm = x_ref.shape[0]
    x = x_ref[...]
    logits = _dot_nt(w_ref[...], x, precision=lax.Precision.HIGHEST)
    scores = jax.nn.sigmoid(logits)
    sel = scores + bias_ref[...]
    neg = -jnp.inf
    i8 = lax.broadcasted_iota(I32, (per, tm), 0)
    gscore = []
    for g in range(N_EXPERT_GROUPS):
        blk = sel[g * per:(g + 1) * per]
        m1 = jnp.max(blk, axis=0, keepdims=True)
        i1 = jnp.min(jnp.where(blk == m1, i8, per), axis=0, keepdims=True)
        m2 = jnp.max(jnp.where(i8 == i1, neg, blk), axis=0, keepdims=True)
        gscore.append(m1 + m2)
    blocks = []
    for g in range(N_EXPERT_GROUPS):
        ahead = jnp.zeros((1, tm), I32)
        for o in range(N_EXPERT_GROUPS):
            if o == g:
                continue
            before = (gscore[o] >= gscore[g]) if o < g else (gscore[o] > gscore[g])
            ahead = ahead + before.astype(I32)
        blocks.append(jnp.where(ahead < TOPK_GROUPS, sel[g * per:(g + 1) * per], neg))
    cur = jnp.concatenate(blocks, axis=0)
    ie = lax.broadcasted_iota(I32, (E, tm), 0)
    picked = jnp.zeros((E, tm), F32)
    idxs = []
    for _ in range(TOP_K):
        m = jnp.max(cur, axis=0, keepdims=True)
        idx = jnp.min(jnp.where(cur == m, ie, E), axis=0, keepdims=True)
        hit = ie == idx
        picked = jnp.where(hit, 1.0, picked)
        cur = jnp.where(hit, neg, cur)
        idxs.append(idx)
    wsel = scores * picked
    wnorm = wsel / jnp.sum(wsel, axis=0, keepdims=True) * ROUTED_SCALE
    pb = picked.astype(BF16)
    tri = (lax.broadcasted_iota(I32, (tm, tm), 0) <= lax.broadcasted_iota(I32, (tm, tm), 1)).astype(BF16)
    incl = _dot(pb, tri)
    cnt_col = jnp.sum(picked, axis=1, keepdims=True)
    lower = (lax.broadcasted_iota(I32, (E, E), 1) < lax.broadcasted_iota(I32, (E, E), 0)).astype(F32)
    al_col = jnp.floor((cnt_col + (SEG_ALIGN - 1.0)) * (1.0 / SEG_ALIGN)) * SEG_ALIGN
    off_col = _dot_exact(lower, jnp.broadcast_to(al_col, (E, LANES)))[:, 0:1]
    pos = off_col + incl - 1.0
    q = jnp.floor(pos * (1.0 / POS_RADIX))
    pq = jnp.concatenate([jnp.where(picked > 0.0, POS_RADIX * q, POS_RADIX * POS_NONE_Q),
                          jnp.where(picked > 0.0, pos - POS_RADIX * q, 0.0)], axis=0)
    pos_ref[...] = pq
    pqb = pq.astype(BF16)
    cnt_ref[0] = jnp.broadcast_to(cnt_col, (E, LANES)).astype(I32)
    off_ref[0] = jnp.broadcast_to(off_col, (E, LANES)).astype(I32)
    cnt_row = _dot_nt(jnp.ones((8, tm), BF16), pb)
    al_row = jnp.floor((cnt_row + (SEG_ALIGN - 1.0)) * (1.0 / SEG_ALIGN)) * SEG_ALIGN
    er = lax.broadcasted_iota(I32, (E, 3 * E), 0)
    ec = lax.broadcasted_iota(I32, (E, 3 * E), 1) % E
    off_row3 = _dot_exact(al_row, (er < ec).astype(F32))[0:1]
    end_row3 = off_row3 + _dot_exact(al_row, (er == ec).astype(F32))[0:1]
    w_hi = wnorm.astype(BF16)
    w_r1 = wnorm - w_hi.astype(F32)
    w_mid = w_r1.astype(BF16)
    w_lo = (w_r1 - w_mid.astype(F32)).astype(BF16)
    w3 = jnp.concatenate([w_hi, w_mid, w_lo], axis=0)

    xb = x.astype(BF16)
    half = D_MODEL // 2

    def build(c, carry):
        r0 = pl.multiple_of(c * OH_ROWS, OH_ROWS)
        jr = (lax.broadcasted_iota(I32, (OH_ROWS, 3 * E), 0) + r0).astype(F32)
        member3 = jnp.where(jr >= off_row3, jnp.where(jr < end_row3, 1.0, 0.0), 0.0)
        target = _dot(member3[:, :2 * E].astype(BF16), pqb)
        j = (lax.broadcasted_iota(I32, (OH_ROWS, tm), 0) + r0).astype(F32)
        ohb = jnp.where(target == j, 1.0, 0.0).astype(BF16)
        rows = _dot(ohb, xb)
        bits = pltpu.bitcast(rows, U32)
        xs_ref[0, pl.ds(r0, OH_ROWS), 0:half] = bits[:, half:] | (bits[:, :half] >> 16)
        wrow = jnp.sum(member3 * _dot_nt(ohb, w3), axis=1, keepdims=True)
        xs_ref[0, pl.ds(r0, OH_ROWS), half:] = pltpu.bitcast(jnp.broadcast_to(wrow, (OH_ROWS, LANES)), U32)
        return carry

    lax.fori_loop(0, SORT_ROWS // OH_ROWS, build, 0)
    xs_ref[0, SORT_ROWS:, :] = jnp.zeros((PAD_ROWS, XW), U32)


def _route_sort(x2, w_router, router_bias):
    T = x2.shape[0]
    E = N_EXPERTS
    nt = T // TW
    seg = pl.BlockSpec((1, E, LANES), lambda i: (i, 0, 0))
    return pl.pallas_call(
        _route_sort_kernel,
        grid=(nt,),
        in_specs=[pl.BlockSpec((TW, D_MODEL), lambda i: (i, 0)), _full((E, D_MODEL)), _full((E, 1))],
        out_specs=[pl.BlockSpec((1, TS, XW), lambda i: (i, 0, 0)),
                   pl.BlockSpec((2 * E, TW), lambda i: (0, i)), seg, seg],
        out_shape=[jax.ShapeDtypeStruct((nt, TS, XW), U32), jax.ShapeDtypeStruct((2 * E, T), F32),
                   jax.ShapeDtypeStruct((nt, E, LANES), I32), jax.ShapeDtypeStruct((nt, E, LANES), I32)],
        compiler_params=_cparams("parallel"),
        name="route_local_sort",
    )(x2, w_router.T, router_bias.reshape(E, 1))


def _plan_kernel(nt, n_pieces_max, n_chunks_max, cnt_ref, off_ref, pa_ref, ce_ref, nch_ref):
    def unused(p, z):
        pa_ref[p] = -SEG_ALIGN
        return z

    def per_expert(e, carry):
        p0, g = carry

        def per_tile(i, p):
            n = cnt_ref[i * N_EXPERTS + e]
            base = i * TS + off_ref[i * N_EXPERTS + e]
            n_pc = (n + PR - 1) // PR

            def put(q, c):
                pa_ref[p + q] = base + q * PR
                return c

            lax.fori_loop(0, n_pc, put, 0)
            return p + n_pc

        p1 = lax.fori_loop(0, nt, per_tile, p0)
        n_ch = (p1 - p0 + PPC - 1) // PPC

        lax.fori_loop(p1, p0 + n_ch * PPC, unused, 0)

        def mark(c, z):
            ce_ref[g + c] = e
            return z

        lax.fori_loop(0, n_ch, mark, 0)
        return p0 + n_ch * PPC, g + n_ch

    p_end, g = lax.fori_loop(0, N_EXPERTS, per_expert, (jnp.int32(0), jnp.int32(0)))
    lax.fori_loop(p_end, n_pieces_max, unused, 0)
    nch_ref[0] = g

    def tail(c, z):
        ce_ref[c] = N_EXPERTS - 1
        return z

    lax.fori_loop(g, n_chunks_max, tail, 0)


def _plan(cnt, off, nt, n_pieces_max, n_chunks_max):
    smem = pl.BlockSpec(memory_space=pltpu.SMEM)
    return pl.pallas_call(
        functools.partial(_plan_kernel, nt, n_pieces_max, n_chunks_max),
        in_specs=[smem, smem],
        out_specs=[smem, smem, smem],
        out_shape=[jax.ShapeDtypeStruct((n_pieces_max,), I32), jax.ShapeDtypeStruct((n_chunks_max,), I32),
                   jax.ShapeDtypeStruct((1,), I32)],
        name="moe_plan",
    )(cnt, off)


def _unpack_bf16_pair(w):
    lo = pltpu.bitcast(w << 16, F32).astype(BF16)
    hi = pltpu.bitcast(w & jnp.uint32(0xFFFF0000), F32).astype(BF16)
    return lo, hi


def _pack_bf16_pair(lo, hi):
    lo_bits = pltpu.bitcast(lo.astype(BF16).astype(F32), U32) >> 16
    hi_bits = pltpu.bitcast(hi.astype(BF16).astype(F32), U32)
    return hi_bits | lo_bits


def _expert_kernel(ce_ref, pa_ref, nch_ref, tot_ref, xs_hbm, wgu_ref, wdn_ref, y_hbm,
                   xbuf, ybuf, zbuf, wgu_b, wdn_b, gsem, wsem, zsem):
    g = pl.program_id(0)
    n = nch_ref[0]
    slot = g % N_SLOTS
    half = D_MODEL // 2

    dummy_base = (tot_ref.shape[0] - 1) * TS

    def gather_copy(s, a, jp):
        return pltpu.make_async_copy(xs_hbm.at[pl.ds(pl.multiple_of(a, SEG_ALIGN), PR), :],
                                     xbuf.at[s, pl.ds(jp * PR, PR), :], gsem.at[s])

    def write_copy(s, a, jp):
        return pltpu.make_async_copy(ybuf.at[s, pl.ds(jp * PR, PR), :],
                                     y_hbm.at[pl.ds(pl.multiple_of(a, SEG_ALIGN), PR), :], wsem.at[s])

    def gather_start(c, s):
        for jp in range(PPC):
            gather_copy(s, jnp.maximum(pa_ref[c * PPC + jp], 0), jp).start()

    def write_start(c, s):
        for jp in range(PPC):
            a = pa_ref[c * PPC + jp]
            write_copy(s, jnp.where(a >= 0, a, dummy_base + jp * PR), jp).start()

    def gather_wait(s):
        for jp in range(PPC):
            gather_copy(s, 0, jp).wait()

    def write_wait(s):
        for jp in range(PPC):
            write_copy(s, 0, jp).wait()

    @pl.when(g == 0)
    def _():
        gather_start(0, 0)
        zbuf[...] = jnp.zeros_like(zbuf)

        def zero_copy(row):
            return pltpu.make_async_copy(zbuf, y_hbm.at[pl.ds(pl.multiple_of(row, SEG_ALIGN), SEG_ALIGN), :], zsem)

        def tail_of(i):
            return i * TS + tot_ref[i], (TS - tot_ref[i]) // SEG_ALIGN

        def zero_start(i, c):
            base, cnt8 = tail_of(i)
            lax.fori_loop(0, cnt8, lambda q, z: (zero_copy(base + q * SEG_ALIGN).start(), z)[1], 0)
            return c

        def zero_wait(i, c):
            base, cnt8 = tail_of(i)
            lax.fori_loop(0, cnt8, lambda q, z: (zero_copy(base + q * SEG_ALIGN).wait(), z)[1], 0)
            return c

        lax.fori_loop(0, tot_ref.shape[0], zero_start, 0)
        lax.fori_loop(0, tot_ref.shape[0], zero_wait, 0)

    @pl.when((g == 0) & (n > 1))
    def _():
        gather_start(1, 1)

    @pl.when(g + GATHER_AHEAD < n)
    def _():
        gather_start(g + GATHER_AHEAD, (g + GATHER_AHEAD) % N_SLOTS)

    e_cur = ce_ref[g]
    e_prev = ce_ref[jnp.maximum(g - 1, 0)]
    e_prev2 = ce_ref[jnp.maximum(g - 2, 0)]

    @pl.when(g < n)
    def _():
        @pl.when((g == 0) | (e_cur != e_prev))
        def _():
            wgu_b[...] = wgu_ref[0].astype(BF16)
            wdn_b[...] = wdn_ref[0].astype(BF16)

        gather_wait(slot)
        xw = xbuf[slot]
        lo, hi = _unpack_bf16_pair(xw[:, :half])
        h = _dot(lo, wgu_b[:half, :]) + _dot(hi, wgu_b[half:, :])
        gate, up = h[:, :EXPERT_FF], h[:, EXPERT_FF:]
        act = (gate * jax.nn.sigmoid(gate) * up).astype(BF16)
        y = _dot(act, wdn_b[...])
        w_row = pltpu.bitcast(xw[:, half:], F32)
        y = y * jnp.concatenate([w_row] * (D_MODEL // LANES), axis=1)

        @pl.when((g >= 2) & (e_prev == e_prev2))
        def _():
            write_wait((g - 2) % N_SLOTS)

        @pl.when((g >= 1) & (e_cur != e_prev))
        def _():
            write_wait((g - 1) % N_SLOTS)

        ybuf[slot] = _pack_bf16_pair(y[:, :half], y[:, half:])
        write_start(g, slot)

    @pl.when(g == pl.num_programs(0) - 1)
    def _():
        @pl.when((n >= 2) & (ce_ref[jnp.maximum(n - 1, 0)] == ce_ref[jnp.maximum(n - 2, 0)]))
        def _():
            write_wait((n - 2) % N_SLOTS)

        write_wait((n - 1) % N_SLOTS)


def _experts(ce, pa, nch, tot, xs, w_exp_gu, w_exp_down):
    n_rows = xs.shape[0]
    half = D_MODEL // 2
    grid_spec = pltpu.PrefetchScalarGridSpec(
        num_scalar_prefetch=4,
        grid=(ce.shape[0],),
        in_specs=[pl.BlockSpec(memory_space=pl.ANY),
                  pl.BlockSpec((1, D_MODEL, 2 * EXPERT_FF), lambda g, ce, pa, nch, tot: (ce[g], 0, 0)),
                  pl.BlockSpec((1, EXPERT_FF, D_MODEL), lambda g, ce, pa, nch, tot: (ce[g], 0, 0))],
        out_specs=pl.BlockSpec(memory_space=pl.ANY),
        scratch_shapes=[pltpu.VMEM((N_SLOTS, CR, XW), U32), pltpu.VMEM((N_SLOTS, CR, half), U32),
                        pltpu.VMEM((SEG_ALIGN, half), U32),
                        pltpu.VMEM((D_MODEL, 2 * EXPERT_FF), BF16), pltpu.VMEM((EXPERT_FF, D_MODEL), BF16),
                        pltpu.SemaphoreType.DMA((N_SLOTS,)), pltpu.SemaphoreType.DMA((N_SLOTS,)),
                        pltpu.SemaphoreType.DMA(())],
    )
    return pl.pallas_call(
        _expert_kernel,
        grid_spec=grid_spec,
        out_shape=jax.ShapeDtypeStruct((n_rows + TS, half), U32),
        compiler_params=_cparams("arbitrary"),
        name="moe_grouped_swiglu",
    )(ce, pa, nch, tot, xs, w_exp_gu, w_exp_down)


def _combine_kernel(tot_ref, y_ref, pos_ref, cnt_ref, off_ref, x2_ref, wsg_ref, wsd_ref, g_ref, b_ref, out_ref,
                    acc_lo, acc_hi):
    tm = x2_ref.shape[0]
    E = N_EXPERTS
    n_sorted = tot_ref[pl.program_id(0)]
    x2 = x2_ref[...]
    h = _dot(x2.astype(BF16), wsg_ref[...])
    gate, up = h[:, :SHARED_FF], h[:, SHARED_FF:]
    shared = _dot((gate * jax.nn.sigmoid(gate) * up).astype(BF16), wsd_ref[...])
    pq_t = pos_ref[...].T.astype(BF16)
    off_col = off_ref[0][:, 0:1].astype(F32)
    cnt_col = cnt_ref[0][:, 0:1].astype(F32)
    end_col = off_col + jnp.floor((cnt_col + (SEG_ALIGN - 1.0)) * (1.0 / SEG_ALIGN)) * SEG_ALIGN
    off_col2 = jnp.concatenate([off_col, off_col], axis=0)
    end_col2 = jnp.concatenate([end_col, end_col], axis=0)
    acc_lo[...] = jnp.zeros_like(acc_lo)
    acc_hi[...] = jnp.zeros_like(acc_hi)

    def gather_back(c, carry):
        r0 = pl.multiple_of(c * OH_ROWS, OH_ROWS)
        jc = (lax.broadcasted_iota(I32, (2 * E, OH_ROWS), 1) + r0).astype(F32)
        member2 = jnp.where(jc >= off_col2, jnp.where(jc < end_col2, 1.0, 0.0), 0.0)
        target = _dot(pq_t, member2.astype(BF16))
        j = (lax.broadcasted_iota(I32, (tm, OH_ROWS), 1) + r0).astype(F32)
        ohb = jnp.where(target == j, 1.0, 0.0).astype(BF16)
        yw = y_ref[0, pl.ds(r0, OH_ROWS), :]
        row = lax.broadcasted_iota(I32, yw.shape, 0) + r0
        lo, hi = _unpack_bf16_pair(jnp.where(row < n_sorted, yw, jnp.uint32(0)))
        acc_lo[...] += _dot(ohb, lo)
        acc_hi[...] += _dot(ohb, hi)
        return carry

    lax.fori_loop(0, SORT_ROWS // OH_ROWS, gather_back, 0)
    ff = shared + jnp.concatenate([acc_lo[...], acc_hi[...]], axis=1)
    out_ref[...] = _layer_norm(DEEPNORM_ALPHA * x2 + ff, g_ref[...], b_ref[...])


def _combine(tot, y, pos, cnt, off, x2, w_sh_gu, w_sh_down, ln_g, ln_b):
    T = x2.shape[0]
    nt = T // TW
    half = D_MODEL // 2
    tile = lambda w: pl.BlockSpec((TW, w), lambda i: (i, 0))
    return pl.pallas_call(
        _combine_kernel,
        grid=(nt,),
        in_specs=[pl.BlockSpec(memory_space=pltpu.SMEM),
                  pl.BlockSpec((1, TS, half), lambda i: (i, 0, 0)),
                  pl.BlockSpec((2 * N_EXPERTS, TW), lambda i: (0, i)),
                  pl.BlockSpec((1, N_EXPERTS, LANES), lambda i: (i, 0, 0)),
                  pl.BlockSpec((1, N_EXPERTS, LANES), lambda i: (i, 0, 0)), tile(D_MODEL),
                  _full((D_MODEL, 2 * SHARED_FF)), _full((SHARED_FF, D_MODEL)),
                  _full((1, D_MODEL)), _full((1, D_MODEL))],
        out_specs=tile(D_MODEL),
        out_shape=jax.ShapeDtypeStruct((T, D_MODEL), F32),
        scratch_shapes=[pltpu.VMEM((TW, half), F32), pltpu.VMEM((TW, half), F32)],
        compiler_params=_cparams("parallel"),
        name="moe_combine_shared_ln3",
    )(tot, y.reshape(nt + 1, TS, half), pos, cnt, off, x2, w_sh_gu.astype(BF16), w_sh_down.astype(BF16),
      ln_g.reshape(1, -1), ln_b.reshape(1, -1))


def _moe(x2, w_router, router_bias, w_exp_gu, w_exp_down, w_sh_gu, w_sh_down, ln_g, ln_b):
    T = x2.shape[0]
    E = N_EXPERTS
    nt = T // TW
    xs, pos, cnt, off = _route_sort(x2, w_router, router_bias)
    n_pieces_max = (T * TOP_K + nt * E * (PR - 1)) // PR
    n_chunks_max = n_pieces_max // PPC + E
    pa, ce, nch = _plan(cnt[:, :, 0].reshape(nt * E), off[:, :, 0].reshape(nt * E), nt,
                        n_chunks_max * PPC, n_chunks_max)
    last_al = (cnt[:, E - 1, 0] + SEG_ALIGN - 1) // SEG_ALIGN * SEG_ALIGN
    tot = jnp.concatenate([off[:, E - 1, 0] + last_al, jnp.zeros((1,), I32)])
    y = _experts(ce, pa, nch, tot, xs.reshape(nt * TS, XW), w_exp_gu, w_exp_down)
    return _combine(tot, y, pos, cnt, off, x2, w_sh_gu, w_sh_down, ln_g, ln_b)


def kernel(x, mem, positions, w_in, q_norm_g, kv_norm_g, w_uq, w_ukv, w_mla_o, s5_a_re, s5_a_im, s5_log_dt, s5_b_re, s5_b_im, s5_c_re, s5_c_im, s5_d, w_s5_glu, w_out, ln1_g, ln1_b, mem_ln_g, mem_ln_b, w_xq, w_xkv, w_xo, ln2_g, ln2_b, w_router, router_bias, w_exp_gu, w_exp_down, w_sh_gu, w_sh_down, ln3_g, ln3_b):
    B, S, D = x.shape
    assert D == D_MODEL and S % TQ == 0 and (B * S) % TM == 0 and S % S5_CHUNK == 0
    xc = x.reshape(B * S, D)
    for l in range(w_in.shape[0]):
        tabs = _rope_tables(positions)
        q, k, v, u, gm, gs = _inproj(xc, tabs, w_in[l], q_norm_g[l], kv_norm_g[l], w_uq[l], w_ukv[l])
        o = _attention(q, k, v, B, S)
        ys = _s5_branch(u, B, S, (s5_a_re[l], s5_a_im[l], s5_log_dt[l], s5_b_re[l], s5_b_im[l],
                                  s5_c_re[l], s5_c_im[l], s5_d[l]))
        x1 = _merge(xc, o, ys, gm, gs, w_mla_o[l], w_s5_glu[l], w_out[l], ln1_g[l], ln1_b[l])
        x2 = _cross_attention(x1, mem, B, S, mem_ln_g[l], mem_ln_b[l], w_xq[l], w_xkv[l], w_xo[l],
                                   ln2_g[l], ln2_b[l])
        xc = _moe(x2, w_router[l], router_bias[l], w_exp_gu[l], w_exp_down[l], w_sh_gu[l],
                  w_sh_down[l], ln3_g[l], ln3_b[l])
    return xc.reshape(B, S, D)
```

```python
import functools
import math

import jax
import jax.numpy as jnp
from jax import lax
from jax.experimental import pallas as pl
from jax.experimental.pallas import tpu as pltpu

F32 = jnp.float32
BF16 = jnp.bfloat16
I32 = jnp.int32
U32 = jnp.uint32

D_MODEL = 1024
MLA_HEADS = 8
QK_NOPE = 64
QK_ROPE = 32
V_HEAD = 64
Q_LORA = 256
KV_LORA = 256
ROPE_THETA = 10000.0
S5_GROUP_CH = 16
S5_WIDTH = 512
S5_GROUPS = 32
S5_STATE = 64
XATTN_HEADS = 4
XATTN_HEAD_DIM = 128
N_EXPERTS = 64
TOP_K = 8
N_EXPERT_GROUPS = 8
TOPK_GROUPS = 4
EXPERT_FF = 256
SHARED_FF = 256
ROUTED_SCALE = 2.5
LN_EPS = 1e-5
RMS_EPS = 1e-6
DEPTH = 1
DEEPNORM_ALPHA = (2.0 * DEPTH) ** 0.25

LANES = 128
HEAD_PAD = 128
ROPE_LO = QK_NOPE
ROPE_HALF = QK_ROPE // 2

TM = 512
TQ = 512
S5_CHUNK = 16
TW = 512
OH_ROWS = 512
PR = 16
PAD_ROWS = PR
SEG_ALIGN = 8
SORT_ROWS = TOP_K * TW + N_EXPERTS * SEG_ALIGN
TS = SORT_ROWS + PAD_ROWS
XW = D_MODEL // 2 + LANES
CR = 512
PPC = CR // PR
GATHER_AHEAD = 2
N_SLOTS = GATHER_AHEAD + 1
POS_RADIX = 64
POS_NONE_Q = 127
assert SORT_ROWS <= POS_RADIX * POS_NONE_Q and SORT_ROWS % OH_ROWS == 0
VMEM_LIMIT = 48 * 1024 * 1024


def _cparams(*sem):
    return pltpu.CompilerParams(dimension_semantics=sem, vmem_limit_bytes=VMEM_LIMIT)


def _dot(a, b):
    return jnp.dot(a, b, preferred_element_type=F32)


def _dot_nt(a, b, precision=None):
    return lax.dot_general(a, b, (((1,), (1,)), ((), ())), preferred_element_type=F32,
                           precision=precision)


def _dot_exact(a, b):
    return jnp.dot(a, b, preferred_element_type=F32, precision=lax.Precision.HIGHEST)


def _layer_norm(h, g, b):
    mu = jnp.mean(h, axis=-1, keepdims=True)
    c = h - mu
    var = jnp.mean(c * c, axis=-1, keepdims=True)
    return c * lax.rsqrt(var + LN_EPS) * g + b


def _rms_norm(h, g):
    return h * lax.rsqrt(jnp.mean(h * h, axis=-1, keepdims=True) + RMS_EPS) * g


def _full(shape):
    n = len(shape)
    return pl.BlockSpec(shape, lambda *_: (0,) * n)


def _rope_angle_kernel(pos_ref, invf_ref, cos_ref, sin_ref):
    ang = pos_ref[...].astype(F32) * invf_ref[...]
    cos_ref[...] = jnp.cos(ang)
    sin_ref[...] = jnp.sin(ang)


def _rope_tables(positions):
    T = positions.size
    rows = T * ROPE_HALF // LANES
    pos_rep = jnp.repeat(positions.reshape(T), ROPE_HALF).reshape(rows, LANES)
    inv_freq = ROPE_THETA ** (-jnp.arange(0, QK_ROPE, 2, dtype=F32) / QK_ROPE)
    invf = jnp.tile(inv_freq, LANES // ROPE_HALF).reshape(1, LANES)
    cos, sin = pl.pallas_call(
        _rope_angle_kernel,
        out_shape=(jax.ShapeDtypeStruct((rows, LANES), F32),) * 2,
        name="rope_angles",
    )(pos_rep, invf)
    cos = cos.reshape(T, ROPE_HALF)
    sin = sin.reshape(T, ROPE_HALF)
    one = jnp.ones((T, ROPE_LO), F32)
    zero_lo = jnp.zeros((T, ROPE_LO), F32)
    zero_h = jnp.zeros((T, ROPE_HALF), F32)
    tail1 = jnp.ones((T, HEAD_PAD - ROPE_LO - QK_ROPE), F32)
    tail0 = jnp.zeros((T, HEAD_PAD - ROPE_LO - QK_ROPE), F32)
    c_tab = jnp.concatenate([one, cos, cos, tail1], axis=1)
    s_up = jnp.concatenate([zero_lo, -sin, zero_h, tail0], axis=1)
    s_dn = jnp.concatenate([zero_lo, zero_h, sin, tail0], axis=1)
    return c_tab, s_up, s_dn


def _rope(x, c_tab, s_up, s_dn):
    return (x * c_tab + pltpu.roll(x, HEAD_PAD - ROPE_HALF, axis=1) * s_up
            + pltpu.roll(x, ROPE_HALF, axis=1) * s_dn)


def _inproj_kernel(x_ref, c_ref, su_ref, sd_ref, wlat_ref, wu_ref, wgm_ref, wgs_ref, qg_ref, kvg_ref,
                   wuq_ref, wuk_ref, wuv_ref, q_ref, k_ref, v_ref, u_ref, gm_ref, gs_ref):
    xb = x_ref[...].astype(BF16)
    lat = _dot(xb, wlat_ref[...])
    qn = _rms_norm(lat[:, :Q_LORA], qg_ref[...]).astype(BF16)
    kvn = _rms_norm(lat[:, Q_LORA:Q_LORA + KV_LORA], kvg_ref[...]).astype(BF16)
    c_tab, s_up, s_dn = c_ref[...], su_ref[...], sd_ref[...]
    k_rope = _rope(lat[:, Q_LORA + KV_LORA:], c_tab, s_up, s_dn)
    q = _dot(qn, wuq_ref[...])
    k = _dot(kvn, wuk_ref[...])
    scale = (QK_NOPE + QK_ROPE) ** -0.5 * math.log2(math.e)
    for h in range(MLA_HEADS):
        sl = slice(h * HEAD_PAD, (h + 1) * HEAD_PAD)
        q_ref[:, sl] = (_rope(q[:, sl], c_tab, s_up, s_dn) * scale).astype(BF16)
        k_ref[:, sl] = (k[:, sl] + k_rope).astype(BF16)
    v = _dot(kvn, wuv_ref[...])
    ones_lane = lax.broadcasted_iota(I32, (1, v.shape[1]), 1) % HEAD_PAD == V_HEAD
    v_ref[...] = jnp.where(ones_lane, 1.0, v).astype(BF16)
    u_ref[...] = _dot(xb, wu_ref[...]).astype(BF16)
    gm_ref[...] = _dot(xb, wgm_ref[...]).astype(BF16)
    gs_ref[...] = _dot(xb, wgs_ref[...]).astype(BF16)


def _pad_heads(w, head_w, lo_w):
    K = w.shape[0]
    w = w.reshape(K, MLA_HEADS, head_w)[:, :, :lo_w]
    w = jnp.pad(w, ((0, 0), (0, 0), (0, HEAD_PAD - lo_w)))
    return w.reshape(K, MLA_HEADS * HEAD_PAD)


def _inproj(x2d, tabs, w_in, q_norm_g, kv_norm_g, w_uq, w_ukv):
    T = x2d.shape[0]
    o_rope = Q_LORA + KV_LORA
    o_u = o_rope + QK_ROPE
    o_gm = o_u + S5_WIDTH
    o_gs = o_gm + D_MODEL
    w_rope = jnp.pad(w_in[:, o_rope:o_u], ((0, 0), (ROPE_LO, HEAD_PAD - ROPE_LO - QK_ROPE)))
    w_lat = jnp.concatenate([w_in[:, :o_rope], w_rope], axis=1).astype(BF16)
    w_u = w_in[:, o_u:o_gm].astype(BF16)
    w_gm = w_in[:, o_gm:o_gs].astype(BF16)
    w_gs = w_in[:, o_gs:].astype(BF16)
    wuq = _pad_heads(w_uq, QK_NOPE + QK_ROPE, QK_NOPE + QK_ROPE).astype(BF16)
    kv3 = w_ukv.reshape(KV_LORA, MLA_HEADS, QK_NOPE + V_HEAD)
    wuk = _pad_heads(kv3[:, :, :QK_NOPE].reshape(KV_LORA, -1), QK_NOPE, QK_NOPE).astype(BF16)
    wuv = _pad_heads(kv3[:, :, QK_NOPE:].reshape(KV_LORA, -1), V_HEAD, V_HEAD).astype(BF16)
    HP = MLA_HEADS * HEAD_PAD
    tile = lambda w: pl.BlockSpec((TM, w), lambda i: (i, 0))
    return pl.pallas_call(
        _inproj_kernel,
        grid=(T // TM,),
        in_specs=[tile(D_MODEL), tile(HEAD_PAD), tile(HEAD_PAD), tile(HEAD_PAD),
                  _full(w_lat.shape), _full(w_u.shape), _full(w_gm.shape), _full(w_gs.shape),
                  _full((1, Q_LORA)), _full((1, KV_LORA)),
                  _full(wuq.shape), _full(wuk.shape), _full(wuv.shape)],
        out_specs=[tile(HP), tile(HP), tile(HP), tile(S5_WIDTH), tile(D_MODEL), tile(D_MODEL)],
        out_shape=[jax.ShapeDtypeStruct((T, HP), BF16)] * 3
        + [jax.ShapeDtypeStruct((T, S5_WIDTH), BF16)]
        + [jax.ShapeDtypeStruct((T, D_MODEL), BF16)] * 2,
        compiler_params=_cparams("parallel"),
        name="inproj_mla_prep",
    )(x2d, *tabs, w_lat, w_u, w_gm, w_gs, q_norm_g.reshape(1, -1), kv_norm_g.reshape(1, -1),
      wuq, wuk, wuv)


def _attn_kernel(q_ref, k_ref, v_ref, o_ref):
    qi = pl.program_id(2)
    q = q_ref[0]

    def step(j, carry, masked):
        m, acc = carry
        start = pl.multiple_of(j * TQ, TQ)
        k = k_ref[0, pl.ds(start, TQ), :]
        v = v_ref[0, pl.ds(start, TQ), :]
        s = _dot_nt(q, k)
        if masked:
            row = lax.broadcasted_iota(I32, (TQ, TQ), 0)
            col = lax.broadcasted_iota(I32, (TQ, TQ), 1)
            s = jnp.where(col <= row, s, -jnp.inf)
        m_new = jnp.maximum(m, jnp.max(s, axis=-1, keepdims=True))
        p = jnp.exp2(s - m_new)
        acc = jnp.exp2(m - m_new) * acc + _dot(p.astype(BF16), v)
        return m_new, acc

    init = (jnp.full((TQ, 1), -jnp.inf, F32), jnp.zeros((TQ, HEAD_PAD), F32))
    carry = lax.fori_loop(0, qi, lambda j, c: step(j, c, False), init)
    m, acc = step(qi, carry, True)
    o_ref[0] = (acc / acc[:, V_HEAD:V_HEAD + 1]).astype(BF16)


def _attention(q, k, v, B, S):
    HP = MLA_HEADS * HEAD_PAD
    q, k, v = (a.reshape(B, S, HP) for a in (q, k, v))
    o = pl.pallas_call(
        _attn_kernel,
        grid=(B, MLA_HEADS, S // TQ),
        in_specs=[pl.BlockSpec((1, TQ, HEAD_PAD), lambda b, h, i: (b, i, h)),
                  pl.BlockSpec((1, S, HEAD_PAD), lambda b, h, i: (b, 0, h)),
                  pl.BlockSpec((1, S, HEAD_PAD), lambda b, h, i: (b, 0, h))],
        out_specs=pl.BlockSpec((1, TQ, HEAD_PAD), lambda b, h, i: (b, i, h)),
        out_shape=jax.ShapeDtypeStruct((B, S, HP), BF16),
        compiler_params=_cparams("parallel", "parallel", "arbitrary"),
        name="mla_flash_attention",
    )(q, k, v)
    return o.reshape(B * S, HP)


CH2 = S5_CHUNK * S5_GROUP_CH
P2 = 2 * S5_STATE
SCAN_LEVELS_MAX = 16


def _s5_prep_kernel(n_levels, arow_ref, acol_ref, bt_ref, ct_ref, d_ref,
                    mt_ref, wt_ref, vt_ref, apow_ref, dvec_ref, mt_acc):
    P, H, L = S5_STATE, S5_GROUP_CH, S5_CHUNK

    def powers(ar, ai, dt, n):
        e = jnp.exp(n * (ar * dt))
        ang = n * (ai * dt)
        return e * jnp.cos(ang), e * jnp.sin(ang)

    def zoh_coef(ar, ai, dt):
        pr, pi = powers(ar, ai, dt, 1.0)
        nr, ni = pr - 1.0, pi
        den = ar * ar + ai * ai
        return (nr * ar + ni * ai) / den, (ni * ar - nr * ai) / den

    row = arow_ref[0]
    ar, ai, dt = row[0:1], row[1:2], jnp.exp(row[2:3])
    cr, ci = zoh_coef(ar, ai, dt)
    bt = bt_ref[0]
    btr = jnp.concatenate([bt[:H], bt[:H]], axis=1)
    bti = jnp.concatenate([bt[H:], bt[H:]], axis=1)
    bbr = cr * btr - ci * bti
    bbi = cr * bti + ci * btr
    lane = lax.broadcasted_iota(I32, (1, P2), 1)
    is_re = lane < P
    ridx = lax.broadcasted_iota(I32, (CH2, P2), 0) // H
    pwr, pwi = powers(ar, ai, dt, (L - 1 - ridx).astype(F32))
    rsel = (lax.broadcasted_iota(I32, (CH2, H), 0) % H == lax.broadcasted_iota(I32, (CH2, H), 1)).astype(F32)
    bbr_t = _dot_exact(rsel, bbr)
    bbi_t = _dot_exact(rsel, bbi)
    w_re = pwr * bbr_t - pwi * bbi_t
    w_im = pwr * bbi_t + pwi * bbr_t
    wt_ref[0] = jnp.where(is_re, w_re, w_im).astype(BF16)
    a_r, a_i = powers(ar, ai, dt, float(L))
    sign = jnp.where(is_re, -1.0, 1.0)
    for lvl in range(SCAN_LEVELS_MAX):
        if lvl < n_levels:
            apow_ref[0, lvl:lvl + 1, :] = a_r
            apow_ref[0, SCAN_LEVELS_MAX + lvl:SCAN_LEVELS_MAX + lvl + 1, :] = sign * a_i
            a_r, a_i = a_r * a_r - a_i * a_i, 2.0 * a_r * a_i
        else:
            apow_ref[0, lvl:lvl + 1, :] = jnp.zeros_like(a_r)
            apow_ref[0, SCAN_LEVELS_MAX + lvl:SCAN_LEVELS_MAX + lvl + 1, :] = jnp.zeros_like(a_r)

    col = acol_ref[0]
    arc, aic, dtc = col[:, 0:1], col[:, 1:2], jnp.exp(col[:, 2:3])
    ct = ct_ref[0]
    csel = (lax.broadcasted_iota(I32, (H, CH2), 1) % H == lax.broadcasted_iota(I32, (H, CH2), 0)).astype(F32)
    ctr = _dot_exact(ct[:, :H], csel)
    cti = _dot_exact(ct[:, H:], csel)
    sidx = (lax.broadcasted_iota(I32, (P, CH2), 1) // H).astype(F32)
    pr, pi = powers(arc, aic, dtc, sidx)
    g_r = ctr * pr - cti * pi
    g_i = ctr * pi + cti * pr
    a1r, a1i = powers(arc, aic, dtc, 1.0)
    v_r = g_r * a1r - g_i * a1i
    v_i = g_r * a1i + g_i * a1r
    vt_ref[0, :P, :] = v_r.astype(BF16)
    vt_ref[0, P:, :] = (-v_i).astype(BF16)
    bbr64, bbi64 = bbr[:, :P], bbi[:, :P]
    kt = _dot_exact(bbr64, g_r) - _dot_exact(bbi64, g_i)
    lane2 = lax.broadcasted_iota(I32, (H, CH2), 1)
    for r in range(L):
        blk = kt if r == 0 else jnp.where(lane2 >= r * H, pltpu.roll(kt, r * H, axis=1), 0.0)
        mt_acc[r * H:(r + 1) * H, :] = blk
    mt_ref[0] = mt_acc[...].astype(BF16)
    dvec_ref[0] = _dot_exact(d_ref[0], csel)


def _s5_prep(n_levels, s5_a_re, s5_a_im, s5_log_dt, s5_b_re, s5_b_im, s5_c_re, s5_c_im, s5_d):
    G, P, H = S5_GROUPS, S5_STATE, S5_GROUP_CH
    ldt = jnp.broadcast_to(s5_log_dt[:, None], (G, P))
    arow = jnp.stack([s5_a_re, s5_a_im, ldt], axis=1)
    arow = jnp.concatenate([arow, arow], axis=2)
    acol = jnp.stack([s5_a_re, s5_a_im, ldt], axis=2)
    bt = jnp.concatenate([s5_b_re.transpose(0, 2, 1), s5_b_im.transpose(0, 2, 1)], axis=1)
    ct = jnp.concatenate([s5_c_re.transpose(0, 2, 1), s5_c_im.transpose(0, 2, 1)], axis=2)
    d = s5_d.reshape(G, 1, H)
    blk = lambda *s: pl.BlockSpec((1,) + s, lambda g: (g, 0, 0))
    return pl.pallas_call(
        functools.partial(_s5_prep_kernel, n_levels),
        grid=(G,),
        in_specs=[blk(3, P2), blk(P, 3), blk(2 * H, P), blk(P, 2 * H), blk(1, H)],
        out_specs=[blk(CH2, CH2), blk(CH2, P2), blk(P2, CH2), blk(2 * SCAN_LEVELS_MAX, P2), blk(1, CH2)],
        out_shape=[jax.ShapeDtypeStruct((G, CH2, CH2), BF16),
                   jax.ShapeDtypeStruct((G, CH2, P2), BF16),
                   jax.ShapeDtypeStruct((G, P2, CH2), BF16),
                   jax.ShapeDtypeStruct((G, 2 * SCAN_LEVELS_MAX, P2), F32),
                   jax.ShapeDtypeStruct((G, 1, CH2), F32)],
        scratch_shapes=[pltpu.VMEM((CH2, CH2), F32)],
        compiler_params=_cparams("parallel"),
        name="s5_discretise",
    )(arow, acol, bt, ct, d)


def _gelu_tanh(y):
    return 0.5 * y * (1.0 + jnp.tanh(math.sqrt(2.0 / math.pi) * (y + 0.044715 * (y * y * y))))


def _s5_scan_kernel(n_levels, batch, u_ref, mt_ref, wt_ref, vt_ref, apow_ref, dvec_ref, y_ref):
    u = u_ref[0]
    rows = u.shape[0]
    x = _dot(u, wt_ref[0])
    ridx = lax.broadcasted_iota(I32, (rows, P2), 0)
    apow = apow_ref[0]
    for lvl in range(n_levels):
        sh = batch * (1 << lvl)
        prev = jnp.where(ridx >= sh, pltpu.roll(x, sh, axis=0), 0.0)
        a_r = apow[lvl:lvl + 1]
        a_i = apow[SCAN_LEVELS_MAX + lvl:SCAN_LEVELS_MAX + lvl + 1]
        x = x + a_r * prev + a_i * pltpu.roll(prev, S5_STATE, axis=1)
    x_in = jnp.where(ridx >= batch, pltpu.roll(x, batch, axis=0), 0.0)
    y = _dot(u, mt_ref[0]) + _dot(x_in.astype(BF16), vt_ref[0]) + dvec_ref[0] * u.astype(F32)
    y_ref[0] = _gelu_tanh(y).astype(BF16)


def _s5_branch(u, B, S, s5_params):
    G, H, L = S5_GROUPS, S5_GROUP_CH, S5_CHUNK
    nc = S // L
    n_levels = max(1, (nc - 1).bit_length())
    assert n_levels <= SCAN_LEVELS_MAX
    mt, wt, vt, apow, dvec = _s5_prep(n_levels, *s5_params)
    rows = nc * B
    ug = u.reshape(B, nc, L, G, H).transpose(3, 1, 0, 2, 4).reshape(G, rows, CH2)
    blk = lambda *s: pl.BlockSpec((1,) + s, lambda g: (g, 0, 0))
    yg = pl.pallas_call(
        functools.partial(_s5_scan_kernel, n_levels, B),
        grid=(G,),
        in_specs=[blk(rows, CH2), blk(CH2, CH2), blk(CH2, P2), blk(P2, CH2),
                  blk(2 * SCAN_LEVELS_MAX, P2), blk(1, CH2)],
        out_specs=blk(rows, CH2),
        out_shape=jax.ShapeDtypeStruct((G, rows, CH2), BF16),
        compiler_params=_cparams("parallel"),
        name="s5_chunk_scan",
    )(ug, mt, wt, vt, apow, dvec)
    return yg.reshape(G, nc, B, L, H).transpose(2, 1, 3, 0, 4).reshape(B * S, S5_WIDTH)


def _merge_kernel(x_ref, o_ref, ys_ref, gm_ref, gs_ref, wmo_ref, wglu_ref, wout_ref, g_ref, b_ref, x1_ref):
    y_mla = _dot(o_ref[...], wmo_ref[...])
    z = _dot(ys_ref[...], wglu_ref[...])
    y_s5 = z[:, :D_MODEL] * jax.nn.sigmoid(z[:, D_MODEL:])
    merged = (jax.nn.sigmoid(gm_ref[...].astype(F32)) * y_mla
              + jax.nn.sigmoid(gs_ref[...].astype(F32)) * y_s5)
    mix = _dot(merged.astype(BF16), wout_ref[...])
    x1_ref[...] = _layer_norm(DEEPNORM_ALPHA * x_ref[...] + mix, g_ref[...], b_ref[...])


def _merge(x2d, o, ys, gm, gs, w_mla_o, w_s5_glu, w_out, ln_g, ln_b):
    T = x2d.shape[0]
    HP = MLA_HEADS * HEAD_PAD
    wmo = jnp.pad(w_mla_o.reshape(MLA_HEADS, V_HEAD, D_MODEL), ((0, 0), (0, HEAD_PAD - V_HEAD), (0, 0)))
    wmo = wmo.reshape(HP, D_MODEL).astype(BF16)
    tile = lambda w: pl.BlockSpec((TM, w), lambda i: (i, 0))
    return pl.pallas_call(
        _merge_kernel,
        grid=(T // TM,),
        in_specs=[tile(D_MODEL), tile(HP), tile(S5_WIDTH), tile(D_MODEL), tile(D_MODEL),
                  _full((HP, D_MODEL)), _full((S5_WIDTH, 2 * D_MODEL)), _full((D_MODEL, D_MODEL)),
                  _full((1, D_MODEL)), _full((1, D_MODEL))],
        out_specs=tile(D_MODEL),
        out_shape=jax.ShapeDtypeStruct((T, D_MODEL), F32),
        compiler_params=_cparams("parallel"),
        name="merge_outproj_ln1",
    )(x2d, o, ys, gm, gs, wmo, w_s5_glu.astype(BF16), w_out.astype(BF16),
      ln_g.reshape(1, -1), ln_b.reshape(1, -1))


def _memkv_kernel(mem_ref, g_ref, b_ref, w_ref, k_ref, v_ref):
    m = _layer_norm(mem_ref[0], g_ref[...], b_ref[...]).astype(BF16)
    kv = _dot(m, w_ref[...])
    hd = XATTN_HEADS * XATTN_HEAD_DIM
    k_ref[0] = kv[:, :hd].astype(BF16)
    v_ref[0] = kv[:, hd:].astype(BF16)


def _xattn_kernel(x1_ref, k_ref, v_ref, wq_ref, wo_ref, g_ref, b_ref, x2_ref):
    x1 = x1_ref[0]
    q = (_dot(x1.astype(BF16), wq_ref[...]) * (XATTN_HEAD_DIM ** -0.5)).astype(BF16)
    k = k_ref[0]
    v = v_ref[0]
    outs = []
    for h in range(XATTN_HEADS):
        sl = slice(h * XATTN_HEAD_DIM, (h + 1) * XATTN_HEAD_DIM)
        s = _dot_nt(q[:, sl], k[:, sl])
        p = jnp.exp(s - jnp.max(s, axis=-1, keepdims=True))
        o = _dot(p.astype(BF16), v[:, sl]) / jnp.sum(p, axis=-1, keepdims=True)
        outs.append(o.astype(BF16))
    xa = _dot(jnp.concatenate(outs, axis=1), wo_ref[...])
    x2 = _layer_norm(DEEPNORM_ALPHA * x1 + xa, g_ref[...], b_ref[...])
    x2_ref[0] = x2


def _cross_attention(x1, mem, B, S, mem_ln_g, mem_ln_b, w_xq, w_xkv, w_xo, ln_g, ln_b):
    M = mem.shape[1]
    hd = XATTN_HEADS * XATTN_HEAD_DIM
    row = lambda a: a.reshape(1, -1)
    k, v = pl.pallas_call(
        _memkv_kernel,
        grid=(B,),
        in_specs=[pl.BlockSpec((1, M, D_MODEL), lambda b: (b, 0, 0)), _full((1, D_MODEL)), _full((1, D_MODEL)),
                  _full((D_MODEL, 2 * hd))],
        out_specs=[pl.BlockSpec((1, M, hd), lambda b: (b, 0, 0))] * 2,
        out_shape=[jax.ShapeDtypeStruct((B, M, hd), BF16)] * 2,
        compiler_params=_cparams("parallel"),
        name="memory_kv",
    )(mem, row(mem_ln_g), row(mem_ln_b), w_xkv.astype(BF16))
    tile = lambda w: pl.BlockSpec((1, TM, w), lambda b, i: (b, i, 0))
    x2 = pl.pallas_call(
        _xattn_kernel,
        grid=(B, S // TM),
        in_specs=[tile(D_MODEL),
                  pl.BlockSpec((1, M, hd), lambda b, i: (b, 0, 0)), pl.BlockSpec((1, M, hd), lambda b, i: (b, 0, 0)),
                  _full((D_MODEL, hd)), _full((hd, D_MODEL)), _full((1, D_MODEL)), _full((1, D_MODEL))],
        out_specs=tile(D_MODEL),
        out_shape=jax.ShapeDtypeStruct((B, S, D_MODEL), F32),
        compiler_params=_cparams("parallel", "parallel"),
        name="cross_attention_ln2",
    )(x1.reshape(B, S, D_MODEL), k, v, w_xq.astype(BF16), w_xo.astype(BF16), row(ln_g), row(ln_b))
    return x2.reshape(B * S, D_MODEL)


def _route_sort_kernel(x_ref, w_ref, bias_ref, xs_ref, pos_ref, cnt_ref, off_ref):
    E, per = N_EXPERTS, N_EXPERTS // N_EXPERT_GROUPS
    tm = x_ref.shape[0]
    x = x_ref[...]
    logits = _dot_nt(w_ref[...], x, precision=lax.Precision.HIGHEST)
    scores = jax.nn.sigmoid(logits)
    sel = scores + bias_ref[...]
    neg = -jnp.inf
    i8 = lax.broadcasted_iota(I32, (per, tm), 0)
    gscore = []
    for g in range(N_EXPERT_GROUPS):
        blk = sel[g * per:(g + 1) * per]
        m1 = jnp.max(blk, axis=0, keepdims=True)
        i1 = jnp.min(jnp.where(blk == m1, i8, per), axis=0, keepdims=True)
        m2 = jnp.max(jnp.where(i8 == i1, neg, blk), axis=0, keepdims=True)
        gscore.append(m1 + m2)
    blocks = []
    for g in range(N_EXPERT_GROUPS):
        ahead = jnp.zeros((1, tm), I32)
        for o in range(N_EXPERT_GROUPS):
            if o == g:
                continue
            before = (gscore[o] >= gscore[g]) if o < g else (gscore[o] > gscore[g])
            ahead = ahead + before.astype(I32)
        blocks.append(jnp.where(ahead < TOPK_GROUPS, sel[g * per:(g + 1) * per], neg))
    cur = jnp.concatenate(blocks, axis=0)
    ie = lax.broadcasted_iota(I32, (E, tm), 0)
    picked = jnp.zeros((E, tm), F32)
    idxs = []
    for _ in range(TOP_K):
        m = jnp.max(cur, axis=0, keepdims=True)
        idx = jnp.min(jnp.where(cur == m, ie, E), axis=0, keepdims=True)
        hit = ie == idx
        picked = jnp.where(hit, 1.0, picked)
        cur = jnp.where(hit, neg, cur)
        idxs.append(idx)
    wsel = scores * picked
    wnorm = wsel / jnp.sum(wsel, axis=0, keepdims=True) * ROUTED_SCALE
    pb = picked.astype(BF16)
    tri = (lax.broadcasted_iota(I32, (tm, tm), 0) <= lax.broadcasted_iota(I32, (tm, tm), 1)).astype(BF16)
    incl = _dot(pb, tri)
    cnt_col = jnp.sum(picked, axis=1, keepdims=True)
    lower = (lax.broadcasted_iota(I32, (E, E), 1) < lax.broadcasted_iota(I32, (E, E), 0)).astype(F32)
    al_col = jnp.floor((cnt_col + (SEG_ALIGN - 1.0)) * (1.0 / SEG_ALIGN)) * SEG_ALIGN
    off_col = _dot_exact(lower, jnp.broadcast_to(al_col, (E, LANES)))[:, 0:1]
    pos = off_col + incl - 1.0
    q = jnp.floor(pos * (1.0 / POS_RADIX))
    pq = jnp.concatenate([jnp.where(picked > 0.0, POS_RADIX * q, POS_RADIX * POS_NONE_Q),
                          jnp.where(picked > 0.0, pos - POS_RADIX * q, 0.0)], axis=0)
    pos_ref[...] = pq
    pqb = pq.astype(BF16)
    cnt_ref[0] = jnp.broadcast_to(cnt_col, (E, LANES)).astype(I32)
    off_ref[0] = jnp.broadcast_to(off_col, (E, LANES)).astype(I32)
    cnt_row = _dot_nt(jnp.ones((8, tm), BF16), pb)
    al_row = jnp.floor((cnt_row + (SEG_ALIGN - 1.0)) * (1.0 / SEG_ALIGN)) * SEG_ALIGN
    er = lax.broadcasted_iota(I32, (E, 3 * E), 0)
    ec = lax.broadcasted_iota(I32, (E, 3 * E), 1) % E
    off_row3 = _dot_exact(al_row, (er < ec).astype(F32))[0:1]
    end_row3 = off_row3 + _dot_exact(al_row, (er == ec).astype(F32))[0:1]
    w_hi = wnorm.astype(BF16)
    w_r1 = wnorm - w_hi.astype(F32)
    w_mid = w_r1.astype(BF16)
    w_lo = (w_r1 - w_mid.astype(F32)).astype(BF16)
    w3 = jnp.concatenate([w_hi, w_mid, w_lo], axis=0)

    xb = x.astype(BF16)
    half = D_MODEL // 2

    def build(c, carry):
        r0 = pl.multiple_of(c * OH_ROWS, OH_ROWS)
        jr = (lax.broadcasted_iota(I32, (OH_ROWS, 3 * E), 0) + r0).astype(F32)
        member3 = jnp.where(jr >= off_row3, jnp.where(jr < end_row3, 1.0, 0.0), 0.0)
        target = _dot(member3[:, :2 * E].astype(BF16), pqb)
        j = (lax.broadcasted_iota(I32, (OH_ROWS, tm), 0) + r0).astype(F32)
        ohb = jnp.where(target == j, 1.0, 0.0).astype(BF16)
        rows = _dot(ohb, xb)
        xs_ref[0, pl.ds(r0, OH_ROWS), 0:half] = _pack_bf16_pair(rows[:, :half], rows[:, half:])
        wrow = jnp.sum(member3 * _dot_nt(ohb, w3), axis=1, keepdims=True)
        xs_ref[0, pl.ds(r0, OH_ROWS), half:] = pltpu.bitcast(jnp.broadcast_to(wrow, (OH_ROWS, LANES)), I32)
        return carry

    lax.fori_loop(0, SORT_ROWS // OH_ROWS, build, 0)
    xs_ref[0, SORT_ROWS:, :] = jnp.zeros((PAD_ROWS, XW), I32)


def _route_sort(x2, w_router, router_bias):
    T = x2.shape[0]
    E = N_EXPERTS
    nt = T // TW
    seg = pl.BlockSpec((1, E, LANES), lambda i: (i, 0, 0))
    return pl.pallas_call(
        _route_sort_kernel,
        grid=(nt,),
        in_specs=[pl.BlockSpec((TW, D_MODEL), lambda i: (i, 0)), _full((E, D_MODEL)), _full((E, 1))],
        out_specs=[pl.BlockSpec((1, TS, XW), lambda i: (i, 0, 0)),
                   pl.BlockSpec((2 * E, TW), lambda i: (0, i)), seg, seg],
        out_shape=[jax.ShapeDtypeStruct((nt, TS, XW), I32), jax.ShapeDtypeStruct((2 * E, T), F32),
                   jax.ShapeDtypeStruct((nt, E, LANES), I32), jax.ShapeDtypeStruct((nt, E, LANES), I32)],
        compiler_params=_cparams("parallel"),
        name="route_local_sort",
    )(x2, w_router.T, router_bias.reshape(E, 1))


def _plan_kernel(nt, n_pieces_max, n_chunks_max, cnt_ref, off_ref, pa_ref, ce_ref, nch_ref):
    def unused(p, z):
        pa_ref[p] = -SEG_ALIGN
        return z

    def per_expert(e, carry):
        p0, g = carry

        def per_tile(i, p):
            n = cnt_ref[i * N_EXPERTS + e]
            base = i * TS + off_ref[i * N_EXPERTS + e]
            n_pc = (n + PR - 1) // PR

            def put(q, c):
                pa_ref[p + q] = base + q * PR
                return c

            lax.fori_loop(0, n_pc, put, 0)
            return p + n_pc

        p1 = lax.fori_loop(0, nt, per_tile, p0)
        n_ch = (p1 - p0 + PPC - 1) // PPC

        lax.fori_loop(p1, p0 + n_ch * PPC, unused, 0)

        def mark(c, z):
            ce_ref[g + c] = e
            return z

        lax.fori_loop(0, n_ch, mark, 0)
        return p0 + n_ch * PPC, g + n_ch

    p_end, g = lax.fori_loop(0, N_EXPERTS, per_expert, (jnp.int32(0), jnp.int32(0)))
    lax.fori_loop(p_end, n_pieces_max, unused, 0)
    nch_ref[0] = g

    def tail(c, z):
        ce_ref[c] = N_EXPERTS - 1
        return z

    lax.fori_loop(g, n_chunks_max, tail, 0)


def _plan(cnt, off, nt, n_pieces_max, n_chunks_max):
    smem = pl.BlockSpec(memory_space=pltpu.SMEM)
    return pl.pallas_call(
        functools.partial(_plan_kernel, nt, n_pieces_max, n_chunks_max),
        in_specs=[smem, smem],
        out_specs=[smem, smem, smem],
        out_shape=[jax.ShapeDtypeStruct((n_pieces_max,), I32), jax.ShapeDtypeStruct((n_chunks_max,), I32),
                   jax.ShapeDtypeStruct((1,), I32)],
        name="moe_plan",
    )(cnt, off)


def _unpack_bf16_pair(w):
    lo = pltpu.bitcast(lax.shift_left(w, jnp.int32(16)), F32).astype(BF16)
    hi = pltpu.bitcast(w & jnp.int32(-65536), F32).astype(BF16)
    return lo, hi


def _pack_bf16_pair(lo, hi):
    lo_bits = lax.shift_right_logical(pltpu.bitcast(lo.astype(BF16).astype(F32), I32), jnp.int32(16))
    hi_bits = pltpu.bitcast(hi.astype(BF16).astype(F32), I32)
    return hi_bits | lo_bits


def _expert_kernel(ce_ref, pa_ref, nch_ref, tot_ref, xs_hbm, wgu_ref, wdn_ref, y_hbm,
                   xbuf, ybuf, zbuf, wgu_b, wdn_b, gsem, wsem, zsem):
    g = pl.program_id(0)
    n = nch_ref[0]
    slot = g % N_SLOTS
    half = D_MODEL // 2

    dummy_base = (tot_ref.shape[0] - 1) * TS

    def gather_copy(s, a, jp):
        return pltpu.make_async_copy(xs_hbm.at[pl.ds(pl.multiple_of(a, SEG_ALIGN), PR), :],
                                     xbuf.at[s, pl.ds(jp * PR, PR), :], gsem.at[s])

    def write_copy(s, a, jp):
        return pltpu.make_async_copy(ybuf.at[s, pl.ds(jp * PR, PR), :],
                                     y_hbm.at[pl.ds(pl.multiple_of(a, SEG_ALIGN), PR), :], wsem.at[s])

    def gather_start(c, s):
        for jp in range(PPC):
            gather_copy(s, jnp.maximum(pa_ref[c * PPC + jp], 0), jp).start()

    def write_start(c, s):
        for jp in range(PPC):
            a = pa_ref[c * PPC + jp]
            write_copy(s, jnp.where(a >= 0, a, dummy_base + jp * PR), jp).start()

    def gather_wait(s):
        for jp in range(PPC):
            gather_copy(s, 0, jp).wait()

    def write_wait(s):
        for jp in range(PPC):
            write_copy(s, 0, jp).wait()

    @pl.when(g == 0)
    def _():
        gather_start(0, 0)
        ybuf[...] = jnp.zeros_like(ybuf)
        zbuf[...] = jnp.zeros_like(zbuf)

        def zero_copy(row):
            return pltpu.make_async_copy(zbuf, y_hbm.at[pl.ds(pl.multiple_of(row, SEG_ALIGN), SEG_ALIGN), :], zsem)

        def tail_of(i):
            return i * TS + tot_ref[i], (TS - tot_ref[i]) // SEG_ALIGN

        def zero_start(i, c):
            base, cnt8 = tail_of(i)
            lax.fori_loop(0, cnt8, lambda q, z: (zero_copy(base + q * SEG_ALIGN).start(), z)[1], 0)
            return c

        def zero_wait(i, c):
            base, cnt8 = tail_of(i)
            lax.fori_loop(0, cnt8, lambda q, z: (zero_copy(base + q * SEG_ALIGN).wait(), z)[1], 0)
            return c

        lax.fori_loop(0, tot_ref.shape[0], zero_start, 0)
        lax.fori_loop(0, tot_ref.shape[0], zero_wait, 0)

    @pl.when((g == 0) & (n > 1))
    def _():
        gather_start(1, 1)

    @pl.when(g + GATHER_AHEAD < n)
    def _():
        gather_start(g + GATHER_AHEAD, (g + GATHER_AHEAD) % N_SLOTS)

    e_cur = ce_ref[g]
    e_prev = ce_ref[jnp.maximum(g - 1, 0)]
    e_prev2 = ce_ref[jnp.maximum(g - 2, 0)]

    @pl.when(g < n)
    def _():
        @pl.when((g == 0) | (e_cur != e_prev))
        def _():
            wgu_b[...] = wgu_ref[0].astype(BF16)
            wdn_b[...] = wdn_ref[0].astype(BF16)

        gather_wait(slot)

        @pl.when((g >= 2) & (e_prev == e_prev2))
        def _():
            write_wait((g - 2) % N_SLOTS)

        @pl.when((g >= 1) & (e_cur != e_prev))
        def _():
            write_wait((g - 1) % N_SLOTS)

        def swiglu_rows(rows):
            xw = xbuf[slot, :rows, :]
            lo, hi = _unpack_bf16_pair(xw[:, :half])
            h = _dot(jnp.concatenate([lo, hi], axis=1), wgu_b[...])
            gate, up = h[:, :EXPERT_FF], h[:, EXPERT_FF:]
            act = (gate * jax.nn.sigmoid(gate) * up).astype(BF16)
            y = _dot(act, wdn_b[...])
            w_row = pltpu.bitcast(xw[:, half:], F32)
            y = y * jnp.concatenate([w_row] * (D_MODEL // LANES), axis=1)
            ybuf[slot, :rows, :] = _pack_bf16_pair(y[:, :half], y[:, half:])

        second_half_used = pa_ref[g * PPC + PPC // 2] >= 0

        @pl.when(second_half_used)
        def _():
            swiglu_rows(CR)

        @pl.when(jnp.logical_not(second_half_used))
        def _():
            swiglu_rows(CR // 2)

        write_start(g, slot)

    @pl.when(g == pl.num_programs(0) - 1)
    def _():
        @pl.when((n >= 2) & (ce_ref[jnp.maximum(n - 1, 0)] == ce_ref[jnp.maximum(n - 2, 0)]))
        def _():
            write_wait((n - 2) % N_SLOTS)

        write_wait((n - 1) % N_SLOTS)


def _experts(ce, pa, nch, tot, xs, w_exp_gu, w_exp_down):
    n_rows = xs.shape[0]
    half = D_MODEL // 2
    grid_spec = pltpu.PrefetchScalarGridSpec(
        num_scalar_prefetch=4,
        grid=(ce.shape[0],),
        in_specs=[pl.BlockSpec(memory_space=pl.ANY),
                  pl.BlockSpec((1, D_MODEL, 2 * EXPERT_FF), lambda g, ce, pa, nch, tot: (ce[g], 0, 0)),
                  pl.BlockSpec((1, EXPERT_FF, D_MODEL), lambda g, ce, pa, nch, tot: (ce[g], 0, 0))],
        out_specs=pl.BlockSpec(memory_space=pl.ANY),
        scratch_shapes=[pltpu.VMEM((N_SLOTS, CR, XW), I32), pltpu.VMEM((N_SLOTS, CR, half), I32),
                        pltpu.VMEM((SEG_ALIGN, half), I32),
                        pltpu.VMEM((D_MODEL, 2 * EXPERT_FF), BF16), pltpu.VMEM((EXPERT_FF, D_MODEL), BF16),
                        pltpu.SemaphoreType.DMA((N_SLOTS,)), pltpu.SemaphoreType.DMA((N_SLOTS,)),
                        pltpu.SemaphoreType.DMA(())],
    )
    return pl.pallas_call(
        _expert_kernel,
        grid_spec=grid_spec,
        out_shape=jax.ShapeDtypeStruct((n_rows + TS, half), I32),
        compiler_params=_cparams("arbitrary"),
        name="moe_grouped_swiglu",
    )(ce, pa, nch, tot, xs, w_exp_gu, w_exp_down)


def _combine_kernel(tot_ref, y_ref, pos_ref, cnt_ref, off_ref, x2_ref, wsg_ref, wsd_ref, g_ref, b_ref, out_ref,
                    acc_lo, acc_hi):
    tm = x2_ref.shape[0]
    E = N_EXPERTS
    n_sorted = tot_ref[pl.program_id(0)]
    x2 = x2_ref[...]
    h = _dot(x2.astype(BF16), wsg_ref[...])
    gate, up = h[:, :SHARED_FF], h[:, SHARED_FF:]
    shared = _dot((gate * jax.nn.sigmoid(gate) * up).astype(BF16), wsd_ref[...])
    pq_t = pos_ref[...].T.astype(BF16)
    off_col = off_ref[0][:, 0:1].astype(F32)
    cnt_col = cnt_ref[0][:, 0:1].astype(F32)
    end_col = off_col + jnp.floor((cnt_col + (SEG_ALIGN - 1.0)) * (1.0 / SEG_ALIGN)) * SEG_ALIGN
    off_col2 = jnp.concatenate([off_col, off_col], axis=0)
    end_col2 = jnp.concatenate([end_col, end_col], axis=0)
    acc_lo[...] = jnp.zeros_like(acc_lo)
    acc_hi[...] = jnp.zeros_like(acc_hi)

    def gather_back(c, carry):
        r0 = pl.multiple_of(c * OH_ROWS, OH_ROWS)
        jc = (lax.broadcasted_iota(I32, (2 * E, OH_ROWS), 1) + r0).astype(F32)
        member2 = jnp.where(jc >= off_col2, jnp.where(jc < end_col2, 1.0, 0.0), 0.0)
        target = _dot(pq_t, member2.astype(BF16))
        j = (lax.broadcasted_iota(I32, (tm, OH_ROWS), 1) + r0).astype(F32)
        ohb = jnp.where(target == j, 1.0, 0.0).astype(BF16)
        yw = y_ref[0, pl.ds(r0, OH_ROWS), :]
        row = lax.broadcasted_iota(I32, yw.shape, 0) + r0
        lo, hi = _unpack_bf16_pair(jnp.where(row < n_sorted, yw, 0))
        acc_lo[...] += _dot(ohb, lo)
        acc_hi[...] += _dot(ohb, hi)
        return carry

    lax.fori_loop(0, SORT_ROWS // OH_ROWS, gather_back, 0)
    ff = shared + jnp.concatenate([acc_lo[...], acc_hi[...]], axis=1)
    out_ref[...] = _layer_norm(DEEPNORM_ALPHA * x2 + ff, g_ref[...], b_ref[...])


def _combine(tot, y, pos, cnt, off, x2, w_sh_gu, w_sh_down, ln_g, ln_b):
    T = x2.shape[0]
    nt = T // TW
    half = D_MODEL // 2
    tile = lambda w: pl.BlockSpec((TW, w), lambda i: (i, 0))
    return pl.pallas_call(
        _combine_kernel,
        grid=(nt,),
        in_specs=[pl.BlockSpec(memory_space=pltpu.SMEM),
                  pl.BlockSpec((1, TS, half), lambda i: (i, 0, 0)),
                  pl.BlockSpec((2 * N_EXPERTS, TW), lambda i: (0, i)),
                  pl.BlockSpec((1, N_EXPERTS, LANES), lambda i: (i, 0, 0)),
                  pl.BlockSpec((1, N_EXPERTS, LANES), lambda i: (i, 0, 0)), tile(D_MODEL),
                  _full((D_MODEL, 2 * SHARED_FF)), _full((SHARED_FF, D_MODEL)),
                  _full((1, D_MODEL)), _full((1, D_MODEL))],
        out_specs=tile(D_MODEL),
        out_shape=jax.ShapeDtypeStruct((T, D_MODEL), F32),
        scratch_shapes=[pltpu.VMEM((TW, half), F32), pltpu.VMEM((TW, half), F32)],
        compiler_params=_cparams("parallel"),
        name="moe_combine_shared_ln3",
    )(tot, y.reshape(nt + 1, TS, half), pos, cnt, off, x2, w_sh_gu.astype(BF16), w_sh_down.astype(BF16),
      ln_g.reshape(1, -1), ln_b.reshape(1, -1))


def _moe(x2, w_router, router_bias, w_exp_gu, w_exp_down, w_sh_gu, w_sh_down, ln_g, ln_b):
    T = x2.shape[0]
    E = N_EXPERTS
    nt = T // TW
    xs, pos, cnt, off = _route_sort(x2, w_router, router_bias)
    n_pieces_max = (T * TOP_K + nt * E * (PR - 1)) // PR
    n_chunks_max = n_pieces_max // PPC + E
    pa, ce, nch = _plan(cnt[:, :, 0].reshape(nt * E), off[:, :, 0].reshape(nt * E), nt,
                        n_chunks_max * PPC, n_chunks_max)
    last_al = (cnt[:, E - 1, 0] + SEG_ALIGN - 1) // SEG_ALIGN * SEG_ALIGN
    tot = jnp.concatenate([off[:, E - 1, 0] + last_al, jnp.zeros((1,), I32)])
    y = _experts(ce, pa, nch, tot, xs.reshape(nt * TS, XW), w_exp_gu, w_exp_down)
    return _combine(tot, y, pos, cnt, off, x2, w_sh_gu, w_sh_down, ln_g, ln_b)


def kernel(x, mem, positions, w_in, q_norm_g, kv_norm_g, w_uq, w_ukv, w_mla_o, s5_a_re, s5_a_im, s5_log_dt, s5_b_re, s5_b_im, s5_c_re, s5_c_im, s5_d, w_s5_glu, w_out, ln1_g, ln1_b, mem_ln_g, mem_ln_b, w_xq, w_xkv, w_xo, ln2_g, ln2_b, w_router, router_bias, w_exp_gu, w_exp_down, w_sh_gu, w_sh_down, ln3_g, ln3_b):
    B, S, D = x.shape
    assert D == D_MODEL and S % TQ == 0 and (B * S) % TM == 0 and S % S5_CHUNK == 0
    xc = x.reshape(B * S, D)
    for l in range(w_in.shape[0]):
        tabs = _rope_tables(positions)
        q, k, v, u, gm, gs = _inproj(xc, tabs, w_in[l], q_norm_g[l], kv_norm_g[l], w_uq[l], w_ukv[l])
        o = _attention(q, k, v, B, S)
        ys = _s5_branch(u, B, S, (s5_a_re[l], s5_a_im[l], s5_log_dt[l], s5_b_re[l], s5_b_im[l],
                                  s5_c_re[l], s5_c_im[l], s5_d[l]))
        x1 = _merge(xc, o, ys, gm, gs, w_mla_o[l], w_s5_glu[l], w_out[l], ln1_g[l], ln1_b[l])
        x2 = _cross_attention(x1, mem, B, S, mem_ln_g[l], mem_ln_b[l], w_xq[l], w_xkv[l], w_xo[l],
                                   ln2_g[l], ln2_b[l])
        xc = _moe(x2, w_router[l], router_bias[l], w_exp_gu[l], w_exp_down[l], w_sh_gu[l],
                  w_sh_down[l], ln3_g[l], ln3_b[l])
    return xc.reshape(B, S, D)
```

```python
import functools
import math

import jax
import jax.numpy as jnp
from jax import lax
from jax.experimental import pallas as pl
from jax.experimental.pallas import tpu as pltpu

F32 = jnp.float32
BF16 = jnp.bfloat16
I32 = jnp.int32
U32 = jnp.uint32

D_MODEL = 1024
MLA_HEADS = 8
QK_NOPE = 64
QK_ROPE = 32
V_HEAD = 64
Q_LORA = 256
KV_LORA = 256
ROPE_THETA = 10000.0
S5_GROUP_CH = 16
S5_WIDTH = 512
S5_GROUPS = 32
S5_STATE = 64
XATTN_HEADS = 4
XATTN_HEAD_DIM = 128
N_EXPERTS = 64
TOP_K = 8
N_EXPERT_GROUPS = 8
TOPK_GROUPS = 4
EXPERT_FF = 256
SHARED_FF = 256
ROUTED_SCALE = 2.5
LN_EPS = 1e-5
RMS_EPS = 1e-6
DEPTH = 1
DEEPNORM_ALPHA = (2.0 * DEPTH) ** 0.25

LANES = 128
HEAD_PAD = 128
ROPE_LO = QK_NOPE
ROPE_HALF = QK_ROPE // 2

TM = 512
TQ = 512
Q_SPLIT = 1
KV_GROUPS = (4, 2, 1)
S5_CHUNK = 16
TW = 512
OH_ROWS = 512
PR = 16
PAD_ROWS = PR
SEG_ALIGN = 8
SORT_ROWS = TOP_K * TW + N_EXPERTS * SEG_ALIGN
TS = SORT_ROWS + PAD_ROWS
XW = D_MODEL // 2 + LANES
CR = 512
PPC = CR // PR
GATHER_AHEAD = 2
N_SLOTS = GATHER_AHEAD + 1
POS_RADIX = 64
POS_NONE_Q = 127
assert SORT_ROWS <= POS_RADIX * POS_NONE_Q and SORT_ROWS % OH_ROWS == 0
VMEM_LIMIT = 48 * 1024 * 1024


def _cparams(*sem):
    return pltpu.CompilerParams(dimension_semantics=sem, vmem_limit_bytes=VMEM_LIMIT)


def _dot(a, b):
    return jnp.dot(a, b, preferred_element_type=F32)


def _dot_nt(a, b, precision=None):
    return lax.dot_general(a, b, (((1,), (1,)), ((), ())), preferred_element_type=F32,
                           precision=precision)


def _dot_exact(a, b):
    return jnp.dot(a, b, preferred_element_type=F32, precision=lax.Precision.HIGHEST)


def _layer_norm(h, g, b):
    mu = jnp.mean(h, axis=-1, keepdims=True)
    c = h - mu
    var = jnp.mean(c * c, axis=-1, keepdims=True)
    return c * lax.rsqrt(var + LN_EPS) * g + b


def _rms_norm(h, g):
    return h * lax.rsqrt(jnp.mean(h * h, axis=-1, keepdims=True) + RMS_EPS) * g


def _full(shape):
    n = len(shape)
    return pl.BlockSpec(shape, lambda *_: (0,) * n)


def _rope_angle_kernel(pos_ref, invf_ref, cos_ref, sin_ref):
    ang = pos_ref[...].astype(F32) * invf_ref[...]
    cos_ref[...] = jnp.cos(ang)
    sin_ref[...] = jnp.sin(ang)


def _rope_tables(positions):
    T = positions.size
    rows = T * ROPE_HALF // LANES
    pos_rep = jnp.repeat(positions.reshape(T), ROPE_HALF).reshape(rows, LANES)
    inv_freq = ROPE_THETA ** (-jnp.arange(0, QK_ROPE, 2, dtype=F32) / QK_ROPE)
    invf = jnp.tile(inv_freq, LANES // ROPE_HALF).reshape(1, LANES)
    cos, sin = pl.pallas_call(
        _rope_angle_kernel,
        out_shape=(jax.ShapeDtypeStruct((rows, LANES), F32),) * 2,
        name="rope_angles",
    )(pos_rep, invf)
    cos = cos.reshape(T, ROPE_HALF)
    sin = sin.reshape(T, ROPE_HALF)
    one = jnp.ones((T, ROPE_LO), F32)
    zero_lo = jnp.zeros((T, ROPE_LO), F32)
    zero_h = jnp.zeros((T, ROPE_HALF), F32)
    tail1 = jnp.ones((T, HEAD_PAD - ROPE_LO - QK_ROPE), F32)
    tail0 = jnp.zeros((T, HEAD_PAD - ROPE_LO - QK_ROPE), F32)
    c_tab = jnp.concatenate([one, cos, cos, tail1], axis=1)
    s_up = jnp.concatenate([zero_lo, -sin, zero_h, tail0], axis=1)
    s_dn = jnp.concatenate([zero_lo, zero_h, sin, tail0], axis=1)
    return c_tab, s_up, s_dn


def _rope(x, c_tab, s_up, s_dn):
    return (x * c_tab + pltpu.roll(x, HEAD_PAD - ROPE_HALF, axis=1) * s_up
            + pltpu.roll(x, ROPE_HALF, axis=1) * s_dn)


def _inproj_kernel(x_ref, c_ref, su_ref, sd_ref, wlat_ref, wu_ref, wgm_ref, wgs_ref, qg_ref, kvg_ref,
                   wuq_ref, wuk_ref, wuv_ref, q_ref, k_ref, v_ref, u_ref, gm_ref, gs_ref):
    xb = x_ref[...].astype(BF16)
    lat = _dot(xb, wlat_ref[...])
    qn = _rms_norm(lat[:, :Q_LORA], qg_ref[...]).astype(BF16)
    kvn = _rms_norm(lat[:, Q_LORA:Q_LORA + KV_LORA], kvg_ref[...]).astype(BF16)
    c_tab, s_up, s_dn = c_ref[...], su_ref[...], sd_ref[...]
    k_rope = _rope(lat[:, Q_LORA + KV_LORA:], c_tab, s_up, s_dn)
    q = _dot(qn, wuq_ref[...])
    k = _dot(kvn, wuk_ref[...])
    scale = (QK_NOPE + QK_ROPE) ** -0.5 * math.log2(math.e)
    for h in range(MLA_HEADS):
        sl = slice(h * HEAD_PAD, (h + 1) * HEAD_PAD)
        q_ref[:, sl] = (_rope(q[:, sl], c_tab, s_up, s_dn) * scale).astype(BF16)
        k_ref[:, sl] = (k[:, sl] + k_rope).astype(BF16)
    v = _dot(kvn, wuv_ref[...])
    ones_lane = lax.broadcasted_iota(I32, (1, v.shape[1]), 1) % HEAD_PAD == V_HEAD
    v_ref[...] = jnp.where(ones_lane, 1.0, v).astype(BF16)
    u_ref[...] = _dot(xb, wu_ref[...]).astype(BF16)
    gm_ref[...] = _dot(xb, wgm_ref[...]).astype(BF16)
    gs_ref[...] = _dot(xb, wgs_ref[...]).astype(BF16)


def _pad_heads(w, head_w, lo_w):
    K = w.shape[0]
    w = w.reshape(K, MLA_HEADS, head_w)[:, :, :lo_w]
    w = jnp.pad(w, ((0, 0), (0, 0), (0, HEAD_PAD - lo_w)))
    return w.reshape(K, MLA_HEADS * HEAD_PAD)


def _inproj(x2d, tabs, w_in, q_norm_g, kv_norm_g, w_uq, w_ukv):
    T = x2d.shape[0]
    o_rope = Q_LORA + KV_LORA
    o_u = o_rope + QK_ROPE
    o_gm = o_u + S5_WIDTH
    o_gs = o_gm + D_MODEL
    w_rope = jnp.pad(w_in[:, o_rope:o_u], ((0, 0), (ROPE_LO, HEAD_PAD - ROPE_LO - QK_ROPE)))
    w_lat = jnp.concatenate([w_in[:, :o_rope], w_rope], axis=1).astype(BF16)
    w_u = w_in[:, o_u:o_gm].astype(BF16)
    w_gm = w_in[:, o_gm:o_gs].astype(BF16)
    w_gs = w_in[:, o_gs:].astype(BF16)
    wuq = _pad_heads(w_uq, QK_NOPE + QK_ROPE, QK_NOPE + QK_ROPE).astype(BF16)
    kv3 = w_ukv.reshape(KV_LORA, MLA_HEADS, QK_NOPE + V_HEAD)
    wuk = _pad_heads(kv3[:, :, :QK_NOPE].reshape(KV_LORA, -1), QK_NOPE, QK_NOPE).astype(BF16)
    wuv = _pad_heads(kv3[:, :, QK_NOPE:].reshape(KV_LORA, -1), V_HEAD, V_HEAD).astype(BF16)
    HP = MLA_HEADS * HEAD_PAD
    tile = lambda w: pl.BlockSpec((TM, w), lambda i: (i, 0))
    return pl.pallas_call(
        _inproj_kernel,
        grid=(T // TM,),
        in_specs=[tile(D_MODEL), tile(HEAD_PAD), tile(HEAD_PAD), tile(HEAD_PAD),
                  _full(w_lat.shape), _full(w_u.shape), _full(w_gm.shape), _full(w_gs.shape),
                  _full((1, Q_LORA)), _full((1, KV_LORA)),
                  _full(wuq.shape), _full(wuk.shape), _full(wuv.shape)],
        out_specs=[tile(HP), tile(HP), tile(HP), tile(S5_WIDTH), tile(D_MODEL), tile(D_MODEL)],
        out_shape=[jax.ShapeDtypeStruct((T, HP), BF16)] * 3
        + [jax.ShapeDtypeStruct((T, S5_WIDTH), BF16)]
        + [jax.ShapeDtypeStruct((T, D_MODEL), BF16)] * 2,
        compiler_params=_cparams("parallel"),
        name="inproj_mla_prep",
    )(x2d, *tabs, w_lat, w_u, w_gm, w_gs, q_norm_g.reshape(1, -1), kv_norm_g.reshape(1, -1),
      wuq, wuk, wuv)


def _attn_kernel(q_ref, k_ref, v_ref, o_ref):
    qi = pl.program_id(2)
    rows = TQ // Q_SPLIT
    qs = [q_ref[0, r * rows:(r + 1) * rows, :] for r in range(Q_SPLIT)]

    def step(blk, n_blk, carry, masked):
        tk = n_blk * TQ
        start = pl.multiple_of(blk * TQ, TQ)
        k = k_ref[0, pl.ds(start, tk), :]
        v = v_ref[0, pl.ds(start, tk), :]
        out = []
        for r in range(Q_SPLIT):
            m, acc = carry[r]
            s = _dot_nt(qs[r], k)
            if masked:
                row = lax.broadcasted_iota(I32, (rows, tk), 0) + r * rows
                col = lax.broadcasted_iota(I32, (rows, tk), 1)
                s = jnp.where(col <= row, s, -jnp.inf)
            m_new = jnp.maximum(m, jnp.max(s, axis=-1, keepdims=True))
            p = jnp.exp2(s - m_new)
            acc = jnp.exp2(m - m_new) * acc + _dot(p.astype(BF16), v)
            out.append((m_new, acc))
        return tuple(out)

    carry = tuple((jnp.full((rows, 1), -jnp.inf, F32), jnp.zeros((rows, HEAD_PAD), F32)) for _ in range(Q_SPLIT))
    done = 0
    for n_blk in KV_GROUPS:
        base = done
        n_it = (qi - base) // n_blk
        carry = lax.fori_loop(0, n_it, lambda j, c, base=base, n_blk=n_blk: step(base + j * n_blk, n_blk, c, False),
                              carry)
        done = base + n_it * n_blk
    carry = step(qi, 1, carry, True)
    for r in range(Q_SPLIT):
        acc = carry[r][1]
        o_ref[0, r * rows:(r + 1) * rows, :] = (acc / acc[:, V_HEAD:V_HEAD + 1]).astype(BF16)


def _attention(q, k, v, B, S):
    HP = MLA_HEADS * HEAD_PAD
    q, k, v = (a.reshape(B, S, HP) for a in (q, k, v))
    o = pl.pallas_call(
        _attn_kernel,
        grid=(B, MLA_HEADS, S // TQ),
        in_specs=[pl.BlockSpec((1, TQ, HEAD_PAD), lambda b, h, i: (b, i, h)),
                  pl.BlockSpec((1, S, HEAD_PAD), lambda b, h, i: (b, 0, h)),
                  pl.BlockSpec((1, S, HEAD_PAD), lambda b, h, i: (b, 0, h))],
        out_specs=pl.BlockSpec((1, TQ, HEAD_PAD), lambda b, h, i: (b, i, h)),
        out_shape=jax.ShapeDtypeStruct((B, S, HP), BF16),
        compiler_params=_cparams("parallel", "parallel", "arbitrary"),
        name="mla_flash_attention",
    )(q, k, v)
    return o.reshape(B * S, HP)


CH2 = S5_CHUNK * S5_GROUP_CH
P2 = 2 * S5_STATE
SCAN_LEVELS_MAX = 16


def _s5_prep_kernel(n_levels, arow_ref, acol_ref, bt_ref, ct_ref, d_ref,
                    mt_ref, wt_ref, vt_ref, apow_ref, dvec_ref, mt_acc):
    P, H, L = S5_STATE, S5_GROUP_CH, S5_CHUNK

    def powers(ar, ai, dt, n):
        e = jnp.exp(n * (ar * dt))
        ang = n * (ai * dt)
        return e * jnp.cos(ang), e * jnp.sin(ang)

    def zoh_coef(ar, ai, dt):
        pr, pi = powers(ar, ai, dt, 1.0)
        nr, ni = pr - 1.0, pi
        den = ar * ar + ai * ai
        return (nr * ar + ni * ai) / den, (ni * ar - nr * ai) / den

    row = arow_ref[0]
    ar, ai, dt = row[0:1], row[1:2], jnp.exp(row[2:3])
    cr, ci = zoh_coef(ar, ai, dt)
    bt = bt_ref[0]
    btr = jnp.concatenate([bt[:H], bt[:H]], axis=1)
    bti = jnp.concatenate([bt[H:], bt[H:]], axis=1)
    bbr = cr * btr - ci * bti
    bbi = cr * bti + ci * btr
    lane = lax.broadcasted_iota(I32, (1, P2), 1)
    is_re = lane < P
    ridx = lax.broadcasted_iota(I32, (CH2, P2), 0) // H
    pwr, pwi = powers(ar, ai, dt, (L - 1 - ridx).astype(F32))
    rsel = (lax.broadcasted_iota(I32, (CH2, H), 0) % H == lax.broadcasted_iota(I32, (CH2, H), 1)).astype(F32)
    bbr_t = _dot_exact(rsel, bbr)
    bbi_t = _dot_exact(rsel, bbi)
    w_re = pwr * bbr_t - pwi * bbi_t
    w_im = pwr * bbi_t + pwi * bbr_t
    wt_ref[0] = jnp.where(is_re, w_re, w_im).astype(BF16)
    a_r, a_i = powers(ar, ai, dt, float(L))
    sign = jnp.where(is_re, -1.0, 1.0)
    for lvl in range(SCAN_LEVELS_MAX):
        if lvl < n_levels:
            apow_ref[0, lvl:lvl + 1, :] = a_r
            apow_ref[0, SCAN_LEVELS_MAX + lvl:SCAN_LEVELS_MAX + lvl + 1, :] = sign * a_i
            a_r, a_i = a_r * a_r - a_i * a_i, 2.0 * a_r * a_i
        else:
            apow_ref[0, lvl:lvl + 1, :] = jnp.zeros_like(a_r)
            apow_ref[0, SCAN_LEVELS_MAX + lvl:SCAN_LEVELS_MAX + lvl + 1, :] = jnp.zeros_like(a_r)

    col = acol_ref[0]
    arc, aic, dtc = col[:, 0:1], col[:, 1:2], jnp.exp(col[:, 2:3])
    ct = ct_ref[0]
    csel = (lax.broadcasted_iota(I32, (H, CH2), 1) % H == lax.broadcasted_iota(I32, (H, CH2), 0)).astype(F32)
    ctr = _dot_exact(ct[:, :H], csel)
    cti = _dot_exact(ct[:, H:], csel)
    sidx = (lax.broadcasted_iota(I32, (P, CH2), 1) // H).astype(F32)
    pr, pi = powers(arc, aic, dtc, sidx)
    g_r = ctr * pr - cti * pi
    g_i = ctr * pi + cti * pr
    a1r, a1i = powers(arc, aic, dtc, 1.0)
    v_r = g_r * a1r - g_i * a1i
    v_i = g_r * a1i + g_i * a1r
    vt_ref[0, :P, :] = v_r.astype(BF16)
    vt_ref[0, P:, :] = (-v_i).astype(BF16)
    bbr64, bbi64 = bbr[:, :P], bbi[:, :P]
    kt = _dot_exact(bbr64, g_r) - _dot_exact(bbi64, g_i)
    lane2 = lax.broadcasted_iota(I32, (H, CH2), 1)
    for r in range(L):
        blk = kt if r == 0 else jnp.where(lane2 >= r * H, pltpu.roll(kt, r * H, axis=1), 0.0)
        mt_acc[r * H:(r + 1) * H, :] = blk
    mt_ref[0] = mt_acc[...].astype(BF16)
    dvec_ref[0] = _dot_exact(d_ref[0], csel)


def _s5_prep(n_levels, s5_a_re, s5_a_im, s5_log_dt, s5_b_re, s5_b_im, s5_c_re, s5_c_im, s5_d):
    G, P, H = S5_GROUPS, S5_STATE, S5_GROUP_CH
    ldt = jnp.broadcast_to(s5_log_dt[:, None], (G, P))
    arow = jnp.stack([s5_a_re, s5_a_im, ldt], axis=1)
    arow = jnp.concatenate([arow, arow], axis=2)
    acol = jnp.stack([s5_a_re, s5_a_im, ldt], axis=2)
    bt = jnp.concatenate([s5_b_re.transpose(0, 2, 1), s5_b_im.transpose(0, 2, 1)], axis=1)
    ct = jnp.concatenate([s5_c_re.transpose(0, 2, 1), s5_c_im.transpose(0, 2, 1)], axis=2)
    d = s5_d.reshape(G, 1, H)
    blk = lambda *s: pl.BlockSpec((1,) + s, lambda g: (g, 0, 0))
    return pl.pallas_call(
        functools.partial(_s5_prep_kernel, n_levels),
        grid=(G,),
        in_specs=[blk(3, P2), blk(P, 3), blk(2 * H, P), blk(P, 2 * H), blk(1, H)],
        out_specs=[blk(CH2, CH2), blk(CH2, P2), blk(P2, CH2), blk(2 * SCAN_LEVELS_MAX, P2), blk(1, CH2)],
        out_shape=[jax.ShapeDtypeStruct((G, CH2, CH2), BF16),
                   jax.ShapeDtypeStruct((G, CH2, P2), BF16),
                   jax.ShapeDtypeStruct((G, P2, CH2), BF16),
                   jax.ShapeDtypeStruct((G, 2 * SCAN_LEVELS_MAX, P2), F32),
                   jax.ShapeDtypeStruct((G, 1, CH2), F32)],
        scratch_shapes=[pltpu.VMEM((CH2, CH2), F32)],
        compiler_params=_cparams("parallel"),
        name="s5_discretise",
    )(arow, acol, bt, ct, d)


def _gelu_tanh(y):
    return 0.5 * y * (1.0 + jnp.tanh(math.sqrt(2.0 / math.pi) * (y + 0.044715 * (y * y * y))))


def _s5_scan_kernel(n_levels, batch, u_ref, mt_ref, wt_ref, vt_ref, apow_ref, dvec_ref, y_ref):
    u = u_ref[0]
    rows = u.shape[0]
    x = _dot(u, wt_ref[0])
    ridx = lax.broadcasted_iota(I32, (rows, P2), 0)
    apow = apow_ref[0]
    for lvl in range(n_levels):
        sh = batch * (1 << lvl)
        prev = jnp.where(ridx >= sh, pltpu.roll(x, sh, axis=0), 0.0)
        a_r = apow[lvl:lvl + 1]
        a_i = apow[SCAN_LEVELS_MAX + lvl:SCAN_LEVELS_MAX + lvl + 1]
        x = x + a_r * prev + a_i * pltpu.roll(prev, S5_STATE, axis=1)
    x_in = jnp.where(ridx >= batch, pltpu.roll(x, batch, axis=0), 0.0)
    y = _dot(u, mt_ref[0]) + _dot(x_in.astype(BF16), vt_ref[0]) + dvec_ref[0] * u.astype(F32)
    y_ref[0] = _gelu_tanh(y).astype(BF16)


def _s5_branch(u, B, S, s5_params):
    G, H, L = S5_GROUPS, S5_GROUP_CH, S5_CHUNK
    nc = S // L
    n_levels = max(1, (nc - 1).bit_length())
    assert n_levels <= SCAN_LEVELS_MAX
    mt, wt, vt, apow, dvec = _s5_prep(n_levels, *s5_params)
    rows = nc * B
    ug = u.reshape(B, nc, L, G, H).transpose(3, 1, 0, 2, 4).reshape(G, rows, CH2)
    blk = lambda *s: pl.BlockSpec((1,) + s, lambda g: (g, 0, 0))
    yg = pl.pallas_call(
        functools.partial(_s5_scan_kernel, n_levels, B),
        grid=(G,),
        in_specs=[blk(rows, CH2), blk(CH2, CH2), blk(CH2, P2), blk(P2, CH2),
                  blk(2 * SCAN_LEVELS_MAX, P2), blk(1, CH2)],
        out_specs=blk(rows, CH2),
        out_shape=jax.ShapeDtypeStruct((G, rows, CH2), BF16),
        compiler_params=_cparams("parallel"),
        name="s5_chunk_scan",
    )(ug, mt, wt, vt, apow, dvec)
    return yg.reshape(G, nc, B, L, H).transpose(2, 1, 3, 0, 4).reshape(B * S, S5_WIDTH)


def _merge_kernel(x_ref, o_ref, ys_ref, gm_ref, gs_ref, wmo_ref, wglu_ref, wout_ref, g_ref, b_ref, x1_ref):
    y_mla = _dot(o_ref[...], wmo_ref[...])
    z = _dot(ys_ref[...], wglu_ref[...])
    y_s5 = z[:, :D_MODEL] * jax.nn.sigmoid(z[:, D_MODEL:])
    merged = (jax.nn.sigmoid(gm_ref[...].astype(F32)) * y_mla
              + jax.nn.sigmoid(gs_ref[...].astype(F32)) * y_s5)
    mix = _dot(merged.astype(BF16), wout_ref[...])
    x1_ref[...] = _layer_norm(DEEPNORM_ALPHA * x_ref[...] + mix, g_ref[...], b_ref[...])


def _merge(x2d, o, ys, gm, gs, w_mla_o, w_s5_glu, w_out, ln_g, ln_b):
    T = x2d.shape[0]
    HP = MLA_HEADS * HEAD_PAD
    wmo = jnp.pad(w_mla_o.reshape(MLA_HEADS, V_HEAD, D_MODEL), ((0, 0), (0, HEAD_PAD - V_HEAD), (0, 0)))
    wmo = wmo.reshape(HP, D_MODEL).astype(BF16)
    tile = lambda w: pl.BlockSpec((TM, w), lambda i: (i, 0))
    return pl.pallas_call(
        _merge_kernel,
        grid=(T // TM,),
        in_specs=[tile(D_MODEL), tile(HP), tile(S5_WIDTH), tile(D_MODEL), tile(D_MODEL),
                  _full((HP, D_MODEL)), _full((S5_WIDTH, 2 * D_MODEL)), _full((D_MODEL, D_MODEL)),
                  _full((1, D_MODEL)), _full((1, D_MODEL))],
        out_specs=tile(D_MODEL),
        out_shape=jax.ShapeDtypeStruct((T, D_MODEL), F32),
        compiler_params=_cparams("parallel"),
        name="merge_outproj_ln1",
    )(x2d, o, ys, gm, gs, wmo, w_s5_glu.astype(BF16), w_out.astype(BF16),
      ln_g.reshape(1, -1), ln_b.reshape(1, -1))


def _memkv_kernel(mem_ref, g_ref, b_ref, w_ref, k_ref, v_ref):
    m = _layer_norm(mem_ref[0], g_ref[...], b_ref[...]).astype(BF16)
    kv = _dot(m, w_ref[...])
    hd = XATTN_HEADS * XATTN_HEAD_DIM
    k_ref[0] = kv[:, :hd].astype(BF16)
    v_ref[0] = kv[:, hd:].astype(BF16)


def _xattn_kernel(x1_ref, k_ref, v_ref, wq_ref, wo_ref, g_ref, b_ref, x2_ref):
    x1 = x1_ref[0]
    q = (_dot(x1.astype(BF16), wq_ref[...]) * (XATTN_HEAD_DIM ** -0.5)).astype(BF16)
    k = k_ref[0]
    v = v_ref[0]
    outs = []
    for h in range(XATTN_HEADS):
        sl = slice(h * XATTN_HEAD_DIM, (h + 1) * XATTN_HEAD_DIM)
        s = _dot_nt(q[:, sl], k[:, sl])
        p = jnp.exp(s - jnp.max(s, axis=-1, keepdims=True))
        o = _dot(p.astype(BF16), v[:, sl]) / jnp.sum(p, axis=-1, keepdims=True)
        outs.append(o.astype(BF16))
    xa = _dot(jnp.concatenate(outs, axis=1), wo_ref[...])
    x2 = _layer_norm(DEEPNORM_ALPHA * x1 + xa, g_ref[...], b_ref[...])
    x2_ref[0] = x2


def _cross_attention(x1, mem, B, S, mem_ln_g, mem_ln_b, w_xq, w_xkv, w_xo, ln_g, ln_b):
    M = mem.shape[1]
    hd = XATTN_HEADS * XATTN_HEAD_DIM
    row = lambda a: a.reshape(1, -1)
    k, v = pl.pallas_call(
        _memkv_kernel,
        grid=(B,),
        in_specs=[pl.BlockSpec((1, M, D_MODEL), lambda b: (b, 0, 0)), _full((1, D_MODEL)), _full((1, D_MODEL)),
                  _full((D_MODEL, 2 * hd))],
        out_specs=[pl.BlockSpec((1, M, hd), lambda b: (b, 0, 0))] * 2,
        out_shape=[jax.ShapeDtypeStruct((B, M, hd), BF16)] * 2,
        compiler_params=_cparams("parallel"),
        name="memory_kv",
    )(mem, row(mem_ln_g), row(mem_ln_b), w_xkv.astype(BF16))
    tile = lambda w: pl.BlockSpec((1, TM, w), lambda b, i: (b, i, 0))
    x2 = pl.pallas_call(
        _xattn_kernel,
        grid=(B, S // TM),
        in_specs=[tile(D_MODEL),
                  pl.BlockSpec((1, M, hd), lambda b, i: (b, 0, 0)), pl.BlockSpec((1, M, hd), lambda b, i: (b, 0, 0)),
                  _full((D_MODEL, hd)), _full((hd, D_MODEL)), _full((1, D_MODEL)), _full((1, D_MODEL))],
        out_specs=tile(D_MODEL),
        out_shape=jax.ShapeDtypeStruct((B, S, D_MODEL), F32),
        compiler_params=_cparams("parallel", "parallel"),
        name="cross_attention_ln2",
    )(x1.reshape(B, S, D_MODEL), k, v, w_xq.astype(BF16), w_xo.astype(BF16), row(ln_g), row(ln_b))
    return x2.reshape(B * S, D_MODEL)


def _route_sort_kernel(x_ref, w_ref, bias_ref, xs_ref, pos_ref, cnt_ref, off_ref):
    E, per = N_EXPERTS, N_EXPERTS // N_EXPERT_GROUPS
    tm = x_ref.shape[0]
    x = x_ref[...]
    logits = _dot_nt(w_ref[...], x, precision=lax.Precision.HIGHEST)
    scores = jax.nn.sigmoid(logits)
    sel = scores + bias_ref[...]
    neg = -jnp.inf
    i8 = lax.broadcasted_iota(I32, (per, tm), 0)
    gscore = []
    for g in range(N_EXPERT_GROUPS):
        blk = sel[g * per:(g + 1) * per]
        m1 = jnp.max(blk, axis=0, keepdims=True)
        i1 = jnp.min(jnp.where(blk == m1, i8, per), axis=0, keepdims=True)
        m2 = jnp.max(jnp.where(i8 == i1, neg, blk), axis=0, keepdims=True)
        gscore.append(m1 + m2)
    blocks = []
    for g in range(N_EXPERT_GROUPS):
        ahead = jnp.zeros((1, tm), I32)
        for o in range(N_EXPERT_GROUPS):
            if o == g:
                continue
            before = (gscore[o] >= gscore[g]) if o < g else (gscore[o] > gscore[g])
            ahead = ahead + before.astype(I32)
        blocks.append(jnp.where(ahead < TOPK_GROUPS, sel[g * per:(g + 1) * per], neg))
    cur = jnp.concatenate(blocks, axis=0)
    ie = lax.broadcasted_iota(I32, (E, tm), 0)
    picked = jnp.zeros((E, tm), F32)
    idxs = []
    for _ in range(TOP_K):
        m = jnp.max(cur, axis=0, keepdims=True)
        idx = jnp.min(jnp.where(cur == m, ie, E), axis=0, keepdims=True)
        hit = ie == idx
        picked = jnp.where(hit, 1.0, picked)
        cur = jnp.where(hit, neg, cur)
        idxs.append(idx)
    wsel = scores * picked
    wnorm = wsel / jnp.sum(wsel, axis=0, keepdims=True) * ROUTED_SCALE
    pb = picked.astype(BF16)
    tri = (lax.broadcasted_iota(I32, (tm, tm), 0) <= lax.broadcasted_iota(I32, (tm, tm), 1)).astype(BF16)
    incl = _dot(pb, tri)
    cnt_col = jnp.sum(picked, axis=1, keepdims=True)
    lower = (lax.broadcasted_iota(I32, (E, E), 1) < lax.broadcasted_iota(I32, (E, E), 0)).astype(F32)
    al_col = jnp.floor((cnt_col + (SEG_ALIGN - 1.0)) * (1.0 / SEG_ALIGN)) * SEG_ALIGN
    off_col = _dot_exact(lower, jnp.broadcast_to(al_col, (E, LANES)))[:, 0:1]
    pos = off_col + incl - 1.0
    q = jnp.floor(pos * (1.0 / POS_RADIX))
    pq = jnp.concatenate([jnp.where(picked > 0.0, POS_RADIX * q, POS_RADIX * POS_NONE_Q),
                          jnp.where(picked > 0.0, pos - POS_RADIX * q, 0.0)], axis=0)
    pos_ref[...] = pq
    pqb = pq.astype(BF16)
    cnt_ref[0] = jnp.broadcast_to(cnt_col, (E, LANES)).astype(I32)
    off_ref[0] = jnp.broadcast_to(off_col, (E, LANES)).astype(I32)
    cnt_row = _dot_nt(jnp.ones((8, tm), BF16), pb)
    al_row = jnp.floor((cnt_row + (SEG_ALIGN - 1.0)) * (1.0 / SEG_ALIGN)) * SEG_ALIGN
    er = lax.broadcasted_iota(I32, (E, 3 * E), 0)
    ec = lax.broadcasted_iota(I32, (E, 3 * E), 1) % E
    off_row3 = _dot_exact(al_row, (er < ec).astype(F32))[0:1]
    end_row3 = off_row3 + _dot_exact(al_row, (er == ec).astype(F32))[0:1]
    w_hi = wnorm.astype(BF16)
    w_r1 = wnorm - w_hi.astype(F32)
    w_mid = w_r1.astype(BF16)
    w_lo = (w_r1 - w_mid.astype(F32)).astype(BF16)
    w3 = jnp.concatenate([w_hi, w_mid, w_lo], axis=0)

    xb = x.astype(BF16)
    half = D_MODEL // 2

    def build(c, carry):
        r0 = pl.multiple_of(c * OH_ROWS, OH_ROWS)
        jr = (lax.broadcasted_iota(I32, (OH_ROWS, 3 * E), 0) + r0).astype(F32)
        member3 = jnp.where(jr >= off_row3, jnp.where(jr < end_row3, 1.0, 0.0), 0.0)
        target = _dot(member3[:, :2 * E].astype(BF16), pqb)
        j = (lax.broadcasted_iota(I32, (OH_ROWS, tm), 0) + r0).astype(F32)
        ohb = jnp.where(target == j, 1.0, 0.0).astype(BF16)
        rows = _dot(ohb, xb)
        xs_ref[0, pl.ds(r0, OH_ROWS), 0:half] = _pack_bf16_pair(rows[:, :half], rows[:, half:])
        wrow = jnp.sum(member3 * _dot_nt(ohb, w3), axis=1, keepdims=True)
        xs_ref[0, pl.ds(r0, OH_ROWS), half:] = pltpu.bitcast(jnp.broadcast_to(wrow, (OH_ROWS, LANES)), I32)
        return carry

    lax.fori_loop(0, SORT_ROWS // OH_ROWS, build, 0)
    xs_ref[0, SORT_ROWS:, :] = jnp.zeros((PAD_ROWS, XW), I32)


def _route_sort(x2, w_router, router_bias):
    T = x2.shape[0]
    E = N_EXPERTS
    nt = T // TW
    seg = pl.BlockSpec((1, E, LANES), lambda i: (i, 0, 0))
    return pl.pallas_call(
        _route_sort_kernel,
        grid=(nt,),
        in_specs=[pl.BlockSpec((TW, D_MODEL), lambda i: (i, 0)), _full((E, D_MODEL)), _full((E, 1))],
        out_specs=[pl.BlockSpec((1, TS, XW), lambda i: (i, 0, 0)),
                   pl.BlockSpec((2 * E, TW), lambda i: (0, i)), seg, seg],
        out_shape=[jax.ShapeDtypeStruct((nt, TS, XW), I32), jax.ShapeDtypeStruct((2 * E, T), F32),
                   jax.ShapeDtypeStruct((nt, E, LANES), I32), jax.ShapeDtypeStruct((nt, E, LANES), I32)],
        compiler_params=_cparams("parallel"),
        name="route_local_sort",
    )(x2, w_router.T, router_bias.reshape(E, 1))


def _plan_kernel(nt, n_pieces_max, n_chunks_max, cnt_ref, off_ref, pa_ref, ce_ref, nch_ref):
    def unused(p, z):
        pa_ref[p] = -SEG_ALIGN
        return z

    def per_expert(e, carry):
        p0, g = carry

        def per_tile(i, p):
            n = cnt_ref[i * N_EXPERTS + e]
            base = i * TS + off_ref[i * N_EXPERTS + e]
            n_pc = (n + PR - 1) // PR

            def put(q, c):
                pa_ref[p + q] = base + q * PR
                return c

            lax.fori_loop(0, n_pc, put, 0)
            return p + n_pc

        p1 = lax.fori_loop(0, nt, per_tile, p0)
        n_ch = (p1 - p0 + PPC - 1) // PPC

        lax.fori_loop(p1, p0 + n_ch * PPC, unused, 0)

        def mark(c, z):
            ce_ref[g + c] = e
            return z

        lax.fori_loop(0, n_ch, mark, 0)
        return p0 + n_ch * PPC, g + n_ch

    p_end, g = lax.fori_loop(0, N_EXPERTS, per_expert, (jnp.int32(0), jnp.int32(0)))
    lax.fori_loop(p_end, n_pieces_max, unused, 0)
    nch_ref[0] = g

    def tail(c, z):
        ce_ref[c] = N_EXPERTS - 1
        return z

    lax.fori_loop(g, n_chunks_max, tail, 0)


def _plan(cnt, off, nt, n_pieces_max, n_chunks_max):
    smem = pl.BlockSpec(memory_space=pltpu.SMEM)
    return pl.pallas_call(
        functools.partial(_plan_kernel, nt, n_pieces_max, n_chunks_max),
        in_specs=[smem, smem],
        out_specs=[smem, smem, smem],
        out_shape=[jax.ShapeDtypeStruct((n_pieces_max,), I32), jax.ShapeDtypeStruct((n_chunks_max,), I32),
                   jax.ShapeDtypeStruct((1,), I32)],
        name="moe_plan",
    )(cnt, off)


def _unpack_bf16_pair(w):
    lo = pltpu.bitcast(lax.shift_left(w, jnp.int32(16)), F32).astype(BF16)
    hi = pltpu.bitcast(w & jnp.int32(-65536), F32).astype(BF16)
    return lo, hi


def _pack_bf16_pair(lo, hi):
    lo_bits = lax.shift_right_logical(pltpu.bitcast(lo.astype(BF16).astype(F32), I32), jnp.int32(16))
    hi_bits = pltpu.bitcast(hi.astype(BF16).astype(F32), I32)
    return hi_bits | lo_bits


def _expert_kernel(ce_ref, pa_ref, nch_ref, tot_ref, xs_hbm, wgu_ref, wdn_ref, y_hbm,
                   xbuf, ybuf, zbuf, wgu_b, wdn_b, gsem, wsem, zsem):
    g = pl.program_id(0)
    n = nch_ref[0]
    slot = g % N_SLOTS
    half = D_MODEL // 2

    dummy_base = (tot_ref.shape[0] - 1) * TS

    def gather_copy(s, a, jp):
        return pltpu.make_async_copy(xs_hbm.at[pl.ds(pl.multiple_of(a, SEG_ALIGN), PR), :],
                                     xbuf.at[s, pl.ds(jp * PR, PR), :], gsem.at[s])

    def write_copy(s, a, jp):
        return pltpu.make_async_copy(ybuf.at[s, pl.ds(jp * PR, PR), :],
                                     y_hbm.at[pl.ds(pl.multiple_of(a, SEG_ALIGN), PR), :], wsem.at[s])

    def gather_start(c, s):
        for jp in range(PPC):
            gather_copy(s, jnp.maximum(pa_ref[c * PPC + jp], 0), jp).start()

    def write_start(c, s):
        for jp in range(PPC):
            a = pa_ref[c * PPC + jp]
            write_copy(s, jnp.where(a >= 0, a, dummy_base + jp * PR), jp).start()

    def gather_wait(s):
        for jp in range(PPC):
            gather_copy(s, 0, jp).wait()

    def write_wait(s):
        for jp in range(PPC):
            write_copy(s, 0, jp).wait()

    @pl.when(g == 0)
    def _():
        gather_start(0, 0)
        ybuf[...] = jnp.zeros_like(ybuf)
        zbuf[...] = jnp.zeros_like(zbuf)

        def zero_copy(row):
            return pltpu.make_async_copy(zbuf, y_hbm.at[pl.ds(pl.multiple_of(row, SEG_ALIGN), SEG_ALIGN), :], zsem)

        def tail_of(i):
            return i * TS + tot_ref[i], (TS - tot_ref[i]) // SEG_ALIGN

        def zero_start(i, c):
            base, cnt8 = tail_of(i)
            lax.fori_loop(0, cnt8, lambda q, z: (zero_copy(base + q * SEG_ALIGN).start(), z)[1], 0)
            return c

        def zero_wait(i, c):
            base, cnt8 = tail_of(i)
            lax.fori_loop(0, cnt8, lambda q, z: (zero_copy(base + q * SEG_ALIGN).wait(), z)[1], 0)
            return c

        lax.fori_loop(0, tot_ref.shape[0], zero_start, 0)
        lax.fori_loop(0, tot_ref.shape[0], zero_wait, 0)

    @pl.when((g == 0) & (n > 1))
    def _():
        gather_start(1, 1)

    @pl.when(g + GATHER_AHEAD < n)
    def _():
        gather_start(g + GATHER_AHEAD, (g + GATHER_AHEAD) % N_SLOTS)

    e_cur = ce_ref[g]
    e_prev = ce_ref[jnp.maximum(g - 1, 0)]
    e_prev2 = ce_ref[jnp.maximum(g - 2, 0)]

    @pl.when(g < n)
    def _():
        @pl.when((g == 0) | (e_cur != e_prev))
        def _():
            wgu_b[...] = wgu_ref[0].astype(BF16)
            wdn_b[...] = wdn_ref[0].astype(BF16)

        gather_wait(slot)

        @pl.when((g >= 2) & (e_prev == e_prev2))
        def _():
            write_wait((g - 2) % N_SLOTS)

        @pl.when((g >= 1) & (e_cur != e_prev))
        def _():
            write_wait((g - 1) % N_SLOTS)

        def swiglu_rows(rows):
            xw = xbuf[slot, :rows, :]
            lo, hi = _unpack_bf16_pair(xw[:, :half])
            h = _dot(jnp.concatenate([lo, hi], axis=1), wgu_b[...])
            gate, up = h[:, :EXPERT_FF], h[:, EXPERT_FF:]
            act = (gate * jax.nn.sigmoid(gate) * up).astype(BF16)
            y = _dot(act, wdn_b[...])
            w_row = pltpu.bitcast(xw[:, half:], F32)
            y = y * jnp.concatenate([w_row] * (D_MODEL // LANES), axis=1)
            ybuf[slot, :rows, :] = _pack_bf16_pair(y[:, :half], y[:, half:])

        second_half_used = pa_ref[g * PPC + PPC // 2] >= 0

        @pl.when(second_half_used)
        def _():
            swiglu_rows(CR)

        @pl.when(jnp.logical_not(second_half_used))
        def _():
            swiglu_rows(CR // 2)

        write_start(g, slot)

    @pl.when(g == pl.num_programs(0) - 1)
    def _():
        @pl.when((n >= 2) & (ce_ref[jnp.maximum(n - 1, 0)] == ce_ref[jnp.maximum(n - 2, 0)]))
        def _():
            write_wait((n - 2) % N_SLOTS)

        write_wait((n - 1) % N_SLOTS)


def _experts(ce, pa, nch, tot, xs, w_exp_gu, w_exp_down):
    n_rows = xs.shape[0]
    half = D_MODEL // 2
    grid_spec = pltpu.PrefetchScalarGridSpec(
        num_scalar_prefetch=4,
        grid=(ce.shape[0],),
        in_specs=[pl.BlockSpec(memory_space=pl.ANY),
                  pl.BlockSpec((1, D_MODEL, 2 * EXPERT_FF), lambda g, ce, pa, nch, tot: (ce[g], 0, 0)),
                  pl.BlockSpec((1, EXPERT_FF, D_MODEL), lambda g, ce, pa, nch, tot: (ce[g], 0, 0))],
        out_specs=pl.BlockSpec(memory_space=pl.ANY),
        scratch_shapes=[pltpu.VMEM((N_SLOTS, CR, XW), I32), pltpu.VMEM((N_SLOTS, CR, half), I32),
                        pltpu.VMEM((SEG_ALIGN, half), I32),
                        pltpu.VMEM((D_MODEL, 2 * EXPERT_FF), BF16), pltpu.VMEM((EXPERT_FF, D_MODEL), BF16),
                        pltpu.SemaphoreType.DMA((N_SLOTS,)), pltpu.SemaphoreType.DMA((N_SLOTS,)),
                        pltpu.SemaphoreType.DMA(())],
    )
    return pl.pallas_call(
        _expert_kernel,
        grid_spec=grid_spec,
        out_shape=jax.ShapeDtypeStruct((n_rows + TS, half), I32),
        compiler_params=_cparams("arbitrary"),
        name="moe_grouped_swiglu",
    )(ce, pa, nch, tot, xs, w_exp_gu, w_exp_down)


def _combine_kernel(tot_ref, y_ref, pos_ref, cnt_ref, off_ref, x2_ref, wsg_ref, wsd_ref, g_ref, b_ref, out_ref,
                    acc_lo, acc_hi):
    tm = x2_ref.shape[0]
    E = N_EXPERTS
    n_sorted = tot_ref[pl.program_id(0)]
    x2 = x2_ref[...]
    h = _dot(x2.astype(BF16), wsg_ref[...])
    gate, up = h[:, :SHARED_FF], h[:, SHARED_FF:]
    shared = _dot((gate * jax.nn.sigmoid(gate) * up).astype(BF16), wsd_ref[...])
    pq_t = pos_ref[...].T.astype(BF16)
    off_col = off_ref[0][:, 0:1].astype(F32)
    cnt_col = cnt_ref[0][:, 0:1].astype(F32)
    end_col = off_col + jnp.floor((cnt_col + (SEG_ALIGN - 1.0)) * (1.0 / SEG_ALIGN)) * SEG_ALIGN
    off_col2 = jnp.concatenate([off_col, off_col], axis=0)
    end_col2 = jnp.concatenate([end_col, end_col], axis=0)
    acc_lo[...] = jnp.zeros_like(acc_lo)
    acc_hi[...] = jnp.zeros_like(acc_hi)

    def gather_back(c, carry):
        r0 = pl.multiple_of(c * OH_ROWS, OH_ROWS)
        jc = (lax.broadcasted_iota(I32, (2 * E, OH_ROWS), 1) + r0).astype(F32)
        member2 = jnp.where(jc >= off_col2, jnp.where(jc < end_col2, 1.0, 0.0), 0.0)
        target = _dot(pq_t, member2.astype(BF16))
        j = (lax.broadcasted_iota(I32, (tm, OH_ROWS), 1) + r0).astype(F32)
        ohb = jnp.where(target == j, 1.0, 0.0).astype(BF16)
        yw = y_ref[0, pl.ds(r0, OH_ROWS), :]
        row = lax.broadcasted_iota(I32, yw.shape, 0) + r0
        lo, hi = _unpack_bf16_pair(jnp.where(row < n_sorted, yw, 0))
        acc_lo[...] += _dot(ohb, lo)
        acc_hi[...] += _dot(ohb, hi)
        return carry

    lax.fori_loop(0, SORT_ROWS // OH_ROWS, gather_back, 0)
    ff = shared + jnp.concatenate([acc_lo[...], acc_hi[...]], axis=1)
    out_ref[...] = _layer_norm(DEEPNORM_ALPHA * x2 + ff, g_ref[...], b_ref[...])


def _combine(tot, y, pos, cnt, off, x2, w_sh_gu, w_sh_down, ln_g, ln_b):
    T = x2.shape[0]
    nt = T // TW
    half = D_MODEL // 2
    tile = lambda w: pl.BlockSpec((TW, w), lambda i: (i, 0))
    return pl.pallas_call(
        _combine_kernel,
        grid=(nt,),
        in_specs=[pl.BlockSpec(memory_space=pltpu.SMEM),
                  pl.BlockSpec((1, TS, half), lambda i: (i, 0, 0)),
                  pl.BlockSpec((2 * N_EXPERTS, TW), lambda i: (0, i)),
                  pl.BlockSpec((1, N_EXPERTS, LANES), lambda i: (i, 0, 0)),
                  pl.BlockSpec((1, N_EXPERTS, LANES), lambda i: (i, 0, 0)), tile(D_MODEL),
                  _full((D_MODEL, 2 * SHARED_FF)), _full((SHARED_FF, D_MODEL)),
                  _full((1, D_MODEL)), _full((1, D_MODEL))],
        out_specs=tile(D_MODEL),
        out_shape=jax.ShapeDtypeStruct((T, D_MODEL), F32),
        scratch_shapes=[pltpu.VMEM((TW, half), F32), pltpu.VMEM((TW, half), F32)],
        compiler_params=_cparams("parallel"),
        name="moe_combine_shared_ln3",
    )(tot, y.reshape(nt + 1, TS, half), pos, cnt, off, x2, w_sh_gu.astype(BF16), w_sh_down.astype(BF16),
      ln_g.reshape(1, -1), ln_b.reshape(1, -1))


def _moe(x2, w_router, router_bias, w_exp_gu, w_exp_down, w_sh_gu, w_sh_down, ln_g, ln_b):
    T = x2.shape[0]
    E = N_EXPERTS
    nt = T // TW
    xs, pos, cnt, off = _route_sort(x2, w_router, router_bias)
    n_pieces_max = (T * TOP_K + nt * E * (PR - 1)) // PR
    n_chunks_max = n_pieces_max // PPC + E
    pa, ce, nch = _plan(cnt[:, :, 0].reshape(nt * E), off[:, :, 0].reshape(nt * E), nt,
                        n_chunks_max * PPC, n_chunks_max)
    last_al = (cnt[:, E - 1, 0] + SEG_ALIGN - 1) // SEG_ALIGN * SEG_ALIGN
    tot = jnp.concatenate([off[:, E - 1, 0] + last_al, jnp.zeros((1,), I32)])
    y = _experts(ce, pa, nch, tot, xs.reshape(nt * TS, XW), w_exp_gu, w_exp_down)
    return _combine(tot, y, pos, cnt, off, x2, w_sh_gu, w_sh_down, ln_g, ln_b)


def kernel(x, mem, positions, w_in, q_norm_g, kv_norm_g, w_uq, w_ukv, w_mla_o, s5_a_re, s5_a_im, s5_log_dt, s5_b_re, s5_b_im, s5_c_re, s5_c_im, s5_d, w_s5_glu, w_out, ln1_g, ln1_b, mem_ln_g, mem_ln_b, w_xq, w_xkv, w_xo, ln2_g, ln2_b, w_router, router_bias, w_exp_gu, w_exp_down, w_sh_gu, w_sh_down, ln3_g, ln3_b):
    B, S, D = x.shape
    assert D == D_MODEL and S % TQ == 0 and (B * S) % TM == 0 and S % S5_CHUNK == 0
    xc = x.reshape(B * S, D)
    for l in range(w_in.shape[0]):
        tabs = _rope_tables(positions)
        q, k, v, u, gm, gs = _inproj(xc, tabs, w_in[l], q_norm_g[l], kv_norm_g[l], w_uq[l], w_ukv[l])
        o = _attention(q, k, v, B, S)
        ys = _s5_branch(u, B, S, (s5_a_re[l], s5_a_im[l], s5_log_dt[l], s5_b_re[l], s5_b_im[l],
                                  s5_c_re[l], s5_c_im[l], s5_d[l]))
        x1 = _merge(xc, o, ys, gm, gs, w_mla_o[l], w_s5_glu[l], w_out[l], ln1_g[l], ln1_b[l])
        x2 = _cross_attention(x1, mem, B, S, mem_ln_g[l], mem_ln_b[l], w_xq[l], w_xkv[l], w_xo[l],
                                   ln2_g[l], ln2_b[l])
        xc = _moe(x2, w_router[l], router_bias[l], w_exp_gu[l], w_exp_down[l], w_sh_gu[l],
                  w_sh_down[l], ln3_g[l], ln3_b[l])
    return xc.reshape(B, S, D)
```

```python
import functools
import math

import jax
import jax.numpy as jnp
from jax import lax
from jax.experimental import pallas as pl
from jax.experimental.pallas import tpu as pltpu

F32 = jnp.float32
BF16 = jnp.bfloat16
I32 = jnp.int32
U32 = jnp.uint32

D_MODEL = 1024
MLA_HEADS = 8
QK_NOPE = 64
QK_ROPE = 32
V_HEAD = 64
Q_LORA = 256
KV_LORA = 256
ROPE_THETA = 10000.0
S5_GROUP_CH = 16
S5_WIDTH = 512
S5_GROUPS = 32
S5_STATE = 64
XATTN_HEADS = 4
XATTN_HEAD_DIM = 128
N_EXPERTS = 64
TOP_K = 8
N_EXPERT_GROUPS = 8
TOPK_GROUPS = 4
EXPERT_FF = 256
SHARED_FF = 256
ROUTED_SCALE = 2.5
LN_EPS = 1e-5
RMS_EPS = 1e-6
DEPTH = 1
DEEPNORM_ALPHA = (2.0 * DEPTH) ** 0.25

LANES = 128
HEAD_PAD = 128
ROPE_LO = QK_NOPE
ROPE_HALF = QK_ROPE // 2

TM = 512
TQ = 512
Q_SPLIT = 1
KV_GROUPS = (4, 2, 1)
S5_CHUNK = 16
TW = 512
OH_ROWS = 512
PR = 16
PAD_ROWS = PR
SEG_ALIGN = 8
SORT_ROWS = TOP_K * TW + N_EXPERTS * SEG_ALIGN
TS = SORT_ROWS + PAD_ROWS
XW = D_MODEL // 2 + LANES
CR = 512
PPC = CR // PR
GATHER_AHEAD = 2
N_SLOTS = GATHER_AHEAD + 1
POS_RADIX = 64
POS_NONE_Q = 127
assert SORT_ROWS <= POS_RADIX * POS_NONE_Q and SORT_ROWS % OH_ROWS == 0
VMEM_LIMIT = 48 * 1024 * 1024


def _cparams(*sem):
    return pltpu.CompilerParams(dimension_semantics=sem, vmem_limit_bytes=VMEM_LIMIT)


def _dot(a, b):
    return jnp.dot(a, b, preferred_element_type=F32)


def _dot_nt(a, b, precision=None):
    return lax.dot_general(a, b, (((1,), (1,)), ((), ())), preferred_element_type=F32,
                           precision=precision)


def _dot_exact(a, b):
    return jnp.dot(a, b, preferred_element_type=F32, precision=lax.Precision.HIGHEST)


def _layer_norm(h, g, b):
    mu = jnp.mean(h, axis=-1, keepdims=True)
    c = h - mu
    var = jnp.mean(c * c, axis=-1, keepdims=True)
    return c * lax.rsqrt(var + LN_EPS) * g + b


def _rms_norm(h, g):
    return h * lax.rsqrt(jnp.mean(h * h, axis=-1, keepdims=True) + RMS_EPS) * g


def _full(shape):
    n = len(shape)
    return pl.BlockSpec(shape, lambda *_: (0,) * n)


def _rope_angle_kernel(pos_ref, invf_ref, cos_ref, sin_ref):
    ang = pos_ref[...].astype(F32) * invf_ref[...]
    cos_ref[...] = jnp.cos(ang)
    sin_ref[...] = jnp.sin(ang)


def _rope_tables(positions):
    T = positions.size
    rows = T * ROPE_HALF // LANES
    pos_rep = jnp.repeat(positions.reshape(T), ROPE_HALF).reshape(rows, LANES)
    inv_freq = ROPE_THETA ** (-jnp.arange(0, QK_ROPE, 2, dtype=F32) / QK_ROPE)
    invf = jnp.tile(inv_freq, LANES // ROPE_HALF).reshape(1, LANES)
    cos, sin = pl.pallas_call(
        _rope_angle_kernel,
        out_shape=(jax.ShapeDtypeStruct((rows, LANES), F32),) * 2,
        name="rope_angles",
    )(pos_rep, invf)
    cos = cos.reshape(T, ROPE_HALF)
    sin = sin.reshape(T, ROPE_HALF)
    one = jnp.ones((T, ROPE_LO), F32)
    zero_lo = jnp.zeros((T, ROPE_LO), F32)
    zero_h = jnp.zeros((T, ROPE_HALF), F32)
    tail1 = jnp.ones((T, HEAD_PAD - ROPE_LO - QK_ROPE), F32)
    tail0 = jnp.zeros((T, HEAD_PAD - ROPE_LO - QK_ROPE), F32)
    c_tab = jnp.concatenate([one, cos, cos, tail1], axis=1)
    s_up = jnp.concatenate([zero_lo, -sin, zero_h, tail0], axis=1)
    s_dn = jnp.concatenate([zero_lo, zero_h, sin, tail0], axis=1)
    return c_tab, s_up, s_dn


def _rope(x, c_tab, s_up, s_dn):
    return (x * c_tab + pltpu.roll(x, HEAD_PAD - ROPE_HALF, axis=1) * s_up
            + pltpu.roll(x, ROPE_HALF, axis=1) * s_dn)


def _inproj_kernel(x_ref, c_ref, su_ref, sd_ref, wlat_ref, wu_ref, wgm_ref, wgs_ref, qg_ref, kvg_ref,
                   wuq_ref, wuk_ref, wuv_ref, q_ref, k_ref, v_ref, u_ref, gm_ref, gs_ref):
    xb = x_ref[...].astype(BF16)
    lat = _dot(xb, wlat_ref[...])
    qn = _rms_norm(lat[:, :Q_LORA], qg_ref[...]).astype(BF16)
    kvn = _rms_norm(lat[:, Q_LORA:Q_LORA + KV_LORA], kvg_ref[...]).astype(BF16)
    c_tab, s_up, s_dn = c_ref[...], su_ref[...], sd_ref[...]
    k_rope = _rope(lat[:, Q_LORA + KV_LORA:], c_tab, s_up, s_dn)
    q = _dot(qn, wuq_ref[...])
    k = _dot(kvn, wuk_ref[...])
    scale = (QK_NOPE + QK_ROPE) ** -0.5 * math.log2(math.e)
    for h in range(MLA_HEADS):
        sl = slice(h * HEAD_PAD, (h + 1) * HEAD_PAD)
        q_ref[:, sl] = (_rope(q[:, sl], c_tab, s_up, s_dn) * scale).astype(BF16)
        k_ref[:, sl] = (k[:, sl] + k_rope).astype(BF16)
    v = _dot(kvn, wuv_ref[...])
    ones_lane = lax.broadcasted_iota(I32, (1, v.shape[1]), 1) % HEAD_PAD == V_HEAD
    v_ref[...] = jnp.where(ones_lane, 1.0, v).astype(BF16)
    u_ref[...] = _dot(xb, wu_ref[...])
    gm_ref[...] = _dot(xb, wgm_ref[...]).astype(BF16)
    gs_ref[...] = _dot(xb, wgs_ref[...]).astype(BF16)


def _pad_heads(w, head_w, lo_w):
    K = w.shape[0]
    w = w.reshape(K, MLA_HEADS, head_w)[:, :, :lo_w]
    w = jnp.pad(w, ((0, 0), (0, 0), (0, HEAD_PAD - lo_w)))
    return w.reshape(K, MLA_HEADS * HEAD_PAD)


def _inproj(x2d, tabs, w_in, q_norm_g, kv_norm_g, w_uq, w_ukv):
    T = x2d.shape[0]
    o_rope = Q_LORA + KV_LORA
    o_u = o_rope + QK_ROPE
    o_gm = o_u + S5_WIDTH
    o_gs = o_gm + D_MODEL
    w_rope = jnp.pad(w_in[:, o_rope:o_u], ((0, 0), (ROPE_LO, HEAD_PAD - ROPE_LO - QK_ROPE)))
    w_lat = jnp.concatenate([w_in[:, :o_rope], w_rope], axis=1).astype(BF16)
    w_u = w_in[:, o_u:o_gm].astype(BF16)
    w_gm = w_in[:, o_gm:o_gs].astype(BF16)
    w_gs = w_in[:, o_gs:].astype(BF16)
    wuq = _pad_heads(w_uq, QK_NOPE + QK_ROPE, QK_NOPE + QK_ROPE).astype(BF16)
    kv3 = w_ukv.reshape(KV_LORA, MLA_HEADS, QK_NOPE + V_HEAD)
    wuk = _pad_heads(kv3[:, :, :QK_NOPE].reshape(KV_LORA, -1), QK_NOPE, QK_NOPE).astype(BF16)
    wuv = _pad_heads(kv3[:, :, QK_NOPE:].reshape(KV_LORA, -1), V_HEAD, V_HEAD).astype(BF16)
    HP = MLA_HEADS * HEAD_PAD
    tile = lambda w: pl.BlockSpec((TM, w), lambda i: (i, 0))
    return pl.pallas_call(
        _inproj_kernel,
        grid=(T // TM,),
        in_specs=[tile(D_MODEL), tile(HEAD_PAD), tile(HEAD_PAD), tile(HEAD_PAD),
                  _full(w_lat.shape), _full(w_u.shape), _full(w_gm.shape), _full(w_gs.shape),
                  _full((1, Q_LORA)), _full((1, KV_LORA)),
                  _full(wuq.shape), _full(wuk.shape), _full(wuv.shape)],
        out_specs=[tile(HP), tile(HP), tile(HP), tile(S5_WIDTH), tile(D_MODEL), tile(D_MODEL)],
        out_shape=[jax.ShapeDtypeStruct((T, HP), BF16)] * 3
        + [jax.ShapeDtypeStruct((T, S5_WIDTH), F32)]
        + [jax.ShapeDtypeStruct((T, D_MODEL), BF16)] * 2,
        compiler_params=_cparams("parallel"),
        name="inproj_mla_prep",
    )(x2d, *tabs, w_lat, w_u, w_gm, w_gs, q_norm_g.reshape(1, -1), kv_norm_g.reshape(1, -1),
      wuq, wuk, wuv)


def _attn_kernel(q_ref, k_ref, v_ref, o_ref):
    qi = pl.program_id(2)
    rows = TQ // Q_SPLIT
    qs = [q_ref[0, r * rows:(r + 1) * rows, :] for r in range(Q_SPLIT)]

    def step(blk, n_blk, carry, masked):
        tk = n_blk * TQ
        start = pl.multiple_of(blk * TQ, TQ)
        k = k_ref[0, pl.ds(start, tk), :]
        v = v_ref[0, pl.ds(start, tk), :]
        out = []
        for r in range(Q_SPLIT):
            m, acc = carry[r]
            s = _dot_nt(qs[r], k)
            if masked:
                row = lax.broadcasted_iota(I32, (rows, tk), 0) + r * rows
                col = lax.broadcasted_iota(I32, (rows, tk), 1)
                s = jnp.where(col <= row, s, -jnp.inf)
            m_new = jnp.maximum(m, jnp.max(s, axis=-1, keepdims=True))
            p = jnp.exp2(s - m_new)
            acc = jnp.exp2(m - m_new) * acc + _dot(p.astype(BF16), v)
            out.append((m_new, acc))
        return tuple(out)

    carry = tuple((jnp.full((rows, 1), -jnp.inf, F32), jnp.zeros((rows, HEAD_PAD), F32)) for _ in range(Q_SPLIT))
    done = 0
    for n_blk in KV_GROUPS:
        base = done
        n_it = (qi - base) // n_blk
        carry = lax.fori_loop(0, n_it, lambda j, c, base=base, n_blk=n_blk: step(base + j * n_blk, n_blk, c, False),
                              carry)
        done = base + n_it * n_blk
    carry = step(qi, 1, carry, True)
    for r in range(Q_SPLIT):
        acc = carry[r][1]
        o_ref[0, r * rows:(r + 1) * rows, :] = (acc / acc[:, V_HEAD:V_HEAD + 1]).astype(BF16)


def _attention(q, k, v, B, S):
    HP = MLA_HEADS * HEAD_PAD
    q, k, v = (a.reshape(B, S, HP) for a in (q, k, v))
    o = pl.pallas_call(
        _attn_kernel,
        grid=(B, MLA_HEADS, S // TQ),
        in_specs=[pl.BlockSpec((1, TQ, HEAD_PAD), lambda b, h, i: (b, i, h)),
                  pl.BlockSpec((1, S, HEAD_PAD), lambda b, h, i: (b, 0, h)),
                  pl.BlockSpec((1, S, HEAD_PAD), lambda b, h, i: (b, 0, h))],
        out_specs=pl.BlockSpec((1, TQ, HEAD_PAD), lambda b, h, i: (b, i, h)),
        out_shape=jax.ShapeDtypeStruct((B, S, HP), BF16),
        compiler_params=_cparams("parallel", "parallel", "arbitrary"),
        name="mla_flash_attention",
    )(q, k, v)
    return o.reshape(B * S, HP)


CH2 = S5_CHUNK * S5_GROUP_CH
P2 = 2 * S5_STATE
SCAN_LEVELS_MAX = 16


def _s5_prep_kernel(n_levels, arow_ref, acol_ref, bt_ref, ct_ref, d_ref,
                    mt_ref, wt_ref, vt_ref, apow_ref, dvec_ref, mt_acc):
    P, H, L = S5_STATE, S5_GROUP_CH, S5_CHUNK

    def powers(ar, ai, dt, n):
        e = jnp.exp(n * (ar * dt))
        ang = n * (ai * dt)
        return e * jnp.cos(ang), e * jnp.sin(ang)

    def zoh_coef(ar, ai, dt):
        pr, pi = powers(ar, ai, dt, 1.0)
        nr, ni = pr - 1.0, pi
        den = ar * ar + ai * ai
        return (nr * ar + ni * ai) / den, (ni * ar - nr * ai) / den

    row = arow_ref[0]
    ar, ai, dt = row[0:1], row[1:2], jnp.exp(row[2:3])
    cr, ci = zoh_coef(ar, ai, dt)
    bt = bt_ref[0]
    btr = jnp.concatenate([bt[:H], bt[:H]], axis=1)
    bti = jnp.concatenate([bt[H:], bt[H:]], axis=1)
    bbr = cr * btr - ci * bti
    bbi = cr * bti + ci * btr
    lane = lax.broadcasted_iota(I32, (1, P2), 1)
    is_re = lane < P
    ridx = lax.broadcasted_iota(I32, (CH2, P2), 0) // H
    pwr, pwi = powers(ar, ai, dt, (L - 1 - ridx).astype(F32))
    rsel = (lax.broadcasted_iota(I32, (CH2, H), 0) % H == lax.broadcasted_iota(I32, (CH2, H), 1)).astype(F32)
    bbr_t = _dot_exact(rsel, bbr)
    bbi_t = _dot_exact(rsel, bbi)
    w_re = pwr * bbr_t - pwi * bbi_t
    w_im = pwr * bbi_t + pwi * bbr_t
    wt_ref[0] = jnp.where(is_re, w_re, w_im).astype(BF16)
    a_r, a_i = powers(ar, ai, dt, float(L))
    sign = jnp.where(is_re, -1.0, 1.0)
    for lvl in range(SCAN_LEVELS_MAX):
        if lvl < n_levels:
            apow_ref[0, lvl:lvl + 1, :] = a_r
            apow_ref[0, SCAN_LEVELS_MAX + lvl:SCAN_LEVELS_MAX + lvl + 1, :] = sign * a_i
            a_r, a_i = a_r * a_r - a_i * a_i, 2.0 * a_r * a_i
        else:
            apow_ref[0, lvl:lvl + 1, :] = jnp.zeros_like(a_r)
            apow_ref[0, SCAN_LEVELS_MAX + lvl:SCAN_LEVELS_MAX + lvl + 1, :] = jnp.zeros_like(a_r)

    col = acol_ref[0]
    arc, aic, dtc = col[:, 0:1], col[:, 1:2], jnp.exp(col[:, 2:3])
    ct = ct_ref[0]
    csel = (lax.broadcasted_iota(I32, (H, CH2), 1) % H == lax.broadcasted_iota(I32, (H, CH2), 0)).astype(F32)
    ctr = _dot_exact(ct[:, :H], csel)
    cti = _dot_exact(ct[:, H:], csel)
    sidx = (lax.broadcasted_iota(I32, (P, CH2), 1) // H).astype(F32)
    pr, pi = powers(arc, aic, dtc, sidx)
    g_r = ctr * pr - cti * pi
    g_i = ctr * pi + cti * pr
    a1r, a1i = powers(arc, aic, dtc, 1.0)
    v_r = g_r * a1r - g_i * a1i
    v_i = g_r * a1i + g_i * a1r
    vt_ref[0, :P, :] = v_r.astype(BF16)
    vt_ref[0, P:, :] = (-v_i).astype(BF16)
    bbr64, bbi64 = bbr[:, :P], bbi[:, :P]
    kt = _dot_exact(bbr64, g_r) - _dot_exact(bbi64, g_i)
    lane2 = lax.broadcasted_iota(I32, (H, CH2), 1)
    for r in range(L):
        blk = kt if r == 0 else jnp.where(lane2 >= r * H, pltpu.roll(kt, r * H, axis=1), 0.0)
        mt_acc[r * H:(r + 1) * H, :] = blk
    mt_ref[0] = mt_acc[...].astype(BF16)
    dvec_ref[0] = _dot_exact(d_ref[0], csel)


def _s5_prep(n_levels, s5_a_re, s5_a_im, s5_log_dt, s5_b_re, s5_b_im, s5_c_re, s5_c_im, s5_d):
    G, P, H = S5_GROUPS, S5_STATE, S5_GROUP_CH
    ldt = jnp.broadcast_to(s5_log_dt[:, None], (G, P))
    arow = jnp.stack([s5_a_re, s5_a_im, ldt], axis=1)
    arow = jnp.concatenate([arow, arow], axis=2)
    acol = jnp.stack([s5_a_re, s5_a_im, ldt], axis=2)
    bt = jnp.concatenate([s5_b_re.transpose(0, 2, 1), s5_b_im.transpose(0, 2, 1)], axis=1)
    ct = jnp.concatenate([s5_c_re.transpose(0, 2, 1), s5_c_im.transpose(0, 2, 1)], axis=2)
    d = s5_d.reshape(G, 1, H)
    blk = lambda *s: pl.BlockSpec((1,) + s, lambda g: (g, 0, 0))
    return pl.pallas_call(
        functools.partial(_s5_prep_kernel, n_levels),
        grid=(G,),
        in_specs=[blk(3, P2), blk(P, 3), blk(2 * H, P), blk(P, 2 * H), blk(1, H)],
        out_specs=[blk(CH2, CH2), blk(CH2, P2), blk(P2, CH2), blk(2 * SCAN_LEVELS_MAX, P2), blk(1, CH2)],
        out_shape=[jax.ShapeDtypeStruct((G, CH2, CH2), BF16),
                   jax.ShapeDtypeStruct((G, CH2, P2), BF16),
                   jax.ShapeDtypeStruct((G, P2, CH2), BF16),
                   jax.ShapeDtypeStruct((G, 2 * SCAN_LEVELS_MAX, P2), F32),
                   jax.ShapeDtypeStruct((G, 1, CH2), F32)],
        scratch_shapes=[pltpu.VMEM((CH2, CH2), F32)],
        compiler_params=_cparams("parallel"),
        name="s5_discretise",
    )(arow, acol, bt, ct, d)


def _gelu_tanh(y):
    return 0.5 * y * (1.0 + jnp.tanh(math.sqrt(2.0 / math.pi) * (y + 0.044715 * (y * y * y))))


GROUPS_PER_TILE = LANES // S5_GROUP_CH
STEPS_PER_TILE = LANES // S5_GROUP_CH


def _s5_scan_kernel(n_levels, u_ref, mt_ref, wt_ref, vt_ref, apow_ref, dvec_ref, y_ref):
    H, L = S5_GROUP_CH, S5_CHUNK
    nc = u_ref.shape[0] // L
    lane_blk = lax.broadcasted_iota(I32, (nc, LANES), 1) // H
    ridx = lax.broadcasted_iota(I32, (nc, P2), 0)
    steps = [u_ref[pl.ds(l, nc, stride=L), :] for l in range(L)]
    outs = [jnp.zeros((nc, LANES), F32) for _ in range(L)]
    for gi in range(GROUPS_PER_TILE):
        halves = []
        for hh in range(L // STEPS_PER_TILE):
            acc = jnp.zeros((nc, LANES), F32)
            for l8 in range(STEPS_PER_TILE):
                src = steps[hh * STEPS_PER_TILE + l8]
                shift = ((l8 - gi) % STEPS_PER_TILE) * H
                moved = src if shift == 0 else pltpu.roll(src, shift, axis=1)
                acc = jnp.where(lane_blk == l8, moved, acc)
            halves.append(acc)
        u = jnp.concatenate(halves, axis=1)
        ub = u.astype(BF16)
        x = _dot(ub, wt_ref[gi])
        apow = apow_ref[gi]
        for lvl in range(n_levels):
            sh = 1 << lvl
            prev = jnp.where(ridx >= sh, pltpu.roll(x, sh, axis=0), 0.0)
            a_r = apow[lvl:lvl + 1]
            a_i = apow[SCAN_LEVELS_MAX + lvl:SCAN_LEVELS_MAX + lvl + 1]
            x = x + a_r * prev + a_i * pltpu.roll(prev, S5_STATE, axis=1)
        x_in = jnp.where(ridx >= 1, pltpu.roll(x, 1, axis=0), 0.0)
        y = _gelu_tanh(_dot(ub, mt_ref[gi]) + _dot(x_in.astype(BF16), vt_ref[gi]) + dvec_ref[gi] * u)
        for l in range(L):
            hh, l8 = divmod(l, STEPS_PER_TILE)
            src = y[:, hh * LANES:(hh + 1) * LANES]
            shift = ((gi - l8) % STEPS_PER_TILE) * H
            moved = src if shift == 0 else pltpu.roll(src, shift, axis=1)
            outs[l] = jnp.where(lane_blk == gi, moved, outs[l])
    for l in range(L):
        y_ref[pl.ds(l, nc, stride=L), :] = outs[l]


def _s5_branch(u, B, S, s5_params):
    G, L = S5_GROUPS, S5_CHUNK
    nc = S // L
    n_levels = max(1, (nc - 1).bit_length())
    assert n_levels <= SCAN_LEVELS_MAX and S5_CHUNK % STEPS_PER_TILE == 0
    mt, wt, vt, apow, dvec = _s5_prep(n_levels, *s5_params)
    n_lt = G // GROUPS_PER_TILE
    blk = lambda *s: pl.BlockSpec((GROUPS_PER_TILE,) + s, lambda b, t: (t, 0, 0))
    io = pl.BlockSpec((S, LANES), lambda b, t: (b, t))
    return pl.pallas_call(
        functools.partial(_s5_scan_kernel, n_levels),
        grid=(B, n_lt),
        in_specs=[io, blk(CH2, CH2), blk(CH2, P2), blk(P2, CH2), blk(2 * SCAN_LEVELS_MAX, P2), blk(1, CH2)],
        out_specs=io,
        out_shape=jax.ShapeDtypeStruct((B * S, S5_WIDTH), F32),
        compiler_params=_cparams("parallel", "parallel"),
        name="s5_chunk_scan",
    )(u, mt, wt, vt, apow, dvec)


def _merge_kernel(x_ref, o_ref, ys_ref, gm_ref, gs_ref, wmo_ref, wglu_ref, wout_ref, g_ref, b_ref, x1_ref):
    y_mla = _dot(o_ref[...], wmo_ref[...])
    z = _dot(ys_ref[...].astype(BF16), wglu_ref[...])
    y_s5 = z[:, :D_MODEL] * jax.nn.sigmoid(z[:, D_MODEL:])
    merged = (jax.nn.sigmoid(gm_ref[...].astype(F32)) * y_mla
              + jax.nn.sigmoid(gs_ref[...].astype(F32)) * y_s5)
    mix = _dot(merged.astype(BF16), wout_ref[...])
    x1_ref[...] = _layer_norm(DEEPNORM_ALPHA * x_ref[...] + mix, g_ref[...], b_ref[...])


def _merge(x2d, o, ys, gm, gs, w_mla_o, w_s5_glu, w_out, ln_g, ln_b):
    T = x2d.shape[0]
    HP = MLA_HEADS * HEAD_PAD
    wmo = jnp.pad(w_mla_o.reshape(MLA_HEADS, V_HEAD, D_MODEL), ((0, 0), (0, HEAD_PAD - V_HEAD), (0, 0)))
    wmo = wmo.reshape(HP, D_MODEL).astype(BF16)
    tile = lambda w: pl.BlockSpec((TM, w), lambda i: (i, 0))
    return pl.pallas_call(
        _merge_kernel,
        grid=(T // TM,),
        in_specs=[tile(D_MODEL), tile(HP), tile(S5_WIDTH), tile(D_MODEL), tile(D_MODEL),
                  _full((HP, D_MODEL)), _full((S5_WIDTH, 2 * D_MODEL)), _full((D_MODEL, D_MODEL)),
                  _full((1, D_MODEL)), _full((1, D_MODEL))],
        out_specs=tile(D_MODEL),
        out_shape=jax.ShapeDtypeStruct((T, D_MODEL), F32),
        compiler_params=_cparams("parallel"),
        name="merge_outproj_ln1",
    )(x2d, o, ys, gm, gs, wmo, w_s5_glu.astype(BF16), w_out.astype(BF16),
      ln_g.reshape(1, -1), ln_b.reshape(1, -1))


def _memkv_kernel(mem_ref, g_ref, b_ref, w_ref, k_ref, v_ref):
    m = _layer_norm(mem_ref[0], g_ref[...], b_ref[...]).astype(BF16)
    kv = _dot(m, w_ref[...])
    hd = XATTN_HEADS * XATTN_HEAD_DIM
    k_ref[0] = kv[:, :hd].astype(BF16)
    v_ref[0] = kv[:, hd:].astype(BF16)


def _xattn_kernel(x1_ref, k_ref, v_ref, wq_ref, wo_ref, g_ref, b_ref, x2_ref):
    x1 = x1_ref[0]
    q = (_dot(x1.astype(BF16), wq_ref[...]) * (XATTN_HEAD_DIM ** -0.5)).astype(BF16)
    k = k_ref[0]
    v = v_ref[0]
    outs = []
    for h in range(XATTN_HEADS):
        sl = slice(h * XATTN_HEAD_DIM, (h + 1) * XATTN_HEAD_DIM)
        s = _dot_nt(q[:, sl], k[:, sl])
        p = jnp.exp(s - jnp.max(s, axis=-1, keepdims=True))
        o = _dot(p.astype(BF16), v[:, sl]) / jnp.sum(p, axis=-1, keepdims=True)
        outs.append(o.astype(BF16))
    xa = _dot(jnp.concatenate(outs, axis=1), wo_ref[...])
    x2 = _layer_norm(DEEPNORM_ALPHA * x1 + xa, g_ref[...], b_ref[...])
    x2_ref[0] = x2


def _cross_attention(x1, mem, B, S, mem_ln_g, mem_ln_b, w_xq, w_xkv, w_xo, ln_g, ln_b):
    M = mem.shape[1]
    hd = XATTN_HEADS * XATTN_HEAD_DIM
    row = lambda a: a.reshape(1, -1)
    k, v = pl.pallas_call(
        _memkv_kernel,
        grid=(B,),
        in_specs=[pl.BlockSpec((1, M, D_MODEL), lambda b: (b, 0, 0)), _full((1, D_MODEL)), _full((1, D_MODEL)),
                  _full((D_MODEL, 2 * hd))],
        out_specs=[pl.BlockSpec((1, M, hd), lambda b: (b, 0, 0))] * 2,
        out_shape=[jax.ShapeDtypeStruct((B, M, hd), BF16)] * 2,
        compiler_params=_cparams("parallel"),
        name="memory_kv",
    )(mem, row(mem_ln_g), row(mem_ln_b), w_xkv.astype(BF16))
    tile = lambda w: pl.BlockSpec((1, TM, w), lambda b, i: (b, i, 0))
    x2 = pl.pallas_call(
        _xattn_kernel,
        grid=(B, S // TM),
        in_specs=[tile(D_MODEL),
                  pl.BlockSpec((1, M, hd), lambda b, i: (b, 0, 0)), pl.BlockSpec((1, M, hd), lambda b, i: (b, 0, 0)),
                  _full((D_MODEL, hd)), _full((hd, D_MODEL)), _full((1, D_MODEL)), _full((1, D_MODEL))],
        out_specs=tile(D_MODEL),
        out_shape=jax.ShapeDtypeStruct((B, S, D_MODEL), F32),
        compiler_params=_cparams("parallel", "parallel"),
        name="cross_attention_ln2",
    )(x1.reshape(B, S, D_MODEL), k, v, w_xq.astype(BF16), w_xo.astype(BF16), row(ln_g), row(ln_b))
    return x2.reshape(B * S, D_MODEL)


def _route_sort_kernel(x_ref, w_ref, bias_ref, xs_ref, pos_ref, cnt_ref, off_ref):
    E, per = N_EXPERTS, N_EXPERTS // N_EXPERT_GROUPS
    tm = x_ref.shape[0]
    x = x_ref[...]
    logits = _dot_nt(w_ref[...], x, precision=lax.Precision.HIGHEST)
    scores = jax.nn.sigmoid(logits)
    sel = scores + bias_ref[...]
    neg = -jnp.inf
    i8 = lax.broadcasted_iota(I32, (per, tm), 0)
    gscore = []
    for g in range(N_EXPERT_GROUPS):
        blk = sel[g * per:(g + 1) * per]
        m1 = jnp.max(blk, axis=0, keepdims=True)
        i1 = jnp.min(jnp.where(blk == m1, i8, per), axis=0, keepdims=True)
        m2 = jnp.max(jnp.where(i8 == i1, neg, blk), axis=0, keepdims=True)
        gscore.append(m1 + m2)
    blocks = []
    for g in range(N_EXPERT_GROUPS):
        ahead = jnp.zeros((1, tm), I32)
        for o in range(N_EXPERT_GROUPS):
            if o == g:
                continue
            before = (gscore[o] >= gscore[g]) if o < g else (gscore[o] > gscore[g])
            ahead = ahead + before.astype(I32)
        blocks.append(jnp.where(ahead < TOPK_GROUPS, sel[g * per:(g + 1) * per], neg))
    cur = jnp.concatenate(blocks, axis=0)
    ie = lax.broadcasted_iota(I32, (E, tm), 0)
    picked = jnp.zeros((E, tm), F32)
    idxs = []
    for _ in range(TOP_K):
        m = jnp.max(cur, axis=0, keepdims=True)
        idx = jnp.min(jnp.where(cur == m, ie, E), axis=0, keepdims=True)
        hit = ie == idx
        picked = jnp.where(hit, 1.0, picked)
        cur = jnp.where(hit, neg, cur)
        idxs.append(idx)
    wsel = scores * picked
    wnorm = wsel / jnp.sum(wsel, axis=0, keepdims=True) * ROUTED_SCALE
    pb = picked.astype(BF16)
    tri = (lax.broadcasted_iota(I32, (tm, tm), 0) <= lax.broadcasted_iota(I32, (tm, tm), 1)).astype(BF16)
    incl = _dot(pb, tri)
    cnt_col = jnp.sum(picked, axis=1, keepdims=True)
    lower = (lax.broadcasted_iota(I32, (E, E), 1) < lax.broadcasted_iota(I32, (E, E), 0)).astype(F32)
    al_col = jnp.floor((cnt_col + (SEG_ALIGN - 1.0)) * (1.0 / SEG_ALIGN)) * SEG_ALIGN
    off_col = _dot_exact(lower, jnp.broadcast_to(al_col, (E, LANES)))[:, 0:1]
    pos = off_col + incl - 1.0
    q = jnp.floor(pos * (1.0 / POS_RADIX))
    pq = jnp.concatenate([jnp.where(picked > 0.0, POS_RADIX * q, POS_RADIX * POS_NONE_Q),
                          jnp.where(picked > 0.0, pos - POS_RADIX * q, 0.0)], axis=0)
    pos_ref[...] = pq
    pqb = pq.astype(BF16)
    cnt_ref[0] = jnp.broadcast_to(cnt_col, (E, LANES)).astype(I32)
    off_ref[0] = jnp.broadcast_to(off_col, (E, LANES)).astype(I32)
    cnt_row = _dot_nt(jnp.ones((8, tm), BF16), pb)
    al_row = jnp.floor((cnt_row + (SEG_ALIGN - 1.0)) * (1.0 / SEG_ALIGN)) * SEG_ALIGN
    er = lax.broadcasted_iota(I32, (E, 3 * E), 0)
    ec = lax.broadcasted_iota(I32, (E, 3 * E), 1) % E
    off_row3 = _dot_exact(al_row, (er < ec).astype(F32))[0:1]
    end_row3 = off_row3 + _dot_exact(al_row, (er == ec).astype(F32))[0:1]
    w_hi = wnorm.astype(BF16)
    w_r1 = wnorm - w_hi.astype(F32)
    w_mid = w_r1.astype(BF16)
    w_lo = (w_r1 - w_mid.astype(F32)).astype(BF16)
    w3 = jnp.concatenate([w_hi, w_mid, w_lo], axis=0)

    xb = x.astype(BF16)
    half = D_MODEL // 2

    def build(c, carry):
        r0 = pl.multiple_of(c * OH_ROWS, OH_ROWS)
        jr = (lax.broadcasted_iota(I32, (OH_ROWS, 3 * E), 0) + r0).astype(F32)
        member3 = jnp.where(jr >= off_row3, jnp.where(jr < end_row3, 1.0, 0.0), 0.0)
        target = _dot(member3[:, :2 * E].astype(BF16), pqb)
        j = (lax.broadcasted_iota(I32, (OH_ROWS, tm), 0) + r0).astype(F32)
        ohb = jnp.where(target == j, 1.0, 0.0).astype(BF16)
        rows = _dot(ohb, xb)
        xs_ref[0, pl.ds(r0, OH_ROWS), 0:half] = _pack_bf16_pair(rows[:, :half], rows[:, half:])
        wrow = jnp.sum(member3 * _dot_nt(ohb, w3), axis=1, keepdims=True)
        xs_ref[0, pl.ds(r0, OH_ROWS), half:] = pltpu.bitcast(jnp.broadcast_to(wrow, (OH_ROWS, LANES)), I32)
        return carry

    lax.fori_loop(0, SORT_ROWS // OH_ROWS, build, 0)
    xs_ref[0, SORT_ROWS:, :] = jnp.zeros((PAD_ROWS, XW), I32)


def _route_sort(x2, w_router, router_bias):
    T = x2.shape[0]
    E = N_EXPERTS
    nt = T // TW
    seg = pl.BlockSpec((1, E, LANES), lambda i: (i, 0, 0))
    return pl.pallas_call(
        _route_sort_kernel,
        grid=(nt,),
        in_specs=[pl.BlockSpec((TW, D_MODEL), lambda i: (i, 0)), _full((E, D_MODEL)), _full((E, 1))],
        out_specs=[pl.BlockSpec((1, TS, XW), lambda i: (i, 0, 0)),
                   pl.BlockSpec((2 * E, TW), lambda i: (0, i)), seg, seg],
        out_shape=[jax.ShapeDtypeStruct((nt, TS, XW), I32), jax.ShapeDtypeStruct((2 * E, T), F32),
                   jax.ShapeDtypeStruct((nt, E, LANES), I32), jax.ShapeDtypeStruct((nt, E, LANES), I32)],
        compiler_params=_cparams("parallel"),
        name="route_local_sort",
    )(x2, w_router.T, router_bias.reshape(E, 1))


def _plan_kernel(nt, n_pieces_max, n_chunks_max, cnt_ref, off_ref, pa_ref, ce_ref, nch_ref):
    def unused(p, z):
        pa_ref[p] = -SEG_ALIGN
        return z

    def per_expert(e, carry):
        p0, g = carry

        def per_tile(i, p):
            n = cnt_ref[i * N_EXPERTS + e]
            base = i * TS + off_ref[i * N_EXPERTS + e]
            n_pc = (n + PR - 1) // PR

            def put(q, c):
                pa_ref[p + q] = base + q * PR
                return c

            lax.fori_loop(0, n_pc, put, 0)
            return p + n_pc

        p1 = lax.fori_loop(0, nt, per_tile, p0)
        n_ch = (p1 - p0 + PPC - 1) // PPC

        lax.fori_loop(p1, p0 + n_ch * PPC, unused, 0)

        def mark(c, z):
            ce_ref[g + c] = e
            return z

        lax.fori_loop(0, n_ch, mark, 0)
        return p0 + n_ch * PPC, g + n_ch

    p_end, g = lax.fori_loop(0, N_EXPERTS, per_expert, (jnp.int32(0), jnp.int32(0)))
    lax.fori_loop(p_end, n_pieces_max, unused, 0)
    nch_ref[0] = g

    def tail(c, z):
        ce_ref[c] = N_EXPERTS - 1
        return z

    lax.fori_loop(g, n_chunks_max, tail, 0)


def _plan(cnt, off, nt, n_pieces_max, n_chunks_max):
    smem = pl.BlockSpec(memory_space=pltpu.SMEM)
    return pl.pallas_call(
        functools.partial(_plan_kernel, nt, n_pieces_max, n_chunks_max),
        in_specs=[smem, smem],
        out_specs=[smem, smem, smem],
        out_shape=[jax.ShapeDtypeStruct((n_pieces_max,), I32), jax.ShapeDtypeStruct((n_chunks_max,), I32),
                   jax.ShapeDtypeStruct((1,), I32)],
        name="moe_plan",
    )(cnt, off)


def _unpack_bf16_pair(w):
    lo = pltpu.bitcast(lax.shift_left(w, jnp.int32(16)), F32).astype(BF16)
    hi = pltpu.bitcast(w & jnp.int32(-65536), F32).astype(BF16)
    return lo, hi


def _pack_bf16_pair(lo, hi):
    lo_bits = lax.shift_right_logical(pltpu.bitcast(lo.astype(BF16).astype(F32), I32), jnp.int32(16))
    hi_bits = pltpu.bitcast(hi.astype(BF16).astype(F32), I32)
    return hi_bits | lo_bits


def _expert_kernel(ce_ref, pa_ref, nch_ref, tot_ref, xs_hbm, wgu_ref, wdn_ref, y_hbm,
                   xbuf, ybuf, zbuf, wgu_b, wdn_b, gsem, wsem, zsem):
    g = pl.program_id(0)
    n = nch_ref[0]
    slot = g % N_SLOTS
    half = D_MODEL // 2

    dummy_base = (tot_ref.shape[0] - 1) * TS

    def gather_copy(s, a, jp):
        return pltpu.make_async_copy(xs_hbm.at[pl.ds(pl.multiple_of(a, SEG_ALIGN), PR), :],
                                     xbuf.at[s, pl.ds(jp * PR, PR), :], gsem.at[s])

    def write_copy(s, a, jp):
        return pltpu.make_async_copy(ybuf.at[s, pl.ds(jp * PR, PR), :],
                                     y_hbm.at[pl.ds(pl.multiple_of(a, SEG_ALIGN), PR), :], wsem.at[s])

    def gather_start(c, s):
        for jp in range(PPC):
            gather_copy(s, jnp.maximum(pa_ref[c * PPC + jp], 0), jp).start()

    def write_start(c, s):
        for jp in range(PPC):
            a = pa_ref[c * PPC + jp]
            write_copy(s, jnp.where(a >= 0, a, dummy_base + jp * PR), jp).start()

    def gather_wait(s):
        for jp in range(PPC):
            gather_copy(s, 0, jp).wait()

    def write_wait(s):
        for jp in range(PPC):
            write_copy(s, 0, jp).wait()

    @pl.when(g == 0)
    def _():
        gather_start(0, 0)
        ybuf[...] = jnp.zeros_like(ybuf)
        zbuf[...] = jnp.zeros_like(zbuf)

        def zero_copy(row):
            return pltpu.make_async_copy(zbuf, y_hbm.at[pl.ds(pl.multiple_of(row, SEG_ALIGN), SEG_ALIGN), :], zsem)

        def tail_of(i):
            return i * TS + tot_ref[i], (TS - tot_ref[i]) // SEG_ALIGN

        def zero_start(i, c):
            base, cnt8 = tail_of(i)
            lax.fori_loop(0, cnt8, lambda q, z: (zero_copy(base + q * SEG_ALIGN).start(), z)[1], 0)
            return c

        def zero_wait(i, c):
            base, cnt8 = tail_of(i)
            lax.fori_loop(0, cnt8, lambda q, z: (zero_copy(base + q * SEG_ALIGN).wait(), z)[1], 0)
            return c

        lax.fori_loop(0, tot_ref.shape[0], zero_start, 0)
        lax.fori_loop(0, tot_ref.shape[0], zero_wait, 0)

    @pl.when((g == 0) & (n > 1))
    def _():
        gather_start(1, 1)

    @pl.when(g + GATHER_AHEAD < n)
    def _():
        gather_start(g + GATHER_AHEAD, (g + GATHER_AHEAD) % N_SLOTS)

    e_cur = ce_ref[g]
    e_prev = ce_ref[jnp.maximum(g - 1, 0)]
    e_prev2 = ce_ref[jnp.maximum(g - 2, 0)]

    @pl.when(g < n)
    def _():
        @pl.when((g == 0) | (e_cur != e_prev))
        def _():
            wgu_b[...] = wgu_ref[0].astype(BF16)
            wdn_b[...] = wdn_ref[0].astype(BF16)

        gather_wait(slot)

        @pl.when((g >= 2) & (e_prev == e_prev2))
        def _():
            write_wait((g - 2) % N_SLOTS)

        @pl.when((g >= 1) & (e_cur != e_prev))
        def _():
            write_wait((g - 1) % N_SLOTS)

        def swiglu_rows(rows):
            xw = xbuf[slot, :rows, :]
            lo, hi = _unpack_bf16_pair(xw[:, :half])
            h = _dot(jnp.concatenate([lo, hi], axis=1), wgu_b[...])
            gate, up = h[:, :EXPERT_FF], h[:, EXPERT_FF:]
            act = (gate * jax.nn.sigmoid(gate) * up).astype(BF16)
            y = _dot(act, wdn_b[...])
            w_row = pltpu.bitcast(xw[:, half:], F32)
            y = y * jnp.concatenate([w_row] * (D_MODEL // LANES), axis=1)
            ybuf[slot, :rows, :] = _pack_bf16_pair(y[:, :half], y[:, half:])

        second_half_used = pa_ref[g * PPC + PPC // 2] >= 0

        @pl.when(second_half_used)
        def _():
            swiglu_rows(CR)

        @pl.when(jnp.logical_not(second_half_used))
        def _():
            swiglu_rows(CR // 2)

        write_start(g, slot)

    @pl.when(g == pl.num_programs(0) - 1)
    def _():
        @pl.when((n >= 2) & (ce_ref[jnp.maximum(n - 1, 0)] == ce_ref[jnp.maximum(n - 2, 0)]))
        def _():
            write_wait((n - 2) % N_SLOTS)

        write_wait((n - 1) % N_SLOTS)


def _experts(ce, pa, nch, tot, xs, w_exp_gu, w_exp_down):
    n_rows = xs.shape[0]
    half = D_MODEL // 2
    grid_spec = pltpu.PrefetchScalarGridSpec(
        num_scalar_prefetch=4,
        grid=(ce.shape[0],),
        in_specs=[pl.BlockSpec(memory_space=pl.ANY),
                  pl.BlockSpec((1, D_MODEL, 2 * EXPERT_FF), lambda g, ce, pa, nch, tot: (ce[g], 0, 0)),
                  pl.BlockSpec((1, EXPERT_FF, D_MODEL), lambda g, ce, pa, nch, tot: (ce[g], 0, 0))],
        out_specs=pl.BlockSpec(memory_space=pl.ANY),
        scratch_shapes=[pltpu.VMEM((N_SLOTS, CR, XW), I32), pltpu.VMEM((N_SLOTS, CR, half), I32),
                        pltpu.VMEM((SEG_ALIGN, half), I32),
                        pltpu.VMEM((D_MODEL, 2 * EXPERT_FF), BF16), pltpu.VMEM((EXPERT_FF, D_MODEL), BF16),
                        pltpu.SemaphoreType.DMA((N_SLOTS,)), pltpu.SemaphoreType.DMA((N_SLOTS,)),
                        pltpu.SemaphoreType.DMA(())],
    )
    return pl.pallas_call(
        _expert_kernel,
        grid_spec=grid_spec,
        out_shape=jax.ShapeDtypeStruct((n_rows + TS, half), I32),
        compiler_params=_cparams("arbitrary"),
        name="moe_grouped_swiglu",
    )(ce, pa, nch, tot, xs, w_exp_gu, w_exp_down)


def _combine_kernel(tot_ref, y_ref, pos_ref, cnt_ref, off_ref, x2_ref, wsg_ref, wsd_ref, g_ref, b_ref, out_ref,
                    acc_lo, acc_hi):
    tm = x2_ref.shape[0]
    E = N_EXPERTS
    n_sorted = tot_ref[pl.program_id(0)]
    x2 = x2_ref[...]
    h = _dot(x2.astype(BF16), wsg_ref[...])
    gate, up = h[:, :SHARED_FF], h[:, SHARED_FF:]
    shared = _dot((gate * jax.nn.sigmoid(gate) * up).astype(BF16), wsd_ref[...])
    pq_t = pos_ref[...].T.astype(BF16)
    off_col = off_ref[0][:, 0:1].astype(F32)
    cnt_col = cnt_ref[0][:, 0:1].astype(F32)
    end_col = off_col + jnp.floor((cnt_col + (SEG_ALIGN - 1.0)) * (1.0 / SEG_ALIGN)) * SEG_ALIGN
    off_col2 = jnp.concatenate([off_col, off_col], axis=0)
    end_col2 = jnp.concatenate([end_col, end_col], axis=0)
    acc_lo[...] = jnp.zeros_like(acc_lo)
    acc_hi[...] = jnp.zeros_like(acc_hi)

    def gather_back(c, carry):
        r0 = pl.multiple_of(c * OH_ROWS, OH_ROWS)
        jc = (lax.broadcasted_iota(I32, (2 * E, OH_ROWS), 1) + r0).astype(F32)
        member2 = jnp.where(jc >= off_col2, jnp.where(jc < end_col2, 1.0, 0.0), 0.0)
        target = _dot(pq_t, member2.astype(BF16))
        j = (lax.broadcasted_iota(I32, (tm, OH_ROWS), 1) + r0).astype(F32)
        ohb = jnp.where(target == j, 1.0, 0.0).astype(BF16)
        yw = y_ref[0, pl.ds(r0, OH_ROWS), :]
        row = lax.broadcasted_iota(I32, yw.shape, 0) + r0
        lo, hi = _unpack_bf16_pair(jnp.where(row < n_sorted, yw, 0))
        acc_lo[...] += _dot(ohb, lo)
        acc_hi[...] += _dot(ohb, hi)
        return carry

    lax.fori_loop(0, SORT_ROWS // OH_ROWS, gather_back, 0)
    ff = shared + jnp.concatenate([acc_lo[...], acc_hi[...]], axis=1)
    out_ref[...] = _layer_norm(DEEPNORM_ALPHA * x2 + ff, g_ref[...], b_ref[...])


def _combine(tot, y, pos, cnt, off, x2, w_sh_gu, w_sh_down, ln_g, ln_b):
    T = x2.shape[0]
    nt = T // TW
    half = D_MODEL // 2
    tile = lambda w: pl.BlockSpec((TW, w), lambda i: (i, 0))
    return pl.pallas_call(
        _combine_kernel,
        grid=(nt,),
        in_specs=[pl.BlockSpec(memory_space=pltpu.SMEM),
                  pl.BlockSpec((1, TS, half), lambda i: (i, 0, 0)),
                  pl.BlockSpec((2 * N_EXPERTS, TW), lambda i: (0, i)),
                  pl.BlockSpec((1, N_EXPERTS, LANES), lambda i: (i, 0, 0)),
                  pl.BlockSpec((1, N_EXPERTS, LANES), lambda i: (i, 0, 0)), tile(D_MODEL),
                  _full((D_MODEL, 2 * SHARED_FF)), _full((SHARED_FF, D_MODEL)),
                  _full((1, D_MODEL)), _full((1, D_MODEL))],
        out_specs=tile(D_MODEL),
        out_shape=jax.ShapeDtypeStruct((T, D_MODEL), F32),
        scratch_shapes=[pltpu.VMEM((TW, half), F32), pltpu.VMEM((TW, half), F32)],
        compiler_params=_cparams("parallel"),
        name="moe_combine_shared_ln3",
    )(tot, y.reshape(nt + 1, TS, half), pos, cnt, off, x2, w_sh_gu.astype(BF16), w_sh_down.astype(BF16),
      ln_g.reshape(1, -1), ln_b.reshape(1, -1))


def _moe(x2, w_router, router_bias, w_exp_gu, w_exp_down, w_sh_gu, w_sh_down, ln_g, ln_b):
    T = x2.shape[0]
    E = N_EXPERTS
    nt = T // TW
    xs, pos, cnt, off = _route_sort(x2, w_router, router_bias)
    n_pieces_max = (T * TOP_K + nt * E * (PR - 1)) // PR
    n_chunks_max = n_pieces_max // PPC + E
    pa, ce, nch = _plan(cnt[:, :, 0].reshape(nt * E), off[:, :, 0].reshape(nt * E), nt,
                        n_chunks_max * PPC, n_chunks_max)
    last_al = (cnt[:, E - 1, 0] + SEG_ALIGN - 1) // SEG_ALIGN * SEG_ALIGN
    tot = jnp.concatenate([off[:, E - 1, 0] + last_al, jnp.zeros((1,), I32)])
    y = _experts(ce, pa, nch, tot, xs.reshape(nt * TS, XW), w_exp_gu, w_exp_down)
    return _combine(tot, y, pos, cnt, off, x2, w_sh_gu, w_sh_down, ln_g, ln_b)


def kernel(x, mem, positions, w_in, q_norm_g, kv_norm_g, w_uq, w_ukv, w_mla_o, s5_a_re, s5_a_im, s5_log_dt, s5_b_re, s5_b_im, s5_c_re, s5_c_im, s5_d, w_s5_glu, w_out, ln1_g, ln1_b, mem_ln_g, mem_ln_b, w_xq, w_xkv, w_xo, ln2_g, ln2_b, w_router, router_bias, w_exp_gu, w_exp_down, w_sh_gu, w_sh_down, ln3_g, ln3_b):
    B, S, D = x.shape
    assert D == D_MODEL and S % TQ == 0 and (B * S) % TM == 0 and S % S5_CHUNK == 0
    xc = x.reshape(B * S, D)
    for l in range(w_in.shape[0]):
        tabs = _rope_tables(positions)
        q, k, v, u, gm, gs = _inproj(xc, tabs, w_in[l], q_norm_g[l], kv_norm_g[l], w_uq[l], w_ukv[l])
        o = _attention(q, k, v, B, S)
        ys = _s5_branch(u, B, S, (s5_a_re[l], s5_a_im[l], s5_log_dt[l], s5_b_re[l], s5_b_im[l],
                                  s5_c_re[l], s5_c_im[l], s5_d[l]))
        x1 = _merge(xc, o, ys, gm, gs, w_mla_o[l], w_s5_glu[l], w_out[l], ln1_g[l], ln1_b[l])
        x2 = _cross_attention(x1, mem, B, S, mem_ln_g[l], mem_ln_b[l], w_xq[l], w_xkv[l], w_xo[l],
                                   ln2_g[l], ln2_b[l])
        xc = _moe(x2, w_router[l], router_bias[l], w_exp_gu[l], w_exp_down[l], w_sh_gu[l],
                  w_sh_down[l], ln3_g[l], ln3_b[l])
    return xc.reshape(B, S, D)
```

```python
import functools
import math

import jax
import jax.numpy as jnp
from jax import lax
from jax.experimental import pallas as pl
from jax.experimental.pallas import tpu as pltpu

F32 = jnp.float32
BF16 = jnp.bfloat16
I32 = jnp.int32
U32 = jnp.uint32

D_MODEL = 1024
MLA_HEADS = 8
QK_NOPE = 64
QK_ROPE = 32
V_HEAD = 64
Q_LORA = 256
KV_LORA = 256
ROPE_THETA = 10000.0
S5_GROUP_CH = 16
S5_WIDTH = 512
S5_GROUPS = 32
S5_STATE = 64
XATTN_HEADS = 4
XATTN_HEAD_DIM = 128
N_EXPERTS = 64
TOP_K = 8
N_EXPERT_GROUPS = 8
TOPK_GROUPS = 4
EXPERT_FF = 256
SHARED_FF = 256
ROUTED_SCALE = 2.5
LN_EPS = 1e-5
RMS_EPS = 1e-6
DEPTH = 1
DEEPNORM_ALPHA = (2.0 * DEPTH) ** 0.25

LANES = 128
HEAD_PAD = 128
ROPE_LO = QK_NOPE
ROPE_HALF = QK_ROPE // 2

TM = 512
TQ = 1024
KV_GROUPS = (2, 1)
S5_CHUNK = 16
TW = 512
OH_ROWS = 512
PR = 16
PAD_ROWS = PR
SEG_ALIGN = 8
SORT_ROWS = TOP_K * TW + N_EXPERTS * SEG_ALIGN
TS = SORT_ROWS + PAD_ROWS
XW = D_MODEL // 2 + LANES
CR = 512
PPC = CR // PR
PLAN_UNROLL = 8
GATHER_AHEAD = 2
N_SLOTS = GATHER_AHEAD + 1
POS_RADIX = 64
POS_NONE_Q = 127
assert SORT_ROWS <= POS_RADIX * POS_NONE_Q and SORT_ROWS % OH_ROWS == 0
VMEM_LIMIT = 48 * 1024 * 1024


def _cparams(*sem):
    return pltpu.CompilerParams(dimension_semantics=sem, vmem_limit_bytes=VMEM_LIMIT)


def _dot(a, b):
    return jnp.dot(a, b, preferred_element_type=F32)


def _dot_nt(a, b, precision=None):
    return lax.dot_general(a, b, (((1,), (1,)), ((), ())), preferred_element_type=F32,
                           precision=precision)


def _dot_exact(a, b):
    return jnp.dot(a, b, preferred_element_type=F32, precision=lax.Precision.HIGHEST)


def _layer_norm(h, g, b):
    mu = jnp.mean(h, axis=-1, keepdims=True)
    c = h - mu
    var = jnp.mean(c * c, axis=-1, keepdims=True)
    return c * lax.rsqrt(var + LN_EPS) * g + b


def _rms_norm(h, g):
    return h * lax.rsqrt(jnp.mean(h * h, axis=-1, keepdims=True) + RMS_EPS) * g


def _full(shape):
    n = len(shape)
    return pl.BlockSpec(shape, lambda *_: (0,) * n)


def _rope_angle_kernel(pos_ref, invf_ref, cos_ref, sin_ref):
    ang = pos_ref[...].astype(F32) * invf_ref[...]
    cos_ref[...] = jnp.cos(ang)
    sin_ref[...] = jnp.sin(ang)


def _rope_tables(positions):
    T = positions.size
    rows = T * ROPE_HALF // LANES
    pos_rep = jnp.repeat(positions.reshape(T), ROPE_HALF).reshape(rows, LANES)
    inv_freq = ROPE_THETA ** (-jnp.arange(0, QK_ROPE, 2, dtype=F32) / QK_ROPE)
    invf = jnp.tile(inv_freq, LANES // ROPE_HALF).reshape(1, LANES)
    cos, sin = pl.pallas_call(
        _rope_angle_kernel,
        out_shape=(jax.ShapeDtypeStruct((rows, LANES), F32),) * 2,
        name="rope_angles",
    )(pos_rep, invf)
    cos = cos.reshape(T, ROPE_HALF)
    sin = sin.reshape(T, ROPE_HALF)
    one = jnp.ones((T, ROPE_LO), F32)
    zero_lo = jnp.zeros((T, ROPE_LO), F32)
    zero_h = jnp.zeros((T, ROPE_HALF), F32)
    tail1 = jnp.ones((T, HEAD_PAD - ROPE_LO - QK_ROPE), F32)
    tail0 = jnp.zeros((T, HEAD_PAD - ROPE_LO - QK_ROPE), F32)
    c_tab = jnp.concatenate([one, cos, cos, tail1], axis=1)
    s_up = jnp.concatenate([zero_lo, -sin, zero_h, tail0], axis=1)
    s_dn = jnp.concatenate([zero_lo, zero_h, sin, tail0], axis=1)
    return c_tab, s_up, s_dn


def _rope(x, c_tab, s_up, s_dn):
    return (x * c_tab + pltpu.roll(x, HEAD_PAD - ROPE_HALF, axis=1) * s_up
            + pltpu.roll(x, ROPE_HALF, axis=1) * s_dn)


def _inproj_kernel(x_ref, c_ref, su_ref, sd_ref, wlat_ref, wu_ref, wgm_ref, wgs_ref, qg_ref, kvg_ref,
                   wuq_ref, wuk_ref, wuv_ref, q_ref, k_ref, v_ref, u_ref, gm_ref, gs_ref):
    xb = x_ref[...].astype(BF16)
    lat = _dot(xb, wlat_ref[...])
    qn = _rms_norm(lat[:, :Q_LORA], qg_ref[...]).astype(BF16)
    kvn = _rms_norm(lat[:, Q_LORA:Q_LORA + KV_LORA], kvg_ref[...]).astype(BF16)
    c_tab, s_up, s_dn = c_ref[...], su_ref[...], sd_ref[...]
    k_rope = _rope(lat[:, Q_LORA + KV_LORA:], c_tab, s_up, s_dn)
    q = _dot(qn, wuq_ref[...])
    k = _dot(kvn, wuk_ref[...])
    scale = (QK_NOPE + QK_ROPE) ** -0.5 * math.log2(math.e)
    for h in range(MLA_HEADS):
        sl = slice(h * HEAD_PAD, (h + 1) * HEAD_PAD)
        q_ref[:, sl] = (_rope(q[:, sl], c_tab, s_up, s_dn) * scale).astype(BF16)
        k_ref[:, sl] = (k[:, sl] + k_rope).astype(BF16)
    v = _dot(kvn, wuv_ref[...])
    ones_lane = lax.broadcasted_iota(I32, (1, v.shape[1]), 1) % HEAD_PAD == V_HEAD
    v_ref[...] = jnp.where(ones_lane, 1.0, v).astype(BF16)
    u_ref[...] = _dot(xb, wu_ref[...])
    gm_ref[...] = _dot(xb, wgm_ref[...]).astype(BF16)
    gs_ref[...] = _dot(xb, wgs_ref[...]).astype(BF16)


def _pad_heads(w, head_w, lo_w):
    K = w.shape[0]
    w = w.reshape(K, MLA_HEADS, head_w)[:, :, :lo_w]
    w = jnp.pad(w, ((0, 0), (0, 0), (0, HEAD_PAD - lo_w)))
    return w.reshape(K, MLA_HEADS * HEAD_PAD)


def _inproj(x2d, tabs, w_in, q_norm_g, kv_norm_g, w_uq, w_ukv):
    T = x2d.shape[0]
    o_rope = Q_LORA + KV_LORA
    o_u = o_rope + QK_ROPE
    o_gm = o_u + S5_WIDTH
    o_gs = o_gm + D_MODEL
    w_rope = jnp.pad(w_in[:, o_rope:o_u], ((0, 0), (ROPE_LO, HEAD_PAD - ROPE_LO - QK_ROPE)))
    w_lat = jnp.concatenate([w_in[:, :o_rope], w_rope], axis=1).astype(BF16)
    w_u = w_in[:, o_u:o_gm].astype(BF16)
    w_gm = w_in[:, o_gm:o_gs].astype(BF16)
    w_gs = w_in[:, o_gs:].astype(BF16)
    wuq = _pad_heads(w_uq, QK_NOPE + QK_ROPE, QK_NOPE + QK_ROPE).astype(BF16)
    kv3 = w_ukv.reshape(KV_LORA, MLA_HEADS, QK_NOPE + V_HEAD)
    wuk = _pad_heads(kv3[:, :, :QK_NOPE].reshape(KV_LORA, -1), QK_NOPE, QK_NOPE).astype(BF16)
    wuv = _pad_heads(kv3[:, :, QK_NOPE:].reshape(KV_LORA, -1), V_HEAD, V_HEAD).astype(BF16)
    HP = MLA_HEADS * HEAD_PAD
    tile = lambda w: pl.BlockSpec((TM, w), lambda i: (i, 0))
    return pl.pallas_call(
        _inproj_kernel,
        grid=(T // TM,),
        in_specs=[tile(D_MODEL), tile(HEAD_PAD), tile(HEAD_PAD), tile(HEAD_PAD),
                  _full(w_lat.shape), _full(w_u.shape), _full(w_gm.shape), _full(w_gs.shape),
                  _full((1, Q_LORA)), _full((1, KV_LORA)),
                  _full(wuq.shape), _full(wuk.shape), _full(wuv.shape)],
        out_specs=[tile(HP), tile(HP), tile(HP), tile(S5_WIDTH), tile(D_MODEL), tile(D_MODEL)],
        out_shape=[jax.ShapeDtypeStruct((T, HP), BF16)] * 3
        + [jax.ShapeDtypeStruct((T, S5_WIDTH), F32)]
        + [jax.ShapeDtypeStruct((T, D_MODEL), BF16)] * 2,
        compiler_params=_cparams("parallel"),
        name="inproj_mla_prep",
    )(x2d, *tabs, w_lat, w_u, w_gm, w_gs, q_norm_g.reshape(1, -1), kv_norm_g.reshape(1, -1),
      wuq, wuk, wuv)


def _attn_kernel(q_ref, k_ref, v_ref, o_ref):
    qi = pl.program_id(2)
    q = q_ref[0]

    def step(blk, n_blk, carry, masked):
        m, acc = carry
        tk = n_blk * TQ
        start = pl.multiple_of(blk * TQ, TQ)
        k = k_ref[0, pl.ds(start, tk), :]
        v = v_ref[0, pl.ds(start, tk), :]
        s = _dot_nt(q, k)
        if masked:
            row = lax.broadcasted_iota(I32, (TQ, tk), 0)
            col = lax.broadcasted_iota(I32, (TQ, tk), 1)
            s = jnp.where(col <= row, s, -jnp.inf)
        m_new = jnp.maximum(m, jnp.max(s, axis=-1, keepdims=True))
        p = jnp.exp2(s - m_new)
        acc = jnp.exp2(m - m_new) * acc + _dot(p.astype(BF16), v)
        return m_new, acc

    carry = (jnp.full((TQ, 1), -jnp.inf, F32), jnp.zeros((TQ, HEAD_PAD), F32))
    done = 0
    for n_blk in KV_GROUPS:
        base = done
        n_it = (qi - base) // n_blk
        carry = lax.fori_loop(0, n_it, lambda j, c, base=base, n_blk=n_blk: step(base + j * n_blk, n_blk, c, False),
                              carry)
        done = base + n_it * n_blk
    acc = step(qi, 1, carry, True)[1]
    o_ref[0] = (acc / acc[:, V_HEAD:V_HEAD + 1]).astype(BF16)


def _attention(q, k, v, B, S):
    HP = MLA_HEADS * HEAD_PAD
    q, k, v = (a.reshape(B, S, HP) for a in (q, k, v))
    o = pl.pallas_call(
        _attn_kernel,
        grid=(B, MLA_HEADS, S // TQ),
        in_specs=[pl.BlockSpec((1, TQ, HEAD_PAD), lambda b, h, i: (b, i, h)),
                  pl.BlockSpec((1, S, HEAD_PAD), lambda b, h, i: (b, 0, h)),
                  pl.BlockSpec((1, S, HEAD_PAD), lambda b, h, i: (b, 0, h))],
        out_specs=pl.BlockSpec((1, TQ, HEAD_PAD), lambda b, h, i: (b, i, h)),
        out_shape=jax.ShapeDtypeStruct((B, S, HP), BF16),
        compiler_params=_cparams("parallel", "parallel", "arbitrary"),
        name="mla_flash_attention",
    )(q, k, v)
    return o.reshape(B * S, HP)


CH2 = S5_CHUNK * S5_GROUP_CH
P2 = 2 * S5_STATE
SCAN_LEVELS_MAX = 16


def _s5_prep_kernel(n_levels, arow_ref, acol_ref, bt_ref, ct_ref, d_ref,
                    mt_ref, wt_ref, vt_ref, apow_ref, dvec_ref, mt_acc):
    P, H, L = S5_STATE, S5_GROUP_CH, S5_CHUNK

    def powers(ar, ai, dt, n):
        e = jnp.exp(n * (ar * dt))
        ang = n * (ai * dt)
        return e * jnp.cos(ang), e * jnp.sin(ang)

    def zoh_coef(ar, ai, dt):
        pr, pi = powers(ar, ai, dt, 1.0)
        nr, ni = pr - 1.0, pi
        den = ar * ar + ai * ai
        return (nr * ar + ni * ai) / den, (ni * ar - nr * ai) / den

    row = arow_ref[0]
    ar, ai, dt = row[0:1], row[1:2], jnp.exp(row[2:3])
    cr, ci = zoh_coef(ar, ai, dt)
    bt = bt_ref[0]
    btr = jnp.concatenate([bt[:H], bt[:H]], axis=1)
    bti = jnp.concatenate([bt[H:], bt[H:]], axis=1)
    bbr = cr * btr - ci * bti
    bbi = cr * bti + ci * btr
    lane = lax.broadcasted_iota(I32, (1, P2), 1)
    is_re = lane < P
    ridx = lax.broadcasted_iota(I32, (CH2, P2), 0) // H
    pwr, pwi = powers(ar, ai, dt, (L - 1 - ridx).astype(F32))
    rsel = (lax.broadcasted_iota(I32, (CH2, H), 0) % H == lax.broadcasted_iota(I32, (CH2, H), 1)).astype(F32)
    bbr_t = _dot_exact(rsel, bbr)
    bbi_t = _dot_exact(rsel, bbi)
    w_re = pwr * bbr_t - pwi * bbi_t
    w_im = pwr * bbi_t + pwi * bbr_t
    wt_ref[0] = jnp.where(is_re, w_re, w_im).astype(BF16)
    a_r, a_i = powers(ar, ai, dt, float(L))
    sign = jnp.where(is_re, -1.0, 1.0)
    for lvl in range(SCAN_LEVELS_MAX):
        if lvl < n_levels:
            apow_ref[0, lvl:lvl + 1, :] = a_r
            apow_ref[0, SCAN_LEVELS_MAX + lvl:SCAN_LEVELS_MAX + lvl + 1, :] = sign * a_i
            a_r, a_i = a_r * a_r - a_i * a_i, 2.0 * a_r * a_i
        else:
            apow_ref[0, lvl:lvl + 1, :] = jnp.zeros_like(a_r)
            apow_ref[0, SCAN_LEVELS_MAX + lvl:SCAN_LEVELS_MAX + lvl + 1, :] = jnp.zeros_like(a_r)

    col = acol_ref[0]
    arc, aic, dtc = col[:, 0:1], col[:, 1:2], jnp.exp(col[:, 2:3])
    ct = ct_ref[0]
    csel = (lax.broadcasted_iota(I32, (H, CH2), 1) % H == lax.broadcasted_iota(I32, (H, CH2), 0)).astype(F32)
    ctr = _dot_exact(ct[:, :H], csel)
    cti = _dot_exact(ct[:, H:], csel)
    sidx = (lax.broadcasted_iota(I32, (P, CH2), 1) // H).astype(F32)
    pr, pi = powers(arc, aic, dtc, sidx)
    g_r = ctr * pr - cti * pi
    g_i = ctr * pi + cti * pr
    a1r, a1i = powers(arc, aic, dtc, 1.0)
    v_r = g_r * a1r - g_i * a1i
    v_i = g_r * a1i + g_i * a1r
    vt_ref[0, :P, :] = v_r.astype(BF16)
    vt_ref[0, P:, :] = (-v_i).astype(BF16)
    bbr64, bbi64 = bbr[:, :P], bbi[:, :P]
    kt = _dot_exact(bbr64, g_r) - _dot_exact(bbi64, g_i)
    lane2 = lax.broadcasted_iota(I32, (H, CH2), 1)
    for r in range(L):
        blk = kt if r == 0 else jnp.where(lane2 >= r * H, pltpu.roll(kt, r * H, axis=1), 0.0)
        mt_acc[r * H:(r + 1) * H, :] = blk
    mt_ref[0] = mt_acc[...].astype(BF16)
    dvec_ref[0] = _dot_exact(d_ref[0], csel)


def _s5_prep(n_levels, s5_a_re, s5_a_im, s5_log_dt, s5_b_re, s5_b_im, s5_c_re, s5_c_im, s5_d):
    G, P, H = S5_GROUPS, S5_STATE, S5_GROUP_CH
    ldt = jnp.broadcast_to(s5_log_dt[:, None], (G, P))
    arow = jnp.stack([s5_a_re, s5_a_im, ldt], axis=1)
    arow = jnp.concatenate([arow, arow], axis=2)
    acol = jnp.stack([s5_a_re, s5_a_im, ldt], axis=2)
    bt = jnp.concatenate([s5_b_re.transpose(0, 2, 1), s5_b_im.transpose(0, 2, 1)], axis=1)
    ct = jnp.concatenate([s5_c_re.transpose(0, 2, 1), s5_c_im.transpose(0, 2, 1)], axis=2)
    d = s5_d.reshape(G, 1, H)
    blk = lambda *s: pl.BlockSpec((1,) + s, lambda g: (g, 0, 0))
    return pl.pallas_call(
        functools.partial(_s5_prep_kernel, n_levels),
        grid=(G,),
        in_specs=[blk(3, P2), blk(P, 3), blk(2 * H, P), blk(P, 2 * H), blk(1, H)],
        out_specs=[blk(CH2, CH2), blk(CH2, P2), blk(P2, CH2), blk(2 * SCAN_LEVELS_MAX, P2), blk(1, CH2)],
        out_shape=[jax.ShapeDtypeStruct((G, CH2, CH2), BF16),
                   jax.ShapeDtypeStruct((G, CH2, P2), BF16),
                   jax.ShapeDtypeStruct((G, P2, CH2), BF16),
                   jax.ShapeDtypeStruct((G, 2 * SCAN_LEVELS_MAX, P2), F32),
                   jax.ShapeDtypeStruct((G, 1, CH2), F32)],
        scratch_shapes=[pltpu.VMEM((CH2, CH2), F32)],
        compiler_params=_cparams("parallel"),
        name="s5_discretise",
    )(arow, acol, bt, ct, d)


def _gelu_tanh(y):
    return 0.5 * y * (1.0 + jnp.tanh(math.sqrt(2.0 / math.pi) * (y + 0.044715 * (y * y * y))))


GROUPS_PER_TILE = LANES // S5_GROUP_CH
STEPS_PER_TILE = LANES // S5_GROUP_CH


def _s5_scan_kernel(n_levels, u_ref, mt_ref, wt_ref, vt_ref, apow_ref, dvec_ref, y_ref):
    H, L = S5_GROUP_CH, S5_CHUNK
    nc = u_ref.shape[0] // L
    lane_blk = lax.broadcasted_iota(I32, (nc, LANES), 1) // H
    ridx = lax.broadcasted_iota(I32, (nc, P2), 0)
    steps = [u_ref[pl.ds(l, nc, stride=L), :] for l in range(L)]
    outs = [jnp.zeros((nc, LANES), F32) for _ in range(L)]
    for gi in range(GROUPS_PER_TILE):
        halves = []
        for hh in range(L // STEPS_PER_TILE):
            acc = jnp.zeros((nc, LANES), F32)
            for l8 in range(STEPS_PER_TILE):
                src = steps[hh * STEPS_PER_TILE + l8]
                shift = ((l8 - gi) % STEPS_PER_TILE) * H
                moved = src if shift == 0 else pltpu.roll(src, shift, axis=1)
                acc = jnp.where(lane_blk == l8, moved, acc)
            halves.append(acc)
        u = jnp.concatenate(halves, axis=1)
        ub = u.astype(BF16)
        x = _dot(ub, wt_ref[gi])
        apow = apow_ref[gi]
        for lvl in range(n_levels):
            sh = 1 << lvl
            prev = jnp.where(ridx >= sh, pltpu.roll(x, sh, axis=0), 0.0)
            a_r = apow[lvl:lvl + 1]
            a_i = apow[SCAN_LEVELS_MAX + lvl:SCAN_LEVELS_MAX + lvl + 1]
            x = x + a_r * prev + a_i * pltpu.roll(prev, S5_STATE, axis=1)
        x_in = jnp.where(ridx >= 1, pltpu.roll(x, 1, axis=0), 0.0)
        y = _gelu_tanh(_dot(ub, mt_ref[gi]) + _dot(x_in.astype(BF16), vt_ref[gi]) + dvec_ref[gi] * u)
        for l in range(L):
            hh, l8 = divmod(l, STEPS_PER_TILE)
            src = y[:, hh * LANES:(hh + 1) * LANES]
            shift = ((gi - l8) % STEPS_PER_TILE) * H
            moved = src if shift == 0 else pltpu.roll(src, shift, axis=1)
            outs[l] = jnp.where(lane_blk == gi, moved, outs[l])
    for l in range(L):
        y_ref[pl.ds(l, nc, stride=L), :] = outs[l]


def _s5_branch(u, B, S, s5_params):
    G, L = S5_GROUPS, S5_CHUNK
    nc = S // L
    n_levels = max(1, (nc - 1).bit_length())
    assert n_levels <= SCAN_LEVELS_MAX and S5_CHUNK % STEPS_PER_TILE == 0
    mt, wt, vt, apow, dvec = _s5_prep(n_levels, *s5_params)
    n_lt = G // GROUPS_PER_TILE
    blk = lambda *s: pl.BlockSpec((GROUPS_PER_TILE,) + s, lambda b, t: (t, 0, 0))
    io = pl.BlockSpec((S, LANES), lambda b, t: (b, t))
    return pl.pallas_call(
        functools.partial(_s5_scan_kernel, n_levels),
        grid=(B, n_lt),
        in_specs=[io, blk(CH2, CH2), blk(CH2, P2), blk(P2, CH2), blk(2 * SCAN_LEVELS_MAX, P2), blk(1, CH2)],
        out_specs=io,
        out_shape=jax.ShapeDtypeStruct((B * S, S5_WIDTH), F32),
        compiler_params=_cparams("parallel", "parallel"),
        name="s5_chunk_scan",
    )(u, mt, wt, vt, apow, dvec)


def _merge_kernel(x_ref, o_ref, ys_ref, gm_ref, gs_ref, wmo_ref, wglu_ref, wout_ref, g_ref, b_ref, x1_ref):
    y_mla = _dot(o_ref[...], wmo_ref[...])
    z = _dot(ys_ref[...].astype(BF16), wglu_ref[...])
    y_s5 = z[:, :D_MODEL] * jax.nn.sigmoid(z[:, D_MODEL:])
    merged = (jax.nn.sigmoid(gm_ref[...].astype(F32)) * y_mla
              + jax.nn.sigmoid(gs_ref[...].astype(F32)) * y_s5)
    mix = _dot(merged.astype(BF16), wout_ref[...])
    x1_ref[...] = _layer_norm(DEEPNORM_ALPHA * x_ref[...] + mix, g_ref[...], b_ref[...])


def _merge(x2d, o, ys, gm, gs, w_mla_o, w_s5_glu, w_out, ln_g, ln_b):
    T = x2d.shape[0]
    HP = MLA_HEADS * HEAD_PAD
    wmo = jnp.pad(w_mla_o.reshape(MLA_HEADS, V_HEAD, D_MODEL), ((0, 0), (0, HEAD_PAD - V_HEAD), (0, 0)))
    wmo = wmo.reshape(HP, D_MODEL).astype(BF16)
    tile = lambda w: pl.BlockSpec((TM, w), lambda i: (i, 0))
    return pl.pallas_call(
        _merge_kernel,
        grid=(T // TM,),
        in_specs=[tile(D_MODEL), tile(HP), tile(S5_WIDTH), tile(D_MODEL), tile(D_MODEL),
                  _full((HP, D_MODEL)), _full((S5_WIDTH, 2 * D_MODEL)), _full((D_MODEL, D_MODEL)),
                  _full((1, D_MODEL)), _full((1, D_MODEL))],
        out_specs=tile(D_MODEL),
        out_shape=jax.ShapeDtypeStruct((T, D_MODEL), F32),
        compiler_params=_cparams("parallel"),
        name="merge_outproj_ln1",
    )(x2d, o, ys, gm, gs, wmo, w_s5_glu.astype(BF16), w_out.astype(BF16),
      ln_g.reshape(1, -1), ln_b.reshape(1, -1))


def _memkv_kernel(mem_ref, g_ref, b_ref, w_ref, k_ref, v_ref):
    m = _layer_norm(mem_ref[0], g_ref[...], b_ref[...]).astype(BF16)
    kv = _dot(m, w_ref[...])
    hd = XATTN_HEADS * XATTN_HEAD_DIM
    k_ref[0] = kv[:, :hd].astype(BF16)
    v_ref[0] = kv[:, hd:].astype(BF16)


def _xattn_kernel(x1_ref, k_ref, v_ref, wq_ref, wo_ref, g_ref, b_ref, x2_ref):
    x1 = x1_ref[0]
    q = (_dot(x1.astype(BF16), wq_ref[...]) * (XATTN_HEAD_DIM ** -0.5)).astype(BF16)
    k = k_ref[0]
    v = v_ref[0]
    outs = []
    for h in range(XATTN_HEADS):
        sl = slice(h * XATTN_HEAD_DIM, (h + 1) * XATTN_HEAD_DIM)
        s = _dot_nt(q[:, sl], k[:, sl])
        p = jnp.exp(s - jnp.max(s, axis=-1, keepdims=True))
        o = _dot(p.astype(BF16), v[:, sl]) / jnp.sum(p, axis=-1, keepdims=True)
        outs.append(o.astype(BF16))
    xa = _dot(jnp.concatenate(outs, axis=1), wo_ref[...])
    x2 = _layer_norm(DEEPNORM_ALPHA * x1 + xa, g_ref[...], b_ref[...])
    x2_ref[0] = x2


def _cross_attention(x1, mem, B, S, mem_ln_g, mem_ln_b, w_xq, w_xkv, w_xo, ln_g, ln_b):
    M = mem.shape[1]
    hd = XATTN_HEADS * XATTN_HEAD_DIM
    row = lambda a: a.reshape(1, -1)
    k, v = pl.pallas_call(
        _memkv_kernel,
        grid=(B,),
        in_specs=[pl.BlockSpec((1, M, D_MODEL), lambda b: (b, 0, 0)), _full((1, D_MODEL)), _full((1, D_MODEL)),
                  _full((D_MODEL, 2 * hd))],
        out_specs=[pl.BlockSpec((1, M, hd), lambda b: (b, 0, 0))] * 2,
        out_shape=[jax.ShapeDtypeStruct((B, M, hd), BF16)] * 2,
        compiler_params=_cparams("parallel"),
        name="memory_kv",
    )(mem, row(mem_ln_g), row(mem_ln_b), w_xkv.astype(BF16))
    tile = lambda w: pl.BlockSpec((1, TM, w), lambda b, i: (b, i, 0))
    x2 = pl.pallas_call(
        _xattn_kernel,
        grid=(B, S // TM),
        in_specs=[tile(D_MODEL),
                  pl.BlockSpec((1, M, hd), lambda b, i: (b, 0, 0)), pl.BlockSpec((1, M, hd), lambda b, i: (b, 0, 0)),
                  _full((D_MODEL, hd)), _full((hd, D_MODEL)), _full((1, D_MODEL)), _full((1, D_MODEL))],
        out_specs=tile(D_MODEL),
        out_shape=jax.ShapeDtypeStruct((B, S, D_MODEL), F32),
        compiler_params=_cparams("parallel", "parallel"),
        name="cross_attention_ln2",
    )(x1.reshape(B, S, D_MODEL), k, v, w_xq.astype(BF16), w_xo.astype(BF16), row(ln_g), row(ln_b))
    return x2.reshape(B * S, D_MODEL)


def _route_sort_kernel(x_ref, w_ref, bias_ref, xs_ref, pos_ref, cnt_ref, off_ref):
    E, per = N_EXPERTS, N_EXPERTS // N_EXPERT_GROUPS
    tm = x_ref.shape[0]
    x = x_ref[...]
    logits = _dot_nt(w_ref[...], x, precision=lax.Precision.HIGHEST)
    scores = jax.nn.sigmoid(logits)
    sel = scores + bias_ref[...]
    neg = -jnp.inf
    i8 = lax.broadcasted_iota(I32, (per, tm), 0)
    gscore = []
    for g in range(N_EXPERT_GROUPS):
        blk = sel[g * per:(g + 1) * per]
        m1 = jnp.max(blk, axis=0, keepdims=True)
        i1 = jnp.min(jnp.where(blk == m1, i8, per), axis=0, keepdims=True)
        m2 = jnp.max(jnp.where(i8 == i1, neg, blk), axis=0, keepdims=True)
        gscore.append(m1 + m2)
    blocks = []
    for g in range(N_EXPERT_GROUPS):
        ahead = jnp.zeros((1, tm), I32)
        for o in range(N_EXPERT_GROUPS):
            if o == g:
                continue
            before = (gscore[o] >= gscore[g]) if o < g else (gscore[o] > gscore[g])
            ahead = ahead + before.astype(I32)
        blocks.append(jnp.where(ahead < TOPK_GROUPS, sel[g * per:(g + 1) * per], neg))
    cur = jnp.concatenate(blocks, axis=0)
    ie = lax.broadcasted_iota(I32, (E, tm), 0)
    picked = jnp.zeros((E, tm), F32)
    idxs = []
    for _ in range(TOP_K):
        m = jnp.max(cur, axis=0, keepdims=True)
        idx = jnp.min(jnp.where(cur == m, ie, E), axis=0, keepdims=True)
        hit = ie == idx
        picked = jnp.where(hit, 1.0, picked)
        cur = jnp.where(hit, neg, cur)
        idxs.append(idx)
    wsel = scores * picked
    wnorm = wsel / jnp.sum(wsel, axis=0, keepdims=True) * ROUTED_SCALE
    pb = picked.astype(BF16)
    tri = (lax.broadcasted_iota(I32, (tm, tm), 0) <= lax.broadcasted_iota(I32, (tm, tm), 1)).astype(BF16)
    incl = _dot(pb, tri)
    cnt_col = jnp.sum(picked, axis=1, keepdims=True)
    lower = (lax.broadcasted_iota(I32, (E, E), 1) < lax.broadcasted_iota(I32, (E, E), 0)).astype(F32)
    al_col = jnp.floor((cnt_col + (SEG_ALIGN - 1.0)) * (1.0 / SEG_ALIGN)) * SEG_ALIGN
    off_col = _dot_exact(lower, jnp.broadcast_to(al_col, (E, LANES)))[:, 0:1]
    pos = off_col + incl - 1.0
    q = jnp.floor(pos * (1.0 / POS_RADIX))
    pq = jnp.concatenate([jnp.where(picked > 0.0, POS_RADIX * q, POS_RADIX * POS_NONE_Q),
                          jnp.where(picked > 0.0, pos - POS_RADIX * q, 0.0)], axis=0)
    pos_ref[...] = pq
    pqb = pq.astype(BF16)
    cnt_ref[0] = jnp.broadcast_to(cnt_col, (E, LANES)).astype(I32)
    off_ref[0] = jnp.broadcast_to(off_col, (E, LANES)).astype(I32)
    cnt_row = _dot_nt(jnp.ones((8, tm), BF16), pb)
    al_row = jnp.floor((cnt_row + (SEG_ALIGN - 1.0)) * (1.0 / SEG_ALIGN)) * SEG_ALIGN
    er = lax.broadcasted_iota(I32, (E, 3 * E), 0)
    ec = lax.broadcasted_iota(I32, (E, 3 * E), 1) % E
    off_row3 = _dot_exact(al_row, (er < ec).astype(F32))[0:1]
    end_row3 = off_row3 + _dot_exact(al_row, (er == ec).astype(F32))[0:1]
    w_hi = wnorm.astype(BF16)
    w_r1 = wnorm - w_hi.astype(F32)
    w_mid = w_r1.astype(BF16)
    w_lo = (w_r1 - w_mid.astype(F32)).astype(BF16)
    w3 = jnp.concatenate([w_hi, w_mid, w_lo], axis=0)

    xb = x.astype(BF16)
    half = D_MODEL // 2

    def build(c, carry):
        r0 = pl.multiple_of(c * OH_ROWS, OH_ROWS)
        jr = (lax.broadcasted_iota(I32, (OH_ROWS, 3 * E), 0) + r0).astype(F32)
        member3 = jnp.where(jr >= off_row3, jnp.where(jr < end_row3, 1.0, 0.0), 0.0)
        target = _dot(member3[:, :2 * E].astype(BF16), pqb)
        j = (lax.broadcasted_iota(I32, (OH_ROWS, tm), 0) + r0).astype(F32)
        ohb = jnp.where(target == j, 1.0, 0.0).astype(BF16)
        rows = _dot(ohb, xb)
        xs_ref[0, pl.ds(r0, OH_ROWS), 0:half] = _pack_bf16_pair(rows[:, :half], rows[:, half:])
        wrow = jnp.sum(member3 * _dot_nt(ohb, w3), axis=1, keepdims=True)
        xs_ref[0, pl.ds(r0, OH_ROWS), half:] = pltpu.bitcast(jnp.broadcast_to(wrow, (OH_ROWS, LANES)), I32)
        return carry

    lax.fori_loop(0, SORT_ROWS // OH_ROWS, build, 0)
    xs_ref[0, SORT_ROWS:, :] = jnp.zeros((PAD_ROWS, XW), I32)


def _route_sort(x2, w_router, router_bias):
    T = x2.shape[0]
    E = N_EXPERTS
    nt = T // TW
    seg = pl.BlockSpec((1, E, LANES), lambda i: (i, 0, 0))
    return pl.pallas_call(
        _route_sort_kernel,
        grid=(nt,),
        in_specs=[pl.BlockSpec((TW, D_MODEL), lambda i: (i, 0)), _full((E, D_MODEL)), _full((E, 1))],
        out_specs=[pl.BlockSpec((1, TS, XW), lambda i: (i, 0, 0)),
                   pl.BlockSpec((2 * E, TW), lambda i: (0, i)), seg, seg],
        out_shape=[jax.ShapeDtypeStruct((nt, TS, XW), I32), jax.ShapeDtypeStruct((2 * E, T), F32),
                   jax.ShapeDtypeStruct((nt, E, LANES), I32), jax.ShapeDtypeStruct((nt, E, LANES), I32)],
        compiler_params=_cparams("parallel"),
        name="route_local_sort",
    )(x2, w_router.T, router_bias.reshape(E, 1))


def _plan_kernel(nt, n_pieces_max, n_chunks_max, cnt_ref, off_ref, pa_ref, ce_ref, nch_ref):
    def unused(p, z):
        pa_ref[p] = -SEG_ALIGN
        return z

    def per_expert(e, carry):
        p0, g = carry

        def per_tile(i, p):
            n = cnt_ref[i * N_EXPERTS + e]
            base = i * TS + off_ref[i * N_EXPERTS + e]
            n_pc = (n + PR - 1) // PR

            def put(q, c):
                pa_ref[p + q] = base + q * PR
                return c

            for q in range(PLAN_UNROLL):
                pa_ref[p + q] = base + q * PR
            lax.fori_loop(PLAN_UNROLL, n_pc, put, 0)
            return p + n_pc

        p1 = lax.fori_loop(0, nt, per_tile, p0)
        n_ch = (p1 - p0 + PPC - 1) // PPC

        lax.fori_loop(p1, p0 + n_ch * PPC, unused, 0)

        def mark(c, z):
            ce_ref[g + c] = e
            return z

        lax.fori_loop(0, n_ch, mark, 0)
        return p0 + n_ch * PPC, g + n_ch

    p_end, g = lax.fori_loop(0, N_EXPERTS, per_expert, (jnp.int32(0), jnp.int32(0)))
    lax.fori_loop(p_end, n_pieces_max, unused, 0)
    nch_ref[0] = g

    def tail(c, z):
        ce_ref[c] = N_EXPERTS - 1
        return z

    lax.fori_loop(g, n_chunks_max, tail, 0)


def _plan(cnt, off, nt, n_pieces_max, n_chunks_max):
    smem = pl.BlockSpec(memory_space=pltpu.SMEM)
    return pl.pallas_call(
        functools.partial(_plan_kernel, nt, n_pieces_max, n_chunks_max),
        in_specs=[smem, smem],
        out_specs=[smem, smem, smem],
        out_shape=[jax.ShapeDtypeStruct((n_pieces_max,), I32), jax.ShapeDtypeStruct((n_chunks_max,), I32),
                   jax.ShapeDtypeStruct((1,), I32)],
        name="moe_plan",
    )(cnt, off)


def _unpack_bf16_pair(w):
    lo = pltpu.bitcast(lax.shift_left(w, jnp.int32(16)), F32).astype(BF16)
    hi = pltpu.bitcast(w & jnp.int32(-65536), F32).astype(BF16)
    return lo, hi


def _pack_bf16_pair(lo, hi):
    lo_bits = lax.shift_right_logical(pltpu.bitcast(lo.astype(BF16).astype(F32), I32), jnp.int32(16))
    hi_bits = pltpu.bitcast(hi.astype(BF16).astype(F32), I32)
    return hi_bits | lo_bits


def _expert_kernel(ce_ref, pa_ref, nch_ref, tot_ref, xs_hbm, wgu_ref, wdn_ref, y_hbm,
                   xbuf, ybuf, zbuf, wgu_b, wdn_b, gsem, wsem, zsem):
    g = pl.program_id(0)
    n = nch_ref[0]
    slot = g % N_SLOTS
    half = D_MODEL // 2

    dummy_base = (tot_ref.shape[0] - 1) * TS

    def gather_copy(s, a, jp):
        return pltpu.make_async_copy(xs_hbm.at[pl.ds(pl.multiple_of(a, SEG_ALIGN), PR), :],
                                     xbuf.at[s, pl.ds(jp * PR, PR), :], gsem.at[s])

    def write_copy(s, a, jp):
        return pltpu.make_async_copy(ybuf.at[s, pl.ds(jp * PR, PR), :],
                                     y_hbm.at[pl.ds(pl.multiple_of(a, SEG_ALIGN), PR), :], wsem.at[s])

    def gather_start(c, s):
        for jp in range(PPC):
            gather_copy(s, jnp.maximum(pa_ref[c * PPC + jp], 0), jp).start()

    def write_start(c, s):
        for jp in range(PPC):
            a = pa_ref[c * PPC + jp]
            write_copy(s, jnp.where(a >= 0, a, dummy_base + jp * PR), jp).start()

    def gather_wait(s):
        for jp in range(PPC):
            gather_copy(s, 0, jp).wait()

    def write_wait(s):
        for jp in range(PPC):
            write_copy(s, 0, jp).wait()

    @pl.when(g == 0)
    def _():
        gather_start(0, 0)
        ybuf[...] = jnp.zeros_like(ybuf)
        zbuf[...] = jnp.zeros_like(zbuf)

        def zero_copy(row):
            return pltpu.make_async_copy(zbuf, y_hbm.at[pl.ds(pl.multiple_of(row, SEG_ALIGN), SEG_ALIGN), :], zsem)

        def tail_of(i):
            return i * TS + tot_ref[i], (TS - tot_ref[i]) // SEG_ALIGN

        def zero_start(i, c):
            base, cnt8 = tail_of(i)
            lax.fori_loop(0, cnt8, lambda q, z: (zero_copy(base + q * SEG_ALIGN).start(), z)[1], 0)
            return c

        def zero_wait(i, c):
            base, cnt8 = tail_of(i)
            lax.fori_loop(0, cnt8, lambda q, z: (zero_copy(base + q * SEG_ALIGN).wait(), z)[1], 0)
            return c

        lax.fori_loop(0, tot_ref.shape[0], zero_start, 0)
        lax.fori_loop(0, tot_ref.shape[0], zero_wait, 0)

    @pl.when((g == 0) & (n > 1))
    def _():
        gather_start(1, 1)

    @pl.when(g + GATHER_AHEAD < n)
    def _():
        gather_start(g + GATHER_AHEAD, (g + GATHER_AHEAD) % N_SLOTS)

    e_cur = ce_ref[g]
    e_prev = ce_ref[jnp.maximum(g - 1, 0)]
    e_prev2 = ce_ref[jnp.maximum(g - 2, 0)]

    @pl.when(g < n)
    def _():
        @pl.when((g == 0) | (e_cur != e_prev))
        def _():
            wgu_b[...] = wgu_ref[0].astype(BF16)
            wdn_b[...] = wdn_ref[0].astype(BF16)

        gather_wait(slot)

        @pl.when((g >= 2) & (e_prev == e_prev2))
        def _():
            write_wait((g - 2) % N_SLOTS)

        @pl.when((g >= 1) & (e_cur != e_prev))
        def _():
            write_wait((g - 1) % N_SLOTS)

        def swiglu_rows(rows):
            xw = xbuf[slot, :rows, :]
            lo, hi = _unpack_bf16_pair(xw[:, :half])
            h = _dot(jnp.concatenate([lo, hi], axis=1), wgu_b[...])
            gate, up = h[:, :EXPERT_FF], h[:, EXPERT_FF:]
            act = (gate * jax.nn.sigmoid(gate) * up).astype(BF16)
            y = _dot(act, wdn_b[...])
            w_row = pltpu.bitcast(xw[:, half:], F32)
            y = y * jnp.concatenate([w_row] * (D_MODEL // LANES), axis=1)
            ybuf[slot, :rows, :] = _pack_bf16_pair(y[:, :half], y[:, half:])

        second_half_used = pa_ref[g * PPC + PPC // 2] >= 0

        @pl.when(second_half_used)
        def _():
            swiglu_rows(CR)

        @pl.when(jnp.logical_not(second_half_used))
        def _():
            swiglu_rows(CR // 2)

        write_start(g, slot)

    @pl.when(g == pl.num_programs(0) - 1)
    def _():
        @pl.when((n >= 2) & (ce_ref[jnp.maximum(n - 1, 0)] == ce_ref[jnp.maximum(n - 2, 0)]))
        def _():
            write_wait((n - 2) % N_SLOTS)

        write_wait((n - 1) % N_SLOTS)


def _experts(ce, pa, nch, tot, xs, w_exp_gu, w_exp_down):
    n_rows = xs.shape[0]
    half = D_MODEL // 2
    grid_spec = pltpu.PrefetchScalarGridSpec(
        num_scalar_prefetch=4,
        grid=(ce.shape[0],),
        in_specs=[pl.BlockSpec(memory_space=pl.ANY),
                  pl.BlockSpec((1, D_MODEL, 2 * EXPERT_FF), lambda g, ce, pa, nch, tot: (ce[g], 0, 0)),
                  pl.BlockSpec((1, EXPERT_FF, D_MODEL), lambda g, ce, pa, nch, tot: (ce[g], 0, 0))],
        out_specs=pl.BlockSpec(memory_space=pl.ANY),
        scratch_shapes=[pltpu.VMEM((N_SLOTS, CR, XW), I32), pltpu.VMEM((N_SLOTS, CR, half), I32),
                        pltpu.VMEM((SEG_ALIGN, half), I32),
                        pltpu.VMEM((D_MODEL, 2 * EXPERT_FF), BF16), pltpu.VMEM((EXPERT_FF, D_MODEL), BF16),
                        pltpu.SemaphoreType.DMA((N_SLOTS,)), pltpu.SemaphoreType.DMA((N_SLOTS,)),
                        pltpu.SemaphoreType.DMA(())],
    )
    return pl.pallas_call(
        _expert_kernel,
        grid_spec=grid_spec,
        out_shape=jax.ShapeDtypeStruct((n_rows + TS, half), I32),
        compiler_params=_cparams("arbitrary"),
        name="moe_grouped_swiglu",
    )(ce, pa, nch, tot, xs, w_exp_gu, w_exp_down)


def _combine_kernel(tot_ref, y_ref, pos_ref, cnt_ref, off_ref, x2_ref, wsg_ref, wsd_ref, g_ref, b_ref, out_ref,
                    acc_lo, acc_hi):
    tm = x2_ref.shape[0]
    E = N_EXPERTS
    n_sorted = tot_ref[pl.program_id(0)]
    x2 = x2_ref[...]
    h = _dot(x2.astype(BF16), wsg_ref[...])
    gate, up = h[:, :SHARED_FF], h[:, SHARED_FF:]
    shared = _dot((gate * jax.nn.sigmoid(gate) * up).astype(BF16), wsd_ref[...])
    pq_t = pos_ref[...].T.astype(BF16)
    off_col = off_ref[0][:, 0:1].astype(F32)
    cnt_col = cnt_ref[0][:, 0:1].astype(F32)
    end_col = off_col + jnp.floor((cnt_col + (SEG_ALIGN - 1.0)) * (1.0 / SEG_ALIGN)) * SEG_ALIGN
    off_col2 = jnp.concatenate([off_col, off_col], axis=0)
    end_col2 = jnp.concatenate([end_col, end_col], axis=0)
    acc_lo[...] = jnp.zeros_like(acc_lo)
    acc_hi[...] = jnp.zeros_like(acc_hi)

    def gather_back(c, carry):
        r0 = pl.multiple_of(c * OH_ROWS, OH_ROWS)
        jc = (lax.broadcasted_iota(I32, (2 * E, OH_ROWS), 1) + r0).astype(F32)
        member2 = jnp.where(jc >= off_col2, jnp.where(jc < end_col2, 1.0, 0.0), 0.0)
        target = _dot(pq_t, member2.astype(BF16))
        j = (lax.broadcasted_iota(I32, (tm, OH_ROWS), 1) + r0).astype(F32)
        ohb = jnp.where(target == j, 1.0, 0.0).astype(BF16)
        yw = y_ref[0, pl.ds(r0, OH_ROWS), :]
        row = lax.broadcasted_iota(I32, yw.shape, 0) + r0
        lo, hi = _unpack_bf16_pair(jnp.where(row < n_sorted, yw, 0))
        acc_lo[...] += _dot(ohb, lo)
        acc_hi[...] += _dot(ohb, hi)
        return carry

    lax.fori_loop(0, SORT_ROWS // OH_ROWS, gather_back, 0)
    ff = shared + jnp.concatenate([acc_lo[...], acc_hi[...]], axis=1)
    out_ref[...] = _layer_norm(DEEPNORM_ALPHA * x2 + ff, g_ref[...], b_ref[...])


def _combine(tot, y, pos, cnt, off, x2, w_sh_gu, w_sh_down, ln_g, ln_b):
    T = x2.shape[0]
    nt = T // TW
    half = D_MODEL // 2
    tile = lambda w: pl.BlockSpec((TW, w), lambda i: (i, 0))
    return pl.pallas_call(
        _combine_kernel,
        grid=(nt,),
        in_specs=[pl.BlockSpec(memory_space=pltpu.SMEM),
                  pl.BlockSpec((1, TS, half), lambda i: (i, 0, 0)),
                  pl.BlockSpec((2 * N_EXPERTS, TW), lambda i: (0, i)),
                  pl.BlockSpec((1, N_EXPERTS, LANES), lambda i: (i, 0, 0)),
                  pl.BlockSpec((1, N_EXPERTS, LANES), lambda i: (i, 0, 0)), tile(D_MODEL),
                  _full((D_MODEL, 2 * SHARED_FF)), _full((SHARED_FF, D_MODEL)),
                  _full((1, D_MODEL)), _full((1, D_MODEL))],
        out_specs=tile(D_MODEL),
        out_shape=jax.ShapeDtypeStruct((T, D_MODEL), F32),
        scratch_shapes=[pltpu.VMEM((TW, half), F32), pltpu.VMEM((TW, half), F32)],
        compiler_params=_cparams("parallel"),
        name="moe_combine_shared_ln3",
    )(tot, y.reshape(nt + 1, TS, half), pos, cnt, off, x2, w_sh_gu.astype(BF16), w_sh_down.astype(BF16),
      ln_g.reshape(1, -1), ln_b.reshape(1, -1))


def _moe(x2, w_router, router_bias, w_exp_gu, w_exp_down, w_sh_gu, w_sh_down, ln_g, ln_b):
    T = x2.shape[0]
    E = N_EXPERTS
    nt = T // TW
    xs, pos, cnt, off = _route_sort(x2, w_router, router_bias)
    n_pieces_max = (T * TOP_K + nt * E * (PR - 1)) // PR
    n_chunks_max = n_pieces_max // PPC + E
    pa, ce, nch = _plan(cnt[:, :, 0].reshape(nt * E), off[:, :, 0].reshape(nt * E), nt,
                        n_chunks_max * PPC + PLAN_UNROLL, n_chunks_max)
    last_al = (cnt[:, E - 1, 0] + SEG_ALIGN - 1) // SEG_ALIGN * SEG_ALIGN
    tot = jnp.concatenate([off[:, E - 1, 0] + last_al, jnp.zeros((1,), I32)])
    y = _experts(ce, pa, nch, tot, xs.reshape(nt * TS, XW), w_exp_gu, w_exp_down)
    return _combine(tot, y, pos, cnt, off, x2, w_sh_gu, w_sh_down, ln_g, ln_b)


def kernel(x, mem, positions, w_in, q_norm_g, kv_norm_g, w_uq, w_ukv, w_mla_o, s5_a_re, s5_a_im, s5_log_dt, s5_b_re, s5_b_im, s5_c_re, s5_c_im, s5_d, w_s5_glu, w_out, ln1_g, ln1_b, mem_ln_g, mem_ln_b, w_xq, w_xkv, w_xo, ln2_g, ln2_b, w_router, router_bias, w_exp_gu, w_exp_down, w_sh_gu, w_sh_down, ln3_g, ln3_b):
    B, S, D = x.shape
    assert D == D_MODEL and S % TQ == 0 and (B * S) % TM == 0 and S % S5_CHUNK == 0
    xc = x.reshape(B * S, D)
    for l in range(w_in.shape[0]):
        tabs = _rope_tables(positions)
        q, k, v, u, gm, gs = _inproj(xc, tabs, w_in[l], q_norm_g[l], kv_norm_g[l], w_uq[l], w_ukv[l])
        o = _attention(q, k, v, B, S)
        ys = _s5_branch(u, B, S, (s5_a_re[l], s5_a_im[l], s5_log_dt[l], s5_b_re[l], s5_b_im[l],
                                  s5_c_re[l], s5_c_im[l], s5_d[l]))
        x1 = _merge(xc, o, ys, gm, gs, w_mla_o[l], w_s5_glu[l], w_out[l], ln1_g[l], ln1_b[l])
        x2 = _cross_attention(x1, mem, B, S, mem_ln_g[l], mem_ln_b[l], w_xq[l], w_xkv[l], w_xo[l],
                                   ln2_g[l], ln2_b[l])
        xc = _moe(x2, w_router[l], router_bias[l], w_exp_gu[l], w_exp_down[l], w_sh_gu[l],
                  w_sh_down[l], ln3_g[l], ln3_b[l])
    return xc.reshape(B, S, D)
```

```python
import functools
import math

import jax
import jax.numpy as jnp
from jax import lax
from jax.experimental import pallas as pl
from jax.experimental.pallas import tpu as pltpu

F32 = jnp.float32
BF16 = jnp.bfloat16
I32 = jnp.int32
U32 = jnp.uint32

D_MODEL = 1024
MLA_HEADS = 8
QK_NOPE = 64
QK_ROPE = 32
V_HEAD = 64
Q_LORA = 256
KV_LORA = 256
ROPE_THETA = 10000.0
S5_GROUP_CH = 16
S5_WIDTH = 512
S5_GROUPS = 32
S5_STATE = 64
XATTN_HEADS = 4
XATTN_HEAD_DIM = 128
N_EXPERTS = 64
TOP_K = 8
N_EXPERT_GROUPS = 8
TOPK_GROUPS = 4
EXPERT_FF = 256
SHARED_FF = 256
ROUTED_SCALE = 2.5
LN_EPS = 1e-5
RMS_EPS = 1e-6
DEPTH = 1
DEEPNORM_ALPHA = (2.0 * DEPTH) ** 0.25

LANES = 128
HEAD_PAD = 128
ROPE_LO = QK_NOPE
ROPE_HALF = QK_ROPE // 2

TM = 512
TQ = 1024
KV_GROUPS = (2, 1)
S5_CHUNK = 16
TW = 512
OH_ROWS = 512
PR = 16
PAD_ROWS = PR
SEG_ALIGN = 8
SORT_ROWS = TOP_K * TW + N_EXPERTS * SEG_ALIGN
TS = SORT_ROWS + PAD_ROWS
XW = D_MODEL // 2 + LANES
CR = 512
PPC = CR // PR
PLAN_UNROLL = 8
GATHER_AHEAD = 2
N_SLOTS = GATHER_AHEAD + 1
POS_RADIX = 64
POS_NONE_Q = 127
assert SORT_ROWS <= POS_RADIX * POS_NONE_Q and SORT_ROWS % OH_ROWS == 0
VMEM_LIMIT = 48 * 1024 * 1024


def _cparams(*sem):
    return pltpu.CompilerParams(dimension_semantics=sem, vmem_limit_bytes=VMEM_LIMIT)


def _dot(a, b):
    return jnp.dot(a, b, preferred_element_type=F32)


def _dot_nt(a, b, precision=None):
    return lax.dot_general(a, b, (((1,), (1,)), ((), ())), preferred_element_type=F32,
                           precision=precision)


def _dot_exact(a, b):
    return jnp.dot(a, b, preferred_element_type=F32, precision=lax.Precision.HIGHEST)


def _layer_norm(h, g, b):
    mu = jnp.mean(h, axis=-1, keepdims=True)
    c = h - mu
    var = jnp.mean(c * c, axis=-1, keepdims=True)
    return c * lax.rsqrt(var + LN_EPS) * g + b


def _rms_norm(h, g):
    return h * lax.rsqrt(jnp.mean(h * h, axis=-1, keepdims=True) + RMS_EPS) * g


def _full(shape):
    n = len(shape)
    return pl.BlockSpec(shape, lambda *_: (0,) * n)


def _rope_angle_kernel(pos_ref, invf_ref, cos_ref, sin_ref):
    ang = pos_ref[...].astype(F32) * invf_ref[...]
    cos_ref[...] = jnp.cos(ang)
    sin_ref[...] = jnp.sin(ang)


def _rope_tables(positions):
    T = positions.size
    rows = T * ROPE_HALF // LANES
    pos_rep = jnp.repeat(positions.reshape(T), ROPE_HALF).reshape(rows, LANES)
    inv_freq = ROPE_THETA ** (-jnp.arange(0, QK_ROPE, 2, dtype=F32) / QK_ROPE)
    invf = jnp.tile(inv_freq, LANES // ROPE_HALF).reshape(1, LANES)
    cos, sin = pl.pallas_call(
        _rope_angle_kernel,
        out_shape=(jax.ShapeDtypeStruct((rows, LANES), F32),) * 2,
        name="rope_angles",
    )(pos_rep, invf)
    cos = cos.reshape(T, ROPE_HALF)
    sin = sin.reshape(T, ROPE_HALF)
    one = jnp.ones((T, ROPE_LO), F32)
    zero_lo = jnp.zeros((T, ROPE_LO), F32)
    zero_h = jnp.zeros((T, ROPE_HALF), F32)
    tail1 = jnp.ones((T, HEAD_PAD - ROPE_LO - QK_ROPE), F32)
    tail0 = jnp.zeros((T, HEAD_PAD - ROPE_LO - QK_ROPE), F32)
    c_tab = jnp.concatenate([one, cos, cos, tail1], axis=1)
    s_up = jnp.concatenate([zero_lo, -sin, zero_h, tail0], axis=1)
    s_dn = jnp.concatenate([zero_lo, zero_h, sin, tail0], axis=1)
    return c_tab, s_up, s_dn


def _rope(x, c_tab, s_up, s_dn):
    return (x * c_tab + pltpu.roll(x, HEAD_PAD - ROPE_HALF, axis=1) * s_up
            + pltpu.roll(x, ROPE_HALF, axis=1) * s_dn)


def _inproj_kernel(x_ref, c_ref, su_ref, sd_ref, wlat_ref, wu_ref, wgm_ref, wgs_ref, qg_ref, kvg_ref,
                   wuq_ref, wuk_ref, wuv_ref, q_ref, k_ref, v_ref, u_ref, gm_ref, gs_ref):
    xb = x_ref[...].astype(BF16)
    lat = _dot(xb, wlat_ref[...])
    qn = _rms_norm(lat[:, :Q_LORA], qg_ref[...]).astype(BF16)
    kvn = _rms_norm(lat[:, Q_LORA:Q_LORA + KV_LORA], kvg_ref[...]).astype(BF16)
    c_tab, s_up, s_dn = c_ref[...], su_ref[...], sd_ref[...]
    k_rope = _rope(lat[:, Q_LORA + KV_LORA:], c_tab, s_up, s_dn)
    q = _dot(qn, wuq_ref[...])
    k = _dot(kvn, wuk_ref[...])
    scale = (QK_NOPE + QK_ROPE) ** -0.5 * math.log2(math.e)
    for h in range(MLA_HEADS):
        sl = slice(h * HEAD_PAD, (h + 1) * HEAD_PAD)
        q_ref[:, sl] = (_rope(q[:, sl], c_tab, s_up, s_dn) * scale).astype(BF16)
        k_ref[:, sl] = (k[:, sl] + k_rope).astype(BF16)
    v = _dot(kvn, wuv_ref[...])
    ones_lane = lax.broadcasted_iota(I32, (1, v.shape[1]), 1) % HEAD_PAD == V_HEAD
    v_ref[...] = jnp.where(ones_lane, 1.0, v).astype(BF16)
    u_ref[...] = _dot(xb, wu_ref[...])
    gm_ref[...] = _dot(xb, wgm_ref[...]).astype(BF16)
    gs_ref[...] = _dot(xb, wgs_ref[...]).astype(BF16)


def _pad_heads(w, head_w, lo_w):
    K = w.shape[0]
    w = w.reshape(K, MLA_HEADS, head_w)[:, :, :lo_w]
    w = jnp.pad(w, ((0, 0), (0, 0), (0, HEAD_PAD - lo_w)))
    return w.reshape(K, MLA_HEADS * HEAD_PAD)


def _inproj(x2d, tabs, w_in, q_norm_g, kv_norm_g, w_uq, w_ukv):
    T = x2d.shape[0]
    o_rope = Q_LORA + KV_LORA
    o_u = o_rope + QK_ROPE
    o_gm = o_u + S5_WIDTH
    o_gs = o_gm + D_MODEL
    w_rope = jnp.pad(w_in[:, o_rope:o_u], ((0, 0), (ROPE_LO, HEAD_PAD - ROPE_LO - QK_ROPE)))
    w_lat = jnp.concatenate([w_in[:, :o_rope], w_rope], axis=1).astype(BF16)
    w_u = w_in[:, o_u:o_gm].astype(BF16)
    w_gm = w_in[:, o_gm:o_gs].astype(BF16)
    w_gs = w_in[:, o_gs:].astype(BF16)
    wuq = _pad_heads(w_uq, QK_NOPE + QK_ROPE, QK_NOPE + QK_ROPE).astype(BF16)
    kv3 = w_ukv.reshape(KV_LORA, MLA_HEADS, QK_NOPE + V_HEAD)
    wuk = _pad_heads(kv3[:, :, :QK_NOPE].reshape(KV_LORA, -1), QK_NOPE, QK_NOPE).astype(BF16)
    wuv = _pad_heads(kv3[:, :, QK_NOPE:].reshape(KV_LORA, -1), V_HEAD, V_HEAD).astype(BF16)
    HP = MLA_HEADS * HEAD_PAD
    tile = lambda w: pl.BlockSpec((TM, w), lambda i: (i, 0))
    return pl.pallas_call(
        _inproj_kernel,
        grid=(T // TM,),
        in_specs=[tile(D_MODEL), tile(HEAD_PAD), tile(HEAD_PAD), tile(HEAD_PAD),
                  _full(w_lat.shape), _full(w_u.shape), _full(w_gm.shape), _full(w_gs.shape),
                  _full((1, Q_LORA)), _full((1, KV_LORA)),
                  _full(wuq.shape), _full(wuk.shape), _full(wuv.shape)],
        out_specs=[tile(HP), tile(HP), tile(HP), tile(S5_WIDTH), tile(D_MODEL), tile(D_MODEL)],
        out_shape=[jax.ShapeDtypeStruct((T, HP), BF16)] * 3
        + [jax.ShapeDtypeStruct((T, S5_WIDTH), F32)]
        + [jax.ShapeDtypeStruct((T, D_MODEL), BF16)] * 2,
        compiler_params=_cparams("parallel"),
        name="inproj_mla_prep",
    )(x2d, *tabs, w_lat, w_u, w_gm, w_gs, q_norm_g.reshape(1, -1), kv_norm_g.reshape(1, -1),
      wuq, wuk, wuv)


def _attn_kernel(q_ref, k_ref, v_ref, o_ref):
    qi = pl.program_id(2)
    q = q_ref[0]

    def step(blk, n_blk, carry, masked):
        m, acc = carry
        tk = n_blk * TQ
        start = pl.multiple_of(blk * TQ, TQ)
        k = k_ref[0, pl.ds(start, tk), :]
        v = v_ref[0, pl.ds(start, tk), :]
        s = _dot_nt(q, k)
        if masked:
            row = lax.broadcasted_iota(I32, (TQ, tk), 0)
            col = lax.broadcasted_iota(I32, (TQ, tk), 1)
            s = jnp.where(col <= row, s, -jnp.inf)
        m_new = jnp.maximum(m, jnp.max(s, axis=-1, keepdims=True))
        p = jnp.exp2(s - m_new)
        acc = jnp.exp2(m - m_new) * acc + _dot(p.astype(BF16), v)
        return m_new, acc

    carry = (jnp.full((TQ, 1), -jnp.inf, F32), jnp.zeros((TQ, HEAD_PAD), F32))
    done = 0
    for n_blk in KV_GROUPS:
        base = done
        n_it = (qi - base) // n_blk
        carry = lax.fori_loop(0, n_it, lambda j, c, base=base, n_blk=n_blk: step(base + j * n_blk, n_blk, c, False),
                              carry)
        done = base + n_it * n_blk
    acc = step(qi, 1, carry, True)[1]
    o_ref[0] = (acc / acc[:, V_HEAD:V_HEAD + 1]).astype(BF16)


def _attention(q, k, v, B, S):
    HP = MLA_HEADS * HEAD_PAD
    q, k, v = (a.reshape(B, S, HP) for a in (q, k, v))
    o = pl.pallas_call(
        _attn_kernel,
        grid=(B, MLA_HEADS, S // TQ),
        in_specs=[pl.BlockSpec((1, TQ, HEAD_PAD), lambda b, h, i: (b, i, h)),
                  pl.BlockSpec((1, S, HEAD_PAD), lambda b, h, i: (b, 0, h)),
                  pl.BlockSpec((1, S, HEAD_PAD), lambda b, h, i: (b, 0, h))],
        out_specs=pl.BlockSpec((1, TQ, HEAD_PAD), lambda b, h, i: (b, i, h)),
        out_shape=jax.ShapeDtypeStruct((B, S, HP), BF16),
        compiler_params=_cparams("parallel", "parallel", "arbitrary"),
        name="mla_flash_attention",
    )(q, k, v)
    return o.reshape(B * S, HP)


CH2 = S5_CHUNK * S5_GROUP_CH
P2 = 2 * S5_STATE
SCAN_LEVELS_MAX = 16


def _s5_prep_kernel(n_levels, arow_ref, acol_ref, bt_ref, ct_ref, d_ref,
                    mt_ref, wt_ref, vt_ref, apow_ref, dvec_ref, mt_acc):
    P, H, L = S5_STATE, S5_GROUP_CH, S5_CHUNK

    def powers(ar, ai, dt, n):
        e = jnp.exp(n * (ar * dt))
        ang = n * (ai * dt)
        return e * jnp.cos(ang), e * jnp.sin(ang)

    def zoh_coef(ar, ai, dt):
        pr, pi = powers(ar, ai, dt, 1.0)
        nr, ni = pr - 1.0, pi
        den = ar * ar + ai * ai
        return (nr * ar + ni * ai) / den, (ni * ar - nr * ai) / den

    row = arow_ref[0]
    ar, ai, dt = row[0:1], row[1:2], jnp.exp(row[2:3])
    cr, ci = zoh_coef(ar, ai, dt)
    bt = bt_ref[0]
    btr = jnp.concatenate([bt[:H], bt[:H]], axis=1)
    bti = jnp.concatenate([bt[H:], bt[H:]], axis=1)
    bbr = cr * btr - ci * bti
    bbi = cr * bti + ci * btr
    lane = lax.broadcasted_iota(I32, (1, P2), 1)
    is_re = lane < P
    ridx = lax.broadcasted_iota(I32, (CH2, P2), 0) // H
    pwr, pwi = powers(ar, ai, dt, (L - 1 - ridx).astype(F32))
    rsel = (lax.broadcasted_iota(I32, (CH2, H), 0) % H == lax.broadcasted_iota(I32, (CH2, H), 1)).astype(F32)
    bbr_t = _dot_exact(rsel, bbr)
    bbi_t = _dot_exact(rsel, bbi)
    w_re = pwr * bbr_t - pwi * bbi_t
    w_im = pwr * bbi_t + pwi * bbr_t
    wt_ref[0] = jnp.where(is_re, w_re, w_im).astype(BF16)
    a_r, a_i = powers(ar, ai, dt, float(L))
    sign = jnp.where(is_re, -1.0, 1.0)
    for lvl in range(SCAN_LEVELS_MAX):
        if lvl < n_levels:
            apow_ref[0, lvl:lvl + 1, :] = a_r
            apow_ref[0, SCAN_LEVELS_MAX + lvl:SCAN_LEVELS_MAX + lvl + 1, :] = sign * a_i
            a_r, a_i = a_r * a_r - a_i * a_i, 2.0 * a_r * a_i
        else:
            apow_ref[0, lvl:lvl + 1, :] = jnp.zeros_like(a_r)
            apow_ref[0, SCAN_LEVELS_MAX + lvl:SCAN_LEVELS_MAX + lvl + 1, :] = jnp.zeros_like(a_r)

    col = acol_ref[0]
    arc, aic, dtc = col[:, 0:1], col[:, 1:2], jnp.exp(col[:, 2:3])
    ct = ct_ref[0]
    csel = (lax.broadcasted_iota(I32, (H, CH2), 1) % H == lax.broadcasted_iota(I32, (H, CH2), 0)).astype(F32)
    ctr = _dot_exact(ct[:, :H], csel)
    cti = _dot_exact(ct[:, H:], csel)
    sidx = (lax.broadcasted_iota(I32, (P, CH2), 1) // H).astype(F32)
    pr, pi = powers(arc, aic, dtc, sidx)
    g_r = ctr * pr - cti * pi
    g_i = ctr * pi + cti * pr
    a1r, a1i = powers(arc, aic, dtc, 1.0)
    v_r = g_r * a1r - g_i * a1i
    v_i = g_r * a1i + g_i * a1r
    vt_ref[0, :P, :] = v_r.astype(BF16)
    vt_ref[0, P:, :] = (-v_i).astype(BF16)
    bbr64, bbi64 = bbr[:, :P], bbi[:, :P]
    kt = _dot_exact(bbr64, g_r) - _dot_exact(bbi64, g_i)
    lane2 = lax.broadcasted_iota(I32, (H, CH2), 1)
    for r in range(L):
        blk = kt if r == 0 else jnp.where(lane2 >= r * H, pltpu.roll(kt, r * H, axis=1), 0.0)
        mt_acc[r * H:(r + 1) * H, :] = blk
    mt_ref[0] = mt_acc[...].astype(BF16)
    dvec_ref[0] = _dot_exact(d_ref[0], csel)


def _s5_prep(n_levels, s5_a_re, s5_a_im, s5_log_dt, s5_b_re, s5_b_im, s5_c_re, s5_c_im, s5_d):
    G, P, H = S5_GROUPS, S5_STATE, S5_GROUP_CH
    ldt = jnp.broadcast_to(s5_log_dt[:, None], (G, P))
    arow = jnp.stack([s5_a_re, s5_a_im, ldt], axis=1)
    arow = jnp.concatenate([arow, arow], axis=2)
    acol = jnp.stack([s5_a_re, s5_a_im, ldt], axis=2)
    bt = jnp.concatenate([s5_b_re.transpose(0, 2, 1), s5_b_im.transpose(0, 2, 1)], axis=1)
    ct = jnp.concatenate([s5_c_re.transpose(0, 2, 1), s5_c_im.transpose(0, 2, 1)], axis=2)
    d = s5_d.reshape(G, 1, H)
    blk = lambda *s: pl.BlockSpec((1,) + s, lambda g: (g, 0, 0))
    return pl.pallas_call(
        functools.partial(_s5_prep_kernel, n_levels),
        grid=(G,),
        in_specs=[blk(3, P2), blk(P, 3), blk(2 * H, P), blk(P, 2 * H), blk(1, H)],
        out_specs=[blk(CH2, CH2), blk(CH2, P2), blk(P2, CH2), blk(2 * SCAN_LEVELS_MAX, P2), blk(1, CH2)],
        out_shape=[jax.ShapeDtypeStruct((G, CH2, CH2), BF16),
                   jax.ShapeDtypeStruct((G, CH2, P2), BF16),
                   jax.ShapeDtypeStruct((G, P2, CH2), BF16),
                   jax.ShapeDtypeStruct((G, 2 * SCAN_LEVELS_MAX, P2), F32),
                   jax.ShapeDtypeStruct((G, 1, CH2), F32)],
        scratch_shapes=[pltpu.VMEM((CH2, CH2), F32)],
        compiler_params=_cparams("parallel"),
        name="s5_discretise",
    )(arow, acol, bt, ct, d)


def _gelu_tanh(y):
    return 0.5 * y * (1.0 + jnp.tanh(math.sqrt(2.0 / math.pi) * (y + 0.044715 * (y * y * y))))


GROUPS_PER_TILE = LANES // S5_GROUP_CH
STEPS_PER_TILE = LANES // S5_GROUP_CH


def _s5_scan_kernel(n_levels, u_ref, mt_ref, wt_ref, vt_ref, apow_ref, dvec_ref, y_ref):
    H, L = S5_GROUP_CH, S5_CHUNK
    nc = u_ref.shape[0] // L
    lane_blk = lax.broadcasted_iota(I32, (nc, LANES), 1) // H
    ridx = lax.broadcasted_iota(I32, (nc, P2), 0)
    steps = [u_ref[pl.ds(l, nc, stride=L), :] for l in range(L)]
    groups = range(GROUPS_PER_TILE)
    us, xs = [], []
    for gi in groups:
        halves = []
        for hh in range(L // STEPS_PER_TILE):
            acc = jnp.zeros((nc, LANES), F32)
            for l8 in range(STEPS_PER_TILE):
                src = steps[hh * STEPS_PER_TILE + l8]
                shift = ((l8 - gi) % STEPS_PER_TILE) * H
                moved = src if shift == 0 else pltpu.roll(src, shift, axis=1)
                acc = jnp.where(lane_blk == l8, moved, acc)
            halves.append(acc)
        u = jnp.concatenate(halves, axis=1)
        us.append(u)
        xs.append(_dot(u.astype(BF16), wt_ref[gi]))
    for lvl in range(n_levels):
        sh = 1 << lvl
        for gi in groups:
            apow = apow_ref[gi]
            prev = jnp.where(ridx >= sh, pltpu.roll(xs[gi], sh, axis=0), 0.0)
            a_r = apow[lvl:lvl + 1]
            a_i = apow[SCAN_LEVELS_MAX + lvl:SCAN_LEVELS_MAX + lvl + 1]
            xs[gi] = xs[gi] + a_r * prev + a_i * pltpu.roll(prev, S5_STATE, axis=1)
    outs = [jnp.zeros((nc, LANES), F32) for _ in range(L)]
    for gi in groups:
        x_in = jnp.where(ridx >= 1, pltpu.roll(xs[gi], 1, axis=0), 0.0)
        u = us[gi]
        y = _gelu_tanh(_dot(u.astype(BF16), mt_ref[gi]) + _dot(x_in.astype(BF16), vt_ref[gi]) + dvec_ref[gi] * u)
        for l in range(L):
            hh, l8 = divmod(l, STEPS_PER_TILE)
            src = y[:, hh * LANES:(hh + 1) * LANES]
            shift = ((gi - l8) % STEPS_PER_TILE) * H
            moved = src if shift == 0 else pltpu.roll(src, shift, axis=1)
            outs[l] = jnp.where(lane_blk == gi, moved, outs[l])
    for l in range(L):
        y_ref[pl.ds(l, nc, stride=L), :] = outs[l]


def _s5_branch(u, B, S, s5_params):
    G, L = S5_GROUPS, S5_CHUNK
    nc = S // L
    n_levels = max(1, (nc - 1).bit_length())
    assert n_levels <= SCAN_LEVELS_MAX and S5_CHUNK % STEPS_PER_TILE == 0
    mt, wt, vt, apow, dvec = _s5_prep(n_levels, *s5_params)
    n_lt = G // GROUPS_PER_TILE
    blk = lambda *s: pl.BlockSpec((GROUPS_PER_TILE,) + s, lambda b, t: (t, 0, 0))
    io = pl.BlockSpec((S, LANES), lambda b, t: (b, t))
    return pl.pallas_call(
        functools.partial(_s5_scan_kernel, n_levels),
        grid=(B, n_lt),
        in_specs=[io, blk(CH2, CH2), blk(CH2, P2), blk(P2, CH2), blk(2 * SCAN_LEVELS_MAX, P2), blk(1, CH2)],
        out_specs=io,
        out_shape=jax.ShapeDtypeStruct((B * S, S5_WIDTH), F32),
        compiler_params=_cparams("parallel", "parallel"),
        name="s5_chunk_scan",
    )(u, mt, wt, vt, apow, dvec)


def _merge_kernel(x_ref, o_ref, ys_ref, gm_ref, gs_ref, wmo_ref, wglu_ref, wout_ref, g_ref, b_ref, x1_ref):
    y_mla = _dot(o_ref[...], wmo_ref[...])
    z = _dot(ys_ref[...].astype(BF16), wglu_ref[...])
    y_s5 = z[:, :D_MODEL] * jax.nn.sigmoid(z[:, D_MODEL:])
    merged = (jax.nn.sigmoid(gm_ref[...].astype(F32)) * y_mla
              + jax.nn.sigmoid(gs_ref[...].astype(F32)) * y_s5)
    mix = _dot(merged.astype(BF16), wout_ref[...])
    x1_ref[...] = _layer_norm(DEEPNORM_ALPHA * x_ref[...] + mix, g_ref[...], b_ref[...])


def _merge(x2d, o, ys, gm, gs, w_mla_o, w_s5_glu, w_out, ln_g, ln_b):
    T = x2d.shape[0]
    HP = MLA_HEADS * HEAD_PAD
    wmo = jnp.pad(w_mla_o.reshape(MLA_HEADS, V_HEAD, D_MODEL), ((0, 0), (0, HEAD_PAD - V_HEAD), (0, 0)))
    wmo = wmo.reshape(HP, D_MODEL).astype(BF16)
    tile = lambda w: pl.BlockSpec((TM, w), lambda i: (i, 0))
    return pl.pallas_call(
        _merge_kernel,
        grid=(T // TM,),
        in_specs=[tile(D_MODEL), tile(HP), tile(S5_WIDTH), tile(D_MODEL), tile(D_MODEL),
                  _full((HP, D_MODEL)), _full((S5_WIDTH, 2 * D_MODEL)), _full((D_MODEL, D_MODEL)),
                  _full((1, D_MODEL)), _full((1, D_MODEL))],
        out_specs=tile(D_MODEL),
        out_shape=jax.ShapeDtypeStruct((T, D_MODEL), F32),
        compiler_params=_cparams("parallel"),
        name="merge_outproj_ln1",
    )(x2d, o, ys, gm, gs, wmo, w_s5_glu.astype(BF16), w_out.astype(BF16),
      ln_g.reshape(1, -1), ln_b.reshape(1, -1))


def _memkv_kernel(mem_ref, g_ref, b_ref, w_ref, k_ref, v_ref):
    m = _layer_norm(mem_ref[0], g_ref[...], b_ref[...]).astype(BF16)
    kv = _dot(m, w_ref[...])
    hd = XATTN_HEADS * XATTN_HEAD_DIM
    k_ref[0] = kv[:, :hd].astype(BF16)
    v_ref[0] = kv[:, hd:].astype(BF16)


def _xattn_kernel(x1_ref, k_ref, v_ref, wq_ref, wo_ref, g_ref, b_ref, x2_ref):
    x1 = x1_ref[0]
    q = (_dot(x1.astype(BF16), wq_ref[...]) * (XATTN_HEAD_DIM ** -0.5)).astype(BF16)
    k = k_ref[0]
    v = v_ref[0]
    outs = []
    for h in range(XATTN_HEADS):
        sl = slice(h * XATTN_HEAD_DIM, (h + 1) * XATTN_HEAD_DIM)
        s = _dot_nt(q[:, sl], k[:, sl])
        p = jnp.exp(s - jnp.max(s, axis=-1, keepdims=True))
        o = _dot(p.astype(BF16), v[:, sl]) / jnp.sum(p, axis=-1, keepdims=True)
        outs.append(o.astype(BF16))
    xa = _dot(jnp.concatenate(outs, axis=1), wo_ref[...])
    x2 = _layer_norm(DEEPNORM_ALPHA * x1 + xa, g_ref[...], b_ref[...])
    x2_ref[0] = x2


def _cross_attention(x1, mem, B, S, mem_ln_g, mem_ln_b, w_xq, w_xkv, w_xo, ln_g, ln_b):
    M = mem.shape[1]
    hd = XATTN_HEADS * XATTN_HEAD_DIM
    row = lambda a: a.reshape(1, -1)
    k, v = pl.pallas_call(
        _memkv_kernel,
        grid=(B,),
        in_specs=[pl.BlockSpec((1, M, D_MODEL), lambda b: (b, 0, 0)), _full((1, D_MODEL)), _full((1, D_MODEL)),
                  _full((D_MODEL, 2 * hd))],
        out_specs=[pl.BlockSpec((1, M, hd), lambda b: (b, 0, 0))] * 2,
        out_shape=[jax.ShapeDtypeStruct((B, M, hd), BF16)] * 2,
        compiler_params=_cparams("parallel"),
        name="memory_kv",
    )(mem, row(mem_ln_g), row(mem_ln_b), w_xkv.astype(BF16))
    tile = lambda w: pl.BlockSpec((1, TM, w), lambda b, i: (b, i, 0))
    x2 = pl.pallas_call(
        _xattn_kernel,
        grid=(B, S // TM),
        in_specs=[tile(D_MODEL),
                  pl.BlockSpec((1, M, hd), lambda b, i: (b, 0, 0)), pl.BlockSpec((1, M, hd), lambda b, i: (b, 0, 0)),
                  _full((D_MODEL, hd)), _full((hd, D_MODEL)), _full((1, D_MODEL)), _full((1, D_MODEL))],
        out_specs=tile(D_MODEL),
        out_shape=jax.ShapeDtypeStruct((B, S, D_MODEL), F32),
        compiler_params=_cparams("parallel", "parallel"),
        name="cross_attention_ln2",
    )(x1.reshape(B, S, D_MODEL), k, v, w_xq.astype(BF16), w_xo.astype(BF16), row(ln_g), row(ln_b))
    return x2.reshape(B * S, D_MODEL)


def _route_sort_kernel(x_ref, w_ref, bias_ref, xs_ref, pos_ref, cnt_ref, off_ref):
    E, per = N_EXPERTS, N_EXPERTS // N_EXPERT_GROUPS
    tm = x_ref.shape[0]
    x = x_ref[...]
    logits = _dot_nt(w_ref[...], x, precision=lax.Precision.HIGHEST)
    scores = jax.nn.sigmoid(logits)
    sel = scores + bias_ref[...]
    neg = -jnp.inf
    i8 = lax.broadcasted_iota(I32, (per, tm), 0)
    gscore = []
    for g in range(N_EXPERT_GROUPS):
        blk = sel[g * per:(g + 1) * per]
        m1 = jnp.max(blk, axis=0, keepdims=True)
        i1 = jnp.min(jnp.where(blk == m1, i8, per), axis=0, keepdims=True)
        m2 = jnp.max(jnp.where(i8 == i1, neg, blk), axis=0, keepdims=True)
        gscore.append(m1 + m2)
    blocks = []
    for g in range(N_EXPERT_GROUPS):
        ahead = jnp.zeros((1, tm), I32)
        for o in range(N_EXPERT_GROUPS):
            if o == g:
                continue
            before = (gscore[o] >= gscore[g]) if o < g else (gscore[o] > gscore[g])
            ahead = ahead + before.astype(I32)
        blocks.append(jnp.where(ahead < TOPK_GROUPS, sel[g * per:(g + 1) * per], neg))
    cur = jnp.concatenate(blocks, axis=0)
    ie = lax.broadcasted_iota(I32, (E, tm), 0)
    picked = jnp.zeros((E, tm), F32)
    idxs = []
    for _ in range(TOP_K):
        m = jnp.max(cur, axis=0, keepdims=True)
        idx = jnp.min(jnp.where(cur == m, ie, E), axis=0, keepdims=True)
        hit = ie == idx
        picked = jnp.where(hit, 1.0, picked)
        cur = jnp.where(hit, neg, cur)
        idxs.append(idx)
    wsel = scores * picked
    wnorm = wsel / jnp.sum(wsel, axis=0, keepdims=True) * ROUTED_SCALE
    pb = picked.astype(BF16)
    tri = (lax.broadcasted_iota(I32, (tm, tm), 0) <= lax.broadcasted_iota(I32, (tm, tm), 1)).astype(BF16)
    incl = _dot(pb, tri)
    cnt_col = jnp.sum(picked, axis=1, keepdims=True)
    lower = (lax.broadcasted_iota(I32, (E, E), 1) < lax.broadcasted_iota(I32, (E, E), 0)).astype(F32)
    al_col = jnp.floor((cnt_col + (SEG_ALIGN - 1.0)) * (1.0 / SEG_ALIGN)) * SEG_ALIGN
    off_col = _dot_exact(lower, jnp.broadcast_to(al_col, (E, LANES)))[:, 0:1]
    pos = off_col + incl - 1.0
    q = jnp.floor(pos * (1.0 / POS_RADIX))
    pq = jnp.concatenate([jnp.where(picked > 0.0, POS_RADIX * q, POS_RADIX * POS_NONE_Q),
                          jnp.where(picked > 0.0, pos - POS_RADIX * q, 0.0)], axis=0)
    pos_ref[...] = pq
    pqb = pq.astype(BF16)
    cnt_ref[0] = jnp.broadcast_to(cnt_col, (E, LANES)).astype(I32)
    off_ref[0] = jnp.broadcast_to(off_col, (E, LANES)).astype(I32)
    cnt_row = _dot_nt(jnp.ones((8, tm), BF16), pb)
    al_row = jnp.floor((cnt_row + (SEG_ALIGN - 1.0)) * (1.0 / SEG_ALIGN)) * SEG_ALIGN
    er = lax.broadcasted_iota(I32, (E, 4 * E), 0)
    ec = lax.broadcasted_iota(I32, (E, 4 * E), 1) % E
    off_row4 = _dot_exact(al_row, (er < ec).astype(F32))[0:1]
    end_row4 = off_row4 + _dot_exact(al_row, (er == ec).astype(F32))[0:1]
    w_hi = wnorm.astype(BF16).astype(F32)
    w_r1 = wnorm - w_hi
    w_mid = w_r1.astype(BF16).astype(F32)
    w_lo = w_r1 - w_mid
    w4_t = jnp.concatenate([w_hi, w_mid, w_lo, jnp.zeros_like(w_lo)], axis=0).T
    xw = jnp.concatenate([x.astype(BF16), w4_t.astype(BF16)], axis=1)
    half = D_MODEL // 2

    def build(c, carry):
        r0 = pl.multiple_of(c * OH_ROWS, OH_ROWS)
        jr = (lax.broadcasted_iota(I32, (OH_ROWS, 4 * E), 0) + r0).astype(F32)
        member4 = jnp.where(jr >= off_row4, jnp.where(jr < end_row4, 1.0, 0.0), 0.0)
        target = _dot(member4[:, :2 * E].astype(BF16), pqb)
        j = (lax.broadcasted_iota(I32, (OH_ROWS, tm), 0) + r0).astype(F32)
        ohb = jnp.where(target == j, 1.0, 0.0).astype(BF16)
        rows = _dot(ohb, xw)
        xs_ref[0, pl.ds(r0, OH_ROWS), 0:half] = _pack_bf16_pair(rows[:, :half], rows[:, half:D_MODEL])
        wrow = jnp.sum(member4 * rows[:, D_MODEL:], axis=1, keepdims=True)
        xs_ref[0, pl.ds(r0, OH_ROWS), half:] = pltpu.bitcast(jnp.broadcast_to(wrow, (OH_ROWS, LANES)), I32)
        return carry

    lax.fori_loop(0, SORT_ROWS // OH_ROWS, build, 0)
    xs_ref[0, SORT_ROWS:, :] = jnp.zeros((PAD_ROWS, XW), I32)


def _route_sort(x2, w_router, router_bias):
    T = x2.shape[0]
    E = N_EXPERTS
    nt = T // TW
    seg = pl.BlockSpec((1, E, LANES), lambda i: (i, 0, 0))
    return pl.pallas_call(
        _route_sort_kernel,
        grid=(nt,),
        in_specs=[pl.BlockSpec((TW, D_MODEL), lambda i: (i, 0)), _full((E, D_MODEL)), _full((E, 1))],
        out_specs=[pl.BlockSpec((1, TS, XW), lambda i: (i, 0, 0)),
                   pl.BlockSpec((2 * E, TW), lambda i: (0, i)), seg, seg],
        out_shape=[jax.ShapeDtypeStruct((nt, TS, XW), I32), jax.ShapeDtypeStruct((2 * E, T), F32),
                   jax.ShapeDtypeStruct((nt, E, LANES), I32), jax.ShapeDtypeStruct((nt, E, LANES), I32)],
        compiler_params=_cparams("parallel"),
        name="route_local_sort",
    )(x2, w_router.T, router_bias.reshape(E, 1))


def _plan_kernel(nt, n_pieces_max, n_chunks_max, cnt_ref, off_ref, pa_ref, ce_ref, nch_ref):
    def unused(p, z):
        pa_ref[p] = -SEG_ALIGN
        return z

    def per_expert(e, carry):
        p0, g = carry

        def per_tile(i, p):
            n = cnt_ref[i * N_EXPERTS + e]
            base = i * TS + off_ref[i * N_EXPERTS + e]
            n_pc = (n + PR - 1) // PR

            def put(q, c):
                pa_ref[p + q] = base + q * PR
                return c

            for q in range(PLAN_UNROLL):
                pa_ref[p + q] = base + q * PR
            lax.fori_loop(PLAN_UNROLL, n_pc, put, 0)
            return p + n_pc

        p1 = lax.fori_loop(0, nt, per_tile, p0)
        n_ch = (p1 - p0 + PPC - 1) // PPC

        lax.fori_loop(p1, p0 + n_ch * PPC, unused, 0)

        def mark(c, z):
            ce_ref[g + c] = e
            return z

        lax.fori_loop(0, n_ch, mark, 0)
        return p0 + n_ch * PPC, g + n_ch

    p_end, g = lax.fori_loop(0, N_EXPERTS, per_expert, (jnp.int32(0), jnp.int32(0)))
    lax.fori_loop(p_end, n_pieces_max, unused, 0)
    nch_ref[0] = g

    def tail(c, z):
        ce_ref[c] = N_EXPERTS - 1
        return z

    lax.fori_loop(g, n_chunks_max, tail, 0)


def _plan(cnt, off, nt, n_pieces_max, n_chunks_max):
    smem = pl.BlockSpec(memory_space=pltpu.SMEM)
    return pl.pallas_call(
        functools.partial(_plan_kernel, nt, n_pieces_max, n_chunks_max),
        in_specs=[smem, smem],
        out_specs=[smem, smem, smem],
        out_shape=[jax.ShapeDtypeStruct((n_pieces_max,), I32), jax.ShapeDtypeStruct((n_chunks_max,), I32),
                   jax.ShapeDtypeStruct((1,), I32)],
        name="moe_plan",
    )(cnt, off)


def _unpack_bf16_pair(w):
    lo = pltpu.bitcast(lax.shift_left(w, jnp.int32(16)), F32).astype(BF16)
    hi = pltpu.bitcast(w & jnp.int32(-65536), F32).astype(BF16)
    return lo, hi


def _pack_bf16_pair(lo, hi):
    lo_bits = lax.shift_right_logical(pltpu.bitcast(lo.astype(BF16).astype(F32), I32), jnp.int32(16))
    hi_bits = pltpu.bitcast(hi.astype(BF16).astype(F32), I32)
    return hi_bits | lo_bits


def _expert_kernel(ce_ref, pa_ref, nch_ref, tot_ref, xs_hbm, wgu_ref, wdn_ref, y_hbm,
                   xbuf, ybuf, zbuf, wgu_b, wdn_b, gsem, wsem, zsem):
    g = pl.program_id(0)
    n = nch_ref[0]
    slot = g % N_SLOTS
    half = D_MODEL // 2

    dummy_base = (tot_ref.shape[0] - 1) * TS

    def gather_copy(s, a, jp):
        return pltpu.make_async_copy(xs_hbm.at[pl.ds(pl.multiple_of(a, SEG_ALIGN), PR), :],
                                     xbuf.at[s, pl.ds(jp * PR, PR), :], gsem.at[s])

    def write_copy(s, a, jp):
        return pltpu.make_async_copy(ybuf.at[s, pl.ds(jp * PR, PR), :],
                                     y_hbm.at[pl.ds(pl.multiple_of(a, SEG_ALIGN), PR), :], wsem.at[s])

    def gather_start(c, s):
        for jp in range(PPC):
            gather_copy(s, jnp.maximum(pa_ref[c * PPC + jp], 0), jp).start()

    def write_start(c, s):
        for jp in range(PPC):
            a = pa_ref[c * PPC + jp]
            write_copy(s, jnp.where(a >= 0, a, dummy_base + jp * PR), jp).start()

    def gather_wait(s):
        for jp in range(PPC):
            gather_copy(s, 0, jp).wait()

    def write_wait(s):
        for jp in range(PPC):
            write_copy(s, 0, jp).wait()

    @pl.when(g == 0)
    def _():
        gather_start(0, 0)
        ybuf[...] = jnp.zeros_like(ybuf)
        zbuf[...] = jnp.zeros_like(zbuf)

        def zero_copy(row):
            return pltpu.make_async_copy(zbuf, y_hbm.at[pl.ds(pl.multiple_of(row, SEG_ALIGN), SEG_ALIGN), :], zsem)

        def tail_of(i):
            return i * TS + tot_ref[i], (TS - tot_ref[i]) // SEG_ALIGN

        def zero_start(i, c):
            base, cnt8 = tail_of(i)
            lax.fori_loop(0, cnt8, lambda q, z: (zero_copy(base + q * SEG_ALIGN).start(), z)[1], 0)
            return c

        def zero_wait(i, c):
            base, cnt8 = tail_of(i)
            lax.fori_loop(0, cnt8, lambda q, z: (zero_copy(base + q * SEG_ALIGN).wait(), z)[1], 0)
            return c

        lax.fori_loop(0, tot_ref.shape[0], zero_start, 0)
        lax.fori_loop(0, tot_ref.shape[0], zero_wait, 0)

    @pl.when((g == 0) & (n > 1))
    def _():
        gather_start(1, 1)

    @pl.when(g + GATHER_AHEAD < n)
    def _():
        gather_start(g + GATHER_AHEAD, (g + GATHER_AHEAD) % N_SLOTS)

    e_cur = ce_ref[g]
    e_prev = ce_ref[jnp.maximum(g - 1, 0)]
    e_prev2 = ce_ref[jnp.maximum(g - 2, 0)]

    @pl.when(g < n)
    def _():
        @pl.when((g == 0) | (e_cur != e_prev))
        def _():
            wgu_b[...] = wgu_ref[0].astype(BF16)
            wdn_b[...] = wdn_ref[0].astype(BF16)

        gather_wait(slot)

        @pl.when((g >= 2) & (e_prev == e_prev2))
        def _():
            write_wait((g - 2) % N_SLOTS)

        @pl.when((g >= 1) & (e_cur != e_prev))
        def _():
            write_wait((g - 1) % N_SLOTS)

        def swiglu_rows(rows):
            xw = xbuf[slot, :rows, :]
            lo, hi = _unpack_bf16_pair(xw[:, :half])
            h = _dot(jnp.concatenate([lo, hi], axis=1), wgu_b[...])
            gate, up = h[:, :EXPERT_FF], h[:, EXPERT_FF:]
            act = (gate * jax.nn.sigmoid(gate) * up).astype(BF16)
            y = _dot(act, wdn_b[...])
            w_row = pltpu.bitcast(xw[:, half:], F32)
            y = y * jnp.concatenate([w_row] * (D_MODEL // LANES), axis=1)
            ybuf[slot, :rows, :] = _pack_bf16_pair(y[:, :half], y[:, half:])

        second_half_used = pa_ref[g * PPC + PPC // 2] >= 0

        @pl.when(second_half_used)
        def _():
            swiglu_rows(CR)

        @pl.when(jnp.logical_not(second_half_used))
        def _():
            swiglu_rows(CR // 2)

        write_start(g, slot)

    @pl.when(g == pl.num_programs(0) - 1)
    def _():
        @pl.when((n >= 2) & (ce_ref[jnp.maximum(n - 1, 0)] == ce_ref[jnp.maximum(n - 2, 0)]))
        def _():
            write_wait((n - 2) % N_SLOTS)

        write_wait((n - 1) % N_SLOTS)


def _experts(ce, pa, nch, tot, xs, w_exp_gu, w_exp_down):
    n_rows = xs.shape[0]
    half = D_MODEL // 2
    grid_spec = pltpu.PrefetchScalarGridSpec(
        num_scalar_prefetch=4,
        grid=(ce.shape[0],),
        in_specs=[pl.BlockSpec(memory_space=pl.ANY),
                  pl.BlockSpec((1, D_MODEL, 2 * EXPERT_FF), lambda g, ce, pa, nch, tot: (ce[g], 0, 0)),
                  pl.BlockSpec((1, EXPERT_FF, D_MODEL), lambda g, ce, pa, nch, tot: (ce[g], 0, 0))],
        out_specs=pl.BlockSpec(memory_space=pl.ANY),
        scratch_shapes=[pltpu.VMEM((N_SLOTS, CR, XW), I32), pltpu.VMEM((N_SLOTS, CR, half), I32),
                        pltpu.VMEM((SEG_ALIGN, half), I32),
                        pltpu.VMEM((D_MODEL, 2 * EXPERT_FF), BF16), pltpu.VMEM((EXPERT_FF, D_MODEL), BF16),
                        pltpu.SemaphoreType.DMA((N_SLOTS,)), pltpu.SemaphoreType.DMA((N_SLOTS,)),
                        pltpu.SemaphoreType.DMA(())],
    )
    return pl.pallas_call(
        _expert_kernel,
        grid_spec=grid_spec,
        out_shape=jax.ShapeDtypeStruct((n_rows + TS, half), I32),
        compiler_params=_cparams("arbitrary"),
        name="moe_grouped_swiglu",
    )(ce, pa, nch, tot, xs, w_exp_gu, w_exp_down)


def _combine_kernel(tot_ref, y_ref, pos_ref, cnt_ref, off_ref, x2_ref, wsg_ref, wsd_ref, g_ref, b_ref, out_ref,
                    acc_lo, acc_hi):
    tm = x2_ref.shape[0]
    E = N_EXPERTS
    n_sorted = tot_ref[pl.program_id(0)]
    x2 = x2_ref[...]
    h = _dot(x2.astype(BF16), wsg_ref[...])
    gate, up = h[:, :SHARED_FF], h[:, SHARED_FF:]
    shared = _dot((gate * jax.nn.sigmoid(gate) * up).astype(BF16), wsd_ref[...])
    pq_t = pos_ref[...].T.astype(BF16)
    off_col = off_ref[0][:, 0:1].astype(F32)
    cnt_col = cnt_ref[0][:, 0:1].astype(F32)
    end_col = off_col + jnp.floor((cnt_col + (SEG_ALIGN - 1.0)) * (1.0 / SEG_ALIGN)) * SEG_ALIGN
    off_col2 = jnp.concatenate([off_col, off_col], axis=0)
    end_col2 = jnp.concatenate([end_col, end_col], axis=0)
    acc_lo[...] = jnp.zeros_like(acc_lo)
    acc_hi[...] = jnp.zeros_like(acc_hi)

    def gather_back(c, carry):
        r0 = pl.multiple_of(c * OH_ROWS, OH_ROWS)
        jc = (lax.broadcasted_iota(I32, (2 * E, OH_ROWS), 1) + r0).astype(F32)
        member2 = jnp.where(jc >= off_col2, jnp.where(jc < end_col2, 1.0, 0.0), 0.0)
        target = _dot(pq_t, member2.astype(BF16))
        j = (lax.broadcasted_iota(I32, (tm, OH_ROWS), 1) + r0).astype(F32)
        ohb = jnp.where(target == j, 1.0, 0.0).astype(BF16)
        yw = y_ref[0, pl.ds(r0, OH_ROWS), :]
        row = lax.broadcasted_iota(I32, yw.shape, 0) + r0
        lo, hi = _unpack_bf16_pair(jnp.where(row < n_sorted, yw, 0))
        acc_lo[...] += _dot(ohb, lo)
        acc_hi[...] += _dot(ohb, hi)
        return carry

    lax.fori_loop(0, SORT_ROWS // OH_ROWS, gather_back, 0)
    ff = shared + jnp.concatenate([acc_lo[...], acc_hi[...]], axis=1)
    out_ref[...] = _layer_norm(DEEPNORM_ALPHA * x2 + ff, g_ref[...], b_ref[...])


def _combine(tot, y, pos, cnt, off, x2, w_sh_gu, w_sh_down, ln_g, ln_b):
    T = x2.shape[0]
    nt = T // TW
    half = D_MODEL // 2
    tile = lambda w: pl.BlockSpec((TW, w), lambda i: (i, 0))
    return pl.pallas_call(
        _combine_kernel,
        grid=(nt,),
        in_specs=[pl.BlockSpec(memory_space=pltpu.SMEM),
                  pl.BlockSpec((1, TS, half), lambda i: (i, 0, 0)),
                  pl.BlockSpec((2 * N_EXPERTS, TW), lambda i: (0, i)),
                  pl.BlockSpec((1, N_EXPERTS, LANES), lambda i: (i, 0, 0)),
                  pl.BlockSpec((1, N_EXPERTS, LANES), lambda i: (i, 0, 0)), tile(D_MODEL),
                  _full((D_MODEL, 2 * SHARED_FF)), _full((SHARED_FF, D_MODEL)),
                  _full((1, D_MODEL)), _full((1, D_MODEL))],
        out_specs=tile(D_MODEL),
        out_shape=jax.ShapeDtypeStruct((T, D_MODEL), F32),
        scratch_shapes=[pltpu.VMEM((TW, half), F32), pltpu.VMEM((TW, half), F32)],
        compiler_params=_cparams("parallel"),
        name="moe_combine_shared_ln3",
    )(tot, y.reshape(nt + 1, TS, half), pos, cnt, off, x2, w_sh_gu.astype(BF16), w_sh_down.astype(BF16),
      ln_g.reshape(1, -1), ln_b.reshape(1, -1))


def _moe(x2, w_router, router_bias, w_exp_gu, w_exp_down, w_sh_gu, w_sh_down, ln_g, ln_b):
    T = x2.shape[0]
    E = N_EXPERTS
    nt = T // TW
    xs, pos, cnt, off = _route_sort(x2, w_router, router_bias)
    n_pieces_max = (T * TOP_K + nt * E * (PR - 1)) // PR
    n_chunks_max = n_pieces_max // PPC + E
    pa, ce, nch = _plan(cnt[:, :, 0].reshape(nt * E), off[:, :, 0].reshape(nt * E), nt,
                        n_chunks_max * PPC + PLAN_UNROLL, n_chunks_max)
    last_al = (cnt[:, E - 1, 0] + SEG_ALIGN - 1) // SEG_ALIGN * SEG_ALIGN
    tot = jnp.concatenate([off[:, E - 1, 0] + last_al, jnp.zeros((1,), I32)])
    y = _experts(ce, pa, nch, tot, xs.reshape(nt * TS, XW), w_exp_gu, w_exp_down)
    return _combine(tot, y, pos, cnt, off, x2, w_sh_gu, w_sh_down, ln_g, ln_b)


def kernel(x, mem, positions, w_in, q_norm_g, kv_norm_g, w_uq, w_ukv, w_mla_o, s5_a_re, s5_a_im, s5_log_dt, s5_b_re, s5_b_im, s5_c_re, s5_c_im, s5_d, w_s5_glu, w_out, ln1_g, ln1_b, mem_ln_g, mem_ln_b, w_xq, w_xkv, w_xo, ln2_g, ln2_b, w_router, router_bias, w_exp_gu, w_exp_down, w_sh_gu, w_sh_down, ln3_g, ln3_b):
    B, S, D = x.shape
    assert D == D_MODEL and S % TQ == 0 and (B * S) % TM == 0 and S % S5_CHUNK == 0
    xc = x.reshape(B * S, D)
    for l in range(w_in.shape[0]):
        tabs = _rope_tables(positions)
        q, k, v, u, gm, gs = _inproj(xc, tabs, w_in[l], q_norm_g[l], kv_norm_g[l], w_uq[l], w_ukv[l])
        o = _attention(q, k, v, B, S)
        ys = _s5_branch(u, B, S, (s5_a_re[l], s5_a_im[l], s5_log_dt[l], s5_b_re[l], s5_b_im[l],
                                  s5_c_re[l], s5_c_im[l], s5_d[l]))
        x1 = _merge(xc, o, ys, gm, gs, w_mla_o[l], w_s5_glu[l], w_out[l], ln1_g[l], ln1_b[l])
        x2 = _cross_attention(x1, mem, B, S, mem_ln_g[l], mem_ln_b[l], w_xq[l], w_xkv[l], w_xo[l],
                                   ln2_g[l], ln2_b[l])
        xc = _moe(x2, w_router[l], router_bias[l], w_exp_gu[l], w_exp_down[l], w_sh_gu[l],
                  w_sh_down[l], ln3_g[l], ln3_b[l])
    return xc.reshape(B, S, D)
```

```python
import functools
import math

import jax
import jax.numpy as jnp
from jax import lax
from jax.experimental import pallas as pl
from jax.experimental.pallas import tpu as pltpu

F32 = jnp.float32
BF16 = jnp.bfloat16
I32 = jnp.int32
U32 = jnp.uint32

D_MODEL = 1024
MLA_HEADS = 8
QK_NOPE = 64
QK_ROPE = 32
V_HEAD = 64
Q_LORA = 256
KV_LORA = 256
ROPE_THETA = 10000.0
S5_GROUP_CH = 16
S5_WIDTH = 512
S5_GROUPS = 32
S5_STATE = 64
XATTN_HEADS = 4
XATTN_HEAD_DIM = 128
N_EXPERTS = 64
TOP_K = 8
N_EXPERT_GROUPS = 8
TOPK_GROUPS = 4
EXPERT_FF = 256
SHARED_FF = 256
ROUTED_SCALE = 2.5
LN_EPS = 1e-5
RMS_EPS = 1e-6
DEPTH = 1
DEEPNORM_ALPHA = (2.0 * DEPTH) ** 0.25

LANES = 128
HEAD_PAD = 128
ROPE_LO = QK_NOPE
ROPE_HALF = QK_ROPE // 2

TM = 512
TQ = 1024
KV_GROUPS = (2, 1)
S5_CHUNK = 16
TW = 512
OH_ROWS = 512
PR = 16
PAD_ROWS = PR
SEG_ALIGN = 8
SORT_ROWS = TOP_K * TW + N_EXPERTS * SEG_ALIGN
TS = SORT_ROWS + PAD_ROWS
XW = D_MODEL // 2 + LANES
CR = 512
PPC = CR // PR
PLAN_UNROLL = 8
GATHER_AHEAD = 2
N_SLOTS = GATHER_AHEAD + 1
POS_RADIX = 64
POS_NONE_Q = 127
assert SORT_ROWS <= POS_RADIX * POS_NONE_Q and SORT_ROWS % OH_ROWS == 0
VMEM_LIMIT = 48 * 1024 * 1024


def _cparams(*sem):
    return pltpu.CompilerParams(dimension_semantics=sem, vmem_limit_bytes=VMEM_LIMIT)


def _dot(a, b):
    return jnp.dot(a, b, preferred_element_type=F32)


def _dot_nt(a, b, precision=None):
    return lax.dot_general(a, b, (((1,), (1,)), ((), ())), preferred_element_type=F32,
                           precision=precision)


def _dot_exact(a, b):
    return jnp.dot(a, b, preferred_element_type=F32, precision=lax.Precision.HIGHEST)


def _layer_norm(h, g, b):
    mu = jnp.mean(h, axis=-1, keepdims=True)
    c = h - mu
    var = jnp.mean(c * c, axis=-1, keepdims=True)
    return c * lax.rsqrt(var + LN_EPS) * g + b


def _rms_norm(h, g):
    return h * lax.rsqrt(jnp.mean(h * h, axis=-1, keepdims=True) + RMS_EPS) * g


def _full(shape):
    n = len(shape)
    return pl.BlockSpec(shape, lambda *_: (0,) * n)


def _rope_angle_kernel(pos_ref, invf_ref, cos_ref, sin_ref):
    ang = pos_ref[...].astype(F32) * invf_ref[...]
    cos_ref[...] = jnp.cos(ang)
    sin_ref[...] = jnp.sin(ang)


def _rope_tables(positions):
    T = positions.size
    rows = T * ROPE_HALF // LANES
    pos_rep = jnp.repeat(positions.reshape(T), ROPE_HALF).reshape(rows, LANES)
    inv_freq = ROPE_THETA ** (-jnp.arange(0, QK_ROPE, 2, dtype=F32) / QK_ROPE)
    invf = jnp.tile(inv_freq, LANES // ROPE_HALF).reshape(1, LANES)
    cos, sin = pl.pallas_call(
        _rope_angle_kernel,
        out_shape=(jax.ShapeDtypeStruct((rows, LANES), F32),) * 2,
        name="rope_angles",
    )(pos_rep, invf)
    cos = cos.reshape(T, ROPE_HALF)
    sin = sin.reshape(T, ROPE_HALF)
    one = jnp.ones((T, ROPE_LO), F32)
    zero_lo = jnp.zeros((T, ROPE_LO), F32)
    zero_h = jnp.zeros((T, ROPE_HALF), F32)
    tail1 = jnp.ones((T, HEAD_PAD - ROPE_LO - QK_ROPE), F32)
    tail0 = jnp.zeros((T, HEAD_PAD - ROPE_LO - QK_ROPE), F32)
    c_tab = jnp.concatenate([one, cos, cos, tail1], axis=1)
    s_up = jnp.concatenate([zero_lo, -sin, zero_h, tail0], axis=1)
    s_dn = jnp.concatenate([zero_lo, zero_h, sin, tail0], axis=1)
    return c_tab, s_up, s_dn


def _rope(x, c_tab, s_up, s_dn):
    return (x * c_tab + pltpu.roll(x, HEAD_PAD - ROPE_HALF, axis=1) * s_up
            + pltpu.roll(x, ROPE_HALF, axis=1) * s_dn)


def _inproj_kernel(x_ref, c_ref, su_ref, sd_ref, wlat_ref, wu_ref, wgm_ref, wgs_ref, qg_ref, kvg_ref,
                   wuq_ref, wuk_ref, wuv_ref, q_ref, k_ref, v_ref, u_ref, gm_ref, gs_ref):
    xb = x_ref[...].astype(BF16)
    lat = _dot(xb, wlat_ref[...])
    qn = _rms_norm(lat[:, :Q_LORA], qg_ref[...]).astype(BF16)
    kvn = _rms_norm(lat[:, Q_LORA:Q_LORA + KV_LORA], kvg_ref[...]).astype(BF16)
    c_tab, s_up, s_dn = c_ref[...], su_ref[...], sd_ref[...]
    k_rope = _rope(lat[:, Q_LORA + KV_LORA:], c_tab, s_up, s_dn)
    q = _dot(qn, wuq_ref[...])
    k = _dot(kvn, wuk_ref[...])
    scale = (QK_NOPE + QK_ROPE) ** -0.5 * math.log2(math.e)
    for h in range(MLA_HEADS):
        sl = slice(h * HEAD_PAD, (h + 1) * HEAD_PAD)
        q_ref[:, sl] = (_rope(q[:, sl], c_tab, s_up, s_dn) * scale).astype(BF16)
        k_ref[:, sl] = (k[:, sl] + k_rope).astype(BF16)
    v = _dot(kvn, wuv_ref[...])
    ones_lane = lax.broadcasted_iota(I32, (1, v.shape[1]), 1) % HEAD_PAD == V_HEAD
    v_ref[...] = jnp.where(ones_lane, 1.0, v).astype(BF16)
    u_ref[...] = _dot(xb, wu_ref[...])
    gm_ref[...] = _dot(xb, wgm_ref[...]).astype(BF16)
    gs_ref[...] = _dot(xb, wgs_ref[...]).astype(BF16)


def _pad_heads(w, head_w, lo_w):
    K = w.shape[0]
    w = w.reshape(K, MLA_HEADS, head_w)[:, :, :lo_w]
    w = jnp.pad(w, ((0, 0), (0, 0), (0, HEAD_PAD - lo_w)))
    return w.reshape(K, MLA_HEADS * HEAD_PAD)


def _inproj(x2d, tabs, w_in, q_norm_g, kv_norm_g, w_uq, w_ukv):
    T = x2d.shape[0]
    o_rope = Q_LORA + KV_LORA
    o_u = o_rope + QK_ROPE
    o_gm = o_u + S5_WIDTH
    o_gs = o_gm + D_MODEL
    w_rope = jnp.pad(w_in[:, o_rope:o_u], ((0, 0), (ROPE_LO, HEAD_PAD - ROPE_LO - QK_ROPE)))
    w_lat = jnp.concatenate([w_in[:, :o_rope], w_rope], axis=1).astype(BF16)
    w_u = w_in[:, o_u:o_gm].astype(BF16)
    w_gm = w_in[:, o_gm:o_gs].astype(BF16)
    w_gs = w_in[:, o_gs:].astype(BF16)
    wuq = _pad_heads(w_uq, QK_NOPE + QK_ROPE, QK_NOPE + QK_ROPE).astype(BF16)
    kv3 = w_ukv.reshape(KV_LORA, MLA_HEADS, QK_NOPE + V_HEAD)
    wuk = _pad_heads(kv3[:, :, :QK_NOPE].reshape(KV_LORA, -1), QK_NOPE, QK_NOPE).astype(BF16)
    wuv = _pad_heads(kv3[:, :, QK_NOPE:].reshape(KV_LORA, -1), V_HEAD, V_HEAD).astype(BF16)
    HP = MLA_HEADS * HEAD_PAD
    tile = lambda w: pl.BlockSpec((TM, w), lambda i: (i, 0))
    return pl.pallas_call(
        _inproj_kernel,
        grid=(T // TM,),
        in_specs=[tile(D_MODEL), tile(HEAD_PAD), tile(HEAD_PAD), tile(HEAD_PAD),
                  _full(w_lat.shape), _full(w_u.shape), _full(w_gm.shape), _full(w_gs.shape),
                  _full((1, Q_LORA)), _full((1, KV_LORA)),
                  _full(wuq.shape), _full(wuk.shape), _full(wuv.shape)],
        out_specs=[tile(HP), tile(HP), tile(HP), tile(S5_WIDTH), tile(D_MODEL), tile(D_MODEL)],
        out_shape=[jax.ShapeDtypeStruct((T, HP), BF16)] * 3
        + [jax.ShapeDtypeStruct((T, S5_WIDTH), F32)]
        + [jax.ShapeDtypeStruct((T, D_MODEL), BF16)] * 2,
        compiler_params=_cparams("parallel"),
        name="inproj_mla_prep",
    )(x2d, *tabs, w_lat, w_u, w_gm, w_gs, q_norm_g.reshape(1, -1), kv_norm_g.reshape(1, -1),
      wuq, wuk, wuv)


def _attn_kernel(q_ref, k_ref, v_ref, o_ref):
    qi = pl.program_id(2)
    q = q_ref[0]

    def update(qq, k, v, carry, row_offset=None):
        m, acc = carry
        s = _dot_nt(qq, k)
        if row_offset is not None:
            row = lax.broadcasted_iota(I32, s.shape, 0) + row_offset
            col = lax.broadcasted_iota(I32, s.shape, 1)
            s = jnp.where(col <= row, s, -jnp.inf)
        m_new = jnp.maximum(m, jnp.max(s, axis=-1, keepdims=True))
        p = jnp.exp2(s - m_new)
        acc = jnp.exp2(m - m_new) * acc + _dot(p.astype(BF16), v)
        return m_new, acc

    def step(blk, n_blk, carry):
        tk = n_blk * TQ
        start = pl.multiple_of(blk * TQ, TQ)
        return update(q, k_ref[0, pl.ds(start, tk), :], v_ref[0, pl.ds(start, tk), :], carry)

    carry = (jnp.full((TQ, 1), -jnp.inf, F32), jnp.zeros((TQ, HEAD_PAD), F32))
    done = 0
    for n_blk in KV_GROUPS:
        base = done
        n_it = (qi - base) // n_blk
        carry = lax.fori_loop(0, n_it, lambda j, c, base=base, n_blk=n_blk: step(base + j * n_blk, n_blk, c), carry)
        done = base + n_it * n_blk
    hq = TQ // 2
    d0 = pl.multiple_of(qi * TQ, TQ)
    m, acc = carry
    _, acc_lo = update(q[:hq], k_ref[0, pl.ds(d0, hq), :], v_ref[0, pl.ds(d0, hq), :], (m[:hq], acc[:hq]), 0)
    _, acc_hi = update(q[hq:], k_ref[0, pl.ds(d0, TQ), :], v_ref[0, pl.ds(d0, TQ), :], (m[hq:], acc[hq:]), hq)
    o_ref[0, :hq, :] = (acc_lo / acc_lo[:, V_HEAD:V_HEAD + 1]).astype(BF16)
    o_ref[0, hq:, :] = (acc_hi / acc_hi[:, V_HEAD:V_HEAD + 1]).astype(BF16)


def _attention(q, k, v, B, S):
    HP = MLA_HEADS * HEAD_PAD
    q, k, v = (a.reshape(B, S, HP) for a in (q, k, v))
    o = pl.pallas_call(
        _attn_kernel,
        grid=(B, MLA_HEADS, S // TQ),
        in_specs=[pl.BlockSpec((1, TQ, HEAD_PAD), lambda b, h, i: (b, i, h)),
                  pl.BlockSpec((1, S, HEAD_PAD), lambda b, h, i: (b, 0, h)),
                  pl.BlockSpec((1, S, HEAD_PAD), lambda b, h, i: (b, 0, h))],
        out_specs=pl.BlockSpec((1, TQ, HEAD_PAD), lambda b, h, i: (b, i, h)),
        out_shape=jax.ShapeDtypeStruct((B, S, HP), BF16),
        compiler_params=_cparams("parallel", "parallel", "arbitrary"),
        name="mla_flash_attention",
    )(q, k, v)
    return o.reshape(B * S, HP)


CH2 = S5_CHUNK * S5_GROUP_CH
P2 = 2 * S5_STATE
SCAN_LEVELS_MAX = 16


def _s5_prep_kernel(n_levels, arow_ref, acol_ref, bt_ref, ct_ref, d_ref,
                    mt_ref, wt_ref, vt_ref, apow_ref, dvec_ref, mt_acc):
    P, H, L = S5_STATE, S5_GROUP_CH, S5_CHUNK

    def powers(ar, ai, dt, n):
        e = jnp.exp(n * (ar * dt))
        ang = n * (ai * dt)
        return e * jnp.cos(ang), e * jnp.sin(ang)

    def zoh_coef(ar, ai, dt):
        pr, pi = powers(ar, ai, dt, 1.0)
        nr, ni = pr - 1.0, pi
        den = ar * ar + ai * ai
        return (nr * ar + ni * ai) / den, (ni * ar - nr * ai) / den

    row = arow_ref[0]
    ar, ai, dt = row[0:1], row[1:2], jnp.exp(row[2:3])
    cr, ci = zoh_coef(ar, ai, dt)
    bt = bt_ref[0]
    btr = jnp.concatenate([bt[:H], bt[:H]], axis=1)
    bti = jnp.concatenate([bt[H:], bt[H:]], axis=1)
    bbr = cr * btr - ci * bti
    bbi = cr * bti + ci * btr
    lane = lax.broadcasted_iota(I32, (1, P2), 1)
    is_re = lane < P
    ridx = lax.broadcasted_iota(I32, (CH2, P2), 0) // H
    pwr, pwi = powers(ar, ai, dt, (L - 1 - ridx).astype(F32))
    rsel = (lax.broadcasted_iota(I32, (CH2, H), 0) % H == lax.broadcasted_iota(I32, (CH2, H), 1)).astype(F32)
    bbr_t = _dot_exact(rsel, bbr)
    bbi_t = _dot_exact(rsel, bbi)
    w_re = pwr * bbr_t - pwi * bbi_t
    w_im = pwr * bbi_t + pwi * bbr_t
    wt_ref[0] = jnp.where(is_re, w_re, w_im).astype(BF16)
    a_r, a_i = powers(ar, ai, dt, float(L))
    sign = jnp.where(is_re, -1.0, 1.0)
    for lvl in range(SCAN_LEVELS_MAX):
        if lvl < n_levels:
            apow_ref[0, lvl:lvl + 1, :] = a_r
            apow_ref[0, SCAN_LEVELS_MAX + lvl:SCAN_LEVELS_MAX + lvl + 1, :] = sign * a_i
            a_r, a_i = a_r * a_r - a_i * a_i, 2.0 * a_r * a_i
        else:
            apow_ref[0, lvl:lvl + 1, :] = jnp.zeros_like(a_r)
            apow_ref[0, SCAN_LEVELS_MAX + lvl:SCAN_LEVELS_MAX + lvl + 1, :] = jnp.zeros_like(a_r)

    col = acol_ref[0]
    arc, aic, dtc = col[:, 0:1], col[:, 1:2], jnp.exp(col[:, 2:3])
    ct = ct_ref[0]
    csel = (lax.broadcasted_iota(I32, (H, CH2), 1) % H == lax.broadcasted_iota(I32, (H, CH2), 0)).astype(F32)
    ctr = _dot_exact(ct[:, :H], csel)
    cti = _dot_exact(ct[:, H:], csel)
    sidx = (lax.broadcasted_iota(I32, (P, CH2), 1) // H).astype(F32)
    pr, pi = powers(arc, aic, dtc, sidx)
    g_r = ctr * pr - cti * pi
    g_i = ctr * pi + cti * pr
    a1r, a1i = powers(arc, aic, dtc, 1.0)
    v_r = g_r * a1r - g_i * a1i
    v_i = g_r * a1i + g_i * a1r
    vt_ref[0, :P, :] = v_r.astype(BF16)
    vt_ref[0, P:, :] = (-v_i).astype(BF16)
    bbr64, bbi64 = bbr[:, :P], bbi[:, :P]
    kt = _dot_exact(bbr64, g_r) - _dot_exact(bbi64, g_i)
    lane2 = lax.broadcasted_iota(I32, (H, CH2), 1)
    for r in range(L):
        blk = kt if r == 0 else jnp.where(lane2 >= r * H, pltpu.roll(kt, r * H, axis=1), 0.0)
        mt_acc[r * H:(r + 1) * H, :] = blk
    mt_ref[0] = mt_acc[...].astype(BF16)
    dvec_ref[0] = _dot_exact(d_ref[0], csel)


def _s5_prep(n_levels, s5_a_re, s5_a_im, s5_log_dt, s5_b_re, s5_b_im, s5_c_re, s5_c_im, s5_d):
    G, P, H = S5_GROUPS, S5_STATE, S5_GROUP_CH
    ldt = jnp.broadcast_to(s5_log_dt[:, None], (G, P))
    arow = jnp.stack([s5_a_re, s5_a_im, ldt], axis=1)
    arow = jnp.concatenate([arow, arow], axis=2)
    acol = jnp.stack([s5_a_re, s5_a_im, ldt], axis=2)
    bt = jnp.concatenate([s5_b_re.transpose(0, 2, 1), s5_b_im.transpose(0, 2, 1)], axis=1)
    ct = jnp.concatenate([s5_c_re.transpose(0, 2, 1), s5_c_im.transpose(0, 2, 1)], axis=2)
    d = s5_d.reshape(G, 1, H)
    blk = lambda *s: pl.BlockSpec((1,) + s, lambda g: (g, 0, 0))
    return pl.pallas_call(
        functools.partial(_s5_prep_kernel, n_levels),
        grid=(G,),
        in_specs=[blk(3, P2), blk(P, 3), blk(2 * H, P), blk(P, 2 * H), blk(1, H)],
        out_specs=[blk(CH2, CH2), blk(CH2, P2), blk(P2, CH2), blk(2 * SCAN_LEVELS_MAX, P2), blk(1, CH2)],
        out_shape=[jax.ShapeDtypeStruct((G, CH2, CH2), BF16),
                   jax.ShapeDtypeStruct((G, CH2, P2), BF16),
                   jax.ShapeDtypeStruct((G, P2, CH2), BF16),
                   jax.ShapeDtypeStruct((G, 2 * SCAN_LEVELS_MAX, P2), F32),
                   jax.ShapeDtypeStruct((G, 1, CH2), F32)],
        scratch_shapes=[pltpu.VMEM((CH2, CH2), F32)],
        compiler_params=_cparams("parallel"),
        name="s5_discretise",
    )(arow, acol, bt, ct, d)


def _gelu_tanh(y):
    return 0.5 * y * (1.0 + jnp.tanh(math.sqrt(2.0 / math.pi) * (y + 0.044715 * (y * y * y))))


GROUPS_PER_TILE = LANES // S5_GROUP_CH
STEPS_PER_TILE = LANES // S5_GROUP_CH


def _s5_scan_kernel(n_levels, u_ref, mt_ref, wt_ref, vt_ref, apow_ref, dvec_ref, y_ref):
    H, L = S5_GROUP_CH, S5_CHUNK
    nc = u_ref.shape[0] // L
    lane_blk = lax.broadcasted_iota(I32, (nc, LANES), 1) // H
    ridx = lax.broadcasted_iota(I32, (nc, P2), 0)
    steps = [u_ref[pl.ds(l, nc, stride=L), :] for l in range(L)]
    groups = range(GROUPS_PER_TILE)
    us, xs = [], []
    for gi in groups:
        halves = []
        for hh in range(L // STEPS_PER_TILE):
            acc = jnp.zeros((nc, LANES), F32)
            for l8 in range(STEPS_PER_TILE):
                src = steps[hh * STEPS_PER_TILE + l8]
                shift = ((l8 - gi) % STEPS_PER_TILE) * H
                moved = src if shift == 0 else pltpu.roll(src, shift, axis=1)
                acc = jnp.where(lane_blk == l8, moved, acc)
            halves.append(acc)
        u = jnp.concatenate(halves, axis=1)
        us.append(u)
        xs.append(_dot(u.astype(BF16), wt_ref[gi]))
    for lvl in range(n_levels):
        sh = 1 << lvl
        for gi in groups:
            apow = apow_ref[gi]
            prev = jnp.where(ridx >= sh, pltpu.roll(xs[gi], sh, axis=0), 0.0)
            a_r = apow[lvl:lvl + 1]
            a_i = apow[SCAN_LEVELS_MAX + lvl:SCAN_LEVELS_MAX + lvl + 1]
            xs[gi] = xs[gi] + a_r * prev + a_i * pltpu.roll(prev, S5_STATE, axis=1)
    outs = [jnp.zeros((nc, LANES), F32) for _ in range(L)]
    for gi in groups:
        x_in = jnp.where(ridx >= 1, pltpu.roll(xs[gi], 1, axis=0), 0.0)
        u = us[gi]
        y = _gelu_tanh(_dot(u.astype(BF16), mt_ref[gi]) + _dot(x_in.astype(BF16), vt_ref[gi]) + dvec_ref[gi] * u)
        for l in range(L):
            hh, l8 = divmod(l, STEPS_PER_TILE)
            src = y[:, hh * LANES:(hh + 1) * LANES]
            shift = ((gi - l8) % STEPS_PER_TILE) * H
            moved = src if shift == 0 else pltpu.roll(src, shift, axis=1)
            outs[l] = jnp.where(lane_blk == gi, moved, outs[l])
    for l in range(L):
        y_ref[pl.ds(l, nc, stride=L), :] = outs[l]


def _s5_branch(u, B, S, s5_params):
    G, L = S5_GROUPS, S5_CHUNK
    nc = S // L
    n_levels = max(1, (nc - 1).bit_length())
    assert n_levels <= SCAN_LEVELS_MAX and S5_CHUNK % STEPS_PER_TILE == 0
    mt, wt, vt, apow, dvec = _s5_prep(n_levels, *s5_params)
    n_lt = G // GROUPS_PER_TILE
    blk = lambda *s: pl.BlockSpec((GROUPS_PER_TILE,) + s, lambda b, t: (t, 0, 0))
    io = pl.BlockSpec((S, LANES), lambda b, t: (b, t))
    return pl.pallas_call(
        functools.partial(_s5_scan_kernel, n_levels),
        grid=(B, n_lt),
        in_specs=[io, blk(CH2, CH2), blk(CH2, P2), blk(P2, CH2), blk(2 * SCAN_LEVELS_MAX, P2), blk(1, CH2)],
        out_specs=io,
        out_shape=jax.ShapeDtypeStruct((B * S, S5_WIDTH), F32),
        compiler_params=_cparams("parallel", "parallel"),
        name="s5_chunk_scan",
    )(u, mt, wt, vt, apow, dvec)


def _merge_kernel(x_ref, o_ref, ys_ref, gm_ref, gs_ref, wmo_ref, wglu_ref, wout_ref, g_ref, b_ref, x1_ref):
    y_mla = _dot(o_ref[...], wmo_ref[...])
    z = _dot(ys_ref[...].astype(BF16), wglu_ref[...])
    y_s5 = z[:, :D_MODEL] * jax.nn.sigmoid(z[:, D_MODEL:])
    merged = (jax.nn.sigmoid(gm_ref[...].astype(F32)) * y_mla
              + jax.nn.sigmoid(gs_ref[...].astype(F32)) * y_s5)
    mix = _dot(merged.astype(BF16), wout_ref[...])
    x1_ref[...] = _layer_norm(DEEPNORM_ALPHA * x_ref[...] + mix, g_ref[...], b_ref[...])


def _merge(x2d, o, ys, gm, gs, w_mla_o, w_s5_glu, w_out, ln_g, ln_b):
    T = x2d.shape[0]
    HP = MLA_HEADS * HEAD_PAD
    wmo = jnp.pad(w_mla_o.reshape(MLA_HEADS, V_HEAD, D_MODEL), ((0, 0), (0, HEAD_PAD - V_HEAD), (0, 0)))
    wmo = wmo.reshape(HP, D_MODEL).astype(BF16)
    tile = lambda w: pl.BlockSpec((TM, w), lambda i: (i, 0))
    return pl.pallas_call(
        _merge_kernel,
        grid=(T // TM,),
        in_specs=[tile(D_MODEL), tile(HP), tile(S5_WIDTH), tile(D_MODEL), tile(D_MODEL),
                  _full((HP, D_MODEL)), _full((S5_WIDTH, 2 * D_MODEL)), _full((D_MODEL, D_MODEL)),
                  _full((1, D_MODEL)), _full((1, D_MODEL))],
        out_specs=tile(D_MODEL),
        out_shape=jax.ShapeDtypeStruct((T, D_MODEL), F32),
        compiler_params=_cparams("parallel"),
        name="merge_outproj_ln1",
    )(x2d, o, ys, gm, gs, wmo, w_s5_glu.astype(BF16), w_out.astype(BF16),
      ln_g.reshape(1, -1), ln_b.reshape(1, -1))


def _memkv_kernel(mem_ref, g_ref, b_ref, w_ref, k_ref, v_ref):
    m = _layer_norm(mem_ref[0], g_ref[...], b_ref[...]).astype(BF16)
    kv = _dot(m, w_ref[...])
    hd = XATTN_HEADS * XATTN_HEAD_DIM
    k_ref[0] = kv[:, :hd].astype(BF16)
    v_ref[0] = kv[:, hd:].astype(BF16)


def _xattn_kernel(x1_ref, k_ref, v_ref, wq_ref, wo_ref, g_ref, b_ref, x2_ref):
    x1 = x1_ref[0]
    q = (_dot(x1.astype(BF16), wq_ref[...]) * (XATTN_HEAD_DIM ** -0.5)).astype(BF16)
    k = k_ref[0]
    v = v_ref[0]
    outs = []
    for h in range(XATTN_HEADS):
        sl = slice(h * XATTN_HEAD_DIM, (h + 1) * XATTN_HEAD_DIM)
        s = _dot_nt(q[:, sl], k[:, sl])
        p = jnp.exp(s - jnp.max(s, axis=-1, keepdims=True))
        o = _dot(p.astype(BF16), v[:, sl]) / jnp.sum(p, axis=-1, keepdims=True)
        outs.append(o.astype(BF16))
    xa = _dot(jnp.concatenate(outs, axis=1), wo_ref[...])
    x2 = _layer_norm(DEEPNORM_ALPHA * x1 + xa, g_ref[...], b_ref[...])
    x2_ref[0] = x2


def _cross_attention(x1, mem, B, S, mem_ln_g, mem_ln_b, w_xq, w_xkv, w_xo, ln_g, ln_b):
    M = mem.shape[1]
    hd = XATTN_HEADS * XATTN_HEAD_DIM
    row = lambda a: a.reshape(1, -1)
    k, v = pl.pallas_call(
        _memkv_kernel,
        grid=(B,),
        in_specs=[pl.BlockSpec((1, M, D_MODEL), lambda b: (b, 0, 0)), _full((1, D_MODEL)), _full((1, D_MODEL)),
                  _full((D_MODEL, 2 * hd))],
        out_specs=[pl.BlockSpec((1, M, hd), lambda b: (b, 0, 0))] * 2,
        out_shape=[jax.ShapeDtypeStruct((B, M, hd), BF16)] * 2,
        compiler_params=_cparams("parallel"),
        name="memory_kv",
    )(mem, row(mem_ln_g), row(mem_ln_b), w_xkv.astype(BF16))
    tile = lambda w: pl.BlockSpec((1, TM, w), lambda b, i: (b, i, 0))
    x2 = pl.pallas_call(
        _xattn_kernel,
        grid=(B, S // TM),
        in_specs=[tile(D_MODEL),
                  pl.BlockSpec((1, M, hd), lambda b, i: (b, 0, 0)), pl.BlockSpec((1, M, hd), lambda b, i: (b, 0, 0)),
                  _full((D_MODEL, hd)), _full((hd, D_MODEL)), _full((1, D_MODEL)), _full((1, D_MODEL))],
        out_specs=tile(D_MODEL),
        out_shape=jax.ShapeDtypeStruct((B, S, D_MODEL), F32),
        compiler_params=_cparams("parallel", "parallel"),
        name="cross_attention_ln2",
    )(x1.reshape(B, S, D_MODEL), k, v, w_xq.astype(BF16), w_xo.astype(BF16), row(ln_g), row(ln_b))
    return x2.reshape(B * S, D_MODEL)


def _route_sort_kernel(x_ref, w_ref, bias_ref, xs_ref, pos_ref, cnt_ref, off_ref):
    E, per = N_EXPERTS, N_EXPERTS // N_EXPERT_GROUPS
    tm = x_ref.shape[0]
    x = x_ref[...]
    xb = x.astype(BF16)
    x_lo = (x - xb.astype(F32)).astype(BF16)
    w = w_ref[...]
    wb = w.astype(BF16)
    w_lo = (w - wb.astype(F32)).astype(BF16)
    logits = _dot_nt(wb, xb) + (_dot_nt(wb, x_lo) + _dot_nt(w_lo, xb))
    scores = jax.nn.sigmoid(logits)
    sel = scores + bias_ref[...]
    neg = -jnp.inf
    i8 = lax.broadcasted_iota(I32, (per, tm), 0)
    gscore = []
    for g in range(N_EXPERT_GROUPS):
        blk = sel[g * per:(g + 1) * per]
        m1 = jnp.max(blk, axis=0, keepdims=True)
        i1 = jnp.min(jnp.where(blk == m1, i8, per), axis=0, keepdims=True)
        m2 = jnp.max(jnp.where(i8 == i1, neg, blk), axis=0, keepdims=True)
        gscore.append(m1 + m2)
    blocks = []
    for g in range(N_EXPERT_GROUPS):
        ahead = jnp.zeros((1, tm), I32)
        for o in range(N_EXPERT_GROUPS):
            if o == g:
                continue
            before = (gscore[o] >= gscore[g]) if o < g else (gscore[o] > gscore[g])
            ahead = ahead + before.astype(I32)
        blocks.append(jnp.where(ahead < TOPK_GROUPS, sel[g * per:(g + 1) * per], neg))
    cur = jnp.concatenate(blocks, axis=0)
    ie = lax.broadcasted_iota(I32, (E, tm), 0)
    picked = jnp.zeros((E, tm), F32)
    idxs = []
    for _ in range(TOP_K):
        m = jnp.max(cur, axis=0, keepdims=True)
        idx = jnp.min(jnp.where(cur == m, ie, E), axis=0, keepdims=True)
        hit = ie == idx
        picked = jnp.where(hit, 1.0, picked)
        cur = jnp.where(hit, neg, cur)
        idxs.append(idx)
    wsel = scores * picked
    wnorm = wsel / jnp.sum(wsel, axis=0, keepdims=True) * ROUTED_SCALE
    pb = picked.astype(BF16)
    tri = (lax.broadcasted_iota(I32, (tm, tm), 0) <= lax.broadcasted_iota(I32, (tm, tm), 1)).astype(BF16)
    incl = _dot(pb, tri)
    cnt_col = jnp.sum(picked, axis=1, keepdims=True)
    lower = (lax.broadcasted_iota(I32, (E, E), 1) < lax.broadcasted_iota(I32, (E, E), 0)).astype(F32)
    al_col = jnp.floor((cnt_col + (SEG_ALIGN - 1.0)) * (1.0 / SEG_ALIGN)) * SEG_ALIGN
    off_col = _dot_exact(lower, jnp.broadcast_to(al_col, (E, LANES)))[:, 0:1]
    pos = off_col + incl - 1.0
    q = jnp.floor(pos * (1.0 / POS_RADIX))
    pq = jnp.concatenate([jnp.where(picked > 0.0, POS_RADIX * q, POS_RADIX * POS_NONE_Q),
                          jnp.where(picked > 0.0, pos - POS_RADIX * q, 0.0)], axis=0)
    pos_ref[...] = pq
    pqb = pq.astype(BF16)
    cnt_ref[0] = jnp.broadcast_to(cnt_col, (E, LANES)).astype(I32)
    off_ref[0] = jnp.broadcast_to(off_col, (E, LANES)).astype(I32)
    cnt_row = _dot_nt(jnp.ones((8, tm), BF16), pb)
    al_row = jnp.floor((cnt_row + (SEG_ALIGN - 1.0)) * (1.0 / SEG_ALIGN)) * SEG_ALIGN
    er = lax.broadcasted_iota(I32, (E, 4 * E), 0)
    ec = lax.broadcasted_iota(I32, (E, 4 * E), 1) % E
    off_row4 = _dot_exact(al_row, (er < ec).astype(F32))[0:1]
    end_row4 = off_row4 + _dot_exact(al_row, (er == ec).astype(F32))[0:1]
    w_hi = wnorm.astype(BF16).astype(F32)
    w_r1 = wnorm - w_hi
    w_mid = w_r1.astype(BF16).astype(F32)
    w_lo = w_r1 - w_mid
    w4_t = jnp.concatenate([w_hi, w_mid, w_lo, jnp.zeros_like(w_lo)], axis=0).T
    xw = jnp.concatenate([xb, w4_t.astype(BF16)], axis=1)
    half = D_MODEL // 2

    def build(c, carry):
        r0 = pl.multiple_of(c * OH_ROWS, OH_ROWS)
        jr = (lax.broadcasted_iota(I32, (OH_ROWS, 4 * E), 0) + r0).astype(F32)
        member4 = jnp.where(jr >= off_row4, jnp.where(jr < end_row4, 1.0, 0.0), 0.0)
        target = _dot(member4[:, :2 * E].astype(BF16), pqb)
        j = (lax.broadcasted_iota(I32, (OH_ROWS, tm), 0) + r0).astype(F32)
        ohb = jnp.where(target == j, 1.0, 0.0).astype(BF16)
        rows = _dot(ohb, xw)
        xs_ref[0, pl.ds(r0, OH_ROWS), 0:half] = _pack_bf16_pair(rows[:, :half], rows[:, half:D_MODEL])
        wrow = jnp.sum(member4 * rows[:, D_MODEL:], axis=1, keepdims=True)
        xs_ref[0, pl.ds(r0, OH_ROWS), half:] = pltpu.bitcast(jnp.broadcast_to(wrow, (OH_ROWS, LANES)), I32)
        return carry

    lax.fori_loop(0, SORT_ROWS // OH_ROWS, build, 0)
    xs_ref[0, SORT_ROWS:, :] = jnp.zeros((PAD_ROWS, XW), I32)


def _route_sort(x2, w_router, router_bias):
    T = x2.shape[0]
    E = N_EXPERTS
    nt = T // TW
    seg = pl.BlockSpec((1, E, LANES), lambda i: (i, 0, 0))
    return pl.pallas_call(
        _route_sort_kernel,
        grid=(nt,),
        in_specs=[pl.BlockSpec((TW, D_MODEL), lambda i: (i, 0)), _full((E, D_MODEL)), _full((E, 1))],
        out_specs=[pl.BlockSpec((1, TS, XW), lambda i: (i, 0, 0)),
                   pl.BlockSpec((2 * E, TW), lambda i: (0, i)), seg, seg],
        out_shape=[jax.ShapeDtypeStruct((nt, TS, XW), I32), jax.ShapeDtypeStruct((2 * E, T), F32),
                   jax.ShapeDtypeStruct((nt, E, LANES), I32), jax.ShapeDtypeStruct((nt, E, LANES), I32)],
        compiler_params=_cparams("parallel"),
        name="route_local_sort",
    )(x2, w_router.T, router_bias.reshape(E, 1))


def _plan_kernel(nt, n_pieces_max, n_chunks_max, cnt_ref, off_ref, pa_ref, ce_ref, nch_ref):
    def unused(p, z):
        pa_ref[p] = -SEG_ALIGN
        return z

    def per_expert(e, carry):
        p0, g = carry

        def per_tile(i, p):
            n = cnt_ref[i * N_EXPERTS + e]
            base = i * TS + off_ref[i * N_EXPERTS + e]
            n_pc = (n + PR - 1) // PR

            def put(q, c):
                pa_ref[p + q] = base + q * PR
                return c

            for q in range(PLAN_UNROLL):
                pa_ref[p + q] = base + q * PR
            lax.fori_loop(PLAN_UNROLL, n_pc, put, 0)
            return p + n_pc

        p1 = lax.fori_loop(0, nt, per_tile, p0)
        n_ch = (p1 - p0 + PPC - 1) // PPC

        lax.fori_loop(p1, p0 + n_ch * PPC, unused, 0)

        def mark(c, z):
            ce_ref[g + c] = e
            return z

        lax.fori_loop(0, n_ch, mark, 0)
        return p0 + n_ch * PPC, g + n_ch

    p_end, g = lax.fori_loop(0, N_EXPERTS, per_expert, (jnp.int32(0), jnp.int32(0)))
    lax.fori_loop(p_end, n_pieces_max, unused, 0)
    nch_ref[0] = g

    def tail(c, z):
        ce_ref[c] = N_EXPERTS - 1
        return z

    lax.fori_loop(g, n_chunks_max, tail, 0)


def _plan(cnt, off, nt, n_pieces_max, n_chunks_max):
    smem = pl.BlockSpec(memory_space=pltpu.SMEM)
    return pl.pallas_call(
        functools.partial(_plan_kernel, nt, n_pieces_max, n_chunks_max),
        in_specs=[smem, smem],
        out_specs=[smem, smem, smem],
        out_shape=[jax.ShapeDtypeStruct((n_pieces_max,), I32), jax.ShapeDtypeStruct((n_chunks_max,), I32),
                   jax.ShapeDtypeStruct((1,), I32)],
        name="moe_plan",
    )(cnt, off)


def _unpack_bf16_pair(w):
    lo = pltpu.bitcast(lax.shift_left(w, jnp.int32(16)), F32).astype(BF16)
    hi = pltpu.bitcast(w & jnp.int32(-65536), F32).astype(BF16)
    return lo, hi


def _pack_bf16_pair(lo, hi):
    lo_bits = lax.shift_right_logical(pltpu.bitcast(lo.astype(BF16).astype(F32), I32), jnp.int32(16))
    hi_bits = pltpu.bitcast(hi.astype(BF16).astype(F32), I32)
    return hi_bits | lo_bits


def _expert_kernel(ce_ref, pa_ref, nch_ref, tot_ref, xs_hbm, wgu_ref, wdn_ref, y_hbm,
                   xbuf, ybuf, zbuf, wgu_b, wdn_b, gsem, wsem, zsem):
    g = pl.program_id(0)
    n = nch_ref[0]
    slot = g % N_SLOTS
    half = D_MODEL // 2

    dummy_base = (tot_ref.shape[0] - 1) * TS

    def gather_copy(s, a, jp):
        return pltpu.make_async_copy(xs_hbm.at[pl.ds(pl.multiple_of(a, SEG_ALIGN), PR), :],
                                     xbuf.at[s, pl.ds(jp * PR, PR), :], gsem.at[s])

    def write_copy(s, a, jp):
        return pltpu.make_async_copy(ybuf.at[s, pl.ds(jp * PR, PR), :],
                                     y_hbm.at[pl.ds(pl.multiple_of(a, SEG_ALIGN), PR), :], wsem.at[s])

    def gather_start(c, s):
        for jp in range(PPC):
            gather_copy(s, jnp.maximum(pa_ref[c * PPC + jp], 0), jp).start()

    def write_start(c, s):
        for jp in range(PPC):
            a = pa_ref[c * PPC + jp]
            write_copy(s, jnp.where(a >= 0, a, dummy_base + jp * PR), jp).start()

    def gather_wait(s):
        for jp in range(PPC):
            gather_copy(s, 0, jp).wait()

    def write_wait(s):
        for jp in range(PPC):
            write_copy(s, 0, jp).wait()

    @pl.when(g == 0)
    def _():
        gather_start(0, 0)
        ybuf[...] = jnp.zeros_like(ybuf)
        zbuf[...] = jnp.zeros_like(zbuf)

        def zero_copy(row):
            return pltpu.make_async_copy(zbuf, y_hbm.at[pl.ds(pl.multiple_of(row, SEG_ALIGN), SEG_ALIGN), :], zsem)

        def tail_of(i):
            return i * TS + tot_ref[i], (TS - tot_ref[i]) // SEG_ALIGN

        def zero_start(i, c):
            base, cnt8 = tail_of(i)
            lax.fori_loop(0, cnt8, lambda q, z: (zero_copy(base + q * SEG_ALIGN).start(), z)[1], 0)
            return c

        def zero_wait(i, c):
            base, cnt8 = tail_of(i)
            lax.fori_loop(0, cnt8, lambda q, z: (zero_copy(base + q * SEG_ALIGN).wait(), z)[1], 0)
            return c

        lax.fori_loop(0, tot_ref.shape[0], zero_start, 0)
        lax.fori_loop(0, tot_ref.shape[0], zero_wait, 0)

    @pl.when((g == 0) & (n > 1))
    def _():
        gather_start(1, 1)

    @pl.when(g + GATHER_AHEAD < n)
    def _():
        gather_start(g + GATHER_AHEAD, (g + GATHER_AHEAD) % N_SLOTS)

    e_cur = ce_ref[g]
    e_prev = ce_ref[jnp.maximum(g - 1, 0)]
    e_prev2 = ce_ref[jnp.maximum(g - 2, 0)]

    @pl.when(g < n)
    def _():
        @pl.when((g == 0) | (e_cur != e_prev))
        def _():
            wgu_b[...] = wgu_ref[0].astype(BF16)
            wdn_b[...] = wdn_ref[0].astype(BF16)

        gather_wait(slot)

        @pl.when((g >= 2) & (e_prev == e_prev2))
        def _():
            write_wait((g - 2) % N_SLOTS)

        @pl.when((g >= 1) & (e_cur != e_prev))
        def _():
            write_wait((g - 1) % N_SLOTS)

        def swiglu_rows(rows):
            xw = xbuf[slot, :rows, :]
            lo, hi = _unpack_bf16_pair(xw[:, :half])
            h = _dot(jnp.concatenate([lo, hi], axis=1), wgu_b[...])
            gate, up = h[:, :EXPERT_FF], h[:, EXPERT_FF:]
            act = (gate * jax.nn.sigmoid(gate) * up).astype(BF16)
            y = _dot(act, wdn_b[...])
            w_row = pltpu.bitcast(xw[:, half:], F32)
            y = y * jnp.concatenate([w_row] * (D_MODEL // LANES), axis=1)
            ybuf[slot, :rows, :] = _pack_bf16_pair(y[:, :half], y[:, half:])

        second_half_used = pa_ref[g * PPC + PPC // 2] >= 0

        @pl.when(second_half_used)
        def _():
            swiglu_rows(CR)

        @pl.when(jnp.logical_not(second_half_used))
        def _():
            swiglu_rows(CR // 2)

        write_start(g, slot)

    @pl.when(g == pl.num_programs(0) - 1)
    def _():
        @pl.when((n >= 2) & (ce_ref[jnp.maximum(n - 1, 0)] == ce_ref[jnp.maximum(n - 2, 0)]))
        def _():
            write_wait((n - 2) % N_SLOTS)

        write_wait((n - 1) % N_SLOTS)


def _experts(ce, pa, nch, tot, xs, w_exp_gu, w_exp_down):
    n_rows = xs.shape[0]
    half = D_MODEL // 2
    grid_spec = pltpu.PrefetchScalarGridSpec(
        num_scalar_prefetch=4,
        grid=(ce.shape[0],),
        in_specs=[pl.BlockSpec(memory_space=pl.ANY),
                  pl.BlockSpec((1, D_MODEL, 2 * EXPERT_FF), lambda g, ce, pa, nch, tot: (ce[g], 0, 0)),
                  pl.BlockSpec((1, EXPERT_FF, D_MODEL), lambda g, ce, pa, nch, tot: (ce[g], 0, 0))],
        out_specs=pl.BlockSpec(memory_space=pl.ANY),
        scratch_shapes=[pltpu.VMEM((N_SLOTS, CR, XW), I32), pltpu.VMEM((N_SLOTS, CR, half), I32),
                        pltpu.VMEM((SEG_ALIGN, half), I32),
                        pltpu.VMEM((D_MODEL, 2 * EXPERT_FF), BF16), pltpu.VMEM((EXPERT_FF, D_MODEL), BF16),
                        pltpu.SemaphoreType.DMA((N_SLOTS,)), pltpu.SemaphoreType.DMA((N_SLOTS,)),
                        pltpu.SemaphoreType.DMA(())],
    )
    return pl.pallas_call(
        _expert_kernel,
        grid_spec=grid_spec,
        out_shape=jax.ShapeDtypeStruct((n_rows + TS, half), I32),
        compiler_params=_cparams("arbitrary"),
        name="moe_grouped_swiglu",
    )(ce, pa, nch, tot, xs, w_exp_gu, w_exp_down)


def _combine_kernel(tot_ref, y_ref, pos_ref, cnt_ref, off_ref, x2_ref, wsg_ref, wsd_ref, g_ref, b_ref, out_ref,
                    acc_lo, acc_hi):
    tm = x2_ref.shape[0]
    E = N_EXPERTS
    n_sorted = tot_ref[pl.program_id(0)]
    x2 = x2_ref[...]
    h = _dot(x2.astype(BF16), wsg_ref[...])
    gate, up = h[:, :SHARED_FF], h[:, SHARED_FF:]
    shared = _dot((gate * jax.nn.sigmoid(gate) * up).astype(BF16), wsd_ref[...])
    pq_t = pos_ref[...].T.astype(BF16)
    off_col = off_ref[0][:, 0:1].astype(F32)
    cnt_col = cnt_ref[0][:, 0:1].astype(F32)
    end_col = off_col + jnp.floor((cnt_col + (SEG_ALIGN - 1.0)) * (1.0 / SEG_ALIGN)) * SEG_ALIGN
    off_col2 = jnp.concatenate([off_col, off_col], axis=0)
    end_col2 = jnp.concatenate([end_col, end_col], axis=0)
    acc_lo[...] = jnp.zeros_like(acc_lo)
    acc_hi[...] = jnp.zeros_like(acc_hi)

    def gather_back(c, carry):
        r0 = pl.multiple_of(c * OH_ROWS, OH_ROWS)
        jc = (lax.broadcasted_iota(I32, (2 * E, OH_ROWS), 1) + r0).astype(F32)
        member2 = jnp.where(jc >= off_col2, jnp.where(jc < end_col2, 1.0, 0.0), 0.0)
        target = _dot(pq_t, member2.astype(BF16))
        j = (lax.broadcasted_iota(I32, (tm, OH_ROWS), 1) + r0).astype(F32)
        ohb = jnp.where(target == j, 1.0, 0.0).astype(BF16)
        yw = y_ref[0, pl.ds(r0, OH_ROWS), :]
        row = lax.broadcasted_iota(I32, yw.shape, 0) + r0
        lo, hi = _unpack_bf16_pair(jnp.where(row < n_sorted, yw, 0))
        acc_lo[...] += _dot(ohb, lo)
        acc_hi[...] += _dot(ohb, hi)
        return carry

    lax.fori_loop(0, SORT_ROWS // OH_ROWS, gather_back, 0)
    ff = shared + jnp.concatenate([acc_lo[...], acc_hi[...]], axis=1)
    out_ref[...] = _layer_norm(DEEPNORM_ALPHA * x2 + ff, g_ref[...], b_ref[...])


def _combine(tot, y, pos, cnt, off, x2, w_sh_gu, w_sh_down, ln_g, ln_b):
    T = x2.shape[0]
    nt = T // TW
    half = D_MODEL // 2
    tile = lambda w: pl.BlockSpec((TW, w), lambda i: (i, 0))
    return pl.pallas_call(
        _combine_kernel,
        grid=(nt,),
        in_specs=[pl.BlockSpec(memory_space=pltpu.SMEM),
                  pl.BlockSpec((1, TS, half), lambda i: (i, 0, 0)),
                  pl.BlockSpec((2 * N_EXPERTS, TW), lambda i: (0, i)),
                  pl.BlockSpec((1, N_EXPERTS, LANES), lambda i: (i, 0, 0)),
                  pl.BlockSpec((1, N_EXPERTS, LANES), lambda i: (i, 0, 0)), tile(D_MODEL),
                  _full((D_MODEL, 2 * SHARED_FF)), _full((SHARED_FF, D_MODEL)),
                  _full((1, D_MODEL)), _full((1, D_MODEL))],
        out_specs=tile(D_MODEL),
        out_shape=jax.ShapeDtypeStruct((T, D_MODEL), F32),
        scratch_shapes=[pltpu.VMEM((TW, half), F32), pltpu.VMEM((TW, half), F32)],
        compiler_params=_cparams("parallel"),
        name="moe_combine_shared_ln3",
    )(tot, y.reshape(nt + 1, TS, half), pos, cnt, off, x2, w_sh_gu.astype(BF16), w_sh_down.astype(BF16),
      ln_g.reshape(1, -1), ln_b.reshape(1, -1))


def _moe(x2, w_router, router_bias, w_exp_gu, w_exp_down, w_sh_gu, w_sh_down, ln_g, ln_b):
    T = x2.shape[0]
    E = N_EXPERTS
    nt = T // TW
    xs, pos, cnt, off = _route_sort(x2, w_router, router_bias)
    n_pieces_max = (T * TOP_K + nt * E * (PR - 1)) // PR
    n_chunks_max = n_pieces_max // PPC + E
    pa, ce, nch = _plan(cnt[:, :, 0].reshape(nt * E), off[:, :, 0].reshape(nt * E), nt,
                        n_chunks_max * PPC + PLAN_UNROLL, n_chunks_max)
    last_al = (cnt[:, E - 1, 0] + SEG_ALIGN - 1) // SEG_ALIGN * SEG_ALIGN
    tot = jnp.concatenate([off[:, E - 1, 0] + last_al, jnp.zeros((1,), I32)])
    y = _experts(ce, pa, nch, tot, xs.reshape(nt * TS, XW), w_exp_gu, w_exp_down)
    return _combine(tot, y, pos, cnt, off, x2, w_sh_gu, w_sh_down, ln_g, ln_b)


def kernel(x, mem, positions, w_in, q_norm_g, kv_norm_g, w_uq, w_ukv, w_mla_o, s5_a_re, s5_a_im, s5_log_dt, s5_b_re, s5_b_im, s5_c_re, s5_c_im, s5_d, w_s5_glu, w_out, ln1_g, ln1_b, mem_ln_g, mem_ln_b, w_xq, w_xkv, w_xo, ln2_g, ln2_b, w_router, router_bias, w_exp_gu, w_exp_down, w_sh_gu, w_sh_down, ln3_g, ln3_b):
    B, S, D = x.shape
    assert D == D_MODEL and S % TQ == 0 and (B * S) % TM == 0 and S % S5_CHUNK == 0
    xc = x.reshape(B * S, D)
    for l in range(w_in.shape[0]):
        tabs = _rope_tables(positions)
        q, k, v, u, gm, gs = _inproj(xc, tabs, w_in[l], q_norm_g[l], kv_norm_g[l], w_uq[l], w_ukv[l])
        o = _attention(q, k, v, B, S)
        ys = _s5_branch(u, B, S, (s5_a_re[l], s5_a_im[l], s5_log_dt[l], s5_b_re[l], s5_b_im[l],
                                  s5_c_re[l], s5_c_im[l], s5_d[l]))
        x1 = _merge(xc, o, ys, gm, gs, w_mla_o[l], w_s5_glu[l], w_out[l], ln1_g[l], ln1_b[l])
        x2 = _cross_attention(x1, mem, B, S, mem_ln_g[l], mem_ln_b[l], w_xq[l], w_xkv[l], w_xo[l],
                                   ln2_g[l], ln2_b[l])
        xc = _moe(x2, w_router[l], router_bias[l], w_exp_gu[l], w_exp_down[l], w_sh_gu[l],
                  w_sh_down[l], ln3_g[l], ln3_b[l])
    return xc.reshape(B, S, D)
```

```python
import functools
import math

import jax
import jax.numpy as jnp
from jax import lax
from jax.experimental import pallas as pl
from jax.experimental.pallas import tpu as pltpu

F32 = jnp.float32
BF16 = jnp.bfloat16
I32 = jnp.int32
U32 = jnp.uint32

D_MODEL = 1024
MLA_HEADS = 8
QK_NOPE = 64
QK_ROPE = 32
V_HEAD = 64
Q_LORA = 256
KV_LORA = 256
ROPE_THETA = 10000.0
S5_GROUP_CH = 16
S5_WIDTH = 512
S5_GROUPS = 32
S5_STATE = 64
XATTN_HEADS = 4
XATTN_HEAD_DIM = 128
N_EXPERTS = 64
TOP_K = 8
N_EXPERT_GROUPS = 8
TOPK_GROUPS = 4
EXPERT_FF = 256
SHARED_FF = 256
ROUTED_SCALE = 2.5
LN_EPS = 1e-5
RMS_EPS = 1e-6
DEPTH = 1
DEEPNORM_ALPHA = (2.0 * DEPTH) ** 0.25

LANES = 128
HEAD_PAD = 128
ROPE_LO = QK_NOPE
ROPE_HALF = QK_ROPE // 2

TM = 512
TQ = 1024
KV_GROUPS = (2, 1)
S5_CHUNK = 16
TW = 512
OH_ROWS = 512
PR = 16
PAD_ROWS = PR
SEG_ALIGN = 8
SORT_ROWS = TOP_K * TW + N_EXPERTS * SEG_ALIGN
TS = SORT_ROWS + PAD_ROWS
XW = D_MODEL // 2 + LANES
CR = 512
PPC = CR // PR
GATHER_AHEAD = 2
N_SLOTS = GATHER_AHEAD + 1
POS_RADIX = 64
POS_NONE_Q = 127
assert SORT_ROWS <= POS_RADIX * POS_NONE_Q and SORT_ROWS % OH_ROWS == 0
VMEM_LIMIT = 48 * 1024 * 1024


def _cparams(*sem):
    return pltpu.CompilerParams(dimension_semantics=sem, vmem_limit_bytes=VMEM_LIMIT)


def _dot(a, b):
    return jnp.dot(a, b, preferred_element_type=F32)


def _dot_nt(a, b, precision=None):
    return lax.dot_general(a, b, (((1,), (1,)), ((), ())), preferred_element_type=F32,
                           precision=precision)


def _dot_exact(a, b):
    return jnp.dot(a, b, preferred_element_type=F32, precision=lax.Precision.HIGHEST)


def _layer_norm(h, g, b):
    mu = jnp.mean(h, axis=-1, keepdims=True)
    c = h - mu
    var = jnp.mean(c * c, axis=-1, keepdims=True)
    return c * lax.rsqrt(var + LN_EPS) * g + b


def _rms_norm(h, g):
    return h * lax.rsqrt(jnp.mean(h * h, axis=-1, keepdims=True) + RMS_EPS) * g


def _full(shape):
    n = len(shape)
    return pl.BlockSpec(shape, lambda *_: (0,) * n)


def _rope_angle_kernel(pos_ref, invf_ref, cos_ref, sin_ref):
    ang = pos_ref[...].astype(F32) * invf_ref[...]
    cos_ref[...] = jnp.cos(ang)
    sin_ref[...] = jnp.sin(ang)


def _rope_tables(positions):
    T = positions.size
    rows = T * ROPE_HALF // LANES
    pos_rep = jnp.repeat(positions.reshape(T), ROPE_HALF).reshape(rows, LANES)
    inv_freq = ROPE_THETA ** (-jnp.arange(0, QK_ROPE, 2, dtype=F32) / QK_ROPE)
    invf = jnp.tile(inv_freq, LANES // ROPE_HALF).reshape(1, LANES)
    cos, sin = pl.pallas_call(
        _rope_angle_kernel,
        out_shape=(jax.ShapeDtypeStruct((rows, LANES), F32),) * 2,
        name="rope_angles",
    )(pos_rep, invf)
    cos = cos.reshape(T, ROPE_HALF)
    sin = sin.reshape(T, ROPE_HALF)
    one = jnp.ones((T, ROPE_LO), F32)
    zero_lo = jnp.zeros((T, ROPE_LO), F32)
    zero_h = jnp.zeros((T, ROPE_HALF), F32)
    tail1 = jnp.ones((T, HEAD_PAD - ROPE_LO - QK_ROPE), F32)
    tail0 = jnp.zeros((T, HEAD_PAD - ROPE_LO - QK_ROPE), F32)
    c_tab = jnp.concatenate([one, cos, cos, tail1], axis=1)
    s_up = jnp.concatenate([zero_lo, -sin, zero_h, tail0], axis=1)
    s_dn = jnp.concatenate([zero_lo, zero_h, sin, tail0], axis=1)
    return c_tab, s_up, s_dn


def _rope(x, c_tab, s_up, s_dn):
    return (x * c_tab + pltpu.roll(x, HEAD_PAD - ROPE_HALF, axis=1) * s_up
            + pltpu.roll(x, ROPE_HALF, axis=1) * s_dn)


def _inproj_kernel(x_ref, c_ref, su_ref, sd_ref, wlat_ref, wu_ref, wgm_ref, wgs_ref, qg_ref, kvg_ref,
                   wuq_ref, wuk_ref, wuv_ref, q_ref, k_ref, v_ref, u_ref, gm_ref, gs_ref):
    xb = x_ref[...].astype(BF16)
    lat = _dot(xb, wlat_ref[...])
    qn = _rms_norm(lat[:, :Q_LORA], qg_ref[...]).astype(BF16)
    kvn = _rms_norm(lat[:, Q_LORA:Q_LORA + KV_LORA], kvg_ref[...]).astype(BF16)
    c_tab, s_up, s_dn = c_ref[...], su_ref[...], sd_ref[...]
    k_rope = _rope(lat[:, Q_LORA + KV_LORA:], c_tab, s_up, s_dn)
    q = _dot(qn, wuq_ref[...])
    k = _dot(kvn, wuk_ref[...])
    scale = (QK_NOPE + QK_ROPE) ** -0.5 * math.log2(math.e)
    for h in range(MLA_HEADS):
        sl = slice(h * HEAD_PAD, (h + 1) * HEAD_PAD)
        q_ref[:, sl] = (_rope(q[:, sl], c_tab, s_up, s_dn) * scale).astype(BF16)
        k_ref[:, sl] = (k[:, sl] + k_rope).astype(BF16)
    v = _dot(kvn, wuv_ref[...])
    ones_lane = lax.broadcasted_iota(I32, (1, v.shape[1]), 1) % HEAD_PAD == V_HEAD
    v_ref[...] = jnp.where(ones_lane, 1.0, v).astype(BF16)
    u_ref[...] = _dot(xb, wu_ref[...])
    gm_ref[...] = _dot(xb, wgm_ref[...]).astype(BF16)
    gs_ref[...] = _dot(xb, wgs_ref[...]).astype(BF16)


def _pad_heads(w, head_w, lo_w):
    K = w.shape[0]
    w = w.reshape(K, MLA_HEADS, head_w)[:, :, :lo_w]
    w = jnp.pad(w, ((0, 0), (0, 0), (0, HEAD_PAD - lo_w)))
    return w.reshape(K, MLA_HEADS * HEAD_PAD)


def _inproj(x2d, tabs, w_in, q_norm_g, kv_norm_g, w_uq, w_ukv):
    T = x2d.shape[0]
    o_rope = Q_LORA + KV_LORA
    o_u = o_rope + QK_ROPE
    o_gm = o_u + S5_WIDTH
    o_gs = o_gm + D_MODEL
    w_rope = jnp.pad(w_in[:, o_rope:o_u], ((0, 0), (ROPE_LO, HEAD_PAD - ROPE_LO - QK_ROPE)))
    w_lat = jnp.concatenate([w_in[:, :o_rope], w_rope], axis=1).astype(BF16)
    w_u = w_in[:, o_u:o_gm].astype(BF16)
    w_gm = w_in[:, o_gm:o_gs].astype(BF16)
    w_gs = w_in[:, o_gs:].astype(BF16)
    wuq = _pad_heads(w_uq, QK_NOPE + QK_ROPE, QK_NOPE + QK_ROPE).astype(BF16)
    kv3 = w_ukv.reshape(KV_LORA, MLA_HEADS, QK_NOPE + V_HEAD)
    wuk = _pad_heads(kv3[:, :, :QK_NOPE].reshape(KV_LORA, -1), QK_NOPE, QK_NOPE).astype(BF16)
    wuv = _pad_heads(kv3[:, :, QK_NOPE:].reshape(KV_LORA, -1), V_HEAD, V_HEAD).astype(BF16)
    HP = MLA_HEADS * HEAD_PAD
    tile = lambda w: pl.BlockSpec((TM, w), lambda i: (i, 0))
    return pl.pallas_call(
        _inproj_kernel,
        grid=(T // TM,),
        in_specs=[tile(D_MODEL), tile(HEAD_PAD), tile(HEAD_PAD), tile(HEAD_PAD),
                  _full(w_lat.shape), _full(w_u.shape), _full(w_gm.shape), _full(w_gs.shape),
                  _full((1, Q_LORA)), _full((1, KV_LORA)),
                  _full(wuq.shape), _full(wuk.shape), _full(wuv.shape)],
        out_specs=[tile(HP), tile(HP), tile(HP), tile(S5_WIDTH), tile(D_MODEL), tile(D_MODEL)],
        out_shape=[jax.ShapeDtypeStruct((T, HP), BF16)] * 3
        + [jax.ShapeDtypeStruct((T, S5_WIDTH), F32)]
        + [jax.ShapeDtypeStruct((T, D_MODEL), BF16)] * 2,
        compiler_params=_cparams("parallel"),
        name="inproj_mla_prep",
    )(x2d, *tabs, w_lat, w_u, w_gm, w_gs, q_norm_g.reshape(1, -1), kv_norm_g.reshape(1, -1),
      wuq, wuk, wuv)


def _attn_kernel(q_ref, k_ref, v_ref, o_ref):
    qi = pl.program_id(2)
    q = q_ref[0]

    def update(qq, k, v, carry, row_offset=None):
        m, acc = carry
        s = _dot_nt(qq, k)
        if row_offset is not None:
            row = lax.broadcasted_iota(I32, s.shape, 0) + row_offset
            col = lax.broadcasted_iota(I32, s.shape, 1)
            s = jnp.where(col <= row, s, -jnp.inf)
        m_new = jnp.maximum(m, jnp.max(s, axis=-1, keepdims=True))
        p = jnp.exp2(s - m_new)
        acc = jnp.exp2(m - m_new) * acc + _dot(p.astype(BF16), v)
        return m_new, acc

    def step(blk, n_blk, carry):
        tk = n_blk * TQ
        start = pl.multiple_of(blk * TQ, TQ)
        return update(q, k_ref[0, pl.ds(start, tk), :], v_ref[0, pl.ds(start, tk), :], carry)

    carry = (jnp.full((TQ, 1), -jnp.inf, F32), jnp.zeros((TQ, HEAD_PAD), F32))
    done = 0
    for n_blk in KV_GROUPS:
        base = done
        n_it = (qi - base) // n_blk
        carry = lax.fori_loop(0, n_it, lambda j, c, base=base, n_blk=n_blk: step(base + j * n_blk, n_blk, c), carry)
        done = base + n_it * n_blk
    hq = TQ // 2
    d0 = pl.multiple_of(qi * TQ, TQ)
    m, acc = carry
    _, acc_lo = update(q[:hq], k_ref[0, pl.ds(d0, hq), :], v_ref[0, pl.ds(d0, hq), :], (m[:hq], acc[:hq]), 0)
    _, acc_hi = update(q[hq:], k_ref[0, pl.ds(d0, TQ), :], v_ref[0, pl.ds(d0, TQ), :], (m[hq:], acc[hq:]), hq)
    o_ref[0, :hq, :] = (acc_lo / acc_lo[:, V_HEAD:V_HEAD + 1]).astype(BF16)
    o_ref[0, hq:, :] = (acc_hi / acc_hi[:, V_HEAD:V_HEAD + 1]).astype(BF16)


def _attention(q, k, v, B, S):
    HP = MLA_HEADS * HEAD_PAD
    q, k, v = (a.reshape(B, S, HP) for a in (q, k, v))
    o = pl.pallas_call(
        _attn_kernel,
        grid=(B, MLA_HEADS, S // TQ),
        in_specs=[pl.BlockSpec((1, TQ, HEAD_PAD), lambda b, h, i: (b, i, h)),
                  pl.BlockSpec((1, S, HEAD_PAD), lambda b, h, i: (b, 0, h)),
                  pl.BlockSpec((1, S, HEAD_PAD), lambda b, h, i: (b, 0, h))],
        out_specs=pl.BlockSpec((1, TQ, HEAD_PAD), lambda b, h, i: (b, i, h)),
        out_shape=jax.ShapeDtypeStruct((B, S, HP), BF16),
        compiler_params=_cparams("parallel", "parallel", "arbitrary"),
        name="mla_flash_attention",
    )(q, k, v)
    return o.reshape(B * S, HP)


CH2 = S5_CHUNK * S5_GROUP_CH
P2 = 2 * S5_STATE
SCAN_LEVELS_MAX = 16


def _s5_prep_kernel(n_levels, arow_ref, acol_ref, bt_ref, ct_ref, d_ref,
                    mt_ref, wt_ref, vt_ref, apow_ref, dvec_ref, mt_acc):
    P, H, L = S5_STATE, S5_GROUP_CH, S5_CHUNK

    def powers(ar, ai, dt, n):
        e = jnp.exp(n * (ar * dt))
        ang = n * (ai * dt)
        return e * jnp.cos(ang), e * jnp.sin(ang)

    def zoh_coef(ar, ai, dt):
        pr, pi = powers(ar, ai, dt, 1.0)
        nr, ni = pr - 1.0, pi
        den = ar * ar + ai * ai
        return (nr * ar + ni * ai) / den, (ni * ar - nr * ai) / den

    row = arow_ref[0]
    ar, ai, dt = row[0:1], row[1:2], jnp.exp(row[2:3])
    cr, ci = zoh_coef(ar, ai, dt)
    bt = bt_ref[0]
    btr = jnp.concatenate([bt[:H], bt[:H]], axis=1)
    bti = jnp.concatenate([bt[H:], bt[H:]], axis=1)
    bbr = cr * btr - ci * bti
    bbi = cr * bti + ci * btr
    lane = lax.broadcasted_iota(I32, (1, P2), 1)
    is_re = lane < P
    ridx = lax.broadcasted_iota(I32, (CH2, P2), 0) // H
    pwr, pwi = powers(ar, ai, dt, (L - 1 - ridx).astype(F32))
    rsel = (lax.broadcasted_iota(I32, (CH2, H), 0) % H == lax.broadcasted_iota(I32, (CH2, H), 1)).astype(F32)
    bbr_t = _dot_exact(rsel, bbr)
    bbi_t = _dot_exact(rsel, bbi)
    w_re = pwr * bbr_t - pwi * bbi_t
    w_im = pwr * bbi_t + pwi * bbr_t
    wt_ref[0] = jnp.where(is_re, w_re, w_im).astype(BF16)
    a_r, a_i = powers(ar, ai, dt, float(L))
    sign = jnp.where(is_re, -1.0, 1.0)
    for lvl in range(SCAN_LEVELS_MAX):
        if lvl < n_levels:
            apow_ref[0, lvl:lvl + 1, :] = a_r
            apow_ref[0, SCAN_LEVELS_MAX + lvl:SCAN_LEVELS_MAX + lvl + 1, :] = sign * a_i
            a_r, a_i = a_r * a_r - a_i * a_i, 2.0 * a_r * a_i
        else:
            apow_ref[0, lvl:lvl + 1, :] = jnp.zeros_like(a_r)
            apow_ref[0, SCAN_LEVELS_MAX + lvl:SCAN_LEVELS_MAX + lvl + 1, :] = jnp.zeros_like(a_r)

    col = acol_ref[0]
    arc, aic, dtc = col[:, 0:1], col[:, 1:2], jnp.exp(col[:, 2:3])
    ct = ct_ref[0]
    csel = (lax.broadcasted_iota(I32, (H, CH2), 1) % H == lax.broadcasted_iota(I32, (H, CH2), 0)).astype(F32)
    ctr = _dot_exact(ct[:, :H], csel)
    cti = _dot_exact(ct[:, H:], csel)
    sidx = (lax.broadcasted_iota(I32, (P, CH2), 1) // H).astype(F32)
    pr, pi = powers(arc, aic, dtc, sidx)
    g_r = ctr * pr - cti * pi
    g_i = ctr * pi + cti * pr
    a1r, a1i = powers(arc, aic, dtc, 1.0)
    v_r = g_r * a1r - g_i * a1i
    v_i = g_r * a1i + g_i * a1r
    vt_ref[0, :P, :] = v_r.astype(BF16)
    vt_ref[0, P:, :] = (-v_i).astype(BF16)
    bbr64, bbi64 = bbr[:, :P], bbi[:, :P]
    kt = _dot_exact(bbr64, g_r) - _dot_exact(bbi64, g_i)
    lane2 = lax.broadcasted_iota(I32, (H, CH2), 1)
    for r in range(L):
        blk = kt if r == 0 else jnp.where(lane2 >= r * H, pltpu.roll(kt, r * H, axis=1), 0.0)
        mt_acc[r * H:(r + 1) * H, :] = blk
    mt_ref[0] = mt_acc[...].astype(BF16)
    dvec_ref[0] = _dot_exact(d_ref[0], csel)


def _s5_prep(n_levels, s5_a_re, s5_a_im, s5_log_dt, s5_b_re, s5_b_im, s5_c_re, s5_c_im, s5_d):
    G, P, H = S5_GROUPS, S5_STATE, S5_GROUP_CH
    ldt = jnp.broadcast_to(s5_log_dt[:, None], (G, P))
    arow = jnp.stack([s5_a_re, s5_a_im, ldt], axis=1)
    arow = jnp.concatenate([arow, arow], axis=2)
    acol = jnp.stack([s5_a_re, s5_a_im, ldt], axis=2)
    bt = jnp.concatenate([s5_b_re.transpose(0, 2, 1), s5_b_im.transpose(0, 2, 1)], axis=1)
    ct = jnp.concatenate([s5_c_re.transpose(0, 2, 1), s5_c_im.transpose(0, 2, 1)], axis=2)
    d = s5_d.reshape(G, 1, H)
    blk = lambda *s: pl.BlockSpec((1,) + s, lambda g: (g, 0, 0))
    return pl.pallas_call(
        functools.partial(_s5_prep_kernel, n_levels),
        grid=(G,),
        in_specs=[blk(3, P2), blk(P, 3), blk(2 * H, P), blk(P, 2 * H), blk(1, H)],
        out_specs=[blk(CH2, CH2), blk(CH2, P2), blk(P2, CH2), blk(2 * SCAN_LEVELS_MAX, P2), blk(1, CH2)],
        out_shape=[jax.ShapeDtypeStruct((G, CH2, CH2), BF16),
                   jax.ShapeDtypeStruct((G, CH2, P2), BF16),
                   jax.ShapeDtypeStruct((G, P2, CH2), BF16),
                   jax.ShapeDtypeStruct((G, 2 * SCAN_LEVELS_MAX, P2), F32),
                   jax.ShapeDtypeStruct((G, 1, CH2), F32)],
        scratch_shapes=[pltpu.VMEM((CH2, CH2), F32)],
        compiler_params=_cparams("parallel"),
        name="s5_discretise",
    )(arow, acol, bt, ct, d)


def _gelu_tanh(y):
    return 0.5 * y * (1.0 + jnp.tanh(math.sqrt(2.0 / math.pi) * (y + 0.044715 * (y * y * y))))


GROUPS_PER_TILE = LANES // S5_GROUP_CH
STEPS_PER_TILE = LANES // S5_GROUP_CH


def _s5_scan_kernel(n_levels, u_ref, mt_ref, wt_ref, vt_ref, apow_ref, dvec_ref, y_ref):
    H, L = S5_GROUP_CH, S5_CHUNK
    nc = u_ref.shape[0] // L
    lane_blk = lax.broadcasted_iota(I32, (nc, LANES), 1) // H
    ridx = lax.broadcasted_iota(I32, (nc, P2), 0)
    steps = [u_ref[pl.ds(l, nc, stride=L), :] for l in range(L)]
    groups = range(GROUPS_PER_TILE)
    us, xs = [], []
    for gi in groups:
        halves = []
        for hh in range(L // STEPS_PER_TILE):
            acc = jnp.zeros((nc, LANES), F32)
            for l8 in range(STEPS_PER_TILE):
                src = steps[hh * STEPS_PER_TILE + l8]
                shift = ((l8 - gi) % STEPS_PER_TILE) * H
                moved = src if shift == 0 else pltpu.roll(src, shift, axis=1)
                acc = jnp.where(lane_blk == l8, moved, acc)
            halves.append(acc)
        u = jnp.concatenate(halves, axis=1)
        us.append(u)
        xs.append(_dot(u.astype(BF16), wt_ref[gi]))
    for lvl in range(n_levels):
        sh = 1 << lvl
        for gi in groups:
            apow = apow_ref[gi]
            prev = jnp.where(ridx >= sh, pltpu.roll(xs[gi], sh, axis=0), 0.0)
            a_r = apow[lvl:lvl + 1]
            a_i = apow[SCAN_LEVELS_MAX + lvl:SCAN_LEVELS_MAX + lvl + 1]
            xs[gi] = xs[gi] + a_r * prev + a_i * pltpu.roll(prev, S5_STATE, axis=1)
    outs = [jnp.zeros((nc, LANES), F32) for _ in range(L)]
    for gi in groups:
        x_in = jnp.where(ridx >= 1, pltpu.roll(xs[gi], 1, axis=0), 0.0)
        u = us[gi]
        y = _gelu_tanh(_dot(u.astype(BF16), mt_ref[gi]) + _dot(x_in.astype(BF16), vt_ref[gi]) + dvec_ref[gi] * u)
        for l in range(L):
            hh, l8 = divmod(l, STEPS_PER_TILE)
            src = y[:, hh * LANES:(hh + 1) * LANES]
            shift = ((gi - l8) % STEPS_PER_TILE) * H
            moved = src if shift == 0 else pltpu.roll(src, shift, axis=1)
            outs[l] = jnp.where(lane_blk == gi, moved, outs[l])
    for l in range(L):
        y_ref[pl.ds(l, nc, stride=L), :] = outs[l]


def _s5_branch(u, B, S, s5_params):
    G, L = S5_GROUPS, S5_CHUNK
    nc = S // L
    n_levels = max(1, (nc - 1).bit_length())
    assert n_levels <= SCAN_LEVELS_MAX and S5_CHUNK % STEPS_PER_TILE == 0
    mt, wt, vt, apow, dvec = _s5_prep(n_levels, *s5_params)
    n_lt = G // GROUPS_PER_TILE
    blk = lambda *s: pl.BlockSpec((GROUPS_PER_TILE,) + s, lambda b, t: (t, 0, 0))
    io = pl.BlockSpec((S, LANES), lambda b, t: (b, t))
    return pl.pallas_call(
        functools.partial(_s5_scan_kernel, n_levels),
        grid=(B, n_lt),
        in_specs=[io, blk(CH2, CH2), blk(CH2, P2), blk(P2, CH2), blk(2 * SCAN_LEVELS_MAX, P2), blk(1, CH2)],
        out_specs=io,
        out_shape=jax.ShapeDtypeStruct((B * S, S5_WIDTH), F32),
        compiler_params=_cparams("parallel", "parallel"),
        name="s5_chunk_scan",
    )(u, mt, wt, vt, apow, dvec)


def _merge_xattn_kernel(x_ref, o_ref, ys_ref, gm_ref, gs_ref, wmo_ref, wglu_ref, wout_ref, g1_ref, b1_ref,
                        k_ref, v_ref, wq_ref, wo_ref, g2_ref, b2_ref, x2_ref):
    y_mla = _dot(o_ref[0], wmo_ref[...])
    z = _dot(ys_ref[0].astype(BF16), wglu_ref[...])
    y_s5 = z[:, :D_MODEL] * jax.nn.sigmoid(z[:, D_MODEL:])
    merged = (jax.nn.sigmoid(gm_ref[0].astype(F32)) * y_mla
              + jax.nn.sigmoid(gs_ref[0].astype(F32)) * y_s5)
    mix = _dot(merged.astype(BF16), wout_ref[...])
    x1 = _layer_norm(DEEPNORM_ALPHA * x_ref[0] + mix, g1_ref[...], b1_ref[...])
    q = (_dot(x1.astype(BF16), wq_ref[...]) * (XATTN_HEAD_DIM ** -0.5)).astype(BF16)
    k = k_ref[0]
    v = v_ref[0]
    outs = []
    for h in range(XATTN_HEADS):
        sl = slice(h * XATTN_HEAD_DIM, (h + 1) * XATTN_HEAD_DIM)
        s = _dot_nt(q[:, sl], k[:, sl])
        p = jnp.exp(s - jnp.max(s, axis=-1, keepdims=True))
        o = _dot(p.astype(BF16), v[:, sl]) / jnp.sum(p, axis=-1, keepdims=True)
        outs.append(o.astype(BF16))
    xa = _dot(jnp.concatenate(outs, axis=1), wo_ref[...])
    x2_ref[0] = _layer_norm(DEEPNORM_ALPHA * x1 + xa, g2_ref[...], b2_ref[...])


def _memkv_kernel(mem_ref, g_ref, b_ref, w_ref, k_ref, v_ref):
    m = _layer_norm(mem_ref[0], g_ref[...], b_ref[...]).astype(BF16)
    kv = _dot(m, w_ref[...])
    hd = XATTN_HEADS * XATTN_HEAD_DIM
    k_ref[0] = kv[:, :hd].astype(BF16)
    v_ref[0] = kv[:, hd:].astype(BF16)


def _merge_cross_attention(x2d, o, ys, gm, gs, mem, B, S, w_mla_o, w_s5_glu, w_out, ln1_g, ln1_b,
                           mem_ln_g, mem_ln_b, w_xq, w_xkv, w_xo, ln2_g, ln2_b):
    M = mem.shape[1]
    hd = XATTN_HEADS * XATTN_HEAD_DIM
    HP = MLA_HEADS * HEAD_PAD
    row = lambda a: a.reshape(1, -1)
    wmo = jnp.pad(w_mla_o.reshape(MLA_HEADS, V_HEAD, D_MODEL), ((0, 0), (0, HEAD_PAD - V_HEAD), (0, 0)))
    wmo = wmo.reshape(HP, D_MODEL).astype(BF16)
    k, v = pl.pallas_call(
        _memkv_kernel,
        grid=(B,),
        in_specs=[pl.BlockSpec((1, M, D_MODEL), lambda b: (b, 0, 0)), _full((1, D_MODEL)), _full((1, D_MODEL)),
                  _full((D_MODEL, 2 * hd))],
        out_specs=[pl.BlockSpec((1, M, hd), lambda b: (b, 0, 0))] * 2,
        out_shape=[jax.ShapeDtypeStruct((B, M, hd), BF16)] * 2,
        compiler_params=_cparams("parallel"),
        name="memory_kv",
    )(mem, row(mem_ln_g), row(mem_ln_b), w_xkv.astype(BF16))
    tile = lambda w: pl.BlockSpec((1, TM, w), lambda b, i: (b, i, 0))
    kv_spec = pl.BlockSpec((1, M, hd), lambda b, i: (b, 0, 0))
    r3 = lambda a: a.reshape(B, S, a.shape[-1])
    x2 = pl.pallas_call(
        _merge_xattn_kernel,
        grid=(B, S // TM),
        in_specs=[tile(D_MODEL), tile(HP), tile(S5_WIDTH), tile(D_MODEL), tile(D_MODEL),
                  _full((HP, D_MODEL)), _full((S5_WIDTH, 2 * D_MODEL)), _full((D_MODEL, D_MODEL)),
                  _full((1, D_MODEL)), _full((1, D_MODEL)),
                  kv_spec, kv_spec, _full((D_MODEL, hd)), _full((hd, D_MODEL)), _full((1, D_MODEL)), _full((1, D_MODEL))],
        out_specs=tile(D_MODEL),
        out_shape=jax.ShapeDtypeStruct((B, S, D_MODEL), F32),
        compiler_params=_cparams("parallel", "parallel"),
        name="merge_ln1_xattn_ln2",
    )(r3(x2d), r3(o), r3(ys), r3(gm), r3(gs), wmo, w_s5_glu.astype(BF16), w_out.astype(BF16), row(ln1_g), row(ln1_b),
      k, v, w_xq.astype(BF16), w_xo.astype(BF16), row(ln2_g), row(ln2_b))
    return x2.reshape(B * S, D_MODEL)


def _route_sort_kernel(x_ref, w_ref, bias_ref, xs_ref, pos_ref, cnt_ref, off_ref):
    E, per = N_EXPERTS, N_EXPERTS // N_EXPERT_GROUPS
    tm = x_ref.shape[0]
    x = x_ref[...]
    xb = x.astype(BF16)
    x_lo = (x - xb.astype(F32)).astype(BF16)
    w = w_ref[...]
    wb = w.astype(BF16)
    w_lo = (w - wb.astype(F32)).astype(BF16)
    logits = _dot_nt(wb, xb) + (_dot_nt(wb, x_lo) + _dot_nt(w_lo, xb))
    scores = jax.nn.sigmoid(logits)
    sel = scores + bias_ref[...]
    neg = -jnp.inf
    i8 = lax.broadcasted_iota(I32, (per, tm), 0)
    gscore = []
    for g in range(N_EXPERT_GROUPS):
        blk = sel[g * per:(g + 1) * per]
        m1 = jnp.max(blk, axis=0, keepdims=True)
        i1 = jnp.min(jnp.where(blk == m1, i8, per), axis=0, keepdims=True)
        m2 = jnp.max(jnp.where(i8 == i1, neg, blk), axis=0, keepdims=True)
        gscore.append(m1 + m2)
    blocks = []
    for g in range(N_EXPERT_GROUPS):
        ahead = jnp.zeros((1, tm), I32)
        for o in range(N_EXPERT_GROUPS):
            if o == g:
                continue
            before = (gscore[o] >= gscore[g]) if o < g else (gscore[o] > gscore[g])
            ahead = ahead + before.astype(I32)
        blocks.append(jnp.where(ahead < TOPK_GROUPS, sel[g * per:(g + 1) * per], neg))
    cur = jnp.concatenate(blocks, axis=0)
    ie = lax.broadcasted_iota(I32, (E, tm), 0)
    picked = jnp.zeros((E, tm), F32)
    idxs = []
    for _ in range(TOP_K):
        m = jnp.max(cur, axis=0, keepdims=True)
        idx = jnp.min(jnp.where(cur == m, ie, E), axis=0, keepdims=True)
        hit = ie == idx
        picked = jnp.where(hit, 1.0, picked)
        cur = jnp.where(hit, neg, cur)
        idxs.append(idx)
    wsel = scores * picked
    wnorm = wsel / jnp.sum(wsel, axis=0, keepdims=True) * ROUTED_SCALE
    pb = picked.astype(BF16)
    tri = (lax.broadcasted_iota(I32, (tm, tm), 0) <= lax.broadcasted_iota(I32, (tm, tm), 1)).astype(BF16)
    incl = _dot(pb, tri)
    cnt_col = jnp.sum(picked, axis=1, keepdims=True)
    lower = (lax.broadcasted_iota(I32, (E, E), 1) < lax.broadcasted_iota(I32, (E, E), 0)).astype(F32)
    al_col = jnp.floor((cnt_col + (SEG_ALIGN - 1.0)) * (1.0 / SEG_ALIGN)) * SEG_ALIGN
    off_col = _dot_exact(lower, jnp.broadcast_to(al_col, (E, LANES)))[:, 0:1]
    pos = off_col + incl - 1.0
    q = jnp.floor(pos * (1.0 / POS_RADIX))
    pq = jnp.concatenate([jnp.where(picked > 0.0, POS_RADIX * q, POS_RADIX * POS_NONE_Q),
                          jnp.where(picked > 0.0, pos - POS_RADIX * q, 0.0)], axis=0)
    pos_ref[...] = pq
    pqb = pq.astype(BF16)
    cnt_ref[0] = jnp.broadcast_to(cnt_col, (E, LANES)).astype(I32)
    off_ref[0] = jnp.broadcast_to(off_col, (E, LANES)).astype(I32)
    cnt_row = _dot_nt(jnp.ones((8, tm), BF16), pb)
    al_row = jnp.floor((cnt_row + (SEG_ALIGN - 1.0)) * (1.0 / SEG_ALIGN)) * SEG_ALIGN
    er = lax.broadcasted_iota(I32, (E, 4 * E), 0)
    ec = lax.broadcasted_iota(I32, (E, 4 * E), 1) % E
    off_row4 = _dot_exact(al_row, (er < ec).astype(F32))[0:1]
    end_row4 = off_row4 + _dot_exact(al_row, (er == ec).astype(F32))[0:1]
    w_hi = wnorm.astype(BF16).astype(F32)
    w_r1 = wnorm - w_hi
    w_mid = w_r1.astype(BF16).astype(F32)
    w_lo = w_r1 - w_mid
    w4_t = jnp.concatenate([w_hi, w_mid, w_lo, jnp.zeros_like(w_lo)], axis=0).T
    xw = jnp.concatenate([xb, w4_t.astype(BF16)], axis=1)
    half = D_MODEL // 2

    def build(c, carry):
        r0 = pl.multiple_of(c * OH_ROWS, OH_ROWS)
        jr = (lax.broadcasted_iota(I32, (OH_ROWS, 4 * E), 0) + r0).astype(F32)
        member4 = jnp.where(jr >= off_row4, jnp.where(jr < end_row4, 1.0, 0.0), 0.0)
        target = _dot(member4[:, :2 * E].astype(BF16), pqb)
        j = (lax.broadcasted_iota(I32, (OH_ROWS, tm), 0) + r0).astype(F32)
        ohb = jnp.where(target == j, 1.0, 0.0).astype(BF16)
        rows = _dot(ohb, xw)
        xs_ref[0, pl.ds(r0, OH_ROWS), 0:half] = _pack_bf16_pair(rows[:, :half], rows[:, half:D_MODEL])
        wrow = jnp.sum(member4 * rows[:, D_MODEL:], axis=1, keepdims=True)
        xs_ref[0, pl.ds(r0, OH_ROWS), half:] = pltpu.bitcast(jnp.broadcast_to(wrow, (OH_ROWS, LANES)), I32)
        return carry

    lax.fori_loop(0, SORT_ROWS // OH_ROWS, build, 0)
    xs_ref[0, SORT_ROWS:, :] = jnp.zeros((PAD_ROWS, XW), I32)


def _route_sort(x2, w_router, router_bias):
    T = x2.shape[0]
    E = N_EXPERTS
    nt = T // TW
    seg = pl.BlockSpec((1, E, LANES), lambda i: (i, 0, 0))
    return pl.pallas_call(
        _route_sort_kernel,
        grid=(nt,),
        in_specs=[pl.BlockSpec((TW, D_MODEL), lambda i: (i, 0)), _full((E, D_MODEL)), _full((E, 1))],
        out_specs=[pl.BlockSpec((1, TS, XW), lambda i: (i, 0, 0)),
                   pl.BlockSpec((2 * E, TW), lambda i: (0, i)), seg, seg],
        out_shape=[jax.ShapeDtypeStruct((nt, TS, XW), I32), jax.ShapeDtypeStruct((2 * E, T), F32),
                   jax.ShapeDtypeStruct((nt, E, LANES), I32), jax.ShapeDtypeStruct((nt, E, LANES), I32)],
        compiler_params=_cparams("parallel"),
        name="route_local_sort",
    )(x2, w_router.T, router_bias.reshape(E, 1))


def _plan_kernel(cnt_ref, off_ref, pa_ref, ce_ref, nch_ref):
    E, nt = cnt_ref.shape
    n_rows = pa_ref.shape[0]
    cnt = cnt_ref[...].astype(F32)
    off = off_ref[...].astype(F32)
    npc = jnp.floor((cnt + (PR - 1.0)) * (1.0 / PR))
    tri = (lax.broadcasted_iota(I32, (nt, nt), 0) <= lax.broadcasted_iota(I32, (nt, nt), 1)).astype(F32)
    p_incl = _dot_exact(npc, tri)
    total = p_incl[:, nt - 1:nt]
    n_ch = jnp.floor((total + (PPC - 1.0)) * (1.0 / PPC))
    lower = (lax.broadcasted_iota(I32, (E, E), 1) < lax.broadcasted_iota(I32, (E, E), 0)).astype(F32)
    wide = lambda col: jnp.broadcast_to(col, (E, LANES))
    g0 = _dot_exact(lower, wide(n_ch))[:, 0:1]
    n_total = jnp.sum(n_ch, axis=0, keepdims=True)
    g0_row = wide(g0).T[0:1, :E]
    g1_row = g0_row + wide(n_ch).T[0:1, :E]
    g_e = lax.broadcasted_iota(I32, (n_rows, E), 0).astype(F32)
    sel = jnp.where(g_e >= g0_row, jnp.where(g_e < g1_row, 1.0, 0.0), 0.0)
    tile_base = lax.broadcasted_iota(I32, (E, nt), 1).astype(F32) * TS
    a_mat = tile_base + off - PR * (p_incl - npc)
    p_g = _dot_exact(sel, p_incl)
    a_g = _dot_exact(sel, a_mat)
    e_idx = lax.broadcasted_iota(I32, (E, LANES), 0).astype(F32)
    pick = lambda col: _dot_exact(sel, col)[:, 0:1]
    g0_g, tot_g, e_g = pick(wide(g0)), pick(wide(total)), pick(e_idx)
    g_p = lax.broadcasted_iota(I32, (n_rows, PPC), 0).astype(F32)
    q = (g_p - g0_g) * PPC + lax.broadcasted_iota(I32, (n_rows, PPC), 1).astype(F32)
    addr = jnp.full((n_rows, PPC), -float(SEG_ALIGN), F32)
    lo = jnp.zeros((n_rows, 1), F32)
    for i in range(nt):
        hi = p_g[:, i:i + 1]
        addr = jnp.where(q >= lo, jnp.where(q < hi, a_g[:, i:i + 1] + PR * q, addr), addr)
        lo = hi
    pa_ref[...] = addr.astype(I32)
    g_col = lax.broadcasted_iota(I32, (n_rows, LANES), 0).astype(F32)
    ce_ref[...] = jnp.where(g_col < n_total, jnp.broadcast_to(e_g, (n_rows, LANES)), E - 1.0).astype(I32)
    nch_ref[...] = jnp.broadcast_to(n_total, nch_ref.shape).astype(I32)


def _plan(cnt, off, n_chunks_max):
    n_rows = (n_chunks_max + 7) // 8 * 8
    pa, ce, nch = pl.pallas_call(
        _plan_kernel,
        out_shape=[jax.ShapeDtypeStruct((n_rows, PPC), I32), jax.ShapeDtypeStruct((n_rows, LANES), I32),
                   jax.ShapeDtypeStruct((8, LANES), I32)],
        name="moe_plan",
    )(cnt, off)
    return pa.reshape(n_rows * PPC), ce[:, 0], nch[0, :1]


def _unpack_bf16_pair(w):
    lo = pltpu.bitcast(lax.shift_left(w, jnp.int32(16)), F32).astype(BF16)
    hi = pltpu.bitcast(w & jnp.int32(-65536), F32).astype(BF16)
    return lo, hi


def _pack_bf16_pair(lo, hi):
    lo_bits = lax.shift_right_logical(pltpu.bitcast(lo.astype(BF16).astype(F32), I32), jnp.int32(16))
    hi_bits = pltpu.bitcast(hi.astype(BF16).astype(F32), I32)
    return hi_bits | lo_bits


def _expert_kernel(ce_ref, pa_ref, nch_ref, tot_ref, xs_hbm, wgu_ref, wdn_ref, y_hbm,
                   xbuf, ybuf, zbuf, wgu_b, wdn_b, gsem, wsem, zsem):
    g = pl.program_id(0)
    n = nch_ref[0]
    slot = g % N_SLOTS
    half = D_MODEL // 2

    dummy_base = (tot_ref.shape[0] - 1) * TS

    def gather_copy(s, a, jp):
        return pltpu.make_async_copy(xs_hbm.at[pl.ds(pl.multiple_of(a, SEG_ALIGN), PR), :],
                                     xbuf.at[s, pl.ds(jp * PR, PR), :], gsem.at[s])

    def write_copy(s, a, jp):
        return pltpu.make_async_copy(ybuf.at[s, pl.ds(jp * PR, PR), :],
                                     y_hbm.at[pl.ds(pl.multiple_of(a, SEG_ALIGN), PR), :], wsem.at[s])

    def gather_start(c, s):
        for jp in range(PPC):
            gather_copy(s, jnp.maximum(pa_ref[c * PPC + jp], 0), jp).start()

    def write_start(c, s):
        for jp in range(PPC):
            a = pa_ref[c * PPC + jp]
            write_copy(s, jnp.where(a >= 0, a, dummy_base + jp * PR), jp).start()

    def gather_wait(s):
        for jp in range(PPC):
            gather_copy(s, 0, jp).wait()

    def write_wait(s):
        for jp in range(PPC):
            write_copy(s, 0, jp).wait()

    @pl.when(g == 0)
    def _():
        gather_start(0, 0)
        ybuf[...] = jnp.zeros_like(ybuf)
        zbuf[...] = jnp.zeros_like(zbuf)

        def zero_copy(row):
            return pltpu.make_async_copy(zbuf, y_hbm.at[pl.ds(pl.multiple_of(row, SEG_ALIGN), SEG_ALIGN), :], zsem)

        def tail_of(i):
            return i * TS + tot_ref[i], (TS - tot_ref[i]) // SEG_ALIGN

        def zero_start(i, c):
            base, cnt8 = tail_of(i)
            lax.fori_loop(0, cnt8, lambda q, z: (zero_copy(base + q * SEG_ALIGN).start(), z)[1], 0)
            return c

        def zero_wait(i, c):
            base, cnt8 = tail_of(i)
            lax.fori_loop(0, cnt8, lambda q, z: (zero_copy(base + q * SEG_ALIGN).wait(), z)[1], 0)
            return c

        lax.fori_loop(0, tot_ref.shape[0], zero_start, 0)
        lax.fori_loop(0, tot_ref.shape[0], zero_wait, 0)

    @pl.when((g == 0) & (n > 1))
    def _():
        gather_start(1, 1)

    @pl.when(g + GATHER_AHEAD < n)
    def _():
        gather_start(g + GATHER_AHEAD, (g + GATHER_AHEAD) % N_SLOTS)

    e_cur = ce_ref[g]
    e_prev = ce_ref[jnp.maximum(g - 1, 0)]
    e_prev2 = ce_ref[jnp.maximum(g - 2, 0)]

    @pl.when(g < n)
    def _():
        @pl.when((g == 0) | (e_cur != e_prev))
        def _():
            wgu_b[...] = wgu_ref[0].astype(BF16)
            wdn_b[...] = wdn_ref[0].astype(BF16)

        gather_wait(slot)

        @pl.when((g >= 2) & (e_prev == e_prev2))
        def _():
            write_wait((g - 2) % N_SLOTS)

        @pl.when((g >= 1) & (e_cur != e_prev))
        def _():
            write_wait((g - 1) % N_SLOTS)

        def swiglu_rows(rows):
            xw = xbuf[slot, :rows, :]
            lo, hi = _unpack_bf16_pair(xw[:, :half])
            h = _dot(jnp.concatenate([lo, hi], axis=1), wgu_b[...])
            gate, up = h[:, :EXPERT_FF], h[:, EXPERT_FF:]
            act = (gate * jax.nn.sigmoid(gate) * up).astype(BF16)
            y = _dot(act, wdn_b[...])
            w_row = pltpu.bitcast(xw[:, half:], F32)
            y = y * jnp.concatenate([w_row] * (D_MODEL // LANES), axis=1)
            ybuf[slot, :rows, :] = _pack_bf16_pair(y[:, :half], y[:, half:])

        second_half_used = pa_ref[g * PPC + PPC // 2] >= 0

        @pl.when(second_half_used)
        def _():
            swiglu_rows(CR)

        @pl.when(jnp.logical_not(second_half_used))
        def _():
            swiglu_rows(CR // 2)

        write_start(g, slot)

    @pl.when(g == pl.num_programs(0) - 1)
    def _():
        @pl.when((n >= 2) & (ce_ref[jnp.maximum(n - 1, 0)] == ce_ref[jnp.maximum(n - 2, 0)]))
        def _():
            write_wait((n - 2) % N_SLOTS)

        write_wait((n - 1) % N_SLOTS)


def _experts(ce, pa, nch, tot, xs, w_exp_gu, w_exp_down):
    n_rows = xs.shape[0]
    half = D_MODEL // 2
    grid_spec = pltpu.PrefetchScalarGridSpec(
        num_scalar_prefetch=4,
        grid=(ce.shape[0],),
        in_specs=[pl.BlockSpec(memory_space=pl.ANY),
                  pl.BlockSpec((1, D_MODEL, 2 * EXPERT_FF), lambda g, ce, pa, nch, tot: (ce[g], 0, 0)),
                  pl.BlockSpec((1, EXPERT_FF, D_MODEL), lambda g, ce, pa, nch, tot: (ce[g], 0, 0))],
        out_specs=pl.BlockSpec(memory_space=pl.ANY),
        scratch_shapes=[pltpu.VMEM((N_SLOTS, CR, XW), I32), pltpu.VMEM((N_SLOTS, CR, half), I32),
                        pltpu.VMEM((SEG_ALIGN, half), I32),
                        pltpu.VMEM((D_MODEL, 2 * EXPERT_FF), BF16), pltpu.VMEM((EXPERT_FF, D_MODEL), BF16),
                        pltpu.SemaphoreType.DMA((N_SLOTS,)), pltpu.SemaphoreType.DMA((N_SLOTS,)),
                        pltpu.SemaphoreType.DMA(())],
    )
    return pl.pallas_call(
        _expert_kernel,
        grid_spec=grid_spec,
        out_shape=jax.ShapeDtypeStruct((n_rows + TS, half), I32),
        compiler_params=_cparams("arbitrary"),
        name="moe_grouped_swiglu",
    )(ce, pa, nch, tot, xs, w_exp_gu, w_exp_down)


def _combine_kernel(tot_ref, y_ref, pos_ref, cnt_ref, off_ref, x2_ref, wsg_ref, wsd_ref, g_ref, b_ref, out_ref,
                    acc_lo, acc_hi):
    tm = x2_ref.shape[0]
    E = N_EXPERTS
    n_sorted = tot_ref[pl.program_id(0)]
    x2 = x2_ref[...]
    h = _dot(x2.astype(BF16), wsg_ref[...])
    gate, up = h[:, :SHARED_FF], h[:, SHARED_FF:]
    shared = _dot((gate * jax.nn.sigmoid(gate) * up).astype(BF16), wsd_ref[...])
    pq_t = pos_ref[...].T.astype(BF16)
    off_col = off_ref[0][:, 0:1].astype(F32)
    cnt_col = cnt_ref[0][:, 0:1].astype(F32)
    end_col = off_col + jnp.floor((cnt_col + (SEG_ALIGN - 1.0)) * (1.0 / SEG_ALIGN)) * SEG_ALIGN
    off_col2 = jnp.concatenate([off_col, off_col], axis=0)
    end_col2 = jnp.concatenate([end_col, end_col], axis=0)
    acc_lo[...] = jnp.zeros_like(acc_lo)
    acc_hi[...] = jnp.zeros_like(acc_hi)

    def gather_back(c, carry):
        r0 = pl.multiple_of(c * OH_ROWS, OH_ROWS)
        jc = (lax.broadcasted_iota(I32, (2 * E, OH_ROWS), 1) + r0).astype(F32)
        member2 = jnp.where(jc >= off_col2, jnp.where(jc < end_col2, 1.0, 0.0), 0.0)
        target = _dot(pq_t, member2.astype(BF16))
        j = (lax.broadcasted_iota(I32, (tm, OH_ROWS), 1) + r0).astype(F32)
        ohb = jnp.where(target == j, 1.0, 0.0).astype(BF16)
        yw = y_ref[0, pl.ds(r0, OH_ROWS), :]
        row = lax.broadcasted_iota(I32, yw.shape, 0) + r0
        lo, hi = _unpack_bf16_pair(jnp.where(row < n_sorted, yw, 0))
        acc_lo[...] += _dot(ohb, lo)
        acc_hi[...] += _dot(ohb, hi)
        return carry

    lax.fori_loop(0, SORT_ROWS // OH_ROWS, gather_back, 0)
    ff = shared + jnp.concatenate([acc_lo[...], acc_hi[...]], axis=1)
    out_ref[...] = _layer_norm(DEEPNORM_ALPHA * x2 + ff, g_ref[...], b_ref[...])


def _combine(tot, y, pos, cnt, off, x2, w_sh_gu, w_sh_down, ln_g, ln_b):
    T = x2.shape[0]
    nt = T // TW
    half = D_MODEL // 2
    tile = lambda w: pl.BlockSpec((TW, w), lambda i: (i, 0))
    return pl.pallas_call(
        _combine_kernel,
        grid=(nt,),
        in_specs=[pl.BlockSpec(memory_space=pltpu.SMEM),
                  pl.BlockSpec((1, TS, half), lambda i: (i, 0, 0)),
                  pl.BlockSpec((2 * N_EXPERTS, TW), lambda i: (0, i)),
                  pl.BlockSpec((1, N_EXPERTS, LANES), lambda i: (i, 0, 0)),
                  pl.BlockSpec((1, N_EXPERTS, LANES), lambda i: (i, 0, 0)), tile(D_MODEL),
                  _full((D_MODEL, 2 * SHARED_FF)), _full((SHARED_FF, D_MODEL)),
                  _full((1, D_MODEL)), _full((1, D_MODEL))],
        out_specs=tile(D_MODEL),
        out_shape=jax.ShapeDtypeStruct((T, D_MODEL), F32),
        scratch_shapes=[pltpu.VMEM((TW, half), F32), pltpu.VMEM((TW, half), F32)],
        compiler_params=_cparams("parallel"),
        name="moe_combine_shared_ln3",
    )(tot, y.reshape(nt + 1, TS, half), pos, cnt, off, x2, w_sh_gu.astype(BF16), w_sh_down.astype(BF16),
      ln_g.reshape(1, -1), ln_b.reshape(1, -1))


def _moe(x2, w_router, router_bias, w_exp_gu, w_exp_down, w_sh_gu, w_sh_down, ln_g, ln_b):
    T = x2.shape[0]
    E = N_EXPERTS
    nt = T // TW
    xs, pos, cnt, off = _route_sort(x2, w_router, router_bias)
    n_pieces_max = (T * TOP_K + nt * E * (PR - 1)) // PR
    n_chunks_max = n_pieces_max // PPC + E
    pa, ce, nch = _plan(cnt[:, :, 0].T, off[:, :, 0].T, n_chunks_max)
    last_al = (cnt[:, E - 1, 0] + SEG_ALIGN - 1) // SEG_ALIGN * SEG_ALIGN
    tot = jnp.concatenate([off[:, E - 1, 0] + last_al, jnp.zeros((1,), I32)])
    y = _experts(ce, pa, nch, tot, xs.reshape(nt * TS, XW), w_exp_gu, w_exp_down)
    return _combine(tot, y, pos, cnt, off, x2, w_sh_gu, w_sh_down, ln_g, ln_b)


def kernel(x, mem, positions, w_in, q_norm_g, kv_norm_g, w_uq, w_ukv, w_mla_o, s5_a_re, s5_a_im, s5_log_dt, s5_b_re, s5_b_im, s5_c_re, s5_c_im, s5_d, w_s5_glu, w_out, ln1_g, ln1_b, mem_ln_g, mem_ln_b, w_xq, w_xkv, w_xo, ln2_g, ln2_b, w_router, router_bias, w_exp_gu, w_exp_down, w_sh_gu, w_sh_down, ln3_g, ln3_b):
    B, S, D = x.shape
    assert D == D_MODEL and S % TQ == 0 and (B * S) % TM == 0 and S % S5_CHUNK == 0
    xc = x.reshape(B * S, D)
    for l in range(w_in.shape[0]):
        tabs = _rope_tables(positions)
        q, k, v, u, gm, gs = _inproj(xc, tabs, w_in[l], q_norm_g[l], kv_norm_g[l], w_uq[l], w_ukv[l])
        o = _attention(q, k, v, B, S)
        ys = _s5_branch(u, B, S, (s5_a_re[l], s5_a_im[l], s5_log_dt[l], s5_b_re[l], s5_b_im[l],
                                  s5_c_re[l], s5_c_im[l], s5_d[l]))
        x2 = _merge_cross_attention(xc, o, ys, gm, gs, mem, B, S, w_mla_o[l], w_s5_glu[l], w_out[l],
                                    ln1_g[l], ln1_b[l], mem_ln_g[l], mem_ln_b[l], w_xq[l], w_xkv[l], w_xo[l],
                                    ln2_g[l], ln2_b[l])
        xc = _moe(x2, w_router[l], router_bias[l], w_exp_gu[l], w_exp_down[l], w_sh_gu[l],
                  w_sh_down[l], ln3_g[l], ln3_b[l])
    return xc.reshape(B, S, D)
```

```python
import functools
import math

import jax
import jax.numpy as jnp
from jax import lax
from jax.experimental import pallas as pl
from jax.experimental.pallas import tpu as pltpu

F32 = jnp.float32
BF16 = jnp.bfloat16
I32 = jnp.int32

D_MODEL = 1024
MLA_HEADS = 8
QK_NOPE = 64
QK_ROPE = 32
V_HEAD = 64
Q_LORA = 256
KV_LORA = 256
ROPE_THETA = 10000.0
S5_GROUP_CH = 16
S5_WIDTH = 512
S5_GROUPS = 32
S5_STATE = 64
XATTN_HEADS = 4
XATTN_HEAD_DIM = 128
N_EXPERTS = 64
TOP_K = 8
N_EXPERT_GROUPS = 8
TOPK_GROUPS = 4
EXPERT_FF = 256
SHARED_FF = 256
ROUTED_SCALE = 2.5
LN_EPS = 1e-5
RMS_EPS = 1e-6
DEPTH = 1
DEEPNORM_ALPHA = (2.0 * DEPTH) ** 0.25

LANES = 128
HEAD_PAD = 128
ROPE_LO = QK_NOPE
ROPE_HALF = QK_ROPE // 2

TM = 512
TQ = 1024
KV_GROUPS = (2, 1)
S5_CHUNK = 16
TW = 512
OH_ROWS = 512
PR = 16
PAD_ROWS = PR
SEG_ALIGN = 8
SORT_ROWS = TOP_K * TW + N_EXPERTS * SEG_ALIGN
TS = SORT_ROWS + PAD_ROWS
XW = D_MODEL // 2 + LANES
CR = 512
PPC = CR // PR
GATHER_AHEAD = 2
N_SLOTS = GATHER_AHEAD + 1
POS_RADIX = 64
POS_NONE_Q = 127
assert SORT_ROWS <= POS_RADIX * POS_NONE_Q and SORT_ROWS % OH_ROWS == 0
VMEM_LIMIT = 48 * 1024 * 1024


def _cparams(*sem):
    return pltpu.CompilerParams(dimension_semantics=sem, vmem_limit_bytes=VMEM_LIMIT)


def _dot(a, b):
    return jnp.dot(a, b, preferred_element_type=F32)


def _dot_nt(a, b, precision=None):
    return lax.dot_general(a, b, (((1,), (1,)), ((), ())), preferred_element_type=F32,
                           precision=precision)


def _dot_exact(a, b):
    return jnp.dot(a, b, preferred_element_type=F32, precision=lax.Precision.HIGHEST)


def _layer_norm(h, g, b):
    mu = jnp.mean(h, axis=-1, keepdims=True)
    c = h - mu
    var = jnp.mean(c * c, axis=-1, keepdims=True)
    return c * lax.rsqrt(var + LN_EPS) * g + b


def _rms_norm(h, g):
    return h * lax.rsqrt(jnp.mean(h * h, axis=-1, keepdims=True) + RMS_EPS) * g


def _full(shape):
    n = len(shape)
    return pl.BlockSpec(shape, lambda *_: (0,) * n)


def _rope_angle_kernel(pos_ref, invf_ref, cos_ref, sin_ref):
    ang = pos_ref[...].astype(F32) * invf_ref[...]
    cos_ref[...] = jnp.cos(ang)
    sin_ref[...] = jnp.sin(ang)


def _rope_tables(positions):
    T = positions.size
    rows = T * ROPE_HALF // LANES
    pos_rep = jnp.repeat(positions.reshape(T), ROPE_HALF).reshape(rows, LANES)
    inv_freq = ROPE_THETA ** (-jnp.arange(0, QK_ROPE, 2, dtype=F32) / QK_ROPE)
    invf = jnp.tile(inv_freq, LANES // ROPE_HALF).reshape(1, LANES)
    cos, sin = pl.pallas_call(
        _rope_angle_kernel,
        out_shape=(jax.ShapeDtypeStruct((rows, LANES), F32),) * 2,
        name="rope_angles",
    )(pos_rep, invf)
    cos = cos.reshape(T, ROPE_HALF)
    sin = sin.reshape(T, ROPE_HALF)
    one = jnp.ones((T, ROPE_LO), F32)
    zero_lo = jnp.zeros((T, ROPE_LO), F32)
    zero_h = jnp.zeros((T, ROPE_HALF), F32)
    tail1 = jnp.ones((T, HEAD_PAD - ROPE_LO - QK_ROPE), F32)
    tail0 = jnp.zeros((T, HEAD_PAD - ROPE_LO - QK_ROPE), F32)
    c_tab = jnp.concatenate([one, cos, cos, tail1], axis=1)
    s_up = jnp.concatenate([zero_lo, -sin, zero_h, tail0], axis=1)
    s_dn = jnp.concatenate([zero_lo, zero_h, sin, tail0], axis=1)
    return c_tab, s_up, s_dn


def _rope(x, c_tab, s_up, s_dn):
    return (x * c_tab + pltpu.roll(x, HEAD_PAD - ROPE_HALF, axis=1) * s_up
            + pltpu.roll(x, ROPE_HALF, axis=1) * s_dn)


def _inproj_kernel(x_ref, c_ref, su_ref, sd_ref, wlat_ref, wu_ref, wgm_ref, wgs_ref, qg_ref, kvg_ref,
                   wuq_ref, wuk_ref, wuv_ref, q_ref, k_ref, v_ref, u_ref, gm_ref, gs_ref):
    xb = x_ref[...].astype(BF16)
    lat = _dot(xb, wlat_ref[...])
    qn = _rms_norm(lat[:, :Q_LORA], qg_ref[...]).astype(BF16)
    kvn = _rms_norm(lat[:, Q_LORA:Q_LORA + KV_LORA], kvg_ref[...]).astype(BF16)
    c_tab, s_up, s_dn = c_ref[...], su_ref[...], sd_ref[...]
    k_rope = _rope(lat[:, Q_LORA + KV_LORA:], c_tab, s_up, s_dn)
    q = _dot(qn, wuq_ref[...])
    k = _dot(kvn, wuk_ref[...])
    scale = (QK_NOPE + QK_ROPE) ** -0.5 * math.log2(math.e)
    for h in range(MLA_HEADS):
        sl = slice(h * HEAD_PAD, (h + 1) * HEAD_PAD)
        q_ref[:, sl] = (_rope(q[:, sl], c_tab, s_up, s_dn) * scale).astype(BF16)
        k_ref[:, sl] = (k[:, sl] + k_rope).astype(BF16)
    v = _dot(kvn, wuv_ref[...])
    ones_lane = lax.broadcasted_iota(I32, (1, v.shape[1]), 1) % HEAD_PAD == V_HEAD
    v_ref[...] = jnp.where(ones_lane, 1.0, v).astype(BF16)
    u_ref[...] = _dot(xb, wu_ref[...])
    gm_ref[...] = _dot(xb, wgm_ref[...]).astype(BF16)
    gs_ref[...] = _dot(xb, wgs_ref[...]).astype(BF16)


def _pad_heads(w, head_w, lo_w):
    K = w.shape[0]
    w = w.reshape(K, MLA_HEADS, head_w)[:, :, :lo_w]
    w = jnp.pad(w, ((0, 0), (0, 0), (0, HEAD_PAD - lo_w)))
    return w.reshape(K, MLA_HEADS * HEAD_PAD)


def _inproj(x2d, tabs, w_in, q_norm_g, kv_norm_g, w_uq, w_ukv):
    T = x2d.shape[0]
    o_rope = Q_LORA + KV_LORA
    o_u = o_rope + QK_ROPE
    o_gm = o_u + S5_WIDTH
    o_gs = o_gm + D_MODEL
    w_rope = jnp.pad(w_in[:, o_rope:o_u], ((0, 0), (ROPE_LO, HEAD_PAD - ROPE_LO - QK_ROPE)))
    w_lat = jnp.concatenate([w_in[:, :o_rope], w_rope], axis=1).astype(BF16)
    w_u = w_in[:, o_u:o_gm].astype(BF16)
    w_gm = w_in[:, o_gm:o_gs].astype(BF16)
    w_gs = w_in[:, o_gs:].astype(BF16)
    wuq = _pad_heads(w_uq, QK_NOPE + QK_ROPE, QK_NOPE + QK_ROPE).astype(BF16)
    kv3 = w_ukv.reshape(KV_LORA, MLA_HEADS, QK_NOPE + V_HEAD)
    wuk = _pad_heads(kv3[:, :, :QK_NOPE].reshape(KV_LORA, -1), QK_NOPE, QK_NOPE).astype(BF16)
    wuv = _pad_heads(kv3[:, :, QK_NOPE:].reshape(KV_LORA, -1), V_HEAD, V_HEAD).astype(BF16)
    HP = MLA_HEADS * HEAD_PAD
    tile = lambda w: pl.BlockSpec((TM, w), lambda i: (i, 0))
    return pl.pallas_call(
        _inproj_kernel,
        grid=(T // TM,),
        in_specs=[tile(D_MODEL), tile(HEAD_PAD), tile(HEAD_PAD), tile(HEAD_PAD),
                  _full(w_lat.shape), _full(w_u.shape), _full(w_gm.shape), _full(w_gs.shape),
                  _full((1, Q_LORA)), _full((1, KV_LORA)),
                  _full(wuq.shape), _full(wuk.shape), _full(wuv.shape)],
        out_specs=[tile(HP), tile(HP), tile(HP), tile(S5_WIDTH), tile(D_MODEL), tile(D_MODEL)],
        out_shape=[jax.ShapeDtypeStruct((T, HP), BF16)] * 3
        + [jax.ShapeDtypeStruct((T, S5_WIDTH), F32)]
        + [jax.ShapeDtypeStruct((T, D_MODEL), BF16)] * 2,
        compiler_params=_cparams("parallel"),
        name="inproj_mla_prep",
    )(x2d, *tabs, w_lat, w_u, w_gm, w_gs, q_norm_g.reshape(1, -1), kv_norm_g.reshape(1, -1),
      wuq, wuk, wuv)


def _attn_kernel(q_ref, k_ref, v_ref, o_ref):
    qi = pl.program_id(2)
    q = q_ref[0]

    def update(qq, k, v, carry, row_offset=None):
        m, acc = carry
        s = _dot_nt(qq, k)
        if row_offset is not None:
            row = lax.broadcasted_iota(I32, s.shape, 0) + row_offset
            col = lax.broadcasted_iota(I32, s.shape, 1)
            s = jnp.where(col <= row, s, -jnp.inf)
        m_new = jnp.maximum(m, jnp.max(s, axis=-1, keepdims=True))
        p = jnp.exp2(s - m_new)
        acc = jnp.exp2(m - m_new) * acc + _dot(p.astype(BF16), v)
        return m_new, acc

    def step(blk, n_blk, carry):
        tk = n_blk * TQ
        start = pl.multiple_of(blk * TQ, TQ)
        return update(q, k_ref[0, pl.ds(start, tk), :], v_ref[0, pl.ds(start, tk), :], carry)

    carry = (jnp.full((TQ, 1), -jnp.inf, F32), jnp.zeros((TQ, HEAD_PAD), F32))
    done = 0
    for n_blk in KV_GROUPS:
        base = done
        n_it = (qi - base) // n_blk
        carry = lax.fori_loop(0, n_it, lambda j, c, base=base, n_blk=n_blk: step(base + j * n_blk, n_blk, c), carry)
        done = base + n_it * n_blk
    hq = TQ // 2
    d0 = pl.multiple_of(qi * TQ, TQ)
    m, acc = carry
    _, acc_lo = update(q[:hq], k_ref[0, pl.ds(d0, hq), :], v_ref[0, pl.ds(d0, hq), :], (m[:hq], acc[:hq]), 0)
    _, acc_hi = update(q[hq:], k_ref[0, pl.ds(d0, TQ), :], v_ref[0, pl.ds(d0, TQ), :], (m[hq:], acc[hq:]), hq)
    o_ref[0, :hq, :] = (acc_lo / acc_lo[:, V_HEAD:V_HEAD + 1]).astype(BF16)
    o_ref[0, hq:, :] = (acc_hi / acc_hi[:, V_HEAD:V_HEAD + 1]).astype(BF16)


def _attention(q, k, v, B, S):
    HP = MLA_HEADS * HEAD_PAD
    q, k, v = (a.reshape(B, S, HP) for a in (q, k, v))
    o = pl.pallas_call(
        _attn_kernel,
        grid=(B, MLA_HEADS, S // TQ),
        in_specs=[pl.BlockSpec((1, TQ, HEAD_PAD), lambda b, h, i: (b, i, h)),
                  pl.BlockSpec((1, S, HEAD_PAD), lambda b, h, i: (b, 0, h)),
                  pl.BlockSpec((1, S, HEAD_PAD), lambda b, h, i: (b, 0, h))],
        out_specs=pl.BlockSpec((1, TQ, HEAD_PAD), lambda b, h, i: (b, i, h)),
        out_shape=jax.ShapeDtypeStruct((B, S, HP), BF16),
        compiler_params=_cparams("parallel", "parallel", "arbitrary"),
        name="mla_flash_attention",
    )(q, k, v)
    return o.reshape(B * S, HP)


CH2 = S5_CHUNK * S5_GROUP_CH
P2 = 2 * S5_STATE
SCAN_LEVELS_MAX = 16


def _s5_prep_kernel(n_levels, arow_ref, acol_ref, bt_ref, ct_ref, d_ref,
                    mt_ref, wt_ref, vt_ref, apow_ref, dvec_ref, mt_acc):
    P, H, L = S5_STATE, S5_GROUP_CH, S5_CHUNK

    def powers(ar, ai, dt, n):
        e = jnp.exp(n * (ar * dt))
        ang = n * (ai * dt)
        return e * jnp.cos(ang), e * jnp.sin(ang)

    def zoh_coef(ar, ai, dt):
        pr, pi = powers(ar, ai, dt, 1.0)
        nr, ni = pr - 1.0, pi
        den = ar * ar + ai * ai
        return (nr * ar + ni * ai) / den, (ni * ar - nr * ai) / den

    row = arow_ref[0]
    ar, ai, dt = row[0:1], row[1:2], jnp.exp(row[2:3])
    cr, ci = zoh_coef(ar, ai, dt)
    bt = bt_ref[0]
    btr = jnp.concatenate([bt[:H], bt[:H]], axis=1)
    bti = jnp.concatenate([bt[H:], bt[H:]], axis=1)
    bbr = cr * btr - ci * bti
    bbi = cr * bti + ci * btr
    lane = lax.broadcasted_iota(I32, (1, P2), 1)
    is_re = lane < P
    ridx = lax.broadcasted_iota(I32, (CH2, P2), 0) // H
    pwr, pwi = powers(ar, ai, dt, (L - 1 - ridx).astype(F32))
    rsel = (lax.broadcasted_iota(I32, (CH2, H), 0) % H == lax.broadcasted_iota(I32, (CH2, H), 1)).astype(F32)
    bbr_t = _dot_exact(rsel, bbr)
    bbi_t = _dot_exact(rsel, bbi)
    w_re = pwr * bbr_t - pwi * bbi_t
    w_im = pwr * bbi_t + pwi * bbr_t
    wt_ref[0] = jnp.where(is_re, w_re, w_im).astype(BF16)
    a_r, a_i = powers(ar, ai, dt, float(L))
    sign = jnp.where(is_re, -1.0, 1.0)
    for lvl in range(SCAN_LEVELS_MAX):
        if lvl < n_levels:
            apow_ref[0, lvl:lvl + 1, :] = a_r
            apow_ref[0, SCAN_LEVELS_MAX + lvl:SCAN_LEVELS_MAX + lvl + 1, :] = sign * a_i
            a_r, a_i = a_r * a_r - a_i * a_i, 2.0 * a_r * a_i
        else:
            apow_ref[0, lvl:lvl + 1, :] = jnp.zeros_like(a_r)
            apow_ref[0, SCAN_LEVELS_MAX + lvl:SCAN_LEVELS_MAX + lvl + 1, :] = jnp.zeros_like(a_r)

    col = acol_ref[0]
    arc, aic, dtc = col[:, 0:1], col[:, 1:2], jnp.exp(col[:, 2:3])
    ct = ct_ref[0]
    csel = (lax.broadcasted_iota(I32, (H, CH2), 1) % H == lax.broadcasted_iota(I32, (H, CH2), 0)).astype(F32)
    ctr = _dot_exact(ct[:, :H], csel)
    cti = _dot_exact(ct[:, H:], csel)
    sidx = (lax.broadcasted_iota(I32, (P, CH2), 1) // H).astype(F32)
    pr, pi = powers(arc, aic, dtc, sidx)
    g_r = ctr * pr - cti * pi
    g_i = ctr * pi + cti * pr
    a1r, a1i = powers(arc, aic, dtc, 1.0)
    v_r = g_r * a1r - g_i * a1i
    v_i = g_r * a1i + g_i * a1r
    vt_ref[0, :P, :] = v_r.astype(BF16)
    vt_ref[0, P:, :] = (-v_i).astype(BF16)
    bbr64, bbi64 = bbr[:, :P], bbi[:, :P]
    kt = _dot_exact(bbr64, g_r) - _dot_exact(bbi64, g_i)
    lane2 = lax.broadcasted_iota(I32, (H, CH2), 1)
    for r in range(L):
        blk = kt if r == 0 else jnp.where(lane2 >= r * H, pltpu.roll(kt, r * H, axis=1), 0.0)
        mt_acc[r * H:(r + 1) * H, :] = blk
    mt_ref[0] = mt_acc[...].astype(BF16)
    dvec_ref[0] = _dot_exact(d_ref[0], csel)


def _s5_prep(n_levels, s5_a_re, s5_a_im, s5_log_dt, s5_b_re, s5_b_im, s5_c_re, s5_c_im, s5_d):
    G, P, H = S5_GROUPS, S5_STATE, S5_GROUP_CH
    ldt = jnp.broadcast_to(s5_log_dt[:, None], (G, P))
    arow = jnp.stack([s5_a_re, s5_a_im, ldt], axis=1)
    arow = jnp.concatenate([arow, arow], axis=2)
    acol = jnp.stack([s5_a_re, s5_a_im, ldt], axis=2)
    bt = jnp.concatenate([s5_b_re.transpose(0, 2, 1), s5_b_im.transpose(0, 2, 1)], axis=1)
    ct = jnp.concatenate([s5_c_re.transpose(0, 2, 1), s5_c_im.transpose(0, 2, 1)], axis=2)
    d = s5_d.reshape(G, 1, H)
    blk = lambda *s: pl.BlockSpec((1,) + s, lambda g: (g, 0, 0))
    return pl.pallas_call(
        functools.partial(_s5_prep_kernel, n_levels),
        grid=(G,),
        in_specs=[blk(3, P2), blk(P, 3), blk(2 * H, P), blk(P, 2 * H), blk(1, H)],
        out_specs=[blk(CH2, CH2), blk(CH2, P2), blk(P2, CH2), blk(2 * SCAN_LEVELS_MAX, P2), blk(1, CH2)],
        out_shape=[jax.ShapeDtypeStruct((G, CH2, CH2), BF16),
                   jax.ShapeDtypeStruct((G, CH2, P2), BF16),
                   jax.ShapeDtypeStruct((G, P2, CH2), BF16),
                   jax.ShapeDtypeStruct((G, 2 * SCAN_LEVELS_MAX, P2), F32),
                   jax.ShapeDtypeStruct((G, 1, CH2), F32)],
        scratch_shapes=[pltpu.VMEM((CH2, CH2), F32)],
        compiler_params=_cparams("parallel"),
        name="s5_discretise",
    )(arow, acol, bt, ct, d)


def _gelu_tanh(y):
    return 0.5 * y * (1.0 + jnp.tanh(math.sqrt(2.0 / math.pi) * (y + 0.044715 * (y * y * y))))


GROUPS_PER_TILE = LANES // S5_GROUP_CH
STEPS_PER_TILE = LANES // S5_GROUP_CH


def _s5_scan_kernel(n_levels, u_ref, mt_ref, wt_ref, vt_ref, apow_ref, dvec_ref, y_ref):
    H, L = S5_GROUP_CH, S5_CHUNK
    nc = u_ref.shape[0] // L
    lane_blk = lax.broadcasted_iota(I32, (nc, LANES), 1) // H
    ridx = lax.broadcasted_iota(I32, (nc, P2), 0)
    steps = [u_ref[pl.ds(l, nc, stride=L), :] for l in range(L)]
    groups = range(GROUPS_PER_TILE)
    us, xs = [], []
    for gi in groups:
        halves = []
        for hh in range(L // STEPS_PER_TILE):
            acc = jnp.zeros((nc, LANES), F32)
            for l8 in range(STEPS_PER_TILE):
                src = steps[hh * STEPS_PER_TILE + l8]
                shift = ((l8 - gi) % STEPS_PER_TILE) * H
                moved = src if shift == 0 else pltpu.roll(src, shift, axis=1)
                acc = jnp.where(lane_blk == l8, moved, acc)
            halves.append(acc)
        u = jnp.concatenate(halves, axis=1)
        us.append(u)
        xs.append(_dot(u.astype(BF16), wt_ref[gi]))
    for lvl in range(n_levels):
        sh = 1 << lvl
        for gi in groups:
            apow = apow_ref[gi]
            prev = jnp.where(ridx >= sh, pltpu.roll(xs[gi], sh, axis=0), 0.0)
            a_r = apow[lvl:lvl + 1]
            a_i = apow[SCAN_LEVELS_MAX + lvl:SCAN_LEVELS_MAX + lvl + 1]
            xs[gi] = xs[gi] + a_r * prev + a_i * pltpu.roll(prev, S5_STATE, axis=1)
    outs = [jnp.zeros((nc, LANES), F32) for _ in range(L)]
    for gi in groups:
        x_in = jnp.where(ridx >= 1, pltpu.roll(xs[gi], 1, axis=0), 0.0)
        u = us[gi]
        y = _gelu_tanh(_dot(u.astype(BF16), mt_ref[gi]) + _dot(x_in.astype(BF16), vt_ref[gi]) + dvec_ref[gi] * u)
        for l in range(L):
            hh, l8 = divmod(l, STEPS_PER_TILE)
            src = y[:, hh * LANES:(hh + 1) * LANES]
            shift = ((gi - l8) % STEPS_PER_TILE) * H
            moved = src if shift == 0 else pltpu.roll(src, shift, axis=1)
            outs[l] = jnp.where(lane_blk == gi, moved, outs[l])
    for l in range(L):
        y_ref[pl.ds(l, nc, stride=L), :] = outs[l]


def _s5_branch(u, B, S, s5_params):
    G, L = S5_GROUPS, S5_CHUNK
    nc = S // L
    n_levels = max(1, (nc - 1).bit_length())
    assert n_levels <= SCAN_LEVELS_MAX and S5_CHUNK % STEPS_PER_TILE == 0
    mt, wt, vt, apow, dvec = _s5_prep(n_levels, *s5_params)
    n_lt = G // GROUPS_PER_TILE
    blk = lambda *s: pl.BlockSpec((GROUPS_PER_TILE,) + s, lambda b, t: (t, 0, 0))
    io = pl.BlockSpec((S, LANES), lambda b, t: (b, t))
    return pl.pallas_call(
        functools.partial(_s5_scan_kernel, n_levels),
        grid=(B, n_lt),
        in_specs=[io, blk(CH2, CH2), blk(CH2, P2), blk(P2, CH2), blk(2 * SCAN_LEVELS_MAX, P2), blk(1, CH2)],
        out_specs=io,
        out_shape=jax.ShapeDtypeStruct((B * S, S5_WIDTH), F32),
        compiler_params=_cparams("parallel", "parallel"),
        name="s5_chunk_scan",
    )(u, mt, wt, vt, apow, dvec)


def _merge_xattn_kernel(x_ref, o_ref, ys_ref, gm_ref, gs_ref, wmo_ref, wglu_ref, wout_ref, g1_ref, b1_ref,
                        k_ref, v_ref, wq_ref, wo_ref, g2_ref, b2_ref, x2_ref):
    y_mla = _dot(o_ref[0], wmo_ref[...])
    z = _dot(ys_ref[0].astype(BF16), wglu_ref[...])
    y_s5 = z[:, :D_MODEL] * jax.nn.sigmoid(z[:, D_MODEL:])
    merged = (jax.nn.sigmoid(gm_ref[0].astype(F32)) * y_mla
              + jax.nn.sigmoid(gs_ref[0].astype(F32)) * y_s5)
    mix = _dot(merged.astype(BF16), wout_ref[...])
    x1 = _layer_norm(DEEPNORM_ALPHA * x_ref[0] + mix, g1_ref[...], b1_ref[...])
    q = (_dot(x1.astype(BF16), wq_ref[...]) * (XATTN_HEAD_DIM ** -0.5)).astype(BF16)
    k = k_ref[0]
    v = v_ref[0]
    outs = []
    for h in range(XATTN_HEADS):
        sl = slice(h * XATTN_HEAD_DIM, (h + 1) * XATTN_HEAD_DIM)
        s = _dot_nt(q[:, sl], k[:, sl])
        p = jnp.exp(s - jnp.max(s, axis=-1, keepdims=True))
        o = _dot(p.astype(BF16), v[:, sl]) / jnp.sum(p, axis=-1, keepdims=True)
        outs.append(o.astype(BF16))
    xa = _dot(jnp.concatenate(outs, axis=1), wo_ref[...])
    x2_ref[0] = _layer_norm(DEEPNORM_ALPHA * x1 + xa, g2_ref[...], b2_ref[...])


def _memkv_kernel(mem_ref, g_ref, b_ref, w_ref, k_ref, v_ref):
    m = _layer_norm(mem_ref[0], g_ref[...], b_ref[...]).astype(BF16)
    kv = _dot(m, w_ref[...])
    hd = XATTN_HEADS * XATTN_HEAD_DIM
    k_ref[0] = kv[:, :hd].astype(BF16)
    v_ref[0] = kv[:, hd:].astype(BF16)


def _merge_cross_attention(x2d, o, ys, gm, gs, mem, B, S, w_mla_o, w_s5_glu, w_out, ln1_g, ln1_b,
                           mem_ln_g, mem_ln_b, w_xq, w_xkv, w_xo, ln2_g, ln2_b):
    M = mem.shape[1]
    hd = XATTN_HEADS * XATTN_HEAD_DIM
    HP = MLA_HEADS * HEAD_PAD
    row = lambda a: a.reshape(1, -1)
    wmo = jnp.pad(w_mla_o.reshape(MLA_HEADS, V_HEAD, D_MODEL), ((0, 0), (0, HEAD_PAD - V_HEAD), (0, 0)))
    wmo = wmo.reshape(HP, D_MODEL).astype(BF16)
    k, v = pl.pallas_call(
        _memkv_kernel,
        grid=(B,),
        in_specs=[pl.BlockSpec((1, M, D_MODEL), lambda b: (b, 0, 0)), _full((1, D_MODEL)), _full((1, D_MODEL)),
                  _full((D_MODEL, 2 * hd))],
        out_specs=[pl.BlockSpec((1, M, hd), lambda b: (b, 0, 0))] * 2,
        out_shape=[jax.ShapeDtypeStruct((B, M, hd), BF16)] * 2,
        compiler_params=_cparams("parallel"),
        name="memory_kv",
    )(mem, row(mem_ln_g), row(mem_ln_b), w_xkv.astype(BF16))
    tile = lambda w: pl.BlockSpec((1, TM, w), lambda b, i: (b, i, 0))
    kv_spec = pl.BlockSpec((1, M, hd), lambda b, i: (b, 0, 0))
    r3 = lambda a: a.reshape(B, S, a.shape[-1])
    x2 = pl.pallas_call(
        _merge_xattn_kernel,
        grid=(B, S // TM),
        in_specs=[tile(D_MODEL), tile(HP), tile(S5_WIDTH), tile(D_MODEL), tile(D_MODEL),
                  _full((HP, D_MODEL)), _full((S5_WIDTH, 2 * D_MODEL)), _full((D_MODEL, D_MODEL)),
                  _full((1, D_MODEL)), _full((1, D_MODEL)),
                  kv_spec, kv_spec, _full((D_MODEL, hd)), _full((hd, D_MODEL)), _full((1, D_MODEL)), _full((1, D_MODEL))],
        out_specs=tile(D_MODEL),
        out_shape=jax.ShapeDtypeStruct((B, S, D_MODEL), F32),
        compiler_params=_cparams("parallel", "parallel"),
        name="merge_ln1_xattn_ln2",
    )(r3(x2d), r3(o), r3(ys), r3(gm), r3(gs), wmo, w_s5_glu.astype(BF16), w_out.astype(BF16), row(ln1_g), row(ln1_b),
      k, v, w_xq.astype(BF16), w_xo.astype(BF16), row(ln2_g), row(ln2_b))
    return x2.reshape(B * S, D_MODEL)


def _route_sort_kernel(x_ref, w_ref, bias_ref, xs_ref, pos_ref, cnt_ref, off_ref):
    E, per = N_EXPERTS, N_EXPERTS // N_EXPERT_GROUPS
    tm = x_ref.shape[0]
    x = x_ref[...]
    xb = x.astype(BF16)
    x_lo = (x - xb.astype(F32)).astype(BF16)
    w = w_ref[...]
    wb = w.astype(BF16)
    w_lo = (w - wb.astype(F32)).astype(BF16)
    logits = _dot_nt(wb, xb) + (_dot_nt(wb, x_lo) + _dot_nt(w_lo, xb))
    scores = jax.nn.sigmoid(logits)
    sel = scores + bias_ref[...]
    neg = -jnp.inf
    i8 = lax.broadcasted_iota(I32, (per, tm), 0)
    gscore = []
    for g in range(N_EXPERT_GROUPS):
        blk = sel[g * per:(g + 1) * per]
        m1 = jnp.max(blk, axis=0, keepdims=True)
        i1 = jnp.min(jnp.where(blk == m1, i8, per), axis=0, keepdims=True)
        m2 = jnp.max(jnp.where(i8 == i1, neg, blk), axis=0, keepdims=True)
        gscore.append(m1 + m2)
    blocks = []
    for g in range(N_EXPERT_GROUPS):
        ahead = jnp.zeros((1, tm), I32)
        for o in range(N_EXPERT_GROUPS):
            if o == g:
                continue
            before = (gscore[o] >= gscore[g]) if o < g else (gscore[o] > gscore[g])
            ahead = ahead + before.astype(I32)
        blocks.append(jnp.where(ahead < TOPK_GROUPS, sel[g * per:(g + 1) * per], neg))
    cur = jnp.concatenate(blocks, axis=0)
    ie = lax.broadcasted_iota(I32, (E, tm), 0)
    picked = jnp.zeros((E, tm), F32)
    for _ in range(TOP_K):
        m = jnp.max(cur, axis=0, keepdims=True)
        idx = jnp.min(jnp.where(cur == m, ie, E), axis=0, keepdims=True)
        hit = ie == idx
        picked = jnp.where(hit, 1.0, picked)
        cur = jnp.where(hit, neg, cur)
    wsel = scores * picked
    wnorm = wsel / jnp.sum(wsel, axis=0, keepdims=True) * ROUTED_SCALE
    pb = picked.astype(BF16)
    tri = (lax.broadcasted_iota(I32, (tm, tm), 0) <= lax.broadcasted_iota(I32, (tm, tm), 1)).astype(BF16)
    incl = _dot(pb, tri)
    cnt_col = jnp.sum(picked, axis=1, keepdims=True)
    lower = (lax.broadcasted_iota(I32, (E, E), 1) < lax.broadcasted_iota(I32, (E, E), 0)).astype(F32)
    al_col = jnp.floor((cnt_col + (SEG_ALIGN - 1.0)) * (1.0 / SEG_ALIGN)) * SEG_ALIGN
    off_col = _dot_exact(lower, jnp.broadcast_to(al_col, (E, LANES)))[:, 0:1]
    pos = off_col + incl - 1.0
    q = jnp.floor(pos * (1.0 / POS_RADIX))
    pq = jnp.concatenate([jnp.where(picked > 0.0, POS_RADIX * q, POS_RADIX * POS_NONE_Q),
                          jnp.where(picked > 0.0, pos - POS_RADIX * q, 0.0)], axis=0)
    pos_ref[...] = pq
    pqb = pq.astype(BF16)
    cnt_ref[0] = jnp.broadcast_to(cnt_col, (E, LANES)).astype(I32)
    off_ref[0] = jnp.broadcast_to(off_col, (E, LANES)).astype(I32)
    cnt_row = _dot_nt(jnp.ones((8, tm), BF16), pb)
    al_row = jnp.floor((cnt_row + (SEG_ALIGN - 1.0)) * (1.0 / SEG_ALIGN)) * SEG_ALIGN
    er = lax.broadcasted_iota(I32, (E, 4 * E), 0)
    ec = lax.broadcasted_iota(I32, (E, 4 * E), 1) % E
    off_row4 = _dot_exact(al_row, (er < ec).astype(F32))[0:1]
    end_row4 = off_row4 + _dot_exact(al_row, (er == ec).astype(F32))[0:1]
    w_hi = wnorm.astype(BF16).astype(F32)
    w_r1 = wnorm - w_hi
    w_mid = w_r1.astype(BF16).astype(F32)
    w_lo = w_r1 - w_mid
    w4_t = jnp.concatenate([w_hi, w_mid, w_lo, jnp.zeros_like(w_lo)], axis=0).T
    xw = jnp.concatenate([xb, w4_t.astype(BF16)], axis=1)
    half = D_MODEL // 2

    def build(c, carry):
        r0 = pl.multiple_of(c * OH_ROWS, OH_ROWS)
        jr = (lax.broadcasted_iota(I32, (OH_ROWS, 4 * E), 0) + r0).astype(F32)
        member4 = jnp.where(jr >= off_row4, jnp.where(jr < end_row4, 1.0, 0.0), 0.0)
        target = _dot(member4[:, :2 * E].astype(BF16), pqb)
        j = (lax.broadcasted_iota(I32, (OH_ROWS, tm), 0) + r0).astype(F32)
        ohb = jnp.where(target == j, 1.0, 0.0).astype(BF16)
        rows = _dot(ohb, xw)
        xs_ref[0, pl.ds(r0, OH_ROWS), 0:half] = _pack_bf16_pair(rows[:, :half], rows[:, half:D_MODEL])
        wrow = jnp.sum(member4 * rows[:, D_MODEL:], axis=1, keepdims=True)
        xs_ref[0, pl.ds(r0, OH_ROWS), half:] = pltpu.bitcast(jnp.broadcast_to(wrow, (OH_ROWS, LANES)), I32)
        return carry

    lax.fori_loop(0, SORT_ROWS // OH_ROWS, build, 0)
    xs_ref[0, SORT_ROWS:, :] = jnp.zeros((PAD_ROWS, XW), I32)


def _route_sort(x2, w_router, router_bias):
    T = x2.shape[0]
    E = N_EXPERTS
    nt = T // TW
    seg = pl.BlockSpec((1, E, LANES), lambda i: (i, 0, 0))
    return pl.pallas_call(
        _route_sort_kernel,
        grid=(nt,),
        in_specs=[pl.BlockSpec((TW, D_MODEL), lambda i: (i, 0)), _full((E, D_MODEL)), _full((E, 1))],
        out_specs=[pl.BlockSpec((1, TS, XW), lambda i: (i, 0, 0)),
                   pl.BlockSpec((2 * E, TW), lambda i: (0, i)), seg, seg],
        out_shape=[jax.ShapeDtypeStruct((nt, TS, XW), I32), jax.ShapeDtypeStruct((2 * E, T), F32),
                   jax.ShapeDtypeStruct((nt, E, LANES), I32), jax.ShapeDtypeStruct((nt, E, LANES), I32)],
        compiler_params=_cparams("parallel"),
        name="route_local_sort",
    )(x2, w_router.T, router_bias.reshape(E, 1))


def _plan_kernel(cnt_ref, off_ref, pa_ref, ce_ref, gn_ref, nch_ref):
    E, nt = cnt_ref.shape
    n_rows = pa_ref.shape[0]
    cnt = cnt_ref[...].astype(F32)
    off = off_ref[...].astype(F32)
    npc = jnp.floor((cnt + (PR - 1.0)) * (1.0 / PR))
    tri = (lax.broadcasted_iota(I32, (nt, nt), 0) <= lax.broadcasted_iota(I32, (nt, nt), 1)).astype(F32)
    p_incl = _dot_exact(npc, tri)
    total = p_incl[:, nt - 1:nt]
    n_ch = jnp.floor((total + (PPC - 1.0)) * (1.0 / PPC))
    lower = (lax.broadcasted_iota(I32, (E, E), 1) < lax.broadcasted_iota(I32, (E, E), 0)).astype(F32)
    wide = lambda col: jnp.broadcast_to(col, (E, LANES))
    g0 = _dot_exact(lower, wide(n_ch))[:, 0:1]
    n_total = jnp.sum(n_ch, axis=0, keepdims=True)
    g0_row = wide(g0).T[0:1, :E]
    g1_row = g0_row + wide(n_ch).T[0:1, :E]
    g_e = lax.broadcasted_iota(I32, (n_rows, E), 0).astype(F32)
    sel = jnp.where(g_e >= g0_row, jnp.where(g_e < g1_row, 1.0, 0.0), 0.0)
    tile_base = lax.broadcasted_iota(I32, (E, nt), 1).astype(F32) * TS
    a_mat = tile_base + off - PR * (p_incl - npc)
    p_g = _dot_exact(sel, p_incl)
    a_g = _dot_exact(sel, a_mat)
    e_idx = lax.broadcasted_iota(I32, (E, LANES), 0).astype(F32)
    pick = lambda col: _dot_exact(sel, col)[:, 0:1]
    g0_g, tot_g, e_g = pick(wide(g0)), pick(wide(total)), pick(e_idx)
    gn_g = pick(wide(g0 + n_ch))
    g_p = lax.broadcasted_iota(I32, (n_rows, PPC), 0).astype(F32)
    q = (g_p - g0_g) * PPC + lax.broadcasted_iota(I32, (n_rows, PPC), 1).astype(F32)
    addr = jnp.full((n_rows, PPC), -float(SEG_ALIGN), F32)
    lo = jnp.zeros((n_rows, 1), F32)
    for i in range(nt):
        hi = p_g[:, i:i + 1]
        addr = jnp.where(q >= lo, jnp.where(q < hi, a_g[:, i:i + 1] + PR * q, addr), addr)
        lo = hi
    pa_ref[...] = addr.astype(I32)
    g_col = lax.broadcasted_iota(I32, (n_rows, LANES), 0).astype(F32)
    ce_ref[...] = jnp.where(g_col < n_total, jnp.broadcast_to(e_g, (n_rows, LANES)), E - 1.0).astype(I32)
    gn_ref[...] = jnp.broadcast_to(gn_g, (n_rows, LANES)).astype(I32)
    nch_ref[...] = jnp.broadcast_to(n_total, nch_ref.shape).astype(I32)


def _plan(cnt, off, n_chunks_max):
    n_rows = (n_chunks_max + 7) // 8 * 8
    pa, ce, gn, nch = pl.pallas_call(
        _plan_kernel,
        out_shape=[jax.ShapeDtypeStruct((n_rows, PPC), I32), jax.ShapeDtypeStruct((n_rows, LANES), I32),
                   jax.ShapeDtypeStruct((n_rows, LANES), I32), jax.ShapeDtypeStruct((8, LANES), I32)],
        name="moe_plan",
    )(cnt, off)
    return pa.reshape(n_rows * PPC), ce[:, 0], gn[:, 0], nch[0, :1]


def _unpack_bf16_pair(w):
    lo = pltpu.bitcast(lax.shift_left(w, jnp.int32(16)), F32).astype(BF16)
    hi = pltpu.bitcast(w & jnp.int32(-65536), F32).astype(BF16)
    return lo, hi


def _pack_bf16_pair(lo, hi):
    lo_bits = lax.shift_right_logical(pltpu.bitcast(lo.astype(BF16).astype(F32), I32), jnp.int32(16))
    hi_bits = pltpu.bitcast(hi.astype(BF16).astype(F32), I32)
    return hi_bits | lo_bits


def _expert_kernel(ce_ref, gn_ref, pa_ref, nch_ref, tot_ref, xs_hbm, wgu_hbm, wdn_hbm, y_hbm,
                   xbuf, ybuf, zbuf, wgu_f, wdn_f, wgu_b, wdn_b, wslot, gsem, wsem, zsem, esem):
    g = pl.program_id(0)
    n = nch_ref[0]
    slot = g % N_SLOTS
    half = D_MODEL // 2

    dummy_base = (tot_ref.shape[0] - 1) * TS

    def gather_copy(s, a, jp):
        return pltpu.make_async_copy(xs_hbm.at[pl.ds(pl.multiple_of(a, SEG_ALIGN), PR), :],
                                     xbuf.at[s, pl.ds(jp * PR, PR), :], gsem.at[s])

    def write_copy(s, a, jp):
        return pltpu.make_async_copy(ybuf.at[s, pl.ds(jp * PR, PR), :],
                                     y_hbm.at[pl.ds(pl.multiple_of(a, SEG_ALIGN), PR), :], wsem.at[s])

    def gather_start(c, s):
        for jp in range(PPC):
            gather_copy(s, jnp.maximum(pa_ref[c * PPC + jp], 0), jp).start()

    def write_start(c, s):
        for jp in range(PPC):
            a = pa_ref[c * PPC + jp]
            write_copy(s, jnp.where(a >= 0, a, dummy_base + jp * PR), jp).start()

    def gather_wait(s):
        for jp in range(PPC):
            gather_copy(s, 0, jp).wait()

    def write_wait(s):
        for jp in range(PPC):
            write_copy(s, 0, jp).wait()

    def weight_copies(e, s):
        return (pltpu.make_async_copy(wgu_hbm.at[e], wgu_f.at[s], esem.at[s]),
                pltpu.make_async_copy(wdn_hbm.at[e], wdn_f.at[s], esem.at[s]))

    @pl.when(g == 0)
    def _():
        wslot[0] = 0
        for cp in weight_copies(ce_ref[0], 0):
            cp.start()
        gather_start(0, 0)
        ybuf[...] = jnp.zeros_like(ybuf)
        zbuf[...] = jnp.zeros_like(zbuf)

        def zero_copy(row):
            return pltpu.make_async_copy(zbuf, y_hbm.at[pl.ds(pl.multiple_of(row, SEG_ALIGN), SEG_ALIGN), :], zsem)

        def tail_of(i):
            return i * TS + tot_ref[i], (TS - tot_ref[i]) // SEG_ALIGN

        def zero_start(i, c):
            base, cnt8 = tail_of(i)
            lax.fori_loop(0, cnt8, lambda q, z: (zero_copy(base + q * SEG_ALIGN).start(), z)[1], 0)
            return c

        def zero_wait(i, c):
            base, cnt8 = tail_of(i)
            lax.fori_loop(0, cnt8, lambda q, z: (zero_copy(base + q * SEG_ALIGN).wait(), z)[1], 0)
            return c

        lax.fori_loop(0, tot_ref.shape[0], zero_start, 0)
        lax.fori_loop(0, tot_ref.shape[0], zero_wait, 0)

    @pl.when((g == 0) & (n > 1))
    def _():
        gather_start(1, 1)

    @pl.when(g + GATHER_AHEAD < n)
    def _():
        gather_start(g + GATHER_AHEAD, (g + GATHER_AHEAD) % N_SLOTS)

    e_cur = ce_ref[g]
    e_prev = ce_ref[jnp.maximum(g - 1, 0)]
    e_prev2 = ce_ref[jnp.maximum(g - 2, 0)]

    @pl.when(g < n)
    def _():
        @pl.when((g == 0) | (e_cur != e_prev))
        def _():
            s = wslot[0]
            for cp in weight_copies(e_cur, s):
                cp.wait()
            wgu_b[...] = wgu_f[s].astype(BF16)
            wdn_b[...] = wdn_f[s].astype(BF16)
            g_next = gn_ref[g]

            @pl.when(g_next < n)
            def _():
                for cp in weight_copies(ce_ref[g_next], 1 - s):
                    cp.start()

            wslot[0] = 1 - s

        gather_wait(slot)

        @pl.when((g >= 2) & (e_prev == e_prev2))
        def _():
            write_wait((g - 2) % N_SLOTS)

        @pl.when((g >= 1) & (e_cur != e_prev))
        def _():
            write_wait((g - 1) % N_SLOTS)

        def swiglu_rows(rows):
            xw = xbuf[slot, :rows, :]
            lo, hi = _unpack_bf16_pair(xw[:, :half])
            h = _dot(jnp.concatenate([lo, hi], axis=1), wgu_b[...])
            gate, up = h[:, :EXPERT_FF], h[:, EXPERT_FF:]
            act = (gate * jax.nn.sigmoid(gate) * up).astype(BF16)
            y = _dot(act, wdn_b[...])
            w_row = pltpu.bitcast(xw[:, half:], F32)
            y = y * jnp.concatenate([w_row] * (D_MODEL // LANES), axis=1)
            ybuf[slot, :rows, :] = _pack_bf16_pair(y[:, :half], y[:, half:])

        second_half_used = pa_ref[g * PPC + PPC // 2] >= 0

        @pl.when(second_half_used)
        def _():
            swiglu_rows(CR)

        @pl.when(jnp.logical_not(second_half_used))
        def _():
            swiglu_rows(CR // 2)

        write_start(g, slot)

    @pl.when(g == pl.num_programs(0) - 1)
    def _():
        @pl.when((n >= 2) & (ce_ref[jnp.maximum(n - 1, 0)] == ce_ref[jnp.maximum(n - 2, 0)]))
        def _():
            write_wait((n - 2) % N_SLOTS)

        write_wait((n - 1) % N_SLOTS)


def _experts(ce, gn, pa, nch, tot, xs, w_exp_gu, w_exp_down):
    n_rows = xs.shape[0]
    half = D_MODEL // 2
    hbm = pl.BlockSpec(memory_space=pl.ANY)
    grid_spec = pltpu.PrefetchScalarGridSpec(
        num_scalar_prefetch=5,
        grid=(ce.shape[0],),
        in_specs=[hbm, hbm, hbm],
        out_specs=hbm,
        scratch_shapes=[pltpu.VMEM((N_SLOTS, CR, XW), I32), pltpu.VMEM((N_SLOTS, CR, half), I32),
                        pltpu.VMEM((SEG_ALIGN, half), I32),
                        pltpu.VMEM((2, D_MODEL, 2 * EXPERT_FF), F32), pltpu.VMEM((2, EXPERT_FF, D_MODEL), F32),
                        pltpu.VMEM((D_MODEL, 2 * EXPERT_FF), BF16), pltpu.VMEM((EXPERT_FF, D_MODEL), BF16),
                        pltpu.SMEM((1,), I32),
                        pltpu.SemaphoreType.DMA((N_SLOTS,)), pltpu.SemaphoreType.DMA((N_SLOTS,)),
                        pltpu.SemaphoreType.DMA(()), pltpu.SemaphoreType.DMA((2,))],
    )
    return pl.pallas_call(
        _expert_kernel,
        grid_spec=grid_spec,
        out_shape=jax.ShapeDtypeStruct((n_rows + TS, half), I32),
        compiler_params=_cparams("arbitrary"),
        name="moe_grouped_swiglu",
    )(ce, gn, pa, nch, tot, xs, w_exp_gu, w_exp_down)


def _combine_kernel(tot_ref, y_ref, pos_ref, cnt_ref, off_ref, x2_ref, wsg_ref, wsd_ref, g_ref, b_ref, out_ref,
                    acc_lo, acc_hi):
    tm = x2_ref.shape[0]
    E = N_EXPERTS
    n_sorted = tot_ref[pl.program_id(0)]
    x2 = x2_ref[...]
    h = _dot(x2.astype(BF16), wsg_ref[...])
    gate, up = h[:, :SHARED_FF], h[:, SHARED_FF:]
    shared = _dot((gate * jax.nn.sigmoid(gate) * up).astype(BF16), wsd_ref[...])
    pq_t = pos_ref[...].T.astype(BF16)
    off_col = off_ref[0][:, 0:1].astype(F32)
    cnt_col = cnt_ref[0][:, 0:1].astype(F32)
    end_col = off_col + jnp.floor((cnt_col + (SEG_ALIGN - 1.0)) * (1.0 / SEG_ALIGN)) * SEG_ALIGN
    off_col2 = jnp.concatenate([off_col, off_col], axis=0)
    end_col2 = jnp.concatenate([end_col, end_col], axis=0)
    acc_lo[...] = jnp.zeros_like(acc_lo)
    acc_hi[...] = jnp.zeros_like(acc_hi)

    def gather_back(c, carry):
        r0 = pl.multiple_of(c * OH_ROWS, OH_ROWS)
        jc = (lax.broadcasted_iota(I32, (2 * E, OH_ROWS), 1) + r0).astype(F32)
        member2 = jnp.where(jc >= off_col2, jnp.where(jc < end_col2, 1.0, 0.0), 0.0)
        target = _dot(pq_t, member2.astype(BF16))
        j = (lax.broadcasted_iota(I32, (tm, OH_ROWS), 1) + r0).astype(F32)
        ohb = jnp.where(target == j, 1.0, 0.0).astype(BF16)
        yw = y_ref[0, pl.ds(r0, OH_ROWS), :]
        row = lax.broadcasted_iota(I32, yw.shape, 0) + r0
        lo, hi = _unpack_bf16_pair(jnp.where(row < n_sorted, yw, 0))
        acc_lo[...] += _dot(ohb, lo)
        acc_hi[...] += _dot(ohb, hi)
        return carry

    lax.fori_loop(0, SORT_ROWS // OH_ROWS, gather_back, 0)
    ff = shared + jnp.concatenate([acc_lo[...], acc_hi[...]], axis=1)
    out_ref[...] = _layer_norm(DEEPNORM_ALPHA * x2 + ff, g_ref[...], b_ref[...])


def _combine(tot, y, pos, cnt, off, x2, w_sh_gu, w_sh_down, ln_g, ln_b):
    T = x2.shape[0]
    nt = T // TW
    half = D_MODEL // 2
    tile = lambda w: pl.BlockSpec((TW, w), lambda i: (i, 0))
    return pl.pallas_call(
        _combine_kernel,
        grid=(nt,),
        in_specs=[pl.BlockSpec(memory_space=pltpu.SMEM),
                  pl.BlockSpec((1, TS, half), lambda i: (i, 0, 0)),
                  pl.BlockSpec((2 * N_EXPERTS, TW), lambda i: (0, i)),
                  pl.BlockSpec((1, N_EXPERTS, LANES), lambda i: (i, 0, 0)),
                  pl.BlockSpec((1, N_EXPERTS, LANES), lambda i: (i, 0, 0)), tile(D_MODEL),
                  _full((D_MODEL, 2 * SHARED_FF)), _full((SHARED_FF, D_MODEL)),
                  _full((1, D_MODEL)), _full((1, D_MODEL))],
        out_specs=tile(D_MODEL),
        out_shape=jax.ShapeDtypeStruct((T, D_MODEL), F32),
        scratch_shapes=[pltpu.VMEM((TW, half), F32), pltpu.VMEM((TW, half), F32)],
        compiler_params=_cparams("parallel"),
        name="moe_combine_shared_ln3",
    )(tot, y.reshape(nt + 1, TS, half), pos, cnt, off, x2, w_sh_gu.astype(BF16), w_sh_down.astype(BF16),
      ln_g.reshape(1, -1), ln_b.reshape(1, -1))


def _moe(x2, w_router, router_bias, w_exp_gu, w_exp_down, w_sh_gu, w_sh_down, ln_g, ln_b):
    T = x2.shape[0]
    E = N_EXPERTS
    nt = T // TW
    xs, pos, cnt, off = _route_sort(x2, w_router, router_bias)
    n_pieces_max = (T * TOP_K + nt * E * (PR - 1)) // PR
    n_chunks_max = n_pieces_max // PPC + E
    pa, ce, gn, nch = _plan(cnt[:, :, 0].T, off[:, :, 0].T, n_chunks_max)
    last_al = (cnt[:, E - 1, 0] + SEG_ALIGN - 1) // SEG_ALIGN * SEG_ALIGN
    tot = jnp.concatenate([off[:, E - 1, 0] + last_al, jnp.zeros((1,), I32)])
    y = _experts(ce, gn, pa, nch, tot, xs.reshape(nt * TS, XW), w_exp_gu, w_exp_down)
    return _combine(tot, y, pos, cnt, off, x2, w_sh_gu, w_sh_down, ln_g, ln_b)


def kernel(x, mem, positions, w_in, q_norm_g, kv_norm_g, w_uq, w_ukv, w_mla_o, s5_a_re, s5_a_im, s5_log_dt, s5_b_re, s5_b_im, s5_c_re, s5_c_im, s5_d, w_s5_glu, w_out, ln1_g, ln1_b, mem_ln_g, mem_ln_b, w_xq, w_xkv, w_xo, ln2_g, ln2_b, w_router, router_bias, w_exp_gu, w_exp_down, w_sh_gu, w_sh_down, ln3_g, ln3_b):
    B, S, D = x.shape
    assert D == D_MODEL and S % TQ == 0 and (B * S) % TM == 0 and S % S5_CHUNK == 0
    xc = x.reshape(B * S, D)
    for l in range(w_in.shape[0]):
        tabs = _rope_tables(positions)
        q, k, v, u, gm, gs = _inproj(xc, tabs, w_in[l], q_norm_g[l], kv_norm_g[l], w_uq[l], w_ukv[l])
        o = _attention(q, k, v, B, S)
        ys = _s5_branch(u, B, S, (s5_a_re[l], s5_a_im[l], s5_log_dt[l], s5_b_re[l], s5_b_im[l],
                                  s5_c_re[l], s5_c_im[l], s5_d[l]))
        x2 = _merge_cross_attention(xc, o, ys, gm, gs, mem, B, S, w_mla_o[l], w_s5_glu[l], w_out[l],
                                    ln1_g[l], ln1_b[l], mem_ln_g[l], mem_ln_b[l], w_xq[l], w_xkv[l], w_xo[l],
                                    ln2_g[l], ln2_b[l])
        xc = _moe(x2, w_router[l], router_bias[l], w_exp_gu[l], w_exp_down[l], w_sh_gu[l],
                  w_sh_down[l], ln3_g[l], ln3_b[l])
    return xc.reshape(B, S, D)
```

```python
import functools
import math

import jax
import jax.numpy as jnp
from jax import lax
from jax.experimental import pallas as pl
from jax.experimental.pallas import tpu as pltpu

F32 = jnp.float32
BF16 = jnp.bfloat16
I32 = jnp.int32

D_MODEL = 1024
MLA_HEADS = 8
QK_NOPE = 64
QK_ROPE = 32
V_HEAD = 64
Q_LORA = 256
KV_LORA = 256
ROPE_THETA = 10000.0
S5_GROUP_CH = 16
S5_WIDTH = 512
S5_GROUPS = 32
S5_STATE = 64
XATTN_HEADS = 4
XATTN_HEAD_DIM = 128
N_EXPERTS = 64
TOP_K = 8
N_EXPERT_GROUPS = 8
TOPK_GROUPS = 4
EXPERT_FF = 256
SHARED_FF = 256
ROUTED_SCALE = 2.5
LN_EPS = 1e-5
RMS_EPS = 1e-6
DEPTH = 1
DEEPNORM_ALPHA = (2.0 * DEPTH) ** 0.25

LANES = 128
HEAD_PAD = 128
ROPE_LO = QK_NOPE
ROPE_HALF = QK_ROPE // 2

TM = 512
TQ = 1024
KV_GROUPS = (2, 1)
S5_CHUNK = 16
TW = 512
OH_ROWS = 512
PR = 16
PAD_ROWS = PR
SEG_ALIGN = 8
SORT_ROWS = TOP_K * TW + N_EXPERTS * SEG_ALIGN
TS = SORT_ROWS + PAD_ROWS
XW = D_MODEL // 2 + LANES
CR = 512
PPC = CR // PR
GATHER_AHEAD = 3
N_SLOTS = GATHER_AHEAD + 1
POS_RADIX = 64
POS_NONE_Q = 127
assert SORT_ROWS <= POS_RADIX * POS_NONE_Q and SORT_ROWS % OH_ROWS == 0
VMEM_LIMIT = 48 * 1024 * 1024


def _cparams(*sem):
    return pltpu.CompilerParams(dimension_semantics=sem, vmem_limit_bytes=VMEM_LIMIT)


def _dot(a, b):
    return jnp.dot(a, b, preferred_element_type=F32)


def _dot_nt(a, b, precision=None):
    return lax.dot_general(a, b, (((1,), (1,)), ((), ())), preferred_element_type=F32,
                           precision=precision)


def _dot_exact(a, b):
    return jnp.dot(a, b, preferred_element_type=F32, precision=lax.Precision.HIGHEST)


def _layer_norm(h, g, b):
    mu = jnp.mean(h, axis=-1, keepdims=True)
    c = h - mu
    var = jnp.mean(c * c, axis=-1, keepdims=True)
    return c * lax.rsqrt(var + LN_EPS) * g + b


def _rms_norm(h, g):
    return h * lax.rsqrt(jnp.mean(h * h, axis=-1, keepdims=True) + RMS_EPS) * g


def _full(shape):
    n = len(shape)
    return pl.BlockSpec(shape, lambda *_: (0,) * n)


def _rope_angle_kernel(pos_ref, invf_ref, cos_ref, sin_ref):
    ang = pos_ref[...].astype(F32) * invf_ref[...]
    cos_ref[...] = jnp.cos(ang)
    sin_ref[...] = jnp.sin(ang)


def _rope_tables(positions):
    T = positions.size
    rows = T * ROPE_HALF // LANES
    pos_rep = jnp.repeat(positions.reshape(T), ROPE_HALF).reshape(rows, LANES)
    inv_freq = ROPE_THETA ** (-jnp.arange(0, QK_ROPE, 2, dtype=F32) / QK_ROPE)
    invf = jnp.tile(inv_freq, LANES // ROPE_HALF).reshape(1, LANES)
    cos, sin = pl.pallas_call(
        _rope_angle_kernel,
        out_shape=(jax.ShapeDtypeStruct((rows, LANES), F32),) * 2,
        name="rope_angles",
    )(pos_rep, invf)
    cos = cos.reshape(T, ROPE_HALF)
    sin = sin.reshape(T, ROPE_HALF)
    one = jnp.ones((T, ROPE_LO), F32)
    zero_lo = jnp.zeros((T, ROPE_LO), F32)
    zero_h = jnp.zeros((T, ROPE_HALF), F32)
    tail1 = jnp.ones((T, HEAD_PAD - ROPE_LO - QK_ROPE), F32)
    tail0 = jnp.zeros((T, HEAD_PAD - ROPE_LO - QK_ROPE), F32)
    c_tab = jnp.concatenate([one, cos, cos, tail1], axis=1)
    s_up = jnp.concatenate([zero_lo, -sin, zero_h, tail0], axis=1)
    s_dn = jnp.concatenate([zero_lo, zero_h, sin, tail0], axis=1)
    return c_tab, s_up, s_dn


def _rope(x, c_tab, s_up, s_dn):
    return (x * c_tab + pltpu.roll(x, HEAD_PAD - ROPE_HALF, axis=1) * s_up
            + pltpu.roll(x, ROPE_HALF, axis=1) * s_dn)


def _inproj_kernel(x_ref, c_ref, su_ref, sd_ref, wlat_ref, wu_ref, wgm_ref, wgs_ref, qg_ref, kvg_ref,
                   wuq_ref, wuk_ref, wuv_ref, q_ref, k_ref, v_ref, u_ref, gm_ref, gs_ref):
    xb = x_ref[...].astype(BF16)
    lat = _dot(xb, wlat_ref[...])
    qn = _rms_norm(lat[:, :Q_LORA], qg_ref[...]).astype(BF16)
    kvn = _rms_norm(lat[:, Q_LORA:Q_LORA + KV_LORA], kvg_ref[...]).astype(BF16)
    c_tab, s_up, s_dn = c_ref[...], su_ref[...], sd_ref[...]
    k_rope = _rope(lat[:, Q_LORA + KV_LORA:], c_tab, s_up, s_dn)
    q = _dot(qn, wuq_ref[...])
    k = _dot(kvn, wuk_ref[...])
    scale = (QK_NOPE + QK_ROPE) ** -0.5 * math.log2(math.e)
    for h in range(MLA_HEADS):
        sl = slice(h * HEAD_PAD, (h + 1) * HEAD_PAD)
        q_ref[:, sl] = (_rope(q[:, sl], c_tab, s_up, s_dn) * scale).astype(BF16)
        k_ref[:, sl] = (k[:, sl] + k_rope).astype(BF16)
    v = _dot(kvn, wuv_ref[...])
    ones_lane = lax.broadcasted_iota(I32, (1, v.shape[1]), 1) % HEAD_PAD == V_HEAD
    v_ref[...] = jnp.where(ones_lane, 1.0, v).astype(BF16)
    u_ref[...] = _dot(xb, wu_ref[...])
    gm_ref[...] = _dot(xb, wgm_ref[...]).astype(BF16)
    gs_ref[...] = _dot(xb, wgs_ref[...]).astype(BF16)


def _pad_heads(w, head_w, lo_w):
    K = w.shape[0]
    w = w.reshape(K, MLA_HEADS, head_w)[:, :, :lo_w]
    w = jnp.pad(w, ((0, 0), (0, 0), (0, HEAD_PAD - lo_w)))
    return w.reshape(K, MLA_HEADS * HEAD_PAD)


def _inproj(x2d, tabs, w_in, q_norm_g, kv_norm_g, w_uq, w_ukv):
    T = x2d.shape[0]
    o_rope = Q_LORA + KV_LORA
    o_u = o_rope + QK_ROPE
    o_gm = o_u + S5_WIDTH
    o_gs = o_gm + D_MODEL
    w_rope = jnp.pad(w_in[:, o_rope:o_u], ((0, 0), (ROPE_LO, HEAD_PAD - ROPE_LO - QK_ROPE)))
    w_lat = jnp.concatenate([w_in[:, :o_rope], w_rope], axis=1).astype(BF16)
    w_u = w_in[:, o_u:o_gm].astype(BF16)
    w_gm = w_in[:, o_gm:o_gs].astype(BF16)
    w_gs = w_in[:, o_gs:].astype(BF16)
    wuq = _pad_heads(w_uq, QK_NOPE + QK_ROPE, QK_NOPE + QK_ROPE).astype(BF16)
    kv3 = w_ukv.reshape(KV_LORA, MLA_HEADS, QK_NOPE + V_HEAD)
    wuk = _pad_heads(kv3[:, :, :QK_NOPE].reshape(KV_LORA, -1), QK_NOPE, QK_NOPE).astype(BF16)
    wuv = _pad_heads(kv3[:, :, QK_NOPE:].reshape(KV_LORA, -1), V_HEAD, V_HEAD).astype(BF16)
    HP = MLA_HEADS * HEAD_PAD
    tile = lambda w: pl.BlockSpec((TM, w), lambda i: (i, 0))
    return pl.pallas_call(
        _inproj_kernel,
        grid=(T // TM,),
        in_specs=[tile(D_MODEL), tile(HEAD_PAD), tile(HEAD_PAD), tile(HEAD_PAD),
                  _full(w_lat.shape), _full(w_u.shape), _full(w_gm.shape), _full(w_gs.shape),
                  _full((1, Q_LORA)), _full((1, KV_LORA)),
                  _full(wuq.shape), _full(wuk.shape), _full(wuv.shape)],
        out_specs=[tile(HP), tile(HP), tile(HP), tile(S5_WIDTH), tile(D_MODEL), tile(D_MODEL)],
        out_shape=[jax.ShapeDtypeStruct((T, HP), BF16)] * 3
        + [jax.ShapeDtypeStruct((T, S5_WIDTH), F32)]
        + [jax.ShapeDtypeStruct((T, D_MODEL), BF16)] * 2,
        compiler_params=_cparams("parallel"),
        name="inproj_mla_prep",
    )(x2d, *tabs, w_lat, w_u, w_gm, w_gs, q_norm_g.reshape(1, -1), kv_norm_g.reshape(1, -1),
      wuq, wuk, wuv)


def _attn_kernel(q_ref, k_ref, v_ref, o_ref):
    qi = pl.program_id(2)
    q = q_ref[0]

    def update(qq, k, v, carry, row_offset=None):
        m, acc = carry
        s = _dot_nt(qq, k)
        if row_offset is not None:
            row = lax.broadcasted_iota(I32, s.shape, 0) + row_offset
            col = lax.broadcasted_iota(I32, s.shape, 1)
            s = jnp.where(col <= row, s, -jnp.inf)
        m_new = jnp.maximum(m, jnp.max(s, axis=-1, keepdims=True))
        p = jnp.exp2(s - m_new)
        acc = jnp.exp2(m - m_new) * acc + _dot(p.astype(BF16), v)
        return m_new, acc

    def step(blk, n_blk, carry):
        tk = n_blk * TQ
        start = pl.multiple_of(blk * TQ, TQ)
        return update(q, k_ref[0, pl.ds(start, tk), :], v_ref[0, pl.ds(start, tk), :], carry)

    carry = (jnp.full((TQ, 1), -jnp.inf, F32), jnp.zeros((TQ, HEAD_PAD), F32))
    done = 0
    for n_blk in KV_GROUPS:
        base = done
        n_it = (qi - base) // n_blk
        carry = lax.fori_loop(0, n_it, lambda j, c, base=base, n_blk=n_blk: step(base + j * n_blk, n_blk, c), carry)
        done = base + n_it * n_blk
    hq = TQ // 2
    d0 = pl.multiple_of(qi * TQ, TQ)
    m, acc = carry
    _, acc_lo = update(q[:hq], k_ref[0, pl.ds(d0, hq), :], v_ref[0, pl.ds(d0, hq), :], (m[:hq], acc[:hq]), 0)
    _, acc_hi = update(q[hq:], k_ref[0, pl.ds(d0, TQ), :], v_ref[0, pl.ds(d0, TQ), :], (m[hq:], acc[hq:]), hq)
    o_ref[0, :hq, :] = (acc_lo / acc_lo[:, V_HEAD:V_HEAD + 1]).astype(BF16)
    o_ref[0, hq:, :] = (acc_hi / acc_hi[:, V_HEAD:V_HEAD + 1]).astype(BF16)


def _attention(q, k, v, B, S):
    HP = MLA_HEADS * HEAD_PAD
    q, k, v = (a.reshape(B, S, HP) for a in (q, k, v))
    o = pl.pallas_call(
        _attn_kernel,
        grid=(B, MLA_HEADS, S // TQ),
        in_specs=[pl.BlockSpec((1, TQ, HEAD_PAD), lambda b, h, i: (b, i, h)),
                  pl.BlockSpec((1, S, HEAD_PAD), lambda b, h, i: (b, 0, h)),
                  pl.BlockSpec((1, S, HEAD_PAD), lambda b, h, i: (b, 0, h))],
        out_specs=pl.BlockSpec((1, TQ, HEAD_PAD), lambda b, h, i: (b, i, h)),
        out_shape=jax.ShapeDtypeStruct((B, S, HP), BF16),
        compiler_params=_cparams("parallel", "parallel", "arbitrary"),
        name="mla_flash_attention",
    )(q, k, v)
    return o.reshape(B * S, HP)


CH2 = S5_CHUNK * S5_GROUP_CH
P2 = 2 * S5_STATE
SCAN_LEVELS_MAX = 16


S5_PREP_GROUPS = 4


def _s5_prep_kernel(n_levels, *refs):
    *io_refs, mt_acc = refs
    for k in range(S5_PREP_GROUPS):
        _s5_prep_group(n_levels, *(r.at[pl.ds(k, 1)] for r in io_refs), mt_acc)


def _s5_prep_group(n_levels, arow_ref, acol_ref, bt_ref, ct_ref, d_ref,
                   mt_ref, wt_ref, vt_ref, apow_ref, dvec_ref, mt_acc):
    P, H, L = S5_STATE, S5_GROUP_CH, S5_CHUNK

    def powers(ar, ai, dt, n):
        e = jnp.exp(n * (ar * dt))
        ang = n * (ai * dt)
        return e * jnp.cos(ang), e * jnp.sin(ang)

    def zoh_coef(ar, ai, dt):
        pr, pi = powers(ar, ai, dt, 1.0)
        nr, ni = pr - 1.0, pi
        den = ar * ar + ai * ai
        return (nr * ar + ni * ai) / den, (ni * ar - nr * ai) / den

    row = arow_ref[0]
    ar, ai, dt = row[0:1], row[1:2], jnp.exp(row[2:3])
    cr, ci = zoh_coef(ar, ai, dt)
    bt = bt_ref[0]
    btr = jnp.concatenate([bt[:H], bt[:H]], axis=1)
    bti = jnp.concatenate([bt[H:], bt[H:]], axis=1)
    bbr = cr * btr - ci * bti
    bbi = cr * bti + ci * btr
    lane = lax.broadcasted_iota(I32, (1, P2), 1)
    is_re = lane < P
    ridx = lax.broadcasted_iota(I32, (CH2, P2), 0) // H
    pwr, pwi = powers(ar, ai, dt, (L - 1 - ridx).astype(F32))
    rsel = (lax.broadcasted_iota(I32, (CH2, H), 0) % H == lax.broadcasted_iota(I32, (CH2, H), 1)).astype(F32)
    bbr_t = _dot_exact(rsel, bbr)
    bbi_t = _dot_exact(rsel, bbi)
    w_re = pwr * bbr_t - pwi * bbi_t
    w_im = pwr * bbi_t + pwi * bbr_t
    wt_ref[0] = jnp.where(is_re, w_re, w_im).astype(BF16)
    a_r, a_i = powers(ar, ai, dt, float(L))
    sign = jnp.where(is_re, -1.0, 1.0)
    for lvl in range(SCAN_LEVELS_MAX):
        if lvl < n_levels:
            apow_ref[0, lvl:lvl + 1, :] = a_r
            apow_ref[0, SCAN_LEVELS_MAX + lvl:SCAN_LEVELS_MAX + lvl + 1, :] = sign * a_i
            a_r, a_i = a_r * a_r - a_i * a_i, 2.0 * a_r * a_i
        else:
            apow_ref[0, lvl:lvl + 1, :] = jnp.zeros_like(a_r)
            apow_ref[0, SCAN_LEVELS_MAX + lvl:SCAN_LEVELS_MAX + lvl + 1, :] = jnp.zeros_like(a_r)

    col = acol_ref[0]
    arc, aic, dtc = col[:, 0:1], col[:, 1:2], jnp.exp(col[:, 2:3])
    ct = ct_ref[0]
    csel = (lax.broadcasted_iota(I32, (H, CH2), 1) % H == lax.broadcasted_iota(I32, (H, CH2), 0)).astype(F32)
    ctr = _dot_exact(ct[:, :H], csel)
    cti = _dot_exact(ct[:, H:], csel)
    sidx = (lax.broadcasted_iota(I32, (P, CH2), 1) // H).astype(F32)
    pr, pi = powers(arc, aic, dtc, sidx)
    g_r = ctr * pr - cti * pi
    g_i = ctr * pi + cti * pr
    a1r, a1i = powers(arc, aic, dtc, 1.0)
    v_r = g_r * a1r - g_i * a1i
    v_i = g_r * a1i + g_i * a1r
    vt_ref[0, :P, :] = v_r.astype(BF16)
    vt_ref[0, P:, :] = (-v_i).astype(BF16)
    bbr64, bbi64 = bbr[:, :P], bbi[:, :P]
    kt = _dot_exact(bbr64, g_r) - _dot_exact(bbi64, g_i)
    lane2 = lax.broadcasted_iota(I32, (H, CH2), 1)
    for r in range(L):
        blk = kt if r == 0 else jnp.where(lane2 >= r * H, pltpu.roll(kt, r * H, axis=1), 0.0)
        mt_acc[r * H:(r + 1) * H, :] = blk
    mt_ref[0] = mt_acc[...].astype(BF16)
    dvec_ref[0] = _dot_exact(d_ref[0], csel)


def _s5_prep(n_levels, s5_a_re, s5_a_im, s5_log_dt, s5_b_re, s5_b_im, s5_c_re, s5_c_im, s5_d):
    G, P, H = S5_GROUPS, S5_STATE, S5_GROUP_CH
    ldt = jnp.broadcast_to(s5_log_dt[:, None], (G, P))
    arow = jnp.stack([s5_a_re, s5_a_im, ldt], axis=1)
    arow = jnp.concatenate([arow, arow], axis=2)
    acol = jnp.stack([s5_a_re, s5_a_im, ldt], axis=2)
    bt = jnp.concatenate([s5_b_re.transpose(0, 2, 1), s5_b_im.transpose(0, 2, 1)], axis=1)
    ct = jnp.concatenate([s5_c_re.transpose(0, 2, 1), s5_c_im.transpose(0, 2, 1)], axis=2)
    d = s5_d.reshape(G, 1, H)
    blk = lambda *s: pl.BlockSpec((S5_PREP_GROUPS,) + s, lambda g: (g, 0, 0))
    return pl.pallas_call(
        functools.partial(_s5_prep_kernel, n_levels),
        grid=(G // S5_PREP_GROUPS,),
        in_specs=[blk(3, P2), blk(P, 3), blk(2 * H, P), blk(P, 2 * H), blk(1, H)],
        out_specs=[blk(CH2, CH2), blk(CH2, P2), blk(P2, CH2), blk(2 * SCAN_LEVELS_MAX, P2), blk(1, CH2)],
        out_shape=[jax.ShapeDtypeStruct((G, CH2, CH2), BF16),
                   jax.ShapeDtypeStruct((G, CH2, P2), BF16),
                   jax.ShapeDtypeStruct((G, P2, CH2), BF16),
                   jax.ShapeDtypeStruct((G, 2 * SCAN_LEVELS_MAX, P2), F32),
                   jax.ShapeDtypeStruct((G, 1, CH2), F32)],
        scratch_shapes=[pltpu.VMEM((CH2, CH2), F32)],
        compiler_params=_cparams("parallel"),
        name="s5_discretise",
    )(arow, acol, bt, ct, d)


def _gelu_tanh(y):
    return 0.5 * y * (1.0 + jnp.tanh(math.sqrt(2.0 / math.pi) * (y + 0.044715 * (y * y * y))))


GROUPS_PER_TILE = LANES // S5_GROUP_CH
STEPS_PER_TILE = LANES // S5_GROUP_CH


def _s5_scan_kernel(n_levels, u_ref, mt_ref, wt_ref, vt_ref, apow_ref, dvec_ref, y_ref):
    H, L = S5_GROUP_CH, S5_CHUNK
    nc = u_ref.shape[0] // L
    lane_blk = lax.broadcasted_iota(I32, (nc, LANES), 1) // H
    ridx = lax.broadcasted_iota(I32, (nc, P2), 0)
    steps = [u_ref[pl.ds(l, nc, stride=L), :] for l in range(L)]
    groups = range(GROUPS_PER_TILE)
    us, xs = [], []
    for gi in groups:
        halves = []
        for hh in range(L // STEPS_PER_TILE):
            acc = jnp.zeros((nc, LANES), F32)
            for l8 in range(STEPS_PER_TILE):
                src = steps[hh * STEPS_PER_TILE + l8]
                shift = ((l8 - gi) % STEPS_PER_TILE) * H
                moved = src if shift == 0 else pltpu.roll(src, shift, axis=1)
                acc = jnp.where(lane_blk == l8, moved, acc)
            halves.append(acc)
        u = jnp.concatenate(halves, axis=1)
        us.append(u)
        xs.append(_dot(u.astype(BF16), wt_ref[gi]))
    for lvl in range(n_levels):
        sh = 1 << lvl
        for gi in groups:
            apow = apow_ref[gi]
            prev = jnp.where(ridx >= sh, pltpu.roll(xs[gi], sh, axis=0), 0.0)
            a_r = apow[lvl:lvl + 1]
            a_i = apow[SCAN_LEVELS_MAX + lvl:SCAN_LEVELS_MAX + lvl + 1]
            xs[gi] = xs[gi] + a_r * prev + a_i * pltpu.roll(prev, S5_STATE, axis=1)
    outs = [jnp.zeros((nc, LANES), F32) for _ in range(L)]
    for gi in groups:
        x_in = jnp.where(ridx >= 1, pltpu.roll(xs[gi], 1, axis=0), 0.0)
        u = us[gi]
        y = _gelu_tanh(_dot(u.astype(BF16), mt_ref[gi]) + _dot(x_in.astype(BF16), vt_ref[gi]) + dvec_ref[gi] * u)
        for l in range(L):
            hh, l8 = divmod(l, STEPS_PER_TILE)
            src = y[:, hh * LANES:(hh + 1) * LANES]
            shift = ((gi - l8) % STEPS_PER_TILE) * H
            moved = src if shift == 0 else pltpu.roll(src, shift, axis=1)
            outs[l] = jnp.where(lane_blk == gi, moved, outs[l])
    for l in range(L):
        y_ref[pl.ds(l, nc, stride=L), :] = outs[l]


def _s5_branch(u, B, S, s5_params):
    G, L = S5_GROUPS, S5_CHUNK
    nc = S // L
    n_levels = max(1, (nc - 1).bit_length())
    assert n_levels <= SCAN_LEVELS_MAX and S5_CHUNK % STEPS_PER_TILE == 0
    mt, wt, vt, apow, dvec = _s5_prep(n_levels, *s5_params)
    n_lt = G // GROUPS_PER_TILE
    blk = lambda *s: pl.BlockSpec((GROUPS_PER_TILE,) + s, lambda b, t: (t, 0, 0))
    io = pl.BlockSpec((S, LANES), lambda b, t: (b, t))
    return pl.pallas_call(
        functools.partial(_s5_scan_kernel, n_levels),
        grid=(B, n_lt),
        in_specs=[io, blk(CH2, CH2), blk(CH2, P2), blk(P2, CH2), blk(2 * SCAN_LEVELS_MAX, P2), blk(1, CH2)],
        out_specs=io,
        out_shape=jax.ShapeDtypeStruct((B * S, S5_WIDTH), F32),
        compiler_params=_cparams("parallel", "parallel"),
        name="s5_chunk_scan",
    )(u, mt, wt, vt, apow, dvec)


def _merge_xattn_kernel(x_ref, o_ref, ys_ref, gm_ref, gs_ref, wmo_ref, wglu_ref, wout_ref, g1_ref, b1_ref,
                        k_ref, v_ref, wq_ref, wo_ref, g2_ref, b2_ref, x2_ref):
    y_mla = _dot(o_ref[0], wmo_ref[...])
    z = _dot(ys_ref[0].astype(BF16), wglu_ref[...])
    y_s5 = z[:, :D_MODEL] * jax.nn.sigmoid(z[:, D_MODEL:])
    merged = (jax.nn.sigmoid(gm_ref[0].astype(F32)) * y_mla
              + jax.nn.sigmoid(gs_ref[0].astype(F32)) * y_s5)
    mix = _dot(merged.astype(BF16), wout_ref[...])
    x1 = _layer_norm(DEEPNORM_ALPHA * x_ref[0] + mix, g1_ref[...], b1_ref[...])
    q = (_dot(x1.astype(BF16), wq_ref[...]) * (XATTN_HEAD_DIM ** -0.5)).astype(BF16)
    k = k_ref[0]
    v = v_ref[0]
    outs = []
    for h in range(XATTN_HEADS):
        sl = slice(h * XATTN_HEAD_DIM, (h + 1) * XATTN_HEAD_DIM)
        s = _dot_nt(q[:, sl], k[:, sl])
        p = jnp.exp(s - jnp.max(s, axis=-1, keepdims=True))
        o = _dot(p.astype(BF16), v[:, sl]) / jnp.sum(p, axis=-1, keepdims=True)
        outs.append(o.astype(BF16))
    xa = _dot(jnp.concatenate(outs, axis=1), wo_ref[...])
    x2_ref[0] = _layer_norm(DEEPNORM_ALPHA * x1 + xa, g2_ref[...], b2_ref[...])


def _memkv_kernel(mem_ref, g_ref, b_ref, w_ref, k_ref, v_ref):
    m = _layer_norm(mem_ref[0], g_ref[...], b_ref[...]).astype(BF16)
    kv = _dot(m, w_ref[...])
    hd = XATTN_HEADS * XATTN_HEAD_DIM
    k_ref[0] = kv[:, :hd].astype(BF16)
    v_ref[0] = kv[:, hd:].astype(BF16)


def _merge_cross_attention(x2d, o, ys, gm, gs, mem, B, S, w_mla_o, w_s5_glu, w_out, ln1_g, ln1_b,
                           mem_ln_g, mem_ln_b, w_xq, w_xkv, w_xo, ln2_g, ln2_b):
    M = mem.shape[1]
    hd = XATTN_HEADS * XATTN_HEAD_DIM
    HP = MLA_HEADS * HEAD_PAD
    row = lambda a: a.reshape(1, -1)
    wmo = jnp.pad(w_mla_o.reshape(MLA_HEADS, V_HEAD, D_MODEL), ((0, 0), (0, HEAD_PAD - V_HEAD), (0, 0)))
    wmo = wmo.reshape(HP, D_MODEL).astype(BF16)
    k, v = pl.pallas_call(
        _memkv_kernel,
        grid=(B,),
        in_specs=[pl.BlockSpec((1, M, D_MODEL), lambda b: (b, 0, 0)), _full((1, D_MODEL)), _full((1, D_MODEL)),
                  _full((D_MODEL, 2 * hd))],
        out_specs=[pl.BlockSpec((1, M, hd), lambda b: (b, 0, 0))] * 2,
        out_shape=[jax.ShapeDtypeStruct((B, M, hd), BF16)] * 2,
        compiler_params=_cparams("parallel"),
        name="memory_kv",
    )(mem, row(mem_ln_g), row(mem_ln_b), w_xkv.astype(BF16))
    tile = lambda w: pl.BlockSpec((1, TM, w), lambda b, i: (b, i, 0))
    kv_spec = pl.BlockSpec((1, M, hd), lambda b, i: (b, 0, 0))
    r3 = lambda a: a.reshape(B, S, a.shape[-1])
    x2 = pl.pallas_call(
        _merge_xattn_kernel,
        grid=(B, S // TM),
        in_specs=[tile(D_MODEL), tile(HP), tile(S5_WIDTH), tile(D_MODEL), tile(D_MODEL),
                  _full((HP, D_MODEL)), _full((S5_WIDTH, 2 * D_MODEL)), _full((D_MODEL, D_MODEL)),
                  _full((1, D_MODEL)), _full((1, D_MODEL)),
                  kv_spec, kv_spec, _full((D_MODEL, hd)), _full((hd, D_MODEL)), _full((1, D_MODEL)), _full((1, D_MODEL))],
        out_specs=tile(D_MODEL),
        out_shape=jax.ShapeDtypeStruct((B, S, D_MODEL), F32),
        compiler_params=_cparams("parallel", "parallel"),
        name="merge_ln1_xattn_ln2",
    )(r3(x2d), r3(o), r3(ys), r3(gm), r3(gs), wmo, w_s5_glu.astype(BF16), w_out.astype(BF16), row(ln1_g), row(ln1_b),
      k, v, w_xq.astype(BF16), w_xo.astype(BF16), row(ln2_g), row(ln2_b))
    return x2.reshape(B * S, D_MODEL)


def _route_sort_kernel(x_ref, w_ref, bias_ref, xs_ref, pos_ref, cnt_ref, off_ref):
    E, per = N_EXPERTS, N_EXPERTS // N_EXPERT_GROUPS
    tm = x_ref.shape[0]
    x = x_ref[...]
    xb = x.astype(BF16)
    x_lo = (x - xb.astype(F32)).astype(BF16)
    w = w_ref[...]
    wb = w.astype(BF16)
    w_lo = (w - wb.astype(F32)).astype(BF16)
    logits = _dot_nt(wb, xb) + (_dot_nt(wb, x_lo) + _dot_nt(w_lo, xb))
    scores = jax.nn.sigmoid(logits)
    sel = scores + bias_ref[...]
    neg = -jnp.inf
    i8 = lax.broadcasted_iota(I32, (per, tm), 0)
    gscore = []
    for g in range(N_EXPERT_GROUPS):
        blk = sel[g * per:(g + 1) * per]
        m1 = jnp.max(blk, axis=0, keepdims=True)
        i1 = jnp.min(jnp.where(blk == m1, i8, per), axis=0, keepdims=True)
        m2 = jnp.max(jnp.where(i8 == i1, neg, blk), axis=0, keepdims=True)
        gscore.append(m1 + m2)
    blocks = []
    for g in range(N_EXPERT_GROUPS):
        ahead = jnp.zeros((1, tm), I32)
        for o in range(N_EXPERT_GROUPS):
            if o == g:
                continue
            before = (gscore[o] >= gscore[g]) if o < g else (gscore[o] > gscore[g])
            ahead = ahead + before.astype(I32)
        blocks.append(jnp.where(ahead < TOPK_GROUPS, sel[g * per:(g + 1) * per], neg))
    cur = jnp.concatenate(blocks, axis=0)
    ie = lax.broadcasted_iota(I32, (E, tm), 0)
    picked = jnp.zeros((E, tm), F32)
    for _ in range(TOP_K):
        m = jnp.max(cur, axis=0, keepdims=True)
        idx = jnp.min(jnp.where(cur == m, ie, E), axis=0, keepdims=True)
        hit = ie == idx
        picked = jnp.where(hit, 1.0, picked)
        cur = jnp.where(hit, neg, cur)
    wsel = scores * picked
    wnorm = wsel / jnp.sum(wsel, axis=0, keepdims=True) * ROUTED_SCALE
    pb = picked.astype(BF16)
    tri = (lax.broadcasted_iota(I32, (tm, tm), 0) <= lax.broadcasted_iota(I32, (tm, tm), 1)).astype(BF16)
    incl = _dot(pb, tri)
    cnt_col = jnp.sum(picked, axis=1, keepdims=True)
    lower = (lax.broadcasted_iota(I32, (E, E), 1) < lax.broadcasted_iota(I32, (E, E), 0)).astype(F32)
    al_col = jnp.floor((cnt_col + (SEG_ALIGN - 1.0)) * (1.0 / SEG_ALIGN)) * SEG_ALIGN
    off_col = _dot_exact(lower, jnp.broadcast_to(al_col, (E, LANES)))[:, 0:1]
    pos = off_col + incl - 1.0
    q = jnp.floor(pos * (1.0 / POS_RADIX))
    pq = jnp.concatenate([jnp.where(picked > 0.0, POS_RADIX * q, POS_RADIX * POS_NONE_Q),
                          jnp.where(picked > 0.0, pos - POS_RADIX * q, 0.0)], axis=0)
    pos_ref[...] = pq
    pqb = pq.astype(BF16)
    cnt_ref[0] = jnp.broadcast_to(cnt_col, (E, LANES)).astype(I32)
    off_ref[0] = jnp.broadcast_to(off_col, (E, LANES)).astype(I32)
    cnt_row = _dot_nt(jnp.ones((8, tm), BF16), pb)
    al_row = jnp.floor((cnt_row + (SEG_ALIGN - 1.0)) * (1.0 / SEG_ALIGN)) * SEG_ALIGN
    er = lax.broadcasted_iota(I32, (E, 4 * E), 0)
    ec = lax.broadcasted_iota(I32, (E, 4 * E), 1) % E
    off_row4 = _dot_exact(al_row, (er < ec).astype(F32))[0:1]
    end_row4 = off_row4 + _dot_exact(al_row, (er == ec).astype(F32))[0:1]
    w_hi = wnorm.astype(BF16).astype(F32)
    w_r1 = wnorm - w_hi
    w_mid = w_r1.astype(BF16).astype(F32)
    w_lo = w_r1 - w_mid
    w4_t = jnp.concatenate([w_hi, w_mid, w_lo, jnp.zeros_like(w_lo)], axis=0).T
    xw = jnp.concatenate([xb, w4_t.astype(BF16)], axis=1)
    half = D_MODEL // 2

    def build(c, carry):
        r0 = pl.multiple_of(c * OH_ROWS, OH_ROWS)
        jr = (lax.broadcasted_iota(I32, (OH_ROWS, 4 * E), 0) + r0).astype(F32)
        member4 = jnp.where(jr >= off_row4, jnp.where(jr < end_row4, 1.0, 0.0), 0.0)
        target = _dot(member4[:, :2 * E].astype(BF16), pqb)
        j = (lax.broadcasted_iota(I32, (OH_ROWS, tm), 0) + r0).astype(F32)
        ohb = jnp.where(target == j, 1.0, 0.0).astype(BF16)
        rows = _dot(ohb, xw)
        xs_ref[0, pl.ds(r0, OH_ROWS), 0:half] = _pack_bf16_pair(rows[:, :half], rows[:, half:D_MODEL])
        wrow = jnp.sum(member4 * rows[:, D_MODEL:], axis=1, keepdims=True)
        xs_ref[0, pl.ds(r0, OH_ROWS), half:] = pltpu.bitcast(jnp.broadcast_to(wrow, (OH_ROWS, LANES)), I32)
        return carry

    lax.fori_loop(0, SORT_ROWS // OH_ROWS, build, 0)
    xs_ref[0, SORT_ROWS:, :] = jnp.zeros((PAD_ROWS, XW), I32)


def _route_sort(x2, w_router, router_bias):
    T = x2.shape[0]
    E = N_EXPERTS
    nt = T // TW
    seg = pl.BlockSpec((1, E, LANES), lambda i: (i, 0, 0))
    return pl.pallas_call(
        _route_sort_kernel,
        grid=(nt,),
        in_specs=[pl.BlockSpec((TW, D_MODEL), lambda i: (i, 0)), _full((E, D_MODEL)), _full((E, 1))],
        out_specs=[pl.BlockSpec((1, TS, XW), lambda i: (i, 0, 0)),
                   pl.BlockSpec((2 * E, TW), lambda i: (0, i)), seg, seg],
        out_shape=[jax.ShapeDtypeStruct((nt, TS, XW), I32), jax.ShapeDtypeStruct((2 * E, T), F32),
                   jax.ShapeDtypeStruct((nt, E, LANES), I32), jax.ShapeDtypeStruct((nt, E, LANES), I32)],
        compiler_params=_cparams("parallel"),
        name="route_local_sort",
    )(x2, w_router.T, router_bias.reshape(E, 1))


def _plan_kernel(cnt_ref, off_ref, pa_ref, ce_ref, gn_ref, nch_ref):
    E, nt = cnt_ref.shape
    n_rows = pa_ref.shape[0]
    cnt = cnt_ref[...].astype(F32)
    off = off_ref[...].astype(F32)
    npc = jnp.floor((cnt + (PR - 1.0)) * (1.0 / PR))
    tri = (lax.broadcasted_iota(I32, (nt, nt), 0) <= lax.broadcasted_iota(I32, (nt, nt), 1)).astype(F32)
    p_incl = _dot_exact(npc, tri)
    total = p_incl[:, nt - 1:nt]
    n_ch = jnp.floor((total + (PPC - 1.0)) * (1.0 / PPC))
    lower = (lax.broadcasted_iota(I32, (E, E), 1) < lax.broadcasted_iota(I32, (E, E), 0)).astype(F32)
    wide = lambda col: jnp.broadcast_to(col, (E, LANES))
    g0 = _dot_exact(lower, wide(n_ch))[:, 0:1]
    n_total = jnp.sum(n_ch, axis=0, keepdims=True)
    g0_row = wide(g0).T[0:1, :E]
    g1_row = g0_row + wide(n_ch).T[0:1, :E]
    g_e = lax.broadcasted_iota(I32, (n_rows, E), 0).astype(F32)
    sel = jnp.where(g_e >= g0_row, jnp.where(g_e < g1_row, 1.0, 0.0), 0.0)
    tile_base = lax.broadcasted_iota(I32, (E, nt), 1).astype(F32) * TS
    a_mat = tile_base + off - PR * (p_incl - npc)
    p_g = _dot_exact(sel, p_incl)
    a_g = _dot_exact(sel, a_mat)
    e_idx = lax.broadcasted_iota(I32, (E, LANES), 0).astype(F32)
    pick = lambda col: _dot_exact(sel, col)[:, 0:1]
    g0_g, tot_g, e_g = pick(wide(g0)), pick(wide(total)), pick(e_idx)
    gn_g = pick(wide(g0 + n_ch))
    g_p = lax.broadcasted_iota(I32, (n_rows, PPC), 0).astype(F32)
    q = (g_p - g0_g) * PPC + lax.broadcasted_iota(I32, (n_rows, PPC), 1).astype(F32)
    addr = jnp.full((n_rows, PPC), -float(SEG_ALIGN), F32)
    lo = jnp.zeros((n_rows, 1), F32)
    for i in range(nt):
        hi = p_g[:, i:i + 1]
        addr = jnp.where(q >= lo, jnp.where(q < hi, a_g[:, i:i + 1] + PR * q, addr), addr)
        lo = hi
    pa_ref[...] = addr.astype(I32)
    g_col = lax.broadcasted_iota(I32, (n_rows, LANES), 0).astype(F32)
    ce_ref[...] = jnp.where(g_col < n_total, jnp.broadcast_to(e_g, (n_rows, LANES)), E - 1.0).astype(I32)
    gn_ref[...] = jnp.broadcast_to(gn_g, (n_rows, LANES)).astype(I32)
    nch_ref[...] = jnp.broadcast_to(n_total, nch_ref.shape).astype(I32)


def _plan(cnt, off, n_chunks_max):
    n_rows = (n_chunks_max + 7) // 8 * 8
    pa, ce, gn, nch = pl.pallas_call(
        _plan_kernel,
        out_shape=[jax.ShapeDtypeStruct((n_rows, PPC), I32), jax.ShapeDtypeStruct((n_rows, LANES), I32),
                   jax.ShapeDtypeStruct((n_rows, LANES), I32), jax.ShapeDtypeStruct((8, LANES), I32)],
        name="moe_plan",
    )(cnt, off)
    return pa.reshape(n_rows * PPC), ce[:, 0], gn[:, 0], nch[0, :1]


def _unpack_bf16_pair(w):
    lo = pltpu.bitcast(lax.shift_left(w, jnp.int32(16)), F32).astype(BF16)
    hi = pltpu.bitcast(w & jnp.int32(-65536), F32).astype(BF16)
    return lo, hi


def _pack_bf16_pair(lo, hi):
    lo_bits = lax.shift_right_logical(pltpu.bitcast(lo.astype(BF16).astype(F32), I32), jnp.int32(16))
    hi_bits = pltpu.bitcast(hi.astype(BF16).astype(F32), I32)
    return hi_bits | lo_bits


def _expert_kernel(ce_ref, gn_ref, pa_ref, nch_ref, tot_ref, xs_hbm, wgu_hbm, wdn_hbm, y_hbm,
                   xbuf, ybuf, zbuf, wgu_f, wdn_f, wgu_b, wdn_b, wslot, gsem, wsem, zsem, esem):
    g = pl.program_id(0)
    n = nch_ref[0]
    slot = g % N_SLOTS
    half = D_MODEL // 2

    dummy_base = (tot_ref.shape[0] - 1) * TS

    def gather_copy(s, a, jp):
        return pltpu.make_async_copy(xs_hbm.at[pl.ds(pl.multiple_of(a, SEG_ALIGN), PR), :],
                                     xbuf.at[s, pl.ds(jp * PR, PR), :], gsem.at[s])

    def write_copy(s, a, jp):
        return pltpu.make_async_copy(ybuf.at[s, pl.ds(jp * PR, PR), :],
                                     y_hbm.at[pl.ds(pl.multiple_of(a, SEG_ALIGN), PR), :], wsem.at[s])

    def gather_start(c, s):
        for jp in range(PPC):
            gather_copy(s, jnp.maximum(pa_ref[c * PPC + jp], 0), jp).start()

    def write_start(c, s):
        for jp in range(PPC):
            a = pa_ref[c * PPC + jp]
            write_copy(s, jnp.where(a >= 0, a, dummy_base + jp * PR), jp).start()

    def gather_wait(s):
        for jp in range(PPC):
            gather_copy(s, 0, jp).wait()

    def write_wait(s):
        for jp in range(PPC):
            write_copy(s, 0, jp).wait()

    def weight_copies(e, s):
        return (pltpu.make_async_copy(wgu_hbm.at[e], wgu_f.at[s], esem.at[s]),
                pltpu.make_async_copy(wdn_hbm.at[e], wdn_f.at[s], esem.at[s]))

    @pl.when(g == 0)
    def _():
        wslot[0] = 0
        for cp in weight_copies(ce_ref[0], 0):
            cp.start()
        gather_start(0, 0)
        ybuf[...] = jnp.zeros_like(ybuf)
        zbuf[...] = jnp.zeros_like(zbuf)

        def zero_copy(row):
            return pltpu.make_async_copy(zbuf, y_hbm.at[pl.ds(pl.multiple_of(row, SEG_ALIGN), SEG_ALIGN), :], zsem)

        def tail_of(i):
            return i * TS + tot_ref[i], (TS - tot_ref[i]) // SEG_ALIGN

        def zero_start(i, c):
            base, cnt8 = tail_of(i)
            lax.fori_loop(0, cnt8, lambda q, z: (zero_copy(base + q * SEG_ALIGN).start(), z)[1], 0)
            return c

        def zero_wait(i, c):
            base, cnt8 = tail_of(i)
            lax.fori_loop(0, cnt8, lambda q, z: (zero_copy(base + q * SEG_ALIGN).wait(), z)[1], 0)
            return c

        lax.fori_loop(0, tot_ref.shape[0], zero_start, 0)
        lax.fori_loop(0, tot_ref.shape[0], zero_wait, 0)

    for c in range(1, GATHER_AHEAD):
        @pl.when((g == 0) & (c < n))
        def _(c=c):
            gather_start(c, c)

    @pl.when(g + GATHER_AHEAD < n)
    def _():
        gather_start(g + GATHER_AHEAD, (g + GATHER_AHEAD) % N_SLOTS)

    e_cur = ce_ref[g]
    e_prev = ce_ref[jnp.maximum(g - 1, 0)]
    e_prev2 = ce_ref[jnp.maximum(g - 2, 0)]

    @pl.when(g < n)
    def _():
        @pl.when((g == 0) | (e_cur != e_prev))
        def _():
            s = wslot[0]
            for cp in weight_copies(e_cur, s):
                cp.wait()
            wgu_b[...] = wgu_f[s].astype(BF16)
            wdn_b[...] = wdn_f[s].astype(BF16)
            g_next = gn_ref[g]

            @pl.when(g_next < n)
            def _():
                for cp in weight_copies(ce_ref[g_next], 1 - s):
                    cp.start()

            wslot[0] = 1 - s

        gather_wait(slot)

        @pl.when((g >= 2) & (e_prev == e_prev2))
        def _():
            write_wait((g - 2) % N_SLOTS)

        @pl.when((g >= 1) & (e_cur != e_prev))
        def _():
            write_wait((g - 1) % N_SLOTS)

        def swiglu_rows(rows):
            xw = xbuf[slot, :rows, :]
            lo, hi = _unpack_bf16_pair(xw[:, :half])
            h = _dot(jnp.concatenate([lo, hi], axis=1), wgu_b[...])
            gate, up = h[:, :EXPERT_FF], h[:, EXPERT_FF:]
            act = (gate * jax.nn.sigmoid(gate) * up).astype(BF16)
            y = _dot(act, wdn_b[...])
            w_row = pltpu.bitcast(xw[:, half:], F32)
            y = y * jnp.concatenate([w_row] * (D_MODEL // LANES), axis=1)
            ybuf[slot, :rows, :] = _pack_bf16_pair(y[:, :half], y[:, half:])

        second_half_used = pa_ref[g * PPC + PPC // 2] >= 0

        @pl.when(second_half_used)
        def _():
            swiglu_rows(CR)

        @pl.when(jnp.logical_not(second_half_used))
        def _():
            swiglu_rows(CR // 2)

        write_start(g, slot)

    @pl.when(g == pl.num_programs(0) - 1)
    def _():
        @pl.when((n >= 2) & (ce_ref[jnp.maximum(n - 1, 0)] == ce_ref[jnp.maximum(n - 2, 0)]))
        def _():
            write_wait((n - 2) % N_SLOTS)

        write_wait((n - 1) % N_SLOTS)


def _experts(ce, gn, pa, nch, tot, xs, w_exp_gu, w_exp_down):
    n_rows = xs.shape[0]
    half = D_MODEL // 2
    hbm = pl.BlockSpec(memory_space=pl.ANY)
    grid_spec = pltpu.PrefetchScalarGridSpec(
        num_scalar_prefetch=5,
        grid=(ce.shape[0],),
        in_specs=[hbm, hbm, hbm],
        out_specs=hbm,
        scratch_shapes=[pltpu.VMEM((N_SLOTS, CR, XW), I32), pltpu.VMEM((N_SLOTS, CR, half), I32),
                        pltpu.VMEM((SEG_ALIGN, half), I32),
                        pltpu.VMEM((2, D_MODEL, 2 * EXPERT_FF), F32), pltpu.VMEM((2, EXPERT_FF, D_MODEL), F32),
                        pltpu.VMEM((D_MODEL, 2 * EXPERT_FF), BF16), pltpu.VMEM((EXPERT_FF, D_MODEL), BF16),
                        pltpu.SMEM((1,), I32),
                        pltpu.SemaphoreType.DMA((N_SLOTS,)), pltpu.SemaphoreType.DMA((N_SLOTS,)),
                        pltpu.SemaphoreType.DMA(()), pltpu.SemaphoreType.DMA((2,))],
    )
    return pl.pallas_call(
        _expert_kernel,
        grid_spec=grid_spec,
        out_shape=jax.ShapeDtypeStruct((n_rows + TS, half), I32),
        compiler_params=_cparams("arbitrary"),
        name="moe_grouped_swiglu",
    )(ce, gn, pa, nch, tot, xs, w_exp_gu, w_exp_down)


def _combine_kernel(tot_ref, y_ref, pos_ref, cnt_ref, off_ref, x2_ref, wsg_ref, wsd_ref, g_ref, b_ref, out_ref,
                    acc_lo, acc_hi):
    tm = x2_ref.shape[0]
    E = N_EXPERTS
    n_sorted = tot_ref[pl.program_id(0)]
    x2 = x2_ref[...]
    h = _dot(x2.astype(BF16), wsg_ref[...])
    gate, up = h[:, :SHARED_FF], h[:, SHARED_FF:]
    shared = _dot((gate * jax.nn.sigmoid(gate) * up).astype(BF16), wsd_ref[...])
    pq_t = pos_ref[...].T.astype(BF16)
    off_col = off_ref[0][:, 0:1].astype(F32)
    cnt_col = cnt_ref[0][:, 0:1].astype(F32)
    end_col = off_col + jnp.floor((cnt_col + (SEG_ALIGN - 1.0)) * (1.0 / SEG_ALIGN)) * SEG_ALIGN
    off_col2 = jnp.concatenate([off_col, off_col], axis=0)
    end_col2 = jnp.concatenate([end_col, end_col], axis=0)
    acc_lo[...] = jnp.zeros_like(acc_lo)
    acc_hi[...] = jnp.zeros_like(acc_hi)

    def gather_back(c, carry):
        r0 = pl.multiple_of(c * OH_ROWS, OH_ROWS)
        jc = (lax.broadcasted_iota(I32, (2 * E, OH_ROWS), 1) + r0).astype(F32)
        member2 = jnp.where(jc >= off_col2, jnp.where(jc < end_col2, 1.0, 0.0), 0.0)
        target = _dot(pq_t, member2.astype(BF16))
        j = (lax.broadcasted_iota(I32, (tm, OH_ROWS), 1) + r0).astype(F32)
        ohb = jnp.where(target == j, 1.0, 0.0).astype(BF16)
        yw = y_ref[0, pl.ds(r0, OH_ROWS), :]
        row = lax.broadcasted_iota(I32, yw.shape, 0) + r0
        lo, hi = _unpack_bf16_pair(jnp.where(row < n_sorted, yw, 0))
        acc_lo[...] += _dot(ohb, lo)
        acc_hi[...] += _dot(ohb, hi)
        return carry

    lax.fori_loop(0, SORT_ROWS // OH_ROWS, gather_back, 0)
    ff = shared + jnp.concatenate([acc_lo[...], acc_hi[...]], axis=1)
    out_ref[...] = _layer_norm(DEEPNORM_ALPHA * x2 + ff, g_ref[...], b_ref[...])


def _combine(tot, y, pos, cnt, off, x2, w_sh_gu, w_sh_down, ln_g, ln_b):
    T = x2.shape[0]
    nt = T // TW
    half = D_MODEL // 2
    tile = lambda w: pl.BlockSpec((TW, w), lambda i: (i, 0))
    return pl.pallas_call(
        _combine_kernel,
        grid=(nt,),
        in_specs=[pl.BlockSpec(memory_space=pltpu.SMEM),
                  pl.BlockSpec((1, TS, half), lambda i: (i, 0, 0)),
                  pl.BlockSpec((2 * N_EXPERTS, TW), lambda i: (0, i)),
                  pl.BlockSpec((1, N_EXPERTS, LANES), lambda i: (i, 0, 0)),
                  pl.BlockSpec((1, N_EXPERTS, LANES), lambda i: (i, 0, 0)), tile(D_MODEL),
                  _full((D_MODEL, 2 * SHARED_FF)), _full((SHARED_FF, D_MODEL)),
                  _full((1, D_MODEL)), _full((1, D_MODEL))],
        out_specs=tile(D_MODEL),
        out_shape=jax.ShapeDtypeStruct((T, D_MODEL), F32),
        scratch_shapes=[pltpu.VMEM((TW, half), F32), pltpu.VMEM((TW, half), F32)],
        compiler_params=_cparams("parallel"),
        name="moe_combine_shared_ln3",
    )(tot, y.reshape(nt + 1, TS, half), pos, cnt, off, x2, w_sh_gu.astype(BF16), w_sh_down.astype(BF16),
      ln_g.reshape(1, -1), ln_b.reshape(1, -1))


def _moe(x2, w_router, router_bias, w_exp_gu, w_exp_down, w_sh_gu, w_sh_down, ln_g, ln_b):
    T = x2.shape[0]
    E = N_EXPERTS
    nt = T // TW
    xs, pos, cnt, off = _route_sort(x2, w_router, router_bias)
    n_pieces_max = (T * TOP_K + nt * E * (PR - 1)) // PR
    n_chunks_max = n_pieces_max // PPC + E
    pa, ce, gn, nch = _plan(cnt[:, :, 0].T, off[:, :, 0].T, n_chunks_max)
    last_al = (cnt[:, E - 1, 0] + SEG_ALIGN - 1) // SEG_ALIGN * SEG_ALIGN
    tot = jnp.concatenate([off[:, E - 1, 0] + last_al, jnp.zeros((1,), I32)])
    y = _experts(ce, gn, pa, nch, tot, xs.reshape(nt * TS, XW), w_exp_gu, w_exp_down)
    return _combine(tot, y, pos, cnt, off, x2, w_sh_gu, w_sh_down, ln_g, ln_b)


def kernel(x, mem, positions, w_in, q_norm_g, kv_norm_g, w_uq, w_ukv, w_mla_o, s5_a_re, s5_a_im, s5_log_dt, s5_b_re, s5_b_im, s5_c_re, s5_c_im, s5_d, w_s5_glu, w_out, ln1_g, ln1_b, mem_ln_g, mem_ln_b, w_xq, w_xkv, w_xo, ln2_g, ln2_b, w_router, router_bias, w_exp_gu, w_exp_down, w_sh_gu, w_sh_down, ln3_g, ln3_b):
    B, S, D = x.shape
    assert D == D_MODEL and S % TQ == 0 and (B * S) % TM == 0 and S % S5_CHUNK == 0
    xc = x.reshape(B * S, D)
    for l in range(w_in.shape[0]):
        tabs = _rope_tables(positions)
        q, k, v, u, gm, gs = _inproj(xc, tabs, w_in[l], q_norm_g[l], kv_norm_g[l], w_uq[l], w_ukv[l])
        o = _attention(q, k, v, B, S)
        ys = _s5_branch(u, B, S, (s5_a_re[l], s5_a_im[l], s5_log_dt[l], s5_b_re[l], s5_b_im[l],
                                  s5_c_re[l], s5_c_im[l], s5_d[l]))
        x2 = _merge_cross_attention(xc, o, ys, gm, gs, mem, B, S, w_mla_o[l], w_s5_glu[l], w_out[l],
                                    ln1_g[l], ln1_b[l], mem_ln_g[l], mem_ln_b[l], w_xq[l], w_xkv[l], w_xo[l],
                                    ln2_g[l], ln2_b[l])
        xc = _moe(x2, w_router[l], router_bias[l], w_exp_gu[l], w_exp_down[l], w_sh_gu[l],
                  w_sh_down[l], ln3_g[l], ln3_b[l])
    return xc.reshape(B, S, D)
```

```python
import functools
import math

import jax
import jax.numpy as jnp
import numpy as np
from jax import lax
from jax.experimental import pallas as pl
from jax.experimental.pallas import tpu as pltpu

F32 = jnp.float32
BF16 = jnp.bfloat16
I32 = jnp.int32

D_MODEL = 1024
MLA_HEADS = 8
QK_NOPE = 64
QK_ROPE = 32
V_HEAD = 64
Q_LORA = 256
KV_LORA = 256
ROPE_THETA = 10000.0
S5_GROUP_CH = 16
S5_WIDTH = 512
S5_GROUPS = 32
S5_STATE = 64
XATTN_HEADS = 4
XATTN_HEAD_DIM = 128
N_EXPERTS = 64
TOP_K = 8
N_EXPERT_GROUPS = 8
TOPK_GROUPS = 4
EXPERT_FF = 256
SHARED_FF = 256
ROUTED_SCALE = 2.5
LN_EPS = 1e-5
RMS_EPS = 1e-6
DEPTH = 1
DEEPNORM_ALPHA = (2.0 * DEPTH) ** 0.25

LANES = 128
HEAD_PAD = 128
ROPE_LO = QK_NOPE
ROPE_HALF = QK_ROPE // 2

TM = 512
TQ = 1024
KV_GROUPS = (2, 1)
S5_CHUNK = 16
TW = 512
OH_ROWS = 512
PR = 16
PAD_ROWS = PR
SEG_ALIGN = 8
SORT_ROWS = TOP_K * TW + N_EXPERTS * SEG_ALIGN
TS = SORT_ROWS + PAD_ROWS
XW = D_MODEL // 2 + LANES
CR = 512
PPC = CR // PR
GATHER_AHEAD = 4
N_SLOTS = GATHER_AHEAD + 1
POS_RADIX = 64
POS_NONE_Q = 127
assert SORT_ROWS <= POS_RADIX * POS_NONE_Q and SORT_ROWS % OH_ROWS == 0
VMEM_LIMIT = 48 * 1024 * 1024


def _cparams(*sem):
    return pltpu.CompilerParams(dimension_semantics=sem, vmem_limit_bytes=VMEM_LIMIT)


def _dot(a, b):
    return jnp.dot(a, b, preferred_element_type=F32)


def _dot_nt(a, b, precision=None):
    return lax.dot_general(a, b, (((1,), (1,)), ((), ())), preferred_element_type=F32,
                           precision=precision)


def _dot_exact(a, b):
    return jnp.dot(a, b, preferred_element_type=F32, precision=lax.Precision.HIGHEST)


def _layer_norm(h, g, b):
    mu = jnp.mean(h, axis=-1, keepdims=True)
    c = h - mu
    var = jnp.mean(c * c, axis=-1, keepdims=True)
    return c * lax.rsqrt(var + LN_EPS) * g + b


def _rms_norm(h, g):
    return h * lax.rsqrt(jnp.mean(h * h, axis=-1, keepdims=True) + RMS_EPS) * g


def _full(shape):
    n = len(shape)
    return pl.BlockSpec(shape, lambda *_: (0,) * n)


def _rope_angle_kernel(pos_ref, invf_ref, cos_ref, sin_ref):
    ang = pos_ref[...].astype(F32) * invf_ref[...]
    cos_ref[...] = jnp.cos(ang)
    sin_ref[...] = jnp.sin(ang)


def _rope_tables(positions):
    T = positions.size
    rows = T * ROPE_HALF // LANES
    pos_rep = jnp.repeat(positions.reshape(T), ROPE_HALF).reshape(rows, LANES)
    inv_freq = ROPE_THETA ** (-jnp.arange(0, QK_ROPE, 2, dtype=F32) / QK_ROPE)
    invf = jnp.tile(inv_freq, LANES // ROPE_HALF).reshape(1, LANES)
    cos, sin = pl.pallas_call(
        _rope_angle_kernel,
        out_shape=(jax.ShapeDtypeStruct((rows, LANES), F32),) * 2,
        name="rope_angles",
    )(pos_rep, invf)
    cos_sin = jnp.concatenate([cos.reshape(T, ROPE_HALF), sin.reshape(T, ROPE_HALF)], axis=1)
    f = np.arange(ROPE_HALF)
    spread = np.zeros((QK_ROPE, 3 * HEAD_PAD), np.float32)
    spread[f, ROPE_LO + f] = 1.0
    spread[f, ROPE_LO + ROPE_HALF + f] = 1.0
    spread[ROPE_HALF + f, HEAD_PAD + ROPE_LO + f] = -1.0
    spread[ROPE_HALF + f, 2 * HEAD_PAD + ROPE_LO + ROPE_HALF + f] = 1.0
    outside = np.ones((1, HEAD_PAD), np.float32)
    outside[0, ROPE_LO:ROPE_LO + QK_ROPE] = 0.0
    return cos_sin, jnp.asarray(spread), jnp.asarray(outside)


def _rope(x, c_tab, s_up, s_dn):
    return (x * c_tab + pltpu.roll(x, HEAD_PAD - ROPE_HALF, axis=1) * s_up
            + pltpu.roll(x, ROPE_HALF, axis=1) * s_dn)


def _inproj_kernel(x_ref, cs_ref, spread_ref, outside_ref, wlat_ref, wu_ref, wgm_ref, wgs_ref, qg_ref, kvg_ref,
                   wuq_ref, wuk_ref, wuv_ref, q_ref, k_ref, v_ref, u_ref, gm_ref, gs_ref):
    xb = x_ref[...].astype(BF16)
    lat = _dot(xb, wlat_ref[...])
    qn = _rms_norm(lat[:, :Q_LORA], qg_ref[...]).astype(BF16)
    kvn = _rms_norm(lat[:, Q_LORA:Q_LORA + KV_LORA], kvg_ref[...]).astype(BF16)
    tabs = _dot_exact(cs_ref[...], spread_ref[...])
    c_tab = tabs[:, :HEAD_PAD] + outside_ref[...]
    s_up, s_dn = tabs[:, HEAD_PAD:2 * HEAD_PAD], tabs[:, 2 * HEAD_PAD:]
    k_rope = _rope(lat[:, Q_LORA + KV_LORA:], c_tab, s_up, s_dn)
    q = _dot(qn, wuq_ref[...])
    k = _dot(kvn, wuk_ref[...])
    scale = (QK_NOPE + QK_ROPE) ** -0.5 * math.log2(math.e)
    for h in range(MLA_HEADS):
        sl = slice(h * HEAD_PAD, (h + 1) * HEAD_PAD)
        q_ref[:, sl] = (_rope(q[:, sl], c_tab, s_up, s_dn) * scale).astype(BF16)
        k_ref[:, sl] = (k[:, sl] + k_rope).astype(BF16)
    v = _dot(kvn, wuv_ref[...])
    ones_lane = lax.broadcasted_iota(I32, (1, v.shape[1]), 1) % HEAD_PAD == V_HEAD
    v_ref[...] = jnp.where(ones_lane, 1.0, v).astype(BF16)
    u_ref[...] = _dot(xb, wu_ref[...])
    gm_ref[...] = _dot(xb, wgm_ref[...]).astype(BF16)
    gs_ref[...] = _dot(xb, wgs_ref[...]).astype(BF16)


def _pad_heads(w, head_w, lo_w):
    K = w.shape[0]
    w = w.reshape(K, MLA_HEADS, head_w)[:, :, :lo_w]
    w = jnp.pad(w, ((0, 0), (0, 0), (0, HEAD_PAD - lo_w)))
    return w.reshape(K, MLA_HEADS * HEAD_PAD)


def _inproj(x2d, tabs, w_in, q_norm_g, kv_norm_g, w_uq, w_ukv):
    T = x2d.shape[0]
    o_rope = Q_LORA + KV_LORA
    o_u = o_rope + QK_ROPE
    o_gm = o_u + S5_WIDTH
    o_gs = o_gm + D_MODEL
    w_rope = jnp.pad(w_in[:, o_rope:o_u], ((0, 0), (ROPE_LO, HEAD_PAD - ROPE_LO - QK_ROPE)))
    w_lat = jnp.concatenate([w_in[:, :o_rope], w_rope], axis=1).astype(BF16)
    w_u = w_in[:, o_u:o_gm].astype(BF16)
    w_gm = w_in[:, o_gm:o_gs].astype(BF16)
    w_gs = w_in[:, o_gs:].astype(BF16)
    wuq = _pad_heads(w_uq, QK_NOPE + QK_ROPE, QK_NOPE + QK_ROPE).astype(BF16)
    kv3 = w_ukv.reshape(KV_LORA, MLA_HEADS, QK_NOPE + V_HEAD)
    wuk = _pad_heads(kv3[:, :, :QK_NOPE].reshape(KV_LORA, -1), QK_NOPE, QK_NOPE).astype(BF16)
    wuv = _pad_heads(kv3[:, :, QK_NOPE:].reshape(KV_LORA, -1), V_HEAD, V_HEAD).astype(BF16)
    HP = MLA_HEADS * HEAD_PAD
    tile = lambda w: pl.BlockSpec((TM, w), lambda i: (i, 0))
    return pl.pallas_call(
        _inproj_kernel,
        grid=(T // TM,),
        in_specs=[tile(D_MODEL), tile(QK_ROPE), _full((QK_ROPE, 3 * HEAD_PAD)), _full((1, HEAD_PAD)),
                  _full(w_lat.shape), _full(w_u.shape), _full(w_gm.shape), _full(w_gs.shape),
                  _full((1, Q_LORA)), _full((1, KV_LORA)),
                  _full(wuq.shape), _full(wuk.shape), _full(wuv.shape)],
        out_specs=[tile(HP), tile(HP), tile(HP), tile(S5_WIDTH), tile(D_MODEL), tile(D_MODEL)],
        out_shape=[jax.ShapeDtypeStruct((T, HP), BF16)] * 3
        + [jax.ShapeDtypeStruct((T, S5_WIDTH), F32)]
        + [jax.ShapeDtypeStruct((T, D_MODEL), BF16)] * 2,
        compiler_params=_cparams("parallel"),
        name="inproj_mla_prep",
    )(x2d, *tabs, w_lat, w_u, w_gm, w_gs, q_norm_g.reshape(1, -1), kv_norm_g.reshape(1, -1),
      wuq, wuk, wuv)


def _attn_kernel(q_ref, k_ref, v_ref, o_ref):
    qi = pl.program_id(2)
    q = q_ref[0]

    def update(qq, k, v, carry, row_offset=None):
        m, acc = carry
        s = _dot_nt(qq, k)
        if row_offset is not None:
            row = lax.broadcasted_iota(I32, s.shape, 0) + row_offset
            col = lax.broadcasted_iota(I32, s.shape, 1)
            s = jnp.where(col <= row, s, -jnp.inf)
        m_new = jnp.maximum(m, jnp.max(s, axis=-1, keepdims=True))
        p = jnp.exp2(s - m_new)
        acc = jnp.exp2(m - m_new) * acc + _dot(p.astype(BF16), v)
        return m_new, acc

    def step(blk, n_blk, carry):
        tk = n_blk * TQ
        start = pl.multiple_of(blk * TQ, TQ)
        return update(q, k_ref[0, pl.ds(start, tk), :], v_ref[0, pl.ds(start, tk), :], carry)

    carry = (jnp.full((TQ, 1), -jnp.inf, F32), jnp.zeros((TQ, HEAD_PAD), F32))
    done = 0
    for n_blk in KV_GROUPS:
        base = done
        n_it = (qi - base) // n_blk
        carry = lax.fori_loop(0, n_it, lambda j, c, base=base, n_blk=n_blk: step(base + j * n_blk, n_blk, c), carry)
        done = base + n_it * n_blk
    hq = TQ // 2
    d0 = pl.multiple_of(qi * TQ, TQ)
    m, acc = carry
    _, acc_lo = update(q[:hq], k_ref[0, pl.ds(d0, hq), :], v_ref[0, pl.ds(d0, hq), :], (m[:hq], acc[:hq]), 0)
    _, acc_hi = update(q[hq:], k_ref[0, pl.ds(d0, TQ), :], v_ref[0, pl.ds(d0, TQ), :], (m[hq:], acc[hq:]), hq)
    o_ref[0, :hq, :] = (acc_lo / acc_lo[:, V_HEAD:V_HEAD + 1]).astype(BF16)
    o_ref[0, hq:, :] = (acc_hi / acc_hi[:, V_HEAD:V_HEAD + 1]).astype(BF16)


def _attention(q, k, v, B, S):
    HP = MLA_HEADS * HEAD_PAD
    q, k, v = (a.reshape(B, S, HP) for a in (q, k, v))
    o = pl.pallas_call(
        _attn_kernel,
        grid=(B, MLA_HEADS, S // TQ),
        in_specs=[pl.BlockSpec((1, TQ, HEAD_PAD), lambda b, h, i: (b, i, h)),
                  pl.BlockSpec((1, S, HEAD_PAD), lambda b, h, i: (b, 0, h)),
                  pl.BlockSpec((1, S, HEAD_PAD), lambda b, h, i: (b, 0, h))],
        out_specs=pl.BlockSpec((1, TQ, HEAD_PAD), lambda b, h, i: (b, i, h)),
        out_shape=jax.ShapeDtypeStruct((B, S, HP), BF16),
        compiler_params=_cparams("parallel", "parallel", "arbitrary"),
        name="mla_flash_attention",
    )(q, k, v)
    return o.reshape(B * S, HP)


CH2 = S5_CHUNK * S5_GROUP_CH
P2 = 2 * S5_STATE
SCAN_LEVELS_MAX = 16


S5_PREP_GROUPS = 4


def _s5_prep_kernel(n_levels, *refs):
    *io_refs, mt_acc = refs
    for k in range(S5_PREP_GROUPS):
        _s5_prep_group(n_levels, *(r.at[pl.ds(k, 1)] for r in io_refs), mt_acc)


def _s5_prep_group(n_levels, arow_ref, acol_ref, bt_ref, ct_ref, d_ref,
                   mt_ref, wt_ref, vt_ref, apow_ref, dvec_ref, mt_acc):
    P, H, L = S5_STATE, S5_GROUP_CH, S5_CHUNK

    def powers(ar, ai, dt, n):
        e = jnp.exp(n * (ar * dt))
        ang = n * (ai * dt)
        return e * jnp.cos(ang), e * jnp.sin(ang)

    def zoh_coef(ar, ai, dt):
        pr, pi = powers(ar, ai, dt, 1.0)
        nr, ni = pr - 1.0, pi
        den = ar * ar + ai * ai
        return (nr * ar + ni * ai) / den, (ni * ar - nr * ai) / den

    row = arow_ref[0]
    ar, ai, dt = row[0:1], row[1:2], jnp.exp(row[2:3])
    cr, ci = zoh_coef(ar, ai, dt)
    bt = bt_ref[0]
    btr = jnp.concatenate([bt[:H], bt[:H]], axis=1)
    bti = jnp.concatenate([bt[H:], bt[H:]], axis=1)
    bbr = cr * btr - ci * bti
    bbi = cr * bti + ci * btr
    lane = lax.broadcasted_iota(I32, (1, P2), 1)
    is_re = lane < P
    ridx = lax.broadcasted_iota(I32, (CH2, P2), 0) // H
    pwr, pwi = powers(ar, ai, dt, (L - 1 - ridx).astype(F32))
    rsel = (lax.broadcasted_iota(I32, (CH2, H), 0) % H == lax.broadcasted_iota(I32, (CH2, H), 1)).astype(F32)
    bbr_t = _dot_exact(rsel, bbr)
    bbi_t = _dot_exact(rsel, bbi)
    w_re = pwr * bbr_t - pwi * bbi_t
    w_im = pwr * bbi_t + pwi * bbr_t
    wt_ref[0] = jnp.where(is_re, w_re, w_im).astype(BF16)
    a_r, a_i = powers(ar, ai, dt, float(L))
    sign = jnp.where(is_re, -1.0, 1.0)
    for lvl in range(SCAN_LEVELS_MAX):
        if lvl < n_levels:
            apow_ref[0, lvl:lvl + 1, :] = a_r
            apow_ref[0, SCAN_LEVELS_MAX + lvl:SCAN_LEVELS_MAX + lvl + 1, :] = sign * a_i
            a_r, a_i = a_r * a_r - a_i * a_i, 2.0 * a_r * a_i
        else:
            apow_ref[0, lvl:lvl + 1, :] = jnp.zeros_like(a_r)
            apow_ref[0, SCAN_LEVELS_MAX + lvl:SCAN_LEVELS_MAX + lvl + 1, :] = jnp.zeros_like(a_r)

    col = acol_ref[0]
    arc, aic, dtc = col[:, 0:1], col[:, 1:2], jnp.exp(col[:, 2:3])
    ct = ct_ref[0]
    csel = (lax.broadcasted_iota(I32, (H, CH2), 1) % H == lax.broadcasted_iota(I32, (H, CH2), 0)).astype(F32)
    ctr = _dot_exact(ct[:, :H], csel)
    cti = _dot_exact(ct[:, H:], csel)
    sidx = (lax.broadcasted_iota(I32, (P, CH2), 1) // H).astype(F32)
    pr, pi = powers(arc, aic, dtc, sidx)
    g_r = ctr * pr - cti * pi
    g_i = ctr * pi + cti * pr
    a1r, a1i = powers(arc, aic, dtc, 1.0)
    v_r = g_r * a1r - g_i * a1i
    v_i = g_r * a1i + g_i * a1r
    vt_ref[0, :P, :] = v_r.astype(BF16)
    vt_ref[0, P:, :] = (-v_i).astype(BF16)
    bbr64, bbi64 = bbr[:, :P], bbi[:, :P]
    kt = _dot_exact(bbr64, g_r) - _dot_exact(bbi64, g_i)
    lane2 = lax.broadcasted_iota(I32, (H, CH2), 1)
    for r in range(L):
        blk = kt if r == 0 else jnp.where(lane2 >= r * H, pltpu.roll(kt, r * H, axis=1), 0.0)
        mt_acc[r * H:(r + 1) * H, :] = blk
    mt_ref[0] = mt_acc[...].astype(BF16)
    dvec_ref[0] = _dot_exact(d_ref[0], csel)


def _s5_prep(n_levels, s5_a_re, s5_a_im, s5_log_dt, s5_b_re, s5_b_im, s5_c_re, s5_c_im, s5_d):
    G, P, H = S5_GROUPS, S5_STATE, S5_GROUP_CH
    ldt = jnp.broadcast_to(s5_log_dt[:, None], (G, P))
    arow = jnp.stack([s5_a_re, s5_a_im, ldt], axis=1)
    arow = jnp.concatenate([arow, arow], axis=2)
    acol = jnp.stack([s5_a_re, s5_a_im, ldt], axis=2)
    bt = jnp.concatenate([s5_b_re.transpose(0, 2, 1), s5_b_im.transpose(0, 2, 1)], axis=1)
    ct = jnp.concatenate([s5_c_re.transpose(0, 2, 1), s5_c_im.transpose(0, 2, 1)], axis=2)
    d = s5_d.reshape(G, 1, H)
    blk = lambda *s: pl.BlockSpec((S5_PREP_GROUPS,) + s, lambda g: (g, 0, 0))
    return pl.pallas_call(
        functools.partial(_s5_prep_kernel, n_levels),
        grid=(G // S5_PREP_GROUPS,),
        in_specs=[blk(3, P2), blk(P, 3), blk(2 * H, P), blk(P, 2 * H), blk(1, H)],
        out_specs=[blk(CH2, CH2), blk(CH2, P2), blk(P2, CH2), blk(2 * SCAN_LEVELS_MAX, P2), blk(1, CH2)],
        out_shape=[jax.ShapeDtypeStruct((G, CH2, CH2), BF16),
                   jax.ShapeDtypeStruct((G, CH2, P2), BF16),
                   jax.ShapeDtypeStruct((G, P2, CH2), BF16),
                   jax.ShapeDtypeStruct((G, 2 * SCAN_LEVELS_MAX, P2), F32),
                   jax.ShapeDtypeStruct((G, 1, CH2), F32)],
        scratch_shapes=[pltpu.VMEM((CH2, CH2), F32)],
        compiler_params=_cparams("parallel"),
        name="s5_discretise",
    )(arow, acol, bt, ct, d)


def _gelu_tanh(y):
    return 0.5 * y * (1.0 + jnp.tanh(math.sqrt(2.0 / math.pi) * (y + 0.044715 * (y * y * y))))


GROUPS_PER_TILE = LANES // S5_GROUP_CH
STEPS_PER_TILE = LANES // S5_GROUP_CH


def _s5_scan_kernel(n_levels, u_ref, mt_ref, wt_ref, vt_ref, apow_ref, dvec_ref, y_ref):
    H, L = S5_GROUP_CH, S5_CHUNK
    nc = u_ref.shape[0] // L
    lane_blk = lax.broadcasted_iota(I32, (nc, LANES), 1) // H
    ridx = lax.broadcasted_iota(I32, (nc, P2), 0)
    steps = [u_ref[pl.ds(l, nc, stride=L), :] for l in range(L)]
    groups = range(GROUPS_PER_TILE)
    us, xs = [], []
    for gi in groups:
        halves = []
        for hh in range(L // STEPS_PER_TILE):
            acc = jnp.zeros((nc, LANES), F32)
            for l8 in range(STEPS_PER_TILE):
                src = steps[hh * STEPS_PER_TILE + l8]
                shift = ((l8 - gi) % STEPS_PER_TILE) * H
                moved = src if shift == 0 else pltpu.roll(src, shift, axis=1)
                acc = jnp.where(lane_blk == l8, moved, acc)
            halves.append(acc)
        u = jnp.concatenate(halves, axis=1)
        us.append(u)
        xs.append(_dot(u.astype(BF16), wt_ref[gi]))
    for lvl in range(n_levels):
        sh = 1 << lvl
        for gi in groups:
            apow = apow_ref[gi]
            prev = jnp.where(ridx >= sh, pltpu.roll(xs[gi], sh, axis=0), 0.0)
            a_r = apow[lvl:lvl + 1]
            a_i = apow[SCAN_LEVELS_MAX + lvl:SCAN_LEVELS_MAX + lvl + 1]
            xs[gi] = xs[gi] + a_r * prev + a_i * pltpu.roll(prev, S5_STATE, axis=1)
    outs = [jnp.zeros((nc, LANES), F32) for _ in range(L)]
    for gi in groups:
        x_in = jnp.where(ridx >= 1, pltpu.roll(xs[gi], 1, axis=0), 0.0)
        u = us[gi]
        y = _gelu_tanh(_dot(u.astype(BF16), mt_ref[gi]) + _dot(x_in.astype(BF16), vt_ref[gi]) + dvec_ref[gi] * u)
        for l in range(L):
            hh, l8 = divmod(l, STEPS_PER_TILE)
            src = y[:, hh * LANES:(hh + 1) * LANES]
            shift = ((gi - l8) % STEPS_PER_TILE) * H
            moved = src if shift == 0 else pltpu.roll(src, shift, axis=1)
            outs[l] = jnp.where(lane_blk == gi, moved, outs[l])
    for l in range(L):
        y_ref[pl.ds(l, nc, stride=L), :] = outs[l]


def _s5_branch(u, B, S, s5_params):
    G, L = S5_GROUPS, S5_CHUNK
    nc = S // L
    n_levels = max(1, (nc - 1).bit_length())
    assert n_levels <= SCAN_LEVELS_MAX and S5_CHUNK % STEPS_PER_TILE == 0
    mt, wt, vt, apow, dvec = _s5_prep(n_levels, *s5_params)
    n_lt = G // GROUPS_PER_TILE
    blk = lambda *s: pl.BlockSpec((GROUPS_PER_TILE,) + s, lambda b, t: (t, 0, 0))
    io = pl.BlockSpec((S, LANES), lambda b, t: (b, t))
    return pl.pallas_call(
        functools.partial(_s5_scan_kernel, n_levels),
        grid=(B, n_lt),
        in_specs=[io, blk(CH2, CH2), blk(CH2, P2), blk(P2, CH2), blk(2 * SCAN_LEVELS_MAX, P2), blk(1, CH2)],
        out_specs=io,
        out_shape=jax.ShapeDtypeStruct((B * S, S5_WIDTH), F32),
        compiler_params=_cparams("parallel", "parallel"),
        name="s5_chunk_scan",
    )(u, mt, wt, vt, apow, dvec)


def _merge_xattn_kernel(x_ref, o_ref, ys_ref, gm_ref, gs_ref, wmo_ref, wglu_ref, wout_ref, g1_ref, b1_ref,
                        k_ref, v_ref, wq_ref, wo_ref, g2_ref, b2_ref, x2_ref):
    y_mla = _dot(o_ref[0], wmo_ref[...])
    z = _dot(ys_ref[0].astype(BF16), wglu_ref[...])
    y_s5 = z[:, :D_MODEL] * jax.nn.sigmoid(z[:, D_MODEL:])
    merged = (jax.nn.sigmoid(gm_ref[0].astype(F32)) * y_mla
              + jax.nn.sigmoid(gs_ref[0].astype(F32)) * y_s5)
    mix = _dot(merged.astype(BF16), wout_ref[...])
    x1 = _layer_norm(DEEPNORM_ALPHA * x_ref[0] + mix, g1_ref[...], b1_ref[...])
    q = (_dot(x1.astype(BF16), wq_ref[...]) * (XATTN_HEAD_DIM ** -0.5)).astype(BF16)
    k = k_ref[0]
    v = v_ref[0]
    outs = []
    for h in range(XATTN_HEADS):
        sl = slice(h * XATTN_HEAD_DIM, (h + 1) * XATTN_HEAD_DIM)
        s = _dot_nt(q[:, sl], k[:, sl])
        p = jnp.exp(s - jnp.max(s, axis=-1, keepdims=True))
        o = _dot(p.astype(BF16), v[:, sl]) / jnp.sum(p, axis=-1, keepdims=True)
        outs.append(o.astype(BF16))
    xa = _dot(jnp.concatenate(outs, axis=1), wo_ref[...])
    x2_ref[0] = _layer_norm(DEEPNORM_ALPHA * x1 + xa, g2_ref[...], b2_ref[...])


def _memkv_kernel(mem_ref, g_ref, b_ref, w_ref, k_ref, v_ref):
    m = _layer_norm(mem_ref[0], g_ref[...], b_ref[...]).astype(BF16)
    kv = _dot(m, w_ref[...])
    hd = XATTN_HEADS * XATTN_HEAD_DIM
    k_ref[0] = kv[:, :hd].astype(BF16)
    v_ref[0] = kv[:, hd:].astype(BF16)


def _merge_cross_attention(x2d, o, ys, gm, gs, mem, B, S, w_mla_o, w_s5_glu, w_out, ln1_g, ln1_b,
                           mem_ln_g, mem_ln_b, w_xq, w_xkv, w_xo, ln2_g, ln2_b):
    M = mem.shape[1]
    hd = XATTN_HEADS * XATTN_HEAD_DIM
    HP = MLA_HEADS * HEAD_PAD
    row = lambda a: a.reshape(1, -1)
    wmo = jnp.pad(w_mla_o.reshape(MLA_HEADS, V_HEAD, D_MODEL), ((0, 0), (0, HEAD_PAD - V_HEAD), (0, 0)))
    wmo = wmo.reshape(HP, D_MODEL).astype(BF16)
    k, v = pl.pallas_call(
        _memkv_kernel,
        grid=(B,),
        in_specs=[pl.BlockSpec((1, M, D_MODEL), lambda b: (b, 0, 0)), _full((1, D_MODEL)), _full((1, D_MODEL)),
                  _full((D_MODEL, 2 * hd))],
        out_specs=[pl.BlockSpec((1, M, hd), lambda b: (b, 0, 0))] * 2,
        out_shape=[jax.ShapeDtypeStruct((B, M, hd), BF16)] * 2,
        compiler_params=_cparams("parallel"),
        name="memory_kv",
    )(mem, row(mem_ln_g), row(mem_ln_b), w_xkv.astype(BF16))
    tile = lambda w: pl.BlockSpec((1, TM, w), lambda b, i: (b, i, 0))
    kv_spec = pl.BlockSpec((1, M, hd), lambda b, i: (b, 0, 0))
    r3 = lambda a: a.reshape(B, S, a.shape[-1])
    x2 = pl.pallas_call(
        _merge_xattn_kernel,
        grid=(B, S // TM),
        in_specs=[tile(D_MODEL), tile(HP), tile(S5_WIDTH), tile(D_MODEL), tile(D_MODEL),
                  _full((HP, D_MODEL)), _full((S5_WIDTH, 2 * D_MODEL)), _full((D_MODEL, D_MODEL)),
                  _full((1, D_MODEL)), _full((1, D_MODEL)),
                  kv_spec, kv_spec, _full((D_MODEL, hd)), _full((hd, D_MODEL)), _full((1, D_MODEL)), _full((1, D_MODEL))],
        out_specs=tile(D_MODEL),
        out_shape=jax.ShapeDtypeStruct((B, S, D_MODEL), F32),
        compiler_params=_cparams("parallel", "parallel"),
        name="merge_ln1_xattn_ln2",
    )(r3(x2d), r3(o), r3(ys), r3(gm), r3(gs), wmo, w_s5_glu.astype(BF16), w_out.astype(BF16), row(ln1_g), row(ln1_b),
      k, v, w_xq.astype(BF16), w_xo.astype(BF16), row(ln2_g), row(ln2_b))
    return x2.reshape(B * S, D_MODEL)


def _route_sort_kernel(x_ref, w_ref, bias_ref, xs_ref, pos_ref, cnt_ref, off_ref):
    E, per = N_EXPERTS, N_EXPERTS // N_EXPERT_GROUPS
    tm = x_ref.shape[0]
    x = x_ref[...]
    xb = x.astype(BF16)
    x_lo = (x - xb.astype(F32)).astype(BF16)
    w = w_ref[...]
    wb = w.astype(BF16)
    w_lo = (w - wb.astype(F32)).astype(BF16)
    logits = _dot_nt(wb, xb) + (_dot_nt(wb, x_lo) + _dot_nt(w_lo, xb))
    scores = jax.nn.sigmoid(logits)
    sel = scores + bias_ref[...]
    neg = -jnp.inf
    i8 = lax.broadcasted_iota(I32, (per, tm), 0)
    gscore = []
    for g in range(N_EXPERT_GROUPS):
        blk = sel[g * per:(g + 1) * per]
        m1 = jnp.max(blk, axis=0, keepdims=True)
        i1 = jnp.min(jnp.where(blk == m1, i8, per), axis=0, keepdims=True)
        m2 = jnp.max(jnp.where(i8 == i1, neg, blk), axis=0, keepdims=True)
        gscore.append(m1 + m2)
    blocks = []
    for g in range(N_EXPERT_GROUPS):
        ahead = jnp.zeros((1, tm), I32)
        for o in range(N_EXPERT_GROUPS):
            if o == g:
                continue
            before = (gscore[o] >= gscore[g]) if o < g else (gscore[o] > gscore[g])
            ahead = ahead + before.astype(I32)
        blocks.append(jnp.where(ahead < TOPK_GROUPS, sel[g * per:(g + 1) * per], neg))
    cur = jnp.concatenate(blocks, axis=0)
    ie = lax.broadcasted_iota(I32, (E, tm), 0)
    picked = jnp.zeros((E, tm), F32)
    for _ in range(TOP_K):
        m = jnp.max(cur, axis=0, keepdims=True)
        idx = jnp.min(jnp.where(cur == m, ie, E), axis=0, keepdims=True)
        hit = ie == idx
        picked = jnp.where(hit, 1.0, picked)
        cur = jnp.where(hit, neg, cur)
    wsel = scores * picked
    wnorm = wsel / jnp.sum(wsel, axis=0, keepdims=True) * ROUTED_SCALE
    pb = picked.astype(BF16)
    tri = (lax.broadcasted_iota(I32, (tm, tm), 0) <= lax.broadcasted_iota(I32, (tm, tm), 1)).astype(BF16)
    incl = _dot(pb, tri)
    cnt_col = jnp.sum(picked, axis=1, keepdims=True)
    lower = (lax.broadcasted_iota(I32, (E, E), 1) < lax.broadcasted_iota(I32, (E, E), 0)).astype(F32)
    al_col = jnp.floor((cnt_col + (SEG_ALIGN - 1.0)) * (1.0 / SEG_ALIGN)) * SEG_ALIGN
    off_col = _dot_exact(lower, jnp.broadcast_to(al_col, (E, LANES)))[:, 0:1]
    pos = off_col + incl - 1.0
    q = jnp.floor(pos * (1.0 / POS_RADIX))
    pq = jnp.concatenate([jnp.where(picked > 0.0, POS_RADIX * q, POS_RADIX * POS_NONE_Q),
                          jnp.where(picked > 0.0, pos - POS_RADIX * q, 0.0)], axis=0)
    pos_ref[...] = pq
    pqb = pq.astype(BF16)
    cnt_ref[0] = jnp.broadcast_to(cnt_col, (E, LANES)).astype(I32)
    off_ref[0] = jnp.broadcast_to(off_col, (E, LANES)).astype(I32)
    cnt_row = _dot_nt(jnp.ones((8, tm), BF16), pb)
    al_row = jnp.floor((cnt_row + (SEG_ALIGN - 1.0)) * (1.0 / SEG_ALIGN)) * SEG_ALIGN
    er = lax.broadcasted_iota(I32, (E, 4 * E), 0)
    ec = lax.broadcasted_iota(I32, (E, 4 * E), 1) % E
    off_row4 = _dot_exact(al_row, (er < ec).astype(F32))[0:1]
    end_row4 = off_row4 + _dot_exact(al_row, (er == ec).astype(F32))[0:1]
    w_hi = wnorm.astype(BF16).astype(F32)
    w_r1 = wnorm - w_hi
    w_mid = w_r1.astype(BF16).astype(F32)
    w_lo = w_r1 - w_mid
    w4_t = jnp.concatenate([w_hi, w_mid, w_lo, jnp.zeros_like(w_lo)], axis=0).T
    xw = jnp.concatenate([xb, w4_t.astype(BF16)], axis=1)
    half = D_MODEL // 2

    def build(c, carry):
        r0 = pl.multiple_of(c * OH_ROWS, OH_ROWS)
        jr = (lax.broadcasted_iota(I32, (OH_ROWS, 4 * E), 0) + r0).astype(F32)
        member4 = jnp.where(jr >= off_row4, jnp.where(jr < end_row4, 1.0, 0.0), 0.0)
        target = _dot(member4[:, :2 * E].astype(BF16), pqb)
        j = (lax.broadcasted_iota(I32, (OH_ROWS, tm), 0) + r0).astype(F32)
        ohb = jnp.where(target == j, 1.0, 0.0).astype(BF16)
        rows = _dot(ohb, xw)
        xs_ref[0, pl.ds(r0, OH_ROWS), 0:half] = _pack_bf16_pair(rows[:, :half], rows[:, half:D_MODEL])
        wrow = jnp.sum(member4 * rows[:, D_MODEL:], axis=1, keepdims=True)
        xs_ref[0, pl.ds(r0, OH_ROWS), half:] = pltpu.bitcast(jnp.broadcast_to(wrow, (OH_ROWS, LANES)), I32)
        return carry

    lax.fori_loop(0, SORT_ROWS // OH_ROWS, build, 0)
    xs_ref[0, SORT_ROWS:, :] = jnp.zeros((PAD_ROWS, XW), I32)


def _route_sort(x2, w_router, router_bias):
    T = x2.shape[0]
    E = N_EXPERTS
    nt = T // TW
    seg = pl.BlockSpec((1, E, LANES), lambda i: (i, 0, 0))
    return pl.pallas_call(
        _route_sort_kernel,
        grid=(nt,),
        in_specs=[pl.BlockSpec((TW, D_MODEL), lambda i: (i, 0)), _full((E, D_MODEL)), _full((E, 1))],
        out_specs=[pl.BlockSpec((1, TS, XW), lambda i: (i, 0, 0)),
                   pl.BlockSpec((2 * E, TW), lambda i: (0, i)), seg, seg],
        out_shape=[jax.ShapeDtypeStruct((nt, TS, XW), I32), jax.ShapeDtypeStruct((2 * E, T), F32),
                   jax.ShapeDtypeStruct((nt, E, LANES), I32), jax.ShapeDtypeStruct((nt, E, LANES), I32)],
        compiler_params=_cparams("parallel"),
        name="route_local_sort",
    )(x2, w_router.T, router_bias.reshape(E, 1))


def _plan_kernel(cnt_ref, off_ref, pa_ref, ce_ref, gn_ref, nch_ref):
    E, nt = cnt_ref.shape
    n_rows = pa_ref.shape[0]
    cnt = cnt_ref[...].astype(F32)
    off = off_ref[...].astype(F32)
    npc = jnp.floor((cnt + (PR - 1.0)) * (1.0 / PR))
    tri = (lax.broadcasted_iota(I32, (nt, nt), 0) <= lax.broadcasted_iota(I32, (nt, nt), 1)).astype(F32)
    p_incl = _dot_exact(npc, tri)
    total = p_incl[:, nt - 1:nt]
    n_ch = jnp.floor((total + (PPC - 1.0)) * (1.0 / PPC))
    lower = (lax.broadcasted_iota(I32, (E, E), 1) < lax.broadcasted_iota(I32, (E, E), 0)).astype(F32)
    wide = lambda col: jnp.broadcast_to(col, (E, LANES))
    g0 = _dot_exact(lower, wide(n_ch))[:, 0:1]
    n_total = jnp.sum(n_ch, axis=0, keepdims=True)
    g0_row = wide(g0).T[0:1, :E]
    g1_row = g0_row + wide(n_ch).T[0:1, :E]
    g_e = lax.broadcasted_iota(I32, (n_rows, E), 0).astype(F32)
    sel = jnp.where(g_e >= g0_row, jnp.where(g_e < g1_row, 1.0, 0.0), 0.0)
    tile_base = lax.broadcasted_iota(I32, (E, nt), 1).astype(F32) * TS
    a_mat = tile_base + off - PR * (p_incl - npc)
    p_g = _dot_exact(sel, p_incl)
    a_g = _dot_exact(sel, a_mat)
    e_idx = lax.broadcasted_iota(I32, (E, LANES), 0).astype(F32)
    pick = lambda col: _dot_exact(sel, col)[:, 0:1]
    g0_g, tot_g, e_g = pick(wide(g0)), pick(wide(total)), pick(e_idx)
    gn_g = pick(wide(g0 + n_ch))
    g_p = lax.broadcasted_iota(I32, (n_rows, PPC), 0).astype(F32)
    q = (g_p - g0_g) * PPC + lax.broadcasted_iota(I32, (n_rows, PPC), 1).astype(F32)
    addr = jnp.full((n_rows, PPC), -float(SEG_ALIGN), F32)
    lo = jnp.zeros((n_rows, 1), F32)
    for i in range(nt):
        hi = p_g[:, i:i + 1]
        addr = jnp.where(q >= lo, jnp.where(q < hi, a_g[:, i:i + 1] + PR * q, addr), addr)
        lo = hi
    pa_ref[...] = addr.astype(I32)
    g_col = lax.broadcasted_iota(I32, (n_rows, LANES), 0).astype(F32)
    ce_ref[...] = jnp.where(g_col < n_total, jnp.broadcast_to(e_g, (n_rows, LANES)), E - 1.0).astype(I32)
    gn_ref[...] = jnp.broadcast_to(gn_g, (n_rows, LANES)).astype(I32)
    nch_ref[...] = jnp.broadcast_to(n_total, nch_ref.shape).astype(I32)


def _plan(cnt, off, n_chunks_max):
    n_rows = (n_chunks_max + 7) // 8 * 8
    pa, ce, gn, nch = pl.pallas_call(
        _plan_kernel,
        out_shape=[jax.ShapeDtypeStruct((n_rows, PPC), I32), jax.ShapeDtypeStruct((n_rows, LANES), I32),
                   jax.ShapeDtypeStruct((n_rows, LANES), I32), jax.ShapeDtypeStruct((8, LANES), I32)],
        name="moe_plan",
    )(cnt, off)
    return pa.reshape(n_rows * PPC), ce[:, 0], gn[:, 0], nch[0, :1]


def _unpack_bf16_pair(w):
    lo = pltpu.bitcast(lax.shift_left(w, jnp.int32(16)), F32).astype(BF16)
    hi = pltpu.bitcast(w & jnp.int32(-65536), F32).astype(BF16)
    return lo, hi


def _pack_bf16_pair(lo, hi):
    lo_bits = lax.shift_right_logical(pltpu.bitcast(lo.astype(BF16).astype(F32), I32), jnp.int32(16))
    hi_bits = pltpu.bitcast(hi.astype(BF16).astype(F32), I32)
    return hi_bits | lo_bits


def _expert_kernel(ce_ref, gn_ref, pa_ref, nch_ref, tot_ref, xs_hbm, wgu_hbm, wdn_hbm, y_hbm,
                   xbuf, ybuf, zbuf, wgu_f, wdn_f, wgu_b, wdn_b, wslot, gsem, wsem, zsem, esem):
    g = pl.program_id(0)
    n = nch_ref[0]
    slot = g % N_SLOTS
    half = D_MODEL // 2

    dummy_base = (tot_ref.shape[0] - 1) * TS

    def gather_copy(s, a, jp):
        return pltpu.make_async_copy(xs_hbm.at[pl.ds(pl.multiple_of(a, SEG_ALIGN), PR), :],
                                     xbuf.at[s, pl.ds(jp * PR, PR), :], gsem.at[s])

    def write_copy(s, a, jp):
        return pltpu.make_async_copy(ybuf.at[s, pl.ds(jp * PR, PR), :],
                                     y_hbm.at[pl.ds(pl.multiple_of(a, SEG_ALIGN), PR), :], wsem.at[s])

    def gather_start(c, s):
        for jp in range(PPC):
            gather_copy(s, jnp.maximum(pa_ref[c * PPC + jp], 0), jp).start()

    def write_start(c, s):
        for jp in range(PPC):
            a = pa_ref[c * PPC + jp]
            write_copy(s, jnp.where(a >= 0, a, dummy_base + jp * PR), jp).start()

    def gather_wait(s):
        for jp in range(PPC):
            gather_copy(s, 0, jp).wait()

    def write_wait(s):
        for jp in range(PPC):
            write_copy(s, 0, jp).wait()

    def weight_copies(e, s):
        return (pltpu.make_async_copy(wgu_hbm.at[e], wgu_f.at[s], esem.at[s]),
                pltpu.make_async_copy(wdn_hbm.at[e], wdn_f.at[s], esem.at[s]))

    @pl.when(g == 0)
    def _():
        wslot[0] = 0
        for cp in weight_copies(ce_ref[0], 0):
            cp.start()
        gather_start(0, 0)
        ybuf[...] = jnp.zeros_like(ybuf)
        zbuf[...] = jnp.zeros_like(zbuf)

        def zero_copy(row):
            return pltpu.make_async_copy(zbuf, y_hbm.at[pl.ds(pl.multiple_of(row, SEG_ALIGN), SEG_ALIGN), :], zsem)

        def tail_of(i):
            return i * TS + tot_ref[i], (TS - tot_ref[i]) // SEG_ALIGN

        def zero_start(i, c):
            base, cnt8 = tail_of(i)
            lax.fori_loop(0, cnt8, lambda q, z: (zero_copy(base + q * SEG_ALIGN).start(), z)[1], 0)
            return c

        def zero_wait(i, c):
            base, cnt8 = tail_of(i)
            lax.fori_loop(0, cnt8, lambda q, z: (zero_copy(base + q * SEG_ALIGN).wait(), z)[1], 0)
            return c

        lax.fori_loop(0, tot_ref.shape[0], zero_start, 0)
        lax.fori_loop(0, tot_ref.shape[0], zero_wait, 0)

    for c in range(1, GATHER_AHEAD):
        @pl.when((g == 0) & (c < n))
        def _(c=c):
            gather_start(c, c)

    @pl.when(g + GATHER_AHEAD < n)
    def _():
        gather_start(g + GATHER_AHEAD, (g + GATHER_AHEAD) % N_SLOTS)

    e_cur = ce_ref[g]
    e_prev = ce_ref[jnp.maximum(g - 1, 0)]
    e_prev2 = ce_ref[jnp.maximum(g - 2, 0)]

    @pl.when(g < n)
    def _():
        @pl.when((g == 0) | (e_cur != e_prev))
        def _():
            s = wslot[0]
            for cp in weight_copies(e_cur, s):
                cp.wait()
            wgu_b[...] = wgu_f[s].astype(BF16)
            wdn_b[...] = wdn_f[s].astype(BF16)
            g_next = gn_ref[g]

            @pl.when(g_next < n)
            def _():
                for cp in weight_copies(ce_ref[g_next], 1 - s):
                    cp.start()

            wslot[0] = 1 - s

        gather_wait(slot)

        @pl.when((g >= 2) & (e_prev == e_prev2))
        def _():
            write_wait((g - 2) % N_SLOTS)

        @pl.when((g >= 1) & (e_cur != e_prev))
        def _():
            write_wait((g - 1) % N_SLOTS)

        def swiglu_rows(rows):
            xw = xbuf[slot, :rows, :]
            lo, hi = _unpack_bf16_pair(xw[:, :half])
            h = _dot(jnp.concatenate([lo, hi], axis=1), wgu_b[...])
            gate, up = h[:, :EXPERT_FF], h[:, EXPERT_FF:]
            act = (gate * jax.nn.sigmoid(gate) * up).astype(BF16)
            y = _dot(act, wdn_b[...])
            w_row = pltpu.bitcast(xw[:, half:], F32)
            y = y * jnp.concatenate([w_row] * (D_MODEL // LANES), axis=1)
            ybuf[slot, :rows, :] = _pack_bf16_pair(y[:, :half], y[:, half:])

        second_half_used = pa_ref[g * PPC + PPC // 2] >= 0

        @pl.when(second_half_used)
        def _():
            swiglu_rows(CR)

        @pl.when(jnp.logical_not(second_half_used))
        def _():
            swiglu_rows(CR // 2)

        write_start(g, slot)

    @pl.when(g == pl.num_programs(0) - 1)
    def _():
        @pl.when((n >= 2) & (ce_ref[jnp.maximum(n - 1, 0)] == ce_ref[jnp.maximum(n - 2, 0)]))
        def _():
            write_wait((n - 2) % N_SLOTS)

        write_wait((n - 1) % N_SLOTS)


def _experts(ce, gn, pa, nch, tot, xs, w_exp_gu, w_exp_down):
    n_rows = xs.shape[0]
    half = D_MODEL // 2
    hbm = pl.BlockSpec(memory_space=pl.ANY)
    grid_spec = pltpu.PrefetchScalarGridSpec(
        num_scalar_prefetch=5,
        grid=(ce.shape[0],),
        in_specs=[hbm, hbm, hbm],
        out_specs=hbm,
        scratch_shapes=[pltpu.VMEM((N_SLOTS, CR, XW), I32), pltpu.VMEM((N_SLOTS, CR, half), I32),
                        pltpu.VMEM((SEG_ALIGN, half), I32),
                        pltpu.VMEM((2, D_MODEL, 2 * EXPERT_FF), F32), pltpu.VMEM((2, EXPERT_FF, D_MODEL), F32),
                        pltpu.VMEM((D_MODEL, 2 * EXPERT_FF), BF16), pltpu.VMEM((EXPERT_FF, D_MODEL), BF16),
                        pltpu.SMEM((1,), I32),
                        pltpu.SemaphoreType.DMA((N_SLOTS,)), pltpu.SemaphoreType.DMA((N_SLOTS,)),
                        pltpu.SemaphoreType.DMA(()), pltpu.SemaphoreType.DMA((2,))],
    )
    return pl.pallas_call(
        _expert_kernel,
        grid_spec=grid_spec,
        out_shape=jax.ShapeDtypeStruct((n_rows + TS, half), I32),
        compiler_params=_cparams("arbitrary"),
        name="moe_grouped_swiglu",
    )(ce, gn, pa, nch, tot, xs, w_exp_gu, w_exp_down)


def _combine_kernel(tot_ref, y_ref, pos_ref, cnt_ref, off_ref, x2_ref, wsg_ref, wsd_ref, g_ref, b_ref, out_ref,
                    acc_lo, acc_hi):
    tm = x2_ref.shape[0]
    E = N_EXPERTS
    n_sorted = tot_ref[pl.program_id(0)]
    x2 = x2_ref[...]
    h = _dot(x2.astype(BF16), wsg_ref[...])
    gate, up = h[:, :SHARED_FF], h[:, SHARED_FF:]
    shared = _dot((gate * jax.nn.sigmoid(gate) * up).astype(BF16), wsd_ref[...])
    pq_t = pos_ref[...].T.astype(BF16)
    off_col = off_ref[0][:, 0:1].astype(F32)
    cnt_col = cnt_ref[0][:, 0:1].astype(F32)
    end_col = off_col + jnp.floor((cnt_col + (SEG_ALIGN - 1.0)) * (1.0 / SEG_ALIGN)) * SEG_ALIGN
    off_col2 = jnp.concatenate([off_col, off_col], axis=0)
    end_col2 = jnp.concatenate([end_col, end_col], axis=0)
    acc_lo[...] = jnp.zeros_like(acc_lo)
    acc_hi[...] = jnp.zeros_like(acc_hi)

    def gather_back(c, carry):
        r0 = pl.multiple_of(c * OH_ROWS, OH_ROWS)
        jc = (lax.broadcasted_iota(I32, (2 * E, OH_ROWS), 1) + r0).astype(F32)
        member2 = jnp.where(jc >= off_col2, jnp.where(jc < end_col2, 1.0, 0.0), 0.0)
        target = _dot(pq_t, member2.astype(BF16))
        j = (lax.broadcasted_iota(I32, (tm, OH_ROWS), 1) + r0).astype(F32)
        ohb = jnp.where(target == j, 1.0, 0.0).astype(BF16)
        yw = y_ref[0, pl.ds(r0, OH_ROWS), :]
        row = lax.broadcasted_iota(I32, yw.shape, 0) + r0
        lo, hi = _unpack_bf16_pair(jnp.where(row < n_sorted, yw, 0))
        acc_lo[...] += _dot(ohb, lo)
        acc_hi[...] += _dot(ohb, hi)
        return carry

    lax.fori_loop(0, SORT_ROWS // OH_ROWS, gather_back, 0)
    ff = shared + jnp.concatenate([acc_lo[...], acc_hi[...]], axis=1)
    out_ref[...] = _layer_norm(DEEPNORM_ALPHA * x2 + ff, g_ref[...], b_ref[...])


def _combine(tot, y, pos, cnt, off, x2, w_sh_gu, w_sh_down, ln_g, ln_b):
    T = x2.shape[0]
    nt = T // TW
    half = D_MODEL // 2
    tile = lambda w: pl.BlockSpec((TW, w), lambda i: (i, 0))
    return pl.pallas_call(
        _combine_kernel,
        grid=(nt,),
        in_specs=[pl.BlockSpec(memory_space=pltpu.SMEM),
                  pl.BlockSpec((1, TS, half), lambda i: (i, 0, 0)),
                  pl.BlockSpec((2 * N_EXPERTS, TW), lambda i: (0, i)),
                  pl.BlockSpec((1, N_EXPERTS, LANES), lambda i: (i, 0, 0)),
                  pl.BlockSpec((1, N_EXPERTS, LANES), lambda i: (i, 0, 0)), tile(D_MODEL),
                  _full((D_MODEL, 2 * SHARED_FF)), _full((SHARED_FF, D_MODEL)),
                  _full((1, D_MODEL)), _full((1, D_MODEL))],
        out_specs=tile(D_MODEL),
        out_shape=jax.ShapeDtypeStruct((T, D_MODEL), F32),
        scratch_shapes=[pltpu.VMEM((TW, half), F32), pltpu.VMEM((TW, half), F32)],
        compiler_params=_cparams("parallel"),
        name="moe_combine_shared_ln3",
    )(tot, y.reshape(nt + 1, TS, half), pos, cnt, off, x2, w_sh_gu.astype(BF16), w_sh_down.astype(BF16),
      ln_g.reshape(1, -1), ln_b.reshape(1, -1))


def _moe(x2, w_router, router_bias, w_exp_gu, w_exp_down, w_sh_gu, w_sh_down, ln_g, ln_b):
    T = x2.shape[0]
    E = N_EXPERTS
    nt = T // TW
    xs, pos, cnt, off = _route_sort(x2, w_router, router_bias)
    n_pieces_max = (T * TOP_K + nt * E * (PR - 1)) // PR
    n_chunks_max = n_pieces_max // PPC + E
    pa, ce, gn, nch = _plan(cnt[:, :, 0].T, off[:, :, 0].T, n_chunks_max)
    last_al = (cnt[:, E - 1, 0] + SEG_ALIGN - 1) // SEG_ALIGN * SEG_ALIGN
    tot = jnp.concatenate([off[:, E - 1, 0] + last_al, jnp.zeros((1,), I32)])
    y = _experts(ce, gn, pa, nch, tot, xs.reshape(nt * TS, XW), w_exp_gu, w_exp_down)
    return _combine(tot, y, pos, cnt, off, x2, w_sh_gu, w_sh_down, ln_g, ln_b)


def kernel(x, mem, positions, w_in, q_norm_g, kv_norm_g, w_uq, w_ukv, w_mla_o, s5_a_re, s5_a_im, s5_log_dt, s5_b_re, s5_b_im, s5_c_re, s5_c_im, s5_d, w_s5_glu, w_out, ln1_g, ln1_b, mem_ln_g, mem_ln_b, w_xq, w_xkv, w_xo, ln2_g, ln2_b, w_router, router_bias, w_exp_gu, w_exp_down, w_sh_gu, w_sh_down, ln3_g, ln3_b):
    B, S, D = x.shape
    assert D == D_MODEL and S % TQ == 0 and (B * S) % TM == 0 and S % S5_CHUNK == 0
    xc = x.reshape(B * S, D)
    for l in range(w_in.shape[0]):
        tabs = _rope_tables(positions)
        q, k, v, u, gm, gs = _inproj(xc, tabs, w_in[l], q_norm_g[l], kv_norm_g[l], w_uq[l], w_ukv[l])
        o = _attention(q, k, v, B, S)
        ys = _s5_branch(u, B, S, (s5_a_re[l], s5_a_im[l], s5_log_dt[l], s5_b_re[l], s5_b_im[l],
                                  s5_c_re[l], s5_c_im[l], s5_d[l]))
        x2 = _merge_cross_attention(xc, o, ys, gm, gs, mem, B, S, w_mla_o[l], w_s5_glu[l], w_out[l],
                                    ln1_g[l], ln1_b[l], mem_ln_g[l], mem_ln_b[l], w_xq[l], w_xkv[l], w_xo[l],
                                    ln2_g[l], ln2_b[l])
        xc = _moe(x2, w_router[l], router_bias[l], w_exp_gu[l], w_exp_down[l], w_sh_gu[l],
                  w_sh_down[l], ln3_g[l], ln3_b[l])
    return xc.reshape(B, S, D)
```

```python
import functools
import math

import jax
import jax.numpy as jnp
import numpy as np
from jax import lax
from jax.experimental import pallas as pl
from jax.experimental.pallas import tpu as pltpu

F32 = jnp.float32
BF16 = jnp.bfloat16
I32 = jnp.int32

D_MODEL = 1024
MLA_HEADS = 8
QK_NOPE = 64
QK_ROPE = 32
V_HEAD = 64
Q_LORA = 256
KV_LORA = 256
ROPE_THETA = 10000.0
S5_GROUP_CH = 16
S5_WIDTH = 512
S5_GROUPS = 32
S5_STATE = 64
XATTN_HEADS = 4
XATTN_HEAD_DIM = 128
N_EXPERTS = 64
TOP_K = 8
N_EXPERT_GROUPS = 8
TOPK_GROUPS = 4
EXPERT_FF = 256
SHARED_FF = 256
ROUTED_SCALE = 2.5
LN_EPS = 1e-5
RMS_EPS = 1e-6
DEPTH = 1
DEEPNORM_ALPHA = (2.0 * DEPTH) ** 0.25

LANES = 128
HEAD_PAD = 128
ROPE_LO = QK_NOPE
ROPE_HALF = QK_ROPE // 2

TM = 512
TQ = 1024
KV_GROUPS = (2, 1)
S5_CHUNK = 16
TW = 512
OH_ROWS = 1536
PR = 16
PAD_ROWS = PR
SEG_ALIGN = 8
SORT_ROWS = TOP_K * TW + N_EXPERTS * SEG_ALIGN
TS = SORT_ROWS + PAD_ROWS
XW = D_MODEL // 2 + LANES
CR = 512
PPC = CR // PR
GATHER_AHEAD = 3
N_SLOTS = GATHER_AHEAD + 1
POS_RADIX = 64
POS_NONE_Q = 127
assert SORT_ROWS <= POS_RADIX * POS_NONE_Q and SORT_ROWS % OH_ROWS == 0
VMEM_LIMIT = 48 * 1024 * 1024


def _cparams(*sem):
    return pltpu.CompilerParams(dimension_semantics=sem, vmem_limit_bytes=VMEM_LIMIT)


def _dot(a, b):
    return jnp.dot(a, b, preferred_element_type=F32)


def _dot_nt(a, b, precision=None):
    return lax.dot_general(a, b, (((1,), (1,)), ((), ())), preferred_element_type=F32,
                           precision=precision)


def _dot_exact(a, b):
    return jnp.dot(a, b, preferred_element_type=F32, precision=lax.Precision.HIGHEST)


def _layer_norm(h, g, b):
    mu = jnp.mean(h, axis=-1, keepdims=True)
    c = h - mu
    var = jnp.mean(c * c, axis=-1, keepdims=True)
    return c * lax.rsqrt(var + LN_EPS) * g + b


def _rms_norm(h, g):
    return h * lax.rsqrt(jnp.mean(h * h, axis=-1, keepdims=True) + RMS_EPS) * g


def _full(shape):
    n = len(shape)
    return pl.BlockSpec(shape, lambda *_: (0,) * n)


def _rope_angle_kernel(pos_ref, invf_ref, cos_ref, sin_ref):
    ang = pos_ref[...].astype(F32) * invf_ref[...]
    cos_ref[...] = jnp.cos(ang)
    sin_ref[...] = jnp.sin(ang)


def _rope_tables(positions):
    T = positions.size
    rows = T * ROPE_HALF // LANES
    pos_rep = jnp.repeat(positions.reshape(T), ROPE_HALF).reshape(rows, LANES)
    inv_freq = ROPE_THETA ** (-jnp.arange(0, QK_ROPE, 2, dtype=F32) / QK_ROPE)
    invf = jnp.tile(inv_freq, LANES // ROPE_HALF).reshape(1, LANES)
    cos, sin = pl.pallas_call(
        _rope_angle_kernel,
        out_shape=(jax.ShapeDtypeStruct((rows, LANES), F32),) * 2,
        name="rope_angles",
    )(pos_rep, invf)
    cos_sin = jnp.concatenate([cos.reshape(T, ROPE_HALF), sin.reshape(T, ROPE_HALF)], axis=1)
    f = np.arange(ROPE_HALF)
    spread = np.zeros((QK_ROPE, 3 * HEAD_PAD), np.float32)
    spread[f, ROPE_LO + f] = 1.0
    spread[f, ROPE_LO + ROPE_HALF + f] = 1.0
    spread[ROPE_HALF + f, HEAD_PAD + ROPE_LO + f] = -1.0
    spread[ROPE_HALF + f, 2 * HEAD_PAD + ROPE_LO + ROPE_HALF + f] = 1.0
    outside = np.ones((1, HEAD_PAD), np.float32)
    outside[0, ROPE_LO:ROPE_LO + QK_ROPE] = 0.0
    return cos_sin, jnp.asarray(spread), jnp.asarray(outside)


def _rope(x, c_tab, s_up, s_dn):
    return (x * c_tab + pltpu.roll(x, HEAD_PAD - ROPE_HALF, axis=1) * s_up
            + pltpu.roll(x, ROPE_HALF, axis=1) * s_dn)


def _inproj_kernel(x_ref, cs_ref, spread_ref, outside_ref, wlat_ref, wu_ref, wgm_ref, wgs_ref, qg_ref, kvg_ref,
                   wuq_ref, wuk_ref, wuv_ref, q_ref, k_ref, v_ref, u_ref, gm_ref, gs_ref):
    xb = x_ref[...].astype(BF16)
    lat = _dot(xb, wlat_ref[...])
    qn = _rms_norm(lat[:, :Q_LORA], qg_ref[...]).astype(BF16)
    kvn = _rms_norm(lat[:, Q_LORA:Q_LORA + KV_LORA], kvg_ref[...]).astype(BF16)
    tabs = _dot_exact(cs_ref[...], spread_ref[...])
    c_tab = tabs[:, :HEAD_PAD] + outside_ref[...]
    s_up, s_dn = tabs[:, HEAD_PAD:2 * HEAD_PAD], tabs[:, 2 * HEAD_PAD:]
    k_rope = _rope(lat[:, Q_LORA + KV_LORA:], c_tab, s_up, s_dn)
    q = _dot(qn, wuq_ref[...])
    k = _dot(kvn, wuk_ref[...])
    scale = (QK_NOPE + QK_ROPE) ** -0.5 * math.log2(math.e)
    for h in range(MLA_HEADS):
        sl = slice(h * HEAD_PAD, (h + 1) * HEAD_PAD)
        q_ref[:, sl] = (_rope(q[:, sl], c_tab, s_up, s_dn) * scale).astype(BF16)
        k_ref[:, sl] = (k[:, sl] + k_rope).astype(BF16)
    v = _dot(kvn, wuv_ref[...])
    ones_lane = lax.broadcasted_iota(I32, (1, v.shape[1]), 1) % HEAD_PAD == V_HEAD
    v_ref[...] = jnp.where(ones_lane, 1.0, v).astype(BF16)
    u_ref[...] = _dot(xb, wu_ref[...])
    gm_ref[...] = _dot(xb, wgm_ref[...]).astype(BF16)
    gs_ref[...] = _dot(xb, wgs_ref[...]).astype(BF16)


def _pad_heads(w, head_w, lo_w):
    K = w.shape[0]
    w = w.reshape(K, MLA_HEADS, head_w)[:, :, :lo_w]
    w = jnp.pad(w, ((0, 0), (0, 0), (0, HEAD_PAD - lo_w)))
    return w.reshape(K, MLA_HEADS * HEAD_PAD)


def _inproj(x2d, tabs, w_in, q_norm_g, kv_norm_g, w_uq, w_ukv):
    T = x2d.shape[0]
    o_rope = Q_LORA + KV_LORA
    o_u = o_rope + QK_ROPE
    o_gm = o_u + S5_WIDTH
    o_gs = o_gm + D_MODEL
    w_rope = jnp.pad(w_in[:, o_rope:o_u], ((0, 0), (ROPE_LO, HEAD_PAD - ROPE_LO - QK_ROPE)))
    w_lat = jnp.concatenate([w_in[:, :o_rope], w_rope], axis=1).astype(BF16)
    w_u = w_in[:, o_u:o_gm].astype(BF16)
    w_gm = w_in[:, o_gm:o_gs].astype(BF16)
    w_gs = w_in[:, o_gs:].astype(BF16)
    wuq = _pad_heads(w_uq, QK_NOPE + QK_ROPE, QK_NOPE + QK_ROPE).astype(BF16)
    kv3 = w_ukv.reshape(KV_LORA, MLA_HEADS, QK_NOPE + V_HEAD)
    wuk = _pad_heads(kv3[:, :, :QK_NOPE].reshape(KV_LORA, -1), QK_NOPE, QK_NOPE).astype(BF16)
    wuv = _pad_heads(kv3[:, :, QK_NOPE:].reshape(KV_LORA, -1), V_HEAD, V_HEAD).astype(BF16)
    HP = MLA_HEADS * HEAD_PAD
    tile = lambda w: pl.BlockSpec((TM, w), lambda i: (i, 0))
    return pl.pallas_call(
        _inproj_kernel,
        grid=(T // TM,),
        in_specs=[tile(D_MODEL), tile(QK_ROPE), _full((QK_ROPE, 3 * HEAD_PAD)), _full((1, HEAD_PAD)),
                  _full(w_lat.shape), _full(w_u.shape), _full(w_gm.shape), _full(w_gs.shape),
                  _full((1, Q_LORA)), _full((1, KV_LORA)),
                  _full(wuq.shape), _full(wuk.shape), _full(wuv.shape)],
        out_specs=[tile(HP), tile(HP), tile(HP), tile(S5_WIDTH), tile(D_MODEL), tile(D_MODEL)],
        out_shape=[jax.ShapeDtypeStruct((T, HP), BF16)] * 3
        + [jax.ShapeDtypeStruct((T, S5_WIDTH), F32)]
        + [jax.ShapeDtypeStruct((T, D_MODEL), BF16)] * 2,
        compiler_params=_cparams("parallel"),
        name="inproj_mla_prep",
    )(x2d, *tabs, w_lat, w_u, w_gm, w_gs, q_norm_g.reshape(1, -1), kv_norm_g.reshape(1, -1),
      wuq, wuk, wuv)


def _attn_kernel(q_ref, k_ref, v_ref, o_ref):
    qi = pl.program_id(2)
    q = q_ref[0]

    def update(qq, k, v, carry, row_offset=None):
        m, acc = carry
        s = _dot_nt(qq, k)
        if row_offset is not None:
            row = lax.broadcasted_iota(I32, s.shape, 0) + row_offset
            col = lax.broadcasted_iota(I32, s.shape, 1)
            s = jnp.where(col <= row, s, -jnp.inf)
        m_new = jnp.maximum(m, jnp.max(s, axis=-1, keepdims=True))
        p = jnp.exp2(s - m_new)
        acc = jnp.exp2(m - m_new) * acc + _dot(p.astype(BF16), v)
        return m_new, acc

    def step(blk, n_blk, carry):
        tk = n_blk * TQ
        start = pl.multiple_of(blk * TQ, TQ)
        return update(q, k_ref[0, pl.ds(start, tk), :], v_ref[0, pl.ds(start, tk), :], carry)

    carry = (jnp.full((TQ, 1), -jnp.inf, F32), jnp.zeros((TQ, HEAD_PAD), F32))
    done = 0
    for n_blk in KV_GROUPS:
        base = done
        n_it = (qi - base) // n_blk
        carry = lax.fori_loop(0, n_it, lambda j, c, base=base, n_blk=n_blk: step(base + j * n_blk, n_blk, c), carry)
        done = base + n_it * n_blk
    hq = TQ // 2
    d0 = pl.multiple_of(qi * TQ, TQ)
    m, acc = carry
    _, acc_lo = update(q[:hq], k_ref[0, pl.ds(d0, hq), :], v_ref[0, pl.ds(d0, hq), :], (m[:hq], acc[:hq]), 0)
    _, acc_hi = update(q[hq:], k_ref[0, pl.ds(d0, TQ), :], v_ref[0, pl.ds(d0, TQ), :], (m[hq:], acc[hq:]), hq)
    o_ref[0, :hq, :] = (acc_lo / acc_lo[:, V_HEAD:V_HEAD + 1]).astype(BF16)
    o_ref[0, hq:, :] = (acc_hi / acc_hi[:, V_HEAD:V_HEAD + 1]).astype(BF16)


def _attention(q, k, v, B, S):
    HP = MLA_HEADS * HEAD_PAD
    q, k, v = (a.reshape(B, S, HP) for a in (q, k, v))
    o = pl.pallas_call(
        _attn_kernel,
        grid=(B, MLA_HEADS, S // TQ),
        in_specs=[pl.BlockSpec((1, TQ, HEAD_PAD), lambda b, h, i: (b, i, h)),
                  pl.BlockSpec((1, S, HEAD_PAD), lambda b, h, i: (b, 0, h)),
                  pl.BlockSpec((1, S, HEAD_PAD), lambda b, h, i: (b, 0, h))],
        out_specs=pl.BlockSpec((1, TQ, HEAD_PAD), lambda b, h, i: (b, i, h)),
        out_shape=jax.ShapeDtypeStruct((B, S, HP), BF16),
        compiler_params=_cparams("parallel", "parallel", "arbitrary"),
        name="mla_flash_attention",
    )(q, k, v)
    return o.reshape(B * S, HP)


CH2 = S5_CHUNK * S5_GROUP_CH
P2 = 2 * S5_STATE
SCAN_LEVELS_MAX = 16


S5_PREP_GROUPS = 4


def _s5_prep_kernel(n_levels, *refs):
    *io_refs, mt_acc = refs
    for k in range(S5_PREP_GROUPS):
        _s5_prep_group(n_levels, *(r.at[pl.ds(k, 1)] for r in io_refs), mt_acc)


def _s5_prep_group(n_levels, arow_ref, acol_ref, bt_ref, ct_ref, d_ref,
                   mt_ref, wt_ref, vt_ref, apow_ref, dvec_ref, mt_acc):
    P, H, L = S5_STATE, S5_GROUP_CH, S5_CHUNK

    def powers(ar, ai, dt, n):
        e = jnp.exp(n * (ar * dt))
        ang = n * (ai * dt)
        return e * jnp.cos(ang), e * jnp.sin(ang)

    def zoh_coef(ar, ai, dt):
        pr, pi = powers(ar, ai, dt, 1.0)
        nr, ni = pr - 1.0, pi
        den = ar * ar + ai * ai
        return (nr * ar + ni * ai) / den, (ni * ar - nr * ai) / den

    row = arow_ref[0]
    ar, ai, dt = row[0:1], row[1:2], jnp.exp(row[2:3])
    cr, ci = zoh_coef(ar, ai, dt)
    bt = bt_ref[0]
    btr = jnp.concatenate([bt[:H], bt[:H]], axis=1)
    bti = jnp.concatenate([bt[H:], bt[H:]], axis=1)
    bbr = cr * btr - ci * bti
    bbi = cr * bti + ci * btr
    lane = lax.broadcasted_iota(I32, (1, P2), 1)
    is_re = lane < P
    ridx = lax.broadcasted_iota(I32, (CH2, P2), 0) // H
    pwr, pwi = powers(ar, ai, dt, (L - 1 - ridx).astype(F32))
    rsel = (lax.broadcasted_iota(I32, (CH2, H), 0) % H == lax.broadcasted_iota(I32, (CH2, H), 1)).astype(F32)
    bbr_t = _dot_exact(rsel, bbr)
    bbi_t = _dot_exact(rsel, bbi)
    w_re = pwr * bbr_t - pwi * bbi_t
    w_im = pwr * bbi_t + pwi * bbr_t
    wt_ref[0] = jnp.where(is_re, w_re, w_im).astype(BF16)
    a_r, a_i = powers(ar, ai, dt, float(L))
    sign = jnp.where(is_re, -1.0, 1.0)
    for lvl in range(SCAN_LEVELS_MAX):
        if lvl < n_levels:
            apow_ref[0, lvl:lvl + 1, :] = a_r
            apow_ref[0, SCAN_LEVELS_MAX + lvl:SCAN_LEVELS_MAX + lvl + 1, :] = sign * a_i
            a_r, a_i = a_r * a_r - a_i * a_i, 2.0 * a_r * a_i
        else:
            apow_ref[0, lvl:lvl + 1, :] = jnp.zeros_like(a_r)
            apow_ref[0, SCAN_LEVELS_MAX + lvl:SCAN_LEVELS_MAX + lvl + 1, :] = jnp.zeros_like(a_r)

    col = acol_ref[0]
    arc, aic, dtc = col[:, 0:1], col[:, 1:2], jnp.exp(col[:, 2:3])
    ct = ct_ref[0]
    csel = (lax.broadcasted_iota(I32, (H, CH2), 1) % H == lax.broadcasted_iota(I32, (H, CH2), 0)).astype(F32)
    ctr = _dot_exact(ct[:, :H], csel)
    cti = _dot_exact(ct[:, H:], csel)
    sidx = (lax.broadcasted_iota(I32, (P, CH2), 1) // H).astype(F32)
    pr, pi = powers(arc, aic, dtc, sidx)
    g_r = ctr * pr - cti * pi
    g_i = ctr * pi + cti * pr
    a1r, a1i = powers(arc, aic, dtc, 1.0)
    v_r = g_r * a1r - g_i * a1i
    v_i = g_r * a1i + g_i * a1r
    vt_ref[0, :P, :] = v_r.astype(BF16)
    vt_ref[0, P:, :] = (-v_i).astype(BF16)
    bbr64, bbi64 = bbr[:, :P], bbi[:, :P]
    kt = _dot_exact(bbr64, g_r) - _dot_exact(bbi64, g_i)
    lane2 = lax.broadcasted_iota(I32, (H, CH2), 1)
    for r in range(L):
        blk = kt if r == 0 else jnp.where(lane2 >= r * H, pltpu.roll(kt, r * H, axis=1), 0.0)
        mt_acc[r * H:(r + 1) * H, :] = blk
    mt_ref[0] = mt_acc[...].astype(BF16)
    dvec_ref[0] = _dot_exact(d_ref[0], csel)


def _s5_prep(n_levels, s5_a_re, s5_a_im, s5_log_dt, s5_b_re, s5_b_im, s5_c_re, s5_c_im, s5_d):
    G, P, H = S5_GROUPS, S5_STATE, S5_GROUP_CH
    ldt = jnp.broadcast_to(s5_log_dt[:, None], (G, P))
    arow = jnp.stack([s5_a_re, s5_a_im, ldt], axis=1)
    arow = jnp.concatenate([arow, arow], axis=2)
    acol = jnp.stack([s5_a_re, s5_a_im, ldt], axis=2)
    bt = jnp.concatenate([s5_b_re.transpose(0, 2, 1), s5_b_im.transpose(0, 2, 1)], axis=1)
    ct = jnp.concatenate([s5_c_re.transpose(0, 2, 1), s5_c_im.transpose(0, 2, 1)], axis=2)
    d = s5_d.reshape(G, 1, H)
    blk = lambda *s: pl.BlockSpec((S5_PREP_GROUPS,) + s, lambda g: (g, 0, 0))
    return pl.pallas_call(
        functools.partial(_s5_prep_kernel, n_levels),
        grid=(G // S5_PREP_GROUPS,),
        in_specs=[blk(3, P2), blk(P, 3), blk(2 * H, P), blk(P, 2 * H), blk(1, H)],
        out_specs=[blk(CH2, CH2), blk(CH2, P2), blk(P2, CH2), blk(2 * SCAN_LEVELS_MAX, P2), blk(1, CH2)],
        out_shape=[jax.ShapeDtypeStruct((G, CH2, CH2), BF16),
                   jax.ShapeDtypeStruct((G, CH2, P2), BF16),
                   jax.ShapeDtypeStruct((G, P2, CH2), BF16),
                   jax.ShapeDtypeStruct((G, 2 * SCAN_LEVELS_MAX, P2), F32),
                   jax.ShapeDtypeStruct((G, 1, CH2), F32)],
        scratch_shapes=[pltpu.VMEM((CH2, CH2), F32)],
        compiler_params=_cparams("parallel"),
        name="s5_discretise",
    )(arow, acol, bt, ct, d)


def _gelu_tanh(y):
    return 0.5 * y * (1.0 + jnp.tanh(math.sqrt(2.0 / math.pi) * (y + 0.044715 * (y * y * y))))


GROUPS_PER_TILE = LANES // S5_GROUP_CH
STEPS_PER_TILE = LANES // S5_GROUP_CH


def _s5_scan_kernel(n_levels, u_ref, mt_ref, wt_ref, vt_ref, apow_ref, dvec_ref, y_ref):
    H, L = S5_GROUP_CH, S5_CHUNK
    nc = u_ref.shape[0] // L
    lane_blk = lax.broadcasted_iota(I32, (nc, LANES), 1) // H
    ridx = lax.broadcasted_iota(I32, (nc, P2), 0)
    steps = [u_ref[pl.ds(l, nc, stride=L), :] for l in range(L)]
    groups = range(GROUPS_PER_TILE)
    us, xs = [], []
    for gi in groups:
        halves = []
        for hh in range(L // STEPS_PER_TILE):
            acc = jnp.zeros((nc, LANES), F32)
            for l8 in range(STEPS_PER_TILE):
                src = steps[hh * STEPS_PER_TILE + l8]
                shift = ((l8 - gi) % STEPS_PER_TILE) * H
                moved = src if shift == 0 else pltpu.roll(src, shift, axis=1)
                acc = jnp.where(lane_blk == l8, moved, acc)
            halves.append(acc)
        u = jnp.concatenate(halves, axis=1)
        us.append(u)
        xs.append(_dot(u.astype(BF16), wt_ref[gi]))
    for lvl in range(n_levels):
        sh = 1 << lvl
        for gi in groups:
            apow = apow_ref[gi]
            prev = jnp.where(ridx >= sh, pltpu.roll(xs[gi], sh, axis=0), 0.0)
            a_r = apow[lvl:lvl + 1]
            a_i = apow[SCAN_LEVELS_MAX + lvl:SCAN_LEVELS_MAX + lvl + 1]
            xs[gi] = xs[gi] + a_r * prev + a_i * pltpu.roll(prev, S5_STATE, axis=1)
    outs = [jnp.zeros((nc, LANES), F32) for _ in range(L)]
    for gi in groups:
        x_in = jnp.where(ridx >= 1, pltpu.roll(xs[gi], 1, axis=0), 0.0)
        u = us[gi]
        y = _gelu_tanh(_dot(u.astype(BF16), mt_ref[gi]) + _dot(x_in.astype(BF16), vt_ref[gi]) + dvec_ref[gi] * u)
        for l in range(L):
            hh, l8 = divmod(l, STEPS_PER_TILE)
            src = y[:, hh * LANES:(hh + 1) * LANES]
            shift = ((gi - l8) % STEPS_PER_TILE) * H
            moved = src if shift == 0 else pltpu.roll(src, shift, axis=1)
            outs[l] = jnp.where(lane_blk == gi, moved, outs[l])
    for l in range(L):
        y_ref[pl.ds(l, nc, stride=L), :] = outs[l]


def _s5_branch(u, B, S, s5_params):
    G, L = S5_GROUPS, S5_CHUNK
    nc = S // L
    n_levels = max(1, (nc - 1).bit_length())
    assert n_levels <= SCAN_LEVELS_MAX and S5_CHUNK % STEPS_PER_TILE == 0
    mt, wt, vt, apow, dvec = _s5_prep(n_levels, *s5_params)
    n_lt = G // GROUPS_PER_TILE
    blk = lambda *s: pl.BlockSpec((GROUPS_PER_TILE,) + s, lambda b, t: (t, 0, 0))
    io = pl.BlockSpec((S, LANES), lambda b, t: (b, t))
    return pl.pallas_call(
        functools.partial(_s5_scan_kernel, n_levels),
        grid=(B, n_lt),
        in_specs=[io, blk(CH2, CH2), blk(CH2, P2), blk(P2, CH2), blk(2 * SCAN_LEVELS_MAX, P2), blk(1, CH2)],
        out_specs=io,
        out_shape=jax.ShapeDtypeStruct((B * S, S5_WIDTH), F32),
        compiler_params=_cparams("parallel", "parallel"),
        name="s5_chunk_scan",
    )(u, mt, wt, vt, apow, dvec)


def _merge_xattn_kernel(x_ref, o_ref, ys_ref, gm_ref, gs_ref, wmo_ref, wglu_ref, wout_ref, g1_ref, b1_ref,
                        k_ref, v_ref, wq_ref, wo_ref, g2_ref, b2_ref, x2_ref):
    y_mla = _dot(o_ref[0], wmo_ref[...])
    z = _dot(ys_ref[0].astype(BF16), wglu_ref[...])
    y_s5 = z[:, :D_MODEL] * jax.nn.sigmoid(z[:, D_MODEL:])
    merged = (jax.nn.sigmoid(gm_ref[0].astype(F32)) * y_mla
              + jax.nn.sigmoid(gs_ref[0].astype(F32)) * y_s5)
    mix = _dot(merged.astype(BF16), wout_ref[...])
    x1 = _layer_norm(DEEPNORM_ALPHA * x_ref[0] + mix, g1_ref[...], b1_ref[...])
    q = (_dot(x1.astype(BF16), wq_ref[...]) * (XATTN_HEAD_DIM ** -0.5)).astype(BF16)
    k = k_ref[0]
    v = v_ref[0]
    outs = []
    for h in range(XATTN_HEADS):
        sl = slice(h * XATTN_HEAD_DIM, (h + 1) * XATTN_HEAD_DIM)
        s = _dot_nt(q[:, sl], k[:, sl])
        p = jnp.exp(s - jnp.max(s, axis=-1, keepdims=True))
        o = _dot(p.astype(BF16), v[:, sl]) / jnp.sum(p, axis=-1, keepdims=True)
        outs.append(o.astype(BF16))
    xa = _dot(jnp.concatenate(outs, axis=1), wo_ref[...])
    x2_ref[0] = _layer_norm(DEEPNORM_ALPHA * x1 + xa, g2_ref[...], b2_ref[...])


def _memkv_kernel(mem_ref, g_ref, b_ref, w_ref, k_ref, v_ref):
    m = _layer_norm(mem_ref[0], g_ref[...], b_ref[...]).astype(BF16)
    kv = _dot(m, w_ref[...])
    hd = XATTN_HEADS * XATTN_HEAD_DIM
    k_ref[0] = kv[:, :hd].astype(BF16)
    v_ref[0] = kv[:, hd:].astype(BF16)


def _merge_cross_attention(x2d, o, ys, gm, gs, mem, B, S, w_mla_o, w_s5_glu, w_out, ln1_g, ln1_b,
                           mem_ln_g, mem_ln_b, w_xq, w_xkv, w_xo, ln2_g, ln2_b):
    M = mem.shape[1]
    hd = XATTN_HEADS * XATTN_HEAD_DIM
    HP = MLA_HEADS * HEAD_PAD
    row = lambda a: a.reshape(1, -1)
    wmo = jnp.pad(w_mla_o.reshape(MLA_HEADS, V_HEAD, D_MODEL), ((0, 0), (0, HEAD_PAD - V_HEAD), (0, 0)))
    wmo = wmo.reshape(HP, D_MODEL).astype(BF16)
    k, v = pl.pallas_call(
        _memkv_kernel,
        grid=(B,),
        in_specs=[pl.BlockSpec((1, M, D_MODEL), lambda b: (b, 0, 0)), _full((1, D_MODEL)), _full((1, D_MODEL)),
                  _full((D_MODEL, 2 * hd))],
        out_specs=[pl.BlockSpec((1, M, hd), lambda b: (b, 0, 0))] * 2,
        out_shape=[jax.ShapeDtypeStruct((B, M, hd), BF16)] * 2,
        compiler_params=_cparams("parallel"),
        name="memory_kv",
    )(mem, row(mem_ln_g), row(mem_ln_b), w_xkv.astype(BF16))
    tile = lambda w: pl.BlockSpec((1, TM, w), lambda b, i: (b, i, 0))
    kv_spec = pl.BlockSpec((1, M, hd), lambda b, i: (b, 0, 0))
    r3 = lambda a: a.reshape(B, S, a.shape[-1])
    x2 = pl.pallas_call(
        _merge_xattn_kernel,
        grid=(B, S // TM),
        in_specs=[tile(D_MODEL), tile(HP), tile(S5_WIDTH), tile(D_MODEL), tile(D_MODEL),
                  _full((HP, D_MODEL)), _full((S5_WIDTH, 2 * D_MODEL)), _full((D_MODEL, D_MODEL)),
                  _full((1, D_MODEL)), _full((1, D_MODEL)),
                  kv_spec, kv_spec, _full((D_MODEL, hd)), _full((hd, D_MODEL)), _full((1, D_MODEL)), _full((1, D_MODEL))],
        out_specs=tile(D_MODEL),
        out_shape=jax.ShapeDtypeStruct((B, S, D_MODEL), F32),
        compiler_params=_cparams("parallel", "parallel"),
        name="merge_ln1_xattn_ln2",
    )(r3(x2d), r3(o), r3(ys), r3(gm), r3(gs), wmo, w_s5_glu.astype(BF16), w_out.astype(BF16), row(ln1_g), row(ln1_b),
      k, v, w_xq.astype(BF16), w_xo.astype(BF16), row(ln2_g), row(ln2_b))
    return x2.reshape(B * S, D_MODEL)


def _route_sort_kernel(x_ref, w_ref, bias_ref, xs_ref, pos_ref, cnt_ref, off_ref):
    E, per = N_EXPERTS, N_EXPERTS // N_EXPERT_GROUPS
    tm = x_ref.shape[0]
    x = x_ref[...]
    xb = x.astype(BF16)
    x_lo = (x - xb.astype(F32)).astype(BF16)
    w = w_ref[...]
    wb = w.astype(BF16)
    w_lo = (w - wb.astype(F32)).astype(BF16)
    logits = _dot_nt(wb, xb) + (_dot_nt(wb, x_lo) + _dot_nt(w_lo, xb))
    scores = jax.nn.sigmoid(logits)
    sel = scores + bias_ref[...]
    neg = -jnp.inf
    i8 = lax.broadcasted_iota(I32, (per, tm), 0)
    gscore = []
    for g in range(N_EXPERT_GROUPS):
        blk = sel[g * per:(g + 1) * per]
        m1 = jnp.max(blk, axis=0, keepdims=True)
        i1 = jnp.min(jnp.where(blk == m1, i8, per), axis=0, keepdims=True)
        m2 = jnp.max(jnp.where(i8 == i1, neg, blk), axis=0, keepdims=True)
        gscore.append(m1 + m2)
    blocks = []
    for g in range(N_EXPERT_GROUPS):
        ahead = jnp.zeros((1, tm), I32)
        for o in range(N_EXPERT_GROUPS):
            if o == g:
                continue
            before = (gscore[o] >= gscore[g]) if o < g else (gscore[o] > gscore[g])
            ahead = ahead + before.astype(I32)
        blocks.append(jnp.where(ahead < TOPK_GROUPS, sel[g * per:(g + 1) * per], neg))
    cur = jnp.concatenate(blocks, axis=0)
    ie = lax.broadcasted_iota(I32, (E, tm), 0)
    picked = jnp.zeros((E, tm), F32)
    for _ in range(TOP_K):
        m = jnp.max(cur, axis=0, keepdims=True)
        idx = jnp.min(jnp.where(cur == m, ie, E), axis=0, keepdims=True)
        hit = ie == idx
        picked = jnp.where(hit, 1.0, picked)
        cur = jnp.where(hit, neg, cur)
    wsel = scores * picked
    wnorm = wsel / jnp.sum(wsel, axis=0, keepdims=True) * ROUTED_SCALE
    pb = picked.astype(BF16)
    tri = (lax.broadcasted_iota(I32, (tm, tm), 0) <= lax.broadcasted_iota(I32, (tm, tm), 1)).astype(BF16)
    incl = _dot(pb, tri)
    cnt_col = jnp.sum(picked, axis=1, keepdims=True)
    lower = (lax.broadcasted_iota(I32, (E, E), 1) < lax.broadcasted_iota(I32, (E, E), 0)).astype(F32)
    al_col = jnp.floor((cnt_col + (SEG_ALIGN - 1.0)) * (1.0 / SEG_ALIGN)) * SEG_ALIGN
    off_col = _dot_exact(lower, jnp.broadcast_to(al_col, (E, LANES)))[:, 0:1]
    pos = off_col + incl - 1.0
    q = jnp.floor(pos * (1.0 / POS_RADIX))
    pq = jnp.concatenate([jnp.where(picked > 0.0, POS_RADIX * q, POS_RADIX * POS_NONE_Q),
                          jnp.where(picked > 0.0, pos - POS_RADIX * q, 0.0)], axis=0)
    pos_ref[...] = pq
    pqb = pq.astype(BF16)
    cnt_ref[0] = jnp.broadcast_to(cnt_col, (E, LANES)).astype(I32)
    off_ref[0] = jnp.broadcast_to(off_col, (E, LANES)).astype(I32)
    cnt_row = _dot_nt(jnp.ones((8, tm), BF16), pb)
    al_row = jnp.floor((cnt_row + (SEG_ALIGN - 1.0)) * (1.0 / SEG_ALIGN)) * SEG_ALIGN
    er = lax.broadcasted_iota(I32, (E, 4 * E), 0)
    ec = lax.broadcasted_iota(I32, (E, 4 * E), 1) % E
    off_row4 = _dot_exact(al_row, (er < ec).astype(F32))[0:1]
    end_row4 = off_row4 + _dot_exact(al_row, (er == ec).astype(F32))[0:1]
    w_hi = wnorm.astype(BF16).astype(F32)
    w_r1 = wnorm - w_hi
    w_mid = w_r1.astype(BF16).astype(F32)
    w_lo = w_r1 - w_mid
    w4_t = jnp.concatenate([w_hi, w_mid, w_lo, jnp.zeros_like(w_lo)], axis=0).T
    xw = jnp.concatenate([xb, w4_t.astype(BF16)], axis=1)
    half = D_MODEL // 2

    def build(c, carry):
        r0 = pl.multiple_of(c * OH_ROWS, OH_ROWS)
        jr = (lax.broadcasted_iota(I32, (OH_ROWS, 4 * E), 0) + r0).astype(F32)
        member4 = jnp.where(jr >= off_row4, jnp.where(jr < end_row4, 1.0, 0.0), 0.0)
        target = _dot(member4[:, :2 * E].astype(BF16), pqb)
        j = (lax.broadcasted_iota(I32, (OH_ROWS, tm), 0) + r0).astype(F32)
        ohb = jnp.where(target == j, 1.0, 0.0).astype(BF16)
        rows = _dot(ohb, xw)
        xs_ref[0, pl.ds(r0, OH_ROWS), 0:half] = _pack_bf16_pair(rows[:, :half], rows[:, half:D_MODEL])
        wrow = jnp.sum(member4 * rows[:, D_MODEL:], axis=1, keepdims=True)
        xs_ref[0, pl.ds(r0, OH_ROWS), half:] = pltpu.bitcast(jnp.broadcast_to(wrow, (OH_ROWS, LANES)), I32)
        return carry

    lax.fori_loop(0, SORT_ROWS // OH_ROWS, build, 0)
    xs_ref[0, SORT_ROWS:, :] = jnp.zeros((PAD_ROWS, XW), I32)


def _route_sort(x2, w_router, router_bias):
    T = x2.shape[0]
    E = N_EXPERTS
    nt = T // TW
    seg = pl.BlockSpec((1, E, LANES), lambda i: (i, 0, 0))
    return pl.pallas_call(
        _route_sort_kernel,
        grid=(nt,),
        in_specs=[pl.BlockSpec((TW, D_MODEL), lambda i: (i, 0)), _full((E, D_MODEL)), _full((E, 1))],
        out_specs=[pl.BlockSpec((1, TS, XW), lambda i: (i, 0, 0)),
                   pl.BlockSpec((2 * E, TW), lambda i: (0, i)), seg, seg],
        out_shape=[jax.ShapeDtypeStruct((nt, TS, XW), I32), jax.ShapeDtypeStruct((2 * E, T), F32),
                   jax.ShapeDtypeStruct((nt, E, LANES), I32), jax.ShapeDtypeStruct((nt, E, LANES), I32)],
        compiler_params=_cparams("parallel"),
        name="route_local_sort",
    )(x2, w_router.T, router_bias.reshape(E, 1))


def _plan_kernel(cnt_ref, off_ref, pa_ref, ce_ref, gn_ref, nch_ref):
    E, nt = cnt_ref.shape
    n_rows = pa_ref.shape[0]
    cnt = cnt_ref[...].astype(F32)
    off = off_ref[...].astype(F32)
    npc = jnp.floor((cnt + (PR - 1.0)) * (1.0 / PR))
    tri = (lax.broadcasted_iota(I32, (nt, nt), 0) <= lax.broadcasted_iota(I32, (nt, nt), 1)).astype(F32)
    p_incl = _dot_exact(npc, tri)
    total = p_incl[:, nt - 1:nt]
    n_ch = jnp.floor((total + (PPC - 1.0)) * (1.0 / PPC))
    lower = (lax.broadcasted_iota(I32, (E, E), 1) < lax.broadcasted_iota(I32, (E, E), 0)).astype(F32)
    wide = lambda col: jnp.broadcast_to(col, (E, LANES))
    g0 = _dot_exact(lower, wide(n_ch))[:, 0:1]
    n_total = jnp.sum(n_ch, axis=0, keepdims=True)
    g0_row = wide(g0).T[0:1, :E]
    g1_row = g0_row + wide(n_ch).T[0:1, :E]
    g_e = lax.broadcasted_iota(I32, (n_rows, E), 0).astype(F32)
    sel = jnp.where(g_e >= g0_row, jnp.where(g_e < g1_row, 1.0, 0.0), 0.0)
    tile_base = lax.broadcasted_iota(I32, (E, nt), 1).astype(F32) * TS
    a_mat = tile_base + off - PR * (p_incl - npc)
    p_g = _dot_exact(sel, p_incl)
    a_g = _dot_exact(sel, a_mat)
    e_idx = lax.broadcasted_iota(I32, (E, LANES), 0).astype(F32)
    pick = lambda col: _dot_exact(sel, col)[:, 0:1]
    g0_g, tot_g, e_g = pick(wide(g0)), pick(wide(total)), pick(e_idx)
    gn_g = pick(wide(g0 + n_ch))
    g_p = lax.broadcasted_iota(I32, (n_rows, PPC), 0).astype(F32)
    q = (g_p - g0_g) * PPC + lax.broadcasted_iota(I32, (n_rows, PPC), 1).astype(F32)
    addr = jnp.full((n_rows, PPC), -float(SEG_ALIGN), F32)
    lo = jnp.zeros((n_rows, 1), F32)
    for i in range(nt):
        hi = p_g[:, i:i + 1]
        addr = jnp.where(q >= lo, jnp.where(q < hi, a_g[:, i:i + 1] + PR * q, addr), addr)
        lo = hi
    pa_ref[...] = addr.astype(I32)
    g_col = lax.broadcasted_iota(I32, (n_rows, LANES), 0).astype(F32)
    ce_ref[...] = jnp.where(g_col < n_total, jnp.broadcast_to(e_g, (n_rows, LANES)), E - 1.0).astype(I32)
    gn_ref[...] = jnp.broadcast_to(gn_g, (n_rows, LANES)).astype(I32)
    nch_ref[...] = jnp.broadcast_to(n_total, nch_ref.shape).astype(I32)


def _plan(cnt, off, n_chunks_max):
    n_rows = (n_chunks_max + 7) // 8 * 8
    pa, ce, gn, nch = pl.pallas_call(
        _plan_kernel,
        out_shape=[jax.ShapeDtypeStruct((n_rows, PPC), I32), jax.ShapeDtypeStruct((n_rows, LANES), I32),
                   jax.ShapeDtypeStruct((n_rows, LANES), I32), jax.ShapeDtypeStruct((8, LANES), I32)],
        name="moe_plan",
    )(cnt, off)
    return pa.reshape(n_rows * PPC), ce[:, 0], gn[:, 0], nch[0, :1]


def _unpack_bf16_pair(w):
    lo = pltpu.bitcast(lax.shift_left(w, jnp.int32(16)), F32).astype(BF16)
    hi = pltpu.bitcast(w & jnp.int32(-65536), F32).astype(BF16)
    return lo, hi


def _pack_bf16_pair(lo, hi):
    lo_bits = lax.shift_right_logical(pltpu.bitcast(lo.astype(BF16).astype(F32), I32), jnp.int32(16))
    hi_bits = pltpu.bitcast(hi.astype(BF16).astype(F32), I32)
    return hi_bits | lo_bits


def _expert_kernel(ce_ref, gn_ref, pa_ref, nch_ref, tot_ref, xs_hbm, wgu_hbm, wdn_hbm, y_hbm,
                   xbuf, ybuf, zbuf, wgu_f, wdn_f, wgu_b, wdn_b, wslot, gsem, wsem, zsem, esem):
    g = pl.program_id(0)
    n = nch_ref[0]
    slot = g % N_SLOTS
    half = D_MODEL // 2

    dummy_base = (tot_ref.shape[0] - 1) * TS

    def gather_copy(s, a, jp):
        return pltpu.make_async_copy(xs_hbm.at[pl.ds(pl.multiple_of(a, SEG_ALIGN), PR), :],
                                     xbuf.at[s, pl.ds(jp * PR, PR), :], gsem.at[s])

    def write_copy(s, a, jp):
        return pltpu.make_async_copy(ybuf.at[s, pl.ds(jp * PR, PR), :],
                                     y_hbm.at[pl.ds(pl.multiple_of(a, SEG_ALIGN), PR), :], wsem.at[s])

    def gather_start(c, s):
        for jp in range(PPC):
            gather_copy(s, jnp.maximum(pa_ref[c * PPC + jp], 0), jp).start()

    def write_start(c, s):
        for jp in range(PPC):
            a = pa_ref[c * PPC + jp]
            write_copy(s, jnp.where(a >= 0, a, dummy_base + jp * PR), jp).start()

    def gather_wait(s):
        for jp in range(PPC):
            gather_copy(s, 0, jp).wait()

    def write_wait(s):
        for jp in range(PPC):
            write_copy(s, 0, jp).wait()

    def weight_copies(e, s):
        return (pltpu.make_async_copy(wgu_hbm.at[e], wgu_f.at[s], esem.at[s]),
                pltpu.make_async_copy(wdn_hbm.at[e], wdn_f.at[s], esem.at[s]))

    @pl.when(g == 0)
    def _():
        wslot[0] = 0
        for cp in weight_copies(ce_ref[0], 0):
            cp.start()
        gather_start(0, 0)
        ybuf[...] = jnp.zeros_like(ybuf)
        zbuf[...] = jnp.zeros_like(zbuf)

        def zero_copy(row):
            return pltpu.make_async_copy(zbuf, y_hbm.at[pl.ds(pl.multiple_of(row, SEG_ALIGN), SEG_ALIGN), :], zsem)

        def tail_of(i):
            return i * TS + tot_ref[i], (TS - tot_ref[i]) // SEG_ALIGN

        def zero_start(i, c):
            base, cnt8 = tail_of(i)
            lax.fori_loop(0, cnt8, lambda q, z: (zero_copy(base + q * SEG_ALIGN).start(), z)[1], 0)
            return c

        def zero_wait(i, c):
            base, cnt8 = tail_of(i)
            lax.fori_loop(0, cnt8, lambda q, z: (zero_copy(base + q * SEG_ALIGN).wait(), z)[1], 0)
            return c

        lax.fori_loop(0, tot_ref.shape[0], zero_start, 0)
        lax.fori_loop(0, tot_ref.shape[0], zero_wait, 0)

    for c in range(1, GATHER_AHEAD):
        @pl.when((g == 0) & (c < n))
        def _(c=c):
            gather_start(c, c)

    @pl.when(g + GATHER_AHEAD < n)
    def _():
        gather_start(g + GATHER_AHEAD, (g + GATHER_AHEAD) % N_SLOTS)

    e_cur = ce_ref[g]
    e_prev = ce_ref[jnp.maximum(g - 1, 0)]
    e_prev2 = ce_ref[jnp.maximum(g - 2, 0)]

    @pl.when(g < n)
    def _():
        @pl.when((g == 0) | (e_cur != e_prev))
        def _():
            s = wslot[0]
            for cp in weight_copies(e_cur, s):
                cp.wait()
            wgu_b[...] = wgu_f[s].astype(BF16)
            wdn_b[...] = wdn_f[s].astype(BF16)
            g_next = gn_ref[g]

            @pl.when(g_next < n)
            def _():
                for cp in weight_copies(ce_ref[g_next], 1 - s):
                    cp.start()

            wslot[0] = 1 - s

        gather_wait(slot)

        @pl.when((g >= 2) & (e_prev == e_prev2))
        def _():
            write_wait((g - 2) % N_SLOTS)

        @pl.when((g >= 1) & (e_cur != e_prev))
        def _():
            write_wait((g - 1) % N_SLOTS)

        def swiglu_rows(rows):
            xw = xbuf[slot, :rows, :]
            lo, hi = _unpack_bf16_pair(xw[:, :half])
            h = _dot(jnp.concatenate([lo, hi], axis=1), wgu_b[...])
            gate, up = h[:, :EXPERT_FF], h[:, EXPERT_FF:]
            act = (gate * jax.nn.sigmoid(gate) * up).astype(BF16)
            y = _dot(act, wdn_b[...])
            w_row = pltpu.bitcast(xw[:, half:], F32)
            y = y * jnp.concatenate([w_row] * (D_MODEL // LANES), axis=1)
            ybuf[slot, :rows, :] = _pack_bf16_pair(y[:, :half], y[:, half:])

        second_half_used = pa_ref[g * PPC + PPC // 2] >= 0

        @pl.when(second_half_used)
        def _():
            swiglu_rows(CR)

        @pl.when(jnp.logical_not(second_half_used))
        def _():
            swiglu_rows(CR // 2)

        write_start(g, slot)

    @pl.when(g == pl.num_programs(0) - 1)
    def _():
        @pl.when((n >= 2) & (ce_ref[jnp.maximum(n - 1, 0)] == ce_ref[jnp.maximum(n - 2, 0)]))
        def _():
            write_wait((n - 2) % N_SLOTS)

        write_wait((n - 1) % N_SLOTS)


def _experts(ce, gn, pa, nch, tot, xs, w_exp_gu, w_exp_down):
    n_rows = xs.shape[0]
    half = D_MODEL // 2
    hbm = pl.BlockSpec(memory_space=pl.ANY)
    grid_spec = pltpu.PrefetchScalarGridSpec(
        num_scalar_prefetch=5,
        grid=(ce.shape[0],),
        in_specs=[hbm, hbm, hbm],
        out_specs=hbm,
        scratch_shapes=[pltpu.VMEM((N_SLOTS, CR, XW), I32), pltpu.VMEM((N_SLOTS, CR, half), I32),
                        pltpu.VMEM((SEG_ALIGN, half), I32),
                        pltpu.VMEM((2, D_MODEL, 2 * EXPERT_FF), F32), pltpu.VMEM((2, EXPERT_FF, D_MODEL), F32),
                        pltpu.VMEM((D_MODEL, 2 * EXPERT_FF), BF16), pltpu.VMEM((EXPERT_FF, D_MODEL), BF16),
                        pltpu.SMEM((1,), I32),
                        pltpu.SemaphoreType.DMA((N_SLOTS,)), pltpu.SemaphoreType.DMA((N_SLOTS,)),
                        pltpu.SemaphoreType.DMA(()), pltpu.SemaphoreType.DMA((2,))],
    )
    return pl.pallas_call(
        _expert_kernel,
        grid_spec=grid_spec,
        out_shape=jax.ShapeDtypeStruct((n_rows + TS, half), I32),
        compiler_params=_cparams("arbitrary"),
        name="moe_grouped_swiglu",
    )(ce, gn, pa, nch, tot, xs, w_exp_gu, w_exp_down)


def _combine_kernel(tot_ref, y_ref, pos_ref, cnt_ref, off_ref, x2_ref, wsg_ref, wsd_ref, g_ref, b_ref, out_ref,
                    acc_lo, acc_hi):
    tm = x2_ref.shape[0]
    E = N_EXPERTS
    n_sorted = tot_ref[pl.program_id(0)]
    x2 = x2_ref[...]
    h = _dot(x2.astype(BF16), wsg_ref[...])
    gate, up = h[:, :SHARED_FF], h[:, SHARED_FF:]
    shared = _dot((gate * jax.nn.sigmoid(gate) * up).astype(BF16), wsd_ref[...])
    pq_t = pos_ref[...].T.astype(BF16)
    off_col = off_ref[0][:, 0:1].astype(F32)
    cnt_col = cnt_ref[0][:, 0:1].astype(F32)
    end_col = off_col + jnp.floor((cnt_col + (SEG_ALIGN - 1.0)) * (1.0 / SEG_ALIGN)) * SEG_ALIGN
    off_col2 = jnp.concatenate([off_col, off_col], axis=0)
    end_col2 = jnp.concatenate([end_col, end_col], axis=0)
    acc_lo[...] = jnp.zeros_like(acc_lo)
    acc_hi[...] = jnp.zeros_like(acc_hi)

    def gather_back(c, carry):
        r0 = pl.multiple_of(c * OH_ROWS, OH_ROWS)
        jc = (lax.broadcasted_iota(I32, (2 * E, OH_ROWS), 1) + r0).astype(F32)
        member2 = jnp.where(jc >= off_col2, jnp.where(jc < end_col2, 1.0, 0.0), 0.0)
        target = _dot(pq_t, member2.astype(BF16))
        j = (lax.broadcasted_iota(I32, (tm, OH_ROWS), 1) + r0).astype(F32)
        ohb = jnp.where(target == j, 1.0, 0.0).astype(BF16)
        yw = y_ref[0, pl.ds(r0, OH_ROWS), :]
        row = lax.broadcasted_iota(I32, yw.shape, 0) + r0
        lo, hi = _unpack_bf16_pair(jnp.where(row < n_sorted, yw, 0))
        acc_lo[...] += _dot(ohb, lo)
        acc_hi[...] += _dot(ohb, hi)
        return carry

    lax.fori_loop(0, SORT_ROWS // OH_ROWS, gather_back, 0)
    ff = shared + jnp.concatenate([acc_lo[...], acc_hi[...]], axis=1)
    out_ref[...] = _layer_norm(DEEPNORM_ALPHA * x2 + ff, g_ref[...], b_ref[...])


def _combine(tot, y, pos, cnt, off, x2, w_sh_gu, w_sh_down, ln_g, ln_b):
    T = x2.shape[0]
    nt = T // TW
    half = D_MODEL // 2
    tile = lambda w: pl.BlockSpec((TW, w), lambda i: (i, 0))
    return pl.pallas_call(
        _combine_kernel,
        grid=(nt,),
        in_specs=[pl.BlockSpec(memory_space=pltpu.SMEM),
                  pl.BlockSpec((1, TS, half), lambda i: (i, 0, 0)),
                  pl.BlockSpec((2 * N_EXPERTS, TW), lambda i: (0, i)),
                  pl.BlockSpec((1, N_EXPERTS, LANES), lambda i: (i, 0, 0)),
                  pl.BlockSpec((1, N_EXPERTS, LANES), lambda i: (i, 0, 0)), tile(D_MODEL),
                  _full((D_MODEL, 2 * SHARED_FF)), _full((SHARED_FF, D_MODEL)),
                  _full((1, D_MODEL)), _full((1, D_MODEL))],
        out_specs=tile(D_MODEL),
        out_shape=jax.ShapeDtypeStruct((T, D_MODEL), F32),
        scratch_shapes=[pltpu.VMEM((TW, half), F32), pltpu.VMEM((TW, half), F32)],
        compiler_params=_cparams("parallel"),
        name="moe_combine_shared_ln3",
    )(tot, y.reshape(nt + 1, TS, half), pos, cnt, off, x2, w_sh_gu.astype(BF16), w_sh_down.astype(BF16),
      ln_g.reshape(1, -1), ln_b.reshape(1, -1))


def _moe(x2, w_router, router_bias, w_exp_gu, w_exp_down, w_sh_gu, w_sh_down, ln_g, ln_b):
    T = x2.shape[0]
    E = N_EXPERTS
    nt = T // TW
    xs, pos, cnt, off = _route_sort(x2, w_router, router_bias)
    n_pieces_max = (T * TOP_K + nt * E * (PR - 1)) // PR
    n_chunks_max = n_pieces_max // PPC + E
    pa, ce, gn, nch = _plan(cnt[:, :, 0].T, off[:, :, 0].T, n_chunks_max)
    last_al = (cnt[:, E - 1, 0] + SEG_ALIGN - 1) // SEG_ALIGN * SEG_ALIGN
    tot = jnp.concatenate([off[:, E - 1, 0] + last_al, jnp.zeros((1,), I32)])
    y = _experts(ce, gn, pa, nch, tot, xs.reshape(nt * TS, XW), w_exp_gu, w_exp_down)
    return _combine(tot, y, pos, cnt, off, x2, w_sh_gu, w_sh_down, ln_g, ln_b)


def kernel(x, mem, positions, w_in, q_norm_g, kv_norm_g, w_uq, w_ukv, w_mla_o, s5_a_re, s5_a_im, s5_log_dt, s5_b_re, s5_b_im, s5_c_re, s5_c_im, s5_d, w_s5_glu, w_out, ln1_g, ln1_b, mem_ln_g, mem_ln_b, w_xq, w_xkv, w_xo, ln2_g, ln2_b, w_router, router_bias, w_exp_gu, w_exp_down, w_sh_gu, w_sh_down, ln3_g, ln3_b):
    B, S, D = x.shape
    assert D == D_MODEL and S % TQ == 0 and (B * S) % TM == 0 and S % S5_CHUNK == 0
    xc = x.reshape(B * S, D)
    for l in range(w_in.shape[0]):
        tabs = _rope_tables(positions)
        q, k, v, u, gm, gs = _inproj(xc, tabs, w_in[l], q_norm_g[l], kv_norm_g[l], w_uq[l], w_ukv[l])
        o = _attention(q, k, v, B, S)
        ys = _s5_branch(u, B, S, (s5_a_re[l], s5_a_im[l], s5_log_dt[l], s5_b_re[l], s5_b_im[l],
                                  s5_c_re[l], s5_c_im[l], s5_d[l]))
        x2 = _merge_cross_attention(xc, o, ys, gm, gs, mem, B, S, w_mla_o[l], w_s5_glu[l], w_out[l],
                                    ln1_g[l], ln1_b[l], mem_ln_g[l], mem_ln_b[l], w_xq[l], w_xkv[l], w_xo[l],
                                    ln2_g[l], ln2_b[l])
        xc = _moe(x2, w_router[l], router_bias[l], w_exp_gu[l], w_exp_down[l], w_sh_gu[l],
                  w_sh_down[l], ln3_g[l], ln3_b[l])
    return xc.reshape(B, S, D)
```

```python
import functools
import math

import jax
import jax.numpy as jnp
import numpy as np
from jax import lax
from jax.experimental import pallas as pl
from jax.experimental.pallas import tpu as pltpu

F32 = jnp.float32
BF16 = jnp.bfloat16
I32 = jnp.int32

D_MODEL = 1024
MLA_HEADS = 8
QK_NOPE = 64
QK_ROPE = 32
V_HEAD = 64
Q_LORA = 256
KV_LORA = 256
ROPE_THETA = 10000.0
S5_GROUP_CH = 16
S5_WIDTH = 512
S5_GROUPS = 32
S5_STATE = 64
XATTN_HEADS = 4
XATTN_HEAD_DIM = 128
N_EXPERTS = 64
TOP_K = 8
N_EXPERT_GROUPS = 8
TOPK_GROUPS = 4
EXPERT_FF = 256
SHARED_FF = 256
ROUTED_SCALE = 2.5
LN_EPS = 1e-5
RMS_EPS = 1e-6
DEPTH = 1
DEEPNORM_ALPHA = (2.0 * DEPTH) ** 0.25

LANES = 128
HEAD_PAD = 128
ROPE_LO = QK_NOPE
ROPE_HALF = QK_ROPE // 2

TM = 512
TQ = 1024
KV_GROUPS = (2, 1)
S5_CHUNK = 16
TW = 512
OH_ROWS = 1536
PR = 16
PAD_ROWS = PR
SEG_ALIGN = 8
SORT_ROWS = TOP_K * TW + N_EXPERTS * SEG_ALIGN
TS = SORT_ROWS + PAD_ROWS
XW = D_MODEL // 2 + LANES
CR = 512
PPC = CR // PR
GATHER_AHEAD = 3
N_SLOTS = GATHER_AHEAD + 1
POS_RADIX = 64
POS_NONE_Q = 127
assert SORT_ROWS <= POS_RADIX * POS_NONE_Q and SORT_ROWS % OH_ROWS == 0
VMEM_LIMIT = 48 * 1024 * 1024


def _cparams(*sem):
    return pltpu.CompilerParams(dimension_semantics=sem, vmem_limit_bytes=VMEM_LIMIT)


def _dot(a, b):
    return jnp.dot(a, b, preferred_element_type=F32)


def _dot_nt(a, b, precision=None):
    return lax.dot_general(a, b, (((1,), (1,)), ((), ())), preferred_element_type=F32,
                           precision=precision)


def _dot_exact(a, b):
    return jnp.dot(a, b, preferred_element_type=F32, precision=lax.Precision.HIGHEST)


def _layer_norm(h, g, b):
    mu = jnp.mean(h, axis=-1, keepdims=True)
    c = h - mu
    var = jnp.mean(c * c, axis=-1, keepdims=True)
    return c * lax.rsqrt(var + LN_EPS) * g + b


def _rms_norm(h, g):
    return h * lax.rsqrt(jnp.mean(h * h, axis=-1, keepdims=True) + RMS_EPS) * g


def _full(shape):
    n = len(shape)
    return pl.BlockSpec(shape, lambda *_: (0,) * n)


def _rope_angle_kernel(pos_ref, invf_ref, cos_ref, sin_ref):
    ang = pos_ref[...].astype(F32) * invf_ref[...]
    cos_ref[...] = jnp.cos(ang)
    sin_ref[...] = jnp.sin(ang)


def _rope_tables(positions):
    T = positions.size
    rows = T * ROPE_HALF // LANES
    pos_rep = jnp.repeat(positions.reshape(T), ROPE_HALF).reshape(rows, LANES)
    inv_freq = ROPE_THETA ** (-jnp.arange(0, QK_ROPE, 2, dtype=F32) / QK_ROPE)
    invf = jnp.tile(inv_freq, LANES // ROPE_HALF).reshape(1, LANES)
    cos, sin = pl.pallas_call(
        _rope_angle_kernel,
        out_shape=(jax.ShapeDtypeStruct((rows, LANES), F32),) * 2,
        name="rope_angles",
    )(pos_rep, invf)
    cos_sin = jnp.concatenate([cos.reshape(T, ROPE_HALF), sin.reshape(T, ROPE_HALF)], axis=1)
    f = np.arange(ROPE_HALF)
    spread = np.zeros((QK_ROPE, 3 * HEAD_PAD), np.float32)
    spread[f, ROPE_LO + f] = 1.0
    spread[f, ROPE_LO + ROPE_HALF + f] = 1.0
    spread[ROPE_HALF + f, HEAD_PAD + ROPE_LO + f] = -1.0
    spread[ROPE_HALF + f, 2 * HEAD_PAD + ROPE_LO + ROPE_HALF + f] = 1.0
    outside = np.ones((1, HEAD_PAD), np.float32)
    outside[0, ROPE_LO:ROPE_LO + QK_ROPE] = 0.0
    return cos_sin, jnp.asarray(spread), jnp.asarray(outside)


def _rope(x, c_tab, s_up, s_dn):
    return (x * c_tab + pltpu.roll(x, HEAD_PAD - ROPE_HALF, axis=1) * s_up
            + pltpu.roll(x, ROPE_HALF, axis=1) * s_dn)


def _inproj_kernel(x_ref, cs_ref, spread_ref, outside_ref, wlat_ref, wu_ref, wgm_ref, wgs_ref, qg_ref, kvg_ref,
                   wuq_ref, wuk_ref, wuv_ref, q_ref, k_ref, v_ref, u_ref, gm_ref, gs_ref):
    xb = x_ref[...].astype(BF16)
    lat = _dot(xb, wlat_ref[...])
    qn = _rms_norm(lat[:, :Q_LORA], qg_ref[...]).astype(BF16)
    kvn = _rms_norm(lat[:, Q_LORA:Q_LORA + KV_LORA], kvg_ref[...]).astype(BF16)
    tabs = _dot_exact(cs_ref[...], spread_ref[...])
    c_tab = tabs[:, :HEAD_PAD] + outside_ref[...]
    s_up, s_dn = tabs[:, HEAD_PAD:2 * HEAD_PAD], tabs[:, 2 * HEAD_PAD:]
    k_rope = _rope(lat[:, Q_LORA + KV_LORA:], c_tab, s_up, s_dn)
    q = _dot(qn, wuq_ref[...])
    k = _dot(kvn, wuk_ref[...])
    scale = (QK_NOPE + QK_ROPE) ** -0.5 * math.log2(math.e)
    for h in range(MLA_HEADS):
        sl = slice(h * HEAD_PAD, (h + 1) * HEAD_PAD)
        q_ref[:, sl] = (_rope(q[:, sl], c_tab, s_up, s_dn) * scale).astype(BF16)
        k_ref[:, sl] = (k[:, sl] + k_rope).astype(BF16)
    v = _dot(kvn, wuv_ref[...])
    ones_lane = lax.broadcasted_iota(I32, (1, v.shape[1]), 1) % HEAD_PAD == V_HEAD
    v_ref[...] = jnp.where(ones_lane, 1.0, v).astype(BF16)
    u_ref[...] = _dot(xb, wu_ref[...])
    gm_ref[...] = _dot(xb, wgm_ref[...]).astype(BF16)
    gs_ref[...] = _dot(xb, wgs_ref[...]).astype(BF16)


def _pad_heads(w, head_w, lo_w):
    K = w.shape[0]
    w = w.reshape(K, MLA_HEADS, head_w)[:, :, :lo_w]
    w = jnp.pad(w, ((0, 0), (0, 0), (0, HEAD_PAD - lo_w)))
    return w.reshape(K, MLA_HEADS * HEAD_PAD)


def _inproj(x2d, tabs, w_in, q_norm_g, kv_norm_g, w_uq, w_ukv):
    T = x2d.shape[0]
    o_rope = Q_LORA + KV_LORA
    o_u = o_rope + QK_ROPE
    o_gm = o_u + S5_WIDTH
    o_gs = o_gm + D_MODEL
    w_rope = jnp.pad(w_in[:, o_rope:o_u], ((0, 0), (ROPE_LO, HEAD_PAD - ROPE_LO - QK_ROPE)))
    w_lat = jnp.concatenate([w_in[:, :o_rope], w_rope], axis=1).astype(BF16)
    w_u = w_in[:, o_u:o_gm].astype(BF16)
    w_gm = w_in[:, o_gm:o_gs].astype(BF16)
    w_gs = w_in[:, o_gs:].astype(BF16)
    wuq = _pad_heads(w_uq, QK_NOPE + QK_ROPE, QK_NOPE + QK_ROPE).astype(BF16)
    kv3 = w_ukv.reshape(KV_LORA, MLA_HEADS, QK_NOPE + V_HEAD)
    wuk = _pad_heads(kv3[:, :, :QK_NOPE].reshape(KV_LORA, -1), QK_NOPE, QK_NOPE).astype(BF16)
    wuv = _pad_heads(kv3[:, :, QK_NOPE:].reshape(KV_LORA, -1), V_HEAD, V_HEAD).astype(BF16)
    HP = MLA_HEADS * HEAD_PAD
    tile = lambda w: pl.BlockSpec((TM, w), lambda i: (i, 0))
    return pl.pallas_call(
        _inproj_kernel,
        grid=(T // TM,),
        in_specs=[tile(D_MODEL), tile(QK_ROPE), _full((QK_ROPE, 3 * HEAD_PAD)), _full((1, HEAD_PAD)),
                  _full(w_lat.shape), _full(w_u.shape), _full(w_gm.shape), _full(w_gs.shape),
                  _full((1, Q_LORA)), _full((1, KV_LORA)),
                  _full(wuq.shape), _full(wuk.shape), _full(wuv.shape)],
        out_specs=[tile(HP), tile(HP), tile(HP), tile(S5_WIDTH), tile(D_MODEL), tile(D_MODEL)],
        out_shape=[jax.ShapeDtypeStruct((T, HP), BF16)] * 3
        + [jax.ShapeDtypeStruct((T, S5_WIDTH), F32)]
        + [jax.ShapeDtypeStruct((T, D_MODEL), BF16)] * 2,
        compiler_params=_cparams("parallel"),
        name="inproj_mla_prep",
    )(x2d, *tabs, w_lat, w_u, w_gm, w_gs, q_norm_g.reshape(1, -1), kv_norm_g.reshape(1, -1),
      wuq, wuk, wuv)


def _attn_kernel(q_ref, k_ref, v_ref, o_ref):
    qi = pl.program_id(2)
    q = q_ref[0]

    def update(qq, k, v, carry, row_offset=None):
        m, acc = carry
        s = _dot_nt(qq, k)
        if row_offset is not None:
            row = lax.broadcasted_iota(I32, s.shape, 0) + row_offset
            col = lax.broadcasted_iota(I32, s.shape, 1)
            s = jnp.where(col <= row, s, -jnp.inf)
        m_new = jnp.maximum(m, jnp.max(s, axis=-1, keepdims=True))
        p = jnp.exp2(s - m_new)
        acc = jnp.exp2(m - m_new) * acc + _dot(p.astype(BF16), v)
        return m_new, acc

    def step(blk, n_blk, carry):
        tk = n_blk * TQ
        start = pl.multiple_of(blk * TQ, TQ)
        return update(q, k_ref[0, pl.ds(start, tk), :], v_ref[0, pl.ds(start, tk), :], carry)

    carry = (jnp.full((TQ, 1), -jnp.inf, F32), jnp.zeros((TQ, HEAD_PAD), F32))
    done = 0
    for n_blk in KV_GROUPS:
        base = done
        n_it = (qi - base) // n_blk
        carry = lax.fori_loop(0, n_it, lambda j, c, base=base, n_blk=n_blk: step(base + j * n_blk, n_blk, c), carry)
        done = base + n_it * n_blk
    hq = TQ // 2
    d0 = pl.multiple_of(qi * TQ, TQ)
    m, acc = carry
    _, acc_lo = update(q[:hq], k_ref[0, pl.ds(d0, hq), :], v_ref[0, pl.ds(d0, hq), :], (m[:hq], acc[:hq]), 0)
    _, acc_hi = update(q[hq:], k_ref[0, pl.ds(d0, TQ), :], v_ref[0, pl.ds(d0, TQ), :], (m[hq:], acc[hq:]), hq)
    o_ref[0, :hq, :] = (acc_lo / acc_lo[:, V_HEAD:V_HEAD + 1]).astype(BF16)
    o_ref[0, hq:, :] = (acc_hi / acc_hi[:, V_HEAD:V_HEAD + 1]).astype(BF16)


def _attention(q, k, v, B, S):
    HP = MLA_HEADS * HEAD_PAD
    q, k, v = (a.reshape(B, S, HP) for a in (q, k, v))
    o = pl.pallas_call(
        _attn_kernel,
        grid=(B, MLA_HEADS, S // TQ),
        in_specs=[pl.BlockSpec((1, TQ, HEAD_PAD), lambda b, h, i: (b, i, h)),
                  pl.BlockSpec((1, S, HEAD_PAD), lambda b, h, i: (b, 0, h)),
                  pl.BlockSpec((1, S, HEAD_PAD), lambda b, h, i: (b, 0, h))],
        out_specs=pl.BlockSpec((1, TQ, HEAD_PAD), lambda b, h, i: (b, i, h)),
        out_shape=jax.ShapeDtypeStruct((B, S, HP), BF16),
        compiler_params=_cparams("parallel", "parallel", "arbitrary"),
        name="mla_flash_attention",
    )(q, k, v)
    return o.reshape(B * S, HP)


CH2 = S5_CHUNK * S5_GROUP_CH
P2 = 2 * S5_STATE
SCAN_LEVELS_MAX = 16


S5_PREP_GROUPS = 4


def _s5_prep_kernel(n_levels, *refs):
    *io_refs, mt_acc = refs
    for k in range(S5_PREP_GROUPS):
        _s5_prep_group(n_levels, *(r.at[pl.ds(k, 1)] for r in io_refs), mt_acc)


def _s5_prep_group(n_levels, arow_ref, acol_ref, bt_ref, ct_ref, d_ref,
                   mt_ref, wt_ref, vt_ref, apow_ref, dvec_ref, mt_acc):
    P, H, L = S5_STATE, S5_GROUP_CH, S5_CHUNK

    def powers(ar, ai, dt, n):
        e = jnp.exp(n * (ar * dt))
        ang = n * (ai * dt)
        return e * jnp.cos(ang), e * jnp.sin(ang)

    def zoh_coef(ar, ai, dt):
        pr, pi = powers(ar, ai, dt, 1.0)
        nr, ni = pr - 1.0, pi
        den = ar * ar + ai * ai
        return (nr * ar + ni * ai) / den, (ni * ar - nr * ai) / den

    row = arow_ref[0]
    ar, ai, dt = row[0:1], row[1:2], jnp.exp(row[2:3])
    cr, ci = zoh_coef(ar, ai, dt)
    bt = bt_ref[0]
    btr = jnp.concatenate([bt[:H], bt[:H]], axis=1)
    bti = jnp.concatenate([bt[H:], bt[H:]], axis=1)
    bbr = cr * btr - ci * bti
    bbi = cr * bti + ci * btr
    lane = lax.broadcasted_iota(I32, (1, P2), 1)
    is_re = lane < P
    ridx = lax.broadcasted_iota(I32, (CH2, P2), 0) // H
    pwr, pwi = powers(ar, ai, dt, (L - 1 - ridx).astype(F32))
    rsel = (lax.broadcasted_iota(I32, (CH2, H), 0) % H == lax.broadcasted_iota(I32, (CH2, H), 1)).astype(F32)
    bbr_t = _dot_exact(rsel, bbr)
    bbi_t = _dot_exact(rsel, bbi)
    w_re = pwr * bbr_t - pwi * bbi_t
    w_im = pwr * bbi_t + pwi * bbr_t
    wt_ref[0] = jnp.where(is_re, w_re, w_im).astype(BF16)
    a_r, a_i = powers(ar, ai, dt, float(L))
    sign = jnp.where(is_re, -1.0, 1.0)
    for lvl in range(SCAN_LEVELS_MAX):
        if lvl < n_levels:
            apow_ref[0, lvl:lvl + 1, :] = a_r
            apow_ref[0, SCAN_LEVELS_MAX + lvl:SCAN_LEVELS_MAX + lvl + 1, :] = sign * a_i
            a_r, a_i = a_r * a_r - a_i * a_i, 2.0 * a_r * a_i
        else:
            apow_ref[0, lvl:lvl + 1, :] = jnp.zeros_like(a_r)
            apow_ref[0, SCAN_LEVELS_MAX + lvl:SCAN_LEVELS_MAX + lvl + 1, :] = jnp.zeros_like(a_r)

    col = acol_ref[0]
    arc, aic, dtc = col[:, 0:1], col[:, 1:2], jnp.exp(col[:, 2:3])
    ct = ct_ref[0]
    csel = (lax.broadcasted_iota(I32, (H, CH2), 1) % H == lax.broadcasted_iota(I32, (H, CH2), 0)).astype(F32)
    ctr = _dot_exact(ct[:, :H], csel)
    cti = _dot_exact(ct[:, H:], csel)
    sidx = (lax.broadcasted_iota(I32, (P, CH2), 1) // H).astype(F32)
    pr, pi = powers(arc, aic, dtc, sidx)
    g_r = ctr * pr - cti * pi
    g_i = ctr * pi + cti * pr
    a1r, a1i = powers(arc, aic, dtc, 1.0)
    v_r = g_r * a1r - g_i * a1i
    v_i = g_r * a1i + g_i * a1r
    vt_ref[0, :P, :] = v_r.astype(BF16)
    vt_ref[0, P:, :] = (-v_i).astype(BF16)
    bbr64, bbi64 = bbr[:, :P], bbi[:, :P]
    kt = _dot_exact(bbr64, g_r) - _dot_exact(bbi64, g_i)
    lane2 = lax.broadcasted_iota(I32, (H, CH2), 1)
    for r in range(L):
        blk = kt if r == 0 else jnp.where(lane2 >= r * H, pltpu.roll(kt, r * H, axis=1), 0.0)
        mt_acc[r * H:(r + 1) * H, :] = blk
    mt_ref[0] = mt_acc[...].astype(BF16)
    dvec_ref[0] = _dot_exact(d_ref[0], csel)


def _s5_prep(n_levels, s5_a_re, s5_a_im, s5_log_dt, s5_b_re, s5_b_im, s5_c_re, s5_c_im, s5_d):
    G, P, H = S5_GROUPS, S5_STATE, S5_GROUP_CH
    ldt = jnp.broadcast_to(s5_log_dt[:, None], (G, P))
    arow = jnp.stack([s5_a_re, s5_a_im, ldt], axis=1)
    arow = jnp.concatenate([arow, arow], axis=2)
    acol = jnp.stack([s5_a_re, s5_a_im, ldt], axis=2)
    bt = jnp.concatenate([s5_b_re.transpose(0, 2, 1), s5_b_im.transpose(0, 2, 1)], axis=1)
    ct = jnp.concatenate([s5_c_re.transpose(0, 2, 1), s5_c_im.transpose(0, 2, 1)], axis=2)
    d = s5_d.reshape(G, 1, H)
    blk = lambda *s: pl.BlockSpec((S5_PREP_GROUPS,) + s, lambda g: (g, 0, 0))
    return pl.pallas_call(
        functools.partial(_s5_prep_kernel, n_levels),
        grid=(G // S5_PREP_GROUPS,),
        in_specs=[blk(3, P2), blk(P, 3), blk(2 * H, P), blk(P, 2 * H), blk(1, H)],
        out_specs=[blk(CH2, CH2), blk(CH2, P2), blk(P2, CH2), blk(2 * SCAN_LEVELS_MAX, P2), blk(1, CH2)],
        out_shape=[jax.ShapeDtypeStruct((G, CH2, CH2), BF16),
                   jax.ShapeDtypeStruct((G, CH2, P2), BF16),
                   jax.ShapeDtypeStruct((G, P2, CH2), BF16),
                   jax.ShapeDtypeStruct((G, 2 * SCAN_LEVELS_MAX, P2), F32),
                   jax.ShapeDtypeStruct((G, 1, CH2), F32)],
        scratch_shapes=[pltpu.VMEM((CH2, CH2), F32)],
        compiler_params=_cparams("parallel"),
        name="s5_discretise",
    )(arow, acol, bt, ct, d)


def _gelu_tanh(y):
    return 0.5 * y * (1.0 + jnp.tanh(math.sqrt(2.0 / math.pi) * (y + 0.044715 * (y * y * y))))


GROUPS_PER_TILE = LANES // S5_GROUP_CH
STEPS_PER_TILE = LANES // S5_GROUP_CH


def _s5_scan_kernel(n_levels, u_ref, mt_ref, wt_ref, vt_ref, apow_ref, dvec_ref, y_ref):
    H, L = S5_GROUP_CH, S5_CHUNK
    nc = u_ref.shape[0] // L
    lane_blk = lax.broadcasted_iota(I32, (nc, LANES), 1) // H
    ridx = lax.broadcasted_iota(I32, (nc, P2), 0)
    steps = [u_ref[pl.ds(l, nc, stride=L), :] for l in range(L)]
    groups = range(GROUPS_PER_TILE)
    us, xs = [], []
    for gi in groups:
        halves = []
        for hh in range(L // STEPS_PER_TILE):
            acc = jnp.zeros((nc, LANES), F32)
            for l8 in range(STEPS_PER_TILE):
                src = steps[hh * STEPS_PER_TILE + l8]
                shift = ((l8 - gi) % STEPS_PER_TILE) * H
                moved = src if shift == 0 else pltpu.roll(src, shift, axis=1)
                acc = jnp.where(lane_blk == l8, moved, acc)
            halves.append(acc)
        u = jnp.concatenate(halves, axis=1)
        us.append(u)
        xs.append(_dot(u.astype(BF16), wt_ref[gi]))
    for lvl in range(n_levels):
        sh = 1 << lvl
        for gi in groups:
            apow = apow_ref[gi]
            prev = jnp.where(ridx >= sh, pltpu.roll(xs[gi], sh, axis=0), 0.0)
            a_r = apow[lvl:lvl + 1]
            a_i = apow[SCAN_LEVELS_MAX + lvl:SCAN_LEVELS_MAX + lvl + 1]
            xs[gi] = xs[gi] + a_r * prev + a_i * pltpu.roll(prev, S5_STATE, axis=1)
    outs = [jnp.zeros((nc, LANES), F32) for _ in range(L)]
    for gi in groups:
        x_in = jnp.where(ridx >= 1, pltpu.roll(xs[gi], 1, axis=0), 0.0)
        u = us[gi]
        y = _gelu_tanh(_dot(u.astype(BF16), mt_ref[gi]) + _dot(x_in.astype(BF16), vt_ref[gi]) + dvec_ref[gi] * u)
        for l in range(L):
            hh, l8 = divmod(l, STEPS_PER_TILE)
            src = y[:, hh * LANES:(hh + 1) * LANES]
            shift = ((gi - l8) % STEPS_PER_TILE) * H
            moved = src if shift == 0 else pltpu.roll(src, shift, axis=1)
            outs[l] = jnp.where(lane_blk == gi, moved, outs[l])
    for l in range(L):
        y_ref[pl.ds(l, nc, stride=L), :] = outs[l]


def _s5_branch(u, B, S, s5_params):
    G, L = S5_GROUPS, S5_CHUNK
    nc = S // L
    n_levels = max(1, (nc - 1).bit_length())
    assert n_levels <= SCAN_LEVELS_MAX and S5_CHUNK % STEPS_PER_TILE == 0
    mt, wt, vt, apow, dvec = _s5_prep(n_levels, *s5_params)
    n_lt = G // GROUPS_PER_TILE
    blk = lambda *s: pl.BlockSpec((GROUPS_PER_TILE,) + s, lambda b, t: (t, 0, 0))
    io = pl.BlockSpec((S, LANES), lambda b, t: (b, t))
    return pl.pallas_call(
        functools.partial(_s5_scan_kernel, n_levels),
        grid=(B, n_lt),
        in_specs=[io, blk(CH2, CH2), blk(CH2, P2), blk(P2, CH2), blk(2 * SCAN_LEVELS_MAX, P2), blk(1, CH2)],
        out_specs=io,
        out_shape=jax.ShapeDtypeStruct((B * S, S5_WIDTH), F32),
        compiler_params=_cparams("parallel", "parallel"),
        name="s5_chunk_scan",
    )(u, mt, wt, vt, apow, dvec)


def _merge_xattn_kernel(x_ref, o_ref, ys_ref, gm_ref, gs_ref, wmo_ref, wglu_ref, wout_ref, g1_ref, b1_ref,
                        k_ref, v_ref, wq_ref, wo_ref, g2_ref, b2_ref, x2_ref):
    o32 = pltpu.bitcast(o_ref[0], I32)
    first_half = lax.broadcasted_iota(I32, (o32.shape[0], HEAD_PAD), 1) < V_HEAD
    pairs = [jnp.where(first_half, o32[:, (2 * p) * HEAD_PAD:(2 * p + 1) * HEAD_PAD],
                       pltpu.roll(o32[:, (2 * p + 1) * HEAD_PAD:(2 * p + 2) * HEAD_PAD], V_HEAD, axis=1))
             for p in range(MLA_HEADS // 2)]
    o_packed = pltpu.bitcast(jnp.concatenate(pairs, axis=1), BF16)
    y_mla = _dot(o_packed, wmo_ref[...])
    z = _dot(ys_ref[0].astype(BF16), wglu_ref[...])
    y_s5 = z[:, :D_MODEL] * jax.nn.sigmoid(z[:, D_MODEL:])
    merged = (jax.nn.sigmoid(gm_ref[0].astype(F32)) * y_mla
              + jax.nn.sigmoid(gs_ref[0].astype(F32)) * y_s5)
    mix = _dot(merged.astype(BF16), wout_ref[...])
    x1 = _layer_norm(DEEPNORM_ALPHA * x_ref[0] + mix, g1_ref[...], b1_ref[...])
    q = (_dot(x1.astype(BF16), wq_ref[...]) * (XATTN_HEAD_DIM ** -0.5)).astype(BF16)
    k = k_ref[0]
    v = v_ref[0]
    outs = []
    for h in range(XATTN_HEADS):
        sl = slice(h * XATTN_HEAD_DIM, (h + 1) * XATTN_HEAD_DIM)
        s = _dot_nt(q[:, sl], k[:, sl])
        p = jnp.exp(s - jnp.max(s, axis=-1, keepdims=True))
        o = _dot(p.astype(BF16), v[:, sl]) / jnp.sum(p, axis=-1, keepdims=True)
        outs.append(o.astype(BF16))
    xa = _dot(jnp.concatenate(outs, axis=1), wo_ref[...])
    x2_ref[0] = _layer_norm(DEEPNORM_ALPHA * x1 + xa, g2_ref[...], b2_ref[...])


def _memkv_kernel(mem_ref, g_ref, b_ref, w_ref, k_ref, v_ref):
    m = _layer_norm(mem_ref[0], g_ref[...], b_ref[...]).astype(BF16)
    kv = _dot(m, w_ref[...])
    hd = XATTN_HEADS * XATTN_HEAD_DIM
    k_ref[0] = kv[:, :hd].astype(BF16)
    v_ref[0] = kv[:, hd:].astype(BF16)


def _merge_cross_attention(x2d, o, ys, gm, gs, mem, B, S, w_mla_o, w_s5_glu, w_out, ln1_g, ln1_b,
                           mem_ln_g, mem_ln_b, w_xq, w_xkv, w_xo, ln2_g, ln2_b):
    M = mem.shape[1]
    hd = XATTN_HEADS * XATTN_HEAD_DIM
    HP = MLA_HEADS * HEAD_PAD
    row = lambda a: a.reshape(1, -1)
    wmo = w_mla_o.astype(BF16)
    k, v = pl.pallas_call(
        _memkv_kernel,
        grid=(B,),
        in_specs=[pl.BlockSpec((1, M, D_MODEL), lambda b: (b, 0, 0)), _full((1, D_MODEL)), _full((1, D_MODEL)),
                  _full((D_MODEL, 2 * hd))],
        out_specs=[pl.BlockSpec((1, M, hd), lambda b: (b, 0, 0))] * 2,
        out_shape=[jax.ShapeDtypeStruct((B, M, hd), BF16)] * 2,
        compiler_params=_cparams("parallel"),
        name="memory_kv",
    )(mem, row(mem_ln_g), row(mem_ln_b), w_xkv.astype(BF16))
    tile = lambda w: pl.BlockSpec((1, TM, w), lambda b, i: (b, i, 0))
    kv_spec = pl.BlockSpec((1, M, hd), lambda b, i: (b, 0, 0))
    r3 = lambda a: a.reshape(B, S, a.shape[-1])
    x2 = pl.pallas_call(
        _merge_xattn_kernel,
        grid=(B, S // TM),
        in_specs=[tile(D_MODEL), tile(HP), tile(S5_WIDTH), tile(D_MODEL), tile(D_MODEL),
                  _full((MLA_HEADS * V_HEAD, D_MODEL)), _full((S5_WIDTH, 2 * D_MODEL)), _full((D_MODEL, D_MODEL)),
                  _full((1, D_MODEL)), _full((1, D_MODEL)),
                  kv_spec, kv_spec, _full((D_MODEL, hd)), _full((hd, D_MODEL)), _full((1, D_MODEL)), _full((1, D_MODEL))],
        out_specs=tile(D_MODEL),
        out_shape=jax.ShapeDtypeStruct((B, S, D_MODEL), F32),
        compiler_params=_cparams("parallel", "parallel"),
        name="merge_ln1_xattn_ln2",
    )(r3(x2d), r3(o), r3(ys), r3(gm), r3(gs), wmo, w_s5_glu.astype(BF16), w_out.astype(BF16), row(ln1_g), row(ln1_b),
      k, v, w_xq.astype(BF16), w_xo.astype(BF16), row(ln2_g), row(ln2_b))
    return x2.reshape(B * S, D_MODEL)


def _route_sort_kernel(x_ref, w_ref, bias_ref, xs_ref, pos_ref, cnt_ref, off_ref):
    E, per = N_EXPERTS, N_EXPERTS // N_EXPERT_GROUPS
    tm = x_ref.shape[0]
    x = x_ref[...]
    xb = x.astype(BF16)
    x_lo = (x - xb.astype(F32)).astype(BF16)
    w = w_ref[...]
    wb = w.astype(BF16)
    w_lo = (w - wb.astype(F32)).astype(BF16)
    logits = _dot_nt(wb, xb) + (_dot_nt(wb, x_lo) + _dot_nt(w_lo, xb))
    scores = jax.nn.sigmoid(logits)
    sel = scores + bias_ref[...]
    neg = -jnp.inf
    i8 = lax.broadcasted_iota(I32, (per, tm), 0)
    gscore = []
    for g in range(N_EXPERT_GROUPS):
        blk = sel[g * per:(g + 1) * per]
        m1 = jnp.max(blk, axis=0, keepdims=True)
        i1 = jnp.min(jnp.where(blk == m1, i8, per), axis=0, keepdims=True)
        m2 = jnp.max(jnp.where(i8 == i1, neg, blk), axis=0, keepdims=True)
        gscore.append(m1 + m2)
    blocks = []
    for g in range(N_EXPERT_GROUPS):
        ahead = jnp.zeros((1, tm), I32)
        for o in range(N_EXPERT_GROUPS):
            if o == g:
                continue
            before = (gscore[o] >= gscore[g]) if o < g else (gscore[o] > gscore[g])
            ahead = ahead + before.astype(I32)
        blocks.append(jnp.where(ahead < TOPK_GROUPS, sel[g * per:(g + 1) * per], neg))
    cur = jnp.concatenate(blocks, axis=0)
    ie = lax.broadcasted_iota(I32, (E, tm), 0)
    picked = jnp.zeros((E, tm), F32)
    for _ in range(TOP_K):
        m = jnp.max(cur, axis=0, keepdims=True)
        idx = jnp.min(jnp.where(cur == m, ie, E), axis=0, keepdims=True)
        hit = ie == idx
        picked = jnp.where(hit, 1.0, picked)
        cur = jnp.where(hit, neg, cur)
    wsel = scores * picked
    wnorm = wsel / jnp.sum(wsel, axis=0, keepdims=True) * ROUTED_SCALE
    pb = picked.astype(BF16)
    tri = (lax.broadcasted_iota(I32, (tm, tm), 0) <= lax.broadcasted_iota(I32, (tm, tm), 1)).astype(BF16)
    incl = _dot(pb, tri)
    cnt_col = jnp.sum(picked, axis=1, keepdims=True)
    lower = (lax.broadcasted_iota(I32, (E, E), 1) < lax.broadcasted_iota(I32, (E, E), 0)).astype(F32)
    al_col = jnp.floor((cnt_col + (SEG_ALIGN - 1.0)) * (1.0 / SEG_ALIGN)) * SEG_ALIGN
    off_col = _dot_exact(lower, jnp.broadcast_to(al_col, (E, LANES)))[:, 0:1]
    pos = off_col + incl - 1.0
    q = jnp.floor(pos * (1.0 / POS_RADIX))
    pq = jnp.concatenate([jnp.where(picked > 0.0, POS_RADIX * q, POS_RADIX * POS_NONE_Q),
                          jnp.where(picked > 0.0, pos - POS_RADIX * q, 0.0)], axis=0)
    pos_ref[...] = pq
    pqb = pq.astype(BF16)
    cnt_ref[0] = jnp.broadcast_to(cnt_col, (E, LANES)).astype(I32)
    off_ref[0] = jnp.broadcast_to(off_col, (E, LANES)).astype(I32)
    cnt_row = _dot_nt(jnp.ones((8, tm), BF16), pb)
    al_row = jnp.floor((cnt_row + (SEG_ALIGN - 1.0)) * (1.0 / SEG_ALIGN)) * SEG_ALIGN
    er = lax.broadcasted_iota(I32, (E, 4 * E), 0)
    ec = lax.broadcasted_iota(I32, (E, 4 * E), 1) % E
    off_row4 = _dot_exact(al_row, (er < ec).astype(F32))[0:1]
    end_row4 = off_row4 + _dot_exact(al_row, (er == ec).astype(F32))[0:1]
    w_hi = wnorm.astype(BF16).astype(F32)
    w_r1 = wnorm - w_hi
    w_mid = w_r1.astype(BF16).astype(F32)
    w_lo = w_r1 - w_mid
    w4_t = jnp.concatenate([w_hi, w_mid, w_lo, jnp.zeros_like(w_lo)], axis=0).T
    xw = jnp.concatenate([xb, w4_t.astype(BF16)], axis=1)
    half = D_MODEL // 2

    def build(c, carry):
        r0 = pl.multiple_of(c * OH_ROWS, OH_ROWS)
        jr = (lax.broadcasted_iota(I32, (OH_ROWS, 4 * E), 0) + r0).astype(F32)
        member4 = jnp.where(jr >= off_row4, jnp.where(jr < end_row4, 1.0, 0.0), 0.0)
        target = _dot(member4[:, :2 * E].astype(BF16), pqb)
        j = (lax.broadcasted_iota(I32, (OH_ROWS, tm), 0) + r0).astype(F32)
        ohb = jnp.where(target == j, 1.0, 0.0).astype(BF16)
        rows = _dot(ohb, xw)
        xs_ref[0, pl.ds(r0, OH_ROWS), 0:half] = _pack_bf16_pair(rows[:, :half], rows[:, half:D_MODEL])
        wrow = jnp.sum(member4 * rows[:, D_MODEL:], axis=1, keepdims=True)
        xs_ref[0, pl.ds(r0, OH_ROWS), half:] = pltpu.bitcast(jnp.broadcast_to(wrow, (OH_ROWS, LANES)), I32)
        return carry

    lax.fori_loop(0, SORT_ROWS // OH_ROWS, build, 0)
    xs_ref[0, SORT_ROWS:, :] = jnp.zeros((PAD_ROWS, XW), I32)


def _route_sort(x2, w_router, router_bias):
    T = x2.shape[0]
    E = N_EXPERTS
    nt = T // TW
    seg = pl.BlockSpec((1, E, LANES), lambda i: (i, 0, 0))
    return pl.pallas_call(
        _route_sort_kernel,
        grid=(nt,),
        in_specs=[pl.BlockSpec((TW, D_MODEL), lambda i: (i, 0)), _full((E, D_MODEL)), _full((E, 1))],
        out_specs=[pl.BlockSpec((1, TS, XW), lambda i: (i, 0, 0)),
                   pl.BlockSpec((2 * E, TW), lambda i: (0, i)), seg, seg],
        out_shape=[jax.ShapeDtypeStruct((nt, TS, XW), I32), jax.ShapeDtypeStruct((2 * E, T), F32),
                   jax.ShapeDtypeStruct((nt, E, LANES), I32), jax.ShapeDtypeStruct((nt, E, LANES), I32)],
        compiler_params=_cparams("parallel"),
        name="route_local_sort",
    )(x2, w_router.T, router_bias.reshape(E, 1))


def _plan_kernel(cnt_ref, off_ref, pa_ref, ce_ref, gn_ref, nch_ref):
    E, nt = cnt_ref.shape
    n_rows = pa_ref.shape[0]
    cnt = cnt_ref[...].astype(F32)
    off = off_ref[...].astype(F32)
    npc = jnp.floor((cnt + (PR - 1.0)) * (1.0 / PR))
    tri = (lax.broadcasted_iota(I32, (nt, nt), 0) <= lax.broadcasted_iota(I32, (nt, nt), 1)).astype(F32)
    p_incl = _dot_exact(npc, tri)
    total = p_incl[:, nt - 1:nt]
    n_ch = jnp.floor((total + (PPC - 1.0)) * (1.0 / PPC))
    lower = (lax.broadcasted_iota(I32, (E, E), 1) < lax.broadcasted_iota(I32, (E, E), 0)).astype(F32)
    wide = lambda col: jnp.broadcast_to(col, (E, LANES))
    g0 = _dot_exact(lower, wide(n_ch))[:, 0:1]
    n_total = jnp.sum(n_ch, axis=0, keepdims=True)
    g0_row = wide(g0).T[0:1, :E]
    g1_row = g0_row + wide(n_ch).T[0:1, :E]
    g_e = lax.broadcasted_iota(I32, (n_rows, E), 0).astype(F32)
    sel = jnp.where(g_e >= g0_row, jnp.where(g_e < g1_row, 1.0, 0.0), 0.0)
    tile_base = lax.broadcasted_iota(I32, (E, nt), 1).astype(F32) * TS
    a_mat = tile_base + off - PR * (p_incl - npc)
    p_g = _dot_exact(sel, p_incl)
    a_g = _dot_exact(sel, a_mat)
    e_idx = lax.broadcasted_iota(I32, (E, LANES), 0).astype(F32)
    pick = lambda col: _dot_exact(sel, col)[:, 0:1]
    g0_g, tot_g, e_g = pick(wide(g0)), pick(wide(total)), pick(e_idx)
    gn_g = pick(wide(g0 + n_ch))
    g_p = lax.broadcasted_iota(I32, (n_rows, PPC), 0).astype(F32)
    q = (g_p - g0_g) * PPC + lax.broadcasted_iota(I32, (n_rows, PPC), 1).astype(F32)
    addr = jnp.full((n_rows, PPC), -float(SEG_ALIGN), F32)
    lo = jnp.zeros((n_rows, 1), F32)
    for i in range(nt):
        hi = p_g[:, i:i + 1]
        addr = jnp.where(q >= lo, jnp.where(q < hi, a_g[:, i:i + 1] + PR * q, addr), addr)
        lo = hi
    pa_ref[...] = addr.astype(I32)
    g_col = lax.broadcasted_iota(I32, (n_rows, LANES), 0).astype(F32)
    ce_ref[...] = jnp.where(g_col < n_total, jnp.broadcast_to(e_g, (n_rows, LANES)), E - 1.0).astype(I32)
    gn_ref[...] = jnp.broadcast_to(gn_g, (n_rows, LANES)).astype(I32)
    nch_ref[...] = jnp.broadcast_to(n_total, nch_ref.shape).astype(I32)


def _plan(cnt, off, n_chunks_max):
    n_rows = (n_chunks_max + 7) // 8 * 8
    pa, ce, gn, nch = pl.pallas_call(
        _plan_kernel,
        out_shape=[jax.ShapeDtypeStruct((n_rows, PPC), I32), jax.ShapeDtypeStruct((n_rows, LANES), I32),
                   jax.ShapeDtypeStruct((n_rows, LANES), I32), jax.ShapeDtypeStruct((8, LANES), I32)],
        name="moe_plan",
    )(cnt, off)
    return pa.reshape(n_rows * PPC), ce[:, 0], gn[:, 0], nch[0, :1]


def _unpack_bf16_pair(w):
    lo = pltpu.bitcast(lax.shift_left(w, jnp.int32(16)), F32).astype(BF16)
    hi = pltpu.bitcast(w & jnp.int32(-65536), F32).astype(BF16)
    return lo, hi


def _pack_bf16_pair(lo, hi):
    lo_bits = lax.shift_right_logical(pltpu.bitcast(lo.astype(BF16).astype(F32), I32), jnp.int32(16))
    hi_bits = pltpu.bitcast(hi.astype(BF16).astype(F32), I32)
    return hi_bits | lo_bits


def _expert_kernel(ce_ref, gn_ref, pa_ref, nch_ref, tot_ref, xs_hbm, wgu_hbm, wdn_hbm, y_hbm,
                   xbuf, ybuf, zbuf, wgu_f, wdn_f, wgu_b, wdn_b, wslot, gsem, wsem, zsem, esem):
    g = pl.program_id(0)
    n = nch_ref[0]
    slot = g % N_SLOTS
    half = D_MODEL // 2

    dummy_base = (tot_ref.shape[0] - 1) * TS

    def gather_copy(s, a, jp):
        return pltpu.make_async_copy(xs_hbm.at[pl.ds(pl.multiple_of(a, SEG_ALIGN), PR), :],
                                     xbuf.at[s, pl.ds(jp * PR, PR), :], gsem.at[s])

    def write_copy(s, a, jp):
        return pltpu.make_async_copy(ybuf.at[s, pl.ds(jp * PR, PR), :],
                                     y_hbm.at[pl.ds(pl.multiple_of(a, SEG_ALIGN), PR), :], wsem.at[s])

    def gather_start(c, s):
        for jp in range(PPC):
            gather_copy(s, jnp.maximum(pa_ref[c * PPC + jp], 0), jp).start()

    def write_start(c, s):
        for jp in range(PPC):
            a = pa_ref[c * PPC + jp]
            write_copy(s, jnp.where(a >= 0, a, dummy_base + jp * PR), jp).start()

    def gather_wait(s):
        for jp in range(PPC):
            gather_copy(s, 0, jp).wait()

    def write_wait(s):
        for jp in range(PPC):
            write_copy(s, 0, jp).wait()

    def weight_copies(e, s):
        return (pltpu.make_async_copy(wgu_hbm.at[e], wgu_f.at[s], esem.at[s]),
                pltpu.make_async_copy(wdn_hbm.at[e], wdn_f.at[s], esem.at[s]))

    @pl.when(g == 0)
    def _():
        wslot[0] = 0
        for cp in weight_copies(ce_ref[0], 0):
            cp.start()
        gather_start(0, 0)
        ybuf[...] = jnp.zeros_like(ybuf)
        zbuf[...] = jnp.zeros_like(zbuf)

        def zero_copy(row):
            return pltpu.make_async_copy(zbuf, y_hbm.at[pl.ds(pl.multiple_of(row, SEG_ALIGN), SEG_ALIGN), :], zsem)

        def tail_of(i):
            return i * TS + tot_ref[i], (TS - tot_ref[i]) // SEG_ALIGN

        def zero_start(i, c):
            base, cnt8 = tail_of(i)
            lax.fori_loop(0, cnt8, lambda q, z: (zero_copy(base + q * SEG_ALIGN).start(), z)[1], 0)
            return c

        def zero_wait(i, c):
            base, cnt8 = tail_of(i)
            lax.fori_loop(0, cnt8, lambda q, z: (zero_copy(base + q * SEG_ALIGN).wait(), z)[1], 0)
            return c

        lax.fori_loop(0, tot_ref.shape[0], zero_start, 0)
        lax.fori_loop(0, tot_ref.shape[0], zero_wait, 0)

    for c in range(1, GATHER_AHEAD):
        @pl.when((g == 0) & (c < n))
        def _(c=c):
            gather_start(c, c)

    @pl.when(g + GATHER_AHEAD < n)
    def _():
        gather_start(g + GATHER_AHEAD, (g + GATHER_AHEAD) % N_SLOTS)

    e_cur = ce_ref[g]
    e_prev = ce_ref[jnp.maximum(g - 1, 0)]
    e_prev2 = ce_ref[jnp.maximum(g - 2, 0)]

    @pl.when(g < n)
    def _():
        @pl.when((g == 0) | (e_cur != e_prev))
        def _():
            s = wslot[0]
            for cp in weight_copies(e_cur, s):
                cp.wait()
            wgu_b[...] = wgu_f[s].astype(BF16)
            wdn_b[...] = wdn_f[s].astype(BF16)
            g_next = gn_ref[g]

            @pl.when(g_next < n)
            def _():
                for cp in weight_copies(ce_ref[g_next], 1 - s):
                    cp.start()

            wslot[0] = 1 - s

        gather_wait(slot)

        @pl.when((g >= 2) & (e_prev == e_prev2))
        def _():
            write_wait((g - 2) % N_SLOTS)

        @pl.when((g >= 1) & (e_cur != e_prev))
        def _():
            write_wait((g - 1) % N_SLOTS)

        def swiglu_rows(rows):
            xw = xbuf[slot, :rows, :]
            lo, hi = _unpack_bf16_pair(xw[:, :half])
            h = _dot(jnp.concatenate([lo, hi], axis=1), wgu_b[...])
            gate, up = h[:, :EXPERT_FF], h[:, EXPERT_FF:]
            act = (gate * jax.nn.sigmoid(gate) * up).astype(BF16)
            y = _dot(act, wdn_b[...])
            w_row = pltpu.bitcast(xw[:, half:], F32)
            y = y * jnp.concatenate([w_row] * (D_MODEL // LANES), axis=1)
            ybuf[slot, :rows, :] = _pack_bf16_pair(y[:, :half], y[:, half:])

        second_half_used = pa_ref[g * PPC + PPC // 2] >= 0

        @pl.when(second_half_used)
        def _():
            swiglu_rows(CR)

        @pl.when(jnp.logical_not(second_half_used))
        def _():
            swiglu_rows(CR // 2)

        write_start(g, slot)

    @pl.when(g == pl.num_programs(0) - 1)
    def _():
        @pl.when((n >= 2) & (ce_ref[jnp.maximum(n - 1, 0)] == ce_ref[jnp.maximum(n - 2, 0)]))
        def _():
            write_wait((n - 2) % N_SLOTS)

        write_wait((n - 1) % N_SLOTS)


def _experts(ce, gn, pa, nch, tot, xs, w_exp_gu, w_exp_down):
    n_rows = xs.shape[0]
    half = D_MODEL // 2
    hbm = pl.BlockSpec(memory_space=pl.ANY)
    grid_spec = pltpu.PrefetchScalarGridSpec(
        num_scalar_prefetch=5,
        grid=(ce.shape[0],),
        in_specs=[hbm, hbm, hbm],
        out_specs=hbm,
        scratch_shapes=[pltpu.VMEM((N_SLOTS, CR, XW), I32), pltpu.VMEM((N_SLOTS, CR, half), I32),
                        pltpu.VMEM((SEG_ALIGN, half), I32),
                        pltpu.VMEM((2, D_MODEL, 2 * EXPERT_FF), F32), pltpu.VMEM((2, EXPERT_FF, D_MODEL), F32),
                        pltpu.VMEM((D_MODEL, 2 * EXPERT_FF), BF16), pltpu.VMEM((EXPERT_FF, D_MODEL), BF16),
                        pltpu.SMEM((1,), I32),
                        pltpu.SemaphoreType.DMA((N_SLOTS,)), pltpu.SemaphoreType.DMA((N_SLOTS,)),
                        pltpu.SemaphoreType.DMA(()), pltpu.SemaphoreType.DMA((2,))],
    )
    return pl.pallas_call(
        _expert_kernel,
        grid_spec=grid_spec,
        out_shape=jax.ShapeDtypeStruct((n_rows + TS, half), I32),
        compiler_params=_cparams("arbitrary"),
        name="moe_grouped_swiglu",
    )(ce, gn, pa, nch, tot, xs, w_exp_gu, w_exp_down)


def _combine_kernel(tot_ref, y_ref, pos_ref, cnt_ref, off_ref, x2_ref, wsg_ref, wsd_ref, g_ref, b_ref, out_ref,
                    acc_lo, acc_hi):
    tm = x2_ref.shape[0]
    E = N_EXPERTS
    n_sorted = tot_ref[pl.program_id(0)]
    x2 = x2_ref[...]
    h = _dot(x2.astype(BF16), wsg_ref[...])
    gate, up = h[:, :SHARED_FF], h[:, SHARED_FF:]
    shared = _dot((gate * jax.nn.sigmoid(gate) * up).astype(BF16), wsd_ref[...])
    pq_t = pos_ref[...].T.astype(BF16)
    off_col = off_ref[0][:, 0:1].astype(F32)
    cnt_col = cnt_ref[0][:, 0:1].astype(F32)
    end_col = off_col + jnp.floor((cnt_col + (SEG_ALIGN - 1.0)) * (1.0 / SEG_ALIGN)) * SEG_ALIGN
    off_col2 = jnp.concatenate([off_col, off_col], axis=0)
    end_col2 = jnp.concatenate([end_col, end_col], axis=0)
    acc_lo[...] = jnp.zeros_like(acc_lo)
    acc_hi[...] = jnp.zeros_like(acc_hi)

    def gather_back(c, carry):
        r0 = pl.multiple_of(c * OH_ROWS, OH_ROWS)
        jc = (lax.broadcasted_iota(I32, (2 * E, OH_ROWS), 1) + r0).astype(F32)
        member2 = jnp.where(jc >= off_col2, jnp.where(jc < end_col2, 1.0, 0.0), 0.0)
        target = _dot(pq_t, member2.astype(BF16))
        j = (lax.broadcasted_iota(I32, (tm, OH_ROWS), 1) + r0).astype(F32)
        ohb = jnp.where(target == j, 1.0, 0.0).astype(BF16)
        yw = y_ref[0, pl.ds(r0, OH_ROWS), :]
        row = lax.broadcasted_iota(I32, yw.shape, 0) + r0
        lo, hi = _unpack_bf16_pair(jnp.where(row < n_sorted, yw, 0))
        acc_lo[...] += _dot(ohb, lo)
        acc_hi[...] += _dot(ohb, hi)
        return carry

    lax.fori_loop(0, SORT_ROWS // OH_ROWS, gather_back, 0)
    ff = shared + jnp.concatenate([acc_lo[...], acc_hi[...]], axis=1)
    out_ref[...] = _layer_norm(DEEPNORM_ALPHA * x2 + ff, g_ref[...], b_ref[...])


def _combine(tot, y, pos, cnt, off, x2, w_sh_gu, w_sh_down, ln_g, ln_b):
    T = x2.shape[0]
    nt = T // TW
    half = D_MODEL // 2
    tile = lambda w: pl.BlockSpec((TW, w), lambda i: (i, 0))
    return pl.pallas_call(
        _combine_kernel,
        grid=(nt,),
        in_specs=[pl.BlockSpec(memory_space=pltpu.SMEM),
                  pl.BlockSpec((1, TS, half), lambda i: (i, 0, 0)),
                  pl.BlockSpec((2 * N_EXPERTS, TW), lambda i: (0, i)),
                  pl.BlockSpec((1, N_EXPERTS, LANES), lambda i: (i, 0, 0)),
                  pl.BlockSpec((1, N_EXPERTS, LANES), lambda i: (i, 0, 0)), tile(D_MODEL),
                  _full((D_MODEL, 2 * SHARED_FF)), _full((SHARED_FF, D_MODEL)),
                  _full((1, D_MODEL)), _full((1, D_MODEL))],
        out_specs=tile(D_MODEL),
        out_shape=jax.ShapeDtypeStruct((T, D_MODEL), F32),
        scratch_shapes=[pltpu.VMEM((TW, half), F32), pltpu.VMEM((TW, half), F32)],
        compiler_params=_cparams("parallel"),
        name="moe_combine_shared_ln3",
    )(tot, y.reshape(nt + 1, TS, half), pos, cnt, off, x2, w_sh_gu.astype(BF16), w_sh_down.astype(BF16),
      ln_g.reshape(1, -1), ln_b.reshape(1, -1))


def _moe(x2, w_router, router_bias, w_exp_gu, w_exp_down, w_sh_gu, w_sh_down, ln_g, ln_b):
    T = x2.shape[0]
    E = N_EXPERTS
    nt = T // TW
    xs, pos, cnt, off = _route_sort(x2, w_router, router_bias)
    n_pieces_max = (T * TOP_K + nt * E * (PR - 1)) // PR
    n_chunks_max = n_pieces_max // PPC + E
    pa, ce, gn, nch = _plan(cnt[:, :, 0].T, off[:, :, 0].T, n_chunks_max)
    last_al = (cnt[:, E - 1, 0] + SEG_ALIGN - 1) // SEG_ALIGN * SEG_ALIGN
    tot = jnp.concatenate([off[:, E - 1, 0] + last_al, jnp.zeros((1,), I32)])
    y = _experts(ce, gn, pa, nch, tot, xs.reshape(nt * TS, XW), w_exp_gu, w_exp_down)
    return _combine(tot, y, pos, cnt, off, x2, w_sh_gu, w_sh_down, ln_g, ln_b)


def kernel(x, mem, positions, w_in, q_norm_g, kv_norm_g, w_uq, w_ukv, w_mla_o, s5_a_re, s5_a_im, s5_log_dt, s5_b_re, s5_b_im, s5_c_re, s5_c_im, s5_d, w_s5_glu, w_out, ln1_g, ln1_b, mem_ln_g, mem_ln_b, w_xq, w_xkv, w_xo, ln2_g, ln2_b, w_router, router_bias, w_exp_gu, w_exp_down, w_sh_gu, w_sh_down, ln3_g, ln3_b):
    B, S, D = x.shape
    assert D == D_MODEL and S % TQ == 0 and (B * S) % TM == 0 and S % S5_CHUNK == 0
    xc = x.reshape(B * S, D)
    for l in range(w_in.shape[0]):
        tabs = _rope_tables(positions)
        q, k, v, u, gm, gs = _inproj(xc, tabs, w_in[l], q_norm_g[l], kv_norm_g[l], w_uq[l], w_ukv[l])
        o = _attention(q, k, v, B, S)
        ys = _s5_branch(u, B, S, (s5_a_re[l], s5_a_im[l], s5_log_dt[l], s5_b_re[l], s5_b_im[l],
                                  s5_c_re[l], s5_c_im[l], s5_d[l]))
        x2 = _merge_cross_attention(xc, o, ys, gm, gs, mem, B, S, w_mla_o[l], w_s5_glu[l], w_out[l],
                                    ln1_g[l], ln1_b[l], mem_ln_g[l], mem_ln_b[l], w_xq[l], w_xkv[l], w_xo[l],
                                    ln2_g[l], ln2_b[l])
        xc = _moe(x2, w_router[l], router_bias[l], w_exp_gu[l], w_exp_down[l], w_sh_gu[l],
                  w_sh_down[l], ln3_g[l], ln3_b[l])
    return xc.reshape(B, S, D)
```

```python
import functools
import math

import jax
import jax.numpy as jnp
import numpy as np
from jax import lax
from jax.experimental import pallas as pl
from jax.experimental.pallas import tpu as pltpu

F32 = jnp.float32
BF16 = jnp.bfloat16
I32 = jnp.int32

D_MODEL = 1024
MLA_HEADS = 8
QK_NOPE = 64
QK_ROPE = 32
V_HEAD = 64
Q_LORA = 256
KV_LORA = 256
ROPE_THETA = 10000.0
S5_GROUP_CH = 16
S5_WIDTH = 512
S5_GROUPS = 32
S5_STATE = 64
XATTN_HEADS = 4
XATTN_HEAD_DIM = 128
N_EXPERTS = 64
TOP_K = 8
N_EXPERT_GROUPS = 8
TOPK_GROUPS = 4
EXPERT_FF = 256
SHARED_FF = 256
ROUTED_SCALE = 2.5
LN_EPS = 1e-5
RMS_EPS = 1e-6
DEPTH = 1
DEEPNORM_ALPHA = (2.0 * DEPTH) ** 0.25

LANES = 128
HEAD_PAD = 128
ROPE_LO = QK_NOPE
ROPE_HALF = QK_ROPE // 2

TM = 512
TQ = 1024
KV_GROUPS = (2, 1)
S5_CHUNK = 16
TW = 512
OH_ROWS = 2304
PR = 16
PAD_ROWS = PR
SEG_ALIGN = 8
SORT_ROWS = TOP_K * TW + N_EXPERTS * SEG_ALIGN
TS = SORT_ROWS + PAD_ROWS
XW = D_MODEL // 2 + LANES
CR = 512
PPC = CR // PR
GATHER_AHEAD = 3
N_SLOTS = GATHER_AHEAD + 1
POS_RADIX = 64
POS_NONE_Q = 127
assert SORT_ROWS <= POS_RADIX * POS_NONE_Q and SORT_ROWS % OH_ROWS == 0
VMEM_LIMIT = 48 * 1024 * 1024


def _cparams(*sem):
    return pltpu.CompilerParams(dimension_semantics=sem, vmem_limit_bytes=VMEM_LIMIT)


def _dot(a, b):
    return jnp.dot(a, b, preferred_element_type=F32)


def _dot_nt(a, b, precision=None):
    return lax.dot_general(a, b, (((1,), (1,)), ((), ())), preferred_element_type=F32,
                           precision=precision)


def _dot_exact(a, b):
    return jnp.dot(a, b, preferred_element_type=F32, precision=lax.Precision.HIGHEST)


def _layer_norm(h, g, b):
    mu = jnp.mean(h, axis=-1, keepdims=True)
    c = h - mu
    var = jnp.mean(c * c, axis=-1, keepdims=True)
    return c * lax.rsqrt(var + LN_EPS) * g + b


def _rms_norm(h, g):
    return h * lax.rsqrt(jnp.mean(h * h, axis=-1, keepdims=True) + RMS_EPS) * g


def _full(shape):
    n = len(shape)
    return pl.BlockSpec(shape, lambda *_: (0,) * n)


def _rope_angle_kernel(pos_ref, invf_ref, cos_ref, sin_ref):
    ang = pos_ref[...].astype(F32) * invf_ref[...]
    cos_ref[...] = jnp.cos(ang)
    sin_ref[...] = jnp.sin(ang)


def _rope_tables(positions):
    T = positions.size
    rows = T * ROPE_HALF // LANES
    pos_rep = jnp.repeat(positions.reshape(T), ROPE_HALF).reshape(rows, LANES)
    inv_freq = ROPE_THETA ** (-jnp.arange(0, QK_ROPE, 2, dtype=F32) / QK_ROPE)
    invf = jnp.tile(inv_freq, LANES // ROPE_HALF).reshape(1, LANES)
    cos, sin = pl.pallas_call(
        _rope_angle_kernel,
        out_shape=(jax.ShapeDtypeStruct((rows, LANES), F32),) * 2,
        name="rope_angles",
    )(pos_rep, invf)
    cos_sin = jnp.concatenate([cos.reshape(T, ROPE_HALF), sin.reshape(T, ROPE_HALF)], axis=1)
    f = np.arange(ROPE_HALF)
    spread = np.zeros((QK_ROPE, 3 * HEAD_PAD), np.float32)
    spread[f, ROPE_LO + f] = 1.0
    spread[f, ROPE_LO + ROPE_HALF + f] = 1.0
    spread[ROPE_HALF + f, HEAD_PAD + ROPE_LO + f] = -1.0
    spread[ROPE_HALF + f, 2 * HEAD_PAD + ROPE_LO + ROPE_HALF + f] = 1.0
    outside = np.ones((1, HEAD_PAD), np.float32)
    outside[0, ROPE_LO:ROPE_LO + QK_ROPE] = 0.0
    return cos_sin, jnp.asarray(spread), jnp.asarray(outside)


def _rope(x, c_tab, s_up, s_dn):
    return (x * c_tab + pltpu.roll(x, HEAD_PAD - ROPE_HALF, axis=1) * s_up
            + pltpu.roll(x, ROPE_HALF, axis=1) * s_dn)


def _inproj_kernel(x_ref, cs_ref, spread_ref, outside_ref, wlat_ref, wu_ref, wgm_ref, wgs_ref, qg_ref, kvg_ref,
                   wuq_ref, wuk_ref, wuv_ref, q_ref, k_ref, v_ref, u_ref, gm_ref, gs_ref):
    xb = x_ref[...].astype(BF16)
    lat = _dot(xb, wlat_ref[...])
    qn = _rms_norm(lat[:, :Q_LORA], qg_ref[...]).astype(BF16)
    kvn = _rms_norm(lat[:, Q_LORA:Q_LORA + KV_LORA], kvg_ref[...]).astype(BF16)
    tabs = _dot_exact(cs_ref[...], spread_ref[...])
    c_tab = tabs[:, :HEAD_PAD] + outside_ref[...]
    s_up, s_dn = tabs[:, HEAD_PAD:2 * HEAD_PAD], tabs[:, 2 * HEAD_PAD:]
    k_rope = _rope(lat[:, Q_LORA + KV_LORA:], c_tab, s_up, s_dn)
    q = _dot(qn, wuq_ref[...])
    k = _dot(kvn, wuk_ref[...])
    scale = (QK_NOPE + QK_ROPE) ** -0.5 * math.log2(math.e)
    for h in range(MLA_HEADS):
        sl = slice(h * HEAD_PAD, (h + 1) * HEAD_PAD)
        q_ref[:, sl] = (_rope(q[:, sl], c_tab, s_up, s_dn) * scale).astype(BF16)
        k_ref[:, sl] = (k[:, sl] + k_rope).astype(BF16)
    v = _dot(kvn, wuv_ref[...])
    ones_lane = lax.broadcasted_iota(I32, (1, v.shape[1]), 1) % HEAD_PAD == V_HEAD
    v_ref[...] = jnp.where(ones_lane, 1.0, v).astype(BF16)
    u_ref[...] = _dot(xb, wu_ref[...])
    gm_ref[...] = _dot(xb, wgm_ref[...]).astype(BF16)
    gs_ref[...] = _dot(xb, wgs_ref[...]).astype(BF16)


def _pad_heads(w, head_w, lo_w):
    K = w.shape[0]
    w = w.reshape(K, MLA_HEADS, head_w)[:, :, :lo_w]
    w = jnp.pad(w, ((0, 0), (0, 0), (0, HEAD_PAD - lo_w)))
    return w.reshape(K, MLA_HEADS * HEAD_PAD)


def _inproj(x2d, tabs, w_in, q_norm_g, kv_norm_g, w_uq, w_ukv):
    T = x2d.shape[0]
    o_rope = Q_LORA + KV_LORA
    o_u = o_rope + QK_ROPE
    o_gm = o_u + S5_WIDTH
    o_gs = o_gm + D_MODEL
    w_rope = jnp.pad(w_in[:, o_rope:o_u], ((0, 0), (ROPE_LO, HEAD_PAD - ROPE_LO - QK_ROPE)))
    w_lat = jnp.concatenate([w_in[:, :o_rope], w_rope], axis=1).astype(BF16)
    w_u = w_in[:, o_u:o_gm].astype(BF16)
    w_gm = w_in[:, o_gm:o_gs].astype(BF16)
    w_gs = w_in[:, o_gs:].astype(BF16)
    wuq = _pad_heads(w_uq, QK_NOPE + QK_ROPE, QK_NOPE + QK_ROPE).astype(BF16)
    kv3 = w_ukv.reshape(KV_LORA, MLA_HEADS, QK_NOPE + V_HEAD)
    wuk = _pad_heads(kv3[:, :, :QK_NOPE].reshape(KV_LORA, -1), QK_NOPE, QK_NOPE).astype(BF16)
    wuv = _pad_heads(kv3[:, :, QK_NOPE:].reshape(KV_LORA, -1), V_HEAD, V_HEAD).astype(BF16)
    HP = MLA_HEADS * HEAD_PAD
    tile = lambda w: pl.BlockSpec((TM, w), lambda i: (i, 0))
    return pl.pallas_call(
        _inproj_kernel,
        grid=(T // TM,),
        in_specs=[tile(D_MODEL), tile(QK_ROPE), _full((QK_ROPE, 3 * HEAD_PAD)), _full((1, HEAD_PAD)),
                  _full(w_lat.shape), _full(w_u.shape), _full(w_gm.shape), _full(w_gs.shape),
                  _full((1, Q_LORA)), _full((1, KV_LORA)),
                  _full(wuq.shape), _full(wuk.shape), _full(wuv.shape)],
        out_specs=[tile(HP), tile(HP), tile(HP), tile(S5_WIDTH), tile(D_MODEL), tile(D_MODEL)],
        out_shape=[jax.ShapeDtypeStruct((T, HP), BF16)] * 3
        + [jax.ShapeDtypeStruct((T, S5_WIDTH), F32)]
        + [jax.ShapeDtypeStruct((T, D_MODEL), BF16)] * 2,
        compiler_params=_cparams("parallel"),
        name="inproj_mla_prep",
    )(x2d, *tabs, w_lat, w_u, w_gm, w_gs, q_norm_g.reshape(1, -1), kv_norm_g.reshape(1, -1),
      wuq, wuk, wuv)


def _attn_kernel(q_ref, k_ref, v_ref, o_ref):
    qi = pl.program_id(2)
    q = q_ref[0]

    def update(qq, k, v, carry, row_offset=None):
        m, acc = carry
        s = _dot_nt(qq, k)
        if row_offset is not None:
            row = lax.broadcasted_iota(I32, s.shape, 0) + row_offset
            col = lax.broadcasted_iota(I32, s.shape, 1)
            s = jnp.where(col <= row, s, -jnp.inf)
        m_new = jnp.maximum(m, jnp.max(s, axis=-1, keepdims=True))
        p = jnp.exp2(s - m_new)
        acc = jnp.exp2(m - m_new) * acc + _dot(p.astype(BF16), v)
        return m_new, acc

    def step(blk, n_blk, carry):
        tk = n_blk * TQ
        start = pl.multiple_of(blk * TQ, TQ)
        return update(q, k_ref[0, pl.ds(start, tk), :], v_ref[0, pl.ds(start, tk), :], carry)

    carry = (jnp.full((TQ, 1), -jnp.inf, F32), jnp.zeros((TQ, HEAD_PAD), F32))
    done = 0
    for n_blk in KV_GROUPS:
        base = done
        n_it = (qi - base) // n_blk
        carry = lax.fori_loop(0, n_it, lambda j, c, base=base, n_blk=n_blk: step(base + j * n_blk, n_blk, c), carry)
        done = base + n_it * n_blk
    hq = TQ // 2
    d0 = pl.multiple_of(qi * TQ, TQ)
    m, acc = carry
    _, acc_lo = update(q[:hq], k_ref[0, pl.ds(d0, hq), :], v_ref[0, pl.ds(d0, hq), :], (m[:hq], acc[:hq]), 0)
    _, acc_hi = update(q[hq:], k_ref[0, pl.ds(d0, TQ), :], v_ref[0, pl.ds(d0, TQ), :], (m[hq:], acc[hq:]), hq)
    o_ref[0, :hq, :] = (acc_lo / acc_lo[:, V_HEAD:V_HEAD + 1]).astype(BF16)
    o_ref[0, hq:, :] = (acc_hi / acc_hi[:, V_HEAD:V_HEAD + 1]).astype(BF16)


def _attention(q, k, v, B, S):
    HP = MLA_HEADS * HEAD_PAD
    q, k, v = (a.reshape(B, S, HP) for a in (q, k, v))
    o = pl.pallas_call(
        _attn_kernel,
        grid=(B, MLA_HEADS, S // TQ),
        in_specs=[pl.BlockSpec((1, TQ, HEAD_PAD), lambda b, h, i: (b, i, h)),
                  pl.BlockSpec((1, S, HEAD_PAD), lambda b, h, i: (b, 0, h)),
                  pl.BlockSpec((1, S, HEAD_PAD), lambda b, h, i: (b, 0, h))],
        out_specs=pl.BlockSpec((1, TQ, HEAD_PAD), lambda b, h, i: (b, i, h)),
        out_shape=jax.ShapeDtypeStruct((B, S, HP), BF16),
        compiler_params=_cparams("parallel", "parallel", "arbitrary"),
        name="mla_flash_attention",
    )(q, k, v)
    return o.reshape(B * S, HP)


CH2 = S5_CHUNK * S5_GROUP_CH
P2 = 2 * S5_STATE
SCAN_LEVELS_MAX = 16


S5_PREP_GROUPS = 4


def _s5_prep_kernel(n_levels, *refs):
    *io_refs, mt_acc = refs
    for k in range(S5_PREP_GROUPS):
        _s5_prep_group(n_levels, *(r.at[pl.ds(k, 1)] for r in io_refs), mt_acc)


def _s5_prep_group(n_levels, arow_ref, acol_ref, bt_ref, ct_ref, d_ref,
                   mt_ref, wt_ref, vt_ref, apow_ref, dvec_ref, mt_acc):
    P, H, L = S5_STATE, S5_GROUP_CH, S5_CHUNK

    def powers(ar, ai, dt, n):
        e = jnp.exp(n * (ar * dt))
        ang = n * (ai * dt)
        return e * jnp.cos(ang), e * jnp.sin(ang)

    def zoh_coef(ar, ai, dt):
        pr, pi = powers(ar, ai, dt, 1.0)
        nr, ni = pr - 1.0, pi
        den = ar * ar + ai * ai
        return (nr * ar + ni * ai) / den, (ni * ar - nr * ai) / den

    row = arow_ref[0]
    ar, ai, dt = row[0:1], row[1:2], jnp.exp(row[2:3])
    cr, ci = zoh_coef(ar, ai, dt)
    bt = bt_ref[0]
    btr = jnp.concatenate([bt[:H], bt[:H]], axis=1)
    bti = jnp.concatenate([bt[H:], bt[H:]], axis=1)
    bbr = cr * btr - ci * bti
    bbi = cr * bti + ci * btr
    lane = lax.broadcasted_iota(I32, (1, P2), 1)
    is_re = lane < P
    ridx = lax.broadcasted_iota(I32, (CH2, P2), 0) // H
    pwr, pwi = powers(ar, ai, dt, (L - 1 - ridx).astype(F32))
    rsel = (lax.broadcasted_iota(I32, (CH2, H), 0) % H == lax.broadcasted_iota(I32, (CH2, H), 1)).astype(F32)
    bbr_t = _dot_exact(rsel, bbr)
    bbi_t = _dot_exact(rsel, bbi)
    w_re = pwr * bbr_t - pwi * bbi_t
    w_im = pwr * bbi_t + pwi * bbr_t
    wt_ref[0] = jnp.where(is_re, w_re, w_im).astype(BF16)
    a_r, a_i = powers(ar, ai, dt, float(L))
    sign = jnp.where(is_re, -1.0, 1.0)
    for lvl in range(SCAN_LEVELS_MAX):
        if lvl < n_levels:
            apow_ref[0, lvl:lvl + 1, :] = a_r
            apow_ref[0, SCAN_LEVELS_MAX + lvl:SCAN_LEVELS_MAX + lvl + 1, :] = sign * a_i
            a_r, a_i = a_r * a_r - a_i * a_i, 2.0 * a_r * a_i
        else:
            apow_ref[0, lvl:lvl + 1, :] = jnp.zeros_like(a_r)
            apow_ref[0, SCAN_LEVELS_MAX + lvl:SCAN_LEVELS_MAX + lvl + 1, :] = jnp.zeros_like(a_r)

    col = acol_ref[0]
    arc, aic, dtc = col[:, 0:1], col[:, 1:2], jnp.exp(col[:, 2:3])
    ct = ct_ref[0]
    csel = (lax.broadcasted_iota(I32, (H, CH2), 1) % H == lax.broadcasted_iota(I32, (H, CH2), 0)).astype(F32)
    ctr = _dot_exact(ct[:, :H], csel)
    cti = _dot_exact(ct[:, H:], csel)
    sidx = (lax.broadcasted_iota(I32, (P, CH2), 1) // H).astype(F32)
    pr, pi = powers(arc, aic, dtc, sidx)
    g_r = ctr * pr - cti * pi
    g_i = ctr * pi + cti * pr
    a1r, a1i = powers(arc, aic, dtc, 1.0)
    v_r = g_r * a1r - g_i * a1i
    v_i = g_r * a1i + g_i * a1r
    vt_ref[0, :P, :] = v_r.astype(BF16)
    vt_ref[0, P:, :] = (-v_i).astype(BF16)
    bbr64, bbi64 = bbr[:, :P], bbi[:, :P]
    kt = _dot_exact(bbr64, g_r) - _dot_exact(bbi64, g_i)
    lane2 = lax.broadcasted_iota(I32, (H, CH2), 1)
    for r in range(L):
        blk = kt if r == 0 else jnp.where(lane2 >= r * H, pltpu.roll(kt, r * H, axis=1), 0.0)
        mt_acc[r * H:(r + 1) * H, :] = blk
    mt_ref[0] = mt_acc[...].astype(BF16)
    dvec_ref[0] = _dot_exact(d_ref[0], csel)


def _s5_prep(n_levels, s5_a_re, s5_a_im, s5_log_dt, s5_b_re, s5_b_im, s5_c_re, s5_c_im, s5_d):
    G, P, H = S5_GROUPS, S5_STATE, S5_GROUP_CH
    ldt = jnp.broadcast_to(s5_log_dt[:, None], (G, P))
    arow = jnp.stack([s5_a_re, s5_a_im, ldt], axis=1)
    arow = jnp.concatenate([arow, arow], axis=2)
    acol = jnp.stack([s5_a_re, s5_a_im, ldt], axis=2)
    bt = jnp.concatenate([s5_b_re.transpose(0, 2, 1), s5_b_im.transpose(0, 2, 1)], axis=1)
    ct = jnp.concatenate([s5_c_re.transpose(0, 2, 1), s5_c_im.transpose(0, 2, 1)], axis=2)
    d = s5_d.reshape(G, 1, H)
    blk = lambda *s: pl.BlockSpec((S5_PREP_GROUPS,) + s, lambda g: (g, 0, 0))
    return pl.pallas_call(
        functools.partial(_s5_prep_kernel, n_levels),
        grid=(G // S5_PREP_GROUPS,),
        in_specs=[blk(3, P2), blk(P, 3), blk(2 * H, P), blk(P, 2 * H), blk(1, H)],
        out_specs=[blk(CH2, CH2), blk(CH2, P2), blk(P2, CH2), blk(2 * SCAN_LEVELS_MAX, P2), blk(1, CH2)],
        out_shape=[jax.ShapeDtypeStruct((G, CH2, CH2), BF16),
                   jax.ShapeDtypeStruct((G, CH2, P2), BF16),
                   jax.ShapeDtypeStruct((G, P2, CH2), BF16),
                   jax.ShapeDtypeStruct((G, 2 * SCAN_LEVELS_MAX, P2), F32),
                   jax.ShapeDtypeStruct((G, 1, CH2), F32)],
        scratch_shapes=[pltpu.VMEM((CH2, CH2), F32)],
        compiler_params=_cparams("parallel"),
        name="s5_discretise",
    )(arow, acol, bt, ct, d)


def _gelu_tanh(y):
    return 0.5 * y * (1.0 + jnp.tanh(math.sqrt(2.0 / math.pi) * (y + 0.044715 * (y * y * y))))


GROUPS_PER_TILE = LANES // S5_GROUP_CH
STEPS_PER_TILE = LANES // S5_GROUP_CH


def _s5_scan_kernel(n_levels, u_ref, mt_ref, wt_ref, vt_ref, apow_ref, dvec_ref, y_ref):
    H, L = S5_GROUP_CH, S5_CHUNK
    nc = u_ref.shape[0] // L
    lane_blk = lax.broadcasted_iota(I32, (nc, LANES), 1) // H
    ridx = lax.broadcasted_iota(I32, (nc, P2), 0)
    steps = [u_ref[pl.ds(l, nc, stride=L), :] for l in range(L)]
    groups = range(GROUPS_PER_TILE)
    us, xs = [], []
    for gi in groups:
        halves = []
        for hh in range(L // STEPS_PER_TILE):
            acc = jnp.zeros((nc, LANES), F32)
            for l8 in range(STEPS_PER_TILE):
                src = steps[hh * STEPS_PER_TILE + l8]
                shift = ((l8 - gi) % STEPS_PER_TILE) * H
                moved = src if shift == 0 else pltpu.roll(src, shift, axis=1)
                acc = jnp.where(lane_blk == l8, moved, acc)
            halves.append(acc)
        u = jnp.concatenate(halves, axis=1)
        us.append(u)
        xs.append(_dot(u.astype(BF16), wt_ref[gi]))
    for lvl in range(n_levels):
        sh = 1 << lvl
        for gi in groups:
            apow = apow_ref[gi]
            prev = jnp.where(ridx >= sh, pltpu.roll(xs[gi], sh, axis=0), 0.0)
            a_r = apow[lvl:lvl + 1]
            a_i = apow[SCAN_LEVELS_MAX + lvl:SCAN_LEVELS_MAX + lvl + 1]
            xs[gi] = xs[gi] + a_r * prev + a_i * pltpu.roll(prev, S5_STATE, axis=1)
    outs = [jnp.zeros((nc, LANES), F32) for _ in range(L)]
    for gi in groups:
        x_in = jnp.where(ridx >= 1, pltpu.roll(xs[gi], 1, axis=0), 0.0)
        u = us[gi]
        y = _gelu_tanh(_dot(u.astype(BF16), mt_ref[gi]) + _dot(x_in.astype(BF16), vt_ref[gi]) + dvec_ref[gi] * u)
        for l in range(L):
            hh, l8 = divmod(l, STEPS_PER_TILE)
            src = y[:, hh * LANES:(hh + 1) * LANES]
            shift = ((gi - l8) % STEPS_PER_TILE) * H
            moved = src if shift == 0 else pltpu.roll(src, shift, axis=1)
            outs[l] = jnp.where(lane_blk == gi, moved, outs[l])
    for l in range(L):
        y_ref[pl.ds(l, nc, stride=L), :] = outs[l]


def _s5_branch(u, B, S, s5_params):
    G, L = S5_GROUPS, S5_CHUNK
    nc = S // L
    n_levels = max(1, (nc - 1).bit_length())
    assert n_levels <= SCAN_LEVELS_MAX and S5_CHUNK % STEPS_PER_TILE == 0
    mt, wt, vt, apow, dvec = _s5_prep(n_levels, *s5_params)
    n_lt = G // GROUPS_PER_TILE
    blk = lambda *s: pl.BlockSpec((GROUPS_PER_TILE,) + s, lambda b, t: (t, 0, 0))
    io = pl.BlockSpec((S, LANES), lambda b, t: (b, t))
    return pl.pallas_call(
        functools.partial(_s5_scan_kernel, n_levels),
        grid=(B, n_lt),
        in_specs=[io, blk(CH2, CH2), blk(CH2, P2), blk(P2, CH2), blk(2 * SCAN_LEVELS_MAX, P2), blk(1, CH2)],
        out_specs=io,
        out_shape=jax.ShapeDtypeStruct((B * S, S5_WIDTH), F32),
        compiler_params=_cparams("parallel", "parallel"),
        name="s5_chunk_scan",
    )(u, mt, wt, vt, apow, dvec)


def _merge_xattn_kernel(x_ref, o_ref, ys_ref, gm_ref, gs_ref, wmo_ref, wglu_ref, wout_ref, g1_ref, b1_ref,
                        k_ref, v_ref, wq_ref, wo_ref, g2_ref, b2_ref, x2_ref):
    o32 = pltpu.bitcast(o_ref[0], I32)
    first_half = lax.broadcasted_iota(I32, (o32.shape[0], HEAD_PAD), 1) < V_HEAD
    pairs = [jnp.where(first_half, o32[:, (2 * p) * HEAD_PAD:(2 * p + 1) * HEAD_PAD],
                       pltpu.roll(o32[:, (2 * p + 1) * HEAD_PAD:(2 * p + 2) * HEAD_PAD], V_HEAD, axis=1))
             for p in range(MLA_HEADS // 2)]
    o_packed = pltpu.bitcast(jnp.concatenate(pairs, axis=1), BF16)
    y_mla = _dot(o_packed, wmo_ref[...])
    z = _dot(ys_ref[0].astype(BF16), wglu_ref[...])
    y_s5 = z[:, :D_MODEL] * jax.nn.sigmoid(z[:, D_MODEL:])
    merged = (jax.nn.sigmoid(gm_ref[0].astype(F32)) * y_mla
              + jax.nn.sigmoid(gs_ref[0].astype(F32)) * y_s5)
    mix = _dot(merged.astype(BF16), wout_ref[...])
    x1 = _layer_norm(DEEPNORM_ALPHA * x_ref[0] + mix, g1_ref[...], b1_ref[...])
    q = (_dot(x1.astype(BF16), wq_ref[...]) * (XATTN_HEAD_DIM ** -0.5)).astype(BF16)
    k = k_ref[0]
    v = v_ref[0]
    outs = []
    for h in range(XATTN_HEADS):
        sl = slice(h * XATTN_HEAD_DIM, (h + 1) * XATTN_HEAD_DIM)
        s = _dot_nt(q[:, sl], k[:, sl])
        p = jnp.exp(s - jnp.max(s, axis=-1, keepdims=True))
        o = _dot(p.astype(BF16), v[:, sl]) / jnp.sum(p, axis=-1, keepdims=True)
        outs.append(o.astype(BF16))
    xa = _dot(jnp.concatenate(outs, axis=1), wo_ref[...])
    x2_ref[0] = _layer_norm(DEEPNORM_ALPHA * x1 + xa, g2_ref[...], b2_ref[...])


def _memkv_kernel(mem_ref, g_ref, b_ref, w_ref, k_ref, v_ref):
    m = _layer_norm(mem_ref[0], g_ref[...], b_ref[...]).astype(BF16)
    kv = _dot(m, w_ref[...])
    hd = XATTN_HEADS * XATTN_HEAD_DIM
    k_ref[0] = kv[:, :hd].astype(BF16)
    v_ref[0] = kv[:, hd:].astype(BF16)


def _merge_cross_attention(x2d, o, ys, gm, gs, mem, B, S, w_mla_o, w_s5_glu, w_out, ln1_g, ln1_b,
                           mem_ln_g, mem_ln_b, w_xq, w_xkv, w_xo, ln2_g, ln2_b):
    M = mem.shape[1]
    hd = XATTN_HEADS * XATTN_HEAD_DIM
    HP = MLA_HEADS * HEAD_PAD
    row = lambda a: a.reshape(1, -1)
    wmo = w_mla_o.astype(BF16)
    k, v = pl.pallas_call(
        _memkv_kernel,
        grid=(B,),
        in_specs=[pl.BlockSpec((1, M, D_MODEL), lambda b: (b, 0, 0)), _full((1, D_MODEL)), _full((1, D_MODEL)),
                  _full((D_MODEL, 2 * hd))],
        out_specs=[pl.BlockSpec((1, M, hd), lambda b: (b, 0, 0))] * 2,
        out_shape=[jax.ShapeDtypeStruct((B, M, hd), BF16)] * 2,
        compiler_params=_cparams("parallel"),
        name="memory_kv",
    )(mem, row(mem_ln_g), row(mem_ln_b), w_xkv.astype(BF16))
    tile = lambda w: pl.BlockSpec((1, TM, w), lambda b, i: (b, i, 0))
    kv_spec = pl.BlockSpec((1, M, hd), lambda b, i: (b, 0, 0))
    r3 = lambda a: a.reshape(B, S, a.shape[-1])
    x2 = pl.pallas_call(
        _merge_xattn_kernel,
        grid=(B, S // TM),
        in_specs=[tile(D_MODEL), tile(HP), tile(S5_WIDTH), tile(D_MODEL), tile(D_MODEL),
                  _full((MLA_HEADS * V_HEAD, D_MODEL)), _full((S5_WIDTH, 2 * D_MODEL)), _full((D_MODEL, D_MODEL)),
                  _full((1, D_MODEL)), _full((1, D_MODEL)),
                  kv_spec, kv_spec, _full((D_MODEL, hd)), _full((hd, D_MODEL)), _full((1, D_MODEL)), _full((1, D_MODEL))],
        out_specs=tile(D_MODEL),
        out_shape=jax.ShapeDtypeStruct((B, S, D_MODEL), F32),
        compiler_params=_cparams("parallel", "parallel"),
        name="merge_ln1_xattn_ln2",
    )(r3(x2d), r3(o), r3(ys), r3(gm), r3(gs), wmo, w_s5_glu.astype(BF16), w_out.astype(BF16), row(ln1_g), row(ln1_b),
      k, v, w_xq.astype(BF16), w_xo.astype(BF16), row(ln2_g), row(ln2_b))
    return x2.reshape(B * S, D_MODEL)


def _route_sort_kernel(x_ref, w_ref, bias_ref, xs_ref, pos_ref, cnt_ref, off_ref):
    E, per = N_EXPERTS, N_EXPERTS // N_EXPERT_GROUPS
    tm = x_ref.shape[0]
    x = x_ref[...]
    xb = x.astype(BF16)
    x_lo = (x - xb.astype(F32)).astype(BF16)
    w = w_ref[...]
    wb = w.astype(BF16)
    w_lo = (w - wb.astype(F32)).astype(BF16)
    logits = _dot_nt(wb, xb) + (_dot_nt(wb, x_lo) + _dot_nt(w_lo, xb))
    scores = jax.nn.sigmoid(logits)
    sel = scores + bias_ref[...]
    neg = -jnp.inf
    i8 = lax.broadcasted_iota(I32, (per, tm), 0)
    gscore = []
    for g in range(N_EXPERT_GROUPS):
        blk = sel[g * per:(g + 1) * per]
        m1 = jnp.max(blk, axis=0, keepdims=True)
        i1 = jnp.min(jnp.where(blk == m1, i8, per), axis=0, keepdims=True)
        m2 = jnp.max(jnp.where(i8 == i1, neg, blk), axis=0, keepdims=True)
        gscore.append(m1 + m2)
    blocks = []
    for g in range(N_EXPERT_GROUPS):
        ahead = jnp.zeros((1, tm), I32)
        for o in range(N_EXPERT_GROUPS):
            if o == g:
                continue
            before = (gscore[o] >= gscore[g]) if o < g else (gscore[o] > gscore[g])
            ahead = ahead + before.astype(I32)
        blocks.append(jnp.where(ahead < TOPK_GROUPS, sel[g * per:(g + 1) * per], neg))
    cur = jnp.concatenate(blocks, axis=0)
    ie = lax.broadcasted_iota(I32, (E, tm), 0)
    picked = jnp.zeros((E, tm), F32)
    for _ in range(TOP_K):
        m = jnp.max(cur, axis=0, keepdims=True)
        idx = jnp.min(jnp.where(cur == m, ie, E), axis=0, keepdims=True)
        hit = ie == idx
        picked = jnp.where(hit, 1.0, picked)
        cur = jnp.where(hit, neg, cur)
    wsel = scores * picked
    wnorm = wsel / jnp.sum(wsel, axis=0, keepdims=True) * ROUTED_SCALE
    pb = picked.astype(BF16)
    tri = (lax.broadcasted_iota(I32, (tm, tm), 0) <= lax.broadcasted_iota(I32, (tm, tm), 1)).astype(BF16)
    incl = _dot(pb, tri)
    cnt_col = jnp.sum(picked, axis=1, keepdims=True)
    lower = (lax.broadcasted_iota(I32, (E, E), 1) < lax.broadcasted_iota(I32, (E, E), 0)).astype(F32)
    al_col = jnp.floor((cnt_col + (SEG_ALIGN - 1.0)) * (1.0 / SEG_ALIGN)) * SEG_ALIGN
    off_col = _dot_exact(lower, jnp.broadcast_to(al_col, (E, LANES)))[:, 0:1]
    pos = off_col + incl - 1.0
    q = jnp.floor(pos * (1.0 / POS_RADIX))
    pq = jnp.concatenate([jnp.where(picked > 0.0, POS_RADIX * q, POS_RADIX * POS_NONE_Q),
                          jnp.where(picked > 0.0, pos - POS_RADIX * q, 0.0)], axis=0)
    pos_ref[...] = pq
    pqb = pq.astype(BF16)
    cnt_ref[0] = jnp.broadcast_to(cnt_col, (E, LANES)).astype(I32)
    off_ref[0] = jnp.broadcast_to(off_col, (E, LANES)).astype(I32)
    cnt_row = _dot_nt(jnp.ones((8, tm), BF16), pb)
    al_row = jnp.floor((cnt_row + (SEG_ALIGN - 1.0)) * (1.0 / SEG_ALIGN)) * SEG_ALIGN
    er = lax.broadcasted_iota(I32, (E, 4 * E), 0)
    ec = lax.broadcasted_iota(I32, (E, 4 * E), 1) % E
    off_row4 = _dot_exact(al_row, (er < ec).astype(F32))[0:1]
    end_row4 = off_row4 + _dot_exact(al_row, (er == ec).astype(F32))[0:1]
    w_hi = wnorm.astype(BF16).astype(F32)
    w_r1 = wnorm - w_hi
    w_mid = w_r1.astype(BF16).astype(F32)
    w_lo = w_r1 - w_mid
    w4_t = jnp.concatenate([w_hi, w_mid, w_lo, jnp.zeros_like(w_lo)], axis=0).T
    xw = jnp.concatenate([xb, w4_t.astype(BF16)], axis=1)
    half = D_MODEL // 2

    def build(c, carry):
        r0 = pl.multiple_of(c * OH_ROWS, OH_ROWS)
        jr = (lax.broadcasted_iota(I32, (OH_ROWS, 4 * E), 0) + r0).astype(F32)
        member4 = jnp.where(jr >= off_row4, jnp.where(jr < end_row4, 1.0, 0.0), 0.0)
        target = _dot(member4[:, :2 * E].astype(BF16), pqb)
        j = (lax.broadcasted_iota(I32, (OH_ROWS, tm), 0) + r0).astype(F32)
        ohb = jnp.where(target == j, 1.0, 0.0).astype(BF16)
        rows = _dot(ohb, xw)
        xs_ref[0, pl.ds(r0, OH_ROWS), 0:half] = _pack_bf16_pair(rows[:, :half], rows[:, half:D_MODEL])
        wrow = jnp.sum(member4 * rows[:, D_MODEL:], axis=1, keepdims=True)
        xs_ref[0, pl.ds(r0, OH_ROWS), half:] = pltpu.bitcast(jnp.broadcast_to(wrow, (OH_ROWS, LANES)), I32)
        return carry

    lax.fori_loop(0, SORT_ROWS // OH_ROWS, build, 0)
    xs_ref[0, SORT_ROWS:, :] = jnp.zeros((PAD_ROWS, XW), I32)


def _route_sort(x2, w_router, router_bias):
    T = x2.shape[0]
    E = N_EXPERTS
    nt = T // TW
    seg = pl.BlockSpec((1, E, LANES), lambda i: (i, 0, 0))
    return pl.pallas_call(
        _route_sort_kernel,
        grid=(nt,),
        in_specs=[pl.BlockSpec((TW, D_MODEL), lambda i: (i, 0)), _full((E, D_MODEL)), _full((E, 1))],
        out_specs=[pl.BlockSpec((1, TS, XW), lambda i: (i, 0, 0)),
                   pl.BlockSpec((2 * E, TW), lambda i: (0, i)), seg, seg],
        out_shape=[jax.ShapeDtypeStruct((nt, TS, XW), I32), jax.ShapeDtypeStruct((2 * E, T), F32),
                   jax.ShapeDtypeStruct((nt, E, LANES), I32), jax.ShapeDtypeStruct((nt, E, LANES), I32)],
        compiler_params=_cparams("parallel"),
        name="route_local_sort",
    )(x2, w_router.T, router_bias.reshape(E, 1))


def _plan_kernel(cnt_ref, off_ref, pa_ref, ce_ref, gn_ref, nch_ref):
    E, nt = cnt_ref.shape
    n_rows = pa_ref.shape[0]
    cnt = cnt_ref[...].astype(F32)
    off = off_ref[...].astype(F32)
    npc = jnp.floor((cnt + (PR - 1.0)) * (1.0 / PR))
    tri = (lax.broadcasted_iota(I32, (nt, nt), 0) <= lax.broadcasted_iota(I32, (nt, nt), 1)).astype(F32)
    p_incl = _dot_exact(npc, tri)
    total = p_incl[:, nt - 1:nt]
    n_ch = jnp.floor((total + (PPC - 1.0)) * (1.0 / PPC))
    lower = (lax.broadcasted_iota(I32, (E, E), 1) < lax.broadcasted_iota(I32, (E, E), 0)).astype(F32)
    wide = lambda col: jnp.broadcast_to(col, (E, LANES))
    g0 = _dot_exact(lower, wide(n_ch))[:, 0:1]
    n_total = jnp.sum(n_ch, axis=0, keepdims=True)
    g0_row = wide(g0).T[0:1, :E]
    g1_row = g0_row + wide(n_ch).T[0:1, :E]
    g_e = lax.broadcasted_iota(I32, (n_rows, E), 0).astype(F32)
    sel = jnp.where(g_e >= g0_row, jnp.where(g_e < g1_row, 1.0, 0.0), 0.0)
    tile_base = lax.broadcasted_iota(I32, (E, nt), 1).astype(F32) * TS
    a_mat = tile_base + off - PR * (p_incl - npc)
    p_g = _dot_exact(sel, p_incl)
    a_g = _dot_exact(sel, a_mat)
    e_idx = lax.broadcasted_iota(I32, (E, LANES), 0).astype(F32)
    pick = lambda col: _dot_exact(sel, col)[:, 0:1]
    g0_g, tot_g, e_g = pick(wide(g0)), pick(wide(total)), pick(e_idx)
    gn_g = pick(wide(g0 + n_ch))
    g_p = lax.broadcasted_iota(I32, (n_rows, PPC), 0).astype(F32)
    q = (g_p - g0_g) * PPC + lax.broadcasted_iota(I32, (n_rows, PPC), 1).astype(F32)
    addr = jnp.full((n_rows, PPC), -float(SEG_ALIGN), F32)
    lo = jnp.zeros((n_rows, 1), F32)
    for i in range(nt):
        hi = p_g[:, i:i + 1]
        addr = jnp.where(q >= lo, jnp.where(q < hi, a_g[:, i:i + 1] + PR * q, addr), addr)
        lo = hi
    pa_ref[...] = addr.astype(I32)
    g_col = lax.broadcasted_iota(I32, (n_rows, LANES), 0).astype(F32)
    ce_ref[...] = jnp.where(g_col < n_total, jnp.broadcast_to(e_g, (n_rows, LANES)), E - 1.0).astype(I32)
    gn_ref[...] = jnp.broadcast_to(gn_g, (n_rows, LANES)).astype(I32)
    nch_ref[...] = jnp.broadcast_to(n_total, nch_ref.shape).astype(I32)


def _plan(cnt, off, n_chunks_max):
    n_rows = (n_chunks_max + 7) // 8 * 8
    pa, ce, gn, nch = pl.pallas_call(
        _plan_kernel,
        out_shape=[jax.ShapeDtypeStruct((n_rows, PPC), I32), jax.ShapeDtypeStruct((n_rows, LANES), I32),
                   jax.ShapeDtypeStruct((n_rows, LANES), I32), jax.ShapeDtypeStruct((8, LANES), I32)],
        name="moe_plan",
    )(cnt, off)
    return pa.reshape(n_rows * PPC), ce[:, 0], gn[:, 0], nch[0, :1]


def _unpack_bf16_pair(w):
    lo = pltpu.bitcast(lax.shift_left(w, jnp.int32(16)), F32).astype(BF16)
    hi = pltpu.bitcast(w & jnp.int32(-65536), F32).astype(BF16)
    return lo, hi


def _pack_bf16_pair(lo, hi):
    lo_bits = lax.shift_right_logical(pltpu.bitcast(lo.astype(BF16).astype(F32), I32), jnp.int32(16))
    hi_bits = pltpu.bitcast(hi.astype(BF16).astype(F32), I32)
    return hi_bits | lo_bits


def _expert_kernel(ce_ref, gn_ref, pa_ref, nch_ref, tot_ref, xs_hbm, wgu_hbm, wdn_hbm, y_hbm,
                   xbuf, ybuf, zbuf, wgu_f, wdn_f, wgu_b, wdn_b, wslot, gsem, wsem, zsem, esem):
    g = pl.program_id(0)
    n = nch_ref[0]
    slot = g % N_SLOTS
    half = D_MODEL // 2

    dummy_base = (tot_ref.shape[0] - 1) * TS

    def gather_copy(s, a, jp):
        return pltpu.make_async_copy(xs_hbm.at[pl.ds(pl.multiple_of(a, SEG_ALIGN), PR), :],
                                     xbuf.at[s, pl.ds(jp * PR, PR), :], gsem.at[s])

    def write_copy(s, a, jp):
        return pltpu.make_async_copy(ybuf.at[s, pl.ds(jp * PR, PR), :],
                                     y_hbm.at[pl.ds(pl.multiple_of(a, SEG_ALIGN), PR), :], wsem.at[s])

    def gather_start(c, s):
        for jp in range(PPC):
            gather_copy(s, jnp.maximum(pa_ref[c * PPC + jp], 0), jp).start()

    def write_start(c, s):
        for jp in range(PPC):
            a = pa_ref[c * PPC + jp]
            write_copy(s, jnp.where(a >= 0, a, dummy_base + jp * PR), jp).start()

    def gather_wait(s):
        for jp in range(PPC):
            gather_copy(s, 0, jp).wait()

    def write_wait(s):
        for jp in range(PPC):
            write_copy(s, 0, jp).wait()

    def weight_copies(e, s):
        return (pltpu.make_async_copy(wgu_hbm.at[e], wgu_f.at[s], esem.at[s]),
                pltpu.make_async_copy(wdn_hbm.at[e], wdn_f.at[s], esem.at[s]))

    @pl.when(g == 0)
    def _():
        wslot[0] = 0
        for cp in weight_copies(ce_ref[0], 0):
            cp.start()
        gather_start(0, 0)
        ybuf[...] = jnp.zeros_like(ybuf)
        zbuf[...] = jnp.zeros_like(zbuf)

        def zero_copy(row):
            return pltpu.make_async_copy(zbuf, y_hbm.at[pl.ds(pl.multiple_of(row, SEG_ALIGN), SEG_ALIGN), :], zsem)

        def tail_of(i):
            return i * TS + tot_ref[i], (TS - tot_ref[i]) // SEG_ALIGN

        def zero_start(i, c):
            base, cnt8 = tail_of(i)
            lax.fori_loop(0, cnt8, lambda q, z: (zero_copy(base + q * SEG_ALIGN).start(), z)[1], 0)
            return c

        def zero_wait(i, c):
            base, cnt8 = tail_of(i)
            lax.fori_loop(0, cnt8, lambda q, z: (zero_copy(base + q * SEG_ALIGN).wait(), z)[1], 0)
            return c

        lax.fori_loop(0, tot_ref.shape[0], zero_start, 0)
        lax.fori_loop(0, tot_ref.shape[0], zero_wait, 0)

    for c in range(1, GATHER_AHEAD):
        @pl.when((g == 0) & (c < n))
        def _(c=c):
            gather_start(c, c)

    @pl.when(g + GATHER_AHEAD < n)
    def _():
        gather_start(g + GATHER_AHEAD, (g + GATHER_AHEAD) % N_SLOTS)

    e_cur = ce_ref[g]
    e_prev = ce_ref[jnp.maximum(g - 1, 0)]
    e_prev2 = ce_ref[jnp.maximum(g - 2, 0)]

    @pl.when(g < n)
    def _():
        @pl.when((g == 0) | (e_cur != e_prev))
        def _():
            s = wslot[0]
            for cp in weight_copies(e_cur, s):
                cp.wait()
            wgu_b[...] = wgu_f[s].astype(BF16)
            wdn_b[...] = wdn_f[s].astype(BF16)
            g_next = gn_ref[g]

            @pl.when(g_next < n)
            def _():
                for cp in weight_copies(ce_ref[g_next], 1 - s):
                    cp.start()

            wslot[0] = 1 - s

        gather_wait(slot)

        @pl.when((g >= 2) & (e_prev == e_prev2))
        def _():
            write_wait((g - 2) % N_SLOTS)

        @pl.when((g >= 1) & (e_cur != e_prev))
        def _():
            write_wait((g - 1) % N_SLOTS)

        def swiglu_rows(rows):
            xw = xbuf[slot, :rows, :]
            lo, hi = _unpack_bf16_pair(xw[:, :half])
            h = _dot(jnp.concatenate([lo, hi], axis=1), wgu_b[...])
            gate, up = h[:, :EXPERT_FF], h[:, EXPERT_FF:]
            act = (gate * jax.nn.sigmoid(gate) * up).astype(BF16)
            y = _dot(act, wdn_b[...])
            w_row = pltpu.bitcast(xw[:, half:], F32)
            y = y * jnp.concatenate([w_row] * (D_MODEL // LANES), axis=1)
            ybuf[slot, :rows, :] = _pack_bf16_pair(y[:, :half], y[:, half:])

        second_half_used = pa_ref[g * PPC + PPC // 2] >= 0

        @pl.when(second_half_used)
        def _():
            swiglu_rows(CR)

        @pl.when(jnp.logical_not(second_half_used))
        def _():
            swiglu_rows(CR // 2)

        write_start(g, slot)

    @pl.when(g == pl.num_programs(0) - 1)
    def _():
        @pl.when((n >= 2) & (ce_ref[jnp.maximum(n - 1, 0)] == ce_ref[jnp.maximum(n - 2, 0)]))
        def _():
            write_wait((n - 2) % N_SLOTS)

        write_wait((n - 1) % N_SLOTS)


def _experts(ce, gn, pa, nch, tot, xs, w_exp_gu, w_exp_down):
    n_rows = xs.shape[0]
    half = D_MODEL // 2
    hbm = pl.BlockSpec(memory_space=pl.ANY)
    grid_spec = pltpu.PrefetchScalarGridSpec(
        num_scalar_prefetch=5,
        grid=(ce.shape[0],),
        in_specs=[hbm, hbm, hbm],
        out_specs=hbm,
        scratch_shapes=[pltpu.VMEM((N_SLOTS, CR, XW), I32), pltpu.VMEM((N_SLOTS, CR, half), I32),
                        pltpu.VMEM((SEG_ALIGN, half), I32),
                        pltpu.VMEM((2, D_MODEL, 2 * EXPERT_FF), F32), pltpu.VMEM((2, EXPERT_FF, D_MODEL), F32),
                        pltpu.VMEM((D_MODEL, 2 * EXPERT_FF), BF16), pltpu.VMEM((EXPERT_FF, D_MODEL), BF16),
                        pltpu.SMEM((1,), I32),
                        pltpu.SemaphoreType.DMA((N_SLOTS,)), pltpu.SemaphoreType.DMA((N_SLOTS,)),
                        pltpu.SemaphoreType.DMA(()), pltpu.SemaphoreType.DMA((2,))],
    )
    return pl.pallas_call(
        _expert_kernel,
        grid_spec=grid_spec,
        out_shape=jax.ShapeDtypeStruct((n_rows + TS, half), I32),
        compiler_params=_cparams("arbitrary"),
        name="moe_grouped_swiglu",
    )(ce, gn, pa, nch, tot, xs, w_exp_gu, w_exp_down)


def _combine_kernel(tot_ref, y_ref, pos_ref, cnt_ref, off_ref, x2_ref, wsg_ref, wsd_ref, g_ref, b_ref, out_ref,
                    acc_lo, acc_hi):
    tm = x2_ref.shape[0]
    E = N_EXPERTS
    n_sorted = tot_ref[pl.program_id(0)]
    x2 = x2_ref[...]
    h = _dot(x2.astype(BF16), wsg_ref[...])
    gate, up = h[:, :SHARED_FF], h[:, SHARED_FF:]
    shared = _dot((gate * jax.nn.sigmoid(gate) * up).astype(BF16), wsd_ref[...])
    pq_t = pos_ref[...].T.astype(BF16)
    off_col = off_ref[0][:, 0:1].astype(F32)
    cnt_col = cnt_ref[0][:, 0:1].astype(F32)
    end_col = off_col + jnp.floor((cnt_col + (SEG_ALIGN - 1.0)) * (1.0 / SEG_ALIGN)) * SEG_ALIGN
    off_col2 = jnp.concatenate([off_col, off_col], axis=0)
    end_col2 = jnp.concatenate([end_col, end_col], axis=0)
    acc_lo[...] = jnp.zeros_like(acc_lo)
    acc_hi[...] = jnp.zeros_like(acc_hi)

    def gather_back(c, carry):
        r0 = pl.multiple_of(c * OH_ROWS, OH_ROWS)
        jc = (lax.broadcasted_iota(I32, (2 * E, OH_ROWS), 1) + r0).astype(F32)
        member2 = jnp.where(jc >= off_col2, jnp.where(jc < end_col2, 1.0, 0.0), 0.0)
        target = _dot(pq_t, member2.astype(BF16))
        j = (lax.broadcasted_iota(I32, (tm, OH_ROWS), 1) + r0).astype(F32)
        ohb = jnp.where(target == j, 1.0, 0.0).astype(BF16)
        yw = y_ref[0, pl.ds(r0, OH_ROWS), :]
        row = lax.broadcasted_iota(I32, yw.shape, 0) + r0
        lo, hi = _unpack_bf16_pair(jnp.where(row < n_sorted, yw, 0))
        acc_lo[...] += _dot(ohb, lo)
        acc_hi[...] += _dot(ohb, hi)
        return carry

    lax.fori_loop(0, SORT_ROWS // OH_ROWS, gather_back, 0)
    ff = shared + jnp.concatenate([acc_lo[...], acc_hi[...]], axis=1)
    out_ref[...] = _layer_norm(DEEPNORM_ALPHA * x2 + ff, g_ref[...], b_ref[...])


def _combine(tot, y, pos, cnt, off, x2, w_sh_gu, w_sh_down, ln_g, ln_b):
    T = x2.shape[0]
    nt = T // TW
    half = D_MODEL // 2
    tile = lambda w: pl.BlockSpec((TW, w), lambda i: (i, 0))
    return pl.pallas_call(
        _combine_kernel,
        grid=(nt,),
        in_specs=[pl.BlockSpec(memory_space=pltpu.SMEM),
                  pl.BlockSpec((1, TS, half), lambda i: (i, 0, 0)),
                  pl.BlockSpec((2 * N_EXPERTS, TW), lambda i: (0, i)),
                  pl.BlockSpec((1, N_EXPERTS, LANES), lambda i: (i, 0, 0)),
                  pl.BlockSpec((1, N_EXPERTS, LANES), lambda i: (i, 0, 0)), tile(D_MODEL),
                  _full((D_MODEL, 2 * SHARED_FF)), _full((SHARED_FF, D_MODEL)),
                  _full((1, D_MODEL)), _full((1, D_MODEL))],
        out_specs=tile(D_MODEL),
        out_shape=jax.ShapeDtypeStruct((T, D_MODEL), F32),
        scratch_shapes=[pltpu.VMEM((TW, half), F32), pltpu.VMEM((TW, half), F32)],
        compiler_params=_cparams("parallel"),
        name="moe_combine_shared_ln3",
    )(tot, y.reshape(nt + 1, TS, half), pos, cnt, off, x2, w_sh_gu.astype(BF16), w_sh_down.astype(BF16),
      ln_g.reshape(1, -1), ln_b.reshape(1, -1))


def _moe(x2, w_router, router_bias, w_exp_gu, w_exp_down, w_sh_gu, w_sh_down, ln_g, ln_b):
    T = x2.shape[0]
    E = N_EXPERTS
    nt = T // TW
    xs, pos, cnt, off = _route_sort(x2, w_router, router_bias)
    n_pieces_max = (T * TOP_K + nt * E * (PR - 1)) // PR
    n_chunks_max = n_pieces_max // PPC + E
    pa, ce, gn, nch = _plan(cnt[:, :, 0].T, off[:, :, 0].T, n_chunks_max)
    last_al = (cnt[:, E - 1, 0] + SEG_ALIGN - 1) // SEG_ALIGN * SEG_ALIGN
    tot = jnp.concatenate([off[:, E - 1, 0] + last_al, jnp.zeros((1,), I32)])
    y = _experts(ce, gn, pa, nch, tot, xs.reshape(nt * TS, XW), w_exp_gu, w_exp_down)
    return _combine(tot, y, pos, cnt, off, x2, w_sh_gu, w_sh_down, ln_g, ln_b)


def kernel(x, mem, positions, w_in, q_norm_g, kv_norm_g, w_uq, w_ukv, w_mla_o, s5_a_re, s5_a_im, s5_log_dt, s5_b_re, s5_b_im, s5_c_re, s5_c_im, s5_d, w_s5_glu, w_out, ln1_g, ln1_b, mem_ln_g, mem_ln_b, w_xq, w_xkv, w_xo, ln2_g, ln2_b, w_router, router_bias, w_exp_gu, w_exp_down, w_sh_gu, w_sh_down, ln3_g, ln3_b):
    B, S, D = x.shape
    assert D == D_MODEL and S % TQ == 0 and (B * S) % TM == 0 and S % S5_CHUNK == 0
    xc = x.reshape(B * S, D)
    for l in range(w_in.shape[0]):
        tabs = _rope_tables(positions)
        q, k, v, u, gm, gs = _inproj(xc, tabs, w_in[l], q_norm_g[l], kv_norm_g[l], w_uq[l], w_ukv[l])
        o = _attention(q, k, v, B, S)
        ys = _s5_branch(u, B, S, (s5_a_re[l], s5_a_im[l], s5_log_dt[l], s5_b_re[l], s5_b_im[l],
                                  s5_c_re[l], s5_c_im[l], s5_d[l]))
        x2 = _merge_cross_attention(xc, o, ys, gm, gs, mem, B, S, w_mla_o[l], w_s5_glu[l], w_out[l],
                                    ln1_g[l], ln1_b[l], mem_ln_g[l], mem_ln_b[l], w_xq[l], w_xkv[l], w_xo[l],
                                    ln2_g[l], ln2_b[l])
        xc = _moe(x2, w_router[l], router_bias[l], w_exp_gu[l], w_exp_down[l], w_sh_gu[l],
                  w_sh_down[l], ln3_g[l], ln3_b[l])
    return xc.reshape(B, S, D)
```

```python
import functools
import math

import jax
import jax.numpy as jnp
import numpy as np
from jax import lax
from jax.experimental import pallas as pl
from jax.experimental.pallas import tpu as pltpu

F32 = jnp.float32
BF16 = jnp.bfloat16
I32 = jnp.int32

D_MODEL = 1024
MLA_HEADS = 8
QK_NOPE = 64
QK_ROPE = 32
V_HEAD = 64
Q_LORA = 256
KV_LORA = 256
ROPE_THETA = 10000.0
S5_GROUP_CH = 16
S5_WIDTH = 512
S5_GROUPS = 32
S5_STATE = 64
XATTN_HEADS = 4
XATTN_HEAD_DIM = 128
N_EXPERTS = 64
TOP_K = 8
N_EXPERT_GROUPS = 8
TOPK_GROUPS = 4
EXPERT_FF = 256
SHARED_FF = 256
ROUTED_SCALE = 2.5
LN_EPS = 1e-5
RMS_EPS = 1e-6
DEPTH = 1
DEEPNORM_ALPHA = (2.0 * DEPTH) ** 0.25

LANES = 128
HEAD_PAD = 128
ROPE_LO = QK_NOPE
ROPE_HALF = QK_ROPE // 2

TM = 512
TQ = 1024
KV_GROUPS = (2, 1)
S5_CHUNK = 16
TW = 512
OH_ROWS = 2304
PR = 16
PAD_ROWS = PR
SEG_ALIGN = 8
SORT_ROWS = TOP_K * TW + N_EXPERTS * SEG_ALIGN
TS = SORT_ROWS + PAD_ROWS
XW = D_MODEL // 2 + LANES
CR = 512
PPC = CR // PR
GATHER_AHEAD = 3
N_SLOTS = GATHER_AHEAD + 1
POS_RADIX = 64
POS_NONE_Q = 127
assert SORT_ROWS <= POS_RADIX * POS_NONE_Q and SORT_ROWS % OH_ROWS == 0
VMEM_LIMIT = 48 * 1024 * 1024


def _cparams(*sem):
    return pltpu.CompilerParams(dimension_semantics=sem, vmem_limit_bytes=VMEM_LIMIT)


def _dot(a, b):
    return jnp.dot(a, b, preferred_element_type=F32)


def _dot_nt(a, b, precision=None):
    return lax.dot_general(a, b, (((1,), (1,)), ((), ())), preferred_element_type=F32,
                           precision=precision)


def _dot_exact(a, b):
    return jnp.dot(a, b, preferred_element_type=F32, precision=lax.Precision.HIGHEST)


def _layer_norm(h, g, b):
    mu = jnp.mean(h, axis=-1, keepdims=True)
    c = h - mu
    var = jnp.mean(c * c, axis=-1, keepdims=True)
    return c * lax.rsqrt(var + LN_EPS) * g + b


def _rms_norm(h, g):
    return h * lax.rsqrt(jnp.mean(h * h, axis=-1, keepdims=True) + RMS_EPS) * g


def _full(shape):
    n = len(shape)
    return pl.BlockSpec(shape, lambda *_: (0,) * n)


def _bf16_terms(v):
    hi = v.astype(BF16)
    r1 = v - hi.astype(F32)
    mid = r1.astype(BF16)
    return hi, mid, (r1 - mid.astype(F32)).astype(BF16)


def _rope_angle_kernel(pos_ref, invf_ref, *out_refs):
    ang = pos_ref[...].astype(F32) * invf_ref[...]
    terms = _bf16_terms(jnp.cos(ang)) + _bf16_terms(jnp.sin(ang))
    for ref, t in zip(out_refs, terms):
        ref[...] = t


def _rope_tables(positions):
    T = positions.size
    rows = T * ROPE_HALF // LANES
    pos_rep = jnp.repeat(positions.reshape(T), ROPE_HALF).reshape(rows, LANES)
    inv_freq = ROPE_THETA ** (-jnp.arange(0, QK_ROPE, 2, dtype=F32) / QK_ROPE)
    invf = jnp.tile(inv_freq, LANES // ROPE_HALF).reshape(1, LANES)
    c_hi, c_mid, c_lo, s_hi, s_mid, s_lo = pl.pallas_call(
        _rope_angle_kernel,
        out_shape=(jax.ShapeDtypeStruct((rows, LANES), BF16),) * 6,
        name="rope_angles",
    )(pos_rep, invf)
    cos_sin = jnp.concatenate([t.reshape(T, ROPE_HALF) for t in (c_hi, s_hi, c_mid, s_mid, c_lo, s_lo)], axis=1)
    f = np.arange(ROPE_HALF)
    spread = np.zeros((QK_ROPE, 3 * HEAD_PAD), np.float32)
    spread[f, ROPE_LO + f] = 1.0
    spread[f, ROPE_LO + ROPE_HALF + f] = 1.0
    spread[ROPE_HALF + f, HEAD_PAD + ROPE_LO + f] = -1.0
    spread[ROPE_HALF + f, 2 * HEAD_PAD + ROPE_LO + ROPE_HALF + f] = 1.0
    outside = np.ones((1, HEAD_PAD), np.float32)
    outside[0, ROPE_LO:ROPE_LO + QK_ROPE] = 0.0
    return cos_sin, jnp.asarray(np.tile(spread, (3, 1)), BF16), jnp.asarray(outside)


def _rope(x, c_tab, s_up, s_dn):
    return (x * c_tab + pltpu.roll(x, HEAD_PAD - ROPE_HALF, axis=1) * s_up
            + pltpu.roll(x, ROPE_HALF, axis=1) * s_dn)


def _inproj_kernel(x_ref, cs_ref, spread_ref, outside_ref, wlat_ref, wu_ref, wgm_ref, wgs_ref, qg_ref, kvg_ref,
                   wuq_ref, wuk_ref, wuv_ref, q_ref, k_ref, v_ref, u_ref, gm_ref, gs_ref):
    xb = x_ref[...].astype(BF16)
    lat = _dot(xb, wlat_ref[...])
    qn = _rms_norm(lat[:, :Q_LORA], qg_ref[...]).astype(BF16)
    kvn = _rms_norm(lat[:, Q_LORA:Q_LORA + KV_LORA], kvg_ref[...]).astype(BF16)
    tabs = _dot(cs_ref[...], spread_ref[...])
    c_tab = tabs[:, :HEAD_PAD] + outside_ref[...]
    s_up, s_dn = tabs[:, HEAD_PAD:2 * HEAD_PAD], tabs[:, 2 * HEAD_PAD:]
    k_rope = _rope(lat[:, Q_LORA + KV_LORA:], c_tab, s_up, s_dn)
    q = _dot(qn, wuq_ref[...])
    k = _dot(kvn, wuk_ref[...])
    scale = (QK_NOPE + QK_ROPE) ** -0.5 * math.log2(math.e)
    for h in range(MLA_HEADS):
        sl = slice(h * HEAD_PAD, (h + 1) * HEAD_PAD)
        q_ref[:, sl] = (_rope(q[:, sl], c_tab, s_up, s_dn) * scale).astype(BF16)
        k_ref[:, sl] = (k[:, sl] + k_rope).astype(BF16)
    v = _dot(kvn, wuv_ref[...])
    ones_lane = lax.broadcasted_iota(I32, (1, v.shape[1]), 1) % HEAD_PAD == V_HEAD
    v_ref[...] = jnp.where(ones_lane, 1.0, v).astype(BF16)
    u_ref[...] = _dot(xb, wu_ref[...])
    gm_ref[...] = _dot(xb, wgm_ref[...]).astype(BF16)
    gs_ref[...] = _dot(xb, wgs_ref[...]).astype(BF16)


def _pad_heads(w, head_w, lo_w):
    K = w.shape[0]
    w = w.reshape(K, MLA_HEADS, head_w)[:, :, :lo_w]
    w = jnp.pad(w, ((0, 0), (0, 0), (0, HEAD_PAD - lo_w)))
    return w.reshape(K, MLA_HEADS * HEAD_PAD)


def _inproj(x2d, tabs, w_in, q_norm_g, kv_norm_g, w_uq, w_ukv):
    T = x2d.shape[0]
    o_rope = Q_LORA + KV_LORA
    o_u = o_rope + QK_ROPE
    o_gm = o_u + S5_WIDTH
    o_gs = o_gm + D_MODEL
    w_rope = jnp.pad(w_in[:, o_rope:o_u], ((0, 0), (ROPE_LO, HEAD_PAD - ROPE_LO - QK_ROPE)))
    w_lat = jnp.concatenate([w_in[:, :o_rope], w_rope], axis=1).astype(BF16)
    w_u = w_in[:, o_u:o_gm].astype(BF16)
    w_gm = w_in[:, o_gm:o_gs].astype(BF16)
    w_gs = w_in[:, o_gs:].astype(BF16)
    wuq = _pad_heads(w_uq, QK_NOPE + QK_ROPE, QK_NOPE + QK_ROPE).astype(BF16)
    kv3 = w_ukv.reshape(KV_LORA, MLA_HEADS, QK_NOPE + V_HEAD)
    wuk = _pad_heads(kv3[:, :, :QK_NOPE].reshape(KV_LORA, -1), QK_NOPE, QK_NOPE).astype(BF16)
    wuv = _pad_heads(kv3[:, :, QK_NOPE:].reshape(KV_LORA, -1), V_HEAD, V_HEAD).astype(BF16)
    HP = MLA_HEADS * HEAD_PAD
    tile = lambda w: pl.BlockSpec((TM, w), lambda i: (i, 0))
    return pl.pallas_call(
        _inproj_kernel,
        grid=(T // TM,),
        in_specs=[tile(D_MODEL), tile(3 * QK_ROPE), _full((3 * QK_ROPE, 3 * HEAD_PAD)), _full((1, HEAD_PAD)),
                  _full(w_lat.shape), _full(w_u.shape), _full(w_gm.shape), _full(w_gs.shape),
                  _full((1, Q_LORA)), _full((1, KV_LORA)),
                  _full(wuq.shape), _full(wuk.shape), _full(wuv.shape)],
        out_specs=[tile(HP), tile(HP), tile(HP), tile(S5_WIDTH), tile(D_MODEL), tile(D_MODEL)],
        out_shape=[jax.ShapeDtypeStruct((T, HP), BF16)] * 3
        + [jax.ShapeDtypeStruct((T, S5_WIDTH), F32)]
        + [jax.ShapeDtypeStruct((T, D_MODEL), BF16)] * 2,
        compiler_params=_cparams("parallel"),
        name="inproj_mla_prep",
    )(x2d, *tabs, w_lat, w_u, w_gm, w_gs, q_norm_g.reshape(1, -1), kv_norm_g.reshape(1, -1),
      wuq, wuk, wuv)


def _attn_kernel(q_ref, k_ref, v_ref, o_ref):
    qi = pl.program_id(2)
    q = q_ref[0]

    def update(qq, k, v, carry, row_offset=None):
        m, acc = carry
        s = _dot_nt(qq, k)
        if row_offset is not None:
            row = lax.broadcasted_iota(I32, s.shape, 0) + row_offset
            col = lax.broadcasted_iota(I32, s.shape, 1)
            s = jnp.where(col <= row, s, -jnp.inf)
        m_new = jnp.maximum(m, jnp.max(s, axis=-1, keepdims=True))
        p = jnp.exp2(s - m_new)
        acc = jnp.exp2(m - m_new) * acc + _dot(p.astype(BF16), v)
        return m_new, acc

    def step(blk, n_blk, carry):
        tk = n_blk * TQ
        start = pl.multiple_of(blk * TQ, TQ)
        return update(q, k_ref[0, pl.ds(start, tk), :], v_ref[0, pl.ds(start, tk), :], carry)

    carry = (jnp.full((TQ, 1), -jnp.inf, F32), jnp.zeros((TQ, HEAD_PAD), F32))
    done = 0
    for n_blk in KV_GROUPS:
        base = done
        n_it = (qi - base) // n_blk
        carry = lax.fori_loop(0, n_it, lambda j, c, base=base, n_blk=n_blk: step(base + j * n_blk, n_blk, c), carry)
        done = base + n_it * n_blk
    hq = TQ // 2
    d0 = pl.multiple_of(qi * TQ, TQ)
    m, acc = carry
    _, acc_lo = update(q[:hq], k_ref[0, pl.ds(d0, hq), :], v_ref[0, pl.ds(d0, hq), :], (m[:hq], acc[:hq]), 0)
    _, acc_hi = update(q[hq:], k_ref[0, pl.ds(d0, TQ), :], v_ref[0, pl.ds(d0, TQ), :], (m[hq:], acc[hq:]), hq)
    o_ref[0, :hq, :] = (acc_lo / acc_lo[:, V_HEAD:V_HEAD + 1]).astype(BF16)
    o_ref[0, hq:, :] = (acc_hi / acc_hi[:, V_HEAD:V_HEAD + 1]).astype(BF16)


def _attention(q, k, v, B, S):
    HP = MLA_HEADS * HEAD_PAD
    q, k, v = (a.reshape(B, S, HP) for a in (q, k, v))
    o = pl.pallas_call(
        _attn_kernel,
        grid=(B, MLA_HEADS, S // TQ),
        in_specs=[pl.BlockSpec((1, TQ, HEAD_PAD), lambda b, h, i: (b, i, h)),
                  pl.BlockSpec((1, S, HEAD_PAD), lambda b, h, i: (b, 0, h)),
                  pl.BlockSpec((1, S, HEAD_PAD), lambda b, h, i: (b, 0, h))],
        out_specs=pl.BlockSpec((1, TQ, HEAD_PAD), lambda b, h, i: (b, i, h)),
        out_shape=jax.ShapeDtypeStruct((B, S, HP), BF16),
        compiler_params=_cparams("parallel", "parallel", "arbitrary"),
        name="mla_flash_attention",
    )(q, k, v)
    return o.reshape(B * S, HP)


CH2 = S5_CHUNK * S5_GROUP_CH
P2 = 2 * S5_STATE
SCAN_LEVELS_MAX = 16


S5_PREP_GROUPS = 4


def _s5_prep_kernel(n_levels, *refs):
    *io_refs, mt_acc = refs
    for k in range(S5_PREP_GROUPS):
        _s5_prep_group(n_levels, *(r.at[pl.ds(k, 1)] for r in io_refs), mt_acc)


def _s5_prep_group(n_levels, arow_ref, acol_ref, bt_ref, ct_ref, d_ref,
                   mt_ref, wt_ref, vt_ref, apow_ref, dvec_ref, mt_acc):
    P, H, L = S5_STATE, S5_GROUP_CH, S5_CHUNK

    def powers(ar, ai, dt, n):
        e = jnp.exp(n * (ar * dt))
        ang = n * (ai * dt)
        return e * jnp.cos(ang), e * jnp.sin(ang)

    def zoh_coef(ar, ai, dt):
        pr, pi = powers(ar, ai, dt, 1.0)
        nr, ni = pr - 1.0, pi
        den = ar * ar + ai * ai
        return (nr * ar + ni * ai) / den, (ni * ar - nr * ai) / den

    row = arow_ref[0]
    ar, ai, dt = row[0:1], row[1:2], jnp.exp(row[2:3])
    cr, ci = zoh_coef(ar, ai, dt)
    bt = bt_ref[0]
    btr = jnp.concatenate([bt[:H], bt[:H]], axis=1)
    bti = jnp.concatenate([bt[H:], bt[H:]], axis=1)
    bbr = cr * btr - ci * bti
    bbi = cr * bti + ci * btr
    lane = lax.broadcasted_iota(I32, (1, P2), 1)
    is_re = lane < P
    ridx = lax.broadcasted_iota(I32, (CH2, P2), 0) // H
    pwr, pwi = powers(ar, ai, dt, (L - 1 - ridx).astype(F32))
    rsel = (lax.broadcasted_iota(I32, (CH2, H), 0) % H == lax.broadcasted_iota(I32, (CH2, H), 1)).astype(F32)
    bbr_t = _dot_exact(rsel, bbr)
    bbi_t = _dot_exact(rsel, bbi)
    w_re = pwr * bbr_t - pwi * bbi_t
    w_im = pwr * bbi_t + pwi * bbr_t
    wt_ref[0] = jnp.where(is_re, w_re, w_im).astype(BF16)
    a_r, a_i = powers(ar, ai, dt, float(L))
    sign = jnp.where(is_re, -1.0, 1.0)
    for lvl in range(SCAN_LEVELS_MAX):
        if lvl < n_levels:
            apow_ref[0, lvl:lvl + 1, :] = a_r
            apow_ref[0, SCAN_LEVELS_MAX + lvl:SCAN_LEVELS_MAX + lvl + 1, :] = sign * a_i
            a_r, a_i = a_r * a_r - a_i * a_i, 2.0 * a_r * a_i
        else:
            apow_ref[0, lvl:lvl + 1, :] = jnp.zeros_like(a_r)
            apow_ref[0, SCAN_LEVELS_MAX + lvl:SCAN_LEVELS_MAX + lvl + 1, :] = jnp.zeros_like(a_r)

    col = acol_ref[0]
    arc, aic, dtc = col[:, 0:1], col[:, 1:2], jnp.exp(col[:, 2:3])
    ct = ct_ref[0]
    csel = (lax.broadcasted_iota(I32, (H, CH2), 1) % H == lax.broadcasted_iota(I32, (H, CH2), 0)).astype(F32)
    ctr = _dot_exact(ct[:, :H], csel)
    cti = _dot_exact(ct[:, H:], csel)
    sidx = (lax.broadcasted_iota(I32, (P, CH2), 1) // H).astype(F32)
    pr, pi = powers(arc, aic, dtc, sidx)
    g_r = ctr * pr - cti * pi
    g_i = ctr * pi + cti * pr
    a1r, a1i = powers(arc, aic, dtc, 1.0)
    v_r = g_r * a1r - g_i * a1i
    v_i = g_r * a1i + g_i * a1r
    vt_ref[0, :P, :] = v_r.astype(BF16)
    vt_ref[0, P:, :] = (-v_i).astype(BF16)
    bbr64, bbi64 = bbr[:, :P], bbi[:, :P]
    kt = _dot_exact(bbr64, g_r) - _dot_exact(bbi64, g_i)
    lane2 = lax.broadcasted_iota(I32, (H, CH2), 1)
    for r in range(L):
        blk = kt if r == 0 else jnp.where(lane2 >= r * H, pltpu.roll(kt, r * H, axis=1), 0.0)
        mt_acc[r * H:(r + 1) * H, :] = blk
    mt_ref[0] = mt_acc[...].astype(BF16)
    dvec_ref[0] = _dot_exact(d_ref[0], csel)


def _s5_prep(n_levels, s5_a_re, s5_a_im, s5_log_dt, s5_b_re, s5_b_im, s5_c_re, s5_c_im, s5_d):
    G, P, H = S5_GROUPS, S5_STATE, S5_GROUP_CH
    ldt = jnp.broadcast_to(s5_log_dt[:, None], (G, P))
    arow = jnp.stack([s5_a_re, s5_a_im, ldt], axis=1)
    arow = jnp.concatenate([arow, arow], axis=2)
    acol = jnp.stack([s5_a_re, s5_a_im, ldt], axis=2)
    bt = jnp.concatenate([s5_b_re.transpose(0, 2, 1), s5_b_im.transpose(0, 2, 1)], axis=1)
    ct = jnp.concatenate([s5_c_re.transpose(0, 2, 1), s5_c_im.transpose(0, 2, 1)], axis=2)
    d = s5_d.reshape(G, 1, H)
    blk = lambda *s: pl.BlockSpec((S5_PREP_GROUPS,) + s, lambda g: (g, 0, 0))
    return pl.pallas_call(
        functools.partial(_s5_prep_kernel, n_levels),
        grid=(G // S5_PREP_GROUPS,),
        in_specs=[blk(3, P2), blk(P, 3), blk(2 * H, P), blk(P, 2 * H), blk(1, H)],
        out_specs=[blk(CH2, CH2), blk(CH2, P2), blk(P2, CH2), blk(2 * SCAN_LEVELS_MAX, P2), blk(1, CH2)],
        out_shape=[jax.ShapeDtypeStruct((G, CH2, CH2), BF16),
                   jax.ShapeDtypeStruct((G, CH2, P2), BF16),
                   jax.ShapeDtypeStruct((G, P2, CH2), BF16),
                   jax.ShapeDtypeStruct((G, 2 * SCAN_LEVELS_MAX, P2), F32),
                   jax.ShapeDtypeStruct((G, 1, CH2), F32)],
        scratch_shapes=[pltpu.VMEM((CH2, CH2), F32)],
        compiler_params=_cparams("parallel"),
        name="s5_discretise",
    )(arow, acol, bt, ct, d)


def _gelu_tanh(y):
    return 0.5 * y * (1.0 + jnp.tanh(math.sqrt(2.0 / math.pi) * (y + 0.044715 * (y * y * y))))


GROUPS_PER_TILE = LANES // S5_GROUP_CH
STEPS_PER_TILE = LANES // S5_GROUP_CH


def _s5_scan_kernel(n_levels, u_ref, mt_ref, wt_ref, vt_ref, apow_ref, dvec_ref, y_ref):
    H, L = S5_GROUP_CH, S5_CHUNK
    nc = u_ref.shape[0] // L
    lane_blk = lax.broadcasted_iota(I32, (nc, LANES), 1) // H
    ridx = lax.broadcasted_iota(I32, (nc, P2), 0)
    steps = [u_ref[pl.ds(l, nc, stride=L), :] for l in range(L)]
    groups = range(GROUPS_PER_TILE)
    us, xs = [], []
    for gi in groups:
        halves = []
        for hh in range(L // STEPS_PER_TILE):
            acc = jnp.zeros((nc, LANES), F32)
            for l8 in range(STEPS_PER_TILE):
                src = steps[hh * STEPS_PER_TILE + l8]
                shift = ((l8 - gi) % STEPS_PER_TILE) * H
                moved = src if shift == 0 else pltpu.roll(src, shift, axis=1)
                acc = jnp.where(lane_blk == l8, moved, acc)
            halves.append(acc)
        u = jnp.concatenate(halves, axis=1)
        us.append(u)
        xs.append(_dot(u.astype(BF16), wt_ref[gi]))
    for lvl in range(n_levels):
        sh = 1 << lvl
        for gi in groups:
            apow = apow_ref[gi]
            prev = jnp.where(ridx >= sh, pltpu.roll(xs[gi], sh, axis=0), 0.0)
            a_r = apow[lvl:lvl + 1]
            a_i = apow[SCAN_LEVELS_MAX + lvl:SCAN_LEVELS_MAX + lvl + 1]
            xs[gi] = xs[gi] + a_r * prev + a_i * pltpu.roll(prev, S5_STATE, axis=1)
    outs = [jnp.zeros((nc, LANES), F32) for _ in range(L)]
    for gi in groups:
        x_in = jnp.where(ridx >= 1, pltpu.roll(xs[gi], 1, axis=0), 0.0)
        u = us[gi]
        y = _gelu_tanh(_dot(u.astype(BF16), mt_ref[gi]) + _dot(x_in.astype(BF16), vt_ref[gi]) + dvec_ref[gi] * u)
        for l in range(L):
            hh, l8 = divmod(l, STEPS_PER_TILE)
            src = y[:, hh * LANES:(hh + 1) * LANES]
            shift = ((gi - l8) % STEPS_PER_TILE) * H
            moved = src if shift == 0 else pltpu.roll(src, shift, axis=1)
            outs[l] = jnp.where(lane_blk == gi, moved, outs[l])
    for l in range(L):
        y_ref[pl.ds(l, nc, stride=L), :] = outs[l]


def _s5_branch(u, B, S, s5_params):
    G, L = S5_GROUPS, S5_CHUNK
    nc = S // L
    n_levels = max(1, (nc - 1).bit_length())
    assert n_levels <= SCAN_LEVELS_MAX and S5_CHUNK % STEPS_PER_TILE == 0
    mt, wt, vt, apow, dvec = _s5_prep(n_levels, *s5_params)
    n_lt = G // GROUPS_PER_TILE
    blk = lambda *s: pl.BlockSpec((GROUPS_PER_TILE,) + s, lambda b, t: (t, 0, 0))
    io = pl.BlockSpec((S, LANES), lambda b, t: (b, t))
    return pl.pallas_call(
        functools.partial(_s5_scan_kernel, n_levels),
        grid=(B, n_lt),
        in_specs=[io, blk(CH2, CH2), blk(CH2, P2), blk(P2, CH2), blk(2 * SCAN_LEVELS_MAX, P2), blk(1, CH2)],
        out_specs=io,
        out_shape=jax.ShapeDtypeStruct((B * S, S5_WIDTH), F32),
        compiler_params=_cparams("parallel", "parallel"),
        name="s5_chunk_scan",
    )(u, mt, wt, vt, apow, dvec)


def _merge_xattn_kernel(x_ref, o_ref, ys_ref, gm_ref, gs_ref, wmo_ref, wglu_ref, wout_ref, g1_ref, b1_ref,
                        k_ref, v_ref, wq_ref, wo_ref, g2_ref, b2_ref, x2_ref):
    o32 = pltpu.bitcast(o_ref[0], I32)
    first_half = lax.broadcasted_iota(I32, (o32.shape[0], HEAD_PAD), 1) < V_HEAD
    pairs = [jnp.where(first_half, o32[:, (2 * p) * HEAD_PAD:(2 * p + 1) * HEAD_PAD],
                       pltpu.roll(o32[:, (2 * p + 1) * HEAD_PAD:(2 * p + 2) * HEAD_PAD], V_HEAD, axis=1))
             for p in range(MLA_HEADS // 2)]
    o_packed = pltpu.bitcast(jnp.concatenate(pairs, axis=1), BF16)
    y_mla = _dot(o_packed, wmo_ref[...])
    z = _dot(ys_ref[0].astype(BF16), wglu_ref[...])
    y_s5 = z[:, :D_MODEL] * jax.nn.sigmoid(z[:, D_MODEL:])
    merged = (jax.nn.sigmoid(gm_ref[0].astype(F32)) * y_mla
              + jax.nn.sigmoid(gs_ref[0].astype(F32)) * y_s5)
    mix = _dot(merged.astype(BF16), wout_ref[...])
    x1 = _layer_norm(DEEPNORM_ALPHA * x_ref[0] + mix, g1_ref[...], b1_ref[...])
    q = (_dot(x1.astype(BF16), wq_ref[...]) * (XATTN_HEAD_DIM ** -0.5)).astype(BF16)
    k = k_ref[0]
    v = v_ref[0]
    outs = []
    for h in range(XATTN_HEADS):
        sl = slice(h * XATTN_HEAD_DIM, (h + 1) * XATTN_HEAD_DIM)
        s = _dot_nt(q[:, sl], k[:, sl])
        p = jnp.exp(s - jnp.max(s, axis=-1, keepdims=True))
        o = _dot(p.astype(BF16), v[:, sl]) / jnp.sum(p, axis=-1, keepdims=True)
        outs.append(o.astype(BF16))
    xa = _dot(jnp.concatenate(outs, axis=1), wo_ref[...])
    x2_ref[0] = _layer_norm(DEEPNORM_ALPHA * x1 + xa, g2_ref[...], b2_ref[...])


def _memkv_kernel(mem_ref, g_ref, b_ref, w_ref, k_ref, v_ref):
    m = _layer_norm(mem_ref[0], g_ref[...], b_ref[...]).astype(BF16)
    kv = _dot(m, w_ref[...])
    hd = XATTN_HEADS * XATTN_HEAD_DIM
    k_ref[0] = kv[:, :hd].astype(BF16)
    v_ref[0] = kv[:, hd:].astype(BF16)


def _merge_cross_attention(x2d, o, ys, gm, gs, mem, B, S, w_mla_o, w_s5_glu, w_out, ln1_g, ln1_b,
                           mem_ln_g, mem_ln_b, w_xq, w_xkv, w_xo, ln2_g, ln2_b):
    M = mem.shape[1]
    hd = XATTN_HEADS * XATTN_HEAD_DIM
    HP = MLA_HEADS * HEAD_PAD
    row = lambda a: a.reshape(1, -1)
    wmo = w_mla_o.astype(BF16)
    k, v = pl.pallas_call(
        _memkv_kernel,
        grid=(B,),
        in_specs=[pl.BlockSpec((1, M, D_MODEL), lambda b: (b, 0, 0)), _full((1, D_MODEL)), _full((1, D_MODEL)),
                  _full((D_MODEL, 2 * hd))],
        out_specs=[pl.BlockSpec((1, M, hd), lambda b: (b, 0, 0))] * 2,
        out_shape=[jax.ShapeDtypeStruct((B, M, hd), BF16)] * 2,
        compiler_params=_cparams("parallel"),
        name="memory_kv",
    )(mem, row(mem_ln_g), row(mem_ln_b), w_xkv.astype(BF16))
    tile = lambda w: pl.BlockSpec((1, TM, w), lambda b, i: (b, i, 0))
    kv_spec = pl.BlockSpec((1, M, hd), lambda b, i: (b, 0, 0))
    r3 = lambda a: a.reshape(B, S, a.shape[-1])
    x2 = pl.pallas_call(
        _merge_xattn_kernel,
        grid=(B, S // TM),
        in_specs=[tile(D_MODEL), tile(HP), tile(S5_WIDTH), tile(D_MODEL), tile(D_MODEL),
                  _full((MLA_HEADS * V_HEAD, D_MODEL)), _full((S5_WIDTH, 2 * D_MODEL)), _full((D_MODEL, D_MODEL)),
                  _full((1, D_MODEL)), _full((1, D_MODEL)),
                  kv_spec, kv_spec, _full((D_MODEL, hd)), _full((hd, D_MODEL)), _full((1, D_MODEL)), _full((1, D_MODEL))],
        out_specs=tile(D_MODEL),
        out_shape=jax.ShapeDtypeStruct((B, S, D_MODEL), F32),
        compiler_params=_cparams("parallel", "parallel"),
        name="merge_ln1_xattn_ln2",
    )(r3(x2d), r3(o), r3(ys), r3(gm), r3(gs), wmo, w_s5_glu.astype(BF16), w_out.astype(BF16), row(ln1_g), row(ln1_b),
      k, v, w_xq.astype(BF16), w_xo.astype(BF16), row(ln2_g), row(ln2_b))
    return x2.reshape(B * S, D_MODEL)


def _route_sort_kernel(x_ref, w_ref, bias_ref, xs_ref, pos_ref, cnt_ref, off_ref):
    E, per = N_EXPERTS, N_EXPERTS // N_EXPERT_GROUPS
    tm = x_ref.shape[0]
    x = x_ref[...]
    xb = x.astype(BF16)
    x_lo = (x - xb.astype(F32)).astype(BF16)
    w = w_ref[...]
    wb = w.astype(BF16)
    w_lo = (w - wb.astype(F32)).astype(BF16)
    logits = _dot_nt(wb, xb) + (_dot_nt(wb, x_lo) + _dot_nt(w_lo, xb))
    scores = jax.nn.sigmoid(logits)
    sel = scores + bias_ref[...]
    neg = -jnp.inf
    i8 = lax.broadcasted_iota(I32, (per, tm), 0)
    gscore = []
    for g in range(N_EXPERT_GROUPS):
        blk = sel[g * per:(g + 1) * per]
        m1 = jnp.max(blk, axis=0, keepdims=True)
        i1 = jnp.min(jnp.where(blk == m1, i8, per), axis=0, keepdims=True)
        m2 = jnp.max(jnp.where(i8 == i1, neg, blk), axis=0, keepdims=True)
        gscore.append(m1 + m2)
    blocks = []
    for g in range(N_EXPERT_GROUPS):
        ahead = jnp.zeros((1, tm), I32)
        for o in range(N_EXPERT_GROUPS):
            if o == g:
                continue
            before = (gscore[o] >= gscore[g]) if o < g else (gscore[o] > gscore[g])
            ahead = ahead + before.astype(I32)
        blocks.append(jnp.where(ahead < TOPK_GROUPS, sel[g * per:(g + 1) * per], neg))
    cur = jnp.concatenate(blocks, axis=0)
    ie = lax.broadcasted_iota(I32, (E, tm), 0)
    picked = jnp.zeros((E, tm), F32)
    for _ in range(TOP_K):
        m = jnp.max(cur, axis=0, keepdims=True)
        idx = jnp.min(jnp.where(cur == m, ie, E), axis=0, keepdims=True)
        hit = ie == idx
        picked = jnp.where(hit, 1.0, picked)
        cur = jnp.where(hit, neg, cur)
    wsel = scores * picked
    wnorm = wsel / jnp.sum(wsel, axis=0, keepdims=True) * ROUTED_SCALE
    pb = picked.astype(BF16)
    tri = (lax.broadcasted_iota(I32, (tm, tm), 0) <= lax.broadcasted_iota(I32, (tm, tm), 1)).astype(BF16)
    incl = _dot(pb, tri)
    cnt_col = jnp.sum(picked, axis=1, keepdims=True)
    lower = (lax.broadcasted_iota(I32, (E, E), 1) < lax.broadcasted_iota(I32, (E, E), 0)).astype(F32)
    al_col = jnp.floor((cnt_col + (SEG_ALIGN - 1.0)) * (1.0 / SEG_ALIGN)) * SEG_ALIGN
    off_col = _dot_exact(lower, jnp.broadcast_to(al_col, (E, LANES)))[:, 0:1]
    pos = off_col + incl - 1.0
    q = jnp.floor(pos * (1.0 / POS_RADIX))
    pq = jnp.concatenate([jnp.where(picked > 0.0, POS_RADIX * q, POS_RADIX * POS_NONE_Q),
                          jnp.where(picked > 0.0, pos - POS_RADIX * q, 0.0)], axis=0)
    pos_ref[...] = pq
    pqb = pq.astype(BF16)
    cnt_ref[0] = jnp.broadcast_to(cnt_col, (E, LANES)).astype(I32)
    off_ref[0] = jnp.broadcast_to(off_col, (E, LANES)).astype(I32)
    cnt_row = _dot_nt(jnp.ones((8, tm), BF16), pb)
    al_row = jnp.floor((cnt_row + (SEG_ALIGN - 1.0)) * (1.0 / SEG_ALIGN)) * SEG_ALIGN
    er = lax.broadcasted_iota(I32, (E, 4 * E), 0)
    ec = lax.broadcasted_iota(I32, (E, 4 * E), 1) % E
    off_row4 = _dot_exact(al_row, (er < ec).astype(F32))[0:1]
    end_row4 = off_row4 + _dot_exact(al_row, (er == ec).astype(F32))[0:1]
    w_hi = wnorm.astype(BF16).astype(F32)
    w_r1 = wnorm - w_hi
    w_mid = w_r1.astype(BF16).astype(F32)
    w_lo = w_r1 - w_mid
    w4_t = jnp.concatenate([w_hi, w_mid, w_lo, jnp.zeros_like(w_lo)], axis=0).T
    xw = jnp.concatenate([xb, w4_t.astype(BF16)], axis=1)
    half = D_MODEL // 2

    def build(c, carry):
        r0 = pl.multiple_of(c * OH_ROWS, OH_ROWS)
        jr = (lax.broadcasted_iota(I32, (OH_ROWS, 4 * E), 0) + r0).astype(F32)
        member4 = jnp.where(jr >= off_row4, jnp.where(jr < end_row4, 1.0, 0.0), 0.0)
        target = _dot(member4[:, :2 * E].astype(BF16), pqb)
        j = (lax.broadcasted_iota(I32, (OH_ROWS, tm), 0) + r0).astype(F32)
        ohb = jnp.where(target == j, 1.0, 0.0).astype(BF16)
        rows = _dot(ohb, xw)
        xs_ref[0, pl.ds(r0, OH_ROWS), 0:half] = _pack_bf16_pair(rows[:, :half], rows[:, half:D_MODEL])
        wrow = jnp.sum(member4 * rows[:, D_MODEL:], axis=1, keepdims=True)
        xs_ref[0, pl.ds(r0, OH_ROWS), half:] = pltpu.bitcast(jnp.broadcast_to(wrow, (OH_ROWS, LANES)), I32)
        return carry

    lax.fori_loop(0, SORT_ROWS // OH_ROWS, build, 0)
    xs_ref[0, SORT_ROWS:, :] = jnp.zeros((PAD_ROWS, XW), I32)


def _route_sort(x2, w_router, router_bias):
    T = x2.shape[0]
    E = N_EXPERTS
    nt = T // TW
    seg = pl.BlockSpec((1, E, LANES), lambda i: (i, 0, 0))
    return pl.pallas_call(
        _route_sort_kernel,
        grid=(nt,),
        in_specs=[pl.BlockSpec((TW, D_MODEL), lambda i: (i, 0)), _full((E, D_MODEL)), _full((E, 1))],
        out_specs=[pl.BlockSpec((1, TS, XW), lambda i: (i, 0, 0)),
                   pl.BlockSpec((2 * E, TW), lambda i: (0, i)), seg, seg],
        out_shape=[jax.ShapeDtypeStruct((nt, TS, XW), I32), jax.ShapeDtypeStruct((2 * E, T), F32),
                   jax.ShapeDtypeStruct((nt, E, LANES), I32), jax.ShapeDtypeStruct((nt, E, LANES), I32)],
        compiler_params=_cparams("parallel"),
        name="route_local_sort",
    )(x2, w_router.T, router_bias.reshape(E, 1))


def _plan_kernel(cnt_ref, off_ref, pa_ref, ce_ref, gn_ref, nch_ref):
    E, nt = cnt_ref.shape
    n_rows = pa_ref.shape[0]
    cnt = cnt_ref[...].astype(F32)
    off = off_ref[...].astype(F32)
    npc = jnp.floor((cnt + (PR - 1.0)) * (1.0 / PR))
    tri = (lax.broadcasted_iota(I32, (nt, nt), 0) <= lax.broadcasted_iota(I32, (nt, nt), 1)).astype(F32)
    p_incl = _dot_exact(npc, tri)
    total = p_incl[:, nt - 1:nt]
    n_ch = jnp.floor((total + (PPC - 1.0)) * (1.0 / PPC))
    lower = (lax.broadcasted_iota(I32, (E, E), 1) < lax.broadcasted_iota(I32, (E, E), 0)).astype(F32)
    wide = lambda col: jnp.broadcast_to(col, (E, LANES))
    g0 = _dot_exact(lower, wide(n_ch))[:, 0:1]
    n_total = jnp.sum(n_ch, axis=0, keepdims=True)
    g0_row = wide(g0).T[0:1, :E]
    g1_row = g0_row + wide(n_ch).T[0:1, :E]
    g_e = lax.broadcasted_iota(I32, (n_rows, E), 0).astype(F32)
    sel = jnp.where(g_e >= g0_row, jnp.where(g_e < g1_row, 1.0, 0.0), 0.0)
    tile_base = lax.broadcasted_iota(I32, (E, nt), 1).astype(F32) * TS
    a_mat = tile_base + off - PR * (p_incl - npc)
    p_g = _dot_exact(sel, p_incl)
    a_g = _dot_exact(sel, a_mat)
    e_idx = lax.broadcasted_iota(I32, (E, LANES), 0).astype(F32)
    pick = lambda col: _dot_exact(sel, col)[:, 0:1]
    g0_g, tot_g, e_g = pick(wide(g0)), pick(wide(total)), pick(e_idx)
    gn_g = pick(wide(g0 + n_ch))
    g_p = lax.broadcasted_iota(I32, (n_rows, PPC), 0).astype(F32)
    q = (g_p - g0_g) * PPC + lax.broadcasted_iota(I32, (n_rows, PPC), 1).astype(F32)
    addr = jnp.full((n_rows, PPC), -float(SEG_ALIGN), F32)
    lo = jnp.zeros((n_rows, 1), F32)
    for i in range(nt):
        hi = p_g[:, i:i + 1]
        addr = jnp.where(q >= lo, jnp.where(q < hi, a_g[:, i:i + 1] + PR * q, addr), addr)
        lo = hi
    pa_ref[...] = addr.astype(I32)
    g_col = lax.broadcasted_iota(I32, (n_rows, LANES), 0).astype(F32)
    ce_ref[...] = jnp.where(g_col < n_total, jnp.broadcast_to(e_g, (n_rows, LANES)), E - 1.0).astype(I32)
    gn_ref[...] = jnp.broadcast_to(gn_g, (n_rows, LANES)).astype(I32)
    nch_ref[...] = jnp.broadcast_to(n_total, nch_ref.shape).astype(I32)


def _plan(cnt, off, n_chunks_max):
    n_rows = (n_chunks_max + 7) // 8 * 8
    pa, ce, gn, nch = pl.pallas_call(
        _plan_kernel,
        out_shape=[jax.ShapeDtypeStruct((n_rows, PPC), I32), jax.ShapeDtypeStruct((n_rows, LANES), I32),
                   jax.ShapeDtypeStruct((n_rows, LANES), I32), jax.ShapeDtypeStruct((8, LANES), I32)],
        name="moe_plan",
    )(cnt, off)
    return pa.reshape(n_rows * PPC), ce[:, 0], gn[:, 0], nch[0, :1]


def _unpack_bf16_pair(w):
    lo = pltpu.bitcast(lax.shift_left(w, jnp.int32(16)), F32).astype(BF16)
    hi = pltpu.bitcast(w & jnp.int32(-65536), F32).astype(BF16)
    return lo, hi


def _pack_bf16_pair(lo, hi):
    lo_bits = lax.shift_right_logical(pltpu.bitcast(lo.astype(BF16).astype(F32), I32), jnp.int32(16))
    hi_bits = pltpu.bitcast(hi.astype(BF16).astype(F32), I32)
    return hi_bits | lo_bits


def _expert_kernel(ce_ref, gn_ref, pa_ref, nch_ref, tot_ref, xs_hbm, wgu_hbm, wdn_hbm, y_hbm,
                   xbuf, ybuf, zbuf, wgu_f, wdn_f, wgu_b, wdn_b, wslot, gsem, wsem, zsem, esem):
    g = pl.program_id(0)
    n = nch_ref[0]
    slot = g % N_SLOTS
    half = D_MODEL // 2

    dummy_base = (tot_ref.shape[0] - 1) * TS

    def gather_copy(s, a, jp):
        return pltpu.make_async_copy(xs_hbm.at[pl.ds(pl.multiple_of(a, SEG_ALIGN), PR), :],
                                     xbuf.at[s, pl.ds(jp * PR, PR), :], gsem.at[s])

    def write_copy(s, a, jp):
        return pltpu.make_async_copy(ybuf.at[s, pl.ds(jp * PR, PR), :],
                                     y_hbm.at[pl.ds(pl.multiple_of(a, SEG_ALIGN), PR), :], wsem.at[s])

    def gather_start(c, s):
        for jp in range(PPC):
            gather_copy(s, jnp.maximum(pa_ref[c * PPC + jp], 0), jp).start()

    def write_start(c, s):
        for jp in range(PPC):
            a = pa_ref[c * PPC + jp]
            write_copy(s, jnp.where(a >= 0, a, dummy_base + jp * PR), jp).start()

    def gather_wait(s):
        for jp in range(PPC):
            gather_copy(s, 0, jp).wait()

    def write_wait(s):
        for jp in range(PPC):
            write_copy(s, 0, jp).wait()

    def weight_copies(e, s):
        return (pltpu.make_async_copy(wgu_hbm.at[e], wgu_f.at[s], esem.at[s]),
                pltpu.make_async_copy(wdn_hbm.at[e], wdn_f.at[s], esem.at[s]))

    @pl.when(g == 0)
    def _():
        wslot[0] = 0
        for cp in weight_copies(ce_ref[0], 0):
            cp.start()
        gather_start(0, 0)
        ybuf[...] = jnp.zeros_like(ybuf)
        zbuf[...] = jnp.zeros_like(zbuf)

        def zero_copy(row):
            return pltpu.make_async_copy(zbuf, y_hbm.at[pl.ds(pl.multiple_of(row, SEG_ALIGN), SEG_ALIGN), :], zsem)

        def tail_of(i):
            return i * TS + tot_ref[i], (TS - tot_ref[i]) // SEG_ALIGN

        def zero_start(i, c):
            base, cnt8 = tail_of(i)
            lax.fori_loop(0, cnt8, lambda q, z: (zero_copy(base + q * SEG_ALIGN).start(), z)[1], 0)
            return c

        def zero_wait(i, c):
            base, cnt8 = tail_of(i)
            lax.fori_loop(0, cnt8, lambda q, z: (zero_copy(base + q * SEG_ALIGN).wait(), z)[1], 0)
            return c

        lax.fori_loop(0, tot_ref.shape[0], zero_start, 0)
        lax.fori_loop(0, tot_ref.shape[0], zero_wait, 0)

    for c in range(1, GATHER_AHEAD):
        @pl.when((g == 0) & (c < n))
        def _(c=c):
            gather_start(c, c)

    @pl.when(g + GATHER_AHEAD < n)
    def _():
        gather_start(g + GATHER_AHEAD, (g + GATHER_AHEAD) % N_SLOTS)

    e_cur = ce_ref[g]
    e_prev = ce_ref[jnp.maximum(g - 1, 0)]
    e_prev2 = ce_ref[jnp.maximum(g - 2, 0)]

    @pl.when(g < n)
    def _():
        @pl.when((g == 0) | (e_cur != e_prev))
        def _():
            s = wslot[0]
            for cp in weight_copies(e_cur, s):
                cp.wait()
            wgu_b[...] = wgu_f[s].astype(BF16)
            wdn_b[...] = wdn_f[s].astype(BF16)
            g_next = gn_ref[g]

            @pl.when(g_next < n)
            def _():
                for cp in weight_copies(ce_ref[g_next], 1 - s):
                    cp.start()

            wslot[0] = 1 - s

        gather_wait(slot)

        @pl.when((g >= 2) & (e_prev == e_prev2))
        def _():
            write_wait((g - 2) % N_SLOTS)

        @pl.when((g >= 1) & (e_cur != e_prev))
        def _():
            write_wait((g - 1) % N_SLOTS)

        def swiglu_rows(rows):
            xw = xbuf[slot, :rows, :]
            lo, hi = _unpack_bf16_pair(xw[:, :half])
            h = _dot(jnp.concatenate([lo, hi], axis=1), wgu_b[...])
            gate, up = h[:, :EXPERT_FF], h[:, EXPERT_FF:]
            act = (gate * jax.nn.sigmoid(gate) * up).astype(BF16)
            y = _dot(act, wdn_b[...])
            w_row = pltpu.bitcast(xw[:, half:], F32)
            y = y * jnp.concatenate([w_row] * (D_MODEL // LANES), axis=1)
            ybuf[slot, :rows, :] = _pack_bf16_pair(y[:, :half], y[:, half:])

        second_half_used = pa_ref[g * PPC + PPC // 2] >= 0

        @pl.when(second_half_used)
        def _():
            swiglu_rows(CR)

        @pl.when(jnp.logical_not(second_half_used))
        def _():
            swiglu_rows(CR // 2)

        write_start(g, slot)

    @pl.when(g == pl.num_programs(0) - 1)
    def _():
        @pl.when((n >= 2) & (ce_ref[jnp.maximum(n - 1, 0)] == ce_ref[jnp.maximum(n - 2, 0)]))
        def _():
            write_wait((n - 2) % N_SLOTS)

        write_wait((n - 1) % N_SLOTS)


def _experts(ce, gn, pa, nch, tot, xs, w_exp_gu, w_exp_down):
    n_rows = xs.shape[0]
    half = D_MODEL // 2
    hbm = pl.BlockSpec(memory_space=pl.ANY)
    grid_spec = pltpu.PrefetchScalarGridSpec(
        num_scalar_prefetch=5,
        grid=(ce.shape[0],),
        in_specs=[hbm, hbm, hbm],
        out_specs=hbm,
        scratch_shapes=[pltpu.VMEM((N_SLOTS, CR, XW), I32), pltpu.VMEM((N_SLOTS, CR, half), I32),
                        pltpu.VMEM((SEG_ALIGN, half), I32),
                        pltpu.VMEM((2, D_MODEL, 2 * EXPERT_FF), F32), pltpu.VMEM((2, EXPERT_FF, D_MODEL), F32),
                        pltpu.VMEM((D_MODEL, 2 * EXPERT_FF), BF16), pltpu.VMEM((EXPERT_FF, D_MODEL), BF16),
                        pltpu.SMEM((1,), I32),
                        pltpu.SemaphoreType.DMA((N_SLOTS,)), pltpu.SemaphoreType.DMA((N_SLOTS,)),
                        pltpu.SemaphoreType.DMA(()), pltpu.SemaphoreType.DMA((2,))],
    )
    return pl.pallas_call(
        _expert_kernel,
        grid_spec=grid_spec,
        out_shape=jax.ShapeDtypeStruct((n_rows + TS, half), I32),
        compiler_params=_cparams("arbitrary"),
        name="moe_grouped_swiglu",
    )(ce, gn, pa, nch, tot, xs, w_exp_gu, w_exp_down)


def _combine_kernel(tot_ref, y_ref, pos_ref, cnt_ref, off_ref, x2_ref, wsg_ref, wsd_ref, g_ref, b_ref, out_ref,
                    acc_lo, acc_hi):
    tm = x2_ref.shape[0]
    E = N_EXPERTS
    n_sorted = tot_ref[pl.program_id(0)]
    x2 = x2_ref[...]
    h = _dot(x2.astype(BF16), wsg_ref[...])
    gate, up = h[:, :SHARED_FF], h[:, SHARED_FF:]
    shared = _dot((gate * jax.nn.sigmoid(gate) * up).astype(BF16), wsd_ref[...])
    pq_t = pos_ref[...].T.astype(BF16)
    off_col = off_ref[0][:, 0:1].astype(F32)
    cnt_col = cnt_ref[0][:, 0:1].astype(F32)
    end_col = off_col + jnp.floor((cnt_col + (SEG_ALIGN - 1.0)) * (1.0 / SEG_ALIGN)) * SEG_ALIGN
    off_col2 = jnp.concatenate([off_col, off_col], axis=0)
    end_col2 = jnp.concatenate([end_col, end_col], axis=0)
    acc_lo[...] = jnp.zeros_like(acc_lo)
    acc_hi[...] = jnp.zeros_like(acc_hi)

    def gather_back(c, carry):
        r0 = pl.multiple_of(c * OH_ROWS, OH_ROWS)
        jc = (lax.broadcasted_iota(I32, (2 * E, OH_ROWS), 1) + r0).astype(F32)
        member2 = jnp.where(jc >= off_col2, jnp.where(jc < end_col2, 1.0, 0.0), 0.0)
        target = _dot(pq_t, member2.astype(BF16))
        j = (lax.broadcasted_iota(I32, (tm, OH_ROWS), 1) + r0).astype(F32)
        ohb = jnp.where(target == j, 1.0, 0.0).astype(BF16)
        yw = y_ref[0, pl.ds(r0, OH_ROWS), :]
        row = lax.broadcasted_iota(I32, yw.shape, 0) + r0
        lo, hi = _unpack_bf16_pair(jnp.where(row < n_sorted, yw, 0))
        acc_lo[...] += _dot(ohb, lo)
        acc_hi[...] += _dot(ohb, hi)
        return carry

    lax.fori_loop(0, SORT_ROWS // OH_ROWS, gather_back, 0)
    ff = shared + jnp.concatenate([acc_lo[...], acc_hi[...]], axis=1)
    out_ref[...] = _layer_norm(DEEPNORM_ALPHA * x2 + ff, g_ref[...], b_ref[...])


def _combine(tot, y, pos, cnt, off, x2, w_sh_gu, w_sh_down, ln_g, ln_b):
    T = x2.shape[0]
    nt = T // TW
    half = D_MODEL // 2
    tile = lambda w: pl.BlockSpec((TW, w), lambda i: (i, 0))
    return pl.pallas_call(
        _combine_kernel,
        grid=(nt,),
        in_specs=[pl.BlockSpec(memory_space=pltpu.SMEM),
                  pl.BlockSpec((1, TS, half), lambda i: (i, 0, 0)),
                  pl.BlockSpec((2 * N_EXPERTS, TW), lambda i: (0, i)),
                  pl.BlockSpec((1, N_EXPERTS, LANES), lambda i: (i, 0, 0)),
                  pl.BlockSpec((1, N_EXPERTS, LANES), lambda i: (i, 0, 0)), tile(D_MODEL),
                  _full((D_MODEL, 2 * SHARED_FF)), _full((SHARED_FF, D_MODEL)),
                  _full((1, D_MODEL)), _full((1, D_MODEL))],
        out_specs=tile(D_MODEL),
        out_shape=jax.ShapeDtypeStruct((T, D_MODEL), F32),
        scratch_shapes=[pltpu.VMEM((TW, half), F32), pltpu.VMEM((TW, half), F32)],
        compiler_params=_cparams("parallel"),
        name="moe_combine_shared_ln3",
    )(tot, y.reshape(nt + 1, TS, half), pos, cnt, off, x2, w_sh_gu.astype(BF16), w_sh_down.astype(BF16),
      ln_g.reshape(1, -1), ln_b.reshape(1, -1))


def _moe(x2, w_router, router_bias, w_exp_gu, w_exp_down, w_sh_gu, w_sh_down, ln_g, ln_b):
    T = x2.shape[0]
    E = N_EXPERTS
    nt = T // TW
    xs, pos, cnt, off = _route_sort(x2, w_router, router_bias)
    n_pieces_max = (T * TOP_K + nt * E * (PR - 1)) // PR
    n_chunks_max = n_pieces_max // PPC + E
    pa, ce, gn, nch = _plan(cnt[:, :, 0].T, off[:, :, 0].T, n_chunks_max)
    last_al = (cnt[:, E - 1, 0] + SEG_ALIGN - 1) // SEG_ALIGN * SEG_ALIGN
    tot = jnp.concatenate([off[:, E - 1, 0] + last_al, jnp.zeros((1,), I32)])
    y = _experts(ce, gn, pa, nch, tot, xs.reshape(nt * TS, XW), w_exp_gu, w_exp_down)
    return _combine(tot, y, pos, cnt, off, x2, w_sh_gu, w_sh_down, ln_g, ln_b)


def kernel(x, mem, positions, w_in, q_norm_g, kv_norm_g, w_uq, w_ukv, w_mla_o, s5_a_re, s5_a_im, s5_log_dt, s5_b_re, s5_b_im, s5_c_re, s5_c_im, s5_d, w_s5_glu, w_out, ln1_g, ln1_b, mem_ln_g, mem_ln_b, w_xq, w_xkv, w_xo, ln2_g, ln2_b, w_router, router_bias, w_exp_gu, w_exp_down, w_sh_gu, w_sh_down, ln3_g, ln3_b):
    B, S, D = x.shape
    assert D == D_MODEL and S % TQ == 0 and (B * S) % TM == 0 and S % S5_CHUNK == 0
    xc = x.reshape(B * S, D)
    for l in range(w_in.shape[0]):
        tabs = _rope_tables(positions)
        q, k, v, u, gm, gs = _inproj(xc, tabs, w_in[l], q_norm_g[l], kv_norm_g[l], w_uq[l], w_ukv[l])
        o = _attention(q, k, v, B, S)
        ys = _s5_branch(u, B, S, (s5_a_re[l], s5_a_im[l], s5_log_dt[l], s5_b_re[l], s5_b_im[l],
                                  s5_c_re[l], s5_c_im[l], s5_d[l]))
        x2 = _merge_cross_attention(xc, o, ys, gm, gs, mem, B, S, w_mla_o[l], w_s5_glu[l], w_out[l],
                                    ln1_g[l], ln1_b[l], mem_ln_g[l], mem_ln_b[l], w_xq[l], w_xkv[l], w_xo[l],
                                    ln2_g[l], ln2_b[l])
        xc = _moe(x2, w_router[l], router_bias[l], w_exp_gu[l], w_exp_down[l], w_sh_gu[l],
                  w_sh_down[l], ln3_g[l], ln3_b[l])
    return xc.reshape(B, S, D)
```

```python
import functools
import math

import jax
import jax.numpy as jnp
import numpy as np
from jax import lax
from jax.experimental import pallas as pl
from jax.experimental.pallas import tpu as pltpu

F32 = jnp.float32
BF16 = jnp.bfloat16
I32 = jnp.int32

D_MODEL = 1024
MLA_HEADS = 8
QK_NOPE = 64
QK_ROPE = 32
V_HEAD = 64
Q_LORA = 256
KV_LORA = 256
ROPE_THETA = 10000.0
S5_GROUP_CH = 16
S5_WIDTH = 512
S5_GROUPS = 32
S5_STATE = 64
XATTN_HEADS = 4
XATTN_HEAD_DIM = 128
N_EXPERTS = 64
TOP_K = 8
N_EXPERT_GROUPS = 8
TOPK_GROUPS = 4
EXPERT_FF = 256
SHARED_FF = 256
ROUTED_SCALE = 2.5
LN_EPS = 1e-5
RMS_EPS = 1e-6
DEPTH = 1
DEEPNORM_ALPHA = (2.0 * DEPTH) ** 0.25

LANES = 128
HEAD_PAD = 128
ROPE_LO = QK_NOPE
ROPE_HALF = QK_ROPE // 2

TM = 512
TQ = 1024
KV_GROUPS = (2, 1)
S5_CHUNK = 16
TW = 512
OH_ROWS = 2304
PR = 16
PAD_ROWS = PR
SEG_ALIGN = 8
SORT_ROWS = TOP_K * TW + N_EXPERTS * SEG_ALIGN
TS = SORT_ROWS + PAD_ROWS
XW = D_MODEL // 2 + LANES
CR = 512
PPC = CR // PR
GATHER_AHEAD = 3
N_SLOTS = GATHER_AHEAD + 1
POS_RADIX = 64
POS_NONE_Q = 127
assert SORT_ROWS <= POS_RADIX * POS_NONE_Q and SORT_ROWS % OH_ROWS == 0
VMEM_LIMIT = 48 * 1024 * 1024


def _cparams(*sem):
    return pltpu.CompilerParams(dimension_semantics=sem, vmem_limit_bytes=VMEM_LIMIT)


def _dot(a, b):
    return jnp.dot(a, b, preferred_element_type=F32)


def _dot_nt(a, b, precision=None):
    return lax.dot_general(a, b, (((1,), (1,)), ((), ())), preferred_element_type=F32,
                           precision=precision)


def _dot_exact(a, b):
    return jnp.dot(a, b, preferred_element_type=F32, precision=lax.Precision.HIGHEST)


def _layer_norm(h, g, b):
    mu = jnp.mean(h, axis=-1, keepdims=True)
    c = h - mu
    var = jnp.mean(c * c, axis=-1, keepdims=True)
    return c * lax.rsqrt(var + LN_EPS) * g + b


def _rms_norm(h, g):
    return h * lax.rsqrt(jnp.mean(h * h, axis=-1, keepdims=True) + RMS_EPS) * g


def _full(shape):
    n = len(shape)
    return pl.BlockSpec(shape, lambda *_: (0,) * n)


def _rope_angle_kernel(pos_ref, invf_ref, cos_ref, sin_ref):
    ang = pos_ref[...].astype(F32) * invf_ref[...]
    cos_ref[...] = jnp.cos(ang)
    sin_ref[...] = jnp.sin(ang)


def _rope_tables(positions):
    T = positions.size
    rows = T * ROPE_HALF // LANES
    pos_rep = jnp.repeat(positions.reshape(T), ROPE_HALF).reshape(rows, LANES)
    inv_freq = ROPE_THETA ** (-jnp.arange(0, QK_ROPE, 2, dtype=F32) / QK_ROPE)
    invf = jnp.tile(inv_freq, LANES // ROPE_HALF).reshape(1, LANES)
    cos, sin = pl.pallas_call(
        _rope_angle_kernel,
        out_shape=(jax.ShapeDtypeStruct((rows, LANES), F32),) * 2,
        name="rope_angles",
    )(pos_rep, invf)
    cos_sin = jnp.concatenate([cos.reshape(T, ROPE_HALF), sin.reshape(T, ROPE_HALF)], axis=1)
    f = np.arange(ROPE_HALF)
    spread = np.zeros((QK_ROPE, 3 * HEAD_PAD), np.float32)
    spread[f, ROPE_LO + f] = 1.0
    spread[f, ROPE_LO + ROPE_HALF + f] = 1.0
    spread[ROPE_HALF + f, HEAD_PAD + ROPE_LO + f] = -1.0
    spread[ROPE_HALF + f, 2 * HEAD_PAD + ROPE_LO + ROPE_HALF + f] = 1.0
    outside = np.ones((1, HEAD_PAD), np.float32)
    outside[0, ROPE_LO:ROPE_LO + QK_ROPE] = 0.0
    return cos_sin, jnp.asarray(spread), jnp.asarray(outside)


def _rope(x, c_tab, s_up, s_dn):
    return (x * c_tab + pltpu.roll(x, HEAD_PAD - ROPE_HALF, axis=1) * s_up
            + pltpu.roll(x, ROPE_HALF, axis=1) * s_dn)


def _inproj_kernel(x_ref, cs_ref, spread_ref, outside_ref, wlat_ref, wu_ref, wgm_ref, wgs_ref, qg_ref, kvg_ref,
                   wuq_ref, wuk_ref, wuv_ref, q_ref, k_ref, v_ref, u_ref, gm_ref, gs_ref):
    xb = x_ref[...].astype(BF16)
    lat = _dot(xb, wlat_ref[...])
    qn = _rms_norm(lat[:, :Q_LORA], qg_ref[...]).astype(BF16)
    kvn = _rms_norm(lat[:, Q_LORA:Q_LORA + KV_LORA], kvg_ref[...]).astype(BF16)
    tabs = _dot_exact(cs_ref[...], spread_ref[...])
    c_tab = tabs[:, :HEAD_PAD] + outside_ref[...]
    s_up, s_dn = tabs[:, HEAD_PAD:2 * HEAD_PAD], tabs[:, 2 * HEAD_PAD:]
    k_rope = _rope(lat[:, Q_LORA + KV_LORA:], c_tab, s_up, s_dn)
    q = _dot(qn, wuq_ref[...])
    k = _dot(kvn, wuk_ref[...])
    scale = (QK_NOPE + QK_ROPE) ** -0.5 * math.log2(math.e)
    for h in range(MLA_HEADS):
        sl = slice(h * HEAD_PAD, (h + 1) * HEAD_PAD)
        q_ref[:, sl] = (_rope(q[:, sl], c_tab, s_up, s_dn) * scale).astype(BF16)
        k_ref[:, sl] = (k[:, sl] + k_rope).astype(BF16)
    v = _dot(kvn, wuv_ref[...])
    ones_lane = lax.broadcasted_iota(I32, (1, v.shape[1]), 1) % HEAD_PAD == V_HEAD
    v_ref[...] = jnp.where(ones_lane, 1.0, v).astype(BF16)
    u_ref[...] = _dot(xb, wu_ref[...])
    gm_ref[...] = _dot(xb, wgm_ref[...]).astype(BF16)
    gs_ref[...] = _dot(xb, wgs_ref[...]).astype(BF16)


def _pad_heads(w, head_w, lo_w):
    K = w.shape[0]
    w = w.reshape(K, MLA_HEADS, head_w)[:, :, :lo_w]
    w = jnp.pad(w, ((0, 0), (0, 0), (0, HEAD_PAD - lo_w)))
    return w.reshape(K, MLA_HEADS * HEAD_PAD)


def _inproj(x2d, tabs, w_in, q_norm_g, kv_norm_g, w_uq, w_ukv):
    T = x2d.shape[0]
    o_rope = Q_LORA + KV_LORA
    o_u = o_rope + QK_ROPE
    o_gm = o_u + S5_WIDTH
    o_gs = o_gm + D_MODEL
    w_rope = jnp.pad(w_in[:, o_rope:o_u], ((0, 0), (ROPE_LO, HEAD_PAD - ROPE_LO - QK_ROPE)))
    w_lat = jnp.concatenate([w_in[:, :o_rope], w_rope], axis=1).astype(BF16)
    w_u = w_in[:, o_u:o_gm].astype(BF16)
    w_gm = w_in[:, o_gm:o_gs].astype(BF16)
    w_gs = w_in[:, o_gs:].astype(BF16)
    wuq = _pad_heads(w_uq, QK_NOPE + QK_ROPE, QK_NOPE + QK_ROPE).astype(BF16)
    kv3 = w_ukv.reshape(KV_LORA, MLA_HEADS, QK_NOPE + V_HEAD)
    wuk = _pad_heads(kv3[:, :, :QK_NOPE].reshape(KV_LORA, -1), QK_NOPE, QK_NOPE).astype(BF16)
    wuv = _pad_heads(kv3[:, :, QK_NOPE:].reshape(KV_LORA, -1), V_HEAD, V_HEAD).astype(BF16)
    HP = MLA_HEADS * HEAD_PAD
    tile = lambda w: pl.BlockSpec((TM, w), lambda i: (i, 0))
    return pl.pallas_call(
        _inproj_kernel,
        grid=(T // TM,),
        in_specs=[tile(D_MODEL), tile(QK_ROPE), _full((QK_ROPE, 3 * HEAD_PAD)), _full((1, HEAD_PAD)),
                  _full(w_lat.shape), _full(w_u.shape), _full(w_gm.shape), _full(w_gs.shape),
                  _full((1, Q_LORA)), _full((1, KV_LORA)),
                  _full(wuq.shape), _full(wuk.shape), _full(wuv.shape)],
        out_specs=[tile(HP), tile(HP), tile(HP), tile(S5_WIDTH), tile(D_MODEL), tile(D_MODEL)],
        out_shape=[jax.ShapeDtypeStruct((T, HP), BF16)] * 3
        + [jax.ShapeDtypeStruct((T, S5_WIDTH), F32)]
        + [jax.ShapeDtypeStruct((T, D_MODEL), BF16)] * 2,
        compiler_params=_cparams("parallel"),
        name="inproj_mla_prep",
    )(x2d, *tabs, w_lat, w_u, w_gm, w_gs, q_norm_g.reshape(1, -1), kv_norm_g.reshape(1, -1),
      wuq, wuk, wuv)


def _attn_kernel(q_ref, k_ref, v_ref, o_ref):
    qi = pl.program_id(2)
    q = q_ref[0]

    def update(qq, k, v, carry, row_offset=None):
        m, acc = carry
        s = _dot_nt(qq, k)
        if row_offset is not None:
            row = lax.broadcasted_iota(I32, s.shape, 0) + row_offset
            col = lax.broadcasted_iota(I32, s.shape, 1)
            s = jnp.where(col <= row, s, -jnp.inf)
        m_new = jnp.maximum(m, jnp.max(s, axis=-1, keepdims=True))
        p = jnp.exp2(s - m_new)
        acc = jnp.exp2(m - m_new) * acc + _dot(p.astype(BF16), v)
        return m_new, acc

    def step(blk, n_blk, carry):
        tk = n_blk * TQ
        start = pl.multiple_of(blk * TQ, TQ)
        return update(q, k_ref[0, pl.ds(start, tk), :], v_ref[0, pl.ds(start, tk), :], carry)

    carry = (jnp.full((TQ, 1), -jnp.inf, F32), jnp.zeros((TQ, HEAD_PAD), F32))
    done = 0
    for n_blk in KV_GROUPS:
        base = done
        n_it = (qi - base) // n_blk
        carry = lax.fori_loop(0, n_it, lambda j, c, base=base, n_blk=n_blk: step(base + j * n_blk, n_blk, c), carry)
        done = base + n_it * n_blk
    hq = TQ // 2
    d0 = pl.multiple_of(qi * TQ, TQ)
    m, acc = carry
    _, acc_lo = update(q[:hq], k_ref[0, pl.ds(d0, hq), :], v_ref[0, pl.ds(d0, hq), :], (m[:hq], acc[:hq]), 0)
    _, acc_hi = update(q[hq:], k_ref[0, pl.ds(d0, TQ), :], v_ref[0, pl.ds(d0, TQ), :], (m[hq:], acc[hq:]), hq)
    o_ref[0, :hq, :] = (acc_lo / acc_lo[:, V_HEAD:V_HEAD + 1]).astype(BF16)
    o_ref[0, hq:, :] = (acc_hi / acc_hi[:, V_HEAD:V_HEAD + 1]).astype(BF16)


def _attention(q, k, v, B, S):
    HP = MLA_HEADS * HEAD_PAD
    q, k, v = (a.reshape(B, S, HP) for a in (q, k, v))
    o = pl.pallas_call(
        _attn_kernel,
        grid=(B, MLA_HEADS, S // TQ),
        in_specs=[pl.BlockSpec((1, TQ, HEAD_PAD), lambda b, h, i: (b, i, h)),
                  pl.BlockSpec((1, S, HEAD_PAD), lambda b, h, i: (b, 0, h)),
                  pl.BlockSpec((1, S, HEAD_PAD), lambda b, h, i: (b, 0, h))],
        out_specs=pl.BlockSpec((1, TQ, HEAD_PAD), lambda b, h, i: (b, i, h)),
        out_shape=jax.ShapeDtypeStruct((B, S, HP), BF16),
        compiler_params=_cparams("parallel", "parallel", "arbitrary"),
        name="mla_flash_attention",
    )(q, k, v)
    return o.reshape(B * S, HP)


CH2 = S5_CHUNK * S5_GROUP_CH
P2 = 2 * S5_STATE
SCAN_LEVELS_MAX = 16


S5_PREP_GROUPS = 4


def _s5_prep_kernel(n_levels, *refs):
    *io_refs, mt_acc = refs
    for k in range(S5_PREP_GROUPS):
        _s5_prep_group(n_levels, *(r.at[pl.ds(k, 1)] for r in io_refs), mt_acc)


def _s5_prep_group(n_levels, arow_ref, acol_ref, bt_ref, ct_ref, d_ref,
                   mt_ref, wt_ref, vt_ref, apow_ref, dvec_ref, mt_acc):
    P, H, L = S5_STATE, S5_GROUP_CH, S5_CHUNK

    def powers(ar, ai, dt, n):
        e = jnp.exp(n * (ar * dt))
        ang = n * (ai * dt)
        return e * jnp.cos(ang), e * jnp.sin(ang)

    def zoh_coef(ar, ai, dt):
        pr, pi = powers(ar, ai, dt, 1.0)
        nr, ni = pr - 1.0, pi
        den = ar * ar + ai * ai
        return (nr * ar + ni * ai) / den, (ni * ar - nr * ai) / den

    row = arow_ref[0]
    ar, ai, dt = row[0:1], row[1:2], jnp.exp(row[2:3])
    cr, ci = zoh_coef(ar, ai, dt)
    bt = bt_ref[0]
    btr = jnp.concatenate([bt[:H], bt[:H]], axis=1)
    bti = jnp.concatenate([bt[H:], bt[H:]], axis=1)
    bbr = cr * btr - ci * bti
    bbi = cr * bti + ci * btr
    lane = lax.broadcasted_iota(I32, (1, P2), 1)
    is_re = lane < P
    ridx = lax.broadcasted_iota(I32, (CH2, P2), 0) // H
    pwr, pwi = powers(ar, ai, dt, (L - 1 - ridx).astype(F32))
    rsel = (lax.broadcasted_iota(I32, (CH2, H), 0) % H == lax.broadcasted_iota(I32, (CH2, H), 1)).astype(F32)
    bbr_t = _dot_exact(rsel, bbr)
    bbi_t = _dot_exact(rsel, bbi)
    w_re = pwr * bbr_t - pwi * bbi_t
    w_im = pwr * bbi_t + pwi * bbr_t
    wt_ref[0] = jnp.where(is_re, w_re, w_im).astype(BF16)
    a_r, a_i = powers(ar, ai, dt, float(L))
    sign = jnp.where(is_re, -1.0, 1.0)
    for lvl in range(SCAN_LEVELS_MAX):
        if lvl < n_levels:
            apow_ref[0, lvl:lvl + 1, :] = a_r
            apow_ref[0, SCAN_LEVELS_MAX + lvl:SCAN_LEVELS_MAX + lvl + 1, :] = sign * a_i
            a_r, a_i = a_r * a_r - a_i * a_i, 2.0 * a_r * a_i
        else:
            apow_ref[0, lvl:lvl + 1, :] = jnp.zeros_like(a_r)
            apow_ref[0, SCAN_LEVELS_MAX + lvl:SCAN_LEVELS_MAX + lvl + 1, :] = jnp.zeros_like(a_r)

    col = acol_ref[0]
    arc, aic, dtc = col[:, 0:1], col[:, 1:2], jnp.exp(col[:, 2:3])
    ct = ct_ref[0]
    csel = (lax.broadcasted_iota(I32, (H, CH2), 1) % H == lax.broadcasted_iota(I32, (H, CH2), 0)).astype(F32)
    ctr = _dot_exact(ct[:, :H], csel)
    cti = _dot_exact(ct[:, H:], csel)
    sidx = (lax.broadcasted_iota(I32, (P, CH2), 1) // H).astype(F32)
    pr, pi = powers(arc, aic, dtc, sidx)
    g_r = ctr * pr - cti * pi
    g_i = ctr * pi + cti * pr
    a1r, a1i = powers(arc, aic, dtc, 1.0)
    v_r = g_r * a1r - g_i * a1i
    v_i = g_r * a1i + g_i * a1r
    vt_ref[0, :P, :] = v_r.astype(BF16)
    vt_ref[0, P:, :] = (-v_i).astype(BF16)
    bbr64, bbi64 = bbr[:, :P], bbi[:, :P]
    kt = _dot_exact(bbr64, g_r) - _dot_exact(bbi64, g_i)
    lane2 = lax.broadcasted_iota(I32, (H, CH2), 1)
    for r in range(L):
        blk = kt if r == 0 else jnp.where(lane2 >= r * H, pltpu.roll(kt, r * H, axis=1), 0.0)
        mt_acc[r * H:(r + 1) * H, :] = blk
    mt_ref[0] = mt_acc[...].astype(BF16)
    dvec_ref[0] = _dot_exact(d_ref[0], csel)


def _s5_prep(n_levels, s5_a_re, s5_a_im, s5_log_dt, s5_b_re, s5_b_im, s5_c_re, s5_c_im, s5_d):
    G, P, H = S5_GROUPS, S5_STATE, S5_GROUP_CH
    ldt = jnp.broadcast_to(s5_log_dt[:, None], (G, P))
    arow = jnp.stack([s5_a_re, s5_a_im, ldt], axis=1)
    arow = jnp.concatenate([arow, arow], axis=2)
    acol = jnp.stack([s5_a_re, s5_a_im, ldt], axis=2)
    bt = jnp.concatenate([s5_b_re.transpose(0, 2, 1), s5_b_im.transpose(0, 2, 1)], axis=1)
    ct = jnp.concatenate([s5_c_re.transpose(0, 2, 1), s5_c_im.transpose(0, 2, 1)], axis=2)
    d = s5_d.reshape(G, 1, H)
    blk = lambda *s: pl.BlockSpec((S5_PREP_GROUPS,) + s, lambda g: (g, 0, 0))
    return pl.pallas_call(
        functools.partial(_s5_prep_kernel, n_levels),
        grid=(G // S5_PREP_GROUPS,),
        in_specs=[blk(3, P2), blk(P, 3), blk(2 * H, P), blk(P, 2 * H), blk(1, H)],
        out_specs=[blk(CH2, CH2), blk(CH2, P2), blk(P2, CH2), blk(2 * SCAN_LEVELS_MAX, P2), blk(1, CH2)],
        out_shape=[jax.ShapeDtypeStruct((G, CH2, CH2), BF16),
                   jax.ShapeDtypeStruct((G, CH2, P2), BF16),
                   jax.ShapeDtypeStruct((G, P2, CH2), BF16),
                   jax.ShapeDtypeStruct((G, 2 * SCAN_LEVELS_MAX, P2), F32),
                   jax.ShapeDtypeStruct((G, 1, CH2), F32)],
        scratch_shapes=[pltpu.VMEM((CH2, CH2), F32)],
        compiler_params=_cparams("parallel"),
        name="s5_discretise",
    )(arow, acol, bt, ct, d)


def _gelu_tanh(y):
    return 0.5 * y * (1.0 + jnp.tanh(math.sqrt(2.0 / math.pi) * (y + 0.044715 * (y * y * y))))


GROUPS_PER_TILE = LANES // S5_GROUP_CH
STEPS_PER_TILE = LANES // S5_GROUP_CH


def _s5_scan_kernel(n_levels, u_ref, mt_ref, wt_ref, vt_ref, apow_ref, dvec_ref, y_ref):
    H, L = S5_GROUP_CH, S5_CHUNK
    nc = u_ref.shape[0] // L
    lane_blk = lax.broadcasted_iota(I32, (nc, LANES), 1) // H
    ridx = lax.broadcasted_iota(I32, (nc, P2), 0)
    steps = [u_ref[pl.ds(l, nc, stride=L), :] for l in range(L)]
    groups = range(GROUPS_PER_TILE)
    us, xs = [], []
    for gi in groups:
        halves = []
        for hh in range(L // STEPS_PER_TILE):
            acc = jnp.zeros((nc, LANES), F32)
            for l8 in range(STEPS_PER_TILE):
                src = steps[hh * STEPS_PER_TILE + l8]
                shift = ((l8 - gi) % STEPS_PER_TILE) * H
                moved = src if shift == 0 else pltpu.roll(src, shift, axis=1)
                acc = jnp.where(lane_blk == l8, moved, acc)
            halves.append(acc)
        u = jnp.concatenate(halves, axis=1)
        us.append(u)
        xs.append(_dot(u.astype(BF16), wt_ref[gi]))
    for lvl in range(n_levels):
        sh = 1 << lvl
        for gi in groups:
            apow = apow_ref[gi]
            prev = jnp.where(ridx >= sh, pltpu.roll(xs[gi], sh, axis=0), 0.0)
            a_r = apow[lvl:lvl + 1]
            a_i = apow[SCAN_LEVELS_MAX + lvl:SCAN_LEVELS_MAX + lvl + 1]
            xs[gi] = xs[gi] + a_r * prev + a_i * pltpu.roll(prev, S5_STATE, axis=1)
    outs = [jnp.zeros((nc, LANES), F32) for _ in range(L)]
    for gi in groups:
        x_in = jnp.where(ridx >= 1, pltpu.roll(xs[gi], 1, axis=0), 0.0)
        u = us[gi]
        y = _gelu_tanh(_dot(u.astype(BF16), mt_ref[gi]) + _dot(x_in.astype(BF16), vt_ref[gi]) + dvec_ref[gi] * u)
        for l in range(L):
            hh, l8 = divmod(l, STEPS_PER_TILE)
            src = y[:, hh * LANES:(hh + 1) * LANES]
            shift = ((gi - l8) % STEPS_PER_TILE) * H
            moved = src if shift == 0 else pltpu.roll(src, shift, axis=1)
            outs[l] = jnp.where(lane_blk == gi, moved, outs[l])
    for l in range(L):
        y_ref[pl.ds(l, nc, stride=L), :] = outs[l]


def _s5_branch(u, B, S, s5_params):
    G, L = S5_GROUPS, S5_CHUNK
    nc = S // L
    n_levels = max(1, (nc - 1).bit_length())
    assert n_levels <= SCAN_LEVELS_MAX and S5_CHUNK % STEPS_PER_TILE == 0
    mt, wt, vt, apow, dvec = _s5_prep(n_levels, *s5_params)
    n_lt = G // GROUPS_PER_TILE
    blk = lambda *s: pl.BlockSpec((GROUPS_PER_TILE,) + s, lambda b, t: (t, 0, 0))
    io = pl.BlockSpec((S, LANES), lambda b, t: (b, t))
    return pl.pallas_call(
        functools.partial(_s5_scan_kernel, n_levels),
        grid=(B, n_lt),
        in_specs=[io, blk(CH2, CH2), blk(CH2, P2), blk(P2, CH2), blk(2 * SCAN_LEVELS_MAX, P2), blk(1, CH2)],
        out_specs=io,
        out_shape=jax.ShapeDtypeStruct((B * S, S5_WIDTH), F32),
        compiler_params=_cparams("parallel", "parallel"),
        name="s5_chunk_scan",
    )(u, mt, wt, vt, apow, dvec)


def _merge_xattn_kernel(x_ref, o_ref, ys_ref, gm_ref, gs_ref, wmo_ref, wglu_ref, wout_ref, g1_ref, b1_ref,
                        k_ref, v_ref, wq_ref, wo_ref, g2_ref, b2_ref, x2_ref):
    o32 = pltpu.bitcast(o_ref[0], I32)
    first_half = lax.broadcasted_iota(I32, (o32.shape[0], HEAD_PAD), 1) < V_HEAD
    pairs = [jnp.where(first_half, o32[:, (2 * p) * HEAD_PAD:(2 * p + 1) * HEAD_PAD],
                       pltpu.roll(o32[:, (2 * p + 1) * HEAD_PAD:(2 * p + 2) * HEAD_PAD], V_HEAD, axis=1))
             for p in range(MLA_HEADS // 2)]
    o_packed = pltpu.bitcast(jnp.concatenate(pairs, axis=1), BF16)
    y_mla = _dot(o_packed, wmo_ref[...])
    z = _dot(ys_ref[0].astype(BF16), wglu_ref[...])
    y_s5 = z[:, :D_MODEL] * jax.nn.sigmoid(z[:, D_MODEL:])
    merged = (jax.nn.sigmoid(gm_ref[0].astype(F32)) * y_mla
              + jax.nn.sigmoid(gs_ref[0].astype(F32)) * y_s5)
    mix = _dot(merged.astype(BF16), wout_ref[...])
    x1 = _layer_norm(DEEPNORM_ALPHA * x_ref[0] + mix, g1_ref[...], b1_ref[...])
    q = (_dot(x1.astype(BF16), wq_ref[...]) * (XATTN_HEAD_DIM ** -0.5)).astype(BF16)
    k = k_ref[0]
    v = v_ref[0]
    outs = []
    for h in range(XATTN_HEADS):
        sl = slice(h * XATTN_HEAD_DIM, (h + 1) * XATTN_HEAD_DIM)
        s = _dot_nt(q[:, sl], k[:, sl])
        p = jnp.exp(s - jnp.max(s, axis=-1, keepdims=True))
        o = _dot(p.astype(BF16), v[:, sl]) / jnp.sum(p, axis=-1, keepdims=True)
        outs.append(o.astype(BF16))
    xa = _dot(jnp.concatenate(outs, axis=1), wo_ref[...])
    x2_ref[0] = _layer_norm(DEEPNORM_ALPHA * x1 + xa, g2_ref[...], b2_ref[...])


def _memkv_kernel(mem_ref, g_ref, b_ref, w_ref, k_ref, v_ref):
    m = _layer_norm(mem_ref[0], g_ref[...], b_ref[...]).astype(BF16)
    kv = _dot(m, w_ref[...])
    hd = XATTN_HEADS * XATTN_HEAD_DIM
    k_ref[0] = kv[:, :hd].astype(BF16)
    v_ref[0] = kv[:, hd:].astype(BF16)


def _merge_cross_attention(x2d, o, ys, gm, gs, mem, B, S, w_mla_o, w_s5_glu, w_out, ln1_g, ln1_b,
                           mem_ln_g, mem_ln_b, w_xq, w_xkv, w_xo, ln2_g, ln2_b):
    M = mem.shape[1]
    hd = XATTN_HEADS * XATTN_HEAD_DIM
    HP = MLA_HEADS * HEAD_PAD
    row = lambda a: a.reshape(1, -1)
    wmo = w_mla_o.astype(BF16)
    k, v = pl.pallas_call(
        _memkv_kernel,
        grid=(B,),
        in_specs=[pl.BlockSpec((1, M, D_MODEL), lambda b: (b, 0, 0)), _full((1, D_MODEL)), _full((1, D_MODEL)),
                  _full((D_MODEL, 2 * hd))],
        out_specs=[pl.BlockSpec((1, M, hd), lambda b: (b, 0, 0))] * 2,
        out_shape=[jax.ShapeDtypeStruct((B, M, hd), BF16)] * 2,
        compiler_params=_cparams("parallel"),
        name="memory_kv",
    )(mem, row(mem_ln_g), row(mem_ln_b), w_xkv.astype(BF16))
    tile = lambda w: pl.BlockSpec((1, TM, w), lambda b, i: (b, i, 0))
    kv_spec = pl.BlockSpec((1, M, hd), lambda b, i: (b, 0, 0))
    r3 = lambda a: a.reshape(B, S, a.shape[-1])
    x2 = pl.pallas_call(
        _merge_xattn_kernel,
        grid=(B, S // TM),
        in_specs=[tile(D_MODEL), tile(HP), tile(S5_WIDTH), tile(D_MODEL), tile(D_MODEL),
                  _full((MLA_HEADS * V_HEAD, D_MODEL)), _full((S5_WIDTH, 2 * D_MODEL)), _full((D_MODEL, D_MODEL)),
                  _full((1, D_MODEL)), _full((1, D_MODEL)),
                  kv_spec, kv_spec, _full((D_MODEL, hd)), _full((hd, D_MODEL)), _full((1, D_MODEL)), _full((1, D_MODEL))],
        out_specs=tile(D_MODEL),
        out_shape=jax.ShapeDtypeStruct((B, S, D_MODEL), F32),
        compiler_params=_cparams("parallel", "parallel"),
        name="merge_ln1_xattn_ln2",
    )(r3(x2d), r3(o), r3(ys), r3(gm), r3(gs), wmo, w_s5_glu.astype(BF16), w_out.astype(BF16), row(ln1_g), row(ln1_b),
      k, v, w_xq.astype(BF16), w_xo.astype(BF16), row(ln2_g), row(ln2_b))
    return x2.reshape(B * S, D_MODEL)


def _route_sort_kernel(x_ref, w_ref, bias_ref, xs_ref, pos_ref, cnt_ref, off_ref):
    E, per = N_EXPERTS, N_EXPERTS // N_EXPERT_GROUPS
    tm = x_ref.shape[0]
    x = x_ref[...]
    xb = x.astype(BF16)
    x_lo = (x - xb.astype(F32)).astype(BF16)
    w = w_ref[...]
    wb = w.astype(BF16)
    w_lo = (w - wb.astype(F32)).astype(BF16)
    logits = _dot_nt(wb, xb) + (_dot_nt(wb, x_lo) + _dot_nt(w_lo, xb))
    scores = jax.nn.sigmoid(logits)
    sel = scores + bias_ref[...]
    neg = -jnp.inf
    i8 = lax.broadcasted_iota(I32, (per, tm), 0)
    gscore = []
    for g in range(N_EXPERT_GROUPS):
        blk = sel[g * per:(g + 1) * per]
        m1 = jnp.max(blk, axis=0, keepdims=True)
        i1 = jnp.min(jnp.where(blk == m1, i8, per), axis=0, keepdims=True)
        m2 = jnp.max(jnp.where(i8 == i1, neg, blk), axis=0, keepdims=True)
        gscore.append(m1 + m2)
    blocks = []
    for g in range(N_EXPERT_GROUPS):
        ahead = jnp.zeros((1, tm), I32)
        for o in range(N_EXPERT_GROUPS):
            if o == g:
                continue
            before = (gscore[o] >= gscore[g]) if o < g else (gscore[o] > gscore[g])
            ahead = ahead + before.astype(I32)
        blocks.append(jnp.where(ahead < TOPK_GROUPS, sel[g * per:(g + 1) * per], neg))
    cur = jnp.concatenate(blocks, axis=0)
    ie = lax.broadcasted_iota(I32, (E, tm), 0)
    picked = jnp.zeros((E, tm), F32)
    for _ in range(TOP_K):
        m = jnp.max(cur, axis=0, keepdims=True)
        idx = jnp.min(jnp.where(cur == m, ie, E), axis=0, keepdims=True)
        hit = ie == idx
        picked = jnp.where(hit, 1.0, picked)
        cur = jnp.where(hit, neg, cur)
    wsel = scores * picked
    wnorm = wsel / jnp.sum(wsel, axis=0, keepdims=True) * ROUTED_SCALE
    pb = picked.astype(BF16)
    tri = (lax.broadcasted_iota(I32, (tm, tm), 0) <= lax.broadcasted_iota(I32, (tm, tm), 1)).astype(BF16)
    incl = _dot(pb, tri)
    cnt_col = jnp.sum(picked, axis=1, keepdims=True)
    lower = (lax.broadcasted_iota(I32, (E, E), 1) < lax.broadcasted_iota(I32, (E, E), 0)).astype(F32)
    al_col = jnp.floor((cnt_col + (SEG_ALIGN - 1.0)) * (1.0 / SEG_ALIGN)) * SEG_ALIGN
    off_col = _dot_exact(lower, jnp.broadcast_to(al_col, (E, LANES)))[:, 0:1]
    pos = off_col + incl - 1.0
    q = jnp.floor(pos * (1.0 / POS_RADIX))
    pq = jnp.concatenate([jnp.where(picked > 0.0, POS_RADIX * q, POS_RADIX * POS_NONE_Q),
                          jnp.where(picked > 0.0, pos - POS_RADIX * q, 0.0)], axis=0)
    pos_ref[...] = pq
    pqb = pq.astype(BF16)
    cnt_ref[0] = jnp.broadcast_to(cnt_col, (E, LANES)).astype(I32)
    off_ref[0] = jnp.broadcast_to(off_col, (E, LANES)).astype(I32)
    cnt_row = _dot_nt(jnp.ones((8, tm), BF16), pb)
    al_row = jnp.floor((cnt_row + (SEG_ALIGN - 1.0)) * (1.0 / SEG_ALIGN)) * SEG_ALIGN
    er = lax.broadcasted_iota(I32, (E, 4 * E), 0)
    ec = lax.broadcasted_iota(I32, (E, 4 * E), 1) % E
    off_row4 = _dot_exact(al_row, (er < ec).astype(F32))[0:1]
    end_row4 = off_row4 + _dot_exact(al_row, (er == ec).astype(F32))[0:1]
    w_hi = wnorm.astype(BF16).astype(F32)
    w_r1 = wnorm - w_hi
    w_mid = w_r1.astype(BF16).astype(F32)
    w_lo = w_r1 - w_mid
    w4_t = jnp.concatenate([w_hi, w_mid, w_lo, jnp.zeros_like(w_lo)], axis=0).T
    xw = jnp.concatenate([xb, w4_t.astype(BF16)], axis=1)
    half = D_MODEL // 2

    def build(c, carry):
        r0 = pl.multiple_of(c * OH_ROWS, OH_ROWS)
        jr = (lax.broadcasted_iota(I32, (OH_ROWS, 4 * E), 0) + r0).astype(F32)
        member4 = jnp.where(jr >= off_row4, jnp.where(jr < end_row4, 1.0, 0.0), 0.0)
        target = _dot(member4[:, :2 * E].astype(BF16), pqb)
        j = (lax.broadcasted_iota(I32, (OH_ROWS, tm), 0) + r0).astype(F32)
        ohb = jnp.where(target == j, 1.0, 0.0).astype(BF16)
        rows = _dot(ohb, xw)
        xs_ref[0, pl.ds(r0, OH_ROWS), 0:half] = _pack_bf16_pair(rows[:, :half], rows[:, half:D_MODEL])
        wrow = jnp.sum(member4 * rows[:, D_MODEL:], axis=1, keepdims=True)
        xs_ref[0, pl.ds(r0, OH_ROWS), half:] = pltpu.bitcast(jnp.broadcast_to(wrow, (OH_ROWS, LANES)), I32)
        return carry

    lax.fori_loop(0, SORT_ROWS // OH_ROWS, build, 0, unroll=True)
    xs_ref[0, SORT_ROWS:, :] = jnp.zeros((PAD_ROWS, XW), I32)


def _route_sort(x2, w_router, router_bias):
    T = x2.shape[0]
    E = N_EXPERTS
    nt = T // TW
    seg = pl.BlockSpec((1, E, LANES), lambda i: (i, 0, 0))
    return pl.pallas_call(
        _route_sort_kernel,
        grid=(nt,),
        in_specs=[pl.BlockSpec((TW, D_MODEL), lambda i: (i, 0)), _full((E, D_MODEL)), _full((E, 1))],
        out_specs=[pl.BlockSpec((1, TS, XW), lambda i: (i, 0, 0)),
                   pl.BlockSpec((2 * E, TW), lambda i: (0, i)), seg, seg],
        out_shape=[jax.ShapeDtypeStruct((nt, TS, XW), I32), jax.ShapeDtypeStruct((2 * E, T), F32),
                   jax.ShapeDtypeStruct((nt, E, LANES), I32), jax.ShapeDtypeStruct((nt, E, LANES), I32)],
        compiler_params=_cparams("parallel"),
        name="route_local_sort",
    )(x2, w_router.T, router_bias.reshape(E, 1))


def _plan_kernel(cnt_ref, off_ref, pa_ref, ce_ref, gn_ref, nch_ref):
    E, nt = cnt_ref.shape
    n_rows = pa_ref.shape[0]
    cnt = cnt_ref[...].astype(F32)
    off = off_ref[...].astype(F32)
    npc = jnp.floor((cnt + (PR - 1.0)) * (1.0 / PR))
    tri = (lax.broadcasted_iota(I32, (nt, nt), 0) <= lax.broadcasted_iota(I32, (nt, nt), 1)).astype(F32)
    p_incl = _dot_exact(npc, tri)
    total = p_incl[:, nt - 1:nt]
    n_ch = jnp.floor((total + (PPC - 1.0)) * (1.0 / PPC))
    lower = (lax.broadcasted_iota(I32, (E, E), 1) < lax.broadcasted_iota(I32, (E, E), 0)).astype(F32)
    wide = lambda col: jnp.broadcast_to(col, (E, LANES))
    g0 = _dot_exact(lower, wide(n_ch))[:, 0:1]
    n_total = jnp.sum(n_ch, axis=0, keepdims=True)
    g0_row = wide(g0).T[0:1, :E]
    g1_row = g0_row + wide(n_ch).T[0:1, :E]
    g_e = lax.broadcasted_iota(I32, (n_rows, E), 0).astype(F32)
    sel = jnp.where(g_e >= g0_row, jnp.where(g_e < g1_row, 1.0, 0.0), 0.0)
    tile_base = lax.broadcasted_iota(I32, (E, nt), 1).astype(F32) * TS
    a_mat = tile_base + off - PR * (p_incl - npc)
    p_g = _dot_exact(sel, p_incl)
    a_g = _dot_exact(sel, a_mat)
    e_idx = lax.broadcasted_iota(I32, (E, LANES), 0).astype(F32)
    pick = lambda col: _dot_exact(sel, col)[:, 0:1]
    g0_g, tot_g, e_g = pick(wide(g0)), pick(wide(total)), pick(e_idx)
    gn_g = pick(wide(g0 + n_ch))
    g_p = lax.broadcasted_iota(I32, (n_rows, PPC), 0).astype(F32)
    q = (g_p - g0_g) * PPC + lax.broadcasted_iota(I32, (n_rows, PPC), 1).astype(F32)
    addr = jnp.full((n_rows, PPC), -float(SEG_ALIGN), F32)
    lo = jnp.zeros((n_rows, 1), F32)
    for i in range(nt):
        hi = p_g[:, i:i + 1]
        addr = jnp.where(q >= lo, jnp.where(q < hi, a_g[:, i:i + 1] + PR * q, addr), addr)
        lo = hi
    pa_ref[...] = addr.astype(I32)
    g_col = lax.broadcasted_iota(I32, (n_rows, LANES), 0).astype(F32)
    ce_ref[...] = jnp.where(g_col < n_total, jnp.broadcast_to(e_g, (n_rows, LANES)), E - 1.0).astype(I32)
    gn_ref[...] = jnp.broadcast_to(gn_g, (n_rows, LANES)).astype(I32)
    nch_ref[...] = jnp.broadcast_to(n_total, nch_ref.shape).astype(I32)


def _plan(cnt, off, n_chunks_max):
    n_rows = (n_chunks_max + 7) // 8 * 8
    pa, ce, gn, nch = pl.pallas_call(
        _plan_kernel,
        out_shape=[jax.ShapeDtypeStruct((n_rows, PPC), I32), jax.ShapeDtypeStruct((n_rows, LANES), I32),
                   jax.ShapeDtypeStruct((n_rows, LANES), I32), jax.ShapeDtypeStruct((8, LANES), I32)],
        name="moe_plan",
    )(cnt, off)
    return pa.reshape(n_rows * PPC), ce[:, 0], gn[:, 0], nch[0, :1]


def _unpack_bf16_pair(w):
    lo = pltpu.bitcast(lax.shift_left(w, jnp.int32(16)), F32).astype(BF16)
    hi = pltpu.bitcast(w & jnp.int32(-65536), F32).astype(BF16)
    return lo, hi


def _pack_bf16_pair(lo, hi):
    lo_bits = lax.shift_right_logical(pltpu.bitcast(lo.astype(BF16).astype(F32), I32), jnp.int32(16))
    hi_bits = pltpu.bitcast(hi.astype(BF16).astype(F32), I32)
    return hi_bits | lo_bits


def _expert_kernel(ce_ref, gn_ref, pa_ref, nch_ref, tot_ref, xs_hbm, wgu_hbm, wdn_hbm, y_hbm,
                   xbuf, ybuf, zbuf, wgu_f, wdn_f, wgu_b, wdn_b, wslot, gsem, wsem, zsem, esem):
    g = pl.program_id(0)
    n = nch_ref[0]
    slot = g % N_SLOTS
    half = D_MODEL // 2

    dummy_base = (tot_ref.shape[0] - 1) * TS

    def gather_copy(s, a, jp):
        return pltpu.make_async_copy(xs_hbm.at[pl.ds(pl.multiple_of(a, SEG_ALIGN), PR), :],
                                     xbuf.at[s, pl.ds(jp * PR, PR), :], gsem.at[s])

    def write_copy(s, a, jp):
        return pltpu.make_async_copy(ybuf.at[s, pl.ds(jp * PR, PR), :],
                                     y_hbm.at[pl.ds(pl.multiple_of(a, SEG_ALIGN), PR), :], wsem.at[s])

    def gather_start(c, s):
        for jp in range(PPC):
            gather_copy(s, jnp.maximum(pa_ref[c * PPC + jp], 0), jp).start()

    def write_start(c, s):
        for jp in range(PPC):
            a = pa_ref[c * PPC + jp]
            write_copy(s, jnp.where(a >= 0, a, dummy_base + jp * PR), jp).start()

    def gather_wait(s):
        for jp in range(PPC):
            gather_copy(s, 0, jp).wait()

    def write_wait(s):
        for jp in range(PPC):
            write_copy(s, 0, jp).wait()

    def weight_copies(e, s):
        return (pltpu.make_async_copy(wgu_hbm.at[e], wgu_f.at[s], esem.at[s]),
                pltpu.make_async_copy(wdn_hbm.at[e], wdn_f.at[s], esem.at[s]))

    @pl.when(g == 0)
    def _():
        wslot[0] = 0
        for cp in weight_copies(ce_ref[0], 0):
            cp.start()
        gather_start(0, 0)
        ybuf[...] = jnp.zeros_like(ybuf)
        zbuf[...] = jnp.zeros_like(zbuf)

        def zero_copy(row):
            return pltpu.make_async_copy(zbuf, y_hbm.at[pl.ds(pl.multiple_of(row, SEG_ALIGN), SEG_ALIGN), :], zsem)

        def tail_of(i):
            return i * TS + tot_ref[i], (TS - tot_ref[i]) // SEG_ALIGN

        def zero_start(i, c):
            base, cnt8 = tail_of(i)
            lax.fori_loop(0, cnt8, lambda q, z: (zero_copy(base + q * SEG_ALIGN).start(), z)[1], 0)
            return c

        def zero_wait(i, c):
            base, cnt8 = tail_of(i)
            lax.fori_loop(0, cnt8, lambda q, z: (zero_copy(base + q * SEG_ALIGN).wait(), z)[1], 0)
            return c

        lax.fori_loop(0, tot_ref.shape[0], zero_start, 0)
        lax.fori_loop(0, tot_ref.shape[0], zero_wait, 0)

    for c in range(1, GATHER_AHEAD):
        @pl.when((g == 0) & (c < n))
        def _(c=c):
            gather_start(c, c)

    @pl.when(g + GATHER_AHEAD < n)
    def _():
        gather_start(g + GATHER_AHEAD, (g + GATHER_AHEAD) % N_SLOTS)

    e_cur = ce_ref[g]
    e_prev = ce_ref[jnp.maximum(g - 1, 0)]
    e_prev2 = ce_ref[jnp.maximum(g - 2, 0)]

    @pl.when(g < n)
    def _():
        @pl.when((g == 0) | (e_cur != e_prev))
        def _():
            s = wslot[0]
            for cp in weight_copies(e_cur, s):
                cp.wait()
            wgu_b[...] = wgu_f[s].astype(BF16)
            wdn_b[...] = wdn_f[s].astype(BF16)
            g_next = gn_ref[g]

            @pl.when(g_next < n)
            def _():
                for cp in weight_copies(ce_ref[g_next], 1 - s):
                    cp.start()

            wslot[0] = 1 - s

        gather_wait(slot)

        @pl.when((g >= 2) & (e_prev == e_prev2))
        def _():
            write_wait((g - 2) % N_SLOTS)

        @pl.when((g >= 1) & (e_cur != e_prev))
        def _():
            write_wait((g - 1) % N_SLOTS)

        def swiglu_rows(rows):
            xw = xbuf[slot, :rows, :]
            lo, hi = _unpack_bf16_pair(xw[:, :half])
            h = _dot(jnp.concatenate([lo, hi], axis=1), wgu_b[...])
            gate, up = h[:, :EXPERT_FF], h[:, EXPERT_FF:]
            act = (gate * jax.nn.sigmoid(gate) * up).astype(BF16)
            y = _dot(act, wdn_b[...])
            w_row = pltpu.bitcast(xw[:, half:], F32)
            y = y * jnp.concatenate([w_row] * (D_MODEL // LANES), axis=1)
            ybuf[slot, :rows, :] = _pack_bf16_pair(y[:, :half], y[:, half:])

        second_half_used = pa_ref[g * PPC + PPC // 2] >= 0

        @pl.when(second_half_used)
        def _():
            swiglu_rows(CR)

        @pl.when(jnp.logical_not(second_half_used))
        def _():
            swiglu_rows(CR // 2)

        write_start(g, slot)

    @pl.when(g == pl.num_programs(0) - 1)
    def _():
        @pl.when((n >= 2) & (ce_ref[jnp.maximum(n - 1, 0)] == ce_ref[jnp.maximum(n - 2, 0)]))
        def _():
            write_wait((n - 2) % N_SLOTS)

        write_wait((n - 1) % N_SLOTS)


def _experts(ce, gn, pa, nch, tot, xs, w_exp_gu, w_exp_down):
    n_rows = xs.shape[0]
    half = D_MODEL // 2
    hbm = pl.BlockSpec(memory_space=pl.ANY)
    grid_spec = pltpu.PrefetchScalarGridSpec(
        num_scalar_prefetch=5,
        grid=(ce.shape[0],),
        in_specs=[hbm, hbm, hbm],
        out_specs=hbm,
        scratch_shapes=[pltpu.VMEM((N_SLOTS, CR, XW), I32), pltpu.VMEM((N_SLOTS, CR, half), I32),
                        pltpu.VMEM((SEG_ALIGN, half), I32),
                        pltpu.VMEM((2, D_MODEL, 2 * EXPERT_FF), F32), pltpu.VMEM((2, EXPERT_FF, D_MODEL), F32),
                        pltpu.VMEM((D_MODEL, 2 * EXPERT_FF), BF16), pltpu.VMEM((EXPERT_FF, D_MODEL), BF16),
                        pltpu.SMEM((1,), I32),
                        pltpu.SemaphoreType.DMA((N_SLOTS,)), pltpu.SemaphoreType.DMA((N_SLOTS,)),
                        pltpu.SemaphoreType.DMA(()), pltpu.SemaphoreType.DMA((2,))],
    )
    return pl.pallas_call(
        _expert_kernel,
        grid_spec=grid_spec,
        out_shape=jax.ShapeDtypeStruct((n_rows + TS, half), I32),
        compiler_params=_cparams("arbitrary"),
        name="moe_grouped_swiglu",
    )(ce, gn, pa, nch, tot, xs, w_exp_gu, w_exp_down)


def _combine_kernel(tot_ref, y_ref, pos_ref, cnt_ref, off_ref, x2_ref, wsg_ref, wsd_ref, g_ref, b_ref, out_ref,
                    acc_lo, acc_hi):
    tm = x2_ref.shape[0]
    E = N_EXPERTS
    n_sorted = tot_ref[pl.program_id(0)]
    x2 = x2_ref[...]
    h = _dot(x2.astype(BF16), wsg_ref[...])
    gate, up = h[:, :SHARED_FF], h[:, SHARED_FF:]
    shared = _dot((gate * jax.nn.sigmoid(gate) * up).astype(BF16), wsd_ref[...])
    pq_t = pos_ref[...].T.astype(BF16)
    off_col = off_ref[0][:, 0:1].astype(F32)
    cnt_col = cnt_ref[0][:, 0:1].astype(F32)
    end_col = off_col + jnp.floor((cnt_col + (SEG_ALIGN - 1.0)) * (1.0 / SEG_ALIGN)) * SEG_ALIGN
    off_col2 = jnp.concatenate([off_col, off_col], axis=0)
    end_col2 = jnp.concatenate([end_col, end_col], axis=0)
    acc_lo[...] = jnp.zeros_like(acc_lo)
    acc_hi[...] = jnp.zeros_like(acc_hi)

    def gather_back(c, carry):
        r0 = pl.multiple_of(c * OH_ROWS, OH_ROWS)
        jc = (lax.broadcasted_iota(I32, (2 * E, OH_ROWS), 1) + r0).astype(F32)
        member2 = jnp.where(jc >= off_col2, jnp.where(jc < end_col2, 1.0, 0.0), 0.0)
        target = _dot(pq_t, member2.astype(BF16))
        j = (lax.broadcasted_iota(I32, (tm, OH_ROWS), 1) + r0).astype(F32)
        ohb = jnp.where(target == j, 1.0, 0.0).astype(BF16)
        yw = y_ref[0, pl.ds(r0, OH_ROWS), :]
        row = lax.broadcasted_iota(I32, yw.shape, 0) + r0
        lo, hi = _unpack_bf16_pair(jnp.where(row < n_sorted, yw, 0))
        acc_lo[...] += _dot(ohb, lo)
        acc_hi[...] += _dot(ohb, hi)
        return carry

    lax.fori_loop(0, SORT_ROWS // OH_ROWS, gather_back, 0, unroll=True)
    ff = shared + jnp.concatenate([acc_lo[...], acc_hi[...]], axis=1)
    out_ref[...] = _layer_norm(DEEPNORM_ALPHA * x2 + ff, g_ref[...], b_ref[...])


def _combine(tot, y, pos, cnt, off, x2, w_sh_gu, w_sh_down, ln_g, ln_b):
    T = x2.shape[0]
    nt = T // TW
    half = D_MODEL // 2
    tile = lambda w: pl.BlockSpec((TW, w), lambda i: (i, 0))
    return pl.pallas_call(
        _combine_kernel,
        grid=(nt,),
        in_specs=[pl.BlockSpec(memory_space=pltpu.SMEM),
                  pl.BlockSpec((1, TS, half), lambda i: (i, 0, 0)),
                  pl.BlockSpec((2 * N_EXPERTS, TW), lambda i: (0, i)),
                  pl.BlockSpec((1, N_EXPERTS, LANES), lambda i: (i, 0, 0)),
                  pl.BlockSpec((1, N_EXPERTS, LANES), lambda i: (i, 0, 0)), tile(D_MODEL),
                  _full((D_MODEL, 2 * SHARED_FF)), _full((SHARED_FF, D_MODEL)),
                  _full((1, D_MODEL)), _full((1, D_MODEL))],
        out_specs=tile(D_MODEL),
        out_shape=jax.ShapeDtypeStruct((T, D_MODEL), F32),
        scratch_shapes=[pltpu.VMEM((TW, half), F32), pltpu.VMEM((TW, half), F32)],
        compiler_params=_cparams("parallel"),
        name="moe_combine_shared_ln3",
    )(tot, y.reshape(nt + 1, TS, half), pos, cnt, off, x2, w_sh_gu.astype(BF16), w_sh_down.astype(BF16),
      ln_g.reshape(1, -1), ln_b.reshape(1, -1))


def _moe(x2, w_router, router_bias, w_exp_gu, w_exp_down, w_sh_gu, w_sh_down, ln_g, ln_b):
    T = x2.shape[0]
    E = N_EXPERTS
    nt = T // TW
    xs, pos, cnt, off = _route_sort(x2, w_router, router_bias)
    n_pieces_max = (T * TOP_K + nt * E * (PR - 1)) // PR
    n_chunks_max = n_pieces_max // PPC + E
    pa, ce, gn, nch = _plan(cnt[:, :, 0].T, off[:, :, 0].T, n_chunks_max)
    last_al = (cnt[:, E - 1, 0] + SEG_ALIGN - 1) // SEG_ALIGN * SEG_ALIGN
    tot = jnp.concatenate([off[:, E - 1, 0] + last_al, jnp.zeros((1,), I32)])
    y = _experts(ce, gn, pa, nch, tot, xs.reshape(nt * TS, XW), w_exp_gu, w_exp_down)
    return _combine(tot, y, pos, cnt, off, x2, w_sh_gu, w_sh_down, ln_g, ln_b)


def kernel(x, mem, positions, w_in, q_norm_g, kv_norm_g, w_uq, w_ukv, w_mla_o, s5_a_re, s5_a_im, s5_log_dt, s5_b_re, s5_b_im, s5_c_re, s5_c_im, s5_d, w_s5_glu, w_out, ln1_g, ln1_b, mem_ln_g, mem_ln_b, w_xq, w_xkv, w_xo, ln2_g, ln2_b, w_router, router_bias, w_exp_gu, w_exp_down, w_sh_gu, w_sh_down, ln3_g, ln3_b):
    B, S, D = x.shape
    assert D == D_MODEL and S % TQ == 0 and (B * S) % TM == 0 and S % S5_CHUNK == 0
    xc = x.reshape(B * S, D)
    for l in range(w_in.shape[0]):
        tabs = _rope_tables(positions)
        q, k, v, u, gm, gs = _inproj(xc, tabs, w_in[l], q_norm_g[l], kv_norm_g[l], w_uq[l], w_ukv[l])
        o = _attention(q, k, v, B, S)
        ys = _s5_branch(u, B, S, (s5_a_re[l], s5_a_im[l], s5_log_dt[l], s5_b_re[l], s5_b_im[l],
                                  s5_c_re[l], s5_c_im[l], s5_d[l]))
        x2 = _merge_cross_attention(xc, o, ys, gm, gs, mem, B, S, w_mla_o[l], w_s5_glu[l], w_out[l],
                                    ln1_g[l], ln1_b[l], mem_ln_g[l], mem_ln_b[l], w_xq[l], w_xkv[l], w_xo[l],
                                    ln2_g[l], ln2_b[l])
        xc = _moe(x2, w_router[l], router_bias[l], w_exp_gu[l], w_exp_down[l], w_sh_gu[l],
                  w_sh_down[l], ln3_g[l], ln3_b[l])
    return xc.reshape(B, S, D)
```

```python
import functools
import math

import jax
import jax.numpy as jnp
import numpy as np
from jax import lax
from jax.experimental import pallas as pl
from jax.experimental.pallas import tpu as pltpu

F32 = jnp.float32
BF16 = jnp.bfloat16
I32 = jnp.int32

D_MODEL = 1024
MLA_HEADS = 8
QK_NOPE = 64
QK_ROPE = 32
V_HEAD = 64
Q_LORA = 256
KV_LORA = 256
ROPE_THETA = 10000.0
S5_GROUP_CH = 16
S5_WIDTH = 512
S5_GROUPS = 32
S5_STATE = 64
XATTN_HEADS = 4
XATTN_HEAD_DIM = 128
N_EXPERTS = 64
TOP_K = 8
N_EXPERT_GROUPS = 8
TOPK_GROUPS = 4
EXPERT_FF = 256
SHARED_FF = 256
ROUTED_SCALE = 2.5
LN_EPS = 1e-5
RMS_EPS = 1e-6
DEPTH = 1
DEEPNORM_ALPHA = (2.0 * DEPTH) ** 0.25

LANES = 128
HEAD_PAD = 128
ROPE_LO = QK_NOPE
ROPE_HALF = QK_ROPE // 2

TM = 512
TQ = 1024
KV_GROUPS = (2, 1)
S5_CHUNK = 16
TW = 512
OH_ROWS = 2304
PR = 16
BULK_DMA_PRIORITY = 1
PAD_ROWS = PR
SEG_ALIGN = 8
SORT_ROWS = TOP_K * TW + N_EXPERTS * SEG_ALIGN
TS = SORT_ROWS + PAD_ROWS
XW = D_MODEL // 2 + LANES
CR = 512
PPC = CR // PR
GATHER_AHEAD = 3
N_SLOTS = GATHER_AHEAD + 1
POS_RADIX = 64
POS_NONE_Q = 127
assert SORT_ROWS <= POS_RADIX * POS_NONE_Q and SORT_ROWS % OH_ROWS == 0
VMEM_LIMIT = 48 * 1024 * 1024


def _cparams(*sem):
    return pltpu.CompilerParams(dimension_semantics=sem, vmem_limit_bytes=VMEM_LIMIT)


def _dot(a, b):
    return jnp.dot(a, b, preferred_element_type=F32)


def _dot_nt(a, b, precision=None):
    return lax.dot_general(a, b, (((1,), (1,)), ((), ())), preferred_element_type=F32,
                           precision=precision)


def _dot_exact(a, b):
    return jnp.dot(a, b, preferred_element_type=F32, precision=lax.Precision.HIGHEST)


def _layer_norm(h, g, b):
    mu = jnp.mean(h, axis=-1, keepdims=True)
    c = h - mu
    var = jnp.mean(c * c, axis=-1, keepdims=True)
    return c * lax.rsqrt(var + LN_EPS) * g + b


def _rms_norm(h, g):
    return h * lax.rsqrt(jnp.mean(h * h, axis=-1, keepdims=True) + RMS_EPS) * g


def _full(shape):
    n = len(shape)
    return pl.BlockSpec(shape, lambda *_: (0,) * n)


def _rope_angle_kernel(pos_ref, invf_ref, cos_ref, sin_ref):
    ang = pos_ref[...].astype(F32) * invf_ref[...]
    cos_ref[...] = jnp.cos(ang)
    sin_ref[...] = jnp.sin(ang)


def _rope_tables(positions):
    T = positions.size
    rows = T * ROPE_HALF // LANES
    pos_rep = jnp.repeat(positions.reshape(T), ROPE_HALF).reshape(rows, LANES)
    inv_freq = ROPE_THETA ** (-jnp.arange(0, QK_ROPE, 2, dtype=F32) / QK_ROPE)
    invf = jnp.tile(inv_freq, LANES // ROPE_HALF).reshape(1, LANES)
    cos, sin = pl.pallas_call(
        _rope_angle_kernel,
        out_shape=(jax.ShapeDtypeStruct((rows, LANES), F32),) * 2,
        name="rope_angles",
    )(pos_rep, invf)
    cos_sin = jnp.concatenate([cos.reshape(T, ROPE_HALF), sin.reshape(T, ROPE_HALF)], axis=1)
    f = np.arange(ROPE_HALF)
    spread = np.zeros((QK_ROPE, 3 * HEAD_PAD), np.float32)
    spread[f, ROPE_LO + f] = 1.0
    spread[f, ROPE_LO + ROPE_HALF + f] = 1.0
    spread[ROPE_HALF + f, HEAD_PAD + ROPE_LO + f] = -1.0
    spread[ROPE_HALF + f, 2 * HEAD_PAD + ROPE_LO + ROPE_HALF + f] = 1.0
    outside = np.ones((1, HEAD_PAD), np.float32)
    outside[0, ROPE_LO:ROPE_LO + QK_ROPE] = 0.0
    return cos_sin, jnp.asarray(spread), jnp.asarray(outside)


def _rope(x, c_tab, s_up, s_dn):
    return (x * c_tab + pltpu.roll(x, HEAD_PAD - ROPE_HALF, axis=1) * s_up
            + pltpu.roll(x, ROPE_HALF, axis=1) * s_dn)


def _inproj_kernel(x_ref, cs_ref, spread_ref, outside_ref, wlat_ref, wu_ref, wgm_ref, wgs_ref, qg_ref, kvg_ref,
                   wuq_ref, wuk_ref, wuv_ref, q_ref, k_ref, v_ref, u_ref, gm_ref, gs_ref):
    xb = x_ref[...].astype(BF16)
    lat = _dot(xb, wlat_ref[...])
    qn = _rms_norm(lat[:, :Q_LORA], qg_ref[...]).astype(BF16)
    kvn = _rms_norm(lat[:, Q_LORA:Q_LORA + KV_LORA], kvg_ref[...]).astype(BF16)
    tabs = _dot_exact(cs_ref[...], spread_ref[...])
    c_tab = tabs[:, :HEAD_PAD] + outside_ref[...]
    s_up, s_dn = tabs[:, HEAD_PAD:2 * HEAD_PAD], tabs[:, 2 * HEAD_PAD:]
    k_rope = _rope(lat[:, Q_LORA + KV_LORA:], c_tab, s_up, s_dn)
    q = _dot(qn, wuq_ref[...])
    k = _dot(kvn, wuk_ref[...])
    scale = (QK_NOPE + QK_ROPE) ** -0.5 * math.log2(math.e)
    for h in range(MLA_HEADS):
        sl = slice(h * HEAD_PAD, (h + 1) * HEAD_PAD)
        q_ref[:, sl] = (_rope(q[:, sl], c_tab, s_up, s_dn) * scale).astype(BF16)
        k_ref[:, sl] = (k[:, sl] + k_rope).astype(BF16)
    v = _dot(kvn, wuv_ref[...])
    ones_lane = lax.broadcasted_iota(I32, (1, v.shape[1]), 1) % HEAD_PAD == V_HEAD
    v_ref[...] = jnp.where(ones_lane, 1.0, v).astype(BF16)
    u_ref[...] = _dot(xb, wu_ref[...])
    gm_ref[...] = _dot(xb, wgm_ref[...]).astype(BF16)
    gs_ref[...] = _dot(xb, wgs_ref[...]).astype(BF16)


def _pad_heads(w, head_w, lo_w):
    K = w.shape[0]
    w = w.reshape(K, MLA_HEADS, head_w)[:, :, :lo_w]
    w = jnp.pad(w, ((0, 0), (0, 0), (0, HEAD_PAD - lo_w)))
    return w.reshape(K, MLA_HEADS * HEAD_PAD)


def _inproj(x2d, tabs, w_in, q_norm_g, kv_norm_g, w_uq, w_ukv):
    T = x2d.shape[0]
    o_rope = Q_LORA + KV_LORA
    o_u = o_rope + QK_ROPE
    o_gm = o_u + S5_WIDTH
    o_gs = o_gm + D_MODEL
    w_rope = jnp.pad(w_in[:, o_rope:o_u], ((0, 0), (ROPE_LO, HEAD_PAD - ROPE_LO - QK_ROPE)))
    w_lat = jnp.concatenate([w_in[:, :o_rope], w_rope], axis=1).astype(BF16)
    w_u = w_in[:, o_u:o_gm].astype(BF16)
    w_gm = w_in[:, o_gm:o_gs].astype(BF16)
    w_gs = w_in[:, o_gs:].astype(BF16)
    wuq = _pad_heads(w_uq, QK_NOPE + QK_ROPE, QK_NOPE + QK_ROPE).astype(BF16)
    kv3 = w_ukv.reshape(KV_LORA, MLA_HEADS, QK_NOPE + V_HEAD)
    wuk = _pad_heads(kv3[:, :, :QK_NOPE].reshape(KV_LORA, -1), QK_NOPE, QK_NOPE).astype(BF16)
    wuv = _pad_heads(kv3[:, :, QK_NOPE:].reshape(KV_LORA, -1), V_HEAD, V_HEAD).astype(BF16)
    HP = MLA_HEADS * HEAD_PAD
    tile = lambda w: pl.BlockSpec((TM, w), lambda i: (i, 0))
    return pl.pallas_call(
        _inproj_kernel,
        grid=(T // TM,),
        in_specs=[tile(D_MODEL), tile(QK_ROPE), _full((QK_ROPE, 3 * HEAD_PAD)), _full((1, HEAD_PAD)),
                  _full(w_lat.shape), _full(w_u.shape), _full(w_gm.shape), _full(w_gs.shape),
                  _full((1, Q_LORA)), _full((1, KV_LORA)),
                  _full(wuq.shape), _full(wuk.shape), _full(wuv.shape)],
        out_specs=[tile(HP), tile(HP), tile(HP), tile(S5_WIDTH), tile(D_MODEL), tile(D_MODEL)],
        out_shape=[jax.ShapeDtypeStruct((T, HP), BF16)] * 3
        + [jax.ShapeDtypeStruct((T, S5_WIDTH), F32)]
        + [jax.ShapeDtypeStruct((T, D_MODEL), BF16)] * 2,
        compiler_params=_cparams("parallel"),
        name="inproj_mla_prep",
    )(x2d, *tabs, w_lat, w_u, w_gm, w_gs, q_norm_g.reshape(1, -1), kv_norm_g.reshape(1, -1),
      wuq, wuk, wuv)


def _attn_kernel(q_ref, k_ref, v_ref, o_ref):
    qi = pl.program_id(2)
    q = q_ref[0]

    def update(qq, k, v, carry, row_offset=None):
        m, acc = carry
        s = _dot_nt(qq, k)
        if row_offset is not None:
            row = lax.broadcasted_iota(I32, s.shape, 0) + row_offset
            col = lax.broadcasted_iota(I32, s.shape, 1)
            s = jnp.where(col <= row, s, -jnp.inf)
        m_new = jnp.maximum(m, jnp.max(s, axis=-1, keepdims=True))
        p = jnp.exp2(s - m_new)
        acc = jnp.exp2(m - m_new) * acc + _dot(p.astype(BF16), v)
        return m_new, acc

    def step(blk, n_blk, carry):
        tk = n_blk * TQ
        start = pl.multiple_of(blk * TQ, TQ)
        return update(q, k_ref[0, pl.ds(start, tk), :], v_ref[0, pl.ds(start, tk), :], carry)

    carry = (jnp.full((TQ, 1), -jnp.inf, F32), jnp.zeros((TQ, HEAD_PAD), F32))
    done = 0
    for n_blk in KV_GROUPS:
        base = done
        n_it = (qi - base) // n_blk
        carry = lax.fori_loop(0, n_it, lambda j, c, base=base, n_blk=n_blk: step(base + j * n_blk, n_blk, c), carry)
        done = base + n_it * n_blk
    hq = TQ // 2
    d0 = pl.multiple_of(qi * TQ, TQ)
    m, acc = carry
    _, acc_lo = update(q[:hq], k_ref[0, pl.ds(d0, hq), :], v_ref[0, pl.ds(d0, hq), :], (m[:hq], acc[:hq]), 0)
    _, acc_hi = update(q[hq:], k_ref[0, pl.ds(d0, TQ), :], v_ref[0, pl.ds(d0, TQ), :], (m[hq:], acc[hq:]), hq)
    o_ref[0, :hq, :] = (acc_lo / acc_lo[:, V_HEAD:V_HEAD + 1]).astype(BF16)
    o_ref[0, hq:, :] = (acc_hi / acc_hi[:, V_HEAD:V_HEAD + 1]).astype(BF16)


def _attention(q, k, v, B, S):
    HP = MLA_HEADS * HEAD_PAD
    q, k, v = (a.reshape(B, S, HP) for a in (q, k, v))
    o = pl.pallas_call(
        _attn_kernel,
        grid=(B, MLA_HEADS, S // TQ),
        in_specs=[pl.BlockSpec((1, TQ, HEAD_PAD), lambda b, h, i: (b, i, h)),
                  pl.BlockSpec((1, S, HEAD_PAD), lambda b, h, i: (b, 0, h)),
                  pl.BlockSpec((1, S, HEAD_PAD), lambda b, h, i: (b, 0, h))],
        out_specs=pl.BlockSpec((1, TQ, HEAD_PAD), lambda b, h, i: (b, i, h)),
        out_shape=jax.ShapeDtypeStruct((B, S, HP), BF16),
        compiler_params=_cparams("parallel", "parallel", "arbitrary"),
        name="mla_flash_attention",
    )(q, k, v)
    return o.reshape(B * S, HP)


CH2 = S5_CHUNK * S5_GROUP_CH
P2 = 2 * S5_STATE
SCAN_LEVELS_MAX = 16


S5_PREP_GROUPS = 4


def _s5_prep_kernel(n_levels, *refs):
    *io_refs, mt_acc = refs
    for k in range(S5_PREP_GROUPS):
        _s5_prep_group(n_levels, *(r.at[pl.ds(k, 1)] for r in io_refs), mt_acc)


def _s5_prep_group(n_levels, arow_ref, acol_ref, bt_ref, ct_ref, d_ref,
                   mt_ref, wt_ref, vt_ref, apow_ref, dvec_ref, mt_acc):
    P, H, L = S5_STATE, S5_GROUP_CH, S5_CHUNK

    def powers(ar, ai, dt, n):
        e = jnp.exp(n * (ar * dt))
        ang = n * (ai * dt)
        return e * jnp.cos(ang), e * jnp.sin(ang)

    def zoh_coef(ar, ai, dt):
        pr, pi = powers(ar, ai, dt, 1.0)
        nr, ni = pr - 1.0, pi
        den = ar * ar + ai * ai
        return (nr * ar + ni * ai) / den, (ni * ar - nr * ai) / den

    row = arow_ref[0]
    ar, ai, dt = row[0:1], row[1:2], jnp.exp(row[2:3])
    cr, ci = zoh_coef(ar, ai, dt)
    bt = bt_ref[0]
    btr = jnp.concatenate([bt[:H], bt[:H]], axis=1)
    bti = jnp.concatenate([bt[H:], bt[H:]], axis=1)
    bbr = cr * btr - ci * bti
    bbi = cr * bti + ci * btr
    lane = lax.broadcasted_iota(I32, (1, P2), 1)
    is_re = lane < P
    ridx = lax.broadcasted_iota(I32, (CH2, P2), 0) // H
    pwr, pwi = powers(ar, ai, dt, (L - 1 - ridx).astype(F32))
    rsel = (lax.broadcasted_iota(I32, (CH2, H), 0) % H == lax.broadcasted_iota(I32, (CH2, H), 1)).astype(F32)
    bbr_t = _dot_exact(rsel, bbr)
    bbi_t = _dot_exact(rsel, bbi)
    w_re = pwr * bbr_t - pwi * bbi_t
    w_im = pwr * bbi_t + pwi * bbr_t
    wt_ref[0] = jnp.where(is_re, w_re, w_im).astype(BF16)
    a_r, a_i = powers(ar, ai, dt, float(L))
    sign = jnp.where(is_re, -1.0, 1.0)
    for lvl in range(SCAN_LEVELS_MAX):
        if lvl < n_levels:
            apow_ref[0, lvl:lvl + 1, :] = a_r
            apow_ref[0, SCAN_LEVELS_MAX + lvl:SCAN_LEVELS_MAX + lvl + 1, :] = sign * a_i
            a_r, a_i = a_r * a_r - a_i * a_i, 2.0 * a_r * a_i
        else:
            apow_ref[0, lvl:lvl + 1, :] = jnp.zeros_like(a_r)
            apow_ref[0, SCAN_LEVELS_MAX + lvl:SCAN_LEVELS_MAX + lvl + 1, :] = jnp.zeros_like(a_r)

    col = acol_ref[0]
    arc, aic, dtc = col[:, 0:1], col[:, 1:2], jnp.exp(col[:, 2:3])
    ct = ct_ref[0]
    csel = (lax.broadcasted_iota(I32, (H, CH2), 1) % H == lax.broadcasted_iota(I32, (H, CH2), 0)).astype(F32)
    ctr = _dot_exact(ct[:, :H], csel)
    cti = _dot_exact(ct[:, H:], csel)
    sidx = (lax.broadcasted_iota(I32, (P, CH2), 1) // H).astype(F32)
    pr, pi = powers(arc, aic, dtc, sidx)
    g_r = ctr * pr - cti * pi
    g_i = ctr * pi + cti * pr
    a1r, a1i = powers(arc, aic, dtc, 1.0)
    v_r = g_r * a1r - g_i * a1i
    v_i = g_r * a1i + g_i * a1r
    vt_ref[0, :P, :] = v_r.astype(BF16)
    vt_ref[0, P:, :] = (-v_i).astype(BF16)
    bbr64, bbi64 = bbr[:, :P], bbi[:, :P]
    kt = _dot_exact(bbr64, g_r) - _dot_exact(bbi64, g_i)
    lane2 = lax.broadcasted_iota(I32, (H, CH2), 1)
    for r in range(L):
        blk = kt if r == 0 else jnp.where(lane2 >= r * H, pltpu.roll(kt, r * H, axis=1), 0.0)
        mt_acc[r * H:(r + 1) * H, :] = blk
    mt_ref[0] = mt_acc[...].astype(BF16)
    dvec_ref[0] = _dot_exact(d_ref[0], csel)


def _s5_prep(n_levels, s5_a_re, s5_a_im, s5_log_dt, s5_b_re, s5_b_im, s5_c_re, s5_c_im, s5_d):
    G, P, H = S5_GROUPS, S5_STATE, S5_GROUP_CH
    ldt = jnp.broadcast_to(s5_log_dt[:, None], (G, P))
    arow = jnp.stack([s5_a_re, s5_a_im, ldt], axis=1)
    arow = jnp.concatenate([arow, arow], axis=2)
    acol = jnp.stack([s5_a_re, s5_a_im, ldt], axis=2)
    bt = jnp.concatenate([s5_b_re.transpose(0, 2, 1), s5_b_im.transpose(0, 2, 1)], axis=1)
    ct = jnp.concatenate([s5_c_re.transpose(0, 2, 1), s5_c_im.transpose(0, 2, 1)], axis=2)
    d = s5_d.reshape(G, 1, H)
    blk = lambda *s: pl.BlockSpec((S5_PREP_GROUPS,) + s, lambda g: (g, 0, 0))
    return pl.pallas_call(
        functools.partial(_s5_prep_kernel, n_levels),
        grid=(G // S5_PREP_GROUPS,),
        in_specs=[blk(3, P2), blk(P, 3), blk(2 * H, P), blk(P, 2 * H), blk(1, H)],
        out_specs=[blk(CH2, CH2), blk(CH2, P2), blk(P2, CH2), blk(2 * SCAN_LEVELS_MAX, P2), blk(1, CH2)],
        out_shape=[jax.ShapeDtypeStruct((G, CH2, CH2), BF16),
                   jax.ShapeDtypeStruct((G, CH2, P2), BF16),
                   jax.ShapeDtypeStruct((G, P2, CH2), BF16),
                   jax.ShapeDtypeStruct((G, 2 * SCAN_LEVELS_MAX, P2), F32),
                   jax.ShapeDtypeStruct((G, 1, CH2), F32)],
        scratch_shapes=[pltpu.VMEM((CH2, CH2), F32)],
        compiler_params=_cparams("parallel"),
        name="s5_discretise",
    )(arow, acol, bt, ct, d)


def _gelu_tanh(y):
    return 0.5 * y * (1.0 + jnp.tanh(math.sqrt(2.0 / math.pi) * (y + 0.044715 * (y * y * y))))


GROUPS_PER_TILE = LANES // S5_GROUP_CH
STEPS_PER_TILE = LANES // S5_GROUP_CH


def _s5_scan_kernel(n_levels, u_ref, mt_ref, wt_ref, vt_ref, apow_ref, dvec_ref, y_ref):
    H, L = S5_GROUP_CH, S5_CHUNK
    nc = u_ref.shape[0] // L
    lane_blk = lax.broadcasted_iota(I32, (nc, LANES), 1) // H
    ridx = lax.broadcasted_iota(I32, (nc, P2), 0)
    steps = [u_ref[pl.ds(l, nc, stride=L), :] for l in range(L)]
    groups = range(GROUPS_PER_TILE)
    us, xs = [], []
    for gi in groups:
        halves = []
        for hh in range(L // STEPS_PER_TILE):
            acc = jnp.zeros((nc, LANES), F32)
            for l8 in range(STEPS_PER_TILE):
                src = steps[hh * STEPS_PER_TILE + l8]
                shift = ((l8 - gi) % STEPS_PER_TILE) * H
                moved = src if shift == 0 else pltpu.roll(src, shift, axis=1)
                acc = jnp.where(lane_blk == l8, moved, acc)
            halves.append(acc)
        u = jnp.concatenate(halves, axis=1)
        us.append(u)
        xs.append(_dot(u.astype(BF16), wt_ref[gi]))
    for lvl in range(n_levels):
        sh = 1 << lvl
        for gi in groups:
            apow = apow_ref[gi]
            prev = jnp.where(ridx >= sh, pltpu.roll(xs[gi], sh, axis=0), 0.0)
            a_r = apow[lvl:lvl + 1]
            a_i = apow[SCAN_LEVELS_MAX + lvl:SCAN_LEVELS_MAX + lvl + 1]
            xs[gi] = xs[gi] + a_r * prev + a_i * pltpu.roll(prev, S5_STATE, axis=1)
    outs = [jnp.zeros((nc, LANES), F32) for _ in range(L)]
    for gi in groups:
        x_in = jnp.where(ridx >= 1, pltpu.roll(xs[gi], 1, axis=0), 0.0)
        u = us[gi]
        y = _gelu_tanh(_dot(u.astype(BF16), mt_ref[gi]) + _dot(x_in.astype(BF16), vt_ref[gi]) + dvec_ref[gi] * u)
        for l in range(L):
            hh, l8 = divmod(l, STEPS_PER_TILE)
            src = y[:, hh * LANES:(hh + 1) * LANES]
            shift = ((gi - l8) % STEPS_PER_TILE) * H
            moved = src if shift == 0 else pltpu.roll(src, shift, axis=1)
            outs[l] = jnp.where(lane_blk == gi, moved, outs[l])
    for l in range(L):
        y_ref[pl.ds(l, nc, stride=L), :] = outs[l]


def _s5_branch(u, B, S, s5_params):
    G, L = S5_GROUPS, S5_CHUNK
    nc = S // L
    n_levels = max(1, (nc - 1).bit_length())
    assert n_levels <= SCAN_LEVELS_MAX and S5_CHUNK % STEPS_PER_TILE == 0
    mt, wt, vt, apow, dvec = _s5_prep(n_levels, *s5_params)
    n_lt = G // GROUPS_PER_TILE
    blk = lambda *s: pl.BlockSpec((GROUPS_PER_TILE,) + s, lambda b, t: (t, 0, 0))
    io = pl.BlockSpec((S, LANES), lambda b, t: (b, t))
    return pl.pallas_call(
        functools.partial(_s5_scan_kernel, n_levels),
        grid=(B, n_lt),
        in_specs=[io, blk(CH2, CH2), blk(CH2, P2), blk(P2, CH2), blk(2 * SCAN_LEVELS_MAX, P2), blk(1, CH2)],
        out_specs=io,
        out_shape=jax.ShapeDtypeStruct((B * S, S5_WIDTH), F32),
        compiler_params=_cparams("parallel", "parallel"),
        name="s5_chunk_scan",
    )(u, mt, wt, vt, apow, dvec)


def _merge_xattn_kernel(x_ref, o_ref, ys_ref, gm_ref, gs_ref, wmo_ref, wglu_ref, wout_ref, g1_ref, b1_ref,
                        k_ref, v_ref, wq_ref, wo_ref, g2_ref, b2_ref, x2_ref):
    o32 = pltpu.bitcast(o_ref[0], I32)
    first_half = lax.broadcasted_iota(I32, (o32.shape[0], HEAD_PAD), 1) < V_HEAD
    pairs = [jnp.where(first_half, o32[:, (2 * p) * HEAD_PAD:(2 * p + 1) * HEAD_PAD],
                       pltpu.roll(o32[:, (2 * p + 1) * HEAD_PAD:(2 * p + 2) * HEAD_PAD], V_HEAD, axis=1))
             for p in range(MLA_HEADS // 2)]
    o_packed = pltpu.bitcast(jnp.concatenate(pairs, axis=1), BF16)
    y_mla = _dot(o_packed, wmo_ref[...])
    z = _dot(ys_ref[0].astype(BF16), wglu_ref[...])
    y_s5 = z[:, :D_MODEL] * jax.nn.sigmoid(z[:, D_MODEL:])
    merged = (jax.nn.sigmoid(gm_ref[0].astype(F32)) * y_mla
              + jax.nn.sigmoid(gs_ref[0].astype(F32)) * y_s5)
    mix = _dot(merged.astype(BF16), wout_ref[...])
    x1 = _layer_norm(DEEPNORM_ALPHA * x_ref[0] + mix, g1_ref[...], b1_ref[...])
    q = (_dot(x1.astype(BF16), wq_ref[...]) * (XATTN_HEAD_DIM ** -0.5)).astype(BF16)
    k = k_ref[0]
    v = v_ref[0]
    outs = []
    for h in range(XATTN_HEADS):
        sl = slice(h * XATTN_HEAD_DIM, (h + 1) * XATTN_HEAD_DIM)
        s = _dot_nt(q[:, sl], k[:, sl])
        p = jnp.exp(s - jnp.max(s, axis=-1, keepdims=True))
        o = _dot(p.astype(BF16), v[:, sl]) / jnp.sum(p, axis=-1, keepdims=True)
        outs.append(o.astype(BF16))
    xa = _dot(jnp.concatenate(outs, axis=1), wo_ref[...])
    x2_ref[0] = _layer_norm(DEEPNORM_ALPHA * x1 + xa, g2_ref[...], b2_ref[...])


def _memkv_kernel(mem_ref, g_ref, b_ref, w_ref, k_ref, v_ref):
    m = _layer_norm(mem_ref[0], g_ref[...], b_ref[...]).astype(BF16)
    kv = _dot(m, w_ref[...])
    hd = XATTN_HEADS * XATTN_HEAD_DIM
    k_ref[0] = kv[:, :hd].astype(BF16)
    v_ref[0] = kv[:, hd:].astype(BF16)


def _merge_cross_attention(x2d, o, ys, gm, gs, mem, B, S, w_mla_o, w_s5_glu, w_out, ln1_g, ln1_b,
                           mem_ln_g, mem_ln_b, w_xq, w_xkv, w_xo, ln2_g, ln2_b):
    M = mem.shape[1]
    hd = XATTN_HEADS * XATTN_HEAD_DIM
    HP = MLA_HEADS * HEAD_PAD
    row = lambda a: a.reshape(1, -1)
    wmo = w_mla_o.astype(BF16)
    k, v = pl.pallas_call(
        _memkv_kernel,
        grid=(B,),
        in_specs=[pl.BlockSpec((1, M, D_MODEL), lambda b: (b, 0, 0)), _full((1, D_MODEL)), _full((1, D_MODEL)),
                  _full((D_MODEL, 2 * hd))],
        out_specs=[pl.BlockSpec((1, M, hd), lambda b: (b, 0, 0))] * 2,
        out_shape=[jax.ShapeDtypeStruct((B, M, hd), BF16)] * 2,
        compiler_params=_cparams("parallel"),
        name="memory_kv",
    )(mem, row(mem_ln_g), row(mem_ln_b), w_xkv.astype(BF16))
    tile = lambda w: pl.BlockSpec((1, TM, w), lambda b, i: (b, i, 0))
    kv_spec = pl.BlockSpec((1, M, hd), lambda b, i: (b, 0, 0))
    r3 = lambda a: a.reshape(B, S, a.shape[-1])
    x2 = pl.pallas_call(
        _merge_xattn_kernel,
        grid=(B, S // TM),
        in_specs=[tile(D_MODEL), tile(HP), tile(S5_WIDTH), tile(D_MODEL), tile(D_MODEL),
                  _full((MLA_HEADS * V_HEAD, D_MODEL)), _full((S5_WIDTH, 2 * D_MODEL)), _full((D_MODEL, D_MODEL)),
                  _full((1, D_MODEL)), _full((1, D_MODEL)),
                  kv_spec, kv_spec, _full((D_MODEL, hd)), _full((hd, D_MODEL)), _full((1, D_MODEL)), _full((1, D_MODEL))],
        out_specs=tile(D_MODEL),
        out_shape=jax.ShapeDtypeStruct((B, S, D_MODEL), F32),
        compiler_params=_cparams("parallel", "parallel"),
        name="merge_ln1_xattn_ln2",
    )(r3(x2d), r3(o), r3(ys), r3(gm), r3(gs), wmo, w_s5_glu.astype(BF16), w_out.astype(BF16), row(ln1_g), row(ln1_b),
      k, v, w_xq.astype(BF16), w_xo.astype(BF16), row(ln2_g), row(ln2_b))
    return x2.reshape(B * S, D_MODEL)


def _route_sort_kernel(x_ref, w_ref, bias_ref, xs_ref, pos_ref, cnt_ref, off_ref):
    E, per = N_EXPERTS, N_EXPERTS // N_EXPERT_GROUPS
    tm = x_ref.shape[0]
    x = x_ref[...]
    xb = x.astype(BF16)
    x_lo = (x - xb.astype(F32)).astype(BF16)
    w = w_ref[...]
    wb = w.astype(BF16)
    w_lo = (w - wb.astype(F32)).astype(BF16)
    logits = _dot_nt(wb, xb) + (_dot_nt(wb, x_lo) + _dot_nt(w_lo, xb))
    scores = jax.nn.sigmoid(logits)
    sel = scores + bias_ref[...]
    neg = -jnp.inf
    i8 = lax.broadcasted_iota(I32, (per, tm), 0)
    gscore = []
    for g in range(N_EXPERT_GROUPS):
        blk = sel[g * per:(g + 1) * per]
        m1 = jnp.max(blk, axis=0, keepdims=True)
        i1 = jnp.min(jnp.where(blk == m1, i8, per), axis=0, keepdims=True)
        m2 = jnp.max(jnp.where(i8 == i1, neg, blk), axis=0, keepdims=True)
        gscore.append(m1 + m2)
    blocks = []
    for g in range(N_EXPERT_GROUPS):
        ahead = jnp.zeros((1, tm), I32)
        for o in range(N_EXPERT_GROUPS):
            if o == g:
                continue
            before = (gscore[o] >= gscore[g]) if o < g else (gscore[o] > gscore[g])
            ahead = ahead + before.astype(I32)
        blocks.append(jnp.where(ahead < TOPK_GROUPS, sel[g * per:(g + 1) * per], neg))
    cur = jnp.concatenate(blocks, axis=0)
    ie = lax.broadcasted_iota(I32, (E, tm), 0)
    picked = jnp.zeros((E, tm), F32)
    for _ in range(TOP_K):
        m = jnp.max(cur, axis=0, keepdims=True)
        idx = jnp.min(jnp.where(cur == m, ie, E), axis=0, keepdims=True)
        hit = ie == idx
        picked = jnp.where(hit, 1.0, picked)
        cur = jnp.where(hit, neg, cur)
    wsel = scores * picked
    wnorm = wsel / jnp.sum(wsel, axis=0, keepdims=True) * ROUTED_SCALE
    pb = picked.astype(BF16)
    tri = (lax.broadcasted_iota(I32, (tm, tm), 0) <= lax.broadcasted_iota(I32, (tm, tm), 1)).astype(BF16)
    incl = _dot(pb, tri)
    cnt_col = jnp.sum(picked, axis=1, keepdims=True)
    lower = (lax.broadcasted_iota(I32, (E, E), 1) < lax.broadcasted_iota(I32, (E, E), 0)).astype(F32)
    al_col = jnp.floor((cnt_col + (SEG_ALIGN - 1.0)) * (1.0 / SEG_ALIGN)) * SEG_ALIGN
    off_col = _dot_exact(lower, jnp.broadcast_to(al_col, (E, LANES)))[:, 0:1]
    pos = off_col + incl - 1.0
    q = jnp.floor(pos * (1.0 / POS_RADIX))
    pq = jnp.concatenate([jnp.where(picked > 0.0, POS_RADIX * q, POS_RADIX * POS_NONE_Q),
                          jnp.where(picked > 0.0, pos - POS_RADIX * q, 0.0)], axis=0)
    pos_ref[...] = pq
    pqb = pq.astype(BF16)
    cnt_ref[0] = jnp.broadcast_to(cnt_col, (E, LANES)).astype(I32)
    off_ref[0] = jnp.broadcast_to(off_col, (E, LANES)).astype(I32)
    cnt_row = _dot_nt(jnp.ones((8, tm), BF16), pb)
    al_row = jnp.floor((cnt_row + (SEG_ALIGN - 1.0)) * (1.0 / SEG_ALIGN)) * SEG_ALIGN
    er = lax.broadcasted_iota(I32, (E, 4 * E), 0)
    ec = lax.broadcasted_iota(I32, (E, 4 * E), 1) % E
    off_row4 = _dot_exact(al_row, (er < ec).astype(F32))[0:1]
    end_row4 = off_row4 + _dot_exact(al_row, (er == ec).astype(F32))[0:1]
    w_hi = wnorm.astype(BF16).astype(F32)
    w_r1 = wnorm - w_hi
    w_mid = w_r1.astype(BF16).astype(F32)
    w_lo = w_r1 - w_mid
    w4_t = jnp.concatenate([w_hi, w_mid, w_lo, jnp.zeros_like(w_lo)], axis=0).T
    xw = jnp.concatenate([xb, w4_t.astype(BF16)], axis=1)
    half = D_MODEL // 2

    def build(c, carry):
        r0 = pl.multiple_of(c * OH_ROWS, OH_ROWS)
        jr = (lax.broadcasted_iota(I32, (OH_ROWS, 4 * E), 0) + r0).astype(F32)
        member4 = jnp.where(jr >= off_row4, jnp.where(jr < end_row4, 1.0, 0.0), 0.0)
        target = _dot(member4[:, :2 * E].astype(BF16), pqb)
        j = (lax.broadcasted_iota(I32, (OH_ROWS, tm), 0) + r0).astype(F32)
        ohb = jnp.where(target == j, 1.0, 0.0).astype(BF16)
        rows = _dot(ohb, xw)
        xs_ref[0, pl.ds(r0, OH_ROWS), 0:half] = _pack_bf16_pair(rows[:, :half], rows[:, half:D_MODEL])
        wrow = jnp.sum(member4 * rows[:, D_MODEL:], axis=1, keepdims=True)
        xs_ref[0, pl.ds(r0, OH_ROWS), half:] = pltpu.bitcast(jnp.broadcast_to(wrow, (OH_ROWS, LANES)), I32)
        return carry

    lax.fori_loop(0, SORT_ROWS // OH_ROWS, build, 0, unroll=True)
    xs_ref[0, SORT_ROWS:, :] = jnp.zeros((PAD_ROWS, XW), I32)


def _route_sort(x2, w_router, router_bias):
    T = x2.shape[0]
    E = N_EXPERTS
    nt = T // TW
    seg = pl.BlockSpec((1, E, LANES), lambda i: (i, 0, 0))
    return pl.pallas_call(
        _route_sort_kernel,
        grid=(nt,),
        in_specs=[pl.BlockSpec((TW, D_MODEL), lambda i: (i, 0)), _full((E, D_MODEL)), _full((E, 1))],
        out_specs=[pl.BlockSpec((1, TS, XW), lambda i: (i, 0, 0)),
                   pl.BlockSpec((2 * E, TW), lambda i: (0, i)), seg, seg],
        out_shape=[jax.ShapeDtypeStruct((nt, TS, XW), I32), jax.ShapeDtypeStruct((2 * E, T), F32),
                   jax.ShapeDtypeStruct((nt, E, LANES), I32), jax.ShapeDtypeStruct((nt, E, LANES), I32)],
        compiler_params=_cparams("parallel"),
        name="route_local_sort",
    )(x2, w_router.T, router_bias.reshape(E, 1))


def _plan_kernel(cnt_ref, off_ref, pa_ref, ce_ref, gn_ref, nch_ref):
    E, nt = cnt_ref.shape
    n_rows = pa_ref.shape[0]
    cnt = cnt_ref[...].astype(F32)
    off = off_ref[...].astype(F32)
    npc = jnp.floor((cnt + (PR - 1.0)) * (1.0 / PR))
    tri = (lax.broadcasted_iota(I32, (nt, nt), 0) <= lax.broadcasted_iota(I32, (nt, nt), 1)).astype(F32)
    p_incl = _dot_exact(npc, tri)
    total = p_incl[:, nt - 1:nt]
    n_ch = jnp.floor((total + (PPC - 1.0)) * (1.0 / PPC))
    lower = (lax.broadcasted_iota(I32, (E, E), 1) < lax.broadcasted_iota(I32, (E, E), 0)).astype(F32)
    wide = lambda col: jnp.broadcast_to(col, (E, LANES))
    g0 = _dot_exact(lower, wide(n_ch))[:, 0:1]
    n_total = jnp.sum(n_ch, axis=0, keepdims=True)
    g0_row = wide(g0).T[0:1, :E]
    g1_row = g0_row + wide(n_ch).T[0:1, :E]
    g_e = lax.broadcasted_iota(I32, (n_rows, E), 0).astype(F32)
    sel = jnp.where(g_e >= g0_row, jnp.where(g_e < g1_row, 1.0, 0.0), 0.0)
    tile_base = lax.broadcasted_iota(I32, (E, nt), 1).astype(F32) * TS
    a_mat = tile_base + off - PR * (p_incl - npc)
    p_g = _dot_exact(sel, p_incl)
    a_g = _dot_exact(sel, a_mat)
    e_idx = lax.broadcasted_iota(I32, (E, LANES), 0).astype(F32)
    pick = lambda col: _dot_exact(sel, col)[:, 0:1]
    g0_g, tot_g, e_g = pick(wide(g0)), pick(wide(total)), pick(e_idx)
    gn_g = pick(wide(g0 + n_ch))
    g_p = lax.broadcasted_iota(I32, (n_rows, PPC), 0).astype(F32)
    q = (g_p - g0_g) * PPC + lax.broadcasted_iota(I32, (n_rows, PPC), 1).astype(F32)
    addr = jnp.full((n_rows, PPC), -float(SEG_ALIGN), F32)
    lo = jnp.zeros((n_rows, 1), F32)
    for i in range(nt):
        hi = p_g[:, i:i + 1]
        addr = jnp.where(q >= lo, jnp.where(q < hi, a_g[:, i:i + 1] + PR * q, addr), addr)
        lo = hi
    pa_ref[...] = addr.astype(I32)
    g_col = lax.broadcasted_iota(I32, (n_rows, LANES), 0).astype(F32)
    ce_ref[...] = jnp.where(g_col < n_total, jnp.broadcast_to(e_g, (n_rows, LANES)), E - 1.0).astype(I32)
    gn_ref[...] = jnp.broadcast_to(gn_g, (n_rows, LANES)).astype(I32)
    nch_ref[...] = jnp.broadcast_to(n_total, nch_ref.shape).astype(I32)


def _plan(cnt, off, n_chunks_max):
    n_rows = (n_chunks_max + 7) // 8 * 8
    pa, ce, gn, nch = pl.pallas_call(
        _plan_kernel,
        out_shape=[jax.ShapeDtypeStruct((n_rows, PPC), I32), jax.ShapeDtypeStruct((n_rows, LANES), I32),
                   jax.ShapeDtypeStruct((n_rows, LANES), I32), jax.ShapeDtypeStruct((8, LANES), I32)],
        name="moe_plan",
    )(cnt, off)
    return pa.reshape(n_rows * PPC), ce[:, 0], gn[:, 0], nch[0, :1]


def _unpack_bf16_pair(w):
    lo = pltpu.bitcast(lax.shift_left(w, jnp.int32(16)), F32).astype(BF16)
    hi = pltpu.bitcast(w & jnp.int32(-65536), F32).astype(BF16)
    return lo, hi


def _pack_bf16_pair(lo, hi):
    lo_bits = lax.shift_right_logical(pltpu.bitcast(lo.astype(BF16).astype(F32), I32), jnp.int32(16))
    hi_bits = pltpu.bitcast(hi.astype(BF16).astype(F32), I32)
    return hi_bits | lo_bits


def _expert_kernel(ce_ref, gn_ref, pa_ref, nch_ref, tot_ref, xs_hbm, wgu_hbm, wdn_hbm, y_hbm,
                   xbuf, ybuf, zbuf, wgu_f, wdn_f, wgu_b, wdn_b, wslot, gsem, wsem, zsem, esem):
    g = pl.program_id(0)
    n = nch_ref[0]
    slot = g % N_SLOTS
    half = D_MODEL // 2

    dummy_base = (tot_ref.shape[0] - 1) * TS

    def gather_copy(s, a, jp):
        return pltpu.make_async_copy(xs_hbm.at[pl.ds(pl.multiple_of(a, SEG_ALIGN), PR), :],
                                     xbuf.at[s, pl.ds(jp * PR, PR), :], gsem.at[s])

    def write_copy(s, a, jp):
        return pltpu.make_async_copy(ybuf.at[s, pl.ds(jp * PR, PR), :],
                                     y_hbm.at[pl.ds(pl.multiple_of(a, SEG_ALIGN), PR), :], wsem.at[s])

    def gather_start(c, s):
        for jp in range(PPC):
            gather_copy(s, jnp.maximum(pa_ref[c * PPC + jp], 0), jp).start()

    def write_start(c, s):
        for jp in range(PPC):
            a = pa_ref[c * PPC + jp]
            write_copy(s, jnp.where(a >= 0, a, dummy_base + jp * PR), jp).start(priority=BULK_DMA_PRIORITY)

    def gather_wait(s):
        for jp in range(PPC):
            gather_copy(s, 0, jp).wait()

    def write_wait(s):
        for jp in range(PPC):
            write_copy(s, 0, jp).wait()

    def weight_copies(e, s):
        return (pltpu.make_async_copy(wgu_hbm.at[e], wgu_f.at[s], esem.at[s]),
                pltpu.make_async_copy(wdn_hbm.at[e], wdn_f.at[s], esem.at[s]))

    @pl.when(g == 0)
    def _():
        wslot[0] = 0
        for cp in weight_copies(ce_ref[0], 0):
            cp.start(priority=BULK_DMA_PRIORITY)
        gather_start(0, 0)
        ybuf[...] = jnp.zeros_like(ybuf)
        zbuf[...] = jnp.zeros_like(zbuf)

        def zero_copy(row):
            return pltpu.make_async_copy(zbuf, y_hbm.at[pl.ds(pl.multiple_of(row, SEG_ALIGN), SEG_ALIGN), :], zsem)

        def tail_of(i):
            return i * TS + tot_ref[i], (TS - tot_ref[i]) // SEG_ALIGN

        def zero_start(i, c):
            base, cnt8 = tail_of(i)
            lax.fori_loop(0, cnt8, lambda q, z: (zero_copy(base + q * SEG_ALIGN).start(), z)[1], 0)
            return c

        def zero_wait(i, c):
            base, cnt8 = tail_of(i)
            lax.fori_loop(0, cnt8, lambda q, z: (zero_copy(base + q * SEG_ALIGN).wait(), z)[1], 0)
            return c

        lax.fori_loop(0, tot_ref.shape[0], zero_start, 0)
        lax.fori_loop(0, tot_ref.shape[0], zero_wait, 0)

    for c in range(1, GATHER_AHEAD):
        @pl.when((g == 0) & (c < n))
        def _(c=c):
            gather_start(c, c)

    @pl.when(g + GATHER_AHEAD < n)
    def _():
        gather_start(g + GATHER_AHEAD, (g + GATHER_AHEAD) % N_SLOTS)

    e_cur = ce_ref[g]
    e_prev = ce_ref[jnp.maximum(g - 1, 0)]
    e_prev2 = ce_ref[jnp.maximum(g - 2, 0)]

    @pl.when(g < n)
    def _():
        @pl.when((g == 0) | (e_cur != e_prev))
        def _():
            s = wslot[0]
            for cp in weight_copies(e_cur, s):
                cp.wait()
            wgu_b[...] = wgu_f[s].astype(BF16)
            wdn_b[...] = wdn_f[s].astype(BF16)
            g_next = gn_ref[g]

            @pl.when(g_next < n)
            def _():
                for cp in weight_copies(ce_ref[g_next], 1 - s):
                    cp.start(priority=BULK_DMA_PRIORITY)

            wslot[0] = 1 - s

        gather_wait(slot)

        @pl.when((g >= 2) & (e_prev == e_prev2))
        def _():
            write_wait((g - 2) % N_SLOTS)

        @pl.when((g >= 1) & (e_cur != e_prev))
        def _():
            write_wait((g - 1) % N_SLOTS)

        def swiglu_rows(rows):
            xw = xbuf[slot, :rows, :]
            lo, hi = _unpack_bf16_pair(xw[:, :half])
            h = _dot(jnp.concatenate([lo, hi], axis=1), wgu_b[...])
            gate, up = h[:, :EXPERT_FF], h[:, EXPERT_FF:]
            act = (gate * jax.nn.sigmoid(gate) * up).astype(BF16)
            y = _dot(act, wdn_b[...])
            w_row = pltpu.bitcast(xw[:, half:], F32)
            y = y * jnp.concatenate([w_row] * (D_MODEL // LANES), axis=1)
            ybuf[slot, :rows, :] = _pack_bf16_pair(y[:, :half], y[:, half:])

        second_half_used = pa_ref[g * PPC + PPC // 2] >= 0

        @pl.when(second_half_used)
        def _():
            swiglu_rows(CR)

        @pl.when(jnp.logical_not(second_half_used))
        def _():
            swiglu_rows(CR // 2)

        write_start(g, slot)

    @pl.when(g == pl.num_programs(0) - 1)
    def _():
        @pl.when((n >= 2) & (ce_ref[jnp.maximum(n - 1, 0)] == ce_ref[jnp.maximum(n - 2, 0)]))
        def _():
            write_wait((n - 2) % N_SLOTS)

        write_wait((n - 1) % N_SLOTS)


def _experts(ce, gn, pa, nch, tot, xs, w_exp_gu, w_exp_down):
    n_rows = xs.shape[0]
    half = D_MODEL // 2
    hbm = pl.BlockSpec(memory_space=pl.ANY)
    grid_spec = pltpu.PrefetchScalarGridSpec(
        num_scalar_prefetch=5,
        grid=(ce.shape[0],),
        in_specs=[hbm, hbm, hbm],
        out_specs=hbm,
        scratch_shapes=[pltpu.VMEM((N_SLOTS, CR, XW), I32), pltpu.VMEM((N_SLOTS, CR, half), I32),
                        pltpu.VMEM((SEG_ALIGN, half), I32),
                        pltpu.VMEM((2, D_MODEL, 2 * EXPERT_FF), F32), pltpu.VMEM((2, EXPERT_FF, D_MODEL), F32),
                        pltpu.VMEM((D_MODEL, 2 * EXPERT_FF), BF16), pltpu.VMEM((EXPERT_FF, D_MODEL), BF16),
                        pltpu.SMEM((1,), I32),
                        pltpu.SemaphoreType.DMA((N_SLOTS,)), pltpu.SemaphoreType.DMA((N_SLOTS,)),
                        pltpu.SemaphoreType.DMA(()), pltpu.SemaphoreType.DMA((2,))],
    )
    return pl.pallas_call(
        _expert_kernel,
        grid_spec=grid_spec,
        out_shape=jax.ShapeDtypeStruct((n_rows + TS, half), I32),
        compiler_params=_cparams("arbitrary"),
        name="moe_grouped_swiglu",
    )(ce, gn, pa, nch, tot, xs, w_exp_gu, w_exp_down)


def _combine_kernel(tot_ref, y_ref, pos_ref, cnt_ref, off_ref, x2_ref, wsg_ref, wsd_ref, g_ref, b_ref, out_ref,
                    acc_lo, acc_hi):
    tm = x2_ref.shape[0]
    E = N_EXPERTS
    n_sorted = tot_ref[pl.program_id(0)]
    x2 = x2_ref[...]
    h = _dot(x2.astype(BF16), wsg_ref[...])
    gate, up = h[:, :SHARED_FF], h[:, SHARED_FF:]
    shared = _dot((gate * jax.nn.sigmoid(gate) * up).astype(BF16), wsd_ref[...])
    pq_t = pos_ref[...].T.astype(BF16)
    off_col = off_ref[0][:, 0:1].astype(F32)
    cnt_col = cnt_ref[0][:, 0:1].astype(F32)
    end_col = off_col + jnp.floor((cnt_col + (SEG_ALIGN - 1.0)) * (1.0 / SEG_ALIGN)) * SEG_ALIGN
    off_col2 = jnp.concatenate([off_col, off_col], axis=0)
    end_col2 = jnp.concatenate([end_col, end_col], axis=0)
    acc_lo[...] = jnp.zeros_like(acc_lo)
    acc_hi[...] = jnp.zeros_like(acc_hi)

    def gather_back(c, carry):
        r0 = pl.multiple_of(c * OH_ROWS, OH_ROWS)
        jc = (lax.broadcasted_iota(I32, (2 * E, OH_ROWS), 1) + r0).astype(F32)
        member2 = jnp.where(jc >= off_col2, jnp.where(jc < end_col2, 1.0, 0.0), 0.0)
        target = _dot(pq_t, member2.astype(BF16))
        j = (lax.broadcasted_iota(I32, (tm, OH_ROWS), 1) + r0).astype(F32)
        ohb = jnp.where(target == j, 1.0, 0.0).astype(BF16)
        yw = y_ref[0, pl.ds(r0, OH_ROWS), :]
        row = lax.broadcasted_iota(I32, yw.shape, 0) + r0
        lo, hi = _unpack_bf16_pair(jnp.where(row < n_sorted, yw, 0))
        acc_lo[...] += _dot(ohb, lo)
        acc_hi[...] += _dot(ohb, hi)
        return carry

    lax.fori_loop(0, SORT_ROWS // OH_ROWS, gather_back, 0, unroll=True)
    ff = shared + jnp.concatenate([acc_lo[...], acc_hi[...]], axis=1)
    out_ref[...] = _layer_norm(DEEPNORM_ALPHA * x2 + ff, g_ref[...], b_ref[...])


def _combine(tot, y, pos, cnt, off, x2, w_sh_gu, w_sh_down, ln_g, ln_b):
    T = x2.shape[0]
    nt = T // TW
    half = D_MODEL // 2
    tile = lambda w: pl.BlockSpec((TW, w), lambda i: (i, 0))
    return pl.pallas_call(
        _combine_kernel,
        grid=(nt,),
        in_specs=[pl.BlockSpec(memory_space=pltpu.SMEM),
                  pl.BlockSpec((1, TS, half), lambda i: (i, 0, 0)),
                  pl.BlockSpec((2 * N_EXPERTS, TW), lambda i: (0, i)),
                  pl.BlockSpec((1, N_EXPERTS, LANES), lambda i: (i, 0, 0)),
                  pl.BlockSpec((1, N_EXPERTS, LANES), lambda i: (i, 0, 0)), tile(D_MODEL),
                  _full((D_MODEL, 2 * SHARED_FF)), _full((SHARED_FF, D_MODEL)),
                  _full((1, D_MODEL)), _full((1, D_MODEL))],
        out_specs=tile(D_MODEL),
        out_shape=jax.ShapeDtypeStruct((T, D_MODEL), F32),
        scratch_shapes=[pltpu.VMEM((TW, half), F32), pltpu.VMEM((TW, half), F32)],
        compiler_params=_cparams("parallel"),
        name="moe_combine_shared_ln3",
    )(tot, y.reshape(nt + 1, TS, half), pos, cnt, off, x2, w_sh_gu.astype(BF16), w_sh_down.astype(BF16),
      ln_g.reshape(1, -1), ln_b.reshape(1, -1))


def _moe(x2, w_router, router_bias, w_exp_gu, w_exp_down, w_sh_gu, w_sh_down, ln_g, ln_b):
    T = x2.shape[0]
    E = N_EXPERTS
    nt = T // TW
    xs, pos, cnt, off = _route_sort(x2, w_router, router_bias)
    n_pieces_max = (T * TOP_K + nt * E * (PR - 1)) // PR
    n_chunks_max = n_pieces_max // PPC + E
    pa, ce, gn, nch = _plan(cnt[:, :, 0].T, off[:, :, 0].T, n_chunks_max)
    last_al = (cnt[:, E - 1, 0] + SEG_ALIGN - 1) // SEG_ALIGN * SEG_ALIGN
    tot = jnp.concatenate([off[:, E - 1, 0] + last_al, jnp.zeros((1,), I32)])
    y = _experts(ce, gn, pa, nch, tot, xs.reshape(nt * TS, XW), w_exp_gu, w_exp_down)
    return _combine(tot, y, pos, cnt, off, x2, w_sh_gu, w_sh_down, ln_g, ln_b)


def kernel(x, mem, positions, w_in, q_norm_g, kv_norm_g, w_uq, w_ukv, w_mla_o, s5_a_re, s5_a_im, s5_log_dt, s5_b_re, s5_b_im, s5_c_re, s5_c_im, s5_d, w_s5_glu, w_out, ln1_g, ln1_b, mem_ln_g, mem_ln_b, w_xq, w_xkv, w_xo, ln2_g, ln2_b, w_router, router_bias, w_exp_gu, w_exp_down, w_sh_gu, w_sh_down, ln3_g, ln3_b):
    B, S, D = x.shape
    assert D == D_MODEL and S % TQ == 0 and (B * S) % TM == 0 and S % S5_CHUNK == 0
    xc = x.reshape(B * S, D)
    for l in range(w_in.shape[0]):
        tabs = _rope_tables(positions)
        q, k, v, u, gm, gs = _inproj(xc, tabs, w_in[l], q_norm_g[l], kv_norm_g[l], w_uq[l], w_ukv[l])
        o = _attention(q, k, v, B, S)
        ys = _s5_branch(u, B, S, (s5_a_re[l], s5_a_im[l], s5_log_dt[l], s5_b_re[l], s5_b_im[l],
                                  s5_c_re[l], s5_c_im[l], s5_d[l]))
        x2 = _merge_cross_attention(xc, o, ys, gm, gs, mem, B, S, w_mla_o[l], w_s5_glu[l], w_out[l],
                                    ln1_g[l], ln1_b[l], mem_ln_g[l], mem_ln_b[l], w_xq[l], w_xkv[l], w_xo[l],
                                    ln2_g[l], ln2_b[l])
        xc = _moe(x2, w_router[l], router_bias[l], w_exp_gu[l], w_exp_down[l], w_sh_gu[l],
                  w_sh_down[l], ln3_g[l], ln3_b[l])
    return xc.reshape(B, S, D)
```
